```python
import jax, jax.numpy as jnp
from jax import lax
import numpy as np

D_MODEL = 1024
BATCH = 8
SEQ = 8192
DEPTH = 2

GRID_W = 64
CTX_LEN = 256
EPS = 1e-6
D_FF = 4 * D_MODEL
SGU_HEADS = 4
SGU_HEAD_DIM = 64
SGU_WIDTH = SGU_HEADS * SGU_HEAD_DIM
SGU_CHUNK = 128
GLA_HEADS = 4
GLA_DK = 32
GLA_DV = 64
GLA_K_WIDTH = GLA_HEADS * GLA_DK
GLA_V_WIDTH = GLA_HEADS * GLA_DV
GLA_GATE_RANK = 16
GLA_TAU = 16.0
GLA_CHUNK = 64
MLA_HEADS = 4
MLA_Q_RANK = 256
MLA_KV_RANK = 256
MLA_NOPE = 128
MLA_ROPE = 64
MLA_V = 128
MLA_WIDTH = MLA_HEADS * MLA_V
ROPE_BASE = 10000.0
ATTN_BLOCK = 128
MIX_WIDTH = SGU_WIDTH + GLA_V_WIDTH + MLA_WIDTH
KV_SPLITS = (GLA_K_WIDTH, GLA_V_WIDTH, GLA_GATE_RANK, GLA_GATE_RANK, MLA_KV_RANK, MLA_ROPE)
Q_SPLITS = (SGU_WIDTH, SGU_WIDTH, GLA_K_WIDTH, GLA_V_WIDTH, MLA_Q_RANK)
KV_COLS = sum(KV_SPLITS)
IN_COLS = KV_COLS + sum(Q_SPLITS)

kernel_name = "hybrid_sgu_gla_mla_prefix_dit_block"


def split_cols(t, widths):
    idx = np.cumsum(widths)[:-1].tolist()
    return jnp.split(t, idx, axis=-1)


def rmsnorm(x, w):
    xf = x.astype(jnp.float32)
    y = xf * lax.rsqrt(jnp.mean(xf * xf, axis=-1, keepdims=True) + EPS)
    return (y * w.astype(jnp.float32)).astype(x.dtype)


def layernorm(x, w, b):
    xf = x.astype(jnp.float32)
    mu = jnp.mean(xf, axis=-1, keepdims=True)
    var = jnp.mean(jnp.square(xf - mu), axis=-1, keepdims=True)
    y = (xf - mu) * lax.rsqrt(var + EPS)
    return (y * w.astype(jnp.float32) + b.astype(jnp.float32)).astype(x.dtype)


def modulate(xn, shift, scale):
    return xn * (1.0 + scale) + shift


def axial_angles(n):
    rows = n // GRID_W
    row = jnp.repeat(jnp.arange(rows), GRID_W).astype(jnp.float32)
    col = jnp.tile(jnp.arange(GRID_W), rows).astype(jnp.float32)
    half = MLA_ROPE // 2
    freq = ROPE_BASE ** (-jnp.arange(half // 2, dtype=jnp.float32) * 2.0 / half)
    return row[:, None] * freq[None, :], col[:, None] * freq[None, :]


def rotate(x, ang):
    m = x.shape[-1] // 2
    xf = x.astype(jnp.float32)
    x1, x2 = xf[..., :m], xf[..., m:]
    cos, sin = jnp.cos(ang), jnp.sin(ang)
    return jnp.concatenate([x1 * cos - x2 * sin, x1 * sin + x2 * cos], axis=-1).astype(x.dtype)


def axial_rope(x, ang_row, ang_col):
    half = x.shape[-1] // 2
    return jnp.concatenate([rotate(x[..., :half], ang_row), rotate(x[..., half:], ang_col)], axis=-1)


def to_heads(t, d):
    bsz, n, _ = t.shape
    return t.reshape(bsz, n, -1, d).transpose(0, 2, 1, 3)


def sgu(p_u, p_v, norm_w, norm_b, w_s, b_s):
    bsz, n, _ = p_u.shape
    u = jax.nn.gelu(p_u)
    v = layernorm(jax.nn.gelu(p_v), norm_w, norm_b)
    v = v.reshape(bsz, n // SGU_CHUNK, SGU_CHUNK, SGU_HEADS, SGU_HEAD_DIM)
    s = jnp.einsum('hij,bcjhe->bcihe', w_s, v) + b_s.T[:, :, None]
    return u * s.reshape(bsz, n, SGU_WIDTH)


def gla_log_gate(p_g, w, b):
    z = (p_g @ w + b).astype(jnp.float32)
    return jax.nn.log_sigmoid(z) / GLA_TAU


def gla_scan(q, k, v, g, s0):
    bsz, h, n, _ = q.shape
    nc = n // GLA_CHUNK

    def to_chunks(t):
        return jnp.moveaxis(t.reshape(bsz, h, nc, GLA_CHUNK, t.shape[-1]), 2, 0)

    mask = jnp.tril(jnp.ones((GLA_CHUNK, GLA_CHUNK), bool))[:, :, None]

    def step(s, inp):
        qc, kc, vc, gc = inp
        b = jnp.cumsum(gc, axis=2)
        diff = b[:, :, :, None, :] - b[:, :, None, :, :]
        decay = jnp.where(mask, jnp.exp(jnp.where(mask, diff, 0.0)), 0.0)
        att = jnp.einsum('bhid,bhjd,bhijd->bhij', qc, kc, decay)
        o = jnp.einsum('bhij,bhje->bhie', att, vc) + jnp.einsum('bhid,bhde->bhie', qc * jnp.exp(b), s)
        b_last = b[:, :, -1:, :]
        s_new = jnp.exp(b_last)[:, :, 0, :, None] * s + jnp.einsum('bhjd,bhje->bhde', kc * jnp.exp(b_last - b), vc)
        return s_new, o

    s_fin, o = lax.scan(step, s0, (to_chunks(q), to_chunks(k), to_chunks(v), to_chunks(g)))
    o = jnp.moveaxis(o, 0, 2).reshape(bsz, h, n, v.shape[-1])
    return o, s_fin


def gla_final_state(k, v, g):
    b = jnp.cumsum(g, axis=2)
    return jnp.einsum('bhjd,bhje->bhde', k * jnp.exp(b[:, :, -1:, :] - b), v)


def gla_output(o, r, norm_w, dtype):
    bsz, h, n, dv = o.shape
    o = rmsnorm(o.transpose(0, 2, 1, 3), norm_w)
    o = o * jax.nn.silu(r.reshape(bsz, n, h, dv).astype(jnp.float32))
    return o.reshape(bsz, n, h * dv).astype(dtype)


def mla_kv(p_ckv, p_krope, kv_norm_w, w_ukv, ang):
    bsz, n, _ = p_ckv.shape
    kv = (rmsnorm(p_ckv, kv_norm_w) @ w_ukv).reshape(bsz, n, MLA_HEADS, MLA_NOPE + MLA_V)
    k_nope, v = kv[..., :MLA_NOPE], kv[..., MLA_NOPE:]
    k_rope = p_krope if ang is None else axial_rope(p_krope, ang[0], ang[1])
    return k_nope, k_rope, v


def mla_q(p_dq, q_norm_w, w_uq, ang):
    bsz, n, _ = p_dq.shape
    q = (rmsnorm(p_dq, q_norm_w) @ w_uq).reshape(bsz, n, MLA_HEADS, MLA_NOPE + MLA_ROPE)
    q_nope, q_rope = q[..., :MLA_NOPE], q[..., MLA_NOPE:]
    if ang is not None:
        q_rope = axial_rope(q_rope, ang[0][:, None, :], ang[1][:, None, :])
    return q_nope, q_rope


def mla_attend(q_nope, q_rope, k_nope, k_rope, v):
    scale = (MLA_NOPE + MLA_ROPE) ** -0.5
    s = jnp.einsum('bqhd,bkhd->bhqk', q_nope, k_nope) + jnp.einsum('bqhr,bkr->bhqk', q_rope, k_rope)
    p = jax.nn.softmax(s.astype(jnp.float32) * scale, axis=-1).astype(v.dtype)
    return jnp.einsum('bhqk,bkhd->bqhd', p, v)


def mla_attend_blocked(q_nope, q_rope, k_nope, k_rope, v):
    bsz, n = q_nope.shape[:2]
    nb = n // ATTN_BLOCK
    qn = jnp.moveaxis(q_nope.reshape(bsz, nb, ATTN_BLOCK, MLA_HEADS, MLA_NOPE), 1, 0)
    qr = jnp.moveaxis(q_rope.reshape(bsz, nb, ATTN_BLOCK, MLA_HEADS, MLA_ROPE), 1, 0)
    out = lax.map(lambda qs: mla_attend(qs[0], qs[1], k_nope, k_rope, v), (qn, qr))
    return jnp.moveaxis(out, 0, 1).reshape(bsz, n, MLA_WIDTH)


def mixing(h, hc, with_ctx_out, w_in, sgu_norm_w, sgu_norm_b, sgu_w, sgu_b, gla_wg_fwd, gla_bg_fwd,
           gla_wg_bwd, gla_bg_bwd, gla_norm_w, mla_q_norm_w, mla_w_uq, mla_kv_norm_w, mla_w_ukv, ang):
    bsz, n, _ = h.shape
    p = h @ w_in
    gk, gv, ggf, ggb, ckv, kr = split_cols(p[..., :KV_COLS], KV_SPLITS)
    su, sv, gq, gr, dq = split_cols(p[..., KV_COLS:], Q_SPLITS)
    pc = hc @ (w_in if with_ctx_out else w_in[:, :KV_COLS])
    gk_c, gv_c, ggf_c, ggb_c, ckv_c, kr_c = split_cols(pc[..., :KV_COLS], KV_SPLITS)

    y_sgu = sgu(su, sv, sgu_norm_w, sgu_norm_b, sgu_w, sgu_b)

    def kvg(k_, v_, gf_, gb_):
        return (to_heads(k_, GLA_DK).astype(jnp.float32), to_heads(v_, GLA_DV).astype(jnp.float32),
                to_heads(gla_log_gate(gf_, gla_wg_fwd, gla_bg_fwd), GLA_DK),
                to_heads(gla_log_gate(gb_, gla_wg_bwd, gla_bg_bwd), GLA_DK))

    def gq_heads(q_):
        return to_heads(q_, GLA_DK).astype(jnp.float32) * (GLA_DK ** -0.5)

    flip = lambda t: jnp.flip(t, axis=2)
    k_l, v_l, gf_l, gb_l = kvg(gk, gv, ggf, ggb)
    k_c, v_c, gf_c, gb_c = kvg(gk_c, gv_c, ggf_c, ggb_c)
    q_l = gq_heads(gq)
    if with_ctx_out:
        su_c, sv_c, gq_c, gr_c, dq_c = split_cols(pc[..., KV_COLS:], Q_SPLITS)
        q_c = gq_heads(gq_c)
        zero = jnp.zeros((bsz, GLA_HEADS, GLA_DK, GLA_DV), jnp.float32)
        o_cf, s_f = gla_scan(q_c, k_c, v_c, gf_c, zero)
        o_cb, s_b = gla_scan(flip(q_c), flip(k_c), flip(v_c), flip(gb_c), zero)
        y_gla_c = gla_output(o_cf + flip(o_cb), gr_c, gla_norm_w, hc.dtype)
    else:
        s_f = gla_final_state(k_c, v_c, gf_c)
        s_b = gla_final_state(flip(k_c), flip(v_c), flip(gb_c))
    o_lf, _ = gla_scan(q_l, k_l, v_l, gf_l, s_f)
    o_lb, _ = gla_scan(flip(q_l), flip(k_l), flip(v_l), flip(gb_l), s_b)
    y_gla = gla_output(o_lf + flip(o_lb), gr, gla_norm_w, h.dtype)

    kn_c, krp_c, vv_c = mla_kv(ckv_c, kr_c, mla_kv_norm_w, mla_w_ukv, None)
    kn_l, krp_l, vv_l = mla_kv(ckv, kr, mla_kv_norm_w, mla_w_ukv, ang)
    qn_l, qr_l = mla_q(dq, mla_q_norm_w, mla_w_uq, ang)
    y_mla = mla_attend_blocked(qn_l, qr_l, jnp.concatenate([kn_c, kn_l], axis=1),
                               jnp.concatenate([krp_c, krp_l], axis=1), jnp.concatenate([vv_c, vv_l], axis=1))
    y = jnp.concatenate([y_sgu, y_gla, y_mla], axis=-1)

    yc = None
    if with_ctx_out:
        y_sgu_c = sgu(su_c, sv_c, sgu_norm_w, sgu_norm_b, sgu_w, sgu_b)
        qn_c, qr_c = mla_q(dq_c, mla_q_norm_w, mla_w_uq, None)
        y_mla_c = mla_attend(qn_c, qr_c, kn_c, krp_c, vv_c).reshape(bsz, -1, MLA_WIDTH)
        yc = jnp.concatenate([y_sgu_c, y_gla_c, y_mla_c], axis=-1)
    return y, yc


def ffn(h, w1, w2):
    a = jax.nn.relu(h @ w1)
    return (a * a) @ w2


def _fwd_setup_inputs(seed: int = 0) -> dict:
    key = jax.random.key(seed)
    ks = jax.random.split(key, 32)
    L, D = DEPTH, D_MODEL

    def nrm(k, shape, scale):
        return jax.random.normal(k, shape, jnp.float32) * scale

    return {
        "x": nrm(ks[0], (BATCH, SEQ, D), 1.0),
        "c": nrm(ks[1], (BATCH, D), 1.0),
        "ctx": nrm(ks[2], (BATCH, CTX_LEN, D), 1.0),
        "c_ctx": nrm(ks[3], (D,), 1.0),
        "w_mod": nrm(ks[4], (L, D, 6 * D), 0.5 * D ** -0.5),
        "b_mod": nrm(ks[5], (L, 6 * D), 0.02),
        "norm1_w": 1.0 + nrm(ks[6], (L, D), 0.02),
        "w_in": nrm(ks[7], (L, D, IN_COLS), D ** -0.5),
        "w_out": nrm(ks[8], (L, MIX_WIDTH, D), MIX_WIDTH ** -0.5),
        "sgu_norm_w": 1.0 + nrm(ks[9], (L, SGU_WIDTH), 0.02),
        "sgu_norm_b": nrm(ks[10], (L, SGU_WIDTH), 0.02),
        "sgu_w": nrm(ks[11], (L, SGU_HEADS, SGU_CHUNK, SGU_CHUNK), SGU_CHUNK ** -0.5),
        "sgu_b": 1.0 + nrm(ks[12], (L, SGU_HEADS, SGU_CHUNK), 0.02),
        "gla_wg_fwd": nrm(ks[13], (L, GLA_GATE_RANK, GLA_K_WIDTH), GLA_GATE_RANK ** -0.5),
        "gla_bg_fwd": nrm(ks[14], (L, GLA_K_WIDTH), 0.1),
        "gla_wg_bwd": nrm(ks[15], (L, GLA_GATE_RANK, GLA_K_WIDTH), GLA_GATE_RANK ** -0.5),
        "gla_bg_bwd": nrm(ks[16], (L, GLA_K_WIDTH), 0.1),
        "gla_norm_w": 1.0 + nrm(ks[17], (L, GLA_DV), 0.02),
        "mla_q_norm_w": 1.0 + nrm(ks[18], (L, MLA_Q_RANK), 0.02),
        "mla_w_uq": nrm(ks[19], (L, MLA_Q_RANK, MLA_HEADS * (MLA_NOPE + MLA_ROPE)), MLA_Q_RANK ** -0.5),
        "mla_kv_norm_w": 1.0 + nrm(ks[20], (L, MLA_KV_RANK), 0.02),
        "mla_w_ukv": nrm(ks[21], (L, MLA_KV_RANK, MLA_HEADS * (MLA_NOPE + MLA_V)), MLA_KV_RANK ** -0.5),
        "norm2_w": 1.0 + nrm(ks[22], (L, D), 0.02),
        "w_ff1": nrm(ks[23], (L, D, D_FF), D ** -0.5),
        "w_ff2": nrm(ks[24], (L, D_FF, D), D_FF ** -0.5),
        "final_norm_w": 1.0 + nrm(ks[25], (D,), 0.02),
    }


def _fwd_reference(x, c, ctx, c_ctx, w_mod, b_mod, norm1_w, w_in, w_out, sgu_norm_w, sgu_norm_b, sgu_w, sgu_b,
              gla_wg_fwd, gla_bg_fwd, gla_wg_bwd, gla_bg_bwd, gla_norm_w, mla_q_norm_w, mla_w_uq,
              mla_kv_norm_w, mla_w_ukv, norm2_w, w_ff1, w_ff2, final_norm_w):
    n = x.shape[1]
    ang = axial_angles(n)
    silu_c = jax.nn.silu(c)
    silu_cc = jax.nn.silu(c_ctx)
    xc = ctx
    for l in range(DEPTH):
        last = l == DEPTH - 1
        mod = (silu_c @ w_mod[l] + b_mod[l])[:, None, :]
        sh1, sc1, g1, sh2, sc2, g2 = jnp.split(mod, 6, axis=-1)
        if last:
            mod_c = silu_cc @ w_mod[l][:, :2 * D_MODEL] + b_mod[l][:2 * D_MODEL]
            sh1c, sc1c = jnp.split(mod_c, 2, axis=-1)
        else:
            mod_c = silu_cc @ w_mod[l] + b_mod[l]
            sh1c, sc1c, g1c, sh2c, sc2c, g2c = jnp.split(mod_c, 6, axis=-1)
        h = modulate(rmsnorm(x, norm1_w[l]), sh1, sc1)
        hc = modulate(rmsnorm(xc, norm1_w[l]), sh1c, sc1c)
        y, yc = mixing(h, hc, not last, w_in[l], sgu_norm_w[l], sgu_norm_b[l], sgu_w[l], sgu_b[l],
                       gla_wg_fwd[l], gla_bg_fwd[l], gla_wg_bwd[l], gla_bg_bwd[l], gla_norm_w[l],
                       mla_q_norm_w[l], mla_w_uq[l], mla_kv_norm_w[l], mla_w_ukv[l], ang)
        x = x + g1 * (y @ w_out[l])
        x = x + g2 * ffn(modulate(rmsnorm(x, norm2_w[l]), sh2, sc2), w_ff1[l], w_ff2[l])
        if not last:
            xc = xc + g1c * (yc @ w_out[l])
            xc = xc + g2c * ffn(modulate(rmsnorm(xc, norm2_w[l]), sh2c, sc2c), w_ff1[l], w_ff2[l])
    return rmsnorm(x, final_norm_w)


import jax as _jax
import jax.numpy as _jnp

TWIN_FORMAT = 'train_step'
FWD_PARAMS = ['x', 'c', 'ctx', 'c_ctx', 'w_mod', 'b_mod', 'norm1_w', 'w_in', 'w_out', 'sgu_norm_w', 'sgu_norm_b', 'sgu_w', 'sgu_b', 'gla_wg_fwd', 'gla_bg_fwd', 'gla_wg_bwd', 'gla_bg_bwd', 'gla_norm_w', 'mla_q_norm_w', 'mla_w_uq', 'mla_kv_norm_w', 'mla_w_ukv', 'norm2_w', 'w_ff1', 'w_ff2', 'final_norm_w']
TWIN_WEIGHTS = ['c_ctx', 'w_mod', 'b_mod', 'norm1_w', 'w_in', 'w_out', 'sgu_norm_w', 'sgu_norm_b', 'sgu_w', 'sgu_b', 'gla_wg_fwd', 'gla_bg_fwd', 'gla_wg_bwd', 'gla_bg_bwd', 'gla_norm_w', 'mla_q_norm_w', 'mla_w_uq', 'mla_kv_norm_w', 'mla_w_ukv', 'norm2_w', 'w_ff1', 'w_ff2', 'final_norm_w']
TWIN_DIFF_INPUT = 'x'
TWIN_INPUTS = ['x', 'c', 'ctx', 'c_ctx', 'w_mod', 'b_mod', 'norm1_w', 'w_in', 'w_out', 'sgu_norm_w', 'sgu_norm_b', 'sgu_w', 'sgu_b', 'gla_wg_fwd', 'gla_bg_fwd', 'gla_wg_bwd', 'gla_bg_bwd', 'gla_norm_w', 'mla_q_norm_w', 'mla_w_uq', 'mla_kv_norm_w', 'mla_w_ukv', 'norm2_w', 'w_ff1', 'w_ff2', 'final_norm_w', 'loss_target', 'm_c_ctx', 'm_w_mod', 'm_b_mod', 'm_norm1_w', 'm_w_in', 'm_w_out', 'm_sgu_norm_w', 'm_sgu_norm_b', 'm_sgu_w', 'm_sgu_b', 'm_gla_wg_fwd', 'm_gla_bg_fwd', 'm_gla_wg_bwd', 'm_gla_bg_bwd', 'm_gla_norm_w', 'm_mla_q_norm_w', 'm_mla_w_uq', 'm_mla_kv_norm_w', 'm_mla_w_ukv', 'm_norm2_w', 'm_w_ff1', 'm_w_ff2', 'm_final_norm_w', 'v_c_ctx', 'v_w_mod', 'v_b_mod', 'v_norm1_w', 'v_w_in', 'v_w_out', 'v_sgu_norm_w', 'v_sgu_norm_b', 'v_sgu_w', 'v_sgu_b', 'v_gla_wg_fwd', 'v_gla_bg_fwd', 'v_gla_wg_bwd', 'v_gla_bg_bwd', 'v_gla_norm_w', 'v_mla_q_norm_w', 'v_mla_w_uq', 'v_mla_kv_norm_w', 'v_mla_w_ukv', 'v_norm2_w', 'v_w_ff1', 'v_w_ff2', 'v_final_norm_w']
TWIN_OUTPUTS = ['loss', 'grad_x', 'grad_c_ctx', 'grad_w_mod', 'grad_b_mod', 'grad_norm1_w', 'grad_w_in', 'grad_w_out', 'grad_sgu_norm_w', 'grad_sgu_norm_b', 'grad_sgu_w', 'grad_sgu_b', 'grad_gla_wg_fwd', 'grad_gla_bg_fwd', 'grad_gla_wg_bwd', 'grad_gla_bg_bwd', 'grad_gla_norm_w', 'grad_mla_q_norm_w', 'grad_mla_w_uq', 'grad_mla_kv_norm_w', 'grad_mla_w_ukv', 'grad_norm2_w', 'grad_w_ff1', 'grad_w_ff2', 'grad_final_norm_w', 'delta_c_ctx', 'delta_w_mod', 'delta_b_mod', 'delta_norm1_w', 'delta_w_in', 'delta_w_out', 'delta_sgu_norm_w', 'delta_sgu_norm_b', 'delta_sgu_w', 'delta_sgu_b', 'delta_gla_wg_fwd', 'delta_gla_bg_fwd', 'delta_gla_wg_bwd', 'delta_gla_bg_bwd', 'delta_gla_norm_w', 'delta_mla_q_norm_w', 'delta_mla_w_uq', 'delta_mla_kv_norm_w', 'delta_mla_w_ukv', 'delta_norm2_w', 'delta_w_ff1', 'delta_w_ff2', 'delta_final_norm_w', 'new_m_c_ctx', 'new_m_w_mod', 'new_m_b_mod', 'new_m_norm1_w', 'new_m_w_in', 'new_m_w_out', 'new_m_sgu_norm_w', 'new_m_sgu_norm_b', 'new_m_sgu_w', 'new_m_sgu_b', 'new_m_gla_wg_fwd', 'new_m_gla_bg_fwd', 'new_m_gla_wg_bwd', 'new_m_gla_bg_bwd', 'new_m_gla_norm_w', 'new_m_mla_q_norm_w', 'new_m_mla_w_uq', 'new_m_mla_kv_norm_w', 'new_m_mla_w_ukv', 'new_m_norm2_w', 'new_m_w_ff1', 'new_m_w_ff2', 'new_m_final_norm_w', 'new_v_c_ctx', 'new_v_w_mod', 'new_v_b_mod', 'new_v_norm1_w', 'new_v_w_in', 'new_v_w_out', 'new_v_sgu_norm_w', 'new_v_sgu_norm_b', 'new_v_sgu_w', 'new_v_sgu_b', 'new_v_gla_wg_fwd', 'new_v_gla_bg_fwd', 'new_v_gla_wg_bwd', 'new_v_gla_bg_bwd', 'new_v_gla_norm_w', 'new_v_mla_q_norm_w', 'new_v_mla_w_uq', 'new_v_mla_kv_norm_w', 'new_v_mla_w_ukv', 'new_v_norm2_w', 'new_v_w_ff1', 'new_v_w_ff2', 'new_v_final_norm_w']
TWIN_LEAF_KINDS = {'loss': 'loss', 'grad_x': 'grad_x', 'grad_c_ctx': 'grad_w', 'grad_w_mod': 'grad_w', 'grad_b_mod': 'grad_w', 'grad_norm1_w': 'grad_w', 'grad_w_in': 'grad_w', 'grad_w_out': 'grad_w', 'grad_sgu_norm_w': 'grad_w', 'grad_sgu_norm_b': 'grad_w', 'grad_sgu_w': 'grad_w', 'grad_sgu_b': 'grad_w', 'grad_gla_wg_fwd': 'grad_w', 'grad_gla_bg_fwd': 'grad_w', 'grad_gla_wg_bwd': 'grad_w', 'grad_gla_bg_bwd': 'grad_w', 'grad_gla_norm_w': 'grad_w', 'grad_mla_q_norm_w': 'grad_w', 'grad_mla_w_uq': 'grad_w', 'grad_mla_kv_norm_w': 'grad_w', 'grad_mla_w_ukv': 'grad_w', 'grad_norm2_w': 'grad_w', 'grad_w_ff1': 'grad_w', 'grad_w_ff2': 'grad_w', 'grad_final_norm_w': 'grad_w', 'delta_c_ctx': 'delta_w', 'delta_w_mod': 'delta_w', 'delta_b_mod': 'delta_w', 'delta_norm1_w': 'delta_w', 'delta_w_in': 'delta_w', 'delta_w_out': 'delta_w', 'delta_sgu_norm_w': 'delta_w', 'delta_sgu_norm_b': 'delta_w', 'delta_sgu_w': 'delta_w', 'delta_sgu_b': 'delta_w', 'delta_gla_wg_fwd': 'delta_w', 'delta_gla_bg_fwd': 'delta_w', 'delta_gla_wg_bwd': 'delta_w', 'delta_gla_bg_bwd': 'delta_w', 'delta_gla_norm_w': 'delta_w', 'delta_mla_q_norm_w': 'delta_w', 'delta_mla_w_uq': 'delta_w', 'delta_mla_kv_norm_w': 'delta_w', 'delta_mla_w_ukv': 'delta_w', 'delta_norm2_w': 'delta_w', 'delta_w_ff1': 'delta_w', 'delta_w_ff2': 'delta_w', 'delta_final_norm_w': 'delta_w', 'new_m_c_ctx': 'new_m', 'new_m_w_mod': 'new_m', 'new_m_b_mod': 'new_m', 'new_m_norm1_w': 'new_m', 'new_m_w_in': 'new_m', 'new_m_w_out': 'new_m', 'new_m_sgu_norm_w': 'new_m', 'new_m_sgu_norm_b': 'new_m', 'new_m_sgu_w': 'new_m', 'new_m_sgu_b': 'new_m', 'new_m_gla_wg_fwd': 'new_m', 'new_m_gla_bg_fwd': 'new_m', 'new_m_gla_wg_bwd': 'new_m', 'new_m_gla_bg_bwd': 'new_m', 'new_m_gla_norm_w': 'new_m', 'new_m_mla_q_norm_w': 'new_m', 'new_m_mla_w_uq': 'new_m', 'new_m_mla_kv_norm_w': 'new_m', 'new_m_mla_w_ukv': 'new_m', 'new_m_norm2_w': 'new_m', 'new_m_w_ff1': 'new_m', 'new_m_w_ff2': 'new_m', 'new_m_final_norm_w': 'new_m', 'new_v_c_ctx': 'new_v', 'new_v_w_mod': 'new_v', 'new_v_b_mod': 'new_v', 'new_v_norm1_w': 'new_v', 'new_v_w_in': 'new_v', 'new_v_w_out': 'new_v', 'new_v_sgu_norm_w': 'new_v', 'new_v_sgu_norm_b': 'new_v', 'new_v_sgu_w': 'new_v', 'new_v_sgu_b': 'new_v', 'new_v_gla_wg_fwd': 'new_v', 'new_v_gla_bg_fwd': 'new_v', 'new_v_gla_wg_bwd': 'new_v', 'new_v_gla_bg_bwd': 'new_v', 'new_v_gla_norm_w': 'new_v', 'new_v_mla_q_norm_w': 'new_v', 'new_v_mla_w_uq': 'new_v', 'new_v_mla_kv_norm_w': 'new_v', 'new_v_mla_w_ukv': 'new_v', 'new_v_norm2_w': 'new_v', 'new_v_w_ff1': 'new_v', 'new_v_w_ff2': 'new_v', 'new_v_final_norm_w': 'new_v'}


def _forward(args):
    return _fwd_reference(*[args[k] for k in FWD_PARAMS])


def _output_shape():
    def fwd():
        inp = _fwd_setup_inputs(0)
        return _fwd_reference(*[inp[k] for k in FWD_PARAMS])
    out = _jax.eval_shape(fwd)
    return out.shape, out.dtype

N_MICROBATCH = 1
ADAM_LR = 0.001
ADAM_B1 = 0.9
ADAM_B2 = 0.999
ADAM_EPS = 1e-08
ADAM_WD = 0.01
ADAM_STEP = 10
PER_EXAMPLE_BATCH_AXIS = {'x': 0, 'c': 0, 'ctx': 0, 'loss_target': 0}
SHARED_INPUTS = []
_WEIGHT_DTYPES = {'c_ctx': _jnp.float32, 'w_mod': _jnp.float32, 'b_mod': _jnp.float32, 'norm1_w': _jnp.float32, 'w_in': _jnp.float32, 'w_out': _jnp.float32, 'sgu_norm_w': _jnp.float32, 'sgu_norm_b': _jnp.float32, 'sgu_w': _jnp.float32, 'sgu_b': _jnp.float32, 'gla_wg_fwd': _jnp.float32, 'gla_bg_fwd': _jnp.float32, 'gla_wg_bwd': _jnp.float32, 'gla_bg_bwd': _jnp.float32, 'gla_norm_w': _jnp.float32, 'mla_q_norm_w': _jnp.float32, 'mla_w_uq': _jnp.float32, 'mla_kv_norm_w': _jnp.float32, 'mla_w_ukv': _jnp.float32, 'norm2_w': _jnp.float32, 'w_ff1': _jnp.float32, 'w_ff2': _jnp.float32, 'final_norm_w': _jnp.float32}
MOMENT_SCALE = {'c_ctx': 1.386345e-02, 'w_mod': 1.177119e-01, 'b_mod': 2.192760e-01, 'norm1_w': 6.792487e-02, 'w_in': 5.384414e-02, 'w_out': 4.805524e-02, 'sgu_norm_w': 5.506524e-02, 'sgu_norm_b': 5.605754e-02, 'sgu_w': 3.698103e-02, 'sgu_b': 3.804730e-02, 'gla_wg_fwd': 9.884117e-03, 'gla_bg_fwd': 2.630056e-02, 'gla_wg_bwd': 8.629677e-03, 'gla_bg_bwd': 2.465834e-02, 'gla_norm_w': 9.696118e-02, 'mla_q_norm_w': 1.039463e-02, 'mla_w_uq': 5.852879e-03, 'mla_kv_norm_w': 3.456481e-02, 'mla_w_ukv': 1.767098e-02, 'norm2_w': 1.096913e-01, 'w_ff1': 5.645631e-02, 'w_ff2': 1.040280e-01, 'final_norm_w': 6.440045e+01}


def _to_microbatches(a, axis):
    t = _jnp.moveaxis(a, axis, 0)
    t = t.reshape((N_MICROBATCH, t.shape[0] // N_MICROBATCH) + t.shape[1:])
    return _jnp.moveaxis(t, 1, axis + 1)


def setup_inputs(seed: int = 0) -> dict:
    inp = _fwd_setup_inputs(seed)
    key = _jax.random.fold_in(_jax.random.key(seed), 7919)
    shape, _ = _output_shape()
    out = dict(inp)
    out["loss_target"] = _jax.random.normal(_jax.random.fold_in(key, 0), shape, _jnp.float32)
    for i, name in enumerate(TWIN_WEIGHTS):
        w = inp[name].astype(_jnp.float32)
        if MOMENT_SCALE is None:
            s = _jnp.sqrt(_jnp.mean(_jnp.square(w)) + 1e-30)
        else:
            s = MOMENT_SCALE[name]
        km, kv = _jax.random.split(_jax.random.fold_in(key, i + 1))
        out[name] = w
        out["m_" + name] = s * _jax.random.normal(km, w.shape, _jnp.float32)
        out["v_" + name] = (s * s) * _jax.random.uniform(kv, w.shape, _jnp.float32, 0.5, 1.5)
    if N_MICROBATCH > 1:
        for name, axis in PER_EXAMPLE_BATCH_AXIS.items():
            out[name] = _to_microbatches(out[name], axis)
    return {'x': out['x'], 'c': out['c'], 'ctx': out['ctx'], 'c_ctx': out['c_ctx'], 'w_mod': out['w_mod'], 'b_mod': out['b_mod'], 'norm1_w': out['norm1_w'], 'w_in': out['w_in'], 'w_out': out['w_out'], 'sgu_norm_w': out['sgu_norm_w'], 'sgu_norm_b': out['sgu_norm_b'], 'sgu_w': out['sgu_w'], 'sgu_b': out['sgu_b'], 'gla_wg_fwd': out['gla_wg_fwd'], 'gla_bg_fwd': out['gla_bg_fwd'], 'gla_wg_bwd': out['gla_wg_bwd'], 'gla_bg_bwd': out['gla_bg_bwd'], 'gla_norm_w': out['gla_norm_w'], 'mla_q_norm_w': out['mla_q_norm_w'], 'mla_w_uq': out['mla_w_uq'], 'mla_kv_norm_w': out['mla_kv_norm_w'], 'mla_w_ukv': out['mla_w_ukv'], 'norm2_w': out['norm2_w'], 'w_ff1': out['w_ff1'], 'w_ff2': out['w_ff2'], 'final_norm_w': out['final_norm_w'], 'loss_target': out['loss_target'], 'm_c_ctx': out['m_c_ctx'], 'm_w_mod': out['m_w_mod'], 'm_b_mod': out['m_b_mod'], 'm_norm1_w': out['m_norm1_w'], 'm_w_in': out['m_w_in'], 'm_w_out': out['m_w_out'], 'm_sgu_norm_w': out['m_sgu_norm_w'], 'm_sgu_norm_b': out['m_sgu_norm_b'], 'm_sgu_w': out['m_sgu_w'], 'm_sgu_b': out['m_sgu_b'], 'm_gla_wg_fwd': out['m_gla_wg_fwd'], 'm_gla_bg_fwd': out['m_gla_bg_fwd'], 'm_gla_wg_bwd': out['m_gla_wg_bwd'], 'm_gla_bg_bwd': out['m_gla_bg_bwd'], 'm_gla_norm_w': out['m_gla_norm_w'], 'm_mla_q_norm_w': out['m_mla_q_norm_w'], 'm_mla_w_uq': out['m_mla_w_uq'], 'm_mla_kv_norm_w': out['m_mla_kv_norm_w'], 'm_mla_w_ukv': out['m_mla_w_ukv'], 'm_norm2_w': out['m_norm2_w'], 'm_w_ff1': out['m_w_ff1'], 'm_w_ff2': out['m_w_ff2'], 'm_final_norm_w': out['m_final_norm_w'], 'v_c_ctx': out['v_c_ctx'], 'v_w_mod': out['v_w_mod'], 'v_b_mod': out['v_b_mod'], 'v_norm1_w': out['v_norm1_w'], 'v_w_in': out['v_w_in'], 'v_w_out': out['v_w_out'], 'v_sgu_norm_w': out['v_sgu_norm_w'], 'v_sgu_norm_b': out['v_sgu_norm_b'], 'v_sgu_w': out['v_sgu_w'], 'v_sgu_b': out['v_sgu_b'], 'v_gla_wg_fwd': out['v_gla_wg_fwd'], 'v_gla_bg_fwd': out['v_gla_bg_fwd'], 'v_gla_wg_bwd': out['v_gla_wg_bwd'], 'v_gla_bg_bwd': out['v_gla_bg_bwd'], 'v_gla_norm_w': out['v_gla_norm_w'], 'v_mla_q_norm_w': out['v_mla_q_norm_w'], 'v_mla_w_uq': out['v_mla_w_uq'], 'v_mla_kv_norm_w': out['v_mla_kv_norm_w'], 'v_mla_w_ukv': out['v_mla_w_ukv'], 'v_norm2_w': out['v_norm2_w'], 'v_w_ff1': out['v_w_ff1'], 'v_w_ff2': out['v_w_ff2'], 'v_final_norm_w': out['v_final_norm_w']}


def _loss(weights, diff, rest, loss_target):
    with _jax.named_scope("forward"):
        args = {**rest, TWIN_DIFF_INPUT: diff, **{k: w.astype(_WEIGHT_DTYPES[k]) for k, w in weights.items()}}
        y = _forward(args)
    with _jax.named_scope("loss_head"):
        err = _jnp.square(y.astype(_jnp.float32) - loss_target)
        return 0.5 * _jnp.sum(_jnp.mean(err, axis=-1)) if err.ndim else 0.5 * err


def _adamw(w, g, m, v):
    m = ADAM_B1 * m + (1.0 - ADAM_B1) * g
    v = ADAM_B2 * v + (1.0 - ADAM_B2) * _jnp.square(g)
    m_hat = m / (1.0 - ADAM_B1 ** ADAM_STEP)
    v_hat = v / (1.0 - ADAM_B2 ** ADAM_STEP)
    delta = -ADAM_LR * (m_hat / (_jnp.sqrt(v_hat) + ADAM_EPS) + ADAM_WD * w)
    return delta, m, v


def reference(x, c, ctx, c_ctx, w_mod, b_mod, norm1_w, w_in, w_out, sgu_norm_w, sgu_norm_b, sgu_w, sgu_b, gla_wg_fwd, gla_bg_fwd, gla_wg_bwd, gla_bg_bwd, gla_norm_w, mla_q_norm_w, mla_w_uq, mla_kv_norm_w, mla_w_ukv, norm2_w, w_ff1, w_ff2, final_norm_w, loss_target, m_c_ctx, m_w_mod, m_b_mod, m_norm1_w, m_w_in, m_w_out, m_sgu_norm_w, m_sgu_norm_b, m_sgu_w, m_sgu_b, m_gla_wg_fwd, m_gla_bg_fwd, m_gla_wg_bwd, m_gla_bg_bwd, m_gla_norm_w, m_mla_q_norm_w, m_mla_w_uq, m_mla_kv_norm_w, m_mla_w_ukv, m_norm2_w, m_w_ff1, m_w_ff2, m_final_norm_w, v_c_ctx, v_w_mod, v_b_mod, v_norm1_w, v_w_in, v_w_out, v_sgu_norm_w, v_sgu_norm_b, v_sgu_w, v_sgu_b, v_gla_wg_fwd, v_gla_bg_fwd, v_gla_wg_bwd, v_gla_bg_bwd, v_gla_norm_w, v_mla_q_norm_w, v_mla_w_uq, v_mla_kv_norm_w, v_mla_w_ukv, v_norm2_w, v_w_ff1, v_w_ff2, v_final_norm_w):
    given = dict(x=x, c=c, ctx=ctx, c_ctx=c_ctx, w_mod=w_mod, b_mod=b_mod, norm1_w=norm1_w, w_in=w_in, w_out=w_out, sgu_norm_w=sgu_norm_w, sgu_norm_b=sgu_norm_b, sgu_w=sgu_w, sgu_b=sgu_b, gla_wg_fwd=gla_wg_fwd, gla_bg_fwd=gla_bg_fwd, gla_wg_bwd=gla_wg_bwd, gla_bg_bwd=gla_bg_bwd, gla_norm_w=gla_norm_w, mla_q_norm_w=mla_q_norm_w, mla_w_uq=mla_w_uq, mla_kv_norm_w=mla_kv_norm_w, mla_w_ukv=mla_w_ukv, norm2_w=norm2_w, w_ff1=w_ff1, w_ff2=w_ff2, final_norm_w=final_norm_w, loss_target=loss_target, m_c_ctx=m_c_ctx, m_w_mod=m_w_mod, m_b_mod=m_b_mod, m_norm1_w=m_norm1_w, m_w_in=m_w_in, m_w_out=m_w_out, m_sgu_norm_w=m_sgu_norm_w, m_sgu_norm_b=m_sgu_norm_b, m_sgu_w=m_sgu_w, m_sgu_b=m_sgu_b, m_gla_wg_fwd=m_gla_wg_fwd, m_gla_bg_fwd=m_gla_bg_fwd, m_gla_wg_bwd=m_gla_wg_bwd, m_gla_bg_bwd=m_gla_bg_bwd, m_gla_norm_w=m_gla_norm_w, m_mla_q_norm_w=m_mla_q_norm_w, m_mla_w_uq=m_mla_w_uq, m_mla_kv_norm_w=m_mla_kv_norm_w, m_mla_w_ukv=m_mla_w_ukv, m_norm2_w=m_norm2_w, m_w_ff1=m_w_ff1, m_w_ff2=m_w_ff2, m_final_norm_w=m_final_norm_w, v_c_ctx=v_c_ctx, v_w_mod=v_w_mod, v_b_mod=v_b_mod, v_norm1_w=v_norm1_w, v_w_in=v_w_in, v_w_out=v_w_out, v_sgu_norm_w=v_sgu_norm_w, v_sgu_norm_b=v_sgu_norm_b, v_sgu_w=v_sgu_w, v_sgu_b=v_sgu_b, v_gla_wg_fwd=v_gla_wg_fwd, v_gla_bg_fwd=v_gla_bg_fwd, v_gla_wg_bwd=v_gla_wg_bwd, v_gla_bg_bwd=v_gla_bg_bwd, v_gla_norm_w=v_gla_norm_w, v_mla_q_norm_w=v_mla_q_norm_w, v_mla_w_uq=v_mla_w_uq, v_mla_kv_norm_w=v_mla_kv_norm_w, v_mla_w_ukv=v_mla_w_ukv, v_norm2_w=v_norm2_w, v_w_ff1=v_w_ff1, v_w_ff2=v_w_ff2, v_final_norm_w=v_final_norm_w)
    weights = {n: given[n] for n in TWIN_WEIGHTS}
    shared = {n: given[n] for n in SHARED_INPUTS}
    per_example = {n: given[n] for n in ['x', 'c', 'ctx']}
    grad_fn = _jax.value_and_grad(_loss, argnums=(0, 1))

    def one_microbatch(ex, loss_target):
        ex = dict(ex)
        diff = ex.pop(TWIN_DIFF_INPUT)
        return grad_fn(weights, diff, {**shared, **ex}, loss_target)

    if N_MICROBATCH == 1:
        loss, (grad_w, grad_x) = one_microbatch(per_example, given["loss_target"])
    else:
        def body(carry, xs):
            loss_sum, grad_sum = carry
            l_k, (gw_k, gx_k) = one_microbatch(xs[0], xs[1])
            with _jax.named_scope("update"):
                return (loss_sum + l_k, _jax.tree.map(_jnp.add, grad_sum, gw_k)), gx_k

        init = (_jnp.zeros((), _jnp.float32), _jax.tree.map(_jnp.zeros_like, weights))
        (loss, grad_w), grad_x = _jax.lax.scan(body, init, (per_example, given["loss_target"]))
    with _jax.named_scope("update"):
        delta_w, new_m, new_v = {}, {}, {}
        for n in TWIN_WEIGHTS:
            delta_w[n], new_m[n], new_v[n] = _adamw(weights[n], grad_w[n], given["m_" + n], given["v_" + n])
    return (loss, grad_x, *[grad_w[n] for n in TWIN_WEIGHTS], *[delta_w[n] for n in TWIN_WEIGHTS],
            *[new_m[n] for n in TWIN_WEIGHTS], *[new_v[n] for n in TWIN_WEIGHTS])
```

```python
import functools

import numpy as np
import jax
import jax.numpy as jnp
from jax import lax
from jax.experimental import pallas as pl
from jax.experimental.pallas import tpu as pltpu

F32 = jnp.float32
BF = jnp.bfloat16

N_DEV = 8
AXES = ("x", "y", "c")
EPS = 1e-6
D = 1024
TILE = 256
GCH = 64
SGU_CHUNK = 128
HEADS = 4
ROPE_BASE = 10000.0
GRID_W = 64
GLA_TAU = 16.0
ATT_SCALE = (128 + 64) ** -0.5
D_FF = 4096
FF_CH = 1024

ADAM_LR = 0.001
ADAM_B1 = 0.9
ADAM_B2 = 0.999
ADAM_EPS = 1e-08
ADAM_WD = 0.01
ADAM_STEP = 10

VMEM_LIMIT_MB = 56

W_IN_SEGS = ((0, 0, 128), (128, 128, 256), (384, 384, 16), (512, 400, 16), (640, 416, 256), (896, 672, 64),
             (1024, 736, 256), (1280, 992, 256), (1536, 1248, 128), (1664, 1376, 256), (1920, 1632, 256))
P_COLS = 2176
O_GK, O_GV, O_GGF, O_GGB, O_CKV, O_KR, O_SU, O_SV, O_GQ, O_GR, O_DQ = (s[0] for s in W_IN_SEGS)
W_UQ_SEGS = tuple((h * 256, h * 192, 192) for h in range(HEADS))

SMALL_NAMES = ("c_ctx", "b_mod", "norm1_w", "sgu_norm_w", "sgu_norm_b", "sgu_w", "sgu_b", "gla_wg_fwd", "gla_bg_fwd",
               "gla_wg_bwd", "gla_bg_bwd", "gla_norm_w", "mla_q_norm_w", "mla_kv_norm_w", "norm2_w", "final_norm_w")
BIG_NAMES = ("w_in", "w_out", "mla_w_uq", "mla_w_ukv", "w_ff1", "w_ff2")
WEIGHT_ORDER = ("c_ctx", "w_mod", "b_mod", "norm1_w", "w_in", "w_out", "sgu_norm_w", "sgu_norm_b", "sgu_w", "sgu_b",
                "gla_wg_fwd", "gla_bg_fwd", "gla_wg_bwd", "gla_bg_bwd", "gla_norm_w", "mla_q_norm_w", "mla_w_uq",
                "mla_kv_norm_w", "mla_w_ukv", "norm2_w", "w_ff1", "w_ff2", "final_norm_w")


def _cparams(n_axes):
    return pltpu.CompilerParams(dimension_semantics=("arbitrary",) * n_axes,
                                vmem_limit_bytes=VMEM_LIMIT_MB * 1024 * 1024)


def _dot(a, b):
    return jnp.dot(a, b, preferred_element_type=F32)


def _dot_nt(a, b):
    return lax.dot_general(a, b, (((1,), (1,)), ((), ())), preferred_element_type=F32)


def _dot_tn(a, b):
    return lax.dot_general(a, b, (((0,), (0,)), ((), ())), preferred_element_type=F32)


def _nn(a, b):
    return _dot(a.astype(BF), b.astype(BF))


def _nt(a, b):
    return _dot_nt(a.astype(BF), b.astype(BF))


def _tn(a, b):
    return _dot_tn(a.astype(BF), b.astype(BF))


nn_d = jax.custom_vjp(_nn)
nt_d = jax.custom_vjp(_nt)
tn_d = jax.custom_vjp(_tn)
nn_d.defvjp(lambda a, b: (_nn(a, b), (a, b)), lambda r, dy: (_nt(dy, r[1]), _tn(r[0], dy)))
nt_d.defvjp(lambda a, b: (_nt(a, b), (a, b)), lambda r, dy: (_nn(dy, r[1]), _tn(dy, r[0])))
tn_d.defvjp(lambda a, b: (_tn(a, b), (a, b)), lambda r, dy: (_nt(r[1], dy), _nn(r[0], dy)))


def nn_const(w_bf, diff):
    def raw(a):
        return _dot(a.astype(BF), w_bf)

    if not diff:
        return raw
    f = jax.custom_vjp(raw)
    f.defvjp(lambda a: (raw(a), None), lambda _, dy: (_dot_nt(dy.astype(BF), w_bf),))
    return f


def _split3(g):
    hi = g.astype(BF)
    r = g - hi.astype(F32)
    mid = r.astype(BF)
    lo = (r - mid.astype(F32)).astype(BF)
    return hi, mid, lo


def make_cum(tri_bf, tri_t_bf, diff):
    def raw(g, t):
        hi, mid, lo = _split3(g)
        return _dot(t, hi) + _dot(t, mid) + _dot(t, lo)

    def fwd(g):
        return raw(g, tri_bf)

    if not diff:
        return fwd
    cum = jax.custom_vjp(fwd)
    cum.defvjp(lambda g: (fwd(g), None), lambda _, db: (raw(db, tri_t_bf),))
    return cum


def _roll_lanes(x, shift):
    return pltpu.roll(x, shift, 1)


def make_rope(c, sa, sb, diff):
    def raw(x):
        return x * c + _roll_lanes(x, 112) * sa + _roll_lanes(x, 16) * sb

    if not diff:
        return raw
    f = jax.custom_vjp(raw)
    f.defvjp(lambda x: (raw(x), None),
             lambda _, dy: (dy * c + _roll_lanes(dy * sa, 16) + _roll_lanes(dy * sb, 112),))
    return f


def _ops(diff):
    return (nn_d, nt_d, tn_d) if diff else (_nn, _nt, _tn)


def _rms(x, w):
    return x * lax.rsqrt(jnp.mean(x * x, axis=-1, keepdims=True) + EPS) * w


def _gelu(x):
    return 0.5 * x * (1.0 + jnp.tanh(0.7978845608028654 * (x + 0.044715 * (x * x * x))))


def _silu(x):
    return x * jax.nn.sigmoid(x)


def _log_sigmoid(z):
    return jnp.minimum(z, 0.0) - jnp.log(1.0 + jnp.exp(-jnp.abs(z)))


def _lane_group_mask(width, group, h):
    lane = lax.broadcasted_iota(jnp.int32, (1, width), 1)
    return ((lane >= h * group) & (lane < (h + 1) * group)).astype(F32)


def pre_tile(d, c, z):
    nn, _, _ = _ops(z is not None)
    rope = make_rope(c["rc"], c["rsa"], c["rsb"], z is not None)
    mod = d["mod"]
    h = _rms(d["x"], d["n1w"]) * (1.0 + mod[1:2]) + mod[0:1]
    p = nn_const(c["w_in"], z is not None)(h)
    if z is not None:
        p = p + z["zp"]
    gk, gv = p[:, O_GK:O_GK + 128], p[:, O_GV:O_GV + 256]
    ggf, ggb = p[:, O_GGF:O_GGF + 128], p[:, O_GGB:O_GGB + 128]
    ckv, kr = p[:, O_CKV:O_CKV + 256], p[:, O_KR:O_KR + 128]
    su, sv = p[:, O_SU:O_SU + 256], p[:, O_SV:O_SV + 256]
    gq, gr, dq = p[:, O_GQ:O_GQ + 128], p[:, O_GR:O_GR + 256], p[:, O_DQ:O_DQ + 256]

    u = _gelu(su)
    gv_ = _gelu(sv)
    mu = jnp.mean(gv_, axis=-1, keepdims=True)
    cen = gv_ - mu
    vn = cen * lax.rsqrt(jnp.mean(cen * cen, axis=-1, keepdims=True) + EPS) * d["sgu_nw"] + d["sgu_nb"]
    hm = [_lane_group_mask(256, 64, hh) for hh in range(HEADS)]
    rows = []
    for ci in range(vn.shape[0] // SGU_CHUNK):
        vc = vn[ci * SGU_CHUNK:(ci + 1) * SGU_CHUNK]
        s = d["sgu_bx"]
        for hh in range(HEADS):
            s = s + hm[hh] * nn(d["sgu_w"][hh], vc)
        rows.append(s)
    y_sgu = u * jnp.concatenate(rows, axis=0)

    qg = gq * (32 ** -0.5)
    lgf = _log_sigmoid(nn(ggf, d["wgf"]) + d["bgf"]) * (1.0 / GLA_TAU)
    lgb = _log_sigmoid(nn(ggb, d["wgb"]) + d["bgb"]) * (1.0 / GLA_TAU)

    kv = nn(_rms(ckv, d["kvnw"]), d["w_ukv"])
    kr_r = rope(kr)
    q = nn(_rms(dq, d["qnw"]), d["w_uq"])
    qs, ks, vs = [], [], []
    for hh in range(HEADS):
        qs += [q[:, hh * 256:hh * 256 + 128], rope(q[:, hh * 256 + 128:(hh + 1) * 256])]
        ks += [kv[:, hh * 256:hh * 256 + 128], kr_r]
        vs += [kv[:, hh * 256 + 128:(hh + 1) * 256]]
    outs = dict(y_sgu=y_sgu, qg=qg, kg=gk, vg=gv, lgf=lgf, lgb=lgb, gr=gr,
                q_cat=jnp.concatenate(qs, axis=-1), k_cat=jnp.concatenate(ks, axis=-1), v=jnp.concatenate(vs, axis=-1))
    return outs, dict(a_in=h)


def attn_out_tile(d, c, z):
    mod = d["mod"]
    o = d["o_f"] + d["o_b"]
    ms = jnp.zeros_like(o)
    for hh in range(HEADS):
        m_h = _lane_group_mask(256, 64, hh)
        ms = ms + m_h * (jnp.sum(o * o * m_h, axis=-1, keepdims=True) * (1.0 / 64))
    yg = o * lax.rsqrt(ms + EPS) * d["gnw"] * _silu(d["gr"])
    y = jnp.concatenate([d["y_sgu"], yg, d["y_mla"]], axis=-1)
    t = nn_const(c["w_out"], z is not None)(y)
    if z is not None:
        t = t + z["zt"]
    return dict(x1=d["x"] + mod[2:3] * t), dict(a_out=y)


def ffn_tile(d, c, z):
    mod = d["mod"]
    h2 = _rms(d["x1"], d["n2w"]) * (1.0 + mod[4:5]) + mod[3:4]
    f = None
    a2s = []
    for j in range(D_FF // FF_CH):
        pre = nn_const(c["w_ff1"][j], z is not None)(h2)
        if z is not None:
            pre = pre + z["zpre"][:, j * FF_CH:(j + 1) * FF_CH]
        a = jnp.maximum(pre, 0.0)
        a2 = a * a
        a2s.append(a2)
        fj = nn_const(c["w_ff2"][j], z is not None)(a2)
        f = fj if f is None else f + fj
    if z is not None:
        f = f + z["zf"]
    return dict(x2=d["x1"] + mod[5:6] * f), dict(a_ff1=h2, a_ff2=jnp.concatenate(a2s, axis=-1))


def _in_spec(kind, arr, tile):
    if kind == "tile":
        return pl.BlockSpec((tile, arr.shape[1]), lambda i: (i, 0))
    if kind == "kind":
        return pl.BlockSpec((1,) + arr.shape[1:], lambda i: (jnp.where(i < TILE // tile, 0, 1), 0, 0))
    nd = arr.ndim
    if kind in ("wfull", "wcols", "wrows"):
        return pl.BlockSpec(arr.shape, lambda i: (0,) * nd, pipeline_mode=pl.Buffered(1))
    return pl.BlockSpec(arr.shape, lambda i: (0,) * nd)


def _load(kind, ref):
    if kind == "kind":
        return ref[0]
    if kind == "wcols":
        return [ref[:, j * FF_CH:(j + 1) * FF_CH] for j in range(ref.shape[1] // FF_CH)]
    if kind == "wrows":
        return [ref[j * FF_CH:(j + 1) * FF_CH, :] for j in range(ref.shape[0] // FF_CH)]
    return ref[...]


def tile_forward(fn, name, t_all, ins, out_defs, tile=TILE):
    keys = [k for k, _, _, _ in ins]
    kinds = [kd for _, kd, _, _ in ins]
    diffs = [df for _, _, df, _ in ins]
    arrs = [a for _, _, _, a in ins]
    n_in = len(ins)

    def body(*refs):
        vals = [_load(kinds[j], refs[j]) for j in range(n_in)]
        d = {keys[j]: vals[j] for j in range(n_in) if diffs[j]}
        c = {keys[j]: vals[j] for j in range(n_in) if not diffs[j]}
        outs, _ = fn(d, c, None)
        for j, (k, _, dt) in enumerate(out_defs):
            refs[n_in + j][...] = outs[k].astype(dt)

    res = pl.pallas_call(
        body, name=name, grid=(t_all // tile,),
        in_specs=[_in_spec(kinds[j], arrs[j], tile) for j in range(n_in)],
        out_specs=[pl.BlockSpec((tile, w), lambda i: (i, 0)) for _, w, _ in out_defs],
        out_shape=[jax.ShapeDtypeStruct((t_all, w), dt) for _, w, dt in out_defs],
        compiler_params=_cparams(1),
    )(*arrs)
    return {k: r for (k, _, _), r in zip(out_defs, res)}


def tile_backward(fn, name, t_all, ins, cots, z_defs, aux_defs, tile=TILE):
    keys = [k for k, _, _, _ in ins]
    kinds = [kd for _, kd, _, _ in ins]
    diffs = [df for _, _, df, _ in ins]
    arrs = [a for _, _, _, a in ins]
    n_in, n_cot = len(ins), len(cots)
    dkeys = [j for j in range(n_in) if diffs[j]]
    ctx_tiles = TILE // tile

    def body(*refs):
        i = pl.program_id(0)
        vals = [_load(kinds[j], refs[j]) for j in range(n_in)]
        d = {keys[j]: vals[j] for j in dkeys}
        c = {keys[j]: vals[j] for j in range(n_in) if not diffs[j]}
        zs = {k: jnp.zeros((tile, w), F32) for k, w in z_defs}
        outs, vjp_fn, aux = jax.vjp(lambda dd, zz: fn(dd, c, zz), d, zs, has_aux=True)
        ct = {k: refs[n_in + j][...] for j, (k, _) in enumerate(cots)}
        dd, dz = vjp_fn({k: ct[k].astype(outs[k].dtype) for k in outs})
        base = n_in + n_cot
        for n, j in enumerate(dkeys):
            ref, g = refs[base + n], dd[keys[j]]
            if kinds[j] == "tile":
                ref[...] = g
            else:
                first = ((i == 0) | (i == ctx_tiles)) if kinds[j] == "kind" else (i == 0)
                gv = g[None] if kinds[j] == "kind" else g

                @pl.when(first)
                def _(ref=ref, gv=gv):
                    ref[...] = gv

                @pl.when(jnp.logical_not(first))
                def _(ref=ref, gv=gv):
                    ref[...] += gv
        base += len(dkeys)
        for n, (k, _) in enumerate(z_defs):
            refs[base + n][...] = dz[k].astype(BF)
        base += len(z_defs)
        for n, (k, _) in enumerate(aux_defs):
            refs[base + n][...] = aux[k].astype(BF)

    out_specs, out_shape = [], []
    for j in dkeys:
        out_specs.append(_in_spec(kinds[j], arrs[j], tile))
        out_shape.append(jax.ShapeDtypeStruct(arrs[j].shape, F32))
    for _, w in list(z_defs) + list(aux_defs):
        out_specs.append(pl.BlockSpec((tile, w), lambda i: (i, 0)))
        out_shape.append(jax.ShapeDtypeStruct((t_all, w), BF))
    res = pl.pallas_call(
        body, name=name, grid=(t_all // tile,),
        in_specs=[_in_spec(kinds[j], arrs[j], tile) for j in range(n_in)]
        + [pl.BlockSpec((tile, a.shape[1]), lambda i: (i, 0)) for _, a in cots],
        out_specs=out_specs, out_shape=out_shape, compiler_params=_cparams(1),
    )(*arrs, *[a for _, a in cots])
    grads = {keys[j]: res[n] for n, j in enumerate(dkeys)}
    extra = {k: res[len(dkeys) + n] for n, (k, _) in enumerate(list(z_defs) + list(aux_defs))}
    return grads, extra


def wgrad(a, b, name, bk1=1024, bk2=1024):
    t, k1 = a.shape
    k2 = b.shape[1]
    bk1, bk2 = min(bk1, k1), min(bk2, k2)
    nt_ = t // TILE

    def body(a_ref, b_ref, o_ref, acc):
        s = pl.program_id(2)

        @pl.when(s == 0)
        def _():
            acc[...] = jnp.zeros_like(acc)

        acc[...] += _dot_tn(a_ref[...], b_ref[...])

        @pl.when(s == nt_ - 1)
        def _():
            o_ref[...] = acc[...]

    return pl.pallas_call(
        body, name=name, grid=(k1 // bk1, k2 // bk2, nt_),
        in_specs=[pl.BlockSpec((TILE, bk1), lambda i, j, s: (s, i)), pl.BlockSpec((TILE, bk2), lambda i, j, s: (s, j))],
        out_specs=pl.BlockSpec((bk1, bk2), lambda i, j, s: (i, j)),
        out_shape=jax.ShapeDtypeStruct((k1, k2), F32),
        scratch_shapes=[pltpu.VMEM((bk1, bk2), F32)],
        compiler_params=_cparams(3),
    )(a, b)


def _gla_consts(reverse, diff):
    r = lax.broadcasted_iota(jnp.int32, (GCH, GCH), 0)
    cc = lax.broadcasted_iota(jnp.int32, (GCH, GCH), 1)
    low = (r >= cc)
    tri = (jnp.logical_not(low) | (r == cc)) if reverse else low
    tri_f = tri.astype(F32)
    tri_t = (cc >= r) if not reverse else (cc <= r)
    hmk = [_lane_group_mask(128, 32, h) for h in range(HEADS)]
    hmv = [_lane_group_mask(256, 64, h) for h in range(HEADS)]
    e = lax.broadcasted_iota(jnp.int32, (256, 128), 0) // 64
    dk = lax.broadcasted_iota(jnp.int32, (256, 128), 1) // 32
    return dict(cum=make_cum(tri_f.astype(BF), tri_t.astype(F32).astype(BF), diff), ops=_ops(diff),
                tri4=jnp.concatenate([tri_f] * HEADS, axis=0), hmk=hmk, hmv=hmv, bd=(e == dk).astype(F32))


def gla_chunk(st, q, k, v, g, cs):
    nn, nt, tn = cs["ops"]
    b = cs["cum"](g)
    bl = jnp.sum(g, axis=0, keepdims=True)
    qe = q * jnp.exp(b)
    ke = k * jnp.exp(-b)
    qstack = jnp.concatenate([qe * cs["hmk"][h] for h in range(HEADS)], axis=0)
    att = nt(qstack, ke) * cs["tri4"]
    ofull = nn(att, v)
    o = nt(qe, st)
    for h in range(HEADS):
        o = o + ofull[h * GCH:(h + 1) * GCH] * cs["hmv"][h]
    kd = k * jnp.exp(bl - b)
    st_new = st * jnp.exp(bl) + tn(v, kd) * cs["bd"]
    return st_new, o


def _gla_chunk_index(s, n_ch, reverse):
    ctx_ch = TILE // GCH
    if not reverse:
        return s
    return jnp.where(s < ctx_ch, ctx_ch - 1 - s, n_ch - 1 + ctx_ch - s)


def gla_forward(q, k, v, g, reverse, name):
    t = q.shape[0]
    n_ch = t // GCH

    def body(q_ref, k_ref, v_ref, g_ref, o_ref, sst_ref, st):
        s = pl.program_id(0)

        @pl.when(s == 0)
        def _():
            st[...] = jnp.zeros_like(st)

        cur = st[...]
        sst_ref[0] = cur
        st_new, o = gla_chunk(cur, q_ref[...], k_ref[...], v_ref[...], g_ref[...], _gla_consts(reverse, False))
        o_ref[...] = o
        st[...] = st_new

    def im(s):
        return (_gla_chunk_index(s, n_ch, reverse), 0)

    return pl.pallas_call(
        body, name=name, grid=(n_ch,),
        in_specs=[pl.BlockSpec((GCH, 128), im), pl.BlockSpec((GCH, 128), im), pl.BlockSpec((GCH, 256), im),
                  pl.BlockSpec((GCH, 128), im)],
        out_specs=[pl.BlockSpec((GCH, 256), im),
                   pl.BlockSpec((1, 256, 128), lambda s: (_gla_chunk_index(s, n_ch, reverse), 0, 0))],
        out_shape=[jax.ShapeDtypeStruct((t, 256), F32), jax.ShapeDtypeStruct((n_ch, 256, 128), F32)],
        scratch_shapes=[pltpu.VMEM((256, 128), F32)],
        compiler_params=_cparams(1),
    )(q, k, v, g)


def gla_backward(q, k, v, g, sst, do, reverse, name):
    t = q.shape[0]
    n_ch = t // GCH

    def body(q_ref, k_ref, v_ref, g_ref, sst_ref, do_ref, dq_ref, dk_ref, dv_ref, dg_ref, dst):
        r = pl.program_id(0)

        @pl.when(r == 0)
        def _():
            dst[...] = jnp.zeros_like(dst)

        cs = _gla_consts(reverse, True)
        _, vjp_fn = jax.vjp(lambda a, b, c_, d_, e_: gla_chunk(a, b, c_, d_, e_, cs),
                            sst_ref[0], q_ref[...], k_ref[...], v_ref[...], g_ref[...])
        dstp, dq, dk, dv, dg = vjp_fn((dst[...], do_ref[...]))
        dq_ref[...] = dq
        dk_ref[...] = dk
        dv_ref[...] = dv
        dg_ref[...] = dg
        dst[...] = dstp

    def im(r):
        return (_gla_chunk_index(n_ch - 1 - r, n_ch, reverse), 0)

    return pl.pallas_call(
        body, name=name, grid=(n_ch,),
        in_specs=[pl.BlockSpec((GCH, 128), im), pl.BlockSpec((GCH, 128), im), pl.BlockSpec((GCH, 256), im),
                  pl.BlockSpec((GCH, 128), im),
                  pl.BlockSpec((1, 256, 128), lambda r: (_gla_chunk_index(n_ch - 1 - r, n_ch, reverse), 0, 0)),
                  pl.BlockSpec((GCH, 256), im)],
        out_specs=[pl.BlockSpec((GCH, 128), im), pl.BlockSpec((GCH, 128), im), pl.BlockSpec((GCH, 256), im),
                   pl.BlockSpec((GCH, 128), im)],
        out_shape=[jax.ShapeDtypeStruct((t, 128), F32), jax.ShapeDtypeStruct((t, 128), F32),
                   jax.ShapeDtypeStruct((t, 256), F32), jax.ShapeDtypeStruct((t, 128), F32)],
        scratch_shapes=[pltpu.VMEM((256, 128), F32)],
        compiler_params=_cparams(1),
    )(q, k, v, g, sst, do)


def _resident(hbm_ref, vmem_ref, sem):
    cp = pltpu.make_async_copy(hbm_ref, vmem_ref, sem)
    cp.start()
    cp.wait()


def mla_forward(q_cat, k_cat, v, name):
    t = q_cat.shape[0]
    n_t = t // TILE

    def body(q_ref, k_hbm, v_hbm, o_ref, lse_ref, k_s, v_s, sem):
        i = pl.program_id(0)

        @pl.when(i == 0)
        def _():
            _resident(k_hbm, k_s, sem.at[0])
            _resident(v_hbm, v_s, sem.at[1])

        n_kv = jnp.where(i == 0, 1, n_t)
        for h in range(HEADS):
            qh = q_ref[:, h * 256:(h + 1) * 256]

            def step(j, carry, h=h, qh=qh):
                m, l, acc = carry
                r0 = pl.multiple_of(j * TILE, TILE)
                kh = k_s[pl.ds(r0, TILE), h * 256:(h + 1) * 256]
                vh = v_s[pl.ds(r0, TILE), h * 128:(h + 1) * 128]
                s = _dot_nt(qh, kh) * ATT_SCALE
                mn = jnp.maximum(m, jnp.max(s, axis=-1, keepdims=True))
                p = jnp.exp(s - mn)
                al = jnp.exp(m - mn)
                return mn, al * l + jnp.sum(p, axis=-1, keepdims=True), al * acc + _dot(p.astype(BF), vh)

            m, l, acc = lax.fori_loop(0, n_kv, step, (jnp.full((TILE, 1), -1e30, F32), jnp.zeros((TILE, 1), F32),
                                                      jnp.zeros((TILE, 128), F32)))
            o_ref[:, h * 128:(h + 1) * 128] = acc / l
            lse_ref[:, h * 128:(h + 1) * 128] = jnp.broadcast_to(m + jnp.log(l), (TILE, 128))

    return pl.pallas_call(
        body, name=name, grid=(n_t,),
        in_specs=[pl.BlockSpec((TILE, 1024), lambda i: (i, 0)), pl.BlockSpec(memory_space=pl.ANY),
                  pl.BlockSpec(memory_space=pl.ANY)],
        out_specs=[pl.BlockSpec((TILE, 512), lambda i: (i, 0)), pl.BlockSpec((TILE, 512), lambda i: (i, 0))],
        out_shape=[jax.ShapeDtypeStruct((t, 512), F32), jax.ShapeDtypeStruct((t, 512), F32)],
        scratch_shapes=[pltpu.VMEM((t, 1024), BF), pltpu.VMEM((t, 512), BF), pltpu.SemaphoreType.DMA((2,))],
        compiler_params=_cparams(1),
    )(q_cat, k_cat, v)


def mla_backward_q(q_cat, k_cat, v, o, lse, do, name):
    t = q_cat.shape[0]
    n_t = t // TILE

    def body(q_ref, k_hbm, v_hbm, o_ref, lse_ref, do_ref, dq_ref, dl_ref, k_s, v_s, sem):
        i = pl.program_id(0)

        @pl.when(i == 0)
        def _():
            _resident(k_hbm, k_s, sem.at[0])
            _resident(v_hbm, v_s, sem.at[1])

        n_kv = jnp.where(i == 0, 1, n_t)
        for h in range(HEADS):
            qh = q_ref[:, h * 256:(h + 1) * 256]
            doh = do_ref[:, h * 128:(h + 1) * 128]
            delta = jnp.sum(doh * o_ref[:, h * 128:(h + 1) * 128], axis=-1, keepdims=True)
            lse_h = jnp.max(lse_ref[:, h * 128:(h + 1) * 128], axis=-1, keepdims=True)
            doh_bf = doh.astype(BF)

            def step(j, dq, h=h, qh=qh, doh_bf=doh_bf, delta=delta, lse_h=lse_h):
                r0 = pl.multiple_of(j * TILE, TILE)
                kh = k_s[pl.ds(r0, TILE), h * 256:(h + 1) * 256]
                vh = v_s[pl.ds(r0, TILE), h * 128:(h + 1) * 128]
                p = jnp.exp(_dot_nt(qh, kh) * ATT_SCALE - lse_h)
                ds = p * (_dot_nt(doh_bf, vh) - delta) * ATT_SCALE
                return dq + _dot(ds.astype(BF), kh)

            dq_ref[:, h * 256:(h + 1) * 256] = lax.fori_loop(0, n_kv, step, jnp.zeros((TILE, 256), F32))
            dl_ref[:, h * 128:(h + 1) * 128] = jnp.broadcast_to(delta, (TILE, 128))

    return pl.pallas_call(
        body, name=name, grid=(n_t,),
        in_specs=[pl.BlockSpec((TILE, 1024), lambda i: (i, 0)), pl.BlockSpec(memory_space=pl.ANY),
                  pl.BlockSpec(memory_space=pl.ANY), pl.BlockSpec((TILE, 512), lambda i: (i, 0)),
                  pl.BlockSpec((TILE, 512), lambda i: (i, 0)), pl.BlockSpec((TILE, 512), lambda i: (i, 0))],
        out_specs=[pl.BlockSpec((TILE, 1024), lambda i: (i, 0)), pl.BlockSpec((TILE, 512), lambda i: (i, 0))],
        out_shape=[jax.ShapeDtypeStruct((t, 1024), F32), jax.ShapeDtypeStruct((t, 512), F32)],
        scratch_shapes=[pltpu.VMEM((t, 1024), BF), pltpu.VMEM((t, 512), BF), pltpu.SemaphoreType.DMA((2,))],
        compiler_params=_cparams(1),
    )(q_cat, k_cat, v, o, lse, do)


def mla_backward_kv(q_cat, k_cat, v, lse, delta, do_bf, name):
    t = q_cat.shape[0]
    n_t = t // TILE

    def body(q_ref, k_ref, v_ref, lse_ref, dl_ref, do_ref, dk_ref, dv_ref, dk_s, dv_s):
        j, i = pl.program_id(0), pl.program_id(1)

        @pl.when(i == 0)
        def _():
            dk_s[...] = jnp.zeros_like(dk_s)
            dv_s[...] = jnp.zeros_like(dv_s)

        @pl.when((j == 0) | (i >= 1))
        def _():
            for h in range(HEADS):
                qh = q_ref[:, h * 256:(h + 1) * 256]
                kh = k_ref[:, h * 256:(h + 1) * 256]
                vh = v_ref[:, h * 128:(h + 1) * 128]
                doh = do_ref[:, h * 128:(h + 1) * 128]
                lse_h = jnp.max(lse_ref[:, h * 128:(h + 1) * 128], axis=-1, keepdims=True)
                dl_h = jnp.max(dl_ref[:, h * 128:(h + 1) * 128], axis=-1, keepdims=True)
                p = jnp.exp(_dot_nt(qh, kh) * ATT_SCALE - lse_h)
                ds = p * (_dot_nt(doh, vh) - dl_h) * ATT_SCALE
                dv_s[:, h * 128:(h + 1) * 128] += _dot_tn(p.astype(BF), doh)
                dk_s[:, h * 256:(h + 1) * 256] += _dot_tn(ds.astype(BF), qh)

        @pl.when(i == n_t - 1)
        def _():
            dk_ref[...] = dk_s[...]
            dv_ref[...] = dv_s[...]

    qspec = lambda w: pl.BlockSpec((TILE, w), lambda j, i: (i, 0))
    kspec = lambda w: pl.BlockSpec((TILE, w), lambda j, i: (j, 0))
    return pl.pallas_call(
        body, name=name, grid=(n_t, n_t),
        in_specs=[qspec(1024), kspec(1024), kspec(512), qspec(512), qspec(512), qspec(512)],
        out_specs=[kspec(1024), kspec(512)],
        out_shape=[jax.ShapeDtypeStruct((t, 1024), F32), jax.ShapeDtypeStruct((t, 512), F32)],
        scratch_shapes=[pltpu.VMEM((TILE, 1024), F32), pltpu.VMEM((TILE, 512), F32)],
        compiler_params=_cparams(2),
    )(q_cat, k_cat, v, lse, delta, do_bf)


def final_loss(xf, target, fnw, name):
    t = xf.shape[0]
    n_t = t // TILE

    def body(x_ref, t_ref, w_ref, loss_ref, dx_ref, dw_ref):
        i = pl.program_id(0)

        @pl.when(i == 0)
        def _():
            loss_ref[...] = jnp.zeros_like(loss_ref)
            dw_ref[...] = jnp.zeros_like(dw_ref)
            dx_ref[...] = jnp.zeros_like(dx_ref)

        @pl.when(i >= 1)
        def _():
            y, vjp_fn = jax.vjp(_rms, x_ref[...], w_ref[...])
            err = y - t_ref[...]
            loss_ref[...] += jnp.broadcast_to(0.5 * jnp.sum(jnp.mean(err * err, axis=-1, keepdims=True)), (8, 128))
            dx, dw = vjp_fn(err * (1.0 / D))
            dx_ref[...] = dx
            dw_ref[...] += dw

    return pl.pallas_call(
        body, name=name, grid=(n_t,),
        in_specs=[pl.BlockSpec((TILE, D), lambda i: (i, 0)), pl.BlockSpec((TILE, D), lambda i: (jnp.maximum(i - 1, 0), 0)),
                  pl.BlockSpec((1, D), lambda i: (0, 0))],
        out_specs=[pl.BlockSpec((8, 128), lambda i: (0, 0)), pl.BlockSpec((TILE, D), lambda i: (i, 0)),
                   pl.BlockSpec((1, D), lambda i: (0, 0))],
        out_shape=[jax.ShapeDtypeStruct((8, 128), F32), jax.ShapeDtypeStruct((t, D), F32),
                   jax.ShapeDtypeStruct((1, D), F32)],
        compiler_params=_cparams(1),
    )(xf, target, fnw)


def exchange(x, name, scatter):
    blk = x.shape[1:] if scatter else x.shape

    def body(x_ref, o_ref, ssem, rsem, lsem):
        xi, yi, ci = lax.axis_index("x"), lax.axis_index("y"), lax.axis_index("c")
        me = 4 * xi + 2 * yi + ci

        def peer(k):
            px = (1 - xi) if (k >> 2) & 1 else xi
            py = (1 - yi) if (k >> 1) & 1 else yi
            pc = (1 - ci) if k & 1 else ci
            return (px, py, pc), 4 * px + 2 * py + pc

        def copy(k):
            dev, pid = peer(k)
            return pltpu.make_async_remote_copy(
                src_ref=x_ref.at[pid] if scatter else x_ref, dst_ref=o_ref.at[me],
                send_sem=ssem.at[k - 1], recv_sem=rsem.at[k - 1], device_id=dev, device_id_type=pl.DeviceIdType.MESH)

        def landing(k):
            dev, pid = peer(k)
            return pltpu.make_async_remote_copy(
                src_ref=x_ref.at[pid] if scatter else x_ref, dst_ref=o_ref.at[pid],
                send_sem=ssem.at[k - 1], recv_sem=rsem.at[k - 1], device_id=dev, device_id_type=pl.DeviceIdType.MESH)

        own = pltpu.make_async_copy(x_ref.at[me] if scatter else x_ref, o_ref.at[me], lsem)
        own.start()
        sends = [copy(k) for k in range(1, N_DEV)]
        for cp in sends:
            cp.start()
        for k in range(1, N_DEV):
            landing(k).wait_recv()
        for cp in sends:
            cp.wait_send()
        own.wait()

    return pl.pallas_call(
        body, name=name,
        in_specs=[pl.BlockSpec(memory_space=pl.ANY)], out_specs=pl.BlockSpec(memory_space=pl.ANY),
        out_shape=jax.ShapeDtypeStruct((N_DEV,) + tuple(blk), x.dtype),
        scratch_shapes=[pltpu.SemaphoreType.DMA((N_DEV - 1,)), pltpu.SemaphoreType.DMA((N_DEV - 1,)),
                        pltpu.SemaphoreType.DMA(())],
        compiler_params=pltpu.CompilerParams(has_side_effects=True),
    )(x)


def mod_forward(crows, w_mod, b_shard, name):
    cols = w_mod.shape[2]

    def body(c_ref, w_ref, b_ref, o_ref):
        o_ref[0] = _dot(_silu(c_ref[...]).astype(BF), w_ref[0].astype(BF)) + b_ref[0]

    return pl.pallas_call(
        body, name=name, grid=(2,),
        in_specs=[pl.BlockSpec((16, D), lambda l: (0, 0)), pl.BlockSpec((1, D, cols), lambda l: (l, 0, 0)),
                  pl.BlockSpec((1, 1, cols), lambda l: (l, 0, 0))],
        out_specs=pl.BlockSpec((1, 16, cols), lambda l: (l, 0, 0)),
        out_shape=jax.ShapeDtypeStruct((2, 16, cols), F32), compiler_params=_cparams(1),
    )(crows, w_mod, b_shard)


def mod_backward(crows, w_mod, d_own, d_ctx, name):
    cols = w_mod.shape[2]

    def body(c_ref, w_ref, do_ref, dc_ref, gw_ref, gs_ref):
        dc = dc_ref[0]
        dsum = dc[0:1]
        for s in range(1, N_DEV):
            dsum = dsum + dc[s:s + 1]
        row = lax.broadcasted_iota(jnp.int32, (8, cols), 0)
        d16 = jnp.concatenate([do_ref[0], jnp.where(row == 0, jnp.broadcast_to(dsum, (8, cols)), 0.0)], axis=0)
        gw_ref[0] = _dot_tn(_silu(c_ref[...]).astype(BF), d16.astype(BF))
        gs_ref[0] = _dot_nt(jnp.broadcast_to(dsum, (8, cols)).astype(BF), w_ref[0].astype(BF))

    return pl.pallas_call(
        body, name=name, grid=(2,),
        in_specs=[pl.BlockSpec((16, D), lambda l: (0, 0)), pl.BlockSpec((1, D, cols), lambda l: (l, 0, 0)),
                  pl.BlockSpec((1, 8, cols), lambda l: (l, 0, 0)), pl.BlockSpec((1, 8, cols), lambda l: (l, 0, 0))],
        out_specs=[pl.BlockSpec((1, D, cols), lambda l: (l, 0, 0)), pl.BlockSpec((1, 8, D), lambda l: (l, 0, 0))],
        out_shape=[jax.ShapeDtypeStruct((2, D, cols), F32), jax.ShapeDtypeStruct((2, 8, D), F32)],
        compiler_params=_cparams(1),
    )(crows, w_mod, d_own, d_ctx)


def silu_grad_scale(c_ctx, ds, name):
    def body(c_ref, ds_ref, o_ref):
        cc = c_ref[...]
        sg = jax.nn.sigmoid(cc)
        o_ref[...] = (ds_ref[0][0:1] + ds_ref[1][0:1]) * (sg * (1.0 + cc * (1.0 - sg)))

    return pl.pallas_call(body, name=name, out_shape=jax.ShapeDtypeStruct((1, D), F32))(c_ctx, ds)


def adamw(parts, w, m, v, name, block_rows):
    n_parts, rows, _ = parts.shape

    def body(p_ref, w_ref, m_ref, v_ref, g_ref, d_ref, nm_ref, nv_ref):
        g = p_ref[0]
        for s in range(1, n_parts):
            g = g + p_ref[s]
        mm = ADAM_B1 * m_ref[...] + (1.0 - ADAM_B1) * g
        vv = ADAM_B2 * v_ref[...] + (1.0 - ADAM_B2) * (g * g)
        m_hat = mm / (1.0 - ADAM_B1 ** ADAM_STEP)
        v_hat = vv / (1.0 - ADAM_B2 ** ADAM_STEP)
        g_ref[...] = g
        d_ref[...] = -ADAM_LR * (m_hat / (jnp.sqrt(v_hat) + ADAM_EPS) + ADAM_WD * w_ref[...])
        nm_ref[...] = mm
        nv_ref[...] = vv

    spec = pl.BlockSpec((block_rows, 128), lambda i: (i, 0))
    return pl.pallas_call(
        body, name=name, grid=(rows // block_rows,),
        in_specs=[pl.BlockSpec((n_parts, block_rows, 128), lambda i: (0, i, 0)), spec, spec, spec],
        out_specs=[spec] * 4, out_shape=[jax.ShapeDtypeStruct((rows, 128), F32)] * 4,
        compiler_params=_cparams(1),
    )(parts, w, m, v)


def _pad_cols(w, segs, total):
    parts, pos = [], 0
    for dst, src, wd in segs:
        if dst > pos:
            parts.append(jnp.zeros(w.shape[:-1] + (dst - pos,), w.dtype))
        parts.append(w[..., src:src + wd])
        pos = dst + wd
    if pos < total:
        parts.append(jnp.zeros(w.shape[:-1] + (total - pos,), w.dtype))
    return jnp.concatenate(parts, axis=-1)


def _unpad_cols(g, segs):
    return jnp.concatenate([g[..., dst:dst + wd] for dst, _, wd in segs], axis=-1)


def _flat128(a, rows_multiple=8):
    f = a.reshape(-1)
    n = f.shape[0]
    per = 128 * rows_multiple
    pad = (-n) % per
    if pad:
        f = jnp.concatenate([f, jnp.zeros((pad,), f.dtype)])
    return f.reshape(-1, 128)


def _rope_tables(n_lat):
    rows = n_lat // GRID_W
    row = jnp.repeat(jnp.arange(rows), GRID_W).astype(F32)
    col = jnp.tile(jnp.arange(GRID_W), rows).astype(F32)
    freq = ROPE_BASE ** (-jnp.arange(16, dtype=F32) * 2.0 / 32)
    ar, ac = row[:, None] * freq[None, :], col[:, None] * freq[None, :]
    z = jnp.zeros((n_lat, 16), F32)
    cos = jnp.concatenate([jnp.cos(ar), jnp.cos(ar), jnp.cos(ac), jnp.cos(ac), jnp.ones((n_lat, 64), F32)], axis=1)
    sa = jnp.concatenate([-jnp.sin(ar), z, -jnp.sin(ac), z, jnp.zeros((n_lat, 64), F32)], axis=1)
    sb = jnp.concatenate([z, jnp.sin(ar), z, jnp.sin(ac), jnp.zeros((n_lat, 64), F32)], axis=1)
    ident = lambda fill: jnp.full((TILE, 128), fill, F32)
    return (jnp.concatenate([ident(1.0), cos]), jnp.concatenate([ident(0.0), sa]), jnp.concatenate([ident(0.0), sb]))


def _big_shard_shapes(w):
    return {k: w[k].shape for k in BIG_NAMES}


def _gathered_to_full(g, name):
    if name in ("w_out", "w_ff2"):
        return jnp.transpose(g, (1, 0, 2, 3)).reshape(2, -1, g.shape[-1])
    return jnp.transpose(g, (1, 2, 0, 3)).reshape(2, g.shape[2], -1)


def _full_to_chunks(gw, name):
    if name in ("w_out", "w_ff2"):
        return jnp.transpose(gw.reshape(2, N_DEV, -1, gw.shape[-1]), (1, 0, 2, 3))
    return jnp.transpose(gw.reshape(2, gw.shape[1], N_DEV, -1), (2, 0, 1, 3))


def kernel(x, c, ctx, c_ctx, w_mod, b_mod, norm1_w, w_in, w_out, sgu_norm_w, sgu_norm_b, sgu_w, sgu_b, gla_wg_fwd, gla_bg_fwd, gla_wg_bwd, gla_bg_bwd, gla_norm_w, mla_q_norm_w, mla_w_uq, mla_kv_norm_w, mla_w_ukv, norm2_w, w_ff1, w_ff2, final_norm_w, loss_target, m_c_ctx, m_w_mod, m_b_mod, m_norm1_w, m_w_in, m_w_out, m_sgu_norm_w, m_sgu_norm_b, m_sgu_w, m_sgu_b, m_gla_wg_fwd, m_gla_bg_fwd, m_gla_wg_bwd, m_gla_bg_bwd, m_gla_norm_w, m_mla_q_norm_w, m_mla_w_uq, m_mla_kv_norm_w, m_mla_w_ukv, m_norm2_w, m_w_ff1, m_w_ff2, m_final_norm_w, v_c_ctx, v_w_mod, v_b_mod, v_norm1_w, v_w_in, v_w_out, v_sgu_norm_w, v_sgu_norm_b, v_sgu_w, v_sgu_b, v_gla_wg_fwd, v_gla_bg_fwd, v_gla_wg_bwd, v_gla_bg_bwd, v_gla_norm_w, v_mla_q_norm_w, v_mla_w_uq, v_mla_kv_norm_w, v_mla_w_ukv, v_norm2_w, v_w_ff1, v_w_ff2, v_final_norm_w):
    W = dict(c_ctx=c_ctx, w_mod=w_mod, b_mod=b_mod, norm1_w=norm1_w, w_in=w_in, w_out=w_out, sgu_norm_w=sgu_norm_w,
             sgu_norm_b=sgu_norm_b, sgu_w=sgu_w, sgu_b=sgu_b, gla_wg_fwd=gla_wg_fwd, gla_bg_fwd=gla_bg_fwd,
             gla_wg_bwd=gla_wg_bwd, gla_bg_bwd=gla_bg_bwd, gla_norm_w=gla_norm_w, mla_q_norm_w=mla_q_norm_w,
             mla_w_uq=mla_w_uq, mla_kv_norm_w=mla_kv_norm_w, mla_w_ukv=mla_w_ukv, norm2_w=norm2_w, w_ff1=w_ff1,
             w_ff2=w_ff2, final_norm_w=final_norm_w)
    M = dict(c_ctx=m_c_ctx, w_mod=m_w_mod, b_mod=m_b_mod, norm1_w=m_norm1_w, w_in=m_w_in, w_out=m_w_out,
             sgu_norm_w=m_sgu_norm_w, sgu_norm_b=m_sgu_norm_b, sgu_w=m_sgu_w, sgu_b=m_sgu_b, gla_wg_fwd=m_gla_wg_fwd,
             gla_bg_fwd=m_gla_bg_fwd, gla_wg_bwd=m_gla_wg_bwd, gla_bg_bwd=m_gla_bg_bwd, gla_norm_w=m_gla_norm_w,
             mla_q_norm_w=m_mla_q_norm_w, mla_w_uq=m_mla_w_uq, mla_kv_norm_w=m_mla_kv_norm_w, mla_w_ukv=m_mla_w_ukv,
             norm2_w=m_norm2_w, w_ff1=m_w_ff1, w_ff2=m_w_ff2, final_norm_w=m_final_norm_w)
    V = dict(c_ctx=v_c_ctx, w_mod=v_w_mod, b_mod=v_b_mod, norm1_w=v_norm1_w, w_in=v_w_in, w_out=v_w_out,
             sgu_norm_w=v_sgu_norm_w, sgu_norm_b=v_sgu_norm_b, sgu_w=v_sgu_w, sgu_b=v_sgu_b, gla_wg_fwd=v_gla_wg_fwd,
             gla_bg_fwd=v_gla_bg_fwd, gla_wg_bwd=v_gla_wg_bwd, gla_bg_bwd=v_gla_bg_bwd, gla_norm_w=v_gla_norm_w,
             mla_q_norm_w=v_mla_q_norm_w, mla_w_uq=v_mla_w_uq, mla_kv_norm_w=v_mla_kv_norm_w, mla_w_ukv=v_mla_w_ukv,
             norm2_w=v_norm2_w, w_ff1=v_w_ff1, w_ff2=v_w_ff2, final_norm_w=v_final_norm_w)

    n_lat = x.shape[1]
    assert ctx.shape[1] == TILE and n_lat % TILE == 0 and x.shape[2] == D
    t_all = TILE + n_lat
    n_t = t_all // TILE
    me = 4 * lax.axis_index("x") + 2 * lax.axis_index("y") + lax.axis_index("c")
    mod_cols = w_mod.shape[2]

    c_all = exchange(c, "ag_c", scatter=False).reshape(N_DEV, D)
    crows = jnp.concatenate([c_all, c_ctx[None, :], jnp.zeros((7, D), F32)], axis=0)
    b_shard = lax.dynamic_slice_in_dim(b_mod, me * mod_cols, mod_cols, axis=1)[:, None, :]
    mod_sh = mod_forward(crows, w_mod, b_shard, "mod_fwd")
    mod_g = exchange(mod_sh.reshape(32, mod_cols), "ag_mod", scatter=False)
    mod_full = jnp.transpose(mod_g.reshape(N_DEV, 2, 16, mod_cols), (1, 2, 0, 3)).reshape(2, 16, 6 * D)
    mod_own = lax.dynamic_index_in_dim(mod_full, me, axis=1, keepdims=False)
    mod_ctx = mod_full[:, 8, :]
    pad2 = jnp.zeros((2, D), F32)
    modl = [jnp.stack([jnp.concatenate([mod_ctx[l].reshape(6, D), pad2]),
                       jnp.concatenate([mod_own[l].reshape(6, D), pad2])]) for l in range(2)]

    shard_shapes = _big_shard_shapes(W)
    pack = jnp.concatenate([W[k].astype(BF).reshape(-1) for k in BIG_NAMES]).reshape(-1, 128)
    gathered = exchange(pack, "ag_weights", scatter=False).reshape(N_DEV, -1)
    full, off = {}, 0
    for k in BIG_NAMES:
        n = int(np.prod(shard_shapes[k]))
        full[k] = _gathered_to_full(gathered[:, off:off + n].reshape((N_DEV,) + shard_shapes[k]), k)
        off += n
    w_in_p = _pad_cols(full["w_in"], W_IN_SEGS, P_COLS)
    w_uq_p = _pad_cols(full["mla_w_uq"], W_UQ_SEGS, 1024).astype(F32)
    w_ukv_f = full["mla_w_ukv"].astype(F32)
    wgf_p = jnp.pad(gla_wg_fwd, ((0, 0), (0, 112), (0, 0)))
    wgb_p = jnp.pad(gla_wg_bwd, ((0, 0), (0, 112), (0, 0)))
    sgu_bx = jnp.repeat(jnp.transpose(sgu_b, (0, 2, 1)), 64, axis=2)
    gnw_t = jnp.tile(gla_norm_w, (1, HEADS))
    rc, rsa, rsb = _rope_tables(n_lat)

    xin = jnp.concatenate([ctx[0], x[0]], axis=0)
    row = lambda a: a[None, :]

    def pre_ins(l, xl):
        return [("x", "tile", True, xl), ("mod", "kind", True, modl[l]), ("n1w", "full", True, row(norm1_w[l])),
                ("w_in", "wfull", False, w_in_p[l]), ("sgu_nw", "full", True, row(sgu_norm_w[l])),
                ("sgu_nb", "full", True, row(sgu_norm_b[l])), ("sgu_w", "full", True, sgu_w[l]),
                ("sgu_bx", "full", True, sgu_bx[l]), ("wgf", "full", True, wgf_p[l]), ("bgf", "full", True, row(gla_bg_fwd[l])),
                ("wgb", "full", True, wgb_p[l]), ("bgb", "full", True, row(gla_bg_bwd[l])),
                ("qnw", "full", True, row(mla_q_norm_w[l])), ("w_uq", "full", True, w_uq_p[l]),
                ("kvnw", "full", True, row(mla_kv_norm_w[l])), ("w_ukv", "full", True, w_ukv_f[l]),
                ("rc", "tile", False, rc), ("rsa", "tile", False, rsa), ("rsb", "tile", False, rsb)]

    pre_outs = [("y_sgu", 256, F32), ("qg", 128, F32), ("kg", 128, F32), ("vg", 256, F32), ("lgf", 128, F32),
                ("lgb", 128, F32), ("gr", 256, F32), ("q_cat", 1024, BF), ("k_cat", 1024, BF), ("v", 512, BF)]

    def out_ins(l, xl, a):
        return [("x", "tile", True, xl), ("mod", "kind", True, modl[l]), ("y_sgu", "tile", True, a["y_sgu"]),
                ("o_f", "tile", True, a["o_f"]), ("o_b", "tile", True, a["o_b"]), ("gr", "tile", True, a["gr"]),
                ("y_mla", "tile", True, a["y_mla"]), ("gnw", "full", True, row(gnw_t[l])),
                ("w_out", "wfull", False, full["w_out"][l])]

    def ffn_ins(l, x1):
        return [("x1", "tile", True, x1), ("mod", "kind", True, modl[l]), ("n2w", "full", True, row(norm2_w[l])),
                ("w_ff1", "wcols", False, full["w_ff1"][l]), ("w_ff2", "wrows", False, full["w_ff2"][l])]

    saved, xl = [], xin
    for l in range(2):
        a = tile_forward(pre_tile, f"pre_fwd{l}", t_all, pre_ins(l, xl), pre_outs)
        a["o_f"], a["sf"] = gla_forward(a["qg"], a["kg"], a["vg"], a["lgf"], False, f"gla_f_fwd{l}")
        a["o_b"], a["sb"] = gla_forward(a["qg"], a["kg"], a["vg"], a["lgb"], True, f"gla_b_fwd{l}")
        a["y_mla"], a["lse"] = mla_forward(a["q_cat"], a["k_cat"], a["v"], f"mla_fwd{l}")
        a["x"] = xl
        a["x1"] = tile_forward(attn_out_tile, f"out_fwd{l}", t_all, out_ins(l, xl, a), [("x1", D, F32)])["x1"]
        xl = tile_forward(ffn_tile, f"ffn_fwd{l}", t_all, ffn_ins(l, a["x1"]), [("x2", D, F32)])["x2"]
        saved.append(a)

    loss_blk, dx, d_fnw = final_loss(xl, loss_target[0], row(final_norm_w), "final_loss")
    loss = lax.psum(loss_blk[0, 0], AXES)

    G = {}
    dmods = []
    for l in (1, 0):
        a = saved[l]
        g3, e3 = tile_backward(ffn_tile, f"ffn_bwd{l}", t_all, ffn_ins(l, a["x1"]), [("x2", dx)],
                               [("zpre", D_FF), ("zf", D)], [("a_ff1", D), ("a_ff2", D_FF)], tile=128)
        gw_ff1 = wgrad(e3["a_ff1"], e3["zpre"], f"wg_ff1_{l}")
        gw_ff2 = wgrad(e3["a_ff2"], e3["zf"], f"wg_ff2_{l}")
        g2, e2 = tile_backward(attn_out_tile, f"out_bwd{l}", t_all, out_ins(l, a["x"], a), [("x1", g3["x1"])],
                               [("zt", D)], [("a_out", D)])
        gw_out = wgrad(e2["a_out"], e2["zt"], f"wg_out_{l}")
        dq_cat, delta = mla_backward_q(a["q_cat"], a["k_cat"], a["v"], a["y_mla"], a["lse"], g2["y_mla"], f"mla_bwd_q{l}")
        dk_cat, dv = mla_backward_kv(a["q_cat"], a["k_cat"], a["v"], a["lse"], delta, g2["y_mla"].astype(BF), f"mla_bwd_kv{l}")
        do = g2["o_f"]
        dqf, dkf, dvf, dgf = gla_backward(a["qg"], a["kg"], a["vg"], a["lgf"], a["sf"], do, False, f"gla_f_bwd{l}")
        dqb, dkb, dvb, dgb = gla_backward(a["qg"], a["kg"], a["vg"], a["lgb"], a["sb"], do, True, f"gla_b_bwd{l}")
        cots = [("y_sgu", g2["y_sgu"]), ("qg", dqf + dqb), ("kg", dkf + dkb), ("vg", dvf + dvb), ("lgf", dgf),
                ("lgb", dgb), ("gr", g2["gr"]), ("q_cat", dq_cat), ("k_cat", dk_cat), ("v", dv)]
        g1, e1 = tile_backward(pre_tile, f"pre_bwd{l}", t_all, pre_ins(l, a["x"]), cots, [("zp", P_COLS)], [("a_in", D)])
        gw_in = _unpad_cols(wgrad(e1["a_in"], e1["zp"], f"wg_in_{l}", bk2=P_COLS), W_IN_SEGS)
        dx = g1["x"] + g2["x"]
        dmods.append(g1["mod"] + g2["mod"] + g3["mod"])
        G[l] = dict(w_in=gw_in, w_out=gw_out, w_ff1=gw_ff1, w_ff2=gw_ff2,
                    mla_w_uq=_unpad_cols(g1["w_uq"], W_UQ_SEGS), mla_w_ukv=g1["w_ukv"],
                    norm1_w=g1["n1w"][0], norm2_w=g3["n2w"][0], sgu_norm_w=g1["sgu_nw"][0], sgu_norm_b=g1["sgu_nb"][0],
                    sgu_w=g1["sgu_w"], sgu_b=jnp.transpose(g1["sgu_bx"].reshape(128, HEADS, 64).sum(-1)),
                    gla_wg_fwd=g1["wgf"][:16], gla_bg_fwd=g1["bgf"][0], gla_wg_bwd=g1["wgb"][:16], gla_bg_bwd=g1["bgb"][0],
                    gla_norm_w=g2["gnw"][0].reshape(HEADS, 64).sum(0), mla_q_norm_w=g1["qnw"][0], mla_kv_norm_w=g1["kvnw"][0])
    dmods = dmods[::-1]
    grad_x = dx[TILE:][None]

    dmod_pack = jnp.stack([jnp.stack([dmods[l][1, :6].reshape(-1), dmods[l][0, :6].reshape(-1)]) for l in range(2)])
    dmod_all = exchange(dmod_pack.reshape(4, 6 * D), "ag_dmod", scatter=False).reshape(N_DEV, 2, 2, 6 * D)
    dsl = lax.dynamic_slice_in_dim(dmod_all, me * mod_cols, mod_cols, axis=3)
    d_own = jnp.transpose(dsl[:, :, 0, :], (1, 0, 2))
    d_ctx = jnp.transpose(dsl[:, :, 1, :], (1, 0, 2))
    g_w_mod, ds_cc = mod_backward(crows, w_mod, d_own, d_ctx, "mod_bwd")
    g_c_ctx_part = silu_grad_scale(c_ctx[None, :], ds_cc, "silu_bwd")[0]
    g_b_mod_part = jnp.stack([dmods[l][1, :6].reshape(-1) + dmods[l][0, :6].reshape(-1) for l in range(2)])

    small_g = dict(c_ctx=g_c_ctx_part, b_mod=g_b_mod_part, final_norm_w=d_fnw[0])
    for k in SMALL_NAMES:
        if k not in small_g:
            small_g[k] = jnp.stack([G[0][k], G[1][k]])
    spack = lambda dct: _flat128(jnp.concatenate([dct[k].reshape(-1) for k in SMALL_NAMES]))
    sparts = exchange(spack(small_g), "ag_small", scatter=False)
    s_rows = sparts.shape[1]
    s_out = adamw(sparts, spack(W), spack(M), spack(V), "adamw_small", s_rows)

    big_chunks = jnp.concatenate(
        [_full_to_chunks(jnp.stack([G[0][k], G[1][k]]), k).reshape(N_DEV, -1) for k in BIG_NAMES], axis=1)
    b_rows = big_chunks.shape[1] // 128
    bparts = exchange(big_chunks.reshape(N_DEV, b_rows, 128), "a2a_grads", scatter=True)
    bpack = lambda dct: jnp.concatenate([dct[k].reshape(-1) for k in BIG_NAMES]).reshape(b_rows, 128)
    blk = b_rows
    for cand in range(2048, 7, -8):
        if b_rows % cand == 0:
            blk = cand
            break
    b_out = adamw(bparts, bpack(W), bpack(M), bpack(V), "adamw_big", blk)

    wm_rows = w_mod.size // 128
    wm = lambda a: a.reshape(wm_rows, 128)
    m_out = adamw(wm(g_w_mod)[None], wm(w_mod), wm(m_w_mod), wm(v_w_mod), "adamw_mod", 1024)

    res = {}
    off = 0
    for k in SMALL_NAMES:
        n = W[k].size
        res[k] = [o.reshape(-1)[off:off + n].reshape(W[k].shape) for o in s_out]
        off += n
    off = 0
    for k in BIG_NAMES:
        n = W[k].size
        res[k] = [o.reshape(-1)[off:off + n].reshape(W[k].shape) for o in b_out]
        off += n
    res["w_mod"] = [o.reshape(w_mod.shape) for o in m_out]
    outs = [loss, grad_x]
    for j in range(4):
        outs += [res[k][j] for k in WEIGHT_ORDER]
    return tuple(outs)
```

```python
import functools

import numpy as np
import jax
import jax.numpy as jnp
from jax import lax
from jax.experimental import pallas as pl
from jax.experimental.pallas import tpu as pltpu

F32 = jnp.float32
BF = jnp.bfloat16

N_DEV = 8
AXES = ("x", "y", "c")
EPS = 1e-6
D = 1024
TILE = 256
GCH = 64
SGU_CHUNK = 128
HEADS = 4
ROPE_BASE = 10000.0
GRID_W = 64
GLA_TAU = 16.0
ATT_SCALE = (128 + 64) ** -0.5
ATT_SCALE_LOG2 = ATT_SCALE * 1.4426950408889634
KV_CH = 512
D_FF = 4096
FF_CH = 1024

ADAM_LR = 0.001
ADAM_B1 = 0.9
ADAM_B2 = 0.999
ADAM_EPS = 1e-08
ADAM_WD = 0.01
ADAM_STEP = 10

VMEM_LIMIT_MB = 56

W_IN_SEGS = ((0, 0, 128), (128, 128, 256), (384, 384, 16), (512, 400, 16), (640, 416, 256), (896, 672, 64),
             (1024, 736, 256), (1280, 992, 256), (1536, 1248, 128), (1664, 1376, 256), (1920, 1632, 256))
P_COLS = 2176
O_GK, O_GV, O_GGF, O_GGB, O_CKV, O_KR, O_SU, O_SV, O_GQ, O_GR, O_DQ = (s[0] for s in W_IN_SEGS)
W_UQ_SEGS = tuple((h * 256, h * 192, 192) for h in range(HEADS))

SMALL_NAMES = ("c_ctx", "b_mod", "norm1_w", "sgu_norm_w", "sgu_norm_b", "sgu_w", "sgu_b", "gla_wg_fwd", "gla_bg_fwd",
               "gla_wg_bwd", "gla_bg_bwd", "gla_norm_w", "mla_q_norm_w", "mla_kv_norm_w", "norm2_w", "final_norm_w")
BIG_NAMES = ("w_in", "w_out", "mla_w_uq", "mla_w_ukv", "w_ff1", "w_ff2")
WEIGHT_ORDER = ("c_ctx", "w_mod", "b_mod", "norm1_w", "w_in", "w_out", "sgu_norm_w", "sgu_norm_b", "sgu_w", "sgu_b",
                "gla_wg_fwd", "gla_bg_fwd", "gla_wg_bwd", "gla_bg_bwd", "gla_norm_w", "mla_q_norm_w", "mla_w_uq",
                "mla_kv_norm_w", "mla_w_ukv", "norm2_w", "w_ff1", "w_ff2", "final_norm_w")


def _cparams(n_axes):
    return pltpu.CompilerParams(dimension_semantics=("arbitrary",) * n_axes,
                                vmem_limit_bytes=VMEM_LIMIT_MB * 1024 * 1024)


def _dot(a, b):
    return jnp.dot(a, b, preferred_element_type=F32)


def _dot_nt(a, b):
    return lax.dot_general(a, b, (((1,), (1,)), ((), ())), preferred_element_type=F32)


def _dot_tn(a, b):
    return lax.dot_general(a, b, (((0,), (0,)), ((), ())), preferred_element_type=F32)


def _nn(a, b):
    return _dot(a.astype(BF), b.astype(BF))


def _nt(a, b):
    return _dot_nt(a.astype(BF), b.astype(BF))


def _tn(a, b):
    return _dot_tn(a.astype(BF), b.astype(BF))


nn_d = jax.custom_vjp(_nn)
nt_d = jax.custom_vjp(_nt)
tn_d = jax.custom_vjp(_tn)
nn_d.defvjp(lambda a, b: (_nn(a, b), (a, b)), lambda r, dy: (_nt(dy, r[1]), _tn(r[0], dy)))
nt_d.defvjp(lambda a, b: (_nt(a, b), (a, b)), lambda r, dy: (_nn(dy, r[1]), _tn(dy, r[0])))
tn_d.defvjp(lambda a, b: (_tn(a, b), (a, b)), lambda r, dy: (_nt(r[1], dy), _nn(r[0], dy)))


def nn_const(w_bf, diff):
    def raw(a):
        return _dot(a.astype(BF), w_bf)

    if not diff:
        return raw
    f = jax.custom_vjp(raw)
    f.defvjp(lambda a: (raw(a), None), lambda _, dy: (_dot_nt(dy.astype(BF), w_bf),))
    return f


def _split3(g):
    hi = g.astype(BF)
    r = g - hi.astype(F32)
    mid = r.astype(BF)
    lo = (r - mid.astype(F32)).astype(BF)
    return hi, mid, lo


def make_cum(tri_bf, tri_t_bf, diff):
    def raw(g, t):
        hi, mid, lo = _split3(g)
        return _dot(t, hi) + _dot(t, mid) + _dot(t, lo)

    def fwd(g):
        return raw(g, tri_bf)

    if not diff:
        return fwd
    cum = jax.custom_vjp(fwd)
    cum.defvjp(lambda g: (fwd(g), None), lambda _, db: (raw(db, tri_t_bf),))
    return cum


def _roll_lanes(x, shift):
    return pltpu.roll(x, shift, 1)


def make_rope(c, sa, sb, diff):
    def raw(x):
        return x * c + _roll_lanes(x, 112) * sa + _roll_lanes(x, 16) * sb

    if not diff:
        return raw
    f = jax.custom_vjp(raw)
    f.defvjp(lambda x: (raw(x), None),
             lambda _, dy: (dy * c + _roll_lanes(dy * sa, 16) + _roll_lanes(dy * sb, 112),))
    return f


def _ops(diff):
    return (nn_d, nt_d, tn_d) if diff else (_nn, _nt, _tn)


def _rms(x, w):
    return x * lax.rsqrt(jnp.mean(x * x, axis=-1, keepdims=True) + EPS) * w


def _gelu(x):
    return 0.5 * x * (1.0 + jnp.tanh(0.7978845608028654 * (x + 0.044715 * (x * x * x))))


def _silu(x):
    return x * jax.nn.sigmoid(x)


def _log_sigmoid(z):
    return jnp.minimum(z, 0.0) - jnp.log(1.0 + jnp.exp(-jnp.abs(z)))


def _lane_group_mask(width, group, h):
    lane = lax.broadcasted_iota(jnp.int32, (1, width), 1)
    return ((lane >= h * group) & (lane < (h + 1) * group)).astype(F32)


def pre_tile(d, c, z):
    nn, _, _ = _ops(z is not None)
    rope = make_rope(c["rc"], c["rsa"], c["rsb"], z is not None)
    mod = d["mod"]
    h = _rms(d["x"], d["n1w"]) * (1.0 + mod[1:2]) + mod[0:1]
    p = nn_const(c["w_in"], z is not None)(h)
    if z is not None:
        p = p + z["zp"]
    gk, gv = p[:, O_GK:O_GK + 128], p[:, O_GV:O_GV + 256]
    ggf, ggb = p[:, O_GGF:O_GGF + 128], p[:, O_GGB:O_GGB + 128]
    ckv, kr = p[:, O_CKV:O_CKV + 256], p[:, O_KR:O_KR + 128]
    su, sv = p[:, O_SU:O_SU + 256], p[:, O_SV:O_SV + 256]
    gq, gr, dq = p[:, O_GQ:O_GQ + 128], p[:, O_GR:O_GR + 256], p[:, O_DQ:O_DQ + 256]

    u = _gelu(su)
    gv_ = _gelu(sv)
    mu = jnp.mean(gv_, axis=-1, keepdims=True)
    cen = gv_ - mu
    vn = cen * lax.rsqrt(jnp.mean(cen * cen, axis=-1, keepdims=True) + EPS) * d["sgu_nw"] + d["sgu_nb"]
    hm = [_lane_group_mask(256, 64, hh) for hh in range(HEADS)]
    rows = []
    for ci in range(vn.shape[0] // SGU_CHUNK):
        vc = vn[ci * SGU_CHUNK:(ci + 1) * SGU_CHUNK]
        s = d["sgu_bx"]
        for hh in range(HEADS):
            s = s + hm[hh] * nn(d["sgu_w"][hh], vc)
        rows.append(s)
    y_sgu = u * jnp.concatenate(rows, axis=0)

    qg = gq * (32 ** -0.5)
    lgf = _log_sigmoid(nn(ggf, d["wgf"]) + d["bgf"]) * (1.0 / GLA_TAU)
    lgb = _log_sigmoid(nn(ggb, d["wgb"]) + d["bgb"]) * (1.0 / GLA_TAU)

    kv = nn(_rms(ckv, d["kvnw"]), d["w_ukv"])
    kr_r = rope(kr)
    q = nn(_rms(dq, d["qnw"]), d["w_uq"])
    qs, ks, vs = [], [], []
    for hh in range(HEADS):
        qs += [q[:, hh * 256:hh * 256 + 128], rope(q[:, hh * 256 + 128:(hh + 1) * 256])]
        ks += [kv[:, hh * 256:hh * 256 + 128], kr_r]
        vs += [kv[:, hh * 256 + 128:(hh + 1) * 256]]
    outs = dict(y_sgu=y_sgu, qg=qg, kg=gk, vg=gv, lgf=lgf, lgb=lgb, gr=gr,
                q_cat=jnp.concatenate(qs, axis=-1), k_cat=jnp.concatenate(ks, axis=-1), v=jnp.concatenate(vs, axis=-1))
    return outs, dict(a_in=h)


def attn_out_tile(d, c, z):
    mod = d["mod"]
    o = d["o_f"] + d["o_b"]
    ms = jnp.zeros_like(o)
    for hh in range(HEADS):
        m_h = _lane_group_mask(256, 64, hh)
        ms = ms + m_h * (jnp.sum(o * o * m_h, axis=-1, keepdims=True) * (1.0 / 64))
    yg = o * lax.rsqrt(ms + EPS) * d["gnw"] * _silu(d["gr"])
    y = jnp.concatenate([d["y_sgu"], yg, d["y_mla"]], axis=-1)
    t = nn_const(c["w_out"], z is not None)(y)
    if z is not None:
        t = t + z["zt"]
    return dict(x1=d["x"] + mod[2:3] * t), dict(a_out=y)


def ffn_tile(d, c, z):
    mod = d["mod"]
    h2 = _rms(d["x1"], d["n2w"]) * (1.0 + mod[4:5]) + mod[3:4]
    f = None
    a2s = []
    for j in range(D_FF // FF_CH):
        pre = nn_const(c["w_ff1"][j], z is not None)(h2)
        if z is not None:
            pre = pre + z["zpre"][:, j * FF_CH:(j + 1) * FF_CH]
        a = jnp.maximum(pre, 0.0)
        a2 = a * a
        a2s.append(a2)
        fj = nn_const(c["w_ff2"][j], z is not None)(a2)
        f = fj if f is None else f + fj
    if z is not None:
        f = f + z["zf"]
    return dict(x2=d["x1"] + mod[5:6] * f), dict(a_ff1=h2, a_ff2=jnp.concatenate(a2s, axis=-1))


def _in_spec(kind, arr, tile):
    if kind == "tile":
        return pl.BlockSpec((tile, arr.shape[1]), lambda i: (i, 0))
    if kind == "kind":
        return pl.BlockSpec((1,) + arr.shape[1:], lambda i: (jnp.where(i < TILE // tile, 0, 1), 0, 0))
    nd = arr.ndim
    if kind in ("wfull", "wcols", "wrows"):
        return pl.BlockSpec(arr.shape, lambda i: (0,) * nd, pipeline_mode=pl.Buffered(1))
    return pl.BlockSpec(arr.shape, lambda i: (0,) * nd)


def _load(kind, ref):
    if kind == "kind":
        return ref[0]
    if kind == "wcols":
        return [ref[:, j * FF_CH:(j + 1) * FF_CH] for j in range(ref.shape[1] // FF_CH)]
    if kind == "wrows":
        return [ref[j * FF_CH:(j + 1) * FF_CH, :] for j in range(ref.shape[0] // FF_CH)]
    return ref[...]


def tile_forward(fn, name, t_all, ins, out_defs, tile=TILE):
    keys = [k for k, _, _, _ in ins]
    kinds = [kd for _, kd, _, _ in ins]
    diffs = [df for _, _, df, _ in ins]
    arrs = [a for _, _, _, a in ins]
    n_in = len(ins)

    def body(*refs):
        vals = [_load(kinds[j], refs[j]) for j in range(n_in)]
        d = {keys[j]: vals[j] for j in range(n_in) if diffs[j]}
        c = {keys[j]: vals[j] for j in range(n_in) if not diffs[j]}
        outs, _ = fn(d, c, None)
        for j, (k, _, dt) in enumerate(out_defs):
            refs[n_in + j][...] = outs[k].astype(dt)

    res = pl.pallas_call(
        body, name=name, grid=(t_all // tile,),
        in_specs=[_in_spec(kinds[j], arrs[j], tile) for j in range(n_in)],
        out_specs=[pl.BlockSpec((tile, w), lambda i: (i, 0)) for _, w, _ in out_defs],
        out_shape=[jax.ShapeDtypeStruct((t_all, w), dt) for _, w, dt in out_defs],
        compiler_params=_cparams(1),
    )(*arrs)
    return {k: r for (k, _, _), r in zip(out_defs, res)}


def tile_backward(fn, name, t_all, ins, cots, z_defs, aux_defs, tile=TILE):
    keys = [k for k, _, _, _ in ins]
    kinds = [kd for _, kd, _, _ in ins]
    diffs = [df for _, _, df, _ in ins]
    arrs = [a for _, _, _, a in ins]
    n_in, n_cot = len(ins), len(cots)
    dkeys = [j for j in range(n_in) if diffs[j]]
    ctx_tiles = TILE // tile

    def body(*refs):
        i = pl.program_id(0)
        vals = [_load(kinds[j], refs[j]) for j in range(n_in)]
        d = {keys[j]: vals[j] for j in dkeys}
        c = {keys[j]: vals[j] for j in range(n_in) if not diffs[j]}
        zs = {k: jnp.zeros((tile, w), F32) for k, w in z_defs}
        outs, vjp_fn, aux = jax.vjp(lambda dd, zz: fn(dd, c, zz), d, zs, has_aux=True)
        ct = {k: refs[n_in + j][...] for j, (k, _) in enumerate(cots)}
        dd, dz = vjp_fn({k: ct[k].astype(outs[k].dtype) for k in outs})
        base = n_in + n_cot
        for n, j in enumerate(dkeys):
            ref, g = refs[base + n], dd[keys[j]]
            if kinds[j] == "tile":
                ref[...] = g
            else:
                first = ((i == 0) | (i == ctx_tiles)) if kinds[j] == "kind" else (i == 0)
                gv = g[None] if kinds[j] == "kind" else g

                @pl.when(first)
                def _(ref=ref, gv=gv):
                    ref[...] = gv

                @pl.when(jnp.logical_not(first))
                def _(ref=ref, gv=gv):
                    ref[...] += gv
        base += len(dkeys)
        for n, (k, _) in enumerate(z_defs):
            refs[base + n][...] = dz[k].astype(BF)
        base += len(z_defs)
        for n, (k, _) in enumerate(aux_defs):
            refs[base + n][...] = aux[k].astype(BF)

    out_specs, out_shape = [], []
    for j in dkeys:
        out_specs.append(_in_spec(kinds[j], arrs[j], tile))
        out_shape.append(jax.ShapeDtypeStruct(arrs[j].shape, F32))
    for _, w in list(z_defs) + list(aux_defs):
        out_specs.append(pl.BlockSpec((tile, w), lambda i: (i, 0)))
        out_shape.append(jax.ShapeDtypeStruct((t_all, w), BF))
    res = pl.pallas_call(
        body, name=name, grid=(t_all // tile,),
        in_specs=[_in_spec(kinds[j], arrs[j], tile) for j in range(n_in)]
        + [pl.BlockSpec((tile, a.shape[1]), lambda i: (i, 0)) for _, a in cots],
        out_specs=out_specs, out_shape=out_shape, compiler_params=_cparams(1),
    )(*arrs, *[a for _, a in cots])
    grads = {keys[j]: res[n] for n, j in enumerate(dkeys)}
    extra = {k: res[len(dkeys) + n] for n, (k, _) in enumerate(list(z_defs) + list(aux_defs))}
    return grads, extra


def wgrad(a, b, name, bk1=1024, bk2=1024):
    t, k1 = a.shape
    k2 = b.shape[1]
    bk1, bk2 = min(bk1, k1), min(bk2, k2)
    nt_ = t // TILE

    def body(a_ref, b_ref, o_ref, acc):
        s = pl.program_id(2)

        @pl.when(s == 0)
        def _():
            acc[...] = jnp.zeros_like(acc)

        acc[...] += _dot_tn(a_ref[...], b_ref[...])

        @pl.when(s == nt_ - 1)
        def _():
            o_ref[...] = acc[...]

    return pl.pallas_call(
        body, name=name, grid=(k1 // bk1, k2 // bk2, nt_),
        in_specs=[pl.BlockSpec((TILE, bk1), lambda i, j, s: (s, i)), pl.BlockSpec((TILE, bk2), lambda i, j, s: (s, j))],
        out_specs=pl.BlockSpec((bk1, bk2), lambda i, j, s: (i, j)),
        out_shape=jax.ShapeDtypeStruct((k1, k2), F32),
        scratch_shapes=[pltpu.VMEM((bk1, bk2), F32)],
        compiler_params=_cparams(3),
    )(a, b)


def _gla_consts(reverse, diff):
    r = lax.broadcasted_iota(jnp.int32, (GCH, GCH), 0)
    cc = lax.broadcasted_iota(jnp.int32, (GCH, GCH), 1)
    low = (r >= cc)
    tri = (jnp.logical_not(low) | (r == cc)) if reverse else low
    tri_f = tri.astype(F32)
    tri_t = (cc >= r) if not reverse else (cc <= r)
    hmk = [_lane_group_mask(128, 32, h) for h in range(HEADS)]
    hmv = [_lane_group_mask(256, 64, h) for h in range(HEADS)]
    e = lax.broadcasted_iota(jnp.int32, (256, 128), 0) // 64
    dk = lax.broadcasted_iota(jnp.int32, (256, 128), 1) // 32
    return dict(cum=make_cum(tri_f.astype(BF), tri_t.astype(F32).astype(BF), diff), ops=_ops(diff),
                tri4=jnp.concatenate([tri_f] * HEADS, axis=0), hmk=hmk, hmv=hmv, bd=(e == dk).astype(F32))


def gla_chunk(st, q, k, v, g, cs):
    nn, nt, tn = cs["ops"]
    b = cs["cum"](g)
    bl = jnp.sum(g, axis=0, keepdims=True)
    qe = q * jnp.exp(b)
    ke = k * jnp.exp(-b)
    qstack = jnp.concatenate([qe * cs["hmk"][h] for h in range(HEADS)], axis=0)
    att = nt(qstack, ke) * cs["tri4"]
    ofull = nn(att, v)
    o = nt(qe, st)
    for h in range(HEADS):
        o = o + ofull[h * GCH:(h + 1) * GCH] * cs["hmv"][h]
    kd = k * jnp.exp(bl - b)
    st_new = st * jnp.exp(bl) + tn(v, kd) * cs["bd"]
    return st_new, o


def _gla_chunk_index(s, n_ch, reverse):
    ctx_ch = TILE // GCH
    if not reverse:
        return s
    return jnp.where(s < ctx_ch, ctx_ch - 1 - s, n_ch - 1 + ctx_ch - s)


def gla_forward(q, k, v, g, reverse, name):
    t = q.shape[0]
    n_ch = t // GCH

    def body(q_ref, k_ref, v_ref, g_ref, o_ref, sst_ref, st):
        s = pl.program_id(0)

        @pl.when(s == 0)
        def _():
            st[...] = jnp.zeros_like(st)

        cur = st[...]
        sst_ref[0] = cur
        st_new, o = gla_chunk(cur, q_ref[...], k_ref[...], v_ref[...], g_ref[...], _gla_consts(reverse, False))
        o_ref[...] = o
        st[...] = st_new

    def im(s):
        return (_gla_chunk_index(s, n_ch, reverse), 0)

    return pl.pallas_call(
        body, name=name, grid=(n_ch,),
        in_specs=[pl.BlockSpec((GCH, 128), im), pl.BlockSpec((GCH, 128), im), pl.BlockSpec((GCH, 256), im),
                  pl.BlockSpec((GCH, 128), im)],
        out_specs=[pl.BlockSpec((GCH, 256), im),
                   pl.BlockSpec((1, 256, 128), lambda s: (_gla_chunk_index(s, n_ch, reverse), 0, 0))],
        out_shape=[jax.ShapeDtypeStruct((t, 256), F32), jax.ShapeDtypeStruct((n_ch, 256, 128), F32)],
        scratch_shapes=[pltpu.VMEM((256, 128), F32)],
        compiler_params=_cparams(1),
    )(q, k, v, g)


def gla_backward(q, k, v, g, sst, do, reverse, name):
    t = q.shape[0]
    n_ch = t // GCH

    def body(q_ref, k_ref, v_ref, g_ref, sst_ref, do_ref, dq_ref, dk_ref, dv_ref, dg_ref, dst):
        r = pl.program_id(0)

        @pl.when(r == 0)
        def _():
            dst[...] = jnp.zeros_like(dst)

        cs = _gla_consts(reverse, True)
        _, vjp_fn = jax.vjp(lambda a, b, c_, d_, e_: gla_chunk(a, b, c_, d_, e_, cs),
                            sst_ref[0], q_ref[...], k_ref[...], v_ref[...], g_ref[...])
        dstp, dq, dk, dv, dg = vjp_fn((dst[...], do_ref[...]))
        dq_ref[...] = dq
        dk_ref[...] = dk
        dv_ref[...] = dv
        dg_ref[...] = dg
        dst[...] = dstp

    def im(r):
        return (_gla_chunk_index(n_ch - 1 - r, n_ch, reverse), 0)

    return pl.pallas_call(
        body, name=name, grid=(n_ch,),
        in_specs=[pl.BlockSpec((GCH, 128), im), pl.BlockSpec((GCH, 128), im), pl.BlockSpec((GCH, 256), im),
                  pl.BlockSpec((GCH, 128), im),
                  pl.BlockSpec((1, 256, 128), lambda r: (_gla_chunk_index(n_ch - 1 - r, n_ch, reverse), 0, 0)),
                  pl.BlockSpec((GCH, 256), im)],
        out_specs=[pl.BlockSpec((GCH, 128), im), pl.BlockSpec((GCH, 128), im), pl.BlockSpec((GCH, 256), im),
                   pl.BlockSpec((GCH, 128), im)],
        out_shape=[jax.ShapeDtypeStruct((t, 128), F32), jax.ShapeDtypeStruct((t, 128), F32),
                   jax.ShapeDtypeStruct((t, 256), F32), jax.ShapeDtypeStruct((t, 128), F32)],
        scratch_shapes=[pltpu.VMEM((256, 128), F32)],
        compiler_params=_cparams(1),
    )(q, k, v, g, sst, do)


def _resident(hbm_ref, vmem_ref, sem):
    cp = pltpu.make_async_copy(hbm_ref, vmem_ref, sem)
    cp.start()
    cp.wait()


def mla_forward(q_cat, k_cat, v, name):
    t = q_cat.shape[0]
    n_t = t // TILE

    n_main = (t - TILE) // KV_CH

    def body(q_ref, k_hbm, v_hbm, o_ref, lse_ref, k_s, v_s, m_s, l_s, acc_s, sem):
        i = pl.program_id(0)

        @pl.when(i == 0)
        def _():
            _resident(k_hbm, k_s, sem.at[0])
            _resident(v_hbm, v_s, sem.at[1])

        m_s[...] = jnp.full(m_s.shape, -1e30, F32)
        l_s[...] = jnp.zeros_like(l_s)
        acc_s[...] = jnp.zeros_like(acc_s)

        def chunk(r0, size):
            for h in range(HEADS):
                kh = k_s[pl.ds(r0, size), h * 256:(h + 1) * 256]
                vh = v_s[pl.ds(r0, size), h * 128:(h + 1) * 128]
                s = _dot_nt(q_ref[:, h * 256:(h + 1) * 256], kh) * ATT_SCALE_LOG2
                m_prev = m_s[h]
                m_next = jnp.maximum(m_prev, jnp.max(s, axis=-1, keepdims=True))
                p = jnp.exp2(s - jnp.tile(m_next, (1, size // 128)))
                alpha = jnp.exp2(m_prev - m_next)
                l_s[h] = alpha * l_s[h] + jnp.sum(p, axis=-1, keepdims=True)
                acc_s[h] = alpha * acc_s[h] + _dot(p.astype(BF), vh)
                m_s[h] = m_next

        chunk(0, TILE)

        @pl.when(i >= 1)
        def _():
            def step(c, carry):
                chunk(pl.multiple_of(TILE + c * KV_CH, TILE), KV_CH)
                return carry

            lax.fori_loop(0, n_main, step, 0)

        for h in range(HEADS):
            o_ref[:, h * 128:(h + 1) * 128] = acc_s[h] / l_s[h]
            lse_ref[:, h * 128:(h + 1) * 128] = m_s[h] + jnp.log2(l_s[h])

    return pl.pallas_call(
        body, name=name, grid=(n_t,),
        in_specs=[pl.BlockSpec((TILE, 1024), lambda i: (i, 0)), pl.BlockSpec(memory_space=pl.ANY),
                  pl.BlockSpec(memory_space=pl.ANY)],
        out_specs=[pl.BlockSpec((TILE, 512), lambda i: (i, 0)), pl.BlockSpec((TILE, 512), lambda i: (i, 0))],
        out_shape=[jax.ShapeDtypeStruct((t, 512), F32), jax.ShapeDtypeStruct((t, 512), F32)],
        scratch_shapes=[pltpu.VMEM((t, 1024), BF), pltpu.VMEM((t, 512), BF), pltpu.VMEM((HEADS, TILE, 128), F32),
                        pltpu.VMEM((HEADS, TILE, 128), F32), pltpu.VMEM((HEADS, TILE, 128), F32),
                        pltpu.SemaphoreType.DMA((2,))],
        compiler_params=_cparams(1),
    )(q_cat, k_cat, v)


def mla_backward_q(q_cat, k_cat, v, o, lse, do, name):
    t = q_cat.shape[0]
    n_t = t // TILE

    n_main = (t - TILE) // KV_CH

    def body(q_ref, k_hbm, v_hbm, o_ref, lse_ref, do_ref, dq_ref, dl_ref, k_s, v_s, dq_s, do_s, sem):
        i = pl.program_id(0)

        @pl.when(i == 0)
        def _():
            _resident(k_hbm, k_s, sem.at[0])
            _resident(v_hbm, v_s, sem.at[1])

        dq_s[...] = jnp.zeros_like(dq_s)
        do_s[...] = do_ref[...].astype(BF)
        for h in range(HEADS):
            hs = slice(h * 128, (h + 1) * 128)
            dl_ref[:, hs] = jnp.broadcast_to(jnp.sum(do_ref[:, hs] * o_ref[:, hs], axis=-1, keepdims=True), (TILE, 128))

        def chunk(r0, size):
            for h in range(HEADS):
                kh = k_s[pl.ds(r0, size), h * 256:(h + 1) * 256]
                vh = v_s[pl.ds(r0, size), h * 128:(h + 1) * 128]
                s = _dot_nt(q_ref[:, h * 256:(h + 1) * 256], kh) * ATT_SCALE_LOG2
                p = jnp.exp2(s - jnp.tile(lse_ref[:, h * 128:(h + 1) * 128], (1, size // 128)))
                dp = _dot_nt(do_s[:, h * 128:(h + 1) * 128], vh)
                ds = p * (dp - jnp.tile(dl_ref[:, h * 128:(h + 1) * 128], (1, size // 128)))
                dq_s[:, h * 256:(h + 1) * 256] += _dot(ds.astype(BF), kh)

        chunk(0, TILE)

        @pl.when(i >= 1)
        def _():
            def step(c, carry):
                chunk(pl.multiple_of(TILE + c * KV_CH, TILE), KV_CH)
                return carry

            lax.fori_loop(0, n_main, step, 0)

        dq_ref[...] = dq_s[...] * ATT_SCALE

    return pl.pallas_call(
        body, name=name, grid=(n_t,),
        in_specs=[pl.BlockSpec((TILE, 1024), lambda i: (i, 0)), pl.BlockSpec(memory_space=pl.ANY),
                  pl.BlockSpec(memory_space=pl.ANY), pl.BlockSpec((TILE, 512), lambda i: (i, 0)),
                  pl.BlockSpec((TILE, 512), lambda i: (i, 0)), pl.BlockSpec((TILE, 512), lambda i: (i, 0))],
        out_specs=[pl.BlockSpec((TILE, 1024), lambda i: (i, 0)), pl.BlockSpec((TILE, 512), lambda i: (i, 0))],
        out_shape=[jax.ShapeDtypeStruct((t, 1024), F32), jax.ShapeDtypeStruct((t, 512), F32)],
        scratch_shapes=[pltpu.VMEM((t, 1024), BF), pltpu.VMEM((t, 512), BF), pltpu.VMEM((TILE, 1024), F32),
                        pltpu.VMEM((TILE, 512), BF), pltpu.SemaphoreType.DMA((2,))],
        compiler_params=_cparams(1),
    )(q_cat, k_cat, v, o, lse, do)


def mla_backward_kv(q_cat, k_cat, v, lse, delta, do_bf, name):
    t = q_cat.shape[0]
    n_t = t // TILE

    n_main = (t - TILE) // KV_CH

    def body(q_hbm, k_ref, v_ref, lse_ref, dl_ref, do_hbm, dk_ref, dv_ref, q_s, do_s, dk_s, dv_s, sem):
        j = pl.program_id(0)

        @pl.when(j == 0)
        def _():
            _resident(q_hbm, q_s, sem.at[0])
            _resident(do_hbm, do_s, sem.at[1])

        dk_s[...] = jnp.zeros_like(dk_s)
        dv_s[...] = jnp.zeros_like(dv_s)

        def chunk(r0, size):
            for h in range(HEADS):
                qh = q_s[pl.ds(r0, size), h * 256:(h + 1) * 256]
                doh = do_s[pl.ds(r0, size), h * 128:(h + 1) * 128]
                st = _dot_nt(k_ref[:, h * 256:(h + 1) * 256], qh) * ATT_SCALE_LOG2
                pt = jnp.exp2(st - lse_ref[h:h + 1, pl.ds(r0, size)])
                dpt = _dot_nt(v_ref[:, h * 128:(h + 1) * 128], doh)
                dst = pt * (dpt - dl_ref[h:h + 1, pl.ds(r0, size)])
                dv_s[:, h * 128:(h + 1) * 128] += _dot(pt.astype(BF), doh)
                dk_s[:, h * 256:(h + 1) * 256] += _dot(dst.astype(BF), qh)

        @pl.when(j == 0)
        def _():
            chunk(0, TILE)

        def step(c, carry):
            chunk(pl.multiple_of(TILE + c * KV_CH, TILE), KV_CH)
            return carry

        lax.fori_loop(0, n_main, step, 0)
        dk_ref[...] = dk_s[...] * ATT_SCALE
        dv_ref[...] = dv_s[...]

    kspec = lambda w: pl.BlockSpec((TILE, w), lambda j: (j, 0))
    rows = pl.BlockSpec((8, t), lambda j: (0, 0))
    return pl.pallas_call(
        body, name=name, grid=(n_t,),
        in_specs=[pl.BlockSpec(memory_space=pl.ANY), kspec(1024), kspec(512), rows, rows,
                  pl.BlockSpec(memory_space=pl.ANY)],
        out_specs=[kspec(1024), kspec(512)],
        out_shape=[jax.ShapeDtypeStruct((t, 1024), F32), jax.ShapeDtypeStruct((t, 512), F32)],
        scratch_shapes=[pltpu.VMEM((t, 1024), BF), pltpu.VMEM((t, 512), BF), pltpu.VMEM((TILE, 1024), F32),
                        pltpu.VMEM((TILE, 512), F32), pltpu.SemaphoreType.DMA((2,))],
        compiler_params=_cparams(1),
    )(q_cat, k_cat, v, lse, delta, do_bf)


def final_loss(xf, target, fnw, name):
    t = xf.shape[0]
    n_t = t // TILE

    def body(x_ref, t_ref, w_ref, loss_ref, dx_ref, dw_ref):
        i = pl.program_id(0)

        @pl.when(i == 0)
        def _():
            loss_ref[...] = jnp.zeros_like(loss_ref)
            dw_ref[...] = jnp.zeros_like(dw_ref)
            dx_ref[...] = jnp.zeros_like(dx_ref)

        @pl.when(i >= 1)
        def _():
            y, vjp_fn = jax.vjp(_rms, x_ref[...], w_ref[...])
            err = y - t_ref[...]
            loss_ref[...] += jnp.broadcast_to(0.5 * jnp.sum(jnp.mean(err * err, axis=-1, keepdims=True)), (8, 128))
            dx, dw = vjp_fn(err * (1.0 / D))
            dx_ref[...] = dx
            dw_ref[...] += dw

    return pl.pallas_call(
        body, name=name, grid=(n_t,),
        in_specs=[pl.BlockSpec((TILE, D), lambda i: (i, 0)), pl.BlockSpec((TILE, D), lambda i: (jnp.maximum(i - 1, 0), 0)),
                  pl.BlockSpec((1, D), lambda i: (0, 0))],
        out_specs=[pl.BlockSpec((8, 128), lambda i: (0, 0)), pl.BlockSpec((TILE, D), lambda i: (i, 0)),
                   pl.BlockSpec((1, D), lambda i: (0, 0))],
        out_shape=[jax.ShapeDtypeStruct((8, 128), F32), jax.ShapeDtypeStruct((t, D), F32),
                   jax.ShapeDtypeStruct((1, D), F32)],
        compiler_params=_cparams(1),
    )(xf, target, fnw)


def exchange(x, name, scatter):
    blk = x.shape[1:] if scatter else x.shape

    def body(x_ref, o_ref, ssem, rsem, lsem):
        xi, yi, ci = lax.axis_index("x"), lax.axis_index("y"), lax.axis_index("c")
        me = 4 * xi + 2 * yi + ci

        def peer(k):
            px = (1 - xi) if (k >> 2) & 1 else xi
            py = (1 - yi) if (k >> 1) & 1 else yi
            pc = (1 - ci) if k & 1 else ci
            return (px, py, pc), 4 * px + 2 * py + pc

        def copy(k):
            dev, pid = peer(k)
            return pltpu.make_async_remote_copy(
                src_ref=x_ref.at[pid] if scatter else x_ref, dst_ref=o_ref.at[me],
                send_sem=ssem.at[k - 1], recv_sem=rsem.at[k - 1], device_id=dev, device_id_type=pl.DeviceIdType.MESH)

        def landing(k):
            dev, pid = peer(k)
            return pltpu.make_async_remote_copy(
                src_ref=x_ref.at[pid] if scatter else x_ref, dst_ref=o_ref.at[pid],
                send_sem=ssem.at[k - 1], recv_sem=rsem.at[k - 1], device_id=dev, device_id_type=pl.DeviceIdType.MESH)

        own = pltpu.make_async_copy(x_ref.at[me] if scatter else x_ref, o_ref.at[me], lsem)
        own.start()
        sends = [copy(k) for k in range(1, N_DEV)]
        for cp in sends:
            cp.start()
        for k in range(1, N_DEV):
            landing(k).wait_recv()
        for cp in sends:
            cp.wait_send()
        own.wait()

    return pl.pallas_call(
        body, name=name,
        in_specs=[pl.BlockSpec(memory_space=pl.ANY)], out_specs=pl.BlockSpec(memory_space=pl.ANY),
        out_shape=jax.ShapeDtypeStruct((N_DEV,) + tuple(blk), x.dtype),
        scratch_shapes=[pltpu.SemaphoreType.DMA((N_DEV - 1,)), pltpu.SemaphoreType.DMA((N_DEV - 1,)),
                        pltpu.SemaphoreType.DMA(())],
        compiler_params=pltpu.CompilerParams(has_side_effects=True),
    )(x)


def mod_forward(crows, w_mod, b_shard, name):
    cols = w_mod.shape[2]

    def body(c_ref, w_ref, b_ref, o_ref):
        o_ref[0] = _dot(_silu(c_ref[...]).astype(BF), w_ref[0].astype(BF)) + b_ref[0]

    return pl.pallas_call(
        body, name=name, grid=(2,),
        in_specs=[pl.BlockSpec((16, D), lambda l: (0, 0)), pl.BlockSpec((1, D, cols), lambda l: (l, 0, 0)),
                  pl.BlockSpec((1, 1, cols), lambda l: (l, 0, 0))],
        out_specs=pl.BlockSpec((1, 16, cols), lambda l: (l, 0, 0)),
        out_shape=jax.ShapeDtypeStruct((2, 16, cols), F32), compiler_params=_cparams(1),
    )(crows, w_mod, b_shard)


def mod_backward(crows, w_mod, d_own, d_ctx, name):
    cols = w_mod.shape[2]

    def body(c_ref, w_ref, do_ref, dc_ref, gw_ref, gs_ref):
        dc = dc_ref[0]
        dsum = dc[0:1]
        for s in range(1, N_DEV):
            dsum = dsum + dc[s:s + 1]
        row = lax.broadcasted_iota(jnp.int32, (8, cols), 0)
        d16 = jnp.concatenate([do_ref[0], jnp.where(row == 0, jnp.broadcast_to(dsum, (8, cols)), 0.0)], axis=0)
        gw_ref[0] = _dot_tn(_silu(c_ref[...]).astype(BF), d16.astype(BF))
        gs_ref[0] = _dot_nt(jnp.broadcast_to(dsum, (8, cols)).astype(BF), w_ref[0].astype(BF))

    return pl.pallas_call(
        body, name=name, grid=(2,),
        in_specs=[pl.BlockSpec((16, D), lambda l: (0, 0)), pl.BlockSpec((1, D, cols), lambda l: (l, 0, 0)),
                  pl.BlockSpec((1, 8, cols), lambda l: (l, 0, 0)), pl.BlockSpec((1, 8, cols), lambda l: (l, 0, 0))],
        out_specs=[pl.BlockSpec((1, D, cols), lambda l: (l, 0, 0)), pl.BlockSpec((1, 8, D), lambda l: (l, 0, 0))],
        out_shape=[jax.ShapeDtypeStruct((2, D, cols), F32), jax.ShapeDtypeStruct((2, 8, D), F32)],
        compiler_params=_cparams(1),
    )(crows, w_mod, d_own, d_ctx)


def silu_grad_scale(c_ctx, ds, name):
    def body(c_ref, ds_ref, o_ref):
        cc = c_ref[...]
        sg = jax.nn.sigmoid(cc)
        o_ref[...] = (ds_ref[0][0:1] + ds_ref[1][0:1]) * (sg * (1.0 + cc * (1.0 - sg)))

    return pl.pallas_call(body, name=name, out_shape=jax.ShapeDtypeStruct((1, D), F32))(c_ctx, ds)


def adamw(parts, w, m, v, name, block_rows):
    n_parts, rows, _ = parts.shape

    def body(p_ref, w_ref, m_ref, v_ref, g_ref, d_ref, nm_ref, nv_ref):
        g = p_ref[0]
        for s in range(1, n_parts):
            g = g + p_ref[s]
        mm = ADAM_B1 * m_ref[...] + (1.0 - ADAM_B1) * g
        vv = ADAM_B2 * v_ref[...] + (1.0 - ADAM_B2) * (g * g)
        m_hat = mm / (1.0 - ADAM_B1 ** ADAM_STEP)
        v_hat = vv / (1.0 - ADAM_B2 ** ADAM_STEP)
        g_ref[...] = g
        d_ref[...] = -ADAM_LR * (m_hat / (jnp.sqrt(v_hat) + ADAM_EPS) + ADAM_WD * w_ref[...])
        nm_ref[...] = mm
        nv_ref[...] = vv

    spec = pl.BlockSpec((block_rows, 128), lambda i: (i, 0))
    return pl.pallas_call(
        body, name=name, grid=(rows // block_rows,),
        in_specs=[pl.BlockSpec((n_parts, block_rows, 128), lambda i: (0, i, 0)), spec, spec, spec],
        out_specs=[spec] * 4, out_shape=[jax.ShapeDtypeStruct((rows, 128), F32)] * 4,
        compiler_params=_cparams(1),
    )(parts, w, m, v)


def _pad_cols(w, segs, total):
    parts, pos = [], 0
    for dst, src, wd in segs:
        if dst > pos:
            parts.append(jnp.zeros(w.shape[:-1] + (dst - pos,), w.dtype))
        parts.append(w[..., src:src + wd])
        pos = dst + wd
    if pos < total:
        parts.append(jnp.zeros(w.shape[:-1] + (total - pos,), w.dtype))
    return jnp.concatenate(parts, axis=-1)


def _unpad_cols(g, segs):
    return jnp.concatenate([g[..., dst:dst + wd] for dst, _, wd in segs], axis=-1)


def _flat128(a, rows_multiple=8):
    f = a.reshape(-1)
    n = f.shape[0]
    per = 128 * rows_multiple
    pad = (-n) % per
    if pad:
        f = jnp.concatenate([f, jnp.zeros((pad,), f.dtype)])
    return f.reshape(-1, 128)


def _rope_tables(n_lat):
    rows = n_lat // GRID_W
    row = jnp.repeat(jnp.arange(rows), GRID_W).astype(F32)
    col = jnp.tile(jnp.arange(GRID_W), rows).astype(F32)
    freq = ROPE_BASE ** (-jnp.arange(16, dtype=F32) * 2.0 / 32)
    ar, ac = row[:, None] * freq[None, :], col[:, None] * freq[None, :]
    z = jnp.zeros((n_lat, 16), F32)
    cos = jnp.concatenate([jnp.cos(ar), jnp.cos(ar), jnp.cos(ac), jnp.cos(ac), jnp.ones((n_lat, 64), F32)], axis=1)
    sa = jnp.concatenate([-jnp.sin(ar), z, -jnp.sin(ac), z, jnp.zeros((n_lat, 64), F32)], axis=1)
    sb = jnp.concatenate([z, jnp.sin(ar), z, jnp.sin(ac), jnp.zeros((n_lat, 64), F32)], axis=1)
    ident = lambda fill: jnp.full((TILE, 128), fill, F32)
    return (jnp.concatenate([ident(1.0), cos]), jnp.concatenate([ident(0.0), sa]), jnp.concatenate([ident(0.0), sb]))


def _big_shard_shapes(w):
    return {k: w[k].shape for k in BIG_NAMES}


def _gathered_to_full(g, name):
    if name in ("w_out", "w_ff2"):
        return jnp.transpose(g, (1, 0, 2, 3)).reshape(2, -1, g.shape[-1])
    return jnp.transpose(g, (1, 2, 0, 3)).reshape(2, g.shape[2], -1)


def _full_to_chunks(gw, name):
    if name in ("w_out", "w_ff2"):
        return jnp.transpose(gw.reshape(2, N_DEV, -1, gw.shape[-1]), (1, 0, 2, 3))
    return jnp.transpose(gw.reshape(2, gw.shape[1], N_DEV, -1), (2, 0, 1, 3))


def kernel(x, c, ctx, c_ctx, w_mod, b_mod, norm1_w, w_in, w_out, sgu_norm_w, sgu_norm_b, sgu_w, sgu_b, gla_wg_fwd, gla_bg_fwd, gla_wg_bwd, gla_bg_bwd, gla_norm_w, mla_q_norm_w, mla_w_uq, mla_kv_norm_w, mla_w_ukv, norm2_w, w_ff1, w_ff2, final_norm_w, loss_target, m_c_ctx, m_w_mod, m_b_mod, m_norm1_w, m_w_in, m_w_out, m_sgu_norm_w, m_sgu_norm_b, m_sgu_w, m_sgu_b, m_gla_wg_fwd, m_gla_bg_fwd, m_gla_wg_bwd, m_gla_bg_bwd, m_gla_norm_w, m_mla_q_norm_w, m_mla_w_uq, m_mla_kv_norm_w, m_mla_w_ukv, m_norm2_w, m_w_ff1, m_w_ff2, m_final_norm_w, v_c_ctx, v_w_mod, v_b_mod, v_norm1_w, v_w_in, v_w_out, v_sgu_norm_w, v_sgu_norm_b, v_sgu_w, v_sgu_b, v_gla_wg_fwd, v_gla_bg_fwd, v_gla_wg_bwd, v_gla_bg_bwd, v_gla_norm_w, v_mla_q_norm_w, v_mla_w_uq, v_mla_kv_norm_w, v_mla_w_ukv, v_norm2_w, v_w_ff1, v_w_ff2, v_final_norm_w):
    W = dict(c_ctx=c_ctx, w_mod=w_mod, b_mod=b_mod, norm1_w=norm1_w, w_in=w_in, w_out=w_out, sgu_norm_w=sgu_norm_w,
             sgu_norm_b=sgu_norm_b, sgu_w=sgu_w, sgu_b=sgu_b, gla_wg_fwd=gla_wg_fwd, gla_bg_fwd=gla_bg_fwd,
             gla_wg_bwd=gla_wg_bwd, gla_bg_bwd=gla_bg_bwd, gla_norm_w=gla_norm_w, mla_q_norm_w=mla_q_norm_w,
             mla_w_uq=mla_w_uq, mla_kv_norm_w=mla_kv_norm_w, mla_w_ukv=mla_w_ukv, norm2_w=norm2_w, w_ff1=w_ff1,
             w_ff2=w_ff2, final_norm_w=final_norm_w)
    M = dict(c_ctx=m_c_ctx, w_mod=m_w_mod, b_mod=m_b_mod, norm1_w=m_norm1_w, w_in=m_w_in, w_out=m_w_out,
             sgu_norm_w=m_sgu_norm_w, sgu_norm_b=m_sgu_norm_b, sgu_w=m_sgu_w, sgu_b=m_sgu_b, gla_wg_fwd=m_gla_wg_fwd,
             gla_bg_fwd=m_gla_bg_fwd, gla_wg_bwd=m_gla_wg_bwd, gla_bg_bwd=m_gla_bg_bwd, gla_norm_w=m_gla_norm_w,
             mla_q_norm_w=m_mla_q_norm_w, mla_w_uq=m_mla_w_uq, mla_kv_norm_w=m_mla_kv_norm_w, mla_w_ukv=m_mla_w_ukv,
             norm2_w=m_norm2_w, w_ff1=m_w_ff1, w_ff2=m_w_ff2, final_norm_w=m_final_norm_w)
    V = dict(c_ctx=v_c_ctx, w_mod=v_w_mod, b_mod=v_b_mod, norm1_w=v_norm1_w, w_in=v_w_in, w_out=v_w_out,
             sgu_norm_w=v_sgu_norm_w, sgu_norm_b=v_sgu_norm_b, sgu_w=v_sgu_w, sgu_b=v_sgu_b, gla_wg_fwd=v_gla_wg_fwd,
             gla_bg_fwd=v_gla_bg_fwd, gla_wg_bwd=v_gla_wg_bwd, gla_bg_bwd=v_gla_bg_bwd, gla_norm_w=v_gla_norm_w,
             mla_q_norm_w=v_mla_q_norm_w, mla_w_uq=v_mla_w_uq, mla_kv_norm_w=v_mla_kv_norm_w, mla_w_ukv=v_mla_w_ukv,
             norm2_w=v_norm2_w, w_ff1=v_w_ff1, w_ff2=v_w_ff2, final_norm_w=v_final_norm_w)

    n_lat = x.shape[1]
    assert ctx.shape[1] == TILE and n_lat % TILE == 0 and x.shape[2] == D
    t_all = TILE + n_lat
    n_t = t_all // TILE
    me = 4 * lax.axis_index("x") + 2 * lax.axis_index("y") + lax.axis_index("c")
    mod_cols = w_mod.shape[2]

    c_all = exchange(c, "ag_c", scatter=False).reshape(N_DEV, D)
    crows = jnp.concatenate([c_all, c_ctx[None, :], jnp.zeros((7, D), F32)], axis=0)
    b_shard = lax.dynamic_slice_in_dim(b_mod, me * mod_cols, mod_cols, axis=1)[:, None, :]
    mod_sh = mod_forward(crows, w_mod, b_shard, "mod_fwd")
    mod_g = exchange(mod_sh.reshape(32, mod_cols), "ag_mod", scatter=False)
    mod_full = jnp.transpose(mod_g.reshape(N_DEV, 2, 16, mod_cols), (1, 2, 0, 3)).reshape(2, 16, 6 * D)
    mod_own = lax.dynamic_index_in_dim(mod_full, me, axis=1, keepdims=False)
    mod_ctx = mod_full[:, 8, :]
    pad2 = jnp.zeros((2, D), F32)
    modl = [jnp.stack([jnp.concatenate([mod_ctx[l].reshape(6, D), pad2]),
                       jnp.concatenate([mod_own[l].reshape(6, D), pad2])]) for l in range(2)]

    shard_shapes = _big_shard_shapes(W)
    pack = jnp.concatenate([W[k].astype(BF).reshape(-1) for k in BIG_NAMES]).reshape(-1, 128)
    gathered = exchange(pack, "ag_weights", scatter=False).reshape(N_DEV, -1)
    full, off = {}, 0
    for k in BIG_NAMES:
        n = int(np.prod(shard_shapes[k]))
        full[k] = _gathered_to_full(gathered[:, off:off + n].reshape((N_DEV,) + shard_shapes[k]), k)
        off += n
    w_in_p = _pad_cols(full["w_in"], W_IN_SEGS, P_COLS)
    w_uq_p = _pad_cols(full["mla_w_uq"], W_UQ_SEGS, 1024).astype(F32)
    w_ukv_f = full["mla_w_ukv"].astype(F32)
    wgf_p = jnp.pad(gla_wg_fwd, ((0, 0), (0, 112), (0, 0)))
    wgb_p = jnp.pad(gla_wg_bwd, ((0, 0), (0, 112), (0, 0)))
    sgu_bx = jnp.repeat(jnp.transpose(sgu_b, (0, 2, 1)), 64, axis=2)
    gnw_t = jnp.tile(gla_norm_w, (1, HEADS))
    rc, rsa, rsb = _rope_tables(n_lat)

    xin = jnp.concatenate([ctx[0], x[0]], axis=0)
    row = lambda a: a[None, :]

    def pre_ins(l, xl):
        return [("x", "tile", True, xl), ("mod", "kind", True, modl[l]), ("n1w", "full", True, row(norm1_w[l])),
                ("w_in", "wfull", False, w_in_p[l]), ("sgu_nw", "full", True, row(sgu_norm_w[l])),
                ("sgu_nb", "full", True, row(sgu_norm_b[l])), ("sgu_w", "full", True, sgu_w[l]),
                ("sgu_bx", "full", True, sgu_bx[l]), ("wgf", "full", True, wgf_p[l]), ("bgf", "full", True, row(gla_bg_fwd[l])),
                ("wgb", "full", True, wgb_p[l]), ("bgb", "full", True, row(gla_bg_bwd[l])),
                ("qnw", "full", True, row(mla_q_norm_w[l])), ("w_uq", "full", True, w_uq_p[l]),
                ("kvnw", "full", True, row(mla_kv_norm_w[l])), ("w_ukv", "full", True, w_ukv_f[l]),
                ("rc", "tile", False, rc), ("rsa", "tile", False, rsa), ("rsb", "tile", False, rsb)]

    pre_outs = [("y_sgu", 256, F32), ("qg", 128, F32), ("kg", 128, F32), ("vg", 256, F32), ("lgf", 128, F32),
                ("lgb", 128, F32), ("gr", 256, F32), ("q_cat", 1024, BF), ("k_cat", 1024, BF), ("v", 512, BF)]

    def out_ins(l, xl, a):
        return [("x", "tile", True, xl), ("mod", "kind", True, modl[l]), ("y_sgu", "tile", True, a["y_sgu"]),
                ("o_f", "tile", True, a["o_f"]), ("o_b", "tile", True, a["o_b"]), ("gr", "tile", True, a["gr"]),
                ("y_mla", "tile", True, a["y_mla"]), ("gnw", "full", True, row(gnw_t[l])),
                ("w_out", "wfull", False, full["w_out"][l])]

    def ffn_ins(l, x1):
        return [("x1", "tile", True, x1), ("mod", "kind", True, modl[l]), ("n2w", "full", True, row(norm2_w[l])),
                ("w_ff1", "wcols", False, full["w_ff1"][l]), ("w_ff2", "wrows", False, full["w_ff2"][l])]

    saved, xl = [], xin
    for l in range(2):
        a = tile_forward(pre_tile, f"pre_fwd{l}", t_all, pre_ins(l, xl), pre_outs)
        a["o_f"], a["sf"] = gla_forward(a["qg"], a["kg"], a["vg"], a["lgf"], False, f"gla_f_fwd{l}")
        a["o_b"], a["sb"] = gla_forward(a["qg"], a["kg"], a["vg"], a["lgb"], True, f"gla_b_fwd{l}")
        a["y_mla"], a["lse"] = mla_forward(a["q_cat"], a["k_cat"], a["v"], f"mla_fwd{l}")
        a["x"] = xl
        a["x1"] = tile_forward(attn_out_tile, f"out_fwd{l}", t_all, out_ins(l, xl, a), [("x1", D, F32)])["x1"]
        xl = tile_forward(ffn_tile, f"ffn_fwd{l}", t_all, ffn_ins(l, a["x1"]), [("x2", D, F32)])["x2"]
        saved.append(a)

    loss_blk, dx, d_fnw = final_loss(xl, loss_target[0], row(final_norm_w), "final_loss")
    loss = lax.psum(loss_blk[0, 0], AXES)

    G = {}
    dmods = []
    for l in (1, 0):
        a = saved[l]
        g3, e3 = tile_backward(ffn_tile, f"ffn_bwd{l}", t_all, ffn_ins(l, a["x1"]), [("x2", dx)],
                               [("zpre", D_FF), ("zf", D)], [("a_ff1", D), ("a_ff2", D_FF)], tile=128)
        gw_ff1 = wgrad(e3["a_ff1"], e3["zpre"], f"wg_ff1_{l}")
        gw_ff2 = wgrad(e3["a_ff2"], e3["zf"], f"wg_ff2_{l}")
        g2, e2 = tile_backward(attn_out_tile, f"out_bwd{l}", t_all, out_ins(l, a["x"], a), [("x1", g3["x1"])],
                               [("zt", D)], [("a_out", D)])
        gw_out = wgrad(e2["a_out"], e2["zt"], f"wg_out_{l}")
        dq_cat, delta = mla_backward_q(a["q_cat"], a["k_cat"], a["v"], a["y_mla"], a["lse"], g2["y_mla"], f"mla_bwd_q{l}")
        head_rows = lambda r: jnp.pad(jnp.transpose(r[:, ::128]), ((0, 8 - HEADS), (0, 0)))
        dk_cat, dv = mla_backward_kv(a["q_cat"], a["k_cat"], a["v"], head_rows(a["lse"]), head_rows(delta),
                                     g2["y_mla"].astype(BF), f"mla_bwd_kv{l}")
        do = g2["o_f"]
        dqf, dkf, dvf, dgf = gla_backward(a["qg"], a["kg"], a["vg"], a["lgf"], a["sf"], do, False, f"gla_f_bwd{l}")
        dqb, dkb, dvb, dgb = gla_backward(a["qg"], a["kg"], a["vg"], a["lgb"], a["sb"], do, True, f"gla_b_bwd{l}")
        cots = [("y_sgu", g2["y_sgu"]), ("qg", dqf + dqb), ("kg", dkf + dkb), ("vg", dvf + dvb), ("lgf", dgf),
                ("lgb", dgb), ("gr", g2["gr"]), ("q_cat", dq_cat), ("k_cat", dk_cat), ("v", dv)]
        g1, e1 = tile_backward(pre_tile, f"pre_bwd{l}", t_all, pre_ins(l, a["x"]), cots, [("zp", P_COLS)], [("a_in", D)])
        gw_in = _unpad_cols(wgrad(e1["a_in"], e1["zp"], f"wg_in_{l}", bk2=P_COLS), W_IN_SEGS)
        dx = g1["x"] + g2["x"]
        dmods.append(g1["mod"] + g2["mod"] + g3["mod"])
        G[l] = dict(w_in=gw_in, w_out=gw_out, w_ff1=gw_ff1, w_ff2=gw_ff2,
                    mla_w_uq=_unpad_cols(g1["w_uq"], W_UQ_SEGS), mla_w_ukv=g1["w_ukv"],
                    norm1_w=g1["n1w"][0], norm2_w=g3["n2w"][0], sgu_norm_w=g1["sgu_nw"][0], sgu_norm_b=g1["sgu_nb"][0],
                    sgu_w=g1["sgu_w"], sgu_b=jnp.transpose(g1["sgu_bx"].reshape(128, HEADS, 64).sum(-1)),
                    gla_wg_fwd=g1["wgf"][:16], gla_bg_fwd=g1["bgf"][0], gla_wg_bwd=g1["wgb"][:16], gla_bg_bwd=g1["bgb"][0],
                    gla_norm_w=g2["gnw"][0].reshape(HEADS, 64).sum(0), mla_q_norm_w=g1["qnw"][0], mla_kv_norm_w=g1["kvnw"][0])
    dmods = dmods[::-1]
    grad_x = dx[TILE:][None]

    dmod_pack = jnp.stack([jnp.stack([dmods[l][1, :6].reshape(-1), dmods[l][0, :6].reshape(-1)]) for l in range(2)])
    dmod_all = exchange(dmod_pack.reshape(4, 6 * D), "ag_dmod", scatter=False).reshape(N_DEV, 2, 2, 6 * D)
    dsl = lax.dynamic_slice_in_dim(dmod_all, me * mod_cols, mod_cols, axis=3)
    d_own = jnp.transpose(dsl[:, :, 0, :], (1, 0, 2))
    d_ctx = jnp.transpose(dsl[:, :, 1, :], (1, 0, 2))
    g_w_mod, ds_cc = mod_backward(crows, w_mod, d_own, d_ctx, "mod_bwd")
    g_c_ctx_part = silu_grad_scale(c_ctx[None, :], ds_cc, "silu_bwd")[0]
    g_b_mod_part = jnp.stack([dmods[l][1, :6].reshape(-1) + dmods[l][0, :6].reshape(-1) for l in range(2)])

    small_g = dict(c_ctx=g_c_ctx_part, b_mod=g_b_mod_part, final_norm_w=d_fnw[0])
    for k in SMALL_NAMES:
        if k not in small_g:
            small_g[k] = jnp.stack([G[0][k], G[1][k]])
    spack = lambda dct: _flat128(jnp.concatenate([dct[k].reshape(-1) for k in SMALL_NAMES]))
    sparts = exchange(spack(small_g), "ag_small", scatter=False)
    s_rows = sparts.shape[1]
    s_out = adamw(sparts, spack(W), spack(M), spack(V), "adamw_small", s_rows)

    big_chunks = jnp.concatenate(
        [_full_to_chunks(jnp.stack([G[0][k], G[1][k]]), k).reshape(N_DEV, -1) for k in BIG_NAMES], axis=1)
    b_rows = big_chunks.shape[1] // 128
    bparts = exchange(big_chunks.reshape(N_DEV, b_rows, 128), "a2a_grads", scatter=True)
    bpack = lambda dct: jnp.concatenate([dct[k].reshape(-1) for k in BIG_NAMES]).reshape(b_rows, 128)
    blk = b_rows
    for cand in range(2048, 7, -8):
        if b_rows % cand == 0:
            blk = cand
            break
    b_out = adamw(bparts, bpack(W), bpack(M), bpack(V), "adamw_big", blk)

    wm_rows = w_mod.size // 128
    wm = lambda a: a.reshape(wm_rows, 128)
    m_out = adamw(wm(g_w_mod)[None], wm(w_mod), wm(m_w_mod), wm(v_w_mod), "adamw_mod", 1024)

    res = {}
    off = 0
    for k in SMALL_NAMES:
        n = W[k].size
        res[k] = [o.reshape(-1)[off:off + n].reshape(W[k].shape) for o in s_out]
        off += n
    off = 0
    for k in BIG_NAMES:
        n = W[k].size
        res[k] = [o.reshape(-1)[off:off + n].reshape(W[k].shape) for o in b_out]
        off += n
    res["w_mod"] = [o.reshape(w_mod.shape) for o in m_out]
    outs = [loss, grad_x]
    for j in range(4):
        outs += [res[k][j] for k in WEIGHT_ORDER]
    return tuple(outs)
```

```python
import functools

import numpy as np
import jax
import jax.numpy as jnp
from jax import lax
from jax.experimental import pallas as pl
from jax.experimental.pallas import tpu as pltpu

F32 = jnp.float32
BF = jnp.bfloat16

N_DEV = 8
AXES = ("x", "y", "c")
EPS = 1e-6
D = 1024
TILE = 256
GCH = 64
SGU_CHUNK = 128
HEADS = 4
ROPE_BASE = 10000.0
GRID_W = 64
GLA_TAU = 16.0
ATT_SCALE = (128 + 64) ** -0.5
ATT_SCALE_LOG2 = ATT_SCALE * 1.4426950408889634
LN2 = 0.6931471805599453
KV_CH = 512
D_FF = 4096
FF_CH = 1024

ADAM_LR = 0.001
ADAM_B1 = 0.9
ADAM_B2 = 0.999
ADAM_EPS = 1e-08
ADAM_WD = 0.01
ADAM_STEP = 10

VMEM_LIMIT_MB = 56

W_IN_SEGS = ((0, 0, 128), (128, 128, 256), (384, 384, 16), (512, 400, 16), (640, 416, 256), (896, 672, 64),
             (1024, 736, 256), (1280, 992, 256), (1536, 1248, 128), (1664, 1376, 256), (1920, 1632, 256))
P_COLS = 2176
O_GK, O_GV, O_GGF, O_GGB, O_CKV, O_KR, O_SU, O_SV, O_GQ, O_GR, O_DQ = (s[0] for s in W_IN_SEGS)
W_UQ_SEGS = tuple((h * 256, h * 192, 192) for h in range(HEADS))

SMALL_NAMES = ("c_ctx", "b_mod", "norm1_w", "sgu_norm_w", "sgu_norm_b", "sgu_w", "sgu_b", "gla_wg_fwd", "gla_bg_fwd",
               "gla_wg_bwd", "gla_bg_bwd", "gla_norm_w", "mla_q_norm_w", "mla_kv_norm_w", "norm2_w", "final_norm_w")
BIG_NAMES = ("w_in", "w_out", "mla_w_uq", "mla_w_ukv", "w_ff1", "w_ff2")
WEIGHT_ORDER = ("c_ctx", "w_mod", "b_mod", "norm1_w", "w_in", "w_out", "sgu_norm_w", "sgu_norm_b", "sgu_w", "sgu_b",
                "gla_wg_fwd", "gla_bg_fwd", "gla_wg_bwd", "gla_bg_bwd", "gla_norm_w", "mla_q_norm_w", "mla_w_uq",
                "mla_kv_norm_w", "mla_w_ukv", "norm2_w", "w_ff1", "w_ff2", "final_norm_w")


def _cparams(n_axes):
    return pltpu.CompilerParams(dimension_semantics=("arbitrary",) * n_axes,
                                vmem_limit_bytes=VMEM_LIMIT_MB * 1024 * 1024)


def _dot(a, b):
    return jnp.dot(a, b, preferred_element_type=F32)


def _dot_nt(a, b):
    return lax.dot_general(a, b, (((1,), (1,)), ((), ())), preferred_element_type=F32)


def _dot_tn(a, b):
    return lax.dot_general(a, b, (((0,), (0,)), ((), ())), preferred_element_type=F32)


def _nn(a, b):
    return _dot(a.astype(BF), b.astype(BF))


def _nt(a, b):
    return _dot_nt(a.astype(BF), b.astype(BF))


def _tn(a, b):
    return _dot_tn(a.astype(BF), b.astype(BF))


nn_d = jax.custom_vjp(_nn)
nt_d = jax.custom_vjp(_nt)
tn_d = jax.custom_vjp(_tn)
nn_d.defvjp(lambda a, b: (_nn(a, b), (a, b)), lambda r, dy: (_nt(dy, r[1]), _tn(r[0], dy)))
nt_d.defvjp(lambda a, b: (_nt(a, b), (a, b)), lambda r, dy: (_nn(dy, r[1]), _tn(dy, r[0])))
tn_d.defvjp(lambda a, b: (_tn(a, b), (a, b)), lambda r, dy: (_nt(r[1], dy), _nn(r[0], dy)))


def nn_const(w_bf, diff):
    def raw(a):
        return _dot(a.astype(BF), w_bf)

    if not diff:
        return raw
    f = jax.custom_vjp(raw)
    f.defvjp(lambda a: (raw(a), None), lambda _, dy: (_dot_nt(dy.astype(BF), w_bf),))
    return f


def _split3(g):
    hi = g.astype(BF)
    r = g - hi.astype(F32)
    mid = r.astype(BF)
    lo = (r - mid.astype(F32)).astype(BF)
    return hi, mid, lo


def make_cum(tri_bf, tri_t_bf, diff):
    def raw(g, t):
        hi, mid, lo = _split3(g)
        return _dot(t, hi) + _dot(t, mid) + _dot(t, lo)

    def fwd(g):
        return raw(g, tri_bf)

    if not diff:
        return fwd
    cum = jax.custom_vjp(fwd)
    cum.defvjp(lambda g: (fwd(g), None), lambda _, db: (raw(db, tri_t_bf),))
    return cum


def _roll_lanes(x, shift):
    return pltpu.roll(x, shift, 1)


def make_rope(c, sa, sb, diff):
    def raw(x):
        return x * c + _roll_lanes(x, 112) * sa + _roll_lanes(x, 16) * sb

    if not diff:
        return raw
    f = jax.custom_vjp(raw)
    f.defvjp(lambda x: (raw(x), None),
             lambda _, dy: (dy * c + _roll_lanes(dy * sa, 16) + _roll_lanes(dy * sb, 112),))
    return f


def _ops(diff):
    return (nn_d, nt_d, tn_d) if diff else (_nn, _nt, _tn)


def _rms(x, w):
    return x * lax.rsqrt(jnp.mean(x * x, axis=-1, keepdims=True) + EPS) * w


def _gelu(x):
    return 0.5 * x * (1.0 + jnp.tanh(0.7978845608028654 * (x + 0.044715 * (x * x * x))))


def _silu(x):
    return x * jax.nn.sigmoid(x)


def _log_sigmoid(z):
    return jnp.minimum(z, 0.0) - jnp.log(1.0 + jnp.exp(-jnp.abs(z)))


def _lane_group_mask(width, group, h):
    lane = lax.broadcasted_iota(jnp.int32, (1, width), 1)
    return ((lane >= h * group) & (lane < (h + 1) * group)).astype(F32)


def pre_tile(d, c, z):
    nn, _, _ = _ops(z is not None)
    rope = make_rope(c["rc"], c["rsa"], c["rsb"], z is not None)
    mod = d["mod"]
    h = _rms(d["x"], d["n1w"]) * (1.0 + mod[1:2]) + mod[0:1]
    p = nn_const(c["w_in"], z is not None)(h)
    if z is not None:
        p = p + z["zp"]
    gk, gv = p[:, O_GK:O_GK + 128], p[:, O_GV:O_GV + 256]
    ggf, ggb = p[:, O_GGF:O_GGF + 128], p[:, O_GGB:O_GGB + 128]
    ckv, kr = p[:, O_CKV:O_CKV + 256], p[:, O_KR:O_KR + 128]
    su, sv = p[:, O_SU:O_SU + 256], p[:, O_SV:O_SV + 256]
    gq, gr, dq = p[:, O_GQ:O_GQ + 128], p[:, O_GR:O_GR + 256], p[:, O_DQ:O_DQ + 256]

    u = _gelu(su)
    gv_ = _gelu(sv)
    mu = jnp.mean(gv_, axis=-1, keepdims=True)
    cen = gv_ - mu
    vn = cen * lax.rsqrt(jnp.mean(cen * cen, axis=-1, keepdims=True) + EPS) * d["sgu_nw"] + d["sgu_nb"]
    hm = [_lane_group_mask(256, 64, hh) for hh in range(HEADS)]
    rows = []
    for ci in range(vn.shape[0] // SGU_CHUNK):
        vc = vn[ci * SGU_CHUNK:(ci + 1) * SGU_CHUNK]
        s = d["sgu_bx"]
        for hh in range(HEADS):
            s = s + hm[hh] * nn(d["sgu_w"][hh], vc)
        rows.append(s)
    y_sgu = u * jnp.concatenate(rows, axis=0)

    qg = gq * (32 ** -0.5)
    lgf = _log_sigmoid(nn(ggf, d["wgf"]) + d["bgf"]) * (1.0 / GLA_TAU)
    lgb = _log_sigmoid(nn(ggb, d["wgb"]) + d["bgb"]) * (1.0 / GLA_TAU)

    kv = nn(_rms(ckv, d["kvnw"]), d["w_ukv"])
    kr_r = rope(kr)
    q = nn(_rms(dq, d["qnw"]), d["w_uq"])
    qs, ks, vs = [], [], []
    for hh in range(HEADS):
        qs += [q[:, hh * 256:hh * 256 + 128], rope(q[:, hh * 256 + 128:(hh + 1) * 256])]
        ks += [kv[:, hh * 256:hh * 256 + 128], kr_r]
        vs += [kv[:, hh * 256 + 128:(hh + 1) * 256]]
    outs = dict(y_sgu=y_sgu, qg=qg, kg=gk, vg=gv, lgf=lgf, lgb=lgb, gr=gr,
                q_cat=jnp.concatenate(qs, axis=-1) * ATT_SCALE_LOG2, k_cat=jnp.concatenate(ks, axis=-1), v=jnp.concatenate(vs, axis=-1))
    return outs, dict(a_in=h)


def attn_out_tile(d, c, z):
    mod = d["mod"]
    o = d["o_f"] + d["o_b"]
    ms = jnp.zeros_like(o)
    for hh in range(HEADS):
        m_h = _lane_group_mask(256, 64, hh)
        ms = ms + m_h * (jnp.sum(o * o * m_h, axis=-1, keepdims=True) * (1.0 / 64))
    yg = o * lax.rsqrt(ms + EPS) * d["gnw"] * _silu(d["gr"])
    y = jnp.concatenate([d["y_sgu"], yg, d["y_mla"]], axis=-1)
    t = nn_const(c["w_out"], z is not None)(y)
    if z is not None:
        t = t + z["zt"]
    return dict(x1=d["x"] + mod[2:3] * t), dict(a_out=y)


def ffn_tile(d, c, z):
    mod = d["mod"]
    h2 = _rms(d["x1"], d["n2w"]) * (1.0 + mod[4:5]) + mod[3:4]
    f = None
    a2s = []
    for j in range(D_FF // FF_CH):
        pre = nn_const(c["w_ff1"][j], z is not None)(h2)
        if z is not None:
            pre = pre + z["zpre"][:, j * FF_CH:(j + 1) * FF_CH]
        a = jnp.maximum(pre, 0.0)
        a2 = a * a
        a2s.append(a2)
        fj = nn_const(c["w_ff2"][j], z is not None)(a2)
        f = fj if f is None else f + fj
    if z is not None:
        f = f + z["zf"]
    return dict(x2=d["x1"] + mod[5:6] * f), dict(a_ff1=h2, a_ff2=jnp.concatenate(a2s, axis=-1))


def _in_spec(kind, arr, tile):
    if kind == "tile":
        return pl.BlockSpec((tile, arr.shape[1]), lambda i: (i, 0))
    if kind == "kind":
        return pl.BlockSpec((1,) + arr.shape[1:], lambda i: (jnp.where(i < TILE // tile, 0, 1), 0, 0))
    nd = arr.ndim
    if kind in ("wfull", "wcols", "wrows"):
        return pl.BlockSpec(arr.shape, lambda i: (0,) * nd, pipeline_mode=pl.Buffered(1))
    return pl.BlockSpec(arr.shape, lambda i: (0,) * nd)


def _load(kind, ref):
    if kind == "kind":
        return ref[0]
    if kind == "wcols":
        return [ref[:, j * FF_CH:(j + 1) * FF_CH] for j in range(ref.shape[1] // FF_CH)]
    if kind == "wrows":
        return [ref[j * FF_CH:(j + 1) * FF_CH, :] for j in range(ref.shape[0] // FF_CH)]
    return ref[...]


def tile_forward(fn, name, t_all, ins, out_defs, tile=TILE):
    keys = [k for k, _, _, _ in ins]
    kinds = [kd for _, kd, _, _ in ins]
    diffs = [df for _, _, df, _ in ins]
    arrs = [a for _, _, _, a in ins]
    n_in = len(ins)

    def body(*refs):
        vals = [_load(kinds[j], refs[j]) for j in range(n_in)]
        d = {keys[j]: vals[j] for j in range(n_in) if diffs[j]}
        c = {keys[j]: vals[j] for j in range(n_in) if not diffs[j]}
        outs, _ = fn(d, c, None)
        for j, (k, _, dt) in enumerate(out_defs):
            refs[n_in + j][...] = outs[k].astype(dt)

    res = pl.pallas_call(
        body, name=name, grid=(t_all // tile,),
        in_specs=[_in_spec(kinds[j], arrs[j], tile) for j in range(n_in)],
        out_specs=[pl.BlockSpec((tile, w), lambda i: (i, 0)) for _, w, _ in out_defs],
        out_shape=[jax.ShapeDtypeStruct((t_all, w), dt) for _, w, dt in out_defs],
        compiler_params=_cparams(1),
    )(*arrs)
    return {k: r for (k, _, _), r in zip(out_defs, res)}


def tile_backward(fn, name, t_all, ins, cots, z_defs, aux_defs, tile=TILE, resid=None):
    keys = [k for k, _, _, _ in ins]
    kinds = [kd for _, kd, _, _ in ins]
    diffs = [df for _, _, df, _ in ins]
    arrs = [a for _, _, _, a in ins]
    cot_keys, cot_arrs = [], []
    for k, a in cots:
        for one in (a if isinstance(a, (list, tuple)) else [a]):
            cot_keys.append(k)
            cot_arrs.append(one)
    if resid is not None:
        cot_keys.append("resid:" + resid[0])
        cot_arrs.append(resid[1])
    n_in, n_cot = len(ins), len(cot_arrs)
    dkeys = [j for j in range(n_in) if diffs[j]]
    ctx_tiles = TILE // tile

    def body(*refs):
        i = pl.program_id(0)
        vals = [_load(kinds[j], refs[j]) for j in range(n_in)]
        d = {keys[j]: vals[j] for j in dkeys}
        c = {keys[j]: vals[j] for j in range(n_in) if not diffs[j]}
        zs = {k: jnp.zeros((tile, w), F32) for k, w in z_defs}
        outs, vjp_fn, aux = jax.vjp(lambda dd, zz: fn(dd, c, zz), d, zs, has_aux=True)
        ct = {}
        for j, k in enumerate(cot_keys):
            ct[k] = refs[n_in + j][...] + ct[k] if k in ct else refs[n_in + j][...]
        dd, dz = vjp_fn({k: ct[k].astype(outs[k].dtype) for k in outs})
        base = n_in + n_cot
        for n, j in enumerate(dkeys):
            ref, g = refs[base + n], dd[keys[j]]
            if kinds[j] == "tile":
                ref[...] = g + ct["resid:" + keys[j]] if "resid:" + keys[j] in ct else g
            else:
                first = ((i == 0) | (i == ctx_tiles)) if kinds[j] == "kind" else (i == 0)
                gv = g[None] if kinds[j] == "kind" else g

                @pl.when(first)
                def _(ref=ref, gv=gv):
                    ref[...] = gv

                @pl.when(jnp.logical_not(first))
                def _(ref=ref, gv=gv):
                    ref[...] += gv
        base += len(dkeys)
        for n, (k, _) in enumerate(z_defs):
            refs[base + n][...] = dz[k].astype(BF)
        base += len(z_defs)
        for n, (k, _) in enumerate(aux_defs):
            refs[base + n][...] = aux[k].astype(BF)

    out_specs, out_shape = [], []
    for j in dkeys:
        out_specs.append(_in_spec(kinds[j], arrs[j], tile))
        out_shape.append(jax.ShapeDtypeStruct(arrs[j].shape, F32))
    for _, w in list(z_defs) + list(aux_defs):
        out_specs.append(pl.BlockSpec((tile, w), lambda i: (i, 0)))
        out_shape.append(jax.ShapeDtypeStruct((t_all, w), BF))
    res = pl.pallas_call(
        body, name=name, grid=(t_all // tile,),
        in_specs=[_in_spec(kinds[j], arrs[j], tile) for j in range(n_in)]
        + [pl.BlockSpec((tile, a.shape[1]), lambda i: (i, 0)) for a in cot_arrs],
        out_specs=out_specs, out_shape=out_shape, compiler_params=_cparams(1),
    )(*arrs, *cot_arrs)
    grads = {keys[j]: res[n] for n, j in enumerate(dkeys)}
    extra = {k: res[len(dkeys) + n] for n, (k, _) in enumerate(list(z_defs) + list(aux_defs))}
    return grads, extra


def ffn_backward(x1, modl, n2w, w1, w2, dx2, name):
    t_all = x1.shape[0]
    n_ch = D_FF // FF_CH

    def head(x, mod, nw):
        return _rms(x, nw) * (1.0 + mod[4:5]) + mod[3:4]

    def body(x_ref, mod_ref, nw_ref, w1_ref, w2_ref, dx2_ref, dx1_ref, dmod_ref, dnw_ref, zpre_ref, zf_ref, a1_ref, a2_ref):
        i = pl.program_id(0)
        mod = mod_ref[0]
        dx2 = dx2_ref[...]
        h2, vjp_head = jax.vjp(head, x_ref[...], mod, nw_ref[...])
        h2b = h2.astype(BF)
        dfb = (dx2 * mod[5:6]).astype(BF)
        f = jnp.zeros((TILE, D), F32)
        dh2 = jnp.zeros((TILE, D), F32)
        for j in range(n_ch):
            cs = slice(j * FF_CH, (j + 1) * FF_CH)
            a = jnp.maximum(_dot(h2b, w1_ref[:, cs]), 0.0)
            a2b = (a * a).astype(BF)
            f = f + _dot(a2b, w2_ref[cs, :])
            dpre = (_dot_nt(dfb, w2_ref[cs, :]) * (2.0 * a)).astype(BF)
            dh2 = dh2 + _dot_nt(dpre, w1_ref[:, cs])
            zpre_ref[:, cs] = dpre
            a2_ref[:, cs] = a2b
        zf_ref[...] = dfb
        a1_ref[...] = h2b
        dx1, dmod, dnw = vjp_head(dh2)
        dx1_ref[...] = dx2 + dx1
        row = lax.broadcasted_iota(jnp.int32, (8, D), 0)
        dmod = dmod + jnp.where(row == 5, jnp.sum(dx2 * f, axis=0, keepdims=True), 0.0)
        first_kind = (i == 0) | (i == 1)

        @pl.when(first_kind)
        def _():
            dmod_ref[0] = dmod

        @pl.when(jnp.logical_not(first_kind))
        def _():
            dmod_ref[0] += dmod

        @pl.when(i == 0)
        def _():
            dnw_ref[...] = dnw

        @pl.when(i > 0)
        def _():
            dnw_ref[...] += dnw

    tspec = lambda w: pl.BlockSpec((TILE, w), lambda i: (i, 0))
    once = lambda shp: pl.BlockSpec(shp, lambda i: (0, 0), pipeline_mode=pl.Buffered(1))
    kind = pl.BlockSpec((1, 8, D), lambda i: (jnp.minimum(i, 1), 0, 0))
    return pl.pallas_call(
        body, name=name, grid=(t_all // TILE,),
        in_specs=[tspec(D), kind, pl.BlockSpec((1, D), lambda i: (0, 0)), once((D, D_FF)), once((D_FF, D)), tspec(D)],
        out_specs=[tspec(D), kind, pl.BlockSpec((1, D), lambda i: (0, 0)), tspec(D_FF), tspec(D), tspec(D), tspec(D_FF)],
        out_shape=[jax.ShapeDtypeStruct((t_all, D), F32), jax.ShapeDtypeStruct((2, 8, D), F32),
                   jax.ShapeDtypeStruct((1, D), F32), jax.ShapeDtypeStruct((t_all, D_FF), BF),
                   jax.ShapeDtypeStruct((t_all, D), BF), jax.ShapeDtypeStruct((t_all, D), BF),
                   jax.ShapeDtypeStruct((t_all, D_FF), BF)],
        compiler_params=_cparams(1),
    )(x1, modl, n2w, w1, w2, dx2)


def wgrad(a, b, name, bk1=1024, bk2=1024):
    t, k1 = a.shape
    k2 = b.shape[1]
    bk1, bk2 = min(bk1, k1), min(bk2, k2)
    nt_ = t // TILE

    def body(a_ref, b_ref, o_ref, acc):
        s = pl.program_id(2)

        @pl.when(s == 0)
        def _():
            acc[...] = jnp.zeros_like(acc)

        acc[...] += _dot_tn(a_ref[...], b_ref[...])

        @pl.when(s == nt_ - 1)
        def _():
            o_ref[...] = acc[...]

    return pl.pallas_call(
        body, name=name, grid=(k1 // bk1, k2 // bk2, nt_),
        in_specs=[pl.BlockSpec((TILE, bk1), lambda i, j, s: (s, i)), pl.BlockSpec((TILE, bk2), lambda i, j, s: (s, j))],
        out_specs=pl.BlockSpec((bk1, bk2), lambda i, j, s: (i, j)),
        out_shape=jax.ShapeDtypeStruct((k1, k2), F32),
        scratch_shapes=[pltpu.VMEM((bk1, bk2), F32)],
        compiler_params=_cparams(3),
    )(a, b)


def _gla_consts(reverse, diff):
    r = lax.broadcasted_iota(jnp.int32, (GCH, GCH), 0)
    cc = lax.broadcasted_iota(jnp.int32, (GCH, GCH), 1)
    low = (r >= cc)
    tri = (jnp.logical_not(low) | (r == cc)) if reverse else low
    tri_f = tri.astype(F32)
    tri_t = (cc >= r) if not reverse else (cc <= r)
    hmk = [_lane_group_mask(128, 32, h) for h in range(HEADS)]
    hmv = [_lane_group_mask(256, 64, h) for h in range(HEADS)]
    e = lax.broadcasted_iota(jnp.int32, (256, 128), 0) // 64
    dk = lax.broadcasted_iota(jnp.int32, (256, 128), 1) // 32
    return dict(cum=make_cum(tri_f.astype(BF), tri_t.astype(F32).astype(BF), diff), ops=_ops(diff),
                tri4=jnp.concatenate([tri_f] * HEADS, axis=0), hmk=hmk, hmv=hmv, bd=(e == dk).astype(F32))


def gla_chunk(st, q, k, v, g, cs):
    nn, nt, tn = cs["ops"]
    b = cs["cum"](g)
    bl = jnp.sum(g, axis=0, keepdims=True)
    qe = q * jnp.exp(b)
    ke = k * jnp.exp(-b)
    qstack = jnp.concatenate([qe * cs["hmk"][h] for h in range(HEADS)], axis=0)
    att = nt(qstack, ke) * cs["tri4"]
    ofull = nn(att, v)
    o = nt(qe, st)
    for h in range(HEADS):
        o = o + ofull[h * GCH:(h + 1) * GCH] * cs["hmv"][h]
    kd = k * jnp.exp(bl - b)
    st_new = st * jnp.exp(bl) + tn(v, kd) * cs["bd"]
    return st_new, o


def _gla_chunk_index(s, n_ch, reverse):
    ctx_ch = TILE // GCH
    if not reverse:
        return s
    return jnp.where(s < ctx_ch, ctx_ch - 1 - s, n_ch - 1 + ctx_ch - s)


def gla_forward(q, k, v, gf, gb, name):
    t = q.shape[0]
    n_ch = t // GCH

    def body(*refs):
        s = pl.program_id(0)
        for dr, reverse in enumerate((False, True)):
            q_ref, k_ref, v_ref, g_ref = refs[4 * dr:4 * dr + 4]
            o_ref, sst_ref = refs[8 + 2 * dr:8 + 2 * dr + 2]
            st = refs[12 + dr]

            @pl.when(s == 0)
            def _(st=st):
                st[...] = jnp.zeros_like(st)

            cur = st[...]
            sst_ref[0] = cur
            st_new, o = gla_chunk(cur, q_ref[...], k_ref[...], v_ref[...], g_ref[...], _gla_consts(reverse, False))
            o_ref[...] = o
            st[...] = st_new

    in_specs, out_specs, out_shape = [], [], []
    for reverse in (False, True):
        im = lambda s, reverse=reverse: (_gla_chunk_index(s, n_ch, reverse), 0)
        im3 = lambda s, reverse=reverse: (_gla_chunk_index(s, n_ch, reverse), 0, 0)
        in_specs += [pl.BlockSpec((GCH, 128), im), pl.BlockSpec((GCH, 128), im), pl.BlockSpec((GCH, 256), im),
                     pl.BlockSpec((GCH, 128), im)]
        out_specs += [pl.BlockSpec((GCH, 256), im), pl.BlockSpec((1, 256, 128), im3)]
        out_shape += [jax.ShapeDtypeStruct((t, 256), F32), jax.ShapeDtypeStruct((n_ch, 256, 128), F32)]
    return pl.pallas_call(
        body, name=name, grid=(n_ch,), in_specs=in_specs, out_specs=out_specs, out_shape=out_shape,
        scratch_shapes=[pltpu.VMEM((256, 128), F32), pltpu.VMEM((256, 128), F32)],
        compiler_params=_cparams(1),
    )(q, k, v, gf, q, k, v, gb)


def gla_backward(q, k, v, gf, gb, sst_f, sst_b, do, name):
    t = q.shape[0]
    n_ch = t // GCH

    def body(*refs):
        r = pl.program_id(0)
        for dr, reverse in enumerate((False, True)):
            q_ref, k_ref, v_ref, g_ref, sst_ref, do_ref = refs[6 * dr:6 * dr + 6]
            outs = refs[12 + 4 * dr:12 + 4 * dr + 4]
            dst = refs[20 + dr]

            @pl.when(r == 0)
            def _(dst=dst):
                dst[...] = jnp.zeros_like(dst)

            cs = _gla_consts(reverse, True)
            _, vjp_fn = jax.vjp(lambda a, b, c_, d_, e_, cs=cs: gla_chunk(a, b, c_, d_, e_, cs),
                                sst_ref[0], q_ref[...], k_ref[...], v_ref[...], g_ref[...])
            grads = vjp_fn((dst[...], do_ref[...]))
            for o_ref, gval in zip(outs, grads[1:]):
                o_ref[...] = gval
            dst[...] = grads[0]

    in_specs, out_specs, out_shape = [], [], []
    for reverse in (False, True):
        im = lambda r, reverse=reverse: (_gla_chunk_index(n_ch - 1 - r, n_ch, reverse), 0)
        im3 = lambda r, reverse=reverse: (_gla_chunk_index(n_ch - 1 - r, n_ch, reverse), 0, 0)
        in_specs += [pl.BlockSpec((GCH, 128), im), pl.BlockSpec((GCH, 128), im), pl.BlockSpec((GCH, 256), im),
                     pl.BlockSpec((GCH, 128), im), pl.BlockSpec((1, 256, 128), im3), pl.BlockSpec((GCH, 256), im)]
        out_specs += [pl.BlockSpec((GCH, 128), im), pl.BlockSpec((GCH, 128), im), pl.BlockSpec((GCH, 256), im),
                      pl.BlockSpec((GCH, 128), im)]
        out_shape += [jax.ShapeDtypeStruct((t, 128), F32), jax.ShapeDtypeStruct((t, 128), F32),
                      jax.ShapeDtypeStruct((t, 256), F32), jax.ShapeDtypeStruct((t, 128), F32)]
    return pl.pallas_call(
        body, name=name, grid=(n_ch,), in_specs=in_specs, out_specs=out_specs, out_shape=out_shape,
        scratch_shapes=[pltpu.VMEM((256, 128), F32), pltpu.VMEM((256, 128), F32)],
        compiler_params=_cparams(1),
    )(q, k, v, gf, sst_f, do, q, k, v, gb, sst_b, do)


def _resident(hbm_ref, vmem_ref, sem):
    cp = pltpu.make_async_copy(hbm_ref, vmem_ref, sem)
    cp.start()
    cp.wait()


def mla_forward(q_cat, k_cat, v, name):
    t = q_cat.shape[0]
    n_t = t // TILE

    n_main = (t - TILE) // KV_CH

    def body(q_ref, k_hbm, v_hbm, o_ref, lse_ref, k_s, v_s, m_s, l_s, acc_s, sem):
        i = pl.program_id(0)

        @pl.when(i == 0)
        def _():
            _resident(k_hbm, k_s, sem.at[0])
            _resident(v_hbm, v_s, sem.at[1])

        m_s[...] = jnp.full(m_s.shape, -1e30, F32)
        l_s[...] = jnp.zeros_like(l_s)
        acc_s[...] = jnp.zeros_like(acc_s)

        def chunk(r0, size):
            for h in range(HEADS):
                kh = k_s[pl.ds(r0, size), h * 256:(h + 1) * 256]
                vh = v_s[pl.ds(r0, size), h * 128:(h + 1) * 128]
                s = _dot_nt(q_ref[:, h * 256:(h + 1) * 256], kh)
                m_prev = m_s[h]
                m_next = jnp.maximum(m_prev, jnp.max(s, axis=-1, keepdims=True))
                p = jnp.exp2(s - jnp.tile(m_next, (1, size // 128)))
                alpha = jnp.exp2(m_prev - m_next)
                l_s[h] = alpha * l_s[h] + jnp.sum(p, axis=-1, keepdims=True)
                acc_s[h] = alpha * acc_s[h] + _dot(p.astype(BF), vh)
                m_s[h] = m_next

        chunk(0, TILE)

        @pl.when(i >= 1)
        def _():
            def step(c, carry):
                chunk(pl.multiple_of(TILE + c * KV_CH, TILE), KV_CH)
                return carry

            lax.fori_loop(0, n_main, step, 0)

        for h in range(HEADS):
            o_ref[:, h * 128:(h + 1) * 128] = acc_s[h] / l_s[h]
            lse_ref[:, h * 128:(h + 1) * 128] = m_s[h] + jnp.log2(l_s[h])

    return pl.pallas_call(
        body, name=name, grid=(n_t,),
        in_specs=[pl.BlockSpec((TILE, 1024), lambda i: (i, 0)), pl.BlockSpec(memory_space=pl.ANY),
                  pl.BlockSpec(memory_space=pl.ANY)],
        out_specs=[pl.BlockSpec((TILE, 512), lambda i: (i, 0)), pl.BlockSpec((TILE, 512), lambda i: (i, 0))],
        out_shape=[jax.ShapeDtypeStruct((t, 512), F32), jax.ShapeDtypeStruct((t, 512), F32)],
        scratch_shapes=[pltpu.VMEM((t, 1024), BF), pltpu.VMEM((t, 512), BF), pltpu.VMEM((HEADS, TILE, 128), F32),
                        pltpu.VMEM((HEADS, TILE, 128), F32), pltpu.VMEM((HEADS, TILE, 128), F32),
                        pltpu.SemaphoreType.DMA((2,))],
        compiler_params=_cparams(1),
    )(q_cat, k_cat, v)


def mla_backward_q(q_cat, k_cat, v, o, lse, do, name):
    t = q_cat.shape[0]
    n_t = t // TILE

    n_main = (t - TILE) // KV_CH

    def body(q_ref, k_hbm, v_hbm, o_ref, lse_ref, do_ref, dq_ref, dl_ref, k_s, v_s, dq_s, do_s, sem):
        i = pl.program_id(0)

        @pl.when(i == 0)
        def _():
            _resident(k_hbm, k_s, sem.at[0])
            _resident(v_hbm, v_s, sem.at[1])

        dq_s[...] = jnp.zeros_like(dq_s)
        do_s[...] = do_ref[...].astype(BF)
        for h in range(HEADS):
            hs = slice(h * 128, (h + 1) * 128)
            dl_ref[:, hs] = jnp.broadcast_to(jnp.sum(do_ref[:, hs] * o_ref[:, hs], axis=-1, keepdims=True), (TILE, 128))

        def chunk(r0, size):
            for h in range(HEADS):
                kh = k_s[pl.ds(r0, size), h * 256:(h + 1) * 256]
                vh = v_s[pl.ds(r0, size), h * 128:(h + 1) * 128]
                s = _dot_nt(q_ref[:, h * 256:(h + 1) * 256], kh)
                p = jnp.exp2(s - jnp.tile(lse_ref[:, h * 128:(h + 1) * 128], (1, size // 128)))
                dp = _dot_nt(do_s[:, h * 128:(h + 1) * 128], vh)
                ds = p * (dp - jnp.tile(dl_ref[:, h * 128:(h + 1) * 128], (1, size // 128)))
                dq_s[:, h * 256:(h + 1) * 256] += _dot(ds.astype(BF), kh)

        chunk(0, TILE)

        @pl.when(i >= 1)
        def _():
            def step(c, carry):
                chunk(pl.multiple_of(TILE + c * KV_CH, TILE), KV_CH)
                return carry

            lax.fori_loop(0, n_main, step, 0)

        dq_ref[...] = dq_s[...] * LN2

    return pl.pallas_call(
        body, name=name, grid=(n_t,),
        in_specs=[pl.BlockSpec((TILE, 1024), lambda i: (i, 0)), pl.BlockSpec(memory_space=pl.ANY),
                  pl.BlockSpec(memory_space=pl.ANY), pl.BlockSpec((TILE, 512), lambda i: (i, 0)),
                  pl.BlockSpec((TILE, 512), lambda i: (i, 0)), pl.BlockSpec((TILE, 512), lambda i: (i, 0))],
        out_specs=[pl.BlockSpec((TILE, 1024), lambda i: (i, 0)), pl.BlockSpec((TILE, 512), lambda i: (i, 0))],
        out_shape=[jax.ShapeDtypeStruct((t, 1024), F32), jax.ShapeDtypeStruct((t, 512), F32)],
        scratch_shapes=[pltpu.VMEM((t, 1024), BF), pltpu.VMEM((t, 512), BF), pltpu.VMEM((TILE, 1024), F32),
                        pltpu.VMEM((TILE, 512), BF), pltpu.SemaphoreType.DMA((2,))],
        compiler_params=_cparams(1),
    )(q_cat, k_cat, v, o, lse, do)


def mla_backward_kv(q_cat, k_cat, v, lse, delta, do_bf, name):
    t = q_cat.shape[0]
    n_t = t // TILE

    n_main = (t - TILE) // KV_CH

    def body(q_hbm, k_ref, v_ref, lse_ref, dl_ref, do_hbm, dk_ref, dv_ref, q_s, do_s, dk_s, dv_s, sem):
        j = pl.program_id(0)

        @pl.when(j == 0)
        def _():
            _resident(q_hbm, q_s, sem.at[0])
            _resident(do_hbm, do_s, sem.at[1])

        dk_s[...] = jnp.zeros_like(dk_s)
        dv_s[...] = jnp.zeros_like(dv_s)

        def chunk(r0, size):
            for h in range(HEADS):
                qh = q_s[pl.ds(r0, size), h * 256:(h + 1) * 256]
                doh = do_s[pl.ds(r0, size), h * 128:(h + 1) * 128]
                st = _dot_nt(k_ref[:, h * 256:(h + 1) * 256], qh)
                pt = jnp.exp2(st - lse_ref[h:h + 1, pl.ds(r0, size)])
                dpt = _dot_nt(v_ref[:, h * 128:(h + 1) * 128], doh)
                dst = pt * (dpt - dl_ref[h:h + 1, pl.ds(r0, size)])
                dv_s[:, h * 128:(h + 1) * 128] += _dot(pt.astype(BF), doh)
                dk_s[:, h * 256:(h + 1) * 256] += _dot(dst.astype(BF), qh)

        @pl.when(j == 0)
        def _():
            chunk(0, TILE)

        def step(c, carry):
            chunk(pl.multiple_of(TILE + c * KV_CH, TILE), KV_CH)
            return carry

        lax.fori_loop(0, n_main, step, 0)
        dk_ref[...] = dk_s[...] * LN2
        dv_ref[...] = dv_s[...]

    kspec = lambda w: pl.BlockSpec((TILE, w), lambda j: (j, 0))
    rows = pl.BlockSpec((8, t), lambda j: (0, 0))
    return pl.pallas_call(
        body, name=name, grid=(n_t,),
        in_specs=[pl.BlockSpec(memory_space=pl.ANY), kspec(1024), kspec(512), rows, rows,
                  pl.BlockSpec(memory_space=pl.ANY)],
        out_specs=[kspec(1024), kspec(512)],
        out_shape=[jax.ShapeDtypeStruct((t, 1024), F32), jax.ShapeDtypeStruct((t, 512), F32)],
        scratch_shapes=[pltpu.VMEM((t, 1024), BF), pltpu.VMEM((t, 512), BF), pltpu.VMEM((TILE, 1024), F32),
                        pltpu.VMEM((TILE, 512), F32), pltpu.SemaphoreType.DMA((2,))],
        compiler_params=_cparams(1),
    )(q_cat, k_cat, v, lse, delta, do_bf)


def final_loss(xf, target, fnw, name):
    t = xf.shape[0]
    n_t = t // TILE

    def body(x_ref, t_ref, w_ref, loss_ref, dx_ref, dw_ref):
        i = pl.program_id(0)

        @pl.when(i == 0)
        def _():
            loss_ref[...] = jnp.zeros_like(loss_ref)
            dw_ref[...] = jnp.zeros_like(dw_ref)
            dx_ref[...] = jnp.zeros_like(dx_ref)

        @pl.when(i >= 1)
        def _():
            y, vjp_fn = jax.vjp(_rms, x_ref[...], w_ref[...])
            err = y - t_ref[...]
            loss_ref[...] += jnp.broadcast_to(0.5 * jnp.sum(jnp.mean(err * err, axis=-1, keepdims=True)), (8, 128))
            dx, dw = vjp_fn(err * (1.0 / D))
            dx_ref[...] = dx
            dw_ref[...] += dw

    return pl.pallas_call(
        body, name=name, grid=(n_t,),
        in_specs=[pl.BlockSpec((TILE, D), lambda i: (i, 0)), pl.BlockSpec((TILE, D), lambda i: (jnp.maximum(i - 1, 0), 0)),
                  pl.BlockSpec((1, D), lambda i: (0, 0))],
        out_specs=[pl.BlockSpec((8, 128), lambda i: (0, 0)), pl.BlockSpec((TILE, D), lambda i: (i, 0)),
                   pl.BlockSpec((1, D), lambda i: (0, 0))],
        out_shape=[jax.ShapeDtypeStruct((8, 128), F32), jax.ShapeDtypeStruct((t, D), F32),
                   jax.ShapeDtypeStruct((1, D), F32)],
        compiler_params=_cparams(1),
    )(xf, target, fnw)


def exchange(x, name, scatter):
    blk = x.shape[1:] if scatter else x.shape

    def body(x_ref, o_ref, ssem, rsem, lsem):
        xi, yi, ci = lax.axis_index("x"), lax.axis_index("y"), lax.axis_index("c")
        me = 4 * xi + 2 * yi + ci

        def peer(k):
            px = (1 - xi) if (k >> 2) & 1 else xi
            py = (1 - yi) if (k >> 1) & 1 else yi
            pc = (1 - ci) if k & 1 else ci
            return (px, py, pc), 4 * px + 2 * py + pc

        def copy(k):
            dev, pid = peer(k)
            return pltpu.make_async_remote_copy(
                src_ref=x_ref.at[pid] if scatter else x_ref, dst_ref=o_ref.at[me],
                send_sem=ssem.at[k - 1], recv_sem=rsem.at[k - 1], device_id=dev, device_id_type=pl.DeviceIdType.MESH)

        def landing(k):
            dev, pid = peer(k)
            return pltpu.make_async_remote_copy(
                src_ref=x_ref.at[pid] if scatter else x_ref, dst_ref=o_ref.at[pid],
                send_sem=ssem.at[k - 1], recv_sem=rsem.at[k - 1], device_id=dev, device_id_type=pl.DeviceIdType.MESH)

        own = pltpu.make_async_copy(x_ref.at[me] if scatter else x_ref, o_ref.at[me], lsem)
        own.start()
        sends = [copy(k) for k in range(1, N_DEV)]
        for cp in sends:
            cp.start()
        for k in range(1, N_DEV):
            landing(k).wait_recv()
        for cp in sends:
            cp.wait_send()
        own.wait()

    return pl.pallas_call(
        body, name=name,
        in_specs=[pl.BlockSpec(memory_space=pl.ANY)], out_specs=pl.BlockSpec(memory_space=pl.ANY),
        out_shape=jax.ShapeDtypeStruct((N_DEV,) + tuple(blk), x.dtype),
        scratch_shapes=[pltpu.SemaphoreType.DMA((N_DEV - 1,)), pltpu.SemaphoreType.DMA((N_DEV - 1,)),
                        pltpu.SemaphoreType.DMA(())],
        compiler_params=pltpu.CompilerParams(has_side_effects=True),
    )(x)


def mod_forward(crows, w_mod, b_shard, name):
    cols = w_mod.shape[2]

    def body(c_ref, w_ref, b_ref, o_ref):
        o_ref[0] = _dot(_silu(c_ref[...]).astype(BF), w_ref[0].astype(BF)) + b_ref[0]

    return pl.pallas_call(
        body, name=name, grid=(2,),
        in_specs=[pl.BlockSpec((16, D), lambda l: (0, 0)), pl.BlockSpec((1, D, cols), lambda l: (l, 0, 0)),
                  pl.BlockSpec((1, 1, cols), lambda l: (l, 0, 0))],
        out_specs=pl.BlockSpec((1, 16, cols), lambda l: (l, 0, 0)),
        out_shape=jax.ShapeDtypeStruct((2, 16, cols), F32), compiler_params=_cparams(1),
    )(crows, w_mod, b_shard)


def mod_backward(crows, w_mod, d_own, d_ctx, name):
    cols = w_mod.shape[2]

    def body(c_ref, w_ref, do_ref, dc_ref, gw_ref, gs_ref):
        dc = dc_ref[0]
        dsum = dc[0:1]
        for s in range(1, N_DEV):
            dsum = dsum + dc[s:s + 1]
        row = lax.broadcasted_iota(jnp.int32, (8, cols), 0)
        d16 = jnp.concatenate([do_ref[0], jnp.where(row == 0, jnp.broadcast_to(dsum, (8, cols)), 0.0)], axis=0)
        gw_ref[0] = _dot_tn(_silu(c_ref[...]).astype(BF), d16.astype(BF))
        gs_ref[0] = _dot_nt(jnp.broadcast_to(dsum, (8, cols)).astype(BF), w_ref[0].astype(BF))

    return pl.pallas_call(
        body, name=name, grid=(2,),
        in_specs=[pl.BlockSpec((16, D), lambda l: (0, 0)), pl.BlockSpec((1, D, cols), lambda l: (l, 0, 0)),
                  pl.BlockSpec((1, 8, cols), lambda l: (l, 0, 0)), pl.BlockSpec((1, 8, cols), lambda l: (l, 0, 0))],
        out_specs=[pl.BlockSpec((1, D, cols), lambda l: (l, 0, 0)), pl.BlockSpec((1, 8, D), lambda l: (l, 0, 0))],
        out_shape=[jax.ShapeDtypeStruct((2, D, cols), F32), jax.ShapeDtypeStruct((2, 8, D), F32)],
        compiler_params=_cparams(1),
    )(crows, w_mod, d_own, d_ctx)


def silu_grad_scale(c_ctx, ds, name):
    def body(c_ref, ds_ref, o_ref):
        cc = c_ref[...]
        sg = jax.nn.sigmoid(cc)
        o_ref[...] = (ds_ref[0][0:1] + ds_ref[1][0:1]) * (sg * (1.0 + cc * (1.0 - sg)))

    return pl.pallas_call(body, name=name, out_shape=jax.ShapeDtypeStruct((1, D), F32))(c_ctx, ds)


def adamw(parts, w, m, v, name, block_rows):
    n_parts, rows, _ = parts.shape

    def body(p_ref, w_ref, m_ref, v_ref, g_ref, d_ref, nm_ref, nv_ref):
        g = p_ref[0].astype(F32)
        for s in range(1, n_parts):
            g = g + p_ref[s].astype(F32)
        mm = ADAM_B1 * m_ref[...] + (1.0 - ADAM_B1) * g
        vv = ADAM_B2 * v_ref[...] + (1.0 - ADAM_B2) * (g * g)
        m_hat = mm / (1.0 - ADAM_B1 ** ADAM_STEP)
        v_hat = vv / (1.0 - ADAM_B2 ** ADAM_STEP)
        g_ref[...] = g
        d_ref[...] = -ADAM_LR * (m_hat / (jnp.sqrt(v_hat) + ADAM_EPS) + ADAM_WD * w_ref[...])
        nm_ref[...] = mm
        nv_ref[...] = vv

    spec = pl.BlockSpec((block_rows, 128), lambda i: (i, 0))
    return pl.pallas_call(
        body, name=name, grid=(rows // block_rows,),
        in_specs=[pl.BlockSpec((n_parts, block_rows, 128), lambda i: (0, i, 0)), spec, spec, spec],
        out_specs=[spec] * 4, out_shape=[jax.ShapeDtypeStruct((rows, 128), F32)] * 4,
        compiler_params=_cparams(1),
    )(parts, w, m, v)


def _pad_cols(w, segs, total):
    parts, pos = [], 0
    for dst, src, wd in segs:
        if dst > pos:
            parts.append(jnp.zeros(w.shape[:-1] + (dst - pos,), w.dtype))
        parts.append(w[..., src:src + wd])
        pos = dst + wd
    if pos < total:
        parts.append(jnp.zeros(w.shape[:-1] + (total - pos,), w.dtype))
    return jnp.concatenate(parts, axis=-1)


def _unpad_cols(g, segs):
    return jnp.concatenate([g[..., dst:dst + wd] for dst, _, wd in segs], axis=-1)


def _flat128(a, rows_multiple=8):
    f = a.reshape(-1)
    n = f.shape[0]
    per = 128 * rows_multiple
    pad = (-n) % per
    if pad:
        f = jnp.concatenate([f, jnp.zeros((pad,), f.dtype)])
    return f.reshape(-1, 128)


def _rope_tables(n_lat):
    rows = n_lat // GRID_W
    row = jnp.repeat(jnp.arange(rows), GRID_W).astype(F32)
    col = jnp.tile(jnp.arange(GRID_W), rows).astype(F32)
    freq = ROPE_BASE ** (-jnp.arange(16, dtype=F32) * 2.0 / 32)
    ar, ac = row[:, None] * freq[None, :], col[:, None] * freq[None, :]
    z = jnp.zeros((n_lat, 16), F32)
    cos = jnp.concatenate([jnp.cos(ar), jnp.cos(ar), jnp.cos(ac), jnp.cos(ac), jnp.ones((n_lat, 64), F32)], axis=1)
    sa = jnp.concatenate([-jnp.sin(ar), z, -jnp.sin(ac), z, jnp.zeros((n_lat, 64), F32)], axis=1)
    sb = jnp.concatenate([z, jnp.sin(ar), z, jnp.sin(ac), jnp.zeros((n_lat, 64), F32)], axis=1)
    ident = lambda fill: jnp.full((TILE, 128), fill, F32)
    return (jnp.concatenate([ident(1.0), cos]), jnp.concatenate([ident(0.0), sa]), jnp.concatenate([ident(0.0), sb]))


def _big_shard_shapes(w):
    return {k: w[k].shape for k in BIG_NAMES}


def _gathered_to_full(g, name):
    if name in ("w_out", "w_ff2"):
        return jnp.transpose(g, (1, 0, 2, 3)).reshape(2, -1, g.shape[-1])
    return jnp.transpose(g, (1, 2, 0, 3)).reshape(2, g.shape[2], -1)


def _full_to_chunks(gw, name):
    if name in ("w_out", "w_ff2"):
        return jnp.transpose(gw.reshape(2, N_DEV, -1, gw.shape[-1]), (1, 0, 2, 3))
    return jnp.transpose(gw.reshape(2, gw.shape[1], N_DEV, -1), (2, 0, 1, 3))


def kernel(x, c, ctx, c_ctx, w_mod, b_mod, norm1_w, w_in, w_out, sgu_norm_w, sgu_norm_b, sgu_w, sgu_b, gla_wg_fwd, gla_bg_fwd, gla_wg_bwd, gla_bg_bwd, gla_norm_w, mla_q_norm_w, mla_w_uq, mla_kv_norm_w, mla_w_ukv, norm2_w, w_ff1, w_ff2, final_norm_w, loss_target, m_c_ctx, m_w_mod, m_b_mod, m_norm1_w, m_w_in, m_w_out, m_sgu_norm_w, m_sgu_norm_b, m_sgu_w, m_sgu_b, m_gla_wg_fwd, m_gla_bg_fwd, m_gla_wg_bwd, m_gla_bg_bwd, m_gla_norm_w, m_mla_q_norm_w, m_mla_w_uq, m_mla_kv_norm_w, m_mla_w_ukv, m_norm2_w, m_w_ff1, m_w_ff2, m_final_norm_w, v_c_ctx, v_w_mod, v_b_mod, v_norm1_w, v_w_in, v_w_out, v_sgu_norm_w, v_sgu_norm_b, v_sgu_w, v_sgu_b, v_gla_wg_fwd, v_gla_bg_fwd, v_gla_wg_bwd, v_gla_bg_bwd, v_gla_norm_w, v_mla_q_norm_w, v_mla_w_uq, v_mla_kv_norm_w, v_mla_w_ukv, v_norm2_w, v_w_ff1, v_w_ff2, v_final_norm_w):
    W = dict(c_ctx=c_ctx, w_mod=w_mod, b_mod=b_mod, norm1_w=norm1_w, w_in=w_in, w_out=w_out, sgu_norm_w=sgu_norm_w,
             sgu_norm_b=sgu_norm_b, sgu_w=sgu_w, sgu_b=sgu_b, gla_wg_fwd=gla_wg_fwd, gla_bg_fwd=gla_bg_fwd,
             gla_wg_bwd=gla_wg_bwd, gla_bg_bwd=gla_bg_bwd, gla_norm_w=gla_norm_w, mla_q_norm_w=mla_q_norm_w,
             mla_w_uq=mla_w_uq, mla_kv_norm_w=mla_kv_norm_w, mla_w_ukv=mla_w_ukv, norm2_w=norm2_w, w_ff1=w_ff1,
             w_ff2=w_ff2, final_norm_w=final_norm_w)
    M = dict(c_ctx=m_c_ctx, w_mod=m_w_mod, b_mod=m_b_mod, norm1_w=m_norm1_w, w_in=m_w_in, w_out=m_w_out,
             sgu_norm_w=m_sgu_norm_w, sgu_norm_b=m_sgu_norm_b, sgu_w=m_sgu_w, sgu_b=m_sgu_b, gla_wg_fwd=m_gla_wg_fwd,
             gla_bg_fwd=m_gla_bg_fwd, gla_wg_bwd=m_gla_wg_bwd, gla_bg_bwd=m_gla_bg_bwd, gla_norm_w=m_gla_norm_w,
             mla_q_norm_w=m_mla_q_norm_w, mla_w_uq=m_mla_w_uq, mla_kv_norm_w=m_mla_kv_norm_w, mla_w_ukv=m_mla_w_ukv,
             norm2_w=m_norm2_w, w_ff1=m_w_ff1, w_ff2=m_w_ff2, final_norm_w=m_final_norm_w)
    V = dict(c_ctx=v_c_ctx, w_mod=v_w_mod, b_mod=v_b_mod, norm1_w=v_norm1_w, w_in=v_w_in, w_out=v_w_out,
             sgu_norm_w=v_sgu_norm_w, sgu_norm_b=v_sgu_norm_b, sgu_w=v_sgu_w, sgu_b=v_sgu_b, gla_wg_fwd=v_gla_wg_fwd,
             gla_bg_fwd=v_gla_bg_fwd, gla_wg_bwd=v_gla_wg_bwd, gla_bg_bwd=v_gla_bg_bwd, gla_norm_w=v_gla_norm_w,
             mla_q_norm_w=v_mla_q_norm_w, mla_w_uq=v_mla_w_uq, mla_kv_norm_w=v_mla_kv_norm_w, mla_w_ukv=v_mla_w_ukv,
             norm2_w=v_norm2_w, w_ff1=v_w_ff1, w_ff2=v_w_ff2, final_norm_w=v_final_norm_w)

    n_lat = x.shape[1]
    assert ctx.shape[1] == TILE and n_lat % TILE == 0 and x.shape[2] == D
    t_all = TILE + n_lat
    n_t = t_all // TILE
    me = 4 * lax.axis_index("x") + 2 * lax.axis_index("y") + lax.axis_index("c")
    mod_cols = w_mod.shape[2]

    c_all = exchange(c, "ag_c", scatter=False).reshape(N_DEV, D)
    crows = jnp.concatenate([c_all, c_ctx[None, :], jnp.zeros((7, D), F32)], axis=0)
    b_shard = lax.dynamic_slice_in_dim(b_mod, me * mod_cols, mod_cols, axis=1)[:, None, :]
    mod_sh = mod_forward(crows, w_mod, b_shard, "mod_fwd")
    mod_g = exchange(mod_sh.reshape(32, mod_cols), "ag_mod", scatter=False)
    mod_full = jnp.transpose(mod_g.reshape(N_DEV, 2, 16, mod_cols), (1, 2, 0, 3)).reshape(2, 16, 6 * D)
    mod_own = lax.dynamic_index_in_dim(mod_full, me, axis=1, keepdims=False)
    mod_ctx = mod_full[:, 8, :]
    pad2 = jnp.zeros((2, D), F32)
    modl = [jnp.stack([jnp.concatenate([mod_ctx[l].reshape(6, D), pad2]),
                       jnp.concatenate([mod_own[l].reshape(6, D), pad2])]) for l in range(2)]

    shard_shapes = _big_shard_shapes(W)
    pack = jnp.concatenate([W[k].astype(BF).reshape(-1) for k in BIG_NAMES]).reshape(-1, 128)
    gathered = exchange(pack, "ag_weights", scatter=False).reshape(N_DEV, -1)
    full, off = {}, 0
    for k in BIG_NAMES:
        n = int(np.prod(shard_shapes[k]))
        full[k] = _gathered_to_full(gathered[:, off:off + n].reshape((N_DEV,) + shard_shapes[k]), k)
        off += n
    w_in_p = _pad_cols(full["w_in"], W_IN_SEGS, P_COLS)
    w_uq_p = _pad_cols(full["mla_w_uq"], W_UQ_SEGS, 1024).astype(F32)
    w_ukv_f = full["mla_w_ukv"].astype(F32)
    wgf_p = jnp.pad(gla_wg_fwd, ((0, 0), (0, 112), (0, 0)))
    wgb_p = jnp.pad(gla_wg_bwd, ((0, 0), (0, 112), (0, 0)))
    sgu_bx = jnp.repeat(jnp.transpose(sgu_b, (0, 2, 1)), 64, axis=2)
    gnw_t = jnp.tile(gla_norm_w, (1, HEADS))
    rc, rsa, rsb = _rope_tables(n_lat)

    xin = jnp.concatenate([ctx[0], x[0]], axis=0)
    row = lambda a: a[None, :]

    def pre_ins(l, xl):
        return [("x", "tile", True, xl), ("mod", "kind", True, modl[l]), ("n1w", "full", True, row(norm1_w[l])),
                ("w_in", "wfull", False, w_in_p[l]), ("sgu_nw", "full", True, row(sgu_norm_w[l])),
                ("sgu_nb", "full", True, row(sgu_norm_b[l])), ("sgu_w", "full", True, sgu_w[l]),
                ("sgu_bx", "full", True, sgu_bx[l]), ("wgf", "full", True, wgf_p[l]), ("bgf", "full", True, row(gla_bg_fwd[l])),
                ("wgb", "full", True, wgb_p[l]), ("bgb", "full", True, row(gla_bg_bwd[l])),
                ("qnw", "full", True, row(mla_q_norm_w[l])), ("w_uq", "full", True, w_uq_p[l]),
                ("kvnw", "full", True, row(mla_kv_norm_w[l])), ("w_ukv", "full", True, w_ukv_f[l]),
                ("rc", "tile", False, rc), ("rsa", "tile", False, rsa), ("rsb", "tile", False, rsb)]

    pre_outs = [("y_sgu", 256, F32), ("qg", 128, F32), ("kg", 128, F32), ("vg", 256, F32), ("lgf", 128, F32),
                ("lgb", 128, F32), ("gr", 256, F32), ("q_cat", 1024, BF), ("k_cat", 1024, BF), ("v", 512, BF)]

    def out_ins(l, xl, a):
        return [("x", "tile", True, xl), ("mod", "kind", True, modl[l]), ("y_sgu", "tile", True, a["y_sgu"]),
                ("o_f", "tile", True, a["o_f"]), ("o_b", "tile", True, a["o_b"]), ("gr", "tile", True, a["gr"]),
                ("y_mla", "tile", True, a["y_mla"]), ("gnw", "full", True, row(gnw_t[l])),
                ("w_out", "wfull", False, full["w_out"][l])]

    def ffn_ins(l, x1):
        return [("x1", "tile", True, x1), ("mod", "kind", True, modl[l]), ("n2w", "full", True, row(norm2_w[l])),
                ("w_ff1", "wcols", False, full["w_ff1"][l]), ("w_ff2", "wrows", False, full["w_ff2"][l])]

    saved, xl = [], xin
    for l in range(2):
        a = tile_forward(pre_tile, f"pre_fwd{l}", t_all, pre_ins(l, xl), pre_outs)
        a["o_f"], a["sf"], a["o_b"], a["sb"] = gla_forward(a["qg"], a["kg"], a["vg"], a["lgf"], a["lgb"], f"gla_fwd{l}")
        a["y_mla"], a["lse"] = mla_forward(a["q_cat"], a["k_cat"], a["v"], f"mla_fwd{l}")
        a["x"] = xl
        a["x1"] = tile_forward(attn_out_tile, f"out_fwd{l}", t_all, out_ins(l, xl, a), [("x1", D, F32)])["x1"]
        xl = tile_forward(ffn_tile, f"ffn_fwd{l}", t_all, ffn_ins(l, a["x1"]), [("x2", D, F32)])["x2"]
        saved.append(a)

    loss_blk, dx, d_fnw = final_loss(xl, loss_target[0], row(final_norm_w), "final_loss")
    loss = lax.psum(loss_blk[0, 0], AXES)

    G = {}
    dmods = []
    for l in (1, 0):
        a = saved[l]
        dx1, dmod3, dn2w, zpre, zf, a_ff1, a_ff2 = ffn_backward(
            a["x1"], modl[l], row(norm2_w[l]), full["w_ff1"][l], full["w_ff2"][l], dx, f"ffn_bwd{l}")
        gw_ff1 = wgrad(a_ff1, zpre, f"wg_ff1_{l}")
        gw_ff2 = wgrad(a_ff2, zf, f"wg_ff2_{l}")
        g2, e2 = tile_backward(attn_out_tile, f"out_bwd{l}", t_all, out_ins(l, a["x"], a), [("x1", dx1)],
                               [("zt", D)], [("a_out", D)])
        gw_out = wgrad(e2["a_out"], e2["zt"], f"wg_out_{l}")
        dq_cat, delta = mla_backward_q(a["q_cat"], a["k_cat"], a["v"], a["y_mla"], a["lse"], g2["y_mla"], f"mla_bwd_q{l}")
        head_rows = lambda r: jnp.pad(jnp.transpose(r[:, ::128]), ((0, 8 - HEADS), (0, 0)))
        dk_cat, dv = mla_backward_kv(a["q_cat"], a["k_cat"], a["v"], head_rows(a["lse"]), head_rows(delta),
                                     g2["y_mla"].astype(BF), f"mla_bwd_kv{l}")
        dqf, dkf, dvf, dgf, dqb, dkb, dvb, dgb = gla_backward(
            a["qg"], a["kg"], a["vg"], a["lgf"], a["lgb"], a["sf"], a["sb"], g2["o_f"], f"gla_bwd{l}")
        cots = [("y_sgu", g2["y_sgu"]), ("qg", [dqf, dqb]), ("kg", [dkf, dkb]), ("vg", [dvf, dvb]), ("lgf", dgf),
                ("lgb", dgb), ("gr", g2["gr"]), ("q_cat", dq_cat), ("k_cat", dk_cat), ("v", dv)]
        g1, e1 = tile_backward(pre_tile, f"pre_bwd{l}", t_all, pre_ins(l, a["x"]), cots, [("zp", P_COLS)], [("a_in", D)],
                               resid=("x", g2["x"]))
        gw_in = _unpad_cols(wgrad(e1["a_in"], e1["zp"], f"wg_in_{l}", bk2=P_COLS), W_IN_SEGS)
        dx = g1["x"]
        dmods.append(g1["mod"] + g2["mod"] + dmod3)
        G[l] = dict(w_in=gw_in, w_out=gw_out, w_ff1=gw_ff1, w_ff2=gw_ff2,
                    mla_w_uq=_unpad_cols(g1["w_uq"], W_UQ_SEGS), mla_w_ukv=g1["w_ukv"],
                    norm1_w=g1["n1w"][0], norm2_w=dn2w[0], sgu_norm_w=g1["sgu_nw"][0], sgu_norm_b=g1["sgu_nb"][0],
                    sgu_w=g1["sgu_w"], sgu_b=jnp.transpose(g1["sgu_bx"].reshape(128, HEADS, 64).sum(-1)),
                    gla_wg_fwd=g1["wgf"][:16], gla_bg_fwd=g1["bgf"][0], gla_wg_bwd=g1["wgb"][:16], gla_bg_bwd=g1["bgb"][0],
                    gla_norm_w=g2["gnw"][0].reshape(HEADS, 64).sum(0), mla_q_norm_w=g1["qnw"][0], mla_kv_norm_w=g1["kvnw"][0])
    dmods = dmods[::-1]
    grad_x = dx[TILE:][None]

    dmod_pack = jnp.stack([jnp.stack([dmods[l][1, :6].reshape(-1), dmods[l][0, :6].reshape(-1)]) for l in range(2)])
    dmod_all = exchange(dmod_pack.reshape(4, 6 * D), "ag_dmod", scatter=False).reshape(N_DEV, 2, 2, 6 * D)
    dsl = lax.dynamic_slice_in_dim(dmod_all, me * mod_cols, mod_cols, axis=3)
    d_own = jnp.transpose(dsl[:, :, 0, :], (1, 0, 2))
    d_ctx = jnp.transpose(dsl[:, :, 1, :], (1, 0, 2))
    g_w_mod, ds_cc = mod_backward(crows, w_mod, d_own, d_ctx, "mod_bwd")
    g_c_ctx_part = silu_grad_scale(c_ctx[None, :], ds_cc, "silu_bwd")[0]
    g_b_mod_part = jnp.stack([dmods[l][1, :6].reshape(-1) + dmods[l][0, :6].reshape(-1) for l in range(2)])

    small_g = dict(c_ctx=g_c_ctx_part, b_mod=g_b_mod_part, final_norm_w=d_fnw[0])
    for k in SMALL_NAMES:
        if k not in small_g:
            small_g[k] = jnp.stack([G[0][k], G[1][k]])
    spack = lambda dct: _flat128(jnp.concatenate([dct[k].reshape(-1) for k in SMALL_NAMES]))
    sparts = exchange(spack(small_g), "ag_small", scatter=False)
    s_rows = sparts.shape[1]
    s_out = adamw(sparts, spack(W), spack(M), spack(V), "adamw_small", s_rows)

    big_chunks = jnp.concatenate(
        [_full_to_chunks(jnp.stack([G[0][k], G[1][k]]), k).reshape(N_DEV, -1) for k in BIG_NAMES], axis=1)
    b_rows = big_chunks.shape[1] // 128
    bparts = exchange(big_chunks.astype(BF).reshape(N_DEV, b_rows, 128), "a2a_grads", scatter=True)
    bpack = lambda dct: jnp.concatenate([dct[k].reshape(-1) for k in BIG_NAMES]).reshape(b_rows, 128)
    blk = b_rows
    for cand in range(2048, 7, -8):
        if b_rows % cand == 0:
            blk = cand
            break
    b_out = adamw(bparts, bpack(W), bpack(M), bpack(V), "adamw_big", blk)

    wm_rows = w_mod.size // 128
    wm = lambda a: a.reshape(wm_rows, 128)
    m_out = adamw(wm(g_w_mod)[None], wm(w_mod), wm(m_w_mod), wm(v_w_mod), "adamw_mod", 1024)

    res = {}
    off = 0
    for k in SMALL_NAMES:
        n = W[k].size
        res[k] = [o.reshape(-1)[off:off + n].reshape(W[k].shape) for o in s_out]
        off += n
    off = 0
    for k in BIG_NAMES:
        n = W[k].size
        res[k] = [o.reshape(-1)[off:off + n].reshape(W[k].shape) for o in b_out]
        off += n
    res["w_mod"] = [o.reshape(w_mod.shape) for o in m_out]
    outs = [loss, grad_x]
    for j in range(4):
        outs += [res[k][j] for k in WEIGHT_ORDER]
    return tuple(outs)
```

```python
import jax
import jax.numpy as jnp
from jax import lax
from jax.experimental import pallas as pl
from jax.experimental.pallas import tpu as pltpu

F32 = jnp.float32
BF = jnp.bfloat16

N_DEV = 8
AXES = ("x", "y", "c")
EPS = 1e-6
D = 1024
TILE = 256
GCH = 64
SGU_CHUNK = 128
HEADS = 4
ROPE_BASE = 10000.0
GRID_W = 64
GLA_TAU = 16.0
ATT_SCALE = (128 + 64) ** -0.5
ATT_SCALE_LOG2 = ATT_SCALE * 1.4426950408889634
LN2 = 0.6931471805599453
KV_CH = 512
MLA_UNROLL = 2
D_FF = 4096
FF_CH = 1024

ADAM_LR = 0.001
ADAM_B1 = 0.9
ADAM_B2 = 0.999
ADAM_EPS = 1e-08
ADAM_WD = 0.01
ADAM_STEP = 10

VMEM_LIMIT_MB = 56
ADAMW_BLOCK_ELEMS = 256 * 1024

W_IN_SEGS = ((0, 0, 128), (128, 128, 256), (384, 384, 16), (512, 400, 16), (640, 416, 256), (896, 672, 64),
             (1024, 736, 256), (1280, 992, 256), (1536, 1248, 128), (1664, 1376, 256), (1920, 1632, 256))
P_COLS = 2176
O_GK, O_GV, O_GGF, O_GGB, O_CKV, O_KR, O_SU, O_SV, O_GQ, O_GR, O_DQ = (s[0] for s in W_IN_SEGS)
W_UQ_SEGS = tuple((h * 256, h * 192, 192) for h in range(HEADS))

SMALL_NAMES = ("c_ctx", "b_mod", "norm1_w", "sgu_norm_w", "sgu_norm_b", "sgu_w", "sgu_b", "gla_wg_fwd", "gla_bg_fwd",
               "gla_wg_bwd", "gla_bg_bwd", "gla_norm_w", "mla_q_norm_w", "mla_kv_norm_w", "norm2_w", "final_norm_w")
BIG_NAMES = ("w_in", "w_out", "mla_w_uq", "mla_w_ukv", "w_ff1", "w_ff2")
WEIGHT_ORDER = ("c_ctx", "w_mod", "b_mod", "norm1_w", "w_in", "w_out", "sgu_norm_w", "sgu_norm_b", "sgu_w", "sgu_b",
                "gla_wg_fwd", "gla_bg_fwd", "gla_wg_bwd", "gla_bg_bwd", "gla_norm_w", "mla_q_norm_w", "mla_w_uq",
                "mla_kv_norm_w", "mla_w_ukv", "norm2_w", "w_ff1", "w_ff2", "final_norm_w")


def _cparams(n_axes):
    return pltpu.CompilerParams(dimension_semantics=("arbitrary",) * n_axes,
                                vmem_limit_bytes=VMEM_LIMIT_MB * 1024 * 1024)


def _dot(a, b):
    return jnp.dot(a, b, preferred_element_type=F32)


def _dot_nt(a, b):
    return lax.dot_general(a, b, (((1,), (1,)), ((), ())), preferred_element_type=F32)


def _dot_tn(a, b):
    return lax.dot_general(a, b, (((0,), (0,)), ((), ())), preferred_element_type=F32)


def _nn(a, b):
    return _dot(a.astype(BF), b.astype(BF))


def _nt(a, b):
    return _dot_nt(a.astype(BF), b.astype(BF))


def _tn(a, b):
    return _dot_tn(a.astype(BF), b.astype(BF))


nn_d = jax.custom_vjp(_nn)
nt_d = jax.custom_vjp(_nt)
tn_d = jax.custom_vjp(_tn)
nn_d.defvjp(lambda a, b: (_nn(a, b), (a, b)), lambda r, dy: (_nt(dy, r[1]), _tn(r[0], dy)))
nt_d.defvjp(lambda a, b: (_nt(a, b), (a, b)), lambda r, dy: (_nn(dy, r[1]), _tn(dy, r[0])))
tn_d.defvjp(lambda a, b: (_tn(a, b), (a, b)), lambda r, dy: (_nt(r[1], dy), _nn(r[0], dy)))


def nn_const(w_bf, diff):
    def raw(a):
        return _dot(a.astype(BF), w_bf)

    if not diff:
        return raw
    f = jax.custom_vjp(raw)
    f.defvjp(lambda a: (raw(a), None), lambda _, dy: (_dot_nt(dy.astype(BF), w_bf),))
    return f


def _split3(g):
    hi = g.astype(BF)
    r = g - hi.astype(F32)
    mid = r.astype(BF)
    lo = (r - mid.astype(F32)).astype(BF)
    return hi, mid, lo


def make_cum(tri_bf, tri_t_bf, diff):
    def raw(g, t):
        hi, mid, lo = _split3(g)
        return _dot(t, hi) + _dot(t, mid) + _dot(t, lo)

    def fwd(g):
        return raw(g, tri_bf)

    if not diff:
        return fwd
    cum = jax.custom_vjp(fwd)
    cum.defvjp(lambda g: (fwd(g), None), lambda _, db: (raw(db, tri_t_bf),))
    return cum


def _roll_lanes(x, shift):
    return pltpu.roll(x, shift, 1)


def make_rope(c, sa, sb, diff):
    def raw(x):
        return x * c + _roll_lanes(x, 112) * sa + _roll_lanes(x, 16) * sb

    if not diff:
        return raw
    f = jax.custom_vjp(raw)
    f.defvjp(lambda x: (raw(x), None),
             lambda _, dy: (dy * c + _roll_lanes(dy * sa, 16) + _roll_lanes(dy * sb, 112),))
    return f


def _ops(diff):
    return (nn_d, nt_d, tn_d) if diff else (_nn, _nt, _tn)


def _rms(x, w):
    return x * lax.rsqrt(jnp.mean(x * x, axis=-1, keepdims=True) + EPS) * w


def _gelu(x):
    return 0.5 * x * (1.0 + jnp.tanh(0.7978845608028654 * (x + 0.044715 * (x * x * x))))


def _silu(x):
    return x * jax.nn.sigmoid(x)


def _log_sigmoid(z):
    return jnp.minimum(z, 0.0) - jnp.log(1.0 + jnp.exp(-jnp.abs(z)))


def _lane_group_mask(width, group, h):
    lane = lax.broadcasted_iota(jnp.int32, (1, width), 1)
    return ((lane >= h * group) & (lane < (h + 1) * group)).astype(F32)


def pre_tile(d, c, z):
    nn, _, _ = _ops(z is not None)
    rope = make_rope(c["rc"], c["rsa"], c["rsb"], z is not None)
    mod = d["mod"]
    h = _rms(d["x"], d["n1w"]) * (1.0 + mod[1:2]) + mod[0:1]
    p = nn_const(c["w_in"], z is not None)(h)
    if z is not None:
        p = p + z["zp"]
    gk, gv = p[:, O_GK:O_GK + 128], p[:, O_GV:O_GV + 256]
    ggf, ggb = p[:, O_GGF:O_GGF + 128], p[:, O_GGB:O_GGB + 128]
    ckv, kr = p[:, O_CKV:O_CKV + 256], p[:, O_KR:O_KR + 128]
    su, sv = p[:, O_SU:O_SU + 256], p[:, O_SV:O_SV + 256]
    gq, gr, dq = p[:, O_GQ:O_GQ + 128], p[:, O_GR:O_GR + 256], p[:, O_DQ:O_DQ + 256]

    u = _gelu(su)
    gv_ = _gelu(sv)
    mu = jnp.mean(gv_, axis=-1, keepdims=True)
    cen = gv_ - mu
    vn = cen * lax.rsqrt(jnp.mean(cen * cen, axis=-1, keepdims=True) + EPS) * d["sgu_nw"] + d["sgu_nb"]
    hm = [_lane_group_mask(256, 64, hh) for hh in range(HEADS)]
    rows = []
    for ci in range(vn.shape[0] // SGU_CHUNK):
        vc = vn[ci * SGU_CHUNK:(ci + 1) * SGU_CHUNK]
        s = d["sgu_bx"]
        for hh in range(HEADS):
            s = s + hm[hh] * nn(d["sgu_w"][hh], vc)
        rows.append(s)
    y_sgu = u * jnp.concatenate(rows, axis=0)

    qg = gq * (32 ** -0.5)
    lgf = _log_sigmoid(nn(ggf, d["wgf"]) + d["bgf"]) * (1.0 / GLA_TAU)
    lgb = _log_sigmoid(nn(ggb, d["wgb"]) + d["bgb"]) * (1.0 / GLA_TAU)

    kv = nn(_rms(ckv, d["kvnw"]), d["w_ukv"])
    kr_r = rope(kr)
    q = nn(_rms(dq, d["qnw"]), d["w_uq"])
    qs, ks, vs = [], [], []
    for hh in range(HEADS):
        qs += [q[:, hh * 256:hh * 256 + 128], rope(q[:, hh * 256 + 128:(hh + 1) * 256])]
        ks += [kv[:, hh * 256:hh * 256 + 128], kr_r]
        vs += [kv[:, hh * 256 + 128:(hh + 1) * 256]]
    outs = dict(y_sgu=y_sgu, qg=qg, kg=gk, vg=gv, lgf=lgf, lgb=lgb, gr=gr,
                q_cat=jnp.concatenate(qs, axis=-1) * ATT_SCALE_LOG2, k_cat=jnp.concatenate(ks, axis=-1), v=jnp.concatenate(vs, axis=-1))
    return outs, dict(a_in=h)


def attn_out_tile(d, c, z):
    mod = d["mod"]
    o = d["o_f"] + d["o_b"]
    ms = jnp.zeros_like(o)
    for hh in range(HEADS):
        m_h = _lane_group_mask(256, 64, hh)
        ms = ms + m_h * (jnp.sum(o * o * m_h, axis=-1, keepdims=True) * (1.0 / 64))
    yg = o * lax.rsqrt(ms + EPS) * d["gnw"] * _silu(d["gr"])
    y = jnp.concatenate([d["y_sgu"], yg, d["y_mla"]], axis=-1)
    t = nn_const(c["w_out"], z is not None)(y)
    if z is not None:
        t = t + z["zt"]
    return dict(x1=d["x"] + mod[2:3] * t), dict(a_out=y)


def ffn_tile(d, c, z):
    mod = d["mod"]
    h2 = _rms(d["x1"], d["n2w"]) * (1.0 + mod[4:5]) + mod[3:4]
    f = None
    a2s = []
    for j in range(D_FF // FF_CH):
        pre = nn_const(c["w_ff1"][j], z is not None)(h2)
        if z is not None:
            pre = pre + z["zpre"][:, j * FF_CH:(j + 1) * FF_CH]
        a = jnp.maximum(pre, 0.0)
        a2 = a * a
        a2s.append(a2)
        fj = nn_const(c["w_ff2"][j], z is not None)(a2)
        f = fj if f is None else f + fj
    if z is not None:
        f = f + z["zf"]
    return dict(x2=d["x1"] + mod[5:6] * f), dict(a_ff1=h2, a_ff2=jnp.concatenate(a2s, axis=-1))


def _in_spec(kind, arr, tile):
    if kind == "tile":
        return pl.BlockSpec((tile, arr.shape[1]), lambda i: (i, 0))
    if kind == "kind":
        return pl.BlockSpec((1,) + arr.shape[1:], lambda i: (jnp.where(i < TILE // tile, 0, 1), 0, 0))
    nd = arr.ndim
    if kind in ("wfull", "wcols", "wrows"):
        return pl.BlockSpec(arr.shape, lambda i: (0,) * nd, pipeline_mode=pl.Buffered(1))
    return pl.BlockSpec(arr.shape, lambda i: (0,) * nd)


def _load(kind, ref):
    if kind == "kind":
        return ref[0]
    if kind == "wcols":
        return [ref[:, j * FF_CH:(j + 1) * FF_CH] for j in range(ref.shape[1] // FF_CH)]
    if kind == "wrows":
        return [ref[j * FF_CH:(j + 1) * FF_CH, :] for j in range(ref.shape[0] // FF_CH)]
    return ref[...]


def tile_forward(fn, name, t_all, ins, out_defs, tile=TILE):
    keys = [k for k, _, _, _ in ins]
    kinds = [kd for _, kd, _, _ in ins]
    diffs = [df for _, _, df, _ in ins]
    arrs = [a for _, _, _, a in ins]
    n_in = len(ins)

    def body(*refs):
        vals = [_load(kinds[j], refs[j]) for j in range(n_in)]
        d = {keys[j]: vals[j] for j in range(n_in) if diffs[j]}
        c = {keys[j]: vals[j] for j in range(n_in) if not diffs[j]}
        outs, _ = fn(d, c, None)
        for j, (k, _, dt) in enumerate(out_defs):
            refs[n_in + j][...] = outs[k].astype(dt)

    res = pl.pallas_call(
        body, name=name, grid=(t_all // tile,),
        in_specs=[_in_spec(kinds[j], arrs[j], tile) for j in range(n_in)],
        out_specs=[pl.BlockSpec((tile, w), lambda i: (i, 0)) for _, w, _ in out_defs],
        out_shape=[jax.ShapeDtypeStruct((t_all, w), dt) for _, w, dt in out_defs],
        compiler_params=_cparams(1),
    )(*arrs)
    return {k: r for (k, _, _), r in zip(out_defs, res)}


def tile_backward(fn, name, t_all, ins, cots, z_defs, aux_defs, tile=TILE, resid=None):
    keys = [k for k, _, _, _ in ins]
    kinds = [kd for _, kd, _, _ in ins]
    diffs = [df for _, _, df, _ in ins]
    arrs = [a for _, _, _, a in ins]
    cot_keys, cot_arrs = [], []
    for k, a in cots:
        for one in (a if isinstance(a, (list, tuple)) else [a]):
            cot_keys.append(k)
            cot_arrs.append(one)
    if resid is not None:
        cot_keys.append("resid:" + resid[0])
        cot_arrs.append(resid[1])
    n_in, n_cot = len(ins), len(cot_arrs)
    dkeys = [j for j in range(n_in) if diffs[j]]
    ctx_tiles = TILE // tile

    def body(*refs):
        i = pl.program_id(0)
        vals = [_load(kinds[j], refs[j]) for j in range(n_in)]
        d = {keys[j]: vals[j] for j in dkeys}
        c = {keys[j]: vals[j] for j in range(n_in) if not diffs[j]}
        zs = {k: jnp.zeros((tile, w), F32) for k, w in z_defs}
        outs, vjp_fn, aux = jax.vjp(lambda dd, zz: fn(dd, c, zz), d, zs, has_aux=True)
        ct = {}
        for j, k in enumerate(cot_keys):
            ct[k] = refs[n_in + j][...] + ct[k] if k in ct else refs[n_in + j][...]
        dd, dz = vjp_fn({k: ct[k].astype(outs[k].dtype) for k in outs})
        base = n_in + n_cot
        for n, j in enumerate(dkeys):
            ref, g = refs[base + n], dd[keys[j]]
            if kinds[j] == "tile":
                ref[...] = g + ct["resid:" + keys[j]] if "resid:" + keys[j] in ct else g
            else:
                first = ((i == 0) | (i == ctx_tiles)) if kinds[j] == "kind" else (i == 0)
                gv = g[None] if kinds[j] == "kind" else g

                @pl.when(first)
                def _(ref=ref, gv=gv):
                    ref[...] = gv

                @pl.when(jnp.logical_not(first))
                def _(ref=ref, gv=gv):
                    ref[...] += gv
        base += len(dkeys)
        for n, (k, _) in enumerate(z_defs):
            refs[base + n][...] = dz[k].astype(BF)
        base += len(z_defs)
        for n, (k, _) in enumerate(aux_defs):
            refs[base + n][...] = aux[k].T.astype(BF)

    out_specs, out_shape = [], []
    for j in dkeys:
        out_specs.append(_in_spec(kinds[j], arrs[j], tile))
        out_shape.append(jax.ShapeDtypeStruct(arrs[j].shape, F32))
    for _, w in z_defs:
        out_specs.append(pl.BlockSpec((tile, w), lambda i: (i, 0)))
        out_shape.append(jax.ShapeDtypeStruct((t_all, w), BF))
    for _, w in aux_defs:
        out_specs.append(pl.BlockSpec((w, tile), lambda i: (0, i)))
        out_shape.append(jax.ShapeDtypeStruct((w, t_all), BF))
    res = pl.pallas_call(
        body, name=name, grid=(t_all // tile,),
        in_specs=[_in_spec(kinds[j], arrs[j], tile) for j in range(n_in)]
        + [pl.BlockSpec((tile, a.shape[1]), lambda i: (i, 0)) for a in cot_arrs],
        out_specs=out_specs, out_shape=out_shape, compiler_params=_cparams(1),
    )(*arrs, *cot_arrs)
    grads = {keys[j]: res[n] for n, j in enumerate(dkeys)}
    extra = {k: res[len(dkeys) + n] for n, (k, _) in enumerate(list(z_defs) + list(aux_defs))}
    return grads, extra


def ffn_backward(x1, modl, n2w, w1, w2, dx2, name):
    t_all = x1.shape[0]
    n_ch = D_FF // FF_CH

    def head(x, mod, nw):
        return _rms(x, nw) * (1.0 + mod[4:5]) + mod[3:4]

    def body(x_ref, mod_ref, nw_ref, w1_ref, w2_ref, dx2_ref, dx1_ref, dmod_ref, dnw_ref, zpre_ref, zf_ref, a1_ref, a2_ref):
        i = pl.program_id(0)
        mod = mod_ref[0]
        dx2 = dx2_ref[...]
        h2, vjp_head = jax.vjp(head, x_ref[...], mod, nw_ref[...])
        h2b = h2.astype(BF)
        dfb = (dx2 * mod[5:6]).astype(BF)
        f = jnp.zeros((TILE, D), F32)
        dh2 = jnp.zeros((TILE, D), F32)
        for j in range(n_ch):
            cs = slice(j * FF_CH, (j + 1) * FF_CH)
            a = jnp.maximum(_dot(h2b, w1_ref[:, cs]), 0.0)
            a2b = (a * a).astype(BF)
            f = f + _dot(a2b, w2_ref[cs, :])
            dpre = (_dot_nt(dfb, w2_ref[cs, :]) * (2.0 * a)).astype(BF)
            dh2 = dh2 + _dot_nt(dpre, w1_ref[:, cs])
            zpre_ref[:, cs] = dpre
            a2_ref[:, cs] = a2b
        zf_ref[...] = (dx2 * mod[5:6]).T.astype(BF)
        a1_ref[...] = h2.T.astype(BF)
        dx1, dmod, dnw = vjp_head(dh2)
        dx1_ref[...] = dx2 + dx1
        row = lax.broadcasted_iota(jnp.int32, (8, D), 0)
        dmod = dmod + jnp.where(row == 5, jnp.sum(dx2 * f, axis=0, keepdims=True), 0.0)
        first_kind = (i == 0) | (i == 1)

        @pl.when(first_kind)
        def _():
            dmod_ref[0] = dmod

        @pl.when(jnp.logical_not(first_kind))
        def _():
            dmod_ref[0] += dmod

        @pl.when(i == 0)
        def _():
            dnw_ref[...] = dnw

        @pl.when(i > 0)
        def _():
            dnw_ref[...] += dnw

    tspec = lambda w: pl.BlockSpec((TILE, w), lambda i: (i, 0))
    once = lambda shp: pl.BlockSpec(shp, lambda i: (0, 0), pipeline_mode=pl.Buffered(1))
    kind = pl.BlockSpec((1, 8, D), lambda i: (jnp.minimum(i, 1), 0, 0))
    tr = pl.BlockSpec((D, TILE), lambda i: (0, i))
    return pl.pallas_call(
        body, name=name, grid=(t_all // TILE,),
        in_specs=[tspec(D), kind, pl.BlockSpec((1, D), lambda i: (0, 0)), once((D, D_FF)), once((D_FF, D)), tspec(D)],
        out_specs=[tspec(D), kind, pl.BlockSpec((1, D), lambda i: (0, 0)), tspec(D_FF), tr, tr, tspec(D_FF)],
        out_shape=[jax.ShapeDtypeStruct((t_all, D), F32), jax.ShapeDtypeStruct((2, 8, D), F32),
                   jax.ShapeDtypeStruct((1, D), F32), jax.ShapeDtypeStruct((t_all, D_FF), BF),
                   jax.ShapeDtypeStruct((D, t_all), BF), jax.ShapeDtypeStruct((D, t_all), BF),
                   jax.ShapeDtypeStruct((t_all, D_FF), BF)],
        compiler_params=_cparams(1),
    )(x1, modl, n2w, w1, w2, dx2)


WG_TOK = 768


def wgrad(at, b, name, bk2=1024):
    k1, t = at.shape
    k2 = b.shape[1]
    bk2 = min(bk2, k2)
    tt = WG_TOK if t % WG_TOK == 0 else TILE
    nt_ = t // tt

    def body(a_ref, b_ref, o_ref, acc):
        s = pl.program_id(1)

        @pl.when(s == 0)
        def _():
            acc[...] = jnp.zeros_like(acc)

        acc[...] += _dot(a_ref[...], b_ref[...])

        @pl.when(s == nt_ - 1)
        def _():
            o_ref[...] = acc[...]

    return pl.pallas_call(
        body, name=name, grid=(k2 // bk2, nt_),
        in_specs=[pl.BlockSpec((k1, tt), lambda j, s: (0, s)), pl.BlockSpec((tt, bk2), lambda j, s: (s, j))],
        out_specs=pl.BlockSpec((k1, bk2), lambda j, s: (0, j)),
        out_shape=jax.ShapeDtypeStruct((k1, k2), F32),
        scratch_shapes=[pltpu.VMEM((k1, bk2), F32)],
        compiler_params=_cparams(2),
    )(at, b)


def _gla_consts(reverse, diff):
    r = lax.broadcasted_iota(jnp.int32, (GCH, GCH), 0)
    cc = lax.broadcasted_iota(jnp.int32, (GCH, GCH), 1)
    low = (r >= cc)
    tri = (jnp.logical_not(low) | (r == cc)) if reverse else low
    tri_f = tri.astype(F32)
    tri_t = (cc >= r) if not reverse else (cc <= r)
    hmk = [_lane_group_mask(128, 32, h) for h in range(HEADS)]
    hmv = [_lane_group_mask(256, 64, h) for h in range(HEADS)]
    e = lax.broadcasted_iota(jnp.int32, (256, 128), 0) // 64
    dk = lax.broadcasted_iota(jnp.int32, (256, 128), 1) // 32
    return dict(cum=make_cum(tri_f.astype(BF), tri_t.astype(F32).astype(BF), diff), ops=_ops(diff),
                tri4=jnp.concatenate([tri_f] * HEADS, axis=0), hmk=hmk, hmv=hmv, bd=(e == dk).astype(F32))


def gla_chunk(st, q, k, v, g, cs):
    nn, nt, tn = cs["ops"]
    b = cs["cum"](g)
    bl = jnp.sum(g, axis=0, keepdims=True)
    qe = q * jnp.exp(b)
    ke = k * jnp.exp(-b)
    qstack = jnp.concatenate([qe * cs["hmk"][h] for h in range(HEADS)], axis=0)
    att = nt(qstack, ke) * cs["tri4"]
    ofull = nn(att, v)
    o = nt(qe, st)
    for h in range(HEADS):
        o = o + ofull[h * GCH:(h + 1) * GCH] * cs["hmv"][h]
    kd = k * jnp.exp(bl - b)
    st_new = st * jnp.exp(bl) + tn(v, kd) * cs["bd"]
    return st_new, o


def _gla_chunk_index(s, n_ch, reverse):
    ctx_ch = TILE // GCH
    if not reverse:
        return s
    return jnp.where(s < ctx_ch, ctx_ch - 1 - s, n_ch - 1 + ctx_ch - s)


def gla_forward(q, k, v, gf, gb, name):
    t = q.shape[0]
    n_ch = t // GCH

    def body(*refs):
        s = pl.program_id(0)
        for dr, reverse in enumerate((False, True)):
            q_ref, k_ref, v_ref, g_ref = refs[4 * dr:4 * dr + 4]
            o_ref, sst_ref = refs[8 + 2 * dr:8 + 2 * dr + 2]
            st = refs[12 + dr]

            @pl.when(s == 0)
            def _(st=st):
                st[...] = jnp.zeros_like(st)

            cur = st[...]
            sst_ref[0] = cur
            st_new, o = gla_chunk(cur, q_ref[...], k_ref[...], v_ref[...], g_ref[...], _gla_consts(reverse, False))
            o_ref[...] = o
            st[...] = st_new

    in_specs, out_specs, out_shape = [], [], []
    for reverse in (False, True):
        im = lambda s, reverse=reverse: (_gla_chunk_index(s, n_ch, reverse), 0)
        im3 = lambda s, reverse=reverse: (_gla_chunk_index(s, n_ch, reverse), 0, 0)
        in_specs += [pl.BlockSpec((GCH, 128), im), pl.BlockSpec((GCH, 128), im), pl.BlockSpec((GCH, 256), im),
                     pl.BlockSpec((GCH, 128), im)]
        out_specs += [pl.BlockSpec((GCH, 256), im), pl.BlockSpec((1, 256, 128), im3)]
        out_shape += [jax.ShapeDtypeStruct((t, 256), F32), jax.ShapeDtypeStruct((n_ch, 256, 128), F32)]
    return pl.pallas_call(
        body, name=name, grid=(n_ch,), in_specs=in_specs, out_specs=out_specs, out_shape=out_shape,
        scratch_shapes=[pltpu.VMEM((256, 128), F32), pltpu.VMEM((256, 128), F32)],
        compiler_params=_cparams(1),
    )(q, k, v, gf, q, k, v, gb)


def gla_backward(q, k, v, gf, gb, sst_f, sst_b, do, name):
    t = q.shape[0]
    n_ch = t // GCH

    def body(*refs):
        r = pl.program_id(0)
        for dr, reverse in enumerate((False, True)):
            q_ref, k_ref, v_ref, g_ref, sst_ref, do_ref = refs[6 * dr:6 * dr + 6]
            outs = refs[12 + 4 * dr:12 + 4 * dr + 4]
            dst = refs[20 + dr]

            @pl.when(r == 0)
            def _(dst=dst):
                dst[...] = jnp.zeros_like(dst)

            cs = _gla_consts(reverse, True)
            _, vjp_fn = jax.vjp(lambda a, b, c_, d_, e_, cs=cs: gla_chunk(a, b, c_, d_, e_, cs),
                                sst_ref[0], q_ref[...], k_ref[...], v_ref[...], g_ref[...])
            grads = vjp_fn((dst[...], do_ref[...]))
            for o_ref, gval in zip(outs, grads[1:]):
                o_ref[...] = gval
            dst[...] = grads[0]

    in_specs, out_specs, out_shape = [], [], []
    for reverse in (False, True):
        im = lambda r, reverse=reverse: (_gla_chunk_index(n_ch - 1 - r, n_ch, reverse), 0)
        im3 = lambda r, reverse=reverse: (_gla_chunk_index(n_ch - 1 - r, n_ch, reverse), 0, 0)
        in_specs += [pl.BlockSpec((GCH, 128), im), pl.BlockSpec((GCH, 128), im), pl.BlockSpec((GCH, 256), im),
                     pl.BlockSpec((GCH, 128), im), pl.BlockSpec((1, 256, 128), im3), pl.BlockSpec((GCH, 256), im)]
        out_specs += [pl.BlockSpec((GCH, 128), im), pl.BlockSpec((GCH, 128), im), pl.BlockSpec((GCH, 256), im),
                      pl.BlockSpec((GCH, 128), im)]
        out_shape += [jax.ShapeDtypeStruct((t, 128), F32), jax.ShapeDtypeStruct((t, 128), F32),
                      jax.ShapeDtypeStruct((t, 256), F32), jax.ShapeDtypeStruct((t, 128), F32)]
    return pl.pallas_call(
        body, name=name, grid=(n_ch,), in_specs=in_specs, out_specs=out_specs, out_shape=out_shape,
        scratch_shapes=[pltpu.VMEM((256, 128), F32), pltpu.VMEM((256, 128), F32)],
        compiler_params=_cparams(1),
    )(q, k, v, gf, sst_f, do, q, k, v, gb, sst_b, do)


def _resident(hbm_ref, vmem_ref, sem):
    cp = pltpu.make_async_copy(hbm_ref, vmem_ref, sem)
    cp.start()
    cp.wait()


def mla_forward(q_cat, k_cat, v, name):
    t = q_cat.shape[0]
    n_t = t // TILE

    n_main = (t - TILE) // KV_CH

    def body(q_ref, k_hbm, v_hbm, o_ref, lse_ref, k_s, v_s, m_s, l_s, acc_s, sem):
        i = pl.program_id(0)

        @pl.when(i == 0)
        def _():
            _resident(k_hbm, k_s, sem.at[0])
            _resident(v_hbm, v_s, sem.at[1])

        m_s[...] = jnp.full(m_s.shape, -1e30, F32)
        l_s[...] = jnp.zeros_like(l_s)
        acc_s[...] = jnp.zeros_like(acc_s)

        def chunk(r0, size):
            for h in range(HEADS):
                kh = k_s[pl.ds(r0, size), h * 256:(h + 1) * 256]
                vh = v_s[pl.ds(r0, size), h * 128:(h + 1) * 128]
                s = _dot_nt(q_ref[:, h * 256:(h + 1) * 256], kh)
                m_prev = m_s[h]
                m_next = jnp.maximum(m_prev, jnp.max(s, axis=-1, keepdims=True))
                p = jnp.exp2(s - jnp.tile(m_next, (1, size // 128)))
                alpha = jnp.exp2(m_prev - m_next)
                l_s[h] = alpha * l_s[h] + jnp.sum(p, axis=-1, keepdims=True)
                acc_s[h] = alpha * acc_s[h] + _dot(p.astype(BF), vh)
                m_s[h] = m_next

        chunk(0, TILE)

        @pl.when(i >= 1)
        def _():
            def step(c, carry):
                chunk(pl.multiple_of(TILE + c * KV_CH, TILE), KV_CH)
                return carry

            lax.fori_loop(0, n_main, step, 0, unroll=MLA_UNROLL)

        for h in range(HEADS):
            o_ref[:, h * 128:(h + 1) * 128] = acc_s[h] / l_s[h]
            lse_ref[:, h * 128:(h + 1) * 128] = m_s[h] + jnp.log2(l_s[h])

    return pl.pallas_call(
        body, name=name, grid=(n_t,),
        in_specs=[pl.BlockSpec((TILE, 1024), lambda i: (i, 0)), pl.BlockSpec(memory_space=pl.ANY),
                  pl.BlockSpec(memory_space=pl.ANY)],
        out_specs=[pl.BlockSpec((TILE, 512), lambda i: (i, 0)), pl.BlockSpec((TILE, 512), lambda i: (i, 0))],
        out_shape=[jax.ShapeDtypeStruct((t, 512), F32), jax.ShapeDtypeStruct((t, 512), F32)],
        scratch_shapes=[pltpu.VMEM((t, 1024), BF), pltpu.VMEM((t, 512), BF), pltpu.VMEM((HEADS, TILE, 128), F32),
                        pltpu.VMEM((HEADS, TILE, 128), F32), pltpu.VMEM((HEADS, TILE, 128), F32),
                        pltpu.SemaphoreType.DMA((2,))],
        compiler_params=_cparams(1),
    )(q_cat, k_cat, v)


def mla_backward_q(q_cat, k_cat, v, o, lse, do, name):
    t = q_cat.shape[0]
    n_t = t // TILE

    n_main = (t - TILE) // KV_CH

    def body(q_ref, k_hbm, v_hbm, o_ref, lse_ref, do_ref, dq_ref, dl_ref, k_s, v_s, dq_s, do_s, sem):
        i = pl.program_id(0)

        @pl.when(i == 0)
        def _():
            _resident(k_hbm, k_s, sem.at[0])
            _resident(v_hbm, v_s, sem.at[1])

        dq_s[...] = jnp.zeros_like(dq_s)
        do_s[...] = do_ref[...].astype(BF)
        for h in range(HEADS):
            hs = slice(h * 128, (h + 1) * 128)
            dl_ref[:, hs] = jnp.broadcast_to(jnp.sum(do_ref[:, hs] * o_ref[:, hs], axis=-1, keepdims=True), (TILE, 128))

        def chunk(r0, size):
            for h in range(HEADS):
                kh = k_s[pl.ds(r0, size), h * 256:(h + 1) * 256]
                vh = v_s[pl.ds(r0, size), h * 128:(h + 1) * 128]
                s = _dot_nt(q_ref[:, h * 256:(h + 1) * 256], kh)
                p = jnp.exp2(s - jnp.tile(lse_ref[:, h * 128:(h + 1) * 128], (1, size // 128)))
                dp = _dot_nt(do_s[:, h * 128:(h + 1) * 128], vh)
                ds = p * (dp - jnp.tile(dl_ref[:, h * 128:(h + 1) * 128], (1, size // 128)))
                dq_s[:, h * 256:(h + 1) * 256] += _dot(ds.astype(BF), kh)

        chunk(0, TILE)

        @pl.when(i >= 1)
        def _():
            def step(c, carry):
                chunk(pl.multiple_of(TILE + c * KV_CH, TILE), KV_CH)
                return carry

            lax.fori_loop(0, n_main, step, 0, unroll=MLA_UNROLL)

        dq_ref[...] = dq_s[...] * LN2

    return pl.pallas_call(
        body, name=name, grid=(n_t,),
        in_specs=[pl.BlockSpec((TILE, 1024), lambda i: (i, 0)), pl.BlockSpec(memory_space=pl.ANY),
                  pl.BlockSpec(memory_space=pl.ANY), pl.BlockSpec((TILE, 512), lambda i: (i, 0)),
                  pl.BlockSpec((TILE, 512), lambda i: (i, 0)), pl.BlockSpec((TILE, 512), lambda i: (i, 0))],
        out_specs=[pl.BlockSpec((TILE, 1024), lambda i: (i, 0)), pl.BlockSpec((TILE, 512), lambda i: (i, 0))],
        out_shape=[jax.ShapeDtypeStruct((t, 1024), F32), jax.ShapeDtypeStruct((t, 512), F32)],
        scratch_shapes=[pltpu.VMEM((t, 1024), BF), pltpu.VMEM((t, 512), BF), pltpu.VMEM((TILE, 1024), F32),
                        pltpu.VMEM((TILE, 512), BF), pltpu.SemaphoreType.DMA((2,))],
        compiler_params=_cparams(1),
    )(q_cat, k_cat, v, o, lse, do)


def mla_backward_kv(q_cat, k_cat, v, lse, delta, do_bf, name):
    t = q_cat.shape[0]
    n_t = t // TILE

    n_main = (t - TILE) // KV_CH

    def body(q_hbm, k_ref, v_ref, lse_ref, dl_ref, do_hbm, dk_ref, dv_ref, q_s, do_s, dk_s, dv_s, sem):
        j = pl.program_id(0)

        @pl.when(j == 0)
        def _():
            _resident(q_hbm, q_s, sem.at[0])
            _resident(do_hbm, do_s, sem.at[1])

        dk_s[...] = jnp.zeros_like(dk_s)
        dv_s[...] = jnp.zeros_like(dv_s)

        def chunk(r0, size):
            for h in range(HEADS):
                qh = q_s[pl.ds(r0, size), h * 256:(h + 1) * 256]
                doh = do_s[pl.ds(r0, size), h * 128:(h + 1) * 128]
                st = _dot_nt(k_ref[:, h * 256:(h + 1) * 256], qh)
                pt = jnp.exp2(st - lse_ref[h:h + 1, pl.ds(r0, size)])
                dpt = _dot_nt(v_ref[:, h * 128:(h + 1) * 128], doh)
                dst = pt * (dpt - dl_ref[h:h + 1, pl.ds(r0, size)])
                dv_s[:, h * 128:(h + 1) * 128] += _dot(pt.astype(BF), doh)
                dk_s[:, h * 256:(h + 1) * 256] += _dot(dst.astype(BF), qh)

        @pl.when(j == 0)
        def _():
            chunk(0, TILE)

        def step(c, carry):
            chunk(pl.multiple_of(TILE + c * KV_CH, TILE), KV_CH)
            return carry

        lax.fori_loop(0, n_main, step, 0, unroll=MLA_UNROLL)
        dk_ref[...] = dk_s[...] * LN2
        dv_ref[...] = dv_s[...]

    kspec = lambda w: pl.BlockSpec((TILE, w), lambda j: (j, 0))
    rows = pl.BlockSpec((8, t), lambda j: (0, 0))
    return pl.pallas_call(
        body, name=name, grid=(n_t,),
        in_specs=[pl.BlockSpec(memory_space=pl.ANY), kspec(1024), kspec(512), rows, rows,
                  pl.BlockSpec(memory_space=pl.ANY)],
        out_specs=[kspec(1024), kspec(512)],
        out_shape=[jax.ShapeDtypeStruct((t, 1024), F32), jax.ShapeDtypeStruct((t, 512), F32)],
        scratch_shapes=[pltpu.VMEM((t, 1024), BF), pltpu.VMEM((t, 512), BF), pltpu.VMEM((TILE, 1024), F32),
                        pltpu.VMEM((TILE, 512), F32), pltpu.SemaphoreType.DMA((2,))],
        compiler_params=_cparams(1),
    )(q_cat, k_cat, v, lse, delta, do_bf)


def final_loss(xf, target, fnw, name):
    t = xf.shape[0]
    n_t = t // TILE

    def body(x_ref, t_ref, w_ref, loss_ref, dx_ref, dw_ref):
        i = pl.program_id(0)

        @pl.when(i == 0)
        def _():
            loss_ref[...] = jnp.zeros_like(loss_ref)
            dw_ref[...] = jnp.zeros_like(dw_ref)
            dx_ref[...] = jnp.zeros_like(dx_ref)

        @pl.when(i >= 1)
        def _():
            y, vjp_fn = jax.vjp(_rms, x_ref[...], w_ref[...])
            err = y - t_ref[...]
            loss_ref[...] += jnp.broadcast_to(0.5 * jnp.sum(jnp.mean(err * err, axis=-1, keepdims=True)), (8, 128))
            dx, dw = vjp_fn(err * (1.0 / D))
            dx_ref[...] = dx
            dw_ref[...] += dw

    return pl.pallas_call(
        body, name=name, grid=(n_t,),
        in_specs=[pl.BlockSpec((TILE, D), lambda i: (i, 0)), pl.BlockSpec((TILE, D), lambda i: (jnp.maximum(i - 1, 0), 0)),
                  pl.BlockSpec((1, D), lambda i: (0, 0))],
        out_specs=[pl.BlockSpec((8, 128), lambda i: (0, 0)), pl.BlockSpec((TILE, D), lambda i: (i, 0)),
                   pl.BlockSpec((1, D), lambda i: (0, 0))],
        out_shape=[jax.ShapeDtypeStruct((8, 128), F32), jax.ShapeDtypeStruct((t, D), F32),
                   jax.ShapeDtypeStruct((1, D), F32)],
        compiler_params=_cparams(1),
    )(xf, target, fnw)


def exchange(xs, name, scatter):
    n = len(xs)
    blks = [tuple(x.shape[1:] if scatter else x.shape) for x in xs]
    per = N_DEV - 1

    def body(*refs):
        x_refs, o_refs = refs[:n], refs[n:2 * n]
        ssem, rsem, lsem = refs[2 * n:]
        xi, yi, ci = lax.axis_index("x"), lax.axis_index("y"), lax.axis_index("c")
        me3 = (xi, yi, ci)
        me = 4 * xi + 2 * yi + ci
        flat = lambda d: 4 * d[0] + 2 * d[1] + d[2]

        if scatter:
            def peer(k):
                return ((1 - xi) if (k >> 2) & 1 else xi, (1 - yi) if (k >> 1) & 1 else yi, (1 - ci) if k & 1 else ci)

            def copy(a, k, lands):
                dev = peer(k)
                return pltpu.make_async_remote_copy(
                    src_ref=x_refs[a].at[flat(dev)], dst_ref=o_refs[a].at[flat(dev) if lands else me],
                    send_sem=ssem.at[a * per + k - 1], recv_sem=rsem.at[a * per + k - 1],
                    device_id=dev, device_id_type=pl.DeviceIdType.MESH)

            own = [pltpu.make_async_copy(x_refs[a].at[me], o_refs[a].at[me], lsem.at[a]) for a in range(n)]
            sends = [copy(a, k, False) for a in range(n) for k in range(1, N_DEV)]
            for cp in own + sends:
                cp.start()
            for a in range(n):
                for k in range(1, N_DEV):
                    copy(a, k, True).wait_recv()
            for cp in sends:
                cp.wait_send()
            for cp in own:
                cp.wait()
            return

        sibling = (xi, yi, 1 - ci)
        chips = [(1 - xi, yi), (xi, 1 - yi), (1 - xi, 1 - yi)]

        def copy(a, k, block, to, src=None):
            rows = o_refs[a].at[flat(block)]
            return pltpu.make_async_remote_copy(
                src_ref=rows if src is None else src, dst_ref=rows,
                send_sem=ssem.at[a * per + k], recv_sem=rsem.at[a * per + k],
                device_id=to, device_id_type=pl.DeviceIdType.MESH)

        own = [pltpu.make_async_copy(x_refs[a], o_refs[a].at[me], lsem.at[a]) for a in range(n)]
        first = []
        for a in range(n):
            first.append(copy(a, 0, me3, sibling, src=x_refs[a]))
            first += [copy(a, 1 + j, me3, (*chip, ci), src=x_refs[a]) for j, chip in enumerate(chips)]
        for cp in own + first:
            cp.start()
        passed = []
        for j, chip in enumerate(chips):
            for a in range(n):
                copy(a, 1 + j, (*chip, ci), me3).wait_recv()
                fw = copy(a, 4 + j, (*chip, ci), sibling)
                fw.start()
                passed.append(fw)
        for a in range(n):
            copy(a, 0, sibling, me3).wait_recv()
            for j, chip in enumerate(chips):
                copy(a, 4 + j, (*chip, 1 - ci), me3).wait_recv()
        for cp in first + passed:
            cp.wait_send()
        for cp in own:
            cp.wait()

    hbm = pl.BlockSpec(memory_space=pl.ANY)
    res = pl.pallas_call(
        body, name=name, in_specs=[hbm] * n, out_specs=[hbm] * n,
        out_shape=[jax.ShapeDtypeStruct((N_DEV,) + blks[a], xs[a].dtype) for a in range(n)],
        scratch_shapes=[pltpu.SemaphoreType.DMA((n * per,)), pltpu.SemaphoreType.DMA((n * per,)),
                        pltpu.SemaphoreType.DMA((n,))],
        compiler_params=pltpu.CompilerParams(has_side_effects=True),
    )(*xs)
    return list(res)


def mod_forward(crows, w_mod, b_shard, name):
    cols = w_mod.shape[2]

    def body(c_ref, w_ref, b_ref, o_ref):
        o_ref[0] = _dot(_silu(c_ref[...]).astype(BF), w_ref[0].astype(BF)) + b_ref[0]

    return pl.pallas_call(
        body, name=name, grid=(2,),
        in_specs=[pl.BlockSpec((16, D), lambda l: (0, 0)), pl.BlockSpec((1, D, cols), lambda l: (l, 0, 0)),
                  pl.BlockSpec((1, 1, cols), lambda l: (l, 0, 0))],
        out_specs=pl.BlockSpec((1, 16, cols), lambda l: (l, 0, 0)),
        out_shape=jax.ShapeDtypeStruct((2, 16, cols), F32), compiler_params=_cparams(1),
    )(crows, w_mod, b_shard)


def mod_backward(crows, w_mod, d_own, d_ctx, name):
    cols = w_mod.shape[2]

    def body(c_ref, w_ref, do_ref, dc_ref, gw_ref, gs_ref):
        dc = dc_ref[0]
        dsum = dc[0:1]
        for s in range(1, N_DEV):
            dsum = dsum + dc[s:s + 1]
        row = lax.broadcasted_iota(jnp.int32, (8, cols), 0)
        d16 = jnp.concatenate([do_ref[0], jnp.where(row == 0, jnp.broadcast_to(dsum, (8, cols)), 0.0)], axis=0)
        gw_ref[0] = _dot_tn(_silu(c_ref[...]).astype(BF), d16.astype(BF))
        gs_ref[0] = _dot_nt(jnp.broadcast_to(dsum, (8, cols)).astype(BF), w_ref[0].astype(BF))

    return pl.pallas_call(
        body, name=name, grid=(2,),
        in_specs=[pl.BlockSpec((16, D), lambda l: (0, 0)), pl.BlockSpec((1, D, cols), lambda l: (l, 0, 0)),
                  pl.BlockSpec((1, 8, cols), lambda l: (l, 0, 0)), pl.BlockSpec((1, 8, cols), lambda l: (l, 0, 0))],
        out_specs=[pl.BlockSpec((1, D, cols), lambda l: (l, 0, 0)), pl.BlockSpec((1, 8, D), lambda l: (l, 0, 0))],
        out_shape=[jax.ShapeDtypeStruct((2, D, cols), F32), jax.ShapeDtypeStruct((2, 8, D), F32)],
        compiler_params=_cparams(1),
    )(crows, w_mod, d_own, d_ctx)


def silu_grad_scale(c_ctx, ds, name):
    def body(c_ref, ds_ref, o_ref):
        cc = c_ref[...]
        sg = jax.nn.sigmoid(cc)
        o_ref[...] = (ds_ref[0][0:1] + ds_ref[1][0:1]) * (sg * (1.0 + cc * (1.0 - sg)))

    return pl.pallas_call(body, name=name, out_shape=jax.ShapeDtypeStruct((1, D), F32))(c_ctx, ds)


def _adamw_math(p_ref, w_ref, m_ref, v_ref, g_ref, d_ref, nm_ref, nv_ref):
    g = p_ref[0].astype(F32)
    for s in range(1, p_ref.shape[0]):
        g = g + p_ref[s].astype(F32)
    mm = ADAM_B1 * m_ref[...] + (1.0 - ADAM_B1) * g
    vv = ADAM_B2 * v_ref[...] + (1.0 - ADAM_B2) * (g * g)
    m_hat = mm / (1.0 - ADAM_B1 ** ADAM_STEP)
    v_hat = vv / (1.0 - ADAM_B2 ** ADAM_STEP)
    g_ref[...] = g
    d_ref[...] = -ADAM_LR * (m_hat / (jnp.sqrt(v_hat) + ADAM_EPS) + ADAM_WD * w_ref[...])
    nm_ref[...] = mm
    nv_ref[...] = vv


def adamw(parts, w, m, v, name):
    n_parts, rows, cols = parts.shape
    lanes = -(-cols // 128) * 128
    block_rows = min(rows, 1 << ((ADAMW_BLOCK_ELEMS // lanes).bit_length() - 1))
    assert rows % block_rows == 0

    def body(*refs):
        _adamw_math(*refs)

    spec = pl.BlockSpec((block_rows, cols), lambda i: (i, 0))
    return pl.pallas_call(
        body, name=name, grid=(rows // block_rows,),
        in_specs=[pl.BlockSpec((n_parts, block_rows, cols), lambda i: (0, i, 0)), spec, spec, spec],
        out_specs=[spec] * 4, out_shape=[jax.ShapeDtypeStruct((rows, cols), F32)] * 4,
        compiler_params=_cparams(1),
    )(parts, w, m, v)


def adamw_group(items, name):
    n = len(items)

    def body(*refs):
        for a in range(n):
            _adamw_math(*refs[4 * a:4 * a + 4], *refs[4 * n + 4 * a:4 * n + 4 * a + 4])

    flat_in = [x for it in items for x in it]
    out_shape = [jax.ShapeDtypeStruct(it[1].shape, F32) for it in items for _ in range(4)]
    res = pl.pallas_call(body, name=name, out_shape=out_shape,
                         compiler_params=pltpu.CompilerParams(vmem_limit_bytes=VMEM_LIMIT_MB * 1024 * 1024))(*flat_in)
    return [tuple(res[4 * a:4 * a + 4]) for a in range(n)]


def _pad_cols(w, segs, total):
    parts, pos = [], 0
    for dst, src, wd in segs:
        if dst > pos:
            parts.append(jnp.zeros(w.shape[:-1] + (dst - pos,), w.dtype))
        parts.append(w[..., src:src + wd])
        pos = dst + wd
    if pos < total:
        parts.append(jnp.zeros(w.shape[:-1] + (total - pos,), w.dtype))
    return jnp.concatenate(parts, axis=-1)


def _unpad_cols(g, segs):
    return jnp.concatenate([g[..., dst:dst + wd] for dst, _, wd in segs], axis=-1)


def _rope_tables(n_lat):
    rows = n_lat // GRID_W
    row = jnp.repeat(jnp.arange(rows), GRID_W).astype(F32)
    col = jnp.tile(jnp.arange(GRID_W), rows).astype(F32)
    freq = ROPE_BASE ** (-jnp.arange(16, dtype=F32) * 2.0 / 32)
    ar, ac = row[:, None] * freq[None, :], col[:, None] * freq[None, :]
    z = jnp.zeros((n_lat, 16), F32)
    cos = jnp.concatenate([jnp.cos(ar), jnp.cos(ar), jnp.cos(ac), jnp.cos(ac), jnp.ones((n_lat, 64), F32)], axis=1)
    sa = jnp.concatenate([-jnp.sin(ar), z, -jnp.sin(ac), z, jnp.zeros((n_lat, 64), F32)], axis=1)
    sb = jnp.concatenate([z, jnp.sin(ar), z, jnp.sin(ac), jnp.zeros((n_lat, 64), F32)], axis=1)
    ident = lambda fill: jnp.full((TILE, 128), fill, F32)
    return (jnp.concatenate([ident(1.0), cos]), jnp.concatenate([ident(0.0), sa]), jnp.concatenate([ident(0.0), sb]))


def _gathered_to_full(g, name):
    if name in ("w_out", "w_ff2"):
        return jnp.transpose(g, (1, 0, 2, 3)).reshape(2, -1, g.shape[-1])
    return jnp.transpose(g, (1, 2, 0, 3)).reshape(2, g.shape[2], -1)


def _full_to_chunks(gw, name):
    if name in ("w_out", "w_ff2"):
        return jnp.transpose(gw.reshape(2, N_DEV, -1, gw.shape[-1]), (1, 0, 2, 3))
    return jnp.transpose(gw.reshape(2, gw.shape[1], N_DEV, -1), (2, 0, 1, 3))


def kernel(x, c, ctx, c_ctx, w_mod, b_mod, norm1_w, w_in, w_out, sgu_norm_w, sgu_norm_b, sgu_w, sgu_b, gla_wg_fwd, gla_bg_fwd, gla_wg_bwd, gla_bg_bwd, gla_norm_w, mla_q_norm_w, mla_w_uq, mla_kv_norm_w, mla_w_ukv, norm2_w, w_ff1, w_ff2, final_norm_w, loss_target, m_c_ctx, m_w_mod, m_b_mod, m_norm1_w, m_w_in, m_w_out, m_sgu_norm_w, m_sgu_norm_b, m_sgu_w, m_sgu_b, m_gla_wg_fwd, m_gla_bg_fwd, m_gla_wg_bwd, m_gla_bg_bwd, m_gla_norm_w, m_mla_q_norm_w, m_mla_w_uq, m_mla_kv_norm_w, m_mla_w_ukv, m_norm2_w, m_w_ff1, m_w_ff2, m_final_norm_w, v_c_ctx, v_w_mod, v_b_mod, v_norm1_w, v_w_in, v_w_out, v_sgu_norm_w, v_sgu_norm_b, v_sgu_w, v_sgu_b, v_gla_wg_fwd, v_gla_bg_fwd, v_gla_wg_bwd, v_gla_bg_bwd, v_gla_norm_w, v_mla_q_norm_w, v_mla_w_uq, v_mla_kv_norm_w, v_mla_w_ukv, v_norm2_w, v_w_ff1, v_w_ff2, v_final_norm_w):
    W = dict(c_ctx=c_ctx, w_mod=w_mod, b_mod=b_mod, norm1_w=norm1_w, w_in=w_in, w_out=w_out, sgu_norm_w=sgu_norm_w,
             sgu_norm_b=sgu_norm_b, sgu_w=sgu_w, sgu_b=sgu_b, gla_wg_fwd=gla_wg_fwd, gla_bg_fwd=gla_bg_fwd,
             gla_wg_bwd=gla_wg_bwd, gla_bg_bwd=gla_bg_bwd, gla_norm_w=gla_norm_w, mla_q_norm_w=mla_q_norm_w,
             mla_w_uq=mla_w_uq, mla_kv_norm_w=mla_kv_norm_w, mla_w_ukv=mla_w_ukv, norm2_w=norm2_w, w_ff1=w_ff1,
             w_ff2=w_ff2, final_norm_w=final_norm_w)
    M = dict(c_ctx=m_c_ctx, w_mod=m_w_mod, b_mod=m_b_mod, norm1_w=m_norm1_w, w_in=m_w_in, w_out=m_w_out,
             sgu_norm_w=m_sgu_norm_w, sgu_norm_b=m_sgu_norm_b, sgu_w=m_sgu_w, sgu_b=m_sgu_b, gla_wg_fwd=m_gla_wg_fwd,
             gla_bg_fwd=m_gla_bg_fwd, gla_wg_bwd=m_gla_wg_bwd, gla_bg_bwd=m_gla_bg_bwd, gla_norm_w=m_gla_norm_w,
             mla_q_norm_w=m_mla_q_norm_w, mla_w_uq=m_mla_w_uq, mla_kv_norm_w=m_mla_kv_norm_w, mla_w_ukv=m_mla_w_ukv,
             norm2_w=m_norm2_w, w_ff1=m_w_ff1, w_ff2=m_w_ff2, final_norm_w=m_final_norm_w)
    V = dict(c_ctx=v_c_ctx, w_mod=v_w_mod, b_mod=v_b_mod, norm1_w=v_norm1_w, w_in=v_w_in, w_out=v_w_out,
             sgu_norm_w=v_sgu_norm_w, sgu_norm_b=v_sgu_norm_b, sgu_w=v_sgu_w, sgu_b=v_sgu_b, gla_wg_fwd=v_gla_wg_fwd,
             gla_bg_fwd=v_gla_bg_fwd, gla_wg_bwd=v_gla_wg_bwd, gla_bg_bwd=v_gla_bg_bwd, gla_norm_w=v_gla_norm_w,
             mla_q_norm_w=v_mla_q_norm_w, mla_w_uq=v_mla_w_uq, mla_kv_norm_w=v_mla_kv_norm_w, mla_w_ukv=v_mla_w_ukv,
             norm2_w=v_norm2_w, w_ff1=v_w_ff1, w_ff2=v_w_ff2, final_norm_w=v_final_norm_w)

    n_lat = x.shape[1]
    assert ctx.shape[1] == TILE and n_lat % TILE == 0 and x.shape[2] == D
    t_all = TILE + n_lat
    n_t = t_all // TILE
    me = 4 * lax.axis_index("x") + 2 * lax.axis_index("y") + lax.axis_index("c")
    mod_cols = w_mod.shape[2]

    c_all = exchange([c], "ag_c", scatter=False)[0].reshape(N_DEV, D)
    crows = jnp.concatenate([c_all, c_ctx[None, :], jnp.zeros((7, D), F32)], axis=0)
    b_shard = lax.dynamic_slice_in_dim(b_mod, me * mod_cols, mod_cols, axis=1)[:, None, :]
    mod_sh = mod_forward(crows, w_mod, b_shard, "mod_fwd")
    mod_g = exchange([mod_sh.reshape(32, mod_cols)], "ag_mod", scatter=False)[0]
    mod_full = jnp.transpose(mod_g.reshape(N_DEV, 2, 16, mod_cols), (1, 2, 0, 3)).reshape(2, 16, 6 * D)
    mod_own = lax.dynamic_index_in_dim(mod_full, me, axis=1, keepdims=False)
    mod_ctx = mod_full[:, 8, :]
    pad2 = jnp.zeros((2, D), F32)
    modl = [jnp.stack([jnp.concatenate([mod_ctx[l].reshape(6, D), pad2]),
                       jnp.concatenate([mod_own[l].reshape(6, D), pad2])]) for l in range(2)]

    v2 = lambda a: a[None, :] if a.ndim == 1 else a.reshape(-1, a.shape[-1])
    gathered = exchange([v2(W[k].astype(BF)) for k in BIG_NAMES], "ag_weights", scatter=False)
    full = {k: _gathered_to_full(g.reshape((N_DEV,) + W[k].shape), k) for k, g in zip(BIG_NAMES, gathered)}
    w_in_p = _pad_cols(full["w_in"], W_IN_SEGS, P_COLS)
    w_uq_p = _pad_cols(full["mla_w_uq"], W_UQ_SEGS, 1024).astype(F32)
    w_ukv_f = full["mla_w_ukv"].astype(F32)
    wgf_p = jnp.pad(gla_wg_fwd, ((0, 0), (0, 112), (0, 0)))
    wgb_p = jnp.pad(gla_wg_bwd, ((0, 0), (0, 112), (0, 0)))
    sgu_bx = jnp.repeat(jnp.transpose(sgu_b, (0, 2, 1)), 64, axis=2)
    gnw_t = jnp.tile(gla_norm_w, (1, HEADS))
    rc, rsa, rsb = _rope_tables(n_lat)

    xin = jnp.concatenate([ctx[0], x[0]], axis=0)
    row = lambda a: a[None, :]

    def pre_ins(l, xl):
        return [("x", "tile", True, xl), ("mod", "kind", True, modl[l]), ("n1w", "full", True, row(norm1_w[l])),
                ("w_in", "wfull", False, w_in_p[l]), ("sgu_nw", "full", True, row(sgu_norm_w[l])),
                ("sgu_nb", "full", True, row(sgu_norm_b[l])), ("sgu_w", "full", True, sgu_w[l]),
                ("sgu_bx", "full", True, sgu_bx[l]), ("wgf", "full", True, wgf_p[l]), ("bgf", "full", True, row(gla_bg_fwd[l])),
                ("wgb", "full", True, wgb_p[l]), ("bgb", "full", True, row(gla_bg_bwd[l])),
                ("qnw", "full", True, row(mla_q_norm_w[l])), ("w_uq", "full", True, w_uq_p[l]),
                ("kvnw", "full", True, row(mla_kv_norm_w[l])), ("w_ukv", "full", True, w_ukv_f[l]),
                ("rc", "tile", False, rc), ("rsa", "tile", False, rsa), ("rsb", "tile", False, rsb)]

    pre_outs = [("y_sgu", 256, F32), ("qg", 128, F32), ("kg", 128, F32), ("vg", 256, F32), ("lgf", 128, F32),
                ("lgb", 128, F32), ("gr", 256, F32), ("q_cat", 1024, BF), ("k_cat", 1024, BF), ("v", 512, BF)]

    def out_ins(l, xl, a):
        return [("x", "tile", True, xl), ("mod", "kind", True, modl[l]), ("y_sgu", "tile", True, a["y_sgu"]),
                ("o_f", "tile", True, a["o_f"]), ("o_b", "tile", True, a["o_b"]), ("gr", "tile", True, a["gr"]),
                ("y_mla", "tile", True, a["y_mla"]), ("gnw", "full", True, row(gnw_t[l])),
                ("w_out", "wfull", False, full["w_out"][l])]

    def ffn_ins(l, x1):
        return [("x1", "tile", True, x1), ("mod", "kind", True, modl[l]), ("n2w", "full", True, row(norm2_w[l])),
                ("w_ff1", "wcols", False, full["w_ff1"][l]), ("w_ff2", "wrows", False, full["w_ff2"][l])]

    saved, xl = [], xin
    for l in range(2):
        a = tile_forward(pre_tile, f"pre_fwd{l}", t_all, pre_ins(l, xl), pre_outs)
        a["o_f"], a["sf"], a["o_b"], a["sb"] = gla_forward(a["qg"], a["kg"], a["vg"], a["lgf"], a["lgb"], f"gla_fwd{l}")
        a["y_mla"], a["lse"] = mla_forward(a["q_cat"], a["k_cat"], a["v"], f"mla_fwd{l}")
        a["x"] = xl
        a["x1"] = tile_forward(attn_out_tile, f"out_fwd{l}", t_all, out_ins(l, xl, a), [("x1", D, F32)])["x1"]
        xl = tile_forward(ffn_tile, f"ffn_fwd{l}", t_all, ffn_ins(l, a["x1"]), [("x2", D, F32)])["x2"]
        saved.append(a)

    loss_blk, dx, d_fnw = final_loss(xl, loss_target[0], row(final_norm_w), "final_loss")
    loss = lax.psum(loss_blk[0, 0], AXES)

    G = {}
    dmods = []
    for l in (1, 0):
        a = saved[l]
        dx1, dmod3, dn2w, zpre, zf_t, h2_t, a_ff2 = ffn_backward(
            a["x1"], modl[l], row(norm2_w[l]), full["w_ff1"][l], full["w_ff2"][l], dx, f"ffn_bwd{l}")
        gw_ff1 = wgrad(h2_t, zpre, f"wg_ff1_{l}")
        gw_ff2 = jnp.transpose(wgrad(zf_t, a_ff2, f"wg_ff2_{l}"))
        g2, e2 = tile_backward(attn_out_tile, f"out_bwd{l}", t_all, out_ins(l, a["x"], a), [("x1", dx1)],
                               [("zt", D)], [("a_out", D)])
        gw_out = wgrad(e2["a_out"], e2["zt"], f"wg_out_{l}")
        dq_cat, delta = mla_backward_q(a["q_cat"], a["k_cat"], a["v"], a["y_mla"], a["lse"], g2["y_mla"], f"mla_bwd_q{l}")
        head_rows = lambda r: jnp.pad(jnp.transpose(r[:, ::128]), ((0, 8 - HEADS), (0, 0)))
        dk_cat, dv = mla_backward_kv(a["q_cat"], a["k_cat"], a["v"], head_rows(a["lse"]), head_rows(delta),
                                     g2["y_mla"].astype(BF), f"mla_bwd_kv{l}")
        dqf, dkf, dvf, dgf, dqb, dkb, dvb, dgb = gla_backward(
            a["qg"], a["kg"], a["vg"], a["lgf"], a["lgb"], a["sf"], a["sb"], g2["o_f"], f"gla_bwd{l}")
        cots = [("y_sgu", g2["y_sgu"]), ("qg", [dqf, dqb]), ("kg", [dkf, dkb]), ("vg", [dvf, dvb]), ("lgf", dgf),
                ("lgb", dgb), ("gr", g2["gr"]), ("q_cat", dq_cat), ("k_cat", dk_cat), ("v", dv)]
        g1, e1 = tile_backward(pre_tile, f"pre_bwd{l}", t_all, pre_ins(l, a["x"]), cots, [("zp", P_COLS)], [("a_in", D)],
                               resid=("x", g2["x"]))
        gw_in = _unpad_cols(wgrad(e1["a_in"], e1["zp"], f"wg_in_{l}", bk2=P_COLS), W_IN_SEGS)
        dx = g1["x"]
        dmods.append(g1["mod"] + g2["mod"] + dmod3)
        G[l] = dict(w_in=gw_in, w_out=gw_out, w_ff1=gw_ff1, w_ff2=gw_ff2,
                    mla_w_uq=_unpad_cols(g1["w_uq"], W_UQ_SEGS), mla_w_ukv=g1["w_ukv"],
                    norm1_w=g1["n1w"][0], norm2_w=dn2w[0], sgu_norm_w=g1["sgu_nw"][0], sgu_norm_b=g1["sgu_nb"][0],
                    sgu_w=g1["sgu_w"], sgu_b=jnp.transpose(g1["sgu_bx"].reshape(128, HEADS, 64).sum(-1)),
                    gla_wg_fwd=g1["wgf"][:16], gla_bg_fwd=g1["bgf"][0], gla_wg_bwd=g1["wgb"][:16], gla_bg_bwd=g1["bgb"][0],
                    gla_norm_w=g2["gnw"][0].reshape(HEADS, 64).sum(0), mla_q_norm_w=g1["qnw"][0], mla_kv_norm_w=g1["kvnw"][0])
    dmods = dmods[::-1]
    grad_x = dx[TILE:][None]

    dmod_pack = jnp.stack([jnp.stack([dmods[l][1, :6].reshape(-1), dmods[l][0, :6].reshape(-1)]) for l in range(2)])
    dmod_all = exchange([dmod_pack.reshape(4, 6 * D)], "ag_dmod", scatter=False)[0].reshape(N_DEV, 2, 2, 6 * D)
    dsl = lax.dynamic_slice_in_dim(dmod_all, me * mod_cols, mod_cols, axis=3)
    d_own = jnp.transpose(dsl[:, :, 0, :], (1, 0, 2))
    d_ctx = jnp.transpose(dsl[:, :, 1, :], (1, 0, 2))
    g_w_mod, ds_cc = mod_backward(crows, w_mod, d_own, d_ctx, "mod_bwd")
    g_c_ctx_part = silu_grad_scale(c_ctx[None, :], ds_cc, "silu_bwd")[0]
    g_b_mod_part = jnp.stack([dmods[l][1, :6].reshape(-1) + dmods[l][0, :6].reshape(-1) for l in range(2)])

    small_g = dict(c_ctx=g_c_ctx_part, b_mod=g_b_mod_part, final_norm_w=d_fnw[0])
    for k in SMALL_NAMES:
        if k not in small_g:
            small_g[k] = jnp.stack([G[0][k], G[1][k]])
    res = {}
    sparts = exchange([v2(small_g[k]) for k in SMALL_NAMES], "ag_small", scatter=False)
    s_out = adamw_group([(sparts[j], v2(W[k]), v2(M[k]), v2(V[k])) for j, k in enumerate(SMALL_NAMES)], "adamw_small")
    for j, k in enumerate(SMALL_NAMES):
        res[k] = [o.reshape(W[k].shape) for o in s_out[j]]

    chunks = []
    for k in BIG_NAMES:
        ch = _full_to_chunks(jnp.stack([G[0][k], G[1][k]]), k).astype(BF)
        chunks.append(ch.reshape(N_DEV, -1, ch.shape[-1]))
    bparts = exchange(chunks, "a2a_grads", scatter=True)
    for j, k in enumerate(BIG_NAMES):
        res[k] = [o.reshape(W[k].shape) for o in adamw(bparts[j], v2(W[k]), v2(M[k]), v2(V[k]), f"adamw_{k}")]
    res["w_mod"] = [o.reshape(w_mod.shape)
                    for o in adamw(v2(g_w_mod)[None], v2(w_mod), v2(m_w_mod), v2(v_w_mod), "adamw_w_mod")]
    outs = [loss, grad_x]
    for j in range(4):
        outs += [res[k][j] for k in WEIGHT_ORDER]
    return tuple(outs)
```

```python
import jax
import jax.numpy as jnp
from jax import lax
from jax.experimental import pallas as pl
from jax.experimental.pallas import tpu as pltpu

F32 = jnp.float32
BF = jnp.bfloat16

N_DEV = 8
AXES = ("x", "y", "c")
EPS = 1e-6
D = 1024
TILE = 256
GCH = 64
SGU_CHUNK = 128
HEADS = 4
ROPE_BASE = 10000.0
GRID_W = 64
GLA_TAU = 16.0
ATT_SCALE = (128 + 64) ** -0.5
ATT_SCALE_LOG2 = ATT_SCALE * 1.4426950408889634
LN2 = 0.6931471805599453
KV_CH = 512
KV_CH_FWD = 2048
Q_CH_BWD = 1024
MLA_UNROLL = 2
D_FF = 4096
FF_CH = 1024

ADAM_LR = 0.001
ADAM_B1 = 0.9
ADAM_B2 = 0.999
ADAM_EPS = 1e-08
ADAM_WD = 0.01
ADAM_STEP = 10

VMEM_LIMIT_MB = 56
ADAMW_BLOCK_ELEMS = 256 * 1024

W_IN_SEGS = ((0, 0, 128), (128, 128, 256), (384, 384, 16), (512, 400, 16), (640, 416, 256), (896, 672, 64),
             (1024, 736, 256), (1280, 992, 256), (1536, 1248, 128), (1664, 1376, 256), (1920, 1632, 256))
P_COLS = 2176
O_GK, O_GV, O_GGF, O_GGB, O_CKV, O_KR, O_SU, O_SV, O_GQ, O_GR, O_DQ = (s[0] for s in W_IN_SEGS)
W_UQ_SEGS = tuple((h * 256, h * 192, 192) for h in range(HEADS))

SMALL_NAMES = ("c_ctx", "b_mod", "norm1_w", "sgu_norm_w", "sgu_norm_b", "sgu_w", "sgu_b", "gla_wg_fwd", "gla_bg_fwd",
               "gla_wg_bwd", "gla_bg_bwd", "gla_norm_w", "mla_q_norm_w", "mla_kv_norm_w", "norm2_w", "final_norm_w")
BIG_NAMES = ("w_in", "w_out", "mla_w_uq", "mla_w_ukv", "w_ff1", "w_ff2")
WEIGHT_ORDER = ("c_ctx", "w_mod", "b_mod", "norm1_w", "w_in", "w_out", "sgu_norm_w", "sgu_norm_b", "sgu_w", "sgu_b",
                "gla_wg_fwd", "gla_bg_fwd", "gla_wg_bwd", "gla_bg_bwd", "gla_norm_w", "mla_q_norm_w", "mla_w_uq",
                "mla_kv_norm_w", "mla_w_ukv", "norm2_w", "w_ff1", "w_ff2", "final_norm_w")


def _cparams(n_axes):
    return pltpu.CompilerParams(dimension_semantics=("arbitrary",) * n_axes,
                                vmem_limit_bytes=VMEM_LIMIT_MB * 1024 * 1024)


def _dot(a, b):
    return jnp.dot(a, b, preferred_element_type=F32)


def _dot_nt(a, b):
    return lax.dot_general(a, b, (((1,), (1,)), ((), ())), preferred_element_type=F32)


def _dot_tn(a, b):
    return lax.dot_general(a, b, (((0,), (0,)), ((), ())), preferred_element_type=F32)


def _nn(a, b):
    return _dot(a.astype(BF), b.astype(BF))


def _nt(a, b):
    return _dot_nt(a.astype(BF), b.astype(BF))


def _tn(a, b):
    return _dot_tn(a.astype(BF), b.astype(BF))


nn_d = jax.custom_vjp(_nn)
nt_d = jax.custom_vjp(_nt)
tn_d = jax.custom_vjp(_tn)
nn_d.defvjp(lambda a, b: (_nn(a, b), (a, b)), lambda r, dy: (_nt(dy, r[1]), _tn(r[0], dy)))
nt_d.defvjp(lambda a, b: (_nt(a, b), (a, b)), lambda r, dy: (_nn(dy, r[1]), _tn(dy, r[0])))
tn_d.defvjp(lambda a, b: (_tn(a, b), (a, b)), lambda r, dy: (_nt(r[1], dy), _nn(r[0], dy)))


def nn_const(w_bf, diff):
    def raw(a):
        return _dot(a.astype(BF), w_bf)

    if not diff:
        return raw
    f = jax.custom_vjp(raw)
    f.defvjp(lambda a: (raw(a), None), lambda _, dy: (_dot_nt(dy.astype(BF), w_bf),))
    return f


def _split3(g):
    hi = g.astype(BF)
    r = g - hi.astype(F32)
    mid = r.astype(BF)
    lo = (r - mid.astype(F32)).astype(BF)
    return hi, mid, lo


def make_cum(tri_bf, tri_t_bf, diff):
    def raw(g, t):
        hi, mid, lo = _split3(g)
        return _dot(t, hi) + _dot(t, mid) + _dot(t, lo)

    def fwd(g):
        return raw(g, tri_bf)

    if not diff:
        return fwd
    cum = jax.custom_vjp(fwd)
    cum.defvjp(lambda g: (fwd(g), None), lambda _, db: (raw(db, tri_t_bf),))
    return cum


def _roll_lanes(x, shift):
    return pltpu.roll(x, shift, 1)


def make_rope(c, sa, sb, diff):
    def raw(x):
        return x * c + _roll_lanes(x, 112) * sa + _roll_lanes(x, 16) * sb

    if not diff:
        return raw
    f = jax.custom_vjp(raw)
    f.defvjp(lambda x: (raw(x), None),
             lambda _, dy: (dy * c + _roll_lanes(dy * sa, 16) + _roll_lanes(dy * sb, 112),))
    return f


def _ops(diff):
    return (nn_d, nt_d, tn_d) if diff else (_nn, _nt, _tn)


def _rms(x, w):
    return x * lax.rsqrt(jnp.mean(x * x, axis=-1, keepdims=True) + EPS) * w


def _gelu(x):
    return 0.5 * x * (1.0 + jnp.tanh(0.7978845608028654 * (x + 0.044715 * (x * x * x))))


def _silu(x):
    return x * jax.nn.sigmoid(x)


def _log_sigmoid(z):
    return jnp.minimum(z, 0.0) - jnp.log(1.0 + jnp.exp(-jnp.abs(z)))


def _lane_group_mask(width, group, h):
    lane = lax.broadcasted_iota(jnp.int32, (1, width), 1)
    return ((lane >= h * group) & (lane < (h + 1) * group)).astype(F32)


def pre_tile(d, c, z):
    nn, _, _ = _ops(z is not None)
    rope = make_rope(c["rc"], c["rsa"], c["rsb"], z is not None)
    mod = d["mod"]
    h = _rms(d["x"], d["n1w"]) * (1.0 + mod[1:2]) + mod[0:1]
    p = nn_const(c["w_in"], z is not None)(h)
    if z is not None:
        p = p + z["zp"]
    gk, gv = p[:, O_GK:O_GK + 128], p[:, O_GV:O_GV + 256]
    ggf, ggb = p[:, O_GGF:O_GGF + 128], p[:, O_GGB:O_GGB + 128]
    ckv, kr = p[:, O_CKV:O_CKV + 256], p[:, O_KR:O_KR + 128]
    su, sv = p[:, O_SU:O_SU + 256], p[:, O_SV:O_SV + 256]
    gq, gr, dq = p[:, O_GQ:O_GQ + 128], p[:, O_GR:O_GR + 256], p[:, O_DQ:O_DQ + 256]

    u = _gelu(su)
    gv_ = _gelu(sv)
    mu = jnp.mean(gv_, axis=-1, keepdims=True)
    cen = gv_ - mu
    vn = cen * lax.rsqrt(jnp.mean(cen * cen, axis=-1, keepdims=True) + EPS) * d["sgu_nw"] + d["sgu_nb"]
    hm = [_lane_group_mask(256, 64, hh) for hh in range(HEADS)]
    rows = []
    for ci in range(vn.shape[0] // SGU_CHUNK):
        vc = vn[ci * SGU_CHUNK:(ci + 1) * SGU_CHUNK]
        s = d["sgu_bx"]
        for hh in range(HEADS):
            s = s + hm[hh] * nn(d["sgu_w"][hh], vc)
        rows.append(s)
    y_sgu = u * jnp.concatenate(rows, axis=0)

    qg = gq * (32 ** -0.5)
    lgf = _log_sigmoid(nn(ggf, d["wgf"]) + d["bgf"]) * (1.0 / GLA_TAU)
    lgb = _log_sigmoid(nn(ggb, d["wgb"]) + d["bgb"]) * (1.0 / GLA_TAU)

    kv = nn(_rms(ckv, d["kvnw"]), d["w_ukv"])
    kr_r = rope(kr)
    q = nn(_rms(dq, d["qnw"]), d["w_uq"])
    qs, ks, vs = [], [], []
    for hh in range(HEADS):
        qs += [q[:, hh * 256:hh * 256 + 128], rope(q[:, hh * 256 + 128:(hh + 1) * 256])]
        ks += [kv[:, hh * 256:hh * 256 + 128], kr_r]
        vs += [kv[:, hh * 256 + 128:(hh + 1) * 256]]
    outs = dict(y_sgu=y_sgu, qg=qg, kg=gk, vg=gv, lgf=lgf, lgb=lgb, gr=gr,
                q_cat=jnp.concatenate(qs, axis=-1) * ATT_SCALE_LOG2, k_cat=jnp.concatenate(ks, axis=-1), v=jnp.concatenate(vs, axis=-1))
    return outs, dict(a_in=h)


def attn_out_tile(d, c, z):
    mod = d["mod"]
    o = d["o_f"] + d["o_b"]
    ms = jnp.zeros_like(o)
    for hh in range(HEADS):
        m_h = _lane_group_mask(256, 64, hh)
        ms = ms + m_h * (jnp.sum(o * o * m_h, axis=-1, keepdims=True) * (1.0 / 64))
    yg = o * lax.rsqrt(ms + EPS) * d["gnw"] * _silu(d["gr"])
    y = jnp.concatenate([d["y_sgu"], yg, d["y_mla"]], axis=-1)
    t = nn_const(c["w_out"], z is not None)(y)
    if z is not None:
        t = t + z["zt"]
    return dict(x1=d["x"] + mod[2:3] * t), dict(a_out=y)


def ffn_tile(d, c, z):
    mod = d["mod"]
    h2 = _rms(d["x1"], d["n2w"]) * (1.0 + mod[4:5]) + mod[3:4]
    f = None
    a2s = []
    for j in range(D_FF // FF_CH):
        pre = nn_const(c["w_ff1"][j], z is not None)(h2)
        if z is not None:
            pre = pre + z["zpre"][:, j * FF_CH:(j + 1) * FF_CH]
        a = jnp.maximum(pre, 0.0)
        a2 = a * a
        a2s.append(a2)
        fj = nn_const(c["w_ff2"][j], z is not None)(a2)
        f = fj if f is None else f + fj
    if z is not None:
        f = f + z["zf"]
    return dict(x2=d["x1"] + mod[5:6] * f), dict(a_ff1=h2, a_ff2=jnp.concatenate(a2s, axis=-1))


def _in_spec(kind, arr, tile):
    if kind == "tile":
        return pl.BlockSpec((tile, arr.shape[1]), lambda i: (i, 0))
    if kind == "kind":
        return pl.BlockSpec((1,) + arr.shape[1:], lambda i: (jnp.where(i < TILE // tile, 0, 1), 0, 0))
    nd = arr.ndim
    if kind in ("wfull", "wcols", "wrows"):
        return pl.BlockSpec(arr.shape, lambda i: (0,) * nd, pipeline_mode=pl.Buffered(1))
    return pl.BlockSpec(arr.shape, lambda i: (0,) * nd)


def _load(kind, ref):
    if kind == "kind":
        return ref[0]
    if kind == "wcols":
        return [ref[:, j * FF_CH:(j + 1) * FF_CH] for j in range(ref.shape[1] // FF_CH)]
    if kind == "wrows":
        return [ref[j * FF_CH:(j + 1) * FF_CH, :] for j in range(ref.shape[0] // FF_CH)]
    return ref[...]


def tile_forward(fn, name, t_all, ins, out_defs, tile=TILE):
    keys = [k for k, _, _, _ in ins]
    kinds = [kd for _, kd, _, _ in ins]
    diffs = [df for _, _, df, _ in ins]
    arrs = [a for _, _, _, a in ins]
    n_in = len(ins)

    def body(*refs):
        vals = [_load(kinds[j], refs[j]) for j in range(n_in)]
        d = {keys[j]: vals[j] for j in range(n_in) if diffs[j]}
        c = {keys[j]: vals[j] for j in range(n_in) if not diffs[j]}
        outs, _ = fn(d, c, None)
        for j, (k, _, dt) in enumerate(out_defs):
            refs[n_in + j][...] = outs[k].astype(dt)

    res = pl.pallas_call(
        body, name=name, grid=(t_all // tile,),
        in_specs=[_in_spec(kinds[j], arrs[j], tile) for j in range(n_in)],
        out_specs=[pl.BlockSpec((tile, w), lambda i: (i, 0)) for _, w, _ in out_defs],
        out_shape=[jax.ShapeDtypeStruct((t_all, w), dt) for _, w, dt in out_defs],
        compiler_params=_cparams(1),
    )(*arrs)
    return {k: r for (k, _, _), r in zip(out_defs, res)}


def tile_backward(fn, name, t_all, ins, cots, z_defs, aux_defs, tile=TILE, resid=None):
    keys = [k for k, _, _, _ in ins]
    kinds = [kd for _, kd, _, _ in ins]
    diffs = [df for _, _, df, _ in ins]
    arrs = [a for _, _, _, a in ins]
    cot_keys, cot_arrs = [], []
    for k, a in cots:
        for one in (a if isinstance(a, (list, tuple)) else [a]):
            cot_keys.append(k)
            cot_arrs.append(one)
    if resid is not None:
        cot_keys.append("resid:" + resid[0])
        cot_arrs.append(resid[1])
    n_in, n_cot = len(ins), len(cot_arrs)
    dkeys = [j for j in range(n_in) if diffs[j]]
    ctx_tiles = TILE // tile

    def body(*refs):
        i = pl.program_id(0)
        vals = [_load(kinds[j], refs[j]) for j in range(n_in)]
        d = {keys[j]: vals[j] for j in dkeys}
        c = {keys[j]: vals[j] for j in range(n_in) if not diffs[j]}
        zs = {k: jnp.zeros((tile, w), F32) for k, w in z_defs}
        outs, vjp_fn, aux = jax.vjp(lambda dd, zz: fn(dd, c, zz), d, zs, has_aux=True)
        ct = {}
        for j, k in enumerate(cot_keys):
            ct[k] = refs[n_in + j][...] + ct[k] if k in ct else refs[n_in + j][...]
        dd, dz = vjp_fn({k: ct[k].astype(outs[k].dtype) for k in outs})
        base = n_in + n_cot
        for n, j in enumerate(dkeys):
            ref, g = refs[base + n], dd[keys[j]]
            if kinds[j] == "tile":
                ref[...] = g + ct["resid:" + keys[j]] if "resid:" + keys[j] in ct else g
            else:
                first = ((i == 0) | (i == ctx_tiles)) if kinds[j] == "kind" else (i == 0)
                gv = g[None] if kinds[j] == "kind" else g

                @pl.when(first)
                def _(ref=ref, gv=gv):
                    ref[...] = gv

                @pl.when(jnp.logical_not(first))
                def _(ref=ref, gv=gv):
                    ref[...] += gv
        base += len(dkeys)
        for n, (k, _) in enumerate(z_defs):
            refs[base + n][...] = dz[k].astype(BF)
        base += len(z_defs)
        for n, (k, _) in enumerate(aux_defs):
            refs[base + n][...] = aux[k].T.astype(BF)

    out_specs, out_shape = [], []
    for j in dkeys:
        out_specs.append(_in_spec(kinds[j], arrs[j], tile))
        out_shape.append(jax.ShapeDtypeStruct(arrs[j].shape, F32))
    for _, w in z_defs:
        out_specs.append(pl.BlockSpec((tile, w), lambda i: (i, 0)))
        out_shape.append(jax.ShapeDtypeStruct((t_all, w), BF))
    for _, w in aux_defs:
        out_specs.append(pl.BlockSpec((w, tile), lambda i: (0, i)))
        out_shape.append(jax.ShapeDtypeStruct((w, t_all), BF))
    res = pl.pallas_call(
        body, name=name, grid=(t_all // tile,),
        in_specs=[_in_spec(kinds[j], arrs[j], tile) for j in range(n_in)]
        + [pl.BlockSpec((tile, a.shape[1]), lambda i: (i, 0)) for a in cot_arrs],
        out_specs=out_specs, out_shape=out_shape, compiler_params=_cparams(1),
    )(*arrs, *cot_arrs)
    grads = {keys[j]: res[n] for n, j in enumerate(dkeys)}
    extra = {k: res[len(dkeys) + n] for n, (k, _) in enumerate(list(z_defs) + list(aux_defs))}
    return grads, extra


def ffn_backward(x1, modl, n2w, w1, w2, dx2, name):
    t_all = x1.shape[0]
    n_ch = D_FF // FF_CH

    def head(x, mod, nw):
        return _rms(x, nw) * (1.0 + mod[4:5]) + mod[3:4]

    def body(x_ref, mod_ref, nw_ref, w1_ref, w2_ref, dx2_ref, dx1_ref, dmod_ref, dnw_ref, zpre_ref, zf_ref, a1_ref, a2_ref):
        i = pl.program_id(0)
        mod = mod_ref[0]
        dx2 = dx2_ref[...]
        h2, vjp_head = jax.vjp(head, x_ref[...], mod, nw_ref[...])
        h2b = h2.astype(BF)
        dfb = (dx2 * mod[5:6]).astype(BF)
        f = jnp.zeros((TILE, D), F32)
        dh2 = jnp.zeros((TILE, D), F32)
        for j in range(n_ch):
            cs = slice(j * FF_CH, (j + 1) * FF_CH)
            a = jnp.maximum(_dot(h2b, w1_ref[:, cs]), 0.0)
            a2b = (a * a).astype(BF)
            f = f + _dot(a2b, w2_ref[cs, :])
            dpre = (_dot_nt(dfb, w2_ref[cs, :]) * (2.0 * a)).astype(BF)
            dh2 = dh2 + _dot_nt(dpre, w1_ref[:, cs])
            zpre_ref[:, cs] = dpre
            a2_ref[:, cs] = a2b
        zf_ref[...] = (dx2 * mod[5:6]).T.astype(BF)
        a1_ref[...] = h2.T.astype(BF)
        dx1, dmod, dnw = vjp_head(dh2)
        dx1_ref[...] = dx2 + dx1
        row = lax.broadcasted_iota(jnp.int32, (8, D), 0)
        dmod = dmod + jnp.where(row == 5, jnp.sum(dx2 * f, axis=0, keepdims=True), 0.0)
        first_kind = (i == 0) | (i == 1)

        @pl.when(first_kind)
        def _():
            dmod_ref[0] = dmod

        @pl.when(jnp.logical_not(first_kind))
        def _():
            dmod_ref[0] += dmod

        @pl.when(i == 0)
        def _():
            dnw_ref[...] = dnw

        @pl.when(i > 0)
        def _():
            dnw_ref[...] += dnw

    tspec = lambda w: pl.BlockSpec((TILE, w), lambda i: (i, 0))
    once = lambda shp: pl.BlockSpec(shp, lambda i: (0, 0), pipeline_mode=pl.Buffered(1))
    kind = pl.BlockSpec((1, 8, D), lambda i: (jnp.minimum(i, 1), 0, 0))
    tr = pl.BlockSpec((D, TILE), lambda i: (0, i))
    return pl.pallas_call(
        body, name=name, grid=(t_all // TILE,),
        in_specs=[tspec(D), kind, pl.BlockSpec((1, D), lambda i: (0, 0)), once((D, D_FF)), once((D_FF, D)), tspec(D)],
        out_specs=[tspec(D), kind, pl.BlockSpec((1, D), lambda i: (0, 0)), tspec(D_FF), tr, tr, tspec(D_FF)],
        out_shape=[jax.ShapeDtypeStruct((t_all, D), F32), jax.ShapeDtypeStruct((2, 8, D), F32),
                   jax.ShapeDtypeStruct((1, D), F32), jax.ShapeDtypeStruct((t_all, D_FF), BF),
                   jax.ShapeDtypeStruct((D, t_all), BF), jax.ShapeDtypeStruct((D, t_all), BF),
                   jax.ShapeDtypeStruct((t_all, D_FF), BF)],
        compiler_params=_cparams(1),
    )(x1, modl, n2w, w1, w2, dx2)


WG_TOK = 768


def wgrad(at, b, name, bk2=1024):
    k1, t = at.shape
    k2 = b.shape[1]
    bk2 = min(bk2, k2)
    tt = WG_TOK if t % WG_TOK == 0 else TILE
    nt_ = t // tt

    def body(a_ref, b_ref, o_ref, acc):
        s = pl.program_id(1)

        @pl.when(s == 0)
        def _():
            acc[...] = jnp.zeros_like(acc)

        acc[...] += _dot(a_ref[...], b_ref[...])

        @pl.when(s == nt_ - 1)
        def _():
            o_ref[...] = acc[...]

    return pl.pallas_call(
        body, name=name, grid=(k2 // bk2, nt_),
        in_specs=[pl.BlockSpec((k1, tt), lambda j, s: (0, s)), pl.BlockSpec((tt, bk2), lambda j, s: (s, j))],
        out_specs=pl.BlockSpec((k1, bk2), lambda j, s: (0, j)),
        out_shape=jax.ShapeDtypeStruct((k1, k2), F32),
        scratch_shapes=[pltpu.VMEM((k1, bk2), F32)],
        compiler_params=_cparams(2),
    )(at, b)


def _gla_consts(reverse, diff):
    r = lax.broadcasted_iota(jnp.int32, (GCH, GCH), 0)
    cc = lax.broadcasted_iota(jnp.int32, (GCH, GCH), 1)
    low = (r >= cc)
    tri = (jnp.logical_not(low) | (r == cc)) if reverse else low
    tri_f = tri.astype(F32)
    tri_t = (cc >= r) if not reverse else (cc <= r)
    hmk = [_lane_group_mask(128, 32, h) for h in range(HEADS)]
    hmv = [_lane_group_mask(256, 64, h) for h in range(HEADS)]
    e = lax.broadcasted_iota(jnp.int32, (256, 128), 0) // 64
    dk = lax.broadcasted_iota(jnp.int32, (256, 128), 1) // 32
    return dict(cum=make_cum(tri_f.astype(BF), tri_t.astype(F32).astype(BF), diff), ops=_ops(diff),
                tri4=jnp.concatenate([tri_f] * HEADS, axis=0), hmk=hmk, hmv=hmv, bd=(e == dk).astype(F32))


def gla_chunk(st, q, k, v, g, cs):
    nn, nt, tn = cs["ops"]
    b = cs["cum"](g)
    bl = jnp.sum(g, axis=0, keepdims=True)
    qe = q * jnp.exp(b)
    ke = k * jnp.exp(-b)
    qstack = jnp.concatenate([qe * cs["hmk"][h] for h in range(HEADS)], axis=0)
    att = nt(qstack, ke) * cs["tri4"]
    ofull = nn(att, v)
    o = nt(qe, st)
    for h in range(HEADS):
        o = o + ofull[h * GCH:(h + 1) * GCH] * cs["hmv"][h]
    kd = k * jnp.exp(bl - b)
    st_new = st * jnp.exp(bl) + tn(v, kd) * cs["bd"]
    return st_new, o


def _gla_chunk_index(s, n_ch, reverse):
    ctx_ch = TILE // GCH
    if not reverse:
        return s
    return jnp.where(s < ctx_ch, ctx_ch - 1 - s, n_ch - 1 + ctx_ch - s)


def gla_forward(q, k, v, gf, gb, name):
    t = q.shape[0]
    n_ch = t // GCH

    def body(*refs):
        s = pl.program_id(0)
        for dr, reverse in enumerate((False, True)):
            q_ref, k_ref, v_ref, g_ref = refs[4 * dr:4 * dr + 4]
            o_ref, sst_ref = refs[8 + 2 * dr:8 + 2 * dr + 2]
            st = refs[12 + dr]

            @pl.when(s == 0)
            def _(st=st):
                st[...] = jnp.zeros_like(st)

            cur = st[...]
            sst_ref[0] = cur
            st_new, o = gla_chunk(cur, q_ref[...], k_ref[...], v_ref[...], g_ref[...], _gla_consts(reverse, False))
            o_ref[...] = o
            st[...] = st_new

    in_specs, out_specs, out_shape = [], [], []
    for reverse in (False, True):
        im = lambda s, reverse=reverse: (_gla_chunk_index(s, n_ch, reverse), 0)
        im3 = lambda s, reverse=reverse: (_gla_chunk_index(s, n_ch, reverse), 0, 0)
        in_specs += [pl.BlockSpec((GCH, 128), im), pl.BlockSpec((GCH, 128), im), pl.BlockSpec((GCH, 256), im),
                     pl.BlockSpec((GCH, 128), im)]
        out_specs += [pl.BlockSpec((GCH, 256), im), pl.BlockSpec((1, 256, 128), im3)]
        out_shape += [jax.ShapeDtypeStruct((t, 256), F32), jax.ShapeDtypeStruct((n_ch, 256, 128), F32)]
    return pl.pallas_call(
        body, name=name, grid=(n_ch,), in_specs=in_specs, out_specs=out_specs, out_shape=out_shape,
        scratch_shapes=[pltpu.VMEM((256, 128), F32), pltpu.VMEM((256, 128), F32)],
        compiler_params=_cparams(1),
    )(q, k, v, gf, q, k, v, gb)


def gla_backward(q, k, v, gf, gb, sst_f, sst_b, do, name):
    t = q.shape[0]
    n_ch = t // GCH

    def body(*refs):
        r = pl.program_id(0)
        for dr, reverse in enumerate((False, True)):
            q_ref, k_ref, v_ref, g_ref, sst_ref, do_ref = refs[6 * dr:6 * dr + 6]
            outs = refs[12 + 4 * dr:12 + 4 * dr + 4]
            dst = refs[20 + dr]

            @pl.when(r == 0)
            def _(dst=dst):
                dst[...] = jnp.zeros_like(dst)

            cs = _gla_consts(reverse, True)
            _, vjp_fn = jax.vjp(lambda a, b, c_, d_, e_, cs=cs: gla_chunk(a, b, c_, d_, e_, cs),
                                sst_ref[0], q_ref[...], k_ref[...], v_ref[...], g_ref[...])
            grads = vjp_fn((dst[...], do_ref[...]))
            for o_ref, gval in zip(outs, grads[1:]):
                o_ref[...] = gval
            dst[...] = grads[0]

    in_specs, out_specs, out_shape = [], [], []
    for reverse in (False, True):
        im = lambda r, reverse=reverse: (_gla_chunk_index(n_ch - 1 - r, n_ch, reverse), 0)
        im3 = lambda r, reverse=reverse: (_gla_chunk_index(n_ch - 1 - r, n_ch, reverse), 0, 0)
        in_specs += [pl.BlockSpec((GCH, 128), im), pl.BlockSpec((GCH, 128), im), pl.BlockSpec((GCH, 256), im),
                     pl.BlockSpec((GCH, 128), im), pl.BlockSpec((1, 256, 128), im3), pl.BlockSpec((GCH, 256), im)]
        out_specs += [pl.BlockSpec((GCH, 128), im), pl.BlockSpec((GCH, 128), im), pl.BlockSpec((GCH, 256), im),
                      pl.BlockSpec((GCH, 128), im)]
        out_shape += [jax.ShapeDtypeStruct((t, 128), F32), jax.ShapeDtypeStruct((t, 128), F32),
                      jax.ShapeDtypeStruct((t, 256), F32), jax.ShapeDtypeStruct((t, 128), F32)]
    return pl.pallas_call(
        body, name=name, grid=(n_ch,), in_specs=in_specs, out_specs=out_specs, out_shape=out_shape,
        scratch_shapes=[pltpu.VMEM((256, 128), F32), pltpu.VMEM((256, 128), F32)],
        compiler_params=_cparams(1),
    )(q, k, v, gf, sst_f, do, q, k, v, gb, sst_b, do)


def _resident(hbm_ref, vmem_ref, sem):
    cp = pltpu.make_async_copy(hbm_ref, vmem_ref, sem)
    cp.start()
    cp.wait()


def mla_forward(q_cat, k_cat, v, name):
    t = q_cat.shape[0]
    n_t = t // TILE

    ch = KV_CH_FWD if (t - TILE) % KV_CH_FWD == 0 else KV_CH
    n_main = (t - TILE) // ch

    def body(q_ref, k_hbm, v_hbm, o_ref, lse_ref, k_s, v_s, m_s, l_s, acc_s, sem):
        i = pl.program_id(0)

        @pl.when(i == 0)
        def _():
            _resident(k_hbm, k_s, sem.at[0])
            _resident(v_hbm, v_s, sem.at[1])

        m_s[...] = jnp.full(m_s.shape, -1e30, F32)
        l_s[...] = jnp.zeros_like(l_s)
        acc_s[...] = jnp.zeros_like(acc_s)

        def chunk(r0, size):
            for h in range(HEADS):
                kh = k_s[pl.ds(r0, size), h * 256:(h + 1) * 256]
                vh = v_s[pl.ds(r0, size), h * 128:(h + 1) * 128]
                s = _dot_nt(q_ref[:, h * 256:(h + 1) * 256], kh)
                m_prev = m_s[h]
                m_next = jnp.maximum(m_prev, jnp.max(s, axis=-1, keepdims=True))
                p = jnp.exp2(s - jnp.tile(m_next, (1, size // 128)))
                alpha = jnp.exp2(m_prev - m_next)
                l_s[h] = alpha * l_s[h] + jnp.sum(p, axis=-1, keepdims=True)
                acc_s[h] = alpha * acc_s[h] + _dot(p.astype(BF), vh)
                m_s[h] = m_next

        chunk(0, TILE)

        @pl.when(i >= 1)
        def _():
            def step(c, carry):
                chunk(pl.multiple_of(TILE + c * ch, TILE), ch)
                return carry

            lax.fori_loop(0, n_main, step, 0, unroll=MLA_UNROLL)

        for h in range(HEADS):
            o_ref[:, h * 128:(h + 1) * 128] = acc_s[h] / l_s[h]
            lse_ref[:, h * 128:(h + 1) * 128] = m_s[h] + jnp.log2(l_s[h])

    return pl.pallas_call(
        body, name=name, grid=(n_t,),
        in_specs=[pl.BlockSpec((TILE, 1024), lambda i: (i, 0)), pl.BlockSpec(memory_space=pl.ANY),
                  pl.BlockSpec(memory_space=pl.ANY)],
        out_specs=[pl.BlockSpec((TILE, 512), lambda i: (i, 0)), pl.BlockSpec((TILE, 512), lambda i: (i, 0))],
        out_shape=[jax.ShapeDtypeStruct((t, 512), F32), jax.ShapeDtypeStruct((t, 512), F32)],
        scratch_shapes=[pltpu.VMEM((t, 1024), BF), pltpu.VMEM((t, 512), BF), pltpu.VMEM((HEADS, TILE, 128), F32),
                        pltpu.VMEM((HEADS, TILE, 128), F32), pltpu.VMEM((HEADS, TILE, 128), F32),
                        pltpu.SemaphoreType.DMA((2,))],
        compiler_params=_cparams(1),
    )(q_cat, k_cat, v)


def mla_delta(do, o, name):
    t = do.shape[0]

    def body(do_ref, o_ref, dl_ref, dob_ref):
        d = do_ref[...]
        prod = d * o_ref[...]
        rows = [jnp.sum(prod[:, h * 128:(h + 1) * 128], axis=-1, keepdims=True) for h in range(HEADS)]
        cols = jnp.concatenate(rows + [jnp.zeros((TILE, 128 - HEADS), F32)], axis=-1)
        dl_ref[...] = cols.T[0:8, :]
        dob_ref[...] = d.astype(BF)

    return pl.pallas_call(
        body, name=name, grid=(t // TILE,),
        in_specs=[pl.BlockSpec((TILE, 512), lambda i: (i, 0)), pl.BlockSpec((TILE, 512), lambda i: (i, 0))],
        out_specs=[pl.BlockSpec((8, TILE), lambda i: (0, i)), pl.BlockSpec((TILE, 512), lambda i: (i, 0))],
        out_shape=[jax.ShapeDtypeStruct((8, t), F32), jax.ShapeDtypeStruct((t, 512), BF)],
        compiler_params=_cparams(1),
    )(do, o)


def mla_backward(q_cat, k_cat, v, lse_rows, dl_rows, do_bf, name):
    t = q_cat.shape[0]
    n_t = t // TILE
    ch = Q_CH_BWD if (t - TILE) % Q_CH_BWD == 0 else KV_CH
    n_main = (t - TILE) // ch

    def body(q_hbm, do_hbm, k_ref, v_ref, lse_ref, dl_ref, dq_ref, dk_ref, dv_ref, q_s, do_s, dk_s, dv_s, sem):
        h, j = pl.program_id(0), pl.program_id(1)

        @pl.when(j == 0)
        def _():
            _resident(q_hbm.at[:, pl.ds(pl.multiple_of(h * 256, 256), 256)], q_s, sem.at[0])
            _resident(do_hbm.at[:, pl.ds(pl.multiple_of(h * 128, 128), 128)], do_s, sem.at[1])
            dq_ref[...] = jnp.zeros_like(dq_ref)

        dk_s[...] = jnp.zeros_like(dk_s)
        dv_s[...] = jnp.zeros_like(dv_s)
        kh = k_ref[...]
        vh = v_ref[...]

        def chunk(r0, size):
            qh = q_s[pl.ds(r0, size), :]
            doh = do_s[pl.ds(r0, size), :]
            pt = jnp.exp2(_dot_nt(kh, qh) - lse_ref[pl.ds(h, 1), pl.ds(r0, size)])
            dst = (pt * (_dot_nt(vh, doh) - dl_ref[pl.ds(h, 1), pl.ds(r0, size)])).astype(BF)
            dv_s[...] += _dot(pt.astype(BF), doh)
            dk_s[...] += _dot(dst, qh)
            dq_ref[pl.ds(r0, size), :] += _dot_tn(dst, kh)

        @pl.when(j == 0)
        def _():
            chunk(0, TILE)

        def step(c, carry):
            chunk(pl.multiple_of(TILE + c * ch, TILE), ch)
            return carry

        lax.fori_loop(0, n_main, step, 0, unroll=MLA_UNROLL)
        dk_ref[...] = dk_s[...] * LN2
        dv_ref[...] = dv_s[...]

        @pl.when(j == n_t - 1)
        def _():
            dq_ref[...] = dq_ref[...] * LN2

    rows = pl.BlockSpec((8, t), lambda h, j: (0, 0))
    hbm = pl.BlockSpec(memory_space=pl.ANY)
    return pl.pallas_call(
        body, name=name, grid=(HEADS, n_t),
        in_specs=[hbm, hbm, pl.BlockSpec((TILE, 256), lambda h, j: (j, h)), pl.BlockSpec((TILE, 128), lambda h, j: (j, h)),
                  rows, rows],
        out_specs=[pl.BlockSpec((t, 256), lambda h, j: (0, h)), pl.BlockSpec((TILE, 256), lambda h, j: (j, h)),
                   pl.BlockSpec((TILE, 128), lambda h, j: (j, h))],
        out_shape=[jax.ShapeDtypeStruct((t, 1024), F32), jax.ShapeDtypeStruct((t, 1024), F32),
                   jax.ShapeDtypeStruct((t, 512), F32)],
        scratch_shapes=[pltpu.VMEM((t, 256), BF), pltpu.VMEM((t, 128), BF), pltpu.VMEM((TILE, 256), F32),
                        pltpu.VMEM((TILE, 128), F32), pltpu.SemaphoreType.DMA((2,))],
        compiler_params=_cparams(2),
    )(q_cat, do_bf, k_cat, v, lse_rows, dl_rows)


def final_loss(xf, target, fnw, name):
    t = xf.shape[0]
    n_t = t // TILE

    def body(x_ref, t_ref, w_ref, loss_ref, dx_ref, dw_ref):
        i = pl.program_id(0)

        @pl.when(i == 0)
        def _():
            loss_ref[...] = jnp.zeros_like(loss_ref)
            dw_ref[...] = jnp.zeros_like(dw_ref)
            dx_ref[...] = jnp.zeros_like(dx_ref)

        @pl.when(i >= 1)
        def _():
            y, vjp_fn = jax.vjp(_rms, x_ref[...], w_ref[...])
            err = y - t_ref[...]
            loss_ref[...] += jnp.broadcast_to(0.5 * jnp.sum(jnp.mean(err * err, axis=-1, keepdims=True)), (8, 128))
            dx, dw = vjp_fn(err * (1.0 / D))
            dx_ref[...] = dx
            dw_ref[...] += dw

    return pl.pallas_call(
        body, name=name, grid=(n_t,),
        in_specs=[pl.BlockSpec((TILE, D), lambda i: (i, 0)), pl.BlockSpec((TILE, D), lambda i: (jnp.maximum(i - 1, 0), 0)),
                  pl.BlockSpec((1, D), lambda i: (0, 0))],
        out_specs=[pl.BlockSpec((8, 128), lambda i: (0, 0)), pl.BlockSpec((TILE, D), lambda i: (i, 0)),
                   pl.BlockSpec((1, D), lambda i: (0, 0))],
        out_shape=[jax.ShapeDtypeStruct((8, 128), F32), jax.ShapeDtypeStruct((t, D), F32),
                   jax.ShapeDtypeStruct((1, D), F32)],
        compiler_params=_cparams(1),
    )(xf, target, fnw)


def exchange(xs, name, scatter):
    n = len(xs)
    blks = [tuple(x.shape[1:] if scatter else x.shape) for x in xs]
    per = N_DEV - 1

    def body(*refs):
        x_refs, o_refs = refs[:n], refs[n:2 * n]
        ssem, rsem, lsem = refs[2 * n:]
        xi, yi, ci = lax.axis_index("x"), lax.axis_index("y"), lax.axis_index("c")
        me3 = (xi, yi, ci)
        me = 4 * xi + 2 * yi + ci
        flat = lambda d: 4 * d[0] + 2 * d[1] + d[2]

        if scatter:
            def peer(k):
                return ((1 - xi) if (k >> 2) & 1 else xi, (1 - yi) if (k >> 1) & 1 else yi, (1 - ci) if k & 1 else ci)

            def copy(a, k, lands):
                dev = peer(k)
                return pltpu.make_async_remote_copy(
                    src_ref=x_refs[a].at[flat(dev)], dst_ref=o_refs[a].at[flat(dev) if lands else me],
                    send_sem=ssem.at[a * per + k - 1], recv_sem=rsem.at[a * per + k - 1],
                    device_id=dev, device_id_type=pl.DeviceIdType.MESH)

            own = [pltpu.make_async_copy(x_refs[a].at[me], o_refs[a].at[me], lsem.at[a]) for a in range(n)]
            sends = [copy(a, k, False) for a in range(n) for k in range(1, N_DEV)]
            for cp in own + sends:
                cp.start()
            for a in range(n):
                for k in range(1, N_DEV):
                    copy(a, k, True).wait_recv()
            for cp in sends:
                cp.wait_send()
            for cp in own:
                cp.wait()
            return

        sibling = (xi, yi, 1 - ci)
        chips = [(1 - xi, yi), (xi, 1 - yi), (1 - xi, 1 - yi)]

        def copy(a, k, block, to, src=None):
            rows = o_refs[a].at[flat(block)]
            return pltpu.make_async_remote_copy(
                src_ref=rows if src is None else src, dst_ref=rows,
                send_sem=ssem.at[a * per + k], recv_sem=rsem.at[a * per + k],
                device_id=to, device_id_type=pl.DeviceIdType.MESH)

        own = [pltpu.make_async_copy(x_refs[a], o_refs[a].at[me], lsem.at[a]) for a in range(n)]
        first = []
        for a in range(n):
            first.append(copy(a, 0, me3, sibling, src=x_refs[a]))
            first += [copy(a, 1 + j, me3, (*chip, ci), src=x_refs[a]) for j, chip in enumerate(chips)]
        for cp in own + first:
            cp.start()
        passed = []
        for j, chip in enumerate(chips):
            for a in range(n):
                copy(a, 1 + j, (*chip, ci), me3).wait_recv()
                fw = copy(a, 4 + j, (*chip, ci), sibling)
                fw.start()
                passed.append(fw)
        for a in range(n):
            copy(a, 0, sibling, me3).wait_recv()
            for j, chip in enumerate(chips):
                copy(a, 4 + j, (*chip, 1 - ci), me3).wait_recv()
        for cp in first + passed:
            cp.wait_send()
        for cp in own:
            cp.wait()

    hbm = pl.BlockSpec(memory_space=pl.ANY)
    res = pl.pallas_call(
        body, name=name, in_specs=[hbm] * n, out_specs=[hbm] * n,
        out_shape=[jax.ShapeDtypeStruct((N_DEV,) + blks[a], xs[a].dtype) for a in range(n)],
        scratch_shapes=[pltpu.SemaphoreType.DMA((n * per,)), pltpu.SemaphoreType.DMA((n * per,)),
                        pltpu.SemaphoreType.DMA((n,))],
        compiler_params=pltpu.CompilerParams(has_side_effects=True),
    )(*xs)
    return list(res)


def mod_forward(crows, w_mod, b_shard, name):
    cols = w_mod.shape[2]

    def body(c_ref, w_ref, b_ref, o_ref):
        o_ref[0] = _dot(_silu(c_ref[...]).astype(BF), w_ref[0].astype(BF)) + b_ref[0]

    return pl.pallas_call(
        body, name=name, grid=(2,),
        in_specs=[pl.BlockSpec((16, D), lambda l: (0, 0)), pl.BlockSpec((1, D, cols), lambda l: (l, 0, 0)),
                  pl.BlockSpec((1, 1, cols), lambda l: (l, 0, 0))],
        out_specs=pl.BlockSpec((1, 16, cols), lambda l: (l, 0, 0)),
        out_shape=jax.ShapeDtypeStruct((2, 16, cols), F32), compiler_params=_cparams(1),
    )(crows, w_mod, b_shard)


def mod_backward(crows, w_mod, d_own, d_ctx, name):
    cols = w_mod.shape[2]

    def body(c_ref, w_ref, do_ref, dc_ref, gw_ref, gs_ref):
        dc = dc_ref[0]
        dsum = dc[0:1]
        for s in range(1, N_DEV):
            dsum = dsum + dc[s:s + 1]
        row = lax.broadcasted_iota(jnp.int32, (8, cols), 0)
        d16 = jnp.concatenate([do_ref[0], jnp.where(row == 0, jnp.broadcast_to(dsum, (8, cols)), 0.0)], axis=0)
        gw_ref[0] = _dot_tn(_silu(c_ref[...]).astype(BF), d16.astype(BF))
        gs_ref[0] = _dot_nt(jnp.broadcast_to(dsum, (8, cols)).astype(BF), w_ref[0].astype(BF))

    return pl.pallas_call(
        body, name=name, grid=(2,),
        in_specs=[pl.BlockSpec((16, D), lambda l: (0, 0)), pl.BlockSpec((1, D, cols), lambda l: (l, 0, 0)),
                  pl.BlockSpec((1, 8, cols), lambda l: (l, 0, 0)), pl.BlockSpec((1, 8, cols), lambda l: (l, 0, 0))],
        out_specs=[pl.BlockSpec((1, D, cols), lambda l: (l, 0, 0)), pl.BlockSpec((1, 8, D), lambda l: (l, 0, 0))],
        out_shape=[jax.ShapeDtypeStruct((2, D, cols), F32), jax.ShapeDtypeStruct((2, 8, D), F32)],
        compiler_params=_cparams(1),
    )(crows, w_mod, d_own, d_ctx)


def silu_grad_scale(c_ctx, ds, name):
    def body(c_ref, ds_ref, o_ref):
        cc = c_ref[...]
        sg = jax.nn.sigmoid(cc)
        o_ref[...] = (ds_ref[0][0:1] + ds_ref[1][0:1]) * (sg * (1.0 + cc * (1.0 - sg)))

    return pl.pallas_call(body, name=name, out_shape=jax.ShapeDtypeStruct((1, D), F32))(c_ctx, ds)


def _adamw_math(p_ref, w_ref, m_ref, v_ref, g_ref, d_ref, nm_ref, nv_ref):
    g = p_ref[0].astype(F32)
    for s in range(1, p_ref.shape[0]):
        g = g + p_ref[s].astype(F32)
    mm = ADAM_B1 * m_ref[...] + (1.0 - ADAM_B1) * g
    vv = ADAM_B2 * v_ref[...] + (1.0 - ADAM_B2) * (g * g)
    m_hat = mm / (1.0 - ADAM_B1 ** ADAM_STEP)
    v_hat = vv / (1.0 - ADAM_B2 ** ADAM_STEP)
    g_ref[...] = g
    d_ref[...] = -ADAM_LR * (m_hat / (jnp.sqrt(v_hat) + ADAM_EPS) + ADAM_WD * w_ref[...])
    nm_ref[...] = mm
    nv_ref[...] = vv


def adamw(parts, w, m, v, name):
    n_parts, rows, cols = parts.shape
    lanes = -(-cols // 128) * 128
    block_rows = min(rows, 1 << ((ADAMW_BLOCK_ELEMS // lanes).bit_length() - 1))
    assert rows % block_rows == 0

    def body(*refs):
        _adamw_math(*refs)

    spec = pl.BlockSpec((block_rows, cols), lambda i: (i, 0))
    return pl.pallas_call(
        body, name=name, grid=(rows // block_rows,),
        in_specs=[pl.BlockSpec((n_parts, block_rows, cols), lambda i: (0, i, 0)), spec, spec, spec],
        out_specs=[spec] * 4, out_shape=[jax.ShapeDtypeStruct((rows, cols), F32)] * 4,
        compiler_params=_cparams(1),
    )(parts, w, m, v)


def adamw_group(items, name):
    n = len(items)

    def body(*refs):
        for a in range(n):
            _adamw_math(*refs[4 * a:4 * a + 4], *refs[4 * n + 4 * a:4 * n + 4 * a + 4])

    flat_in = [x for it in items for x in it]
    out_shape = [jax.ShapeDtypeStruct(it[1].shape, F32) for it in items for _ in range(4)]
    res = pl.pallas_call(body, name=name, out_shape=out_shape,
                         compiler_params=pltpu.CompilerParams(vmem_limit_bytes=VMEM_LIMIT_MB * 1024 * 1024))(*flat_in)
    return [tuple(res[4 * a:4 * a + 4]) for a in range(n)]


def _pad_cols(w, segs, total):
    parts, pos = [], 0
    for dst, src, wd in segs:
        if dst > pos:
            parts.append(jnp.zeros(w.shape[:-1] + (dst - pos,), w.dtype))
        parts.append(w[..., src:src + wd])
        pos = dst + wd
    if pos < total:
        parts.append(jnp.zeros(w.shape[:-1] + (total - pos,), w.dtype))
    return jnp.concatenate(parts, axis=-1)


def _unpad_cols(g, segs):
    return jnp.concatenate([g[..., dst:dst + wd] for dst, _, wd in segs], axis=-1)


def _rope_tables(n_lat):
    rows = n_lat // GRID_W
    row = jnp.repeat(jnp.arange(rows), GRID_W).astype(F32)
    col = jnp.tile(jnp.arange(GRID_W), rows).astype(F32)
    freq = ROPE_BASE ** (-jnp.arange(16, dtype=F32) * 2.0 / 32)
    ar, ac = row[:, None] * freq[None, :], col[:, None] * freq[None, :]
    z = jnp.zeros((n_lat, 16), F32)
    cos = jnp.concatenate([jnp.cos(ar), jnp.cos(ar), jnp.cos(ac), jnp.cos(ac), jnp.ones((n_lat, 64), F32)], axis=1)
    sa = jnp.concatenate([-jnp.sin(ar), z, -jnp.sin(ac), z, jnp.zeros((n_lat, 64), F32)], axis=1)
    sb = jnp.concatenate([z, jnp.sin(ar), z, jnp.sin(ac), jnp.zeros((n_lat, 64), F32)], axis=1)
    ident = lambda fill: jnp.full((TILE, 128), fill, F32)
    return (jnp.concatenate([ident(1.0), cos]), jnp.concatenate([ident(0.0), sa]), jnp.concatenate([ident(0.0), sb]))


def _gathered_to_full(g, name):
    if name in ("w_out", "w_ff2"):
        return jnp.transpose(g, (1, 0, 2, 3)).reshape(2, -1, g.shape[-1])
    return jnp.transpose(g, (1, 2, 0, 3)).reshape(2, g.shape[2], -1)


def _full_to_chunks(gw, name):
    if name in ("w_out", "w_ff2"):
        return jnp.transpose(gw.reshape(2, N_DEV, -1, gw.shape[-1]), (1, 0, 2, 3))
    return jnp.transpose(gw.reshape(2, gw.shape[1], N_DEV, -1), (2, 0, 1, 3))


def kernel(x, c, ctx, c_ctx, w_mod, b_mod, norm1_w, w_in, w_out, sgu_norm_w, sgu_norm_b, sgu_w, sgu_b, gla_wg_fwd, gla_bg_fwd, gla_wg_bwd, gla_bg_bwd, gla_norm_w, mla_q_norm_w, mla_w_uq, mla_kv_norm_w, mla_w_ukv, norm2_w, w_ff1, w_ff2, final_norm_w, loss_target, m_c_ctx, m_w_mod, m_b_mod, m_norm1_w, m_w_in, m_w_out, m_sgu_norm_w, m_sgu_norm_b, m_sgu_w, m_sgu_b, m_gla_wg_fwd, m_gla_bg_fwd, m_gla_wg_bwd, m_gla_bg_bwd, m_gla_norm_w, m_mla_q_norm_w, m_mla_w_uq, m_mla_kv_norm_w, m_mla_w_ukv, m_norm2_w, m_w_ff1, m_w_ff2, m_final_norm_w, v_c_ctx, v_w_mod, v_b_mod, v_norm1_w, v_w_in, v_w_out, v_sgu_norm_w, v_sgu_norm_b, v_sgu_w, v_sgu_b, v_gla_wg_fwd, v_gla_bg_fwd, v_gla_wg_bwd, v_gla_bg_bwd, v_gla_norm_w, v_mla_q_norm_w, v_mla_w_uq, v_mla_kv_norm_w, v_mla_w_ukv, v_norm2_w, v_w_ff1, v_w_ff2, v_final_norm_w):
    W = dict(c_ctx=c_ctx, w_mod=w_mod, b_mod=b_mod, norm1_w=norm1_w, w_in=w_in, w_out=w_out, sgu_norm_w=sgu_norm_w,
             sgu_norm_b=sgu_norm_b, sgu_w=sgu_w, sgu_b=sgu_b, gla_wg_fwd=gla_wg_fwd, gla_bg_fwd=gla_bg_fwd,
             gla_wg_bwd=gla_wg_bwd, gla_bg_bwd=gla_bg_bwd, gla_norm_w=gla_norm_w, mla_q_norm_w=mla_q_norm_w,
             mla_w_uq=mla_w_uq, mla_kv_norm_w=mla_kv_norm_w, mla_w_ukv=mla_w_ukv, norm2_w=norm2_w, w_ff1=w_ff1,
             w_ff2=w_ff2, final_norm_w=final_norm_w)
    M = dict(c_ctx=m_c_ctx, w_mod=m_w_mod, b_mod=m_b_mod, norm1_w=m_norm1_w, w_in=m_w_in, w_out=m_w_out,
             sgu_norm_w=m_sgu_norm_w, sgu_norm_b=m_sgu_norm_b, sgu_w=m_sgu_w, sgu_b=m_sgu_b, gla_wg_fwd=m_gla_wg_fwd,
             gla_bg_fwd=m_gla_bg_fwd, gla_wg_bwd=m_gla_wg_bwd, gla_bg_bwd=m_gla_bg_bwd, gla_norm_w=m_gla_norm_w,
             mla_q_norm_w=m_mla_q_norm_w, mla_w_uq=m_mla_w_uq, mla_kv_norm_w=m_mla_kv_norm_w, mla_w_ukv=m_mla_w_ukv,
             norm2_w=m_norm2_w, w_ff1=m_w_ff1, w_ff2=m_w_ff2, final_norm_w=m_final_norm_w)
    V = dict(c_ctx=v_c_ctx, w_mod=v_w_mod, b_mod=v_b_mod, norm1_w=v_norm1_w, w_in=v_w_in, w_out=v_w_out,
             sgu_norm_w=v_sgu_norm_w, sgu_norm_b=v_sgu_norm_b, sgu_w=v_sgu_w, sgu_b=v_sgu_b, gla_wg_fwd=v_gla_wg_fwd,
             gla_bg_fwd=v_gla_bg_fwd, gla_wg_bwd=v_gla_wg_bwd, gla_bg_bwd=v_gla_bg_bwd, gla_norm_w=v_gla_norm_w,
             mla_q_norm_w=v_mla_q_norm_w, mla_w_uq=v_mla_w_uq, mla_kv_norm_w=v_mla_kv_norm_w, mla_w_ukv=v_mla_w_ukv,
             norm2_w=v_norm2_w, w_ff1=v_w_ff1, w_ff2=v_w_ff2, final_norm_w=v_final_norm_w)

    n_lat = x.shape[1]
    assert ctx.shape[1] == TILE and n_lat % TILE == 0 and x.shape[2] == D
    t_all = TILE + n_lat
    n_t = t_all // TILE
    me = 4 * lax.axis_index("x") + 2 * lax.axis_index("y") + lax.axis_index("c")
    mod_cols = w_mod.shape[2]

    c_all = exchange([c], "ag_c", scatter=False)[0].reshape(N_DEV, D)
    crows = jnp.concatenate([c_all, c_ctx[None, :], jnp.zeros((7, D), F32)], axis=0)
    b_shard = lax.dynamic_slice_in_dim(b_mod, me * mod_cols, mod_cols, axis=1)[:, None, :]
    mod_sh = mod_forward(crows, w_mod, b_shard, "mod_fwd")
    mod_g = exchange([mod_sh.reshape(32, mod_cols)], "ag_mod", scatter=False)[0]
    mod_full = jnp.transpose(mod_g.reshape(N_DEV, 2, 16, mod_cols), (1, 2, 0, 3)).reshape(2, 16, 6 * D)
    mod_own = lax.dynamic_index_in_dim(mod_full, me, axis=1, keepdims=False)
    mod_ctx = mod_full[:, 8, :]
    pad2 = jnp.zeros((2, D), F32)
    modl = [jnp.stack([jnp.concatenate([mod_ctx[l].reshape(6, D), pad2]),
                       jnp.concatenate([mod_own[l].reshape(6, D), pad2])]) for l in range(2)]

    v2 = lambda a: a[None, :] if a.ndim == 1 else a.reshape(-1, a.shape[-1])
    gathered = exchange([v2(W[k].astype(BF)) for k in BIG_NAMES], "ag_weights", scatter=False)
    full = {k: _gathered_to_full(g.reshape((N_DEV,) + W[k].shape), k) for k, g in zip(BIG_NAMES, gathered)}
    w_in_p = _pad_cols(full["w_in"], W_IN_SEGS, P_COLS)
    w_uq_p = _pad_cols(full["mla_w_uq"], W_UQ_SEGS, 1024).astype(F32)
    w_ukv_f = full["mla_w_ukv"].astype(F32)
    wgf_p = jnp.pad(gla_wg_fwd, ((0, 0), (0, 112), (0, 0)))
    wgb_p = jnp.pad(gla_wg_bwd, ((0, 0), (0, 112), (0, 0)))
    sgu_bx = jnp.repeat(jnp.transpose(sgu_b, (0, 2, 1)), 64, axis=2)
    gnw_t = jnp.tile(gla_norm_w, (1, HEADS))
    rc, rsa, rsb = _rope_tables(n_lat)

    xin = jnp.concatenate([ctx[0], x[0]], axis=0)
    row = lambda a: a[None, :]

    def pre_ins(l, xl):
        return [("x", "tile", True, xl), ("mod", "kind", True, modl[l]), ("n1w", "full", True, row(norm1_w[l])),
                ("w_in", "wfull", False, w_in_p[l]), ("sgu_nw", "full", True, row(sgu_norm_w[l])),
                ("sgu_nb", "full", True, row(sgu_norm_b[l])), ("sgu_w", "full", True, sgu_w[l]),
                ("sgu_bx", "full", True, sgu_bx[l]), ("wgf", "full", True, wgf_p[l]), ("bgf", "full", True, row(gla_bg_fwd[l])),
                ("wgb", "full", True, wgb_p[l]), ("bgb", "full", True, row(gla_bg_bwd[l])),
                ("qnw", "full", True, row(mla_q_norm_w[l])), ("w_uq", "full", True, w_uq_p[l]),
                ("kvnw", "full", True, row(mla_kv_norm_w[l])), ("w_ukv", "full", True, w_ukv_f[l]),
                ("rc", "tile", False, rc), ("rsa", "tile", False, rsa), ("rsb", "tile", False, rsb)]

    pre_outs = [("y_sgu", 256, F32), ("qg", 128, F32), ("kg", 128, F32), ("vg", 256, F32), ("lgf", 128, F32),
                ("lgb", 128, F32), ("gr", 256, F32), ("q_cat", 1024, BF), ("k_cat", 1024, BF), ("v", 512, BF)]

    def out_ins(l, xl, a):
        return [("x", "tile", True, xl), ("mod", "kind", True, modl[l]), ("y_sgu", "tile", True, a["y_sgu"]),
                ("o_f", "tile", True, a["o_f"]), ("o_b", "tile", True, a["o_b"]), ("gr", "tile", True, a["gr"]),
                ("y_mla", "tile", True, a["y_mla"]), ("gnw", "full", True, row(gnw_t[l])),
                ("w_out", "wfull", False, full["w_out"][l])]

    def ffn_ins(l, x1):
        return [("x1", "tile", True, x1), ("mod", "kind", True, modl[l]), ("n2w", "full", True, row(norm2_w[l])),
                ("w_ff1", "wcols", False, full["w_ff1"][l]), ("w_ff2", "wrows", False, full["w_ff2"][l])]

    saved, xl = [], xin
    for l in range(2):
        a = tile_forward(pre_tile, f"pre_fwd{l}", t_all, pre_ins(l, xl), pre_outs)
        a["o_f"], a["sf"], a["o_b"], a["sb"] = gla_forward(a["qg"], a["kg"], a["vg"], a["lgf"], a["lgb"], f"gla_fwd{l}")
        a["y_mla"], a["lse"] = mla_forward(a["q_cat"], a["k_cat"], a["v"], f"mla_fwd{l}")
        a["x"] = xl
        a["x1"] = tile_forward(attn_out_tile, f"out_fwd{l}", t_all, out_ins(l, xl, a), [("x1", D, F32)])["x1"]
        xl = tile_forward(ffn_tile, f"ffn_fwd{l}", t_all, ffn_ins(l, a["x1"]), [("x2", D, F32)])["x2"]
        saved.append(a)

    loss_blk, dx, d_fnw = final_loss(xl, loss_target[0], row(final_norm_w), "final_loss")
    loss = lax.psum(loss_blk[0, 0], AXES)

    G = {}
    dmods = []
    for l in (1, 0):
        a = saved[l]
        dx1, dmod3, dn2w, zpre, zf_t, h2_t, a_ff2 = ffn_backward(
            a["x1"], modl[l], row(norm2_w[l]), full["w_ff1"][l], full["w_ff2"][l], dx, f"ffn_bwd{l}")
        gw_ff1 = wgrad(h2_t, zpre, f"wg_ff1_{l}")
        gw_ff2 = jnp.transpose(wgrad(zf_t, a_ff2, f"wg_ff2_{l}"))
        g2, e2 = tile_backward(attn_out_tile, f"out_bwd{l}", t_all, out_ins(l, a["x"], a), [("x1", dx1)],
                               [("zt", D)], [("a_out", D)])
        gw_out = wgrad(e2["a_out"], e2["zt"], f"wg_out_{l}")
        dl_rows, do_bf = mla_delta(g2["y_mla"], a["y_mla"], f"mla_delta{l}")
        lse_rows = jnp.pad(jnp.transpose(a["lse"][:, ::128]), ((0, 8 - HEADS), (0, 0)))
        dq_cat, dk_cat, dv = mla_backward(a["q_cat"], a["k_cat"], a["v"], lse_rows, dl_rows, do_bf, f"mla_bwd{l}")
        dqf, dkf, dvf, dgf, dqb, dkb, dvb, dgb = gla_backward(
            a["qg"], a["kg"], a["vg"], a["lgf"], a["lgb"], a["sf"], a["sb"], g2["o_f"], f"gla_bwd{l}")
        cots = [("y_sgu", g2["y_sgu"]), ("qg", [dqf, dqb]), ("kg", [dkf, dkb]), ("vg", [dvf, dvb]), ("lgf", dgf),
                ("lgb", dgb), ("gr", g2["gr"]), ("q_cat", dq_cat), ("k_cat", dk_cat), ("v", dv)]
        g1, e1 = tile_backward(pre_tile, f"pre_bwd{l}", t_all, pre_ins(l, a["x"]), cots, [("zp", P_COLS)], [("a_in", D)],
                               resid=("x", g2["x"]))
        gw_in = _unpad_cols(wgrad(e1["a_in"], e1["zp"], f"wg_in_{l}", bk2=P_COLS), W_IN_SEGS)
        dx = g1["x"]
        dmods.append(g1["mod"] + g2["mod"] + dmod3)
        G[l] = dict(w_in=gw_in, w_out=gw_out, w_ff1=gw_ff1, w_ff2=gw_ff2,
                    mla_w_uq=_unpad_cols(g1["w_uq"], W_UQ_SEGS), mla_w_ukv=g1["w_ukv"],
                    norm1_w=g1["n1w"][0], norm2_w=dn2w[0], sgu_norm_w=g1["sgu_nw"][0], sgu_norm_b=g1["sgu_nb"][0],
                    sgu_w=g1["sgu_w"], sgu_b=jnp.transpose(g1["sgu_bx"].reshape(128, HEADS, 64).sum(-1)),
                    gla_wg_fwd=g1["wgf"][:16], gla_bg_fwd=g1["bgf"][0], gla_wg_bwd=g1["wgb"][:16], gla_bg_bwd=g1["bgb"][0],
                    gla_norm_w=g2["gnw"][0].reshape(HEADS, 64).sum(0), mla_q_norm_w=g1["qnw"][0], mla_kv_norm_w=g1["kvnw"][0])
    dmods = dmods[::-1]
    grad_x = dx[TILE:][None]

    dmod_pack = jnp.stack([jnp.stack([dmods[l][1, :6].reshape(-1), dmods[l][0, :6].reshape(-1)]) for l in range(2)])
    dmod_all = exchange([dmod_pack.reshape(4, 6 * D)], "ag_dmod", scatter=False)[0].reshape(N_DEV, 2, 2, 6 * D)
    dsl = lax.dynamic_slice_in_dim(dmod_all, me * mod_cols, mod_cols, axis=3)
    d_own = jnp.transpose(dsl[:, :, 0, :], (1, 0, 2))
    d_ctx = jnp.transpose(dsl[:, :, 1, :], (1, 0, 2))
    g_w_mod, ds_cc = mod_backward(crows, w_mod, d_own, d_ctx, "mod_bwd")
    g_c_ctx_part = silu_grad_scale(c_ctx[None, :], ds_cc, "silu_bwd")[0]
    g_b_mod_part = jnp.stack([dmods[l][1, :6].reshape(-1) + dmods[l][0, :6].reshape(-1) for l in range(2)])

    small_g = dict(c_ctx=g_c_ctx_part, b_mod=g_b_mod_part, final_norm_w=d_fnw[0])
    for k in SMALL_NAMES:
        if k not in small_g:
            small_g[k] = jnp.stack([G[0][k], G[1][k]])
    res = {}
    sparts = exchange([v2(small_g[k]) for k in SMALL_NAMES], "ag_small", scatter=False)
    s_out = adamw_group([(sparts[j], v2(W[k]), v2(M[k]), v2(V[k])) for j, k in enumerate(SMALL_NAMES)], "adamw_small")
    for j, k in enumerate(SMALL_NAMES):
        res[k] = [o.reshape(W[k].shape) for o in s_out[j]]

    chunks = []
    for k in BIG_NAMES:
        ch = _full_to_chunks(jnp.stack([G[0][k], G[1][k]]), k).astype(BF)
        chunks.append(ch.reshape(N_DEV, -1, ch.shape[-1]))
    bparts = exchange(chunks, "a2a_grads", scatter=True)
    for j, k in enumerate(BIG_NAMES):
        res[k] = [o.reshape(W[k].shape) for o in adamw(bparts[j], v2(W[k]), v2(M[k]), v2(V[k]), f"adamw_{k}")]
    res["w_mod"] = [o.reshape(w_mod.shape)
                    for o in adamw(v2(g_w_mod)[None], v2(w_mod), v2(m_w_mod), v2(v_w_mod), "adamw_w_mod")]
    outs = [loss, grad_x]
    for j in range(4):
        outs += [res[k][j] for k in WEIGHT_ORDER]
    return tuple(outs)
```

```python
import jax
import jax.numpy as jnp
from jax import lax
from jax.experimental import pallas as pl
from jax.experimental.pallas import tpu as pltpu

F32 = jnp.float32
BF = jnp.bfloat16

N_DEV = 8
AXES = ("x", "y", "c")
EPS = 1e-6
D = 1024
TILE = 256
GCH = 128
SGU_CHUNK = 128
HEADS = 4
ROPE_BASE = 10000.0
GRID_W = 64
GLA_TAU = 16.0
ATT_SCALE = (128 + 64) ** -0.5
ATT_SCALE_LOG2 = ATT_SCALE * 1.4426950408889634
LN2 = 0.6931471805599453
KV_CH = 512
KV_CH_FWD = 2048
Q_CH_BWD = 1024
MLA_UNROLL = 2
D_FF = 4096
FF_CH = 1024

ADAM_LR = 0.001
ADAM_B1 = 0.9
ADAM_B2 = 0.999
ADAM_EPS = 1e-08
ADAM_WD = 0.01
ADAM_STEP = 10

VMEM_LIMIT_MB = 56
ADAMW_BLOCK_ELEMS = 256 * 1024

W_IN_SEGS = ((0, 0, 128), (128, 128, 256), (384, 384, 16), (512, 400, 16), (640, 416, 256), (896, 672, 64),
             (1024, 736, 256), (1280, 992, 256), (1536, 1248, 128), (1664, 1376, 256), (1920, 1632, 256))
P_COLS = 2176
O_GK, O_GV, O_GGF, O_GGB, O_CKV, O_KR, O_SU, O_SV, O_GQ, O_GR, O_DQ = (s[0] for s in W_IN_SEGS)
W_UQ_SEGS = tuple((h * 256, h * 192, 192) for h in range(HEADS))

SMALL_NAMES = ("c_ctx", "b_mod", "norm1_w", "sgu_norm_w", "sgu_norm_b", "sgu_w", "sgu_b", "gla_wg_fwd", "gla_bg_fwd",
               "gla_wg_bwd", "gla_bg_bwd", "gla_norm_w", "mla_q_norm_w", "mla_kv_norm_w", "norm2_w", "final_norm_w")
BIG_NAMES = ("w_in", "w_out", "mla_w_uq", "mla_w_ukv", "w_ff1", "w_ff2")
WEIGHT_ORDER = ("c_ctx", "w_mod", "b_mod", "norm1_w", "w_in", "w_out", "sgu_norm_w", "sgu_norm_b", "sgu_w", "sgu_b",
                "gla_wg_fwd", "gla_bg_fwd", "gla_wg_bwd", "gla_bg_bwd", "gla_norm_w", "mla_q_norm_w", "mla_w_uq",
                "mla_kv_norm_w", "mla_w_ukv", "norm2_w", "w_ff1", "w_ff2", "final_norm_w")


def _cparams(n_axes):
    return pltpu.CompilerParams(dimension_semantics=("arbitrary",) * n_axes,
                                vmem_limit_bytes=VMEM_LIMIT_MB * 1024 * 1024)


def _dot(a, b):
    return jnp.dot(a, b, preferred_element_type=F32)


def _dot_nt(a, b):
    return lax.dot_general(a, b, (((1,), (1,)), ((), ())), preferred_element_type=F32)


def _dot_tn(a, b):
    return lax.dot_general(a, b, (((0,), (0,)), ((), ())), preferred_element_type=F32)


def _nn(a, b):
    return _dot(a.astype(BF), b.astype(BF))


def _nt(a, b):
    return _dot_nt(a.astype(BF), b.astype(BF))


def _tn(a, b):
    return _dot_tn(a.astype(BF), b.astype(BF))


nn_d = jax.custom_vjp(_nn)
nt_d = jax.custom_vjp(_nt)
tn_d = jax.custom_vjp(_tn)
nn_d.defvjp(lambda a, b: (_nn(a, b), (a, b)), lambda r, dy: (_nt(dy, r[1]), _tn(r[0], dy)))
nt_d.defvjp(lambda a, b: (_nt(a, b), (a, b)), lambda r, dy: (_nn(dy, r[1]), _tn(dy, r[0])))
tn_d.defvjp(lambda a, b: (_tn(a, b), (a, b)), lambda r, dy: (_nt(r[1], dy), _nn(r[0], dy)))


def nn_const(w_bf, diff):
    def raw(a):
        return _dot(a.astype(BF), w_bf)

    if not diff:
        return raw
    f = jax.custom_vjp(raw)
    f.defvjp(lambda a: (raw(a), None), lambda _, dy: (_dot_nt(dy.astype(BF), w_bf),))
    return f


def _split3(g):
    hi = g.astype(BF)
    r = g - hi.astype(F32)
    mid = r.astype(BF)
    lo = (r - mid.astype(F32)).astype(BF)
    return hi, mid, lo


def make_cum(tri_bf, tri_t_bf, diff):
    def raw(g, t):
        hi, mid, lo = _split3(g)
        return _dot(t, hi) + _dot(t, mid) + _dot(t, lo)

    def fwd(g):
        return raw(g, tri_bf)

    if not diff:
        return fwd
    cum = jax.custom_vjp(fwd)
    cum.defvjp(lambda g: (fwd(g), None), lambda _, db: (raw(db, tri_t_bf),))
    return cum


def _roll_lanes(x, shift):
    return pltpu.roll(x, shift, 1)


def make_rope(c, sa, sb, diff):
    def raw(x):
        return x * c + _roll_lanes(x, 112) * sa + _roll_lanes(x, 16) * sb

    if not diff:
        return raw
    f = jax.custom_vjp(raw)
    f.defvjp(lambda x: (raw(x), None),
             lambda _, dy: (dy * c + _roll_lanes(dy * sa, 16) + _roll_lanes(dy * sb, 112),))
    return f


def _ops(diff):
    return (nn_d, nt_d, tn_d) if diff else (_nn, _nt, _tn)


def _rms(x, w):
    return x * lax.rsqrt(jnp.mean(x * x, axis=-1, keepdims=True) + EPS) * w


def _gelu(x):
    return 0.5 * x * (1.0 + jnp.tanh(0.7978845608028654 * (x + 0.044715 * (x * x * x))))


def _silu(x):
    return x * jax.nn.sigmoid(x)


def _log_sigmoid(z):
    return jnp.minimum(z, 0.0) - jnp.log(1.0 + jnp.exp(-jnp.abs(z)))


def _lane_group_mask(width, group, h):
    lane = lax.broadcasted_iota(jnp.int32, (1, width), 1)
    return ((lane >= h * group) & (lane < (h + 1) * group)).astype(F32)


def pre_tile(d, c, z):
    nn, _, _ = _ops(z is not None)
    rope = make_rope(c["rc"], c["rsa"], c["rsb"], z is not None)
    mod = d["mod"]
    h = _rms(d["x"], d["n1w"]) * (1.0 + mod[1:2]) + mod[0:1]
    p = nn_const(c["w_in"], z is not None)(h)
    if z is not None:
        p = p + z["zp"]
    gk, gv = p[:, O_GK:O_GK + 128], p[:, O_GV:O_GV + 256]
    ggf, ggb = p[:, O_GGF:O_GGF + 128], p[:, O_GGB:O_GGB + 128]
    ckv, kr = p[:, O_CKV:O_CKV + 256], p[:, O_KR:O_KR + 128]
    su, sv = p[:, O_SU:O_SU + 256], p[:, O_SV:O_SV + 256]
    gq, gr, dq = p[:, O_GQ:O_GQ + 128], p[:, O_GR:O_GR + 256], p[:, O_DQ:O_DQ + 256]

    u = _gelu(su)
    gv_ = _gelu(sv)
    mu = jnp.mean(gv_, axis=-1, keepdims=True)
    cen = gv_ - mu
    vn = cen * lax.rsqrt(jnp.mean(cen * cen, axis=-1, keepdims=True) + EPS) * d["sgu_nw"] + d["sgu_nb"]
    hm = [_lane_group_mask(256, 64, hh) for hh in range(HEADS)]
    rows = []
    for ci in range(vn.shape[0] // SGU_CHUNK):
        vc = vn[ci * SGU_CHUNK:(ci + 1) * SGU_CHUNK]
        s = d["sgu_bx"]
        for hh in range(HEADS):
            s = s + hm[hh] * nn(d["sgu_w"][hh], vc)
        rows.append(s)
    y_sgu = u * jnp.concatenate(rows, axis=0)

    qg = gq * (32 ** -0.5)
    lgf = _log_sigmoid(nn(ggf, d["wgf"]) + d["bgf"]) * (1.0 / GLA_TAU)
    lgb = _log_sigmoid(nn(ggb, d["wgb"]) + d["bgb"]) * (1.0 / GLA_TAU)

    kv = nn(_rms(ckv, d["kvnw"]), d["w_ukv"])
    kr_r = rope(kr)
    q = nn(_rms(dq, d["qnw"]), d["w_uq"])
    qs, ks, vs = [], [], []
    for hh in range(HEADS):
        qs += [q[:, hh * 256:hh * 256 + 128], rope(q[:, hh * 256 + 128:(hh + 1) * 256])]
        ks += [kv[:, hh * 256:hh * 256 + 128], kr_r]
        vs += [kv[:, hh * 256 + 128:(hh + 1) * 256]]
    outs = dict(y_sgu=y_sgu, qg=qg, kg=gk, vg=gv, lgf=lgf, lgb=lgb, gr=gr,
                q_cat=jnp.concatenate(qs, axis=-1) * ATT_SCALE_LOG2, k_cat=jnp.concatenate(ks, axis=-1), v=jnp.concatenate(vs, axis=-1))
    return outs, dict(a_in=h)


def attn_out_tile(d, c, z):
    mod = d["mod"]
    o = d["o_f"] + d["o_b"]
    ms = jnp.zeros_like(o)
    for hh in range(HEADS):
        m_h = _lane_group_mask(256, 64, hh)
        ms = ms + m_h * (jnp.sum(o * o * m_h, axis=-1, keepdims=True) * (1.0 / 64))
    yg = o * lax.rsqrt(ms + EPS) * d["gnw"] * _silu(d["gr"])
    y = jnp.concatenate([d["y_sgu"], yg, d["y_mla"]], axis=-1)
    t = nn_const(c["w_out"], z is not None)(y)
    if z is not None:
        t = t + z["zt"]
    return dict(x1=d["x"] + mod[2:3] * t), dict(a_out=y)


def ffn_tile(d, c, z):
    mod = d["mod"]
    h2 = _rms(d["x1"], d["n2w"]) * (1.0 + mod[4:5]) + mod[3:4]
    f = None
    a2s = []
    for j in range(D_FF // FF_CH):
        pre = nn_const(c["w_ff1"][j], z is not None)(h2)
        if z is not None:
            pre = pre + z["zpre"][:, j * FF_CH:(j + 1) * FF_CH]
        a = jnp.maximum(pre, 0.0)
        a2 = a * a
        a2s.append(a2)
        fj = nn_const(c["w_ff2"][j], z is not None)(a2)
        f = fj if f is None else f + fj
    if z is not None:
        f = f + z["zf"]
    return dict(x2=d["x1"] + mod[5:6] * f, f=f), dict(a_ff1=h2, a_ff2=jnp.concatenate(a2s, axis=-1))


def _in_spec(kind, arr, tile):
    if kind == "tile":
        return pl.BlockSpec((tile, arr.shape[1]), lambda i: (i, 0))
    if kind == "kind":
        return pl.BlockSpec((1,) + arr.shape[1:], lambda i: (jnp.where(i < TILE // tile, 0, 1), 0, 0))
    nd = arr.ndim
    if kind in ("wfull", "wcols", "wrows"):
        return pl.BlockSpec(arr.shape, lambda i: (0,) * nd, pipeline_mode=pl.Buffered(1))
    return pl.BlockSpec(arr.shape, lambda i: (0,) * nd)


def _load(kind, ref):
    if kind == "kind":
        return ref[0]
    if kind == "wcols":
        return [ref[:, j * FF_CH:(j + 1) * FF_CH] for j in range(ref.shape[1] // FF_CH)]
    if kind == "wrows":
        return [ref[j * FF_CH:(j + 1) * FF_CH, :] for j in range(ref.shape[0] // FF_CH)]
    return ref[...]


def tile_forward(fn, name, t_all, ins, out_defs, tile=TILE):
    keys = [k for k, _, _, _ in ins]
    kinds = [kd for _, kd, _, _ in ins]
    diffs = [df for _, _, df, _ in ins]
    arrs = [a for _, _, _, a in ins]
    n_in = len(ins)

    def body(*refs):
        vals = [_load(kinds[j], refs[j]) for j in range(n_in)]
        d = {keys[j]: vals[j] for j in range(n_in) if diffs[j]}
        c = {keys[j]: vals[j] for j in range(n_in) if not diffs[j]}
        outs, _ = fn(d, c, None)
        for j, (k, _, dt) in enumerate(out_defs):
            refs[n_in + j][...] = outs[k].astype(dt)

    res = pl.pallas_call(
        body, name=name, grid=(t_all // tile,),
        in_specs=[_in_spec(kinds[j], arrs[j], tile) for j in range(n_in)],
        out_specs=[pl.BlockSpec((tile, w), lambda i: (i, 0)) for _, w, _ in out_defs],
        out_shape=[jax.ShapeDtypeStruct((t_all, w), dt) for _, w, dt in out_defs],
        compiler_params=_cparams(1),
    )(*arrs)
    return {k: r for (k, _, _), r in zip(out_defs, res)}


def tile_backward(fn, name, t_all, ins, cots, z_defs, aux_defs, tile=TILE, resid=None):
    keys = [k for k, _, _, _ in ins]
    kinds = [kd for _, kd, _, _ in ins]
    diffs = [df for _, _, df, _ in ins]
    arrs = [a for _, _, _, a in ins]
    cot_keys, cot_arrs = [], []
    for k, a in cots:
        for one in (a if isinstance(a, (list, tuple)) else [a]):
            cot_keys.append(k)
            cot_arrs.append(one)
    if resid is not None:
        cot_keys.append("resid:" + resid[0])
        cot_arrs.append(resid[1])
    n_in, n_cot = len(ins), len(cot_arrs)
    dkeys = [j for j in range(n_in) if diffs[j]]
    ctx_tiles = TILE // tile

    def body(*refs):
        i = pl.program_id(0)
        vals = [_load(kinds[j], refs[j]) for j in range(n_in)]
        d = {keys[j]: vals[j] for j in dkeys}
        c = {keys[j]: vals[j] for j in range(n_in) if not diffs[j]}
        zs = {k: jnp.zeros((tile, w), F32) for k, w in z_defs}
        outs, vjp_fn, aux = jax.vjp(lambda dd, zz: fn(dd, c, zz), d, zs, has_aux=True)
        ct = {}
        for j, k in enumerate(cot_keys):
            ct[k] = refs[n_in + j][...] + ct[k] if k in ct else refs[n_in + j][...]
        dd, dz = vjp_fn({k: ct[k].astype(outs[k].dtype) for k in outs})
        base = n_in + n_cot
        for n, j in enumerate(dkeys):
            ref, g = refs[base + n], dd[keys[j]]
            if kinds[j] == "tile":
                ref[...] = g + ct["resid:" + keys[j]] if "resid:" + keys[j] in ct else g
            else:
                first = ((i == 0) | (i == ctx_tiles)) if kinds[j] == "kind" else (i == 0)
                gv = g[None] if kinds[j] == "kind" else g

                @pl.when(first)
                def _(ref=ref, gv=gv):
                    ref[...] = gv

                @pl.when(jnp.logical_not(first))
                def _(ref=ref, gv=gv):
                    ref[...] += gv
        base += len(dkeys)
        for n, (k, _) in enumerate(z_defs):
            refs[base + n][...] = dz[k].astype(BF)
        base += len(z_defs)
        for n, (k, _) in enumerate(aux_defs):
            refs[base + n][...] = aux[k].T.astype(BF)

    out_specs, out_shape = [], []
    for j in dkeys:
        out_specs.append(_in_spec(kinds[j], arrs[j], tile))
        out_shape.append(jax.ShapeDtypeStruct(arrs[j].shape, F32))
    for _, w in z_defs:
        out_specs.append(pl.BlockSpec((tile, w), lambda i: (i, 0)))
        out_shape.append(jax.ShapeDtypeStruct((t_all, w), BF))
    for _, w in aux_defs:
        out_specs.append(pl.BlockSpec((w, tile), lambda i: (0, i)))
        out_shape.append(jax.ShapeDtypeStruct((w, t_all), BF))
    res = pl.pallas_call(
        body, name=name, grid=(t_all // tile,),
        in_specs=[_in_spec(kinds[j], arrs[j], tile) for j in range(n_in)]
        + [pl.BlockSpec((tile, a.shape[1]), lambda i: (i, 0)) for a in cot_arrs],
        out_specs=out_specs, out_shape=out_shape, compiler_params=_cparams(1),
    )(*arrs, *cot_arrs)
    grads = {keys[j]: res[n] for n, j in enumerate(dkeys)}
    extra = {k: res[len(dkeys) + n] for n, (k, _) in enumerate(list(z_defs) + list(aux_defs))}
    return grads, extra


def ffn_backward(x1, modl, n2w, w1, w2, dx2, f, name):
    t_all = x1.shape[0]
    n_ch = D_FF // FF_CH

    def head(x, mod, nw):
        return _rms(x, nw) * (1.0 + mod[4:5]) + mod[3:4]

    def body(x_ref, mod_ref, nw_ref, w1_ref, w2_ref, dx2_ref, f_ref, dx1_ref, dmod_ref, dnw_ref, zpre_ref, zf_ref, a1_ref,
             a2_ref):
        i = pl.program_id(0)
        mod = mod_ref[0]
        dx2 = dx2_ref[...]
        h2, vjp_head = jax.vjp(head, x_ref[...], mod, nw_ref[...])
        h2b = h2.astype(BF)
        dfb = (dx2 * mod[5:6]).astype(BF)
        f = f_ref[...]
        dh2 = jnp.zeros((TILE, D), F32)
        for j in range(n_ch):
            cs = slice(j * FF_CH, (j + 1) * FF_CH)
            a = jnp.maximum(_dot(h2b, w1_ref[:, cs]), 0.0)
            a2b = (a * a).astype(BF)
            dpre = (_dot_nt(dfb, w2_ref[cs, :]) * (2.0 * a)).astype(BF)
            dh2 = dh2 + _dot_nt(dpre, w1_ref[:, cs])
            zpre_ref[:, cs] = dpre
            a2_ref[:, cs] = a2b
        zf_ref[...] = (dx2 * mod[5:6]).T.astype(BF)
        a1_ref[...] = h2.T.astype(BF)
        dx1, dmod, dnw = vjp_head(dh2)
        dx1_ref[...] = dx2 + dx1
        row = lax.broadcasted_iota(jnp.int32, (8, D), 0)
        dmod = dmod + jnp.where(row == 5, jnp.sum(dx2 * f, axis=0, keepdims=True), 0.0)
        first_kind = (i == 0) | (i == 1)

        @pl.when(first_kind)
        def _():
            dmod_ref[0] = dmod

        @pl.when(jnp.logical_not(first_kind))
        def _():
            dmod_ref[0] += dmod

        @pl.when(i == 0)
        def _():
            dnw_ref[...] = dnw

        @pl.when(i > 0)
        def _():
            dnw_ref[...] += dnw

    tspec = lambda w: pl.BlockSpec((TILE, w), lambda i: (i, 0))
    once = lambda shp: pl.BlockSpec(shp, lambda i: (0, 0), pipeline_mode=pl.Buffered(1))
    kind = pl.BlockSpec((1, 8, D), lambda i: (jnp.minimum(i, 1), 0, 0))
    tr = pl.BlockSpec((D, TILE), lambda i: (0, i))
    return pl.pallas_call(
        body, name=name, grid=(t_all // TILE,),
        in_specs=[tspec(D), kind, pl.BlockSpec((1, D), lambda i: (0, 0)), once((D, D_FF)), once((D_FF, D)), tspec(D),
                  tspec(D)],
        out_specs=[tspec(D), kind, pl.BlockSpec((1, D), lambda i: (0, 0)), tspec(D_FF), tr, tr, tspec(D_FF)],
        out_shape=[jax.ShapeDtypeStruct((t_all, D), F32), jax.ShapeDtypeStruct((2, 8, D), F32),
                   jax.ShapeDtypeStruct((1, D), F32), jax.ShapeDtypeStruct((t_all, D_FF), BF),
                   jax.ShapeDtypeStruct((D, t_all), BF), jax.ShapeDtypeStruct((D, t_all), BF),
                   jax.ShapeDtypeStruct((t_all, D_FF), BF)],
        compiler_params=_cparams(1),
    )(x1, modl, n2w, w1, w2, dx2, f)


WG_TOK = 768


def wgrad(at, b, name, bk2=1024):
    k1, t = at.shape
    k2 = b.shape[1]
    bk2 = min(bk2, k2)
    tt = WG_TOK if t % WG_TOK == 0 else TILE
    nt_ = t // tt

    def body(a_ref, b_ref, o_ref, acc):
        s = pl.program_id(1)

        @pl.when(s == 0)
        def _():
            acc[...] = jnp.zeros_like(acc)

        acc[...] += _dot(a_ref[...], b_ref[...])

        @pl.when(s == nt_ - 1)
        def _():
            o_ref[...] = acc[...]

    return pl.pallas_call(
        body, name=name, grid=(k2 // bk2, nt_),
        in_specs=[pl.BlockSpec((k1, tt), lambda j, s: (0, s)), pl.BlockSpec((tt, bk2), lambda j, s: (s, j))],
        out_specs=pl.BlockSpec((k1, bk2), lambda j, s: (0, j)),
        out_shape=jax.ShapeDtypeStruct((k1, k2), F32),
        scratch_shapes=[pltpu.VMEM((k1, bk2), F32)],
        compiler_params=_cparams(2),
    )(at, b)


def _gla_consts(reverse, diff):
    r = lax.broadcasted_iota(jnp.int32, (GCH, GCH), 0)
    cc = lax.broadcasted_iota(jnp.int32, (GCH, GCH), 1)
    low = (r >= cc)
    tri = (jnp.logical_not(low) | (r == cc)) if reverse else low
    tri_f = tri.astype(F32)
    tri_t = (cc >= r) if not reverse else (cc <= r)
    hmk = [_lane_group_mask(128, 32, h) for h in range(HEADS)]
    hmv = [_lane_group_mask(256, 64, h) for h in range(HEADS)]
    e = lax.broadcasted_iota(jnp.int32, (256, 128), 0) // 64
    dk = lax.broadcasted_iota(jnp.int32, (256, 128), 1) // 32
    return dict(cum=make_cum(tri_f.astype(BF), tri_t.astype(F32).astype(BF), diff), ops=_ops(diff), reverse=reverse,
                tri4=jnp.concatenate([tri_f] * HEADS, axis=0), hmk=hmk, hmv=hmv, bd=(e == dk).astype(F32))


def gla_chunk(st, q, k, v, g, cs):
    nn, nt, tn = cs["ops"]
    b = cs["cum"](g)
    bl = jnp.sum(g, axis=0, keepdims=True)
    b_ref = jnp.sum(g[GCH // 2:] if cs["reverse"] else g[:GCH // 2], axis=0, keepdims=True)
    qe = q * jnp.exp(b)
    qs = q * jnp.exp(b - b_ref)
    ks = k * jnp.exp(b_ref - b)
    qstack = jnp.concatenate([qs * cs["hmk"][h] for h in range(HEADS)], axis=0)
    att = nt(qstack, ks) * cs["tri4"]
    ofull = nn(att, v)
    o = nt(qe, st)
    for h in range(HEADS):
        o = o + ofull[h * GCH:(h + 1) * GCH] * cs["hmv"][h]
    kd = k * jnp.exp(bl - b)
    st_new = st * jnp.exp(bl) + tn(v, kd) * cs["bd"]
    return st_new, o


def _gla_chunk_index(s, n_ch, reverse):
    ctx_ch = TILE // GCH
    if not reverse:
        return s
    return jnp.where(s < ctx_ch, ctx_ch - 1 - s, n_ch - 1 + ctx_ch - s)


def gla_forward(q, k, v, gf, gb, name):
    t = q.shape[0]
    n_ch = t // GCH

    def body(*refs):
        s = pl.program_id(0)
        for dr, reverse in enumerate((False, True)):
            q_ref, k_ref, v_ref, g_ref = refs[4 * dr:4 * dr + 4]
            o_ref, sst_ref = refs[8 + 2 * dr:8 + 2 * dr + 2]
            st = refs[12 + dr]

            @pl.when(s == 0)
            def _(st=st):
                st[...] = jnp.zeros_like(st)

            cur = st[...]
            sst_ref[0] = cur
            st_new, o = gla_chunk(cur, q_ref[...], k_ref[...], v_ref[...], g_ref[...], _gla_consts(reverse, False))
            o_ref[...] = o
            st[...] = st_new

    in_specs, out_specs, out_shape = [], [], []
    for reverse in (False, True):
        im = lambda s, reverse=reverse: (_gla_chunk_index(s, n_ch, reverse), 0)
        im3 = lambda s, reverse=reverse: (_gla_chunk_index(s, n_ch, reverse), 0, 0)
        in_specs += [pl.BlockSpec((GCH, 128), im), pl.BlockSpec((GCH, 128), im), pl.BlockSpec((GCH, 256), im),
                     pl.BlockSpec((GCH, 128), im)]
        out_specs += [pl.BlockSpec((GCH, 256), im), pl.BlockSpec((1, 256, 128), im3)]
        out_shape += [jax.ShapeDtypeStruct((t, 256), F32), jax.ShapeDtypeStruct((n_ch, 256, 128), F32)]
    return pl.pallas_call(
        body, name=name, grid=(n_ch,), in_specs=in_specs, out_specs=out_specs, out_shape=out_shape,
        scratch_shapes=[pltpu.VMEM((256, 128), F32), pltpu.VMEM((256, 128), F32)],
        compiler_params=_cparams(1),
    )(q, k, v, gf, q, k, v, gb)


def gla_backward(q, k, v, gf, gb, sst_f, sst_b, do, name):
    t = q.shape[0]
    n_ch = t // GCH

    def body(*refs):
        r = pl.program_id(0)
        for dr, reverse in enumerate((False, True)):
            q_ref, k_ref, v_ref, g_ref, sst_ref, do_ref = refs[6 * dr:6 * dr + 6]
            outs = refs[12 + 4 * dr:12 + 4 * dr + 4]
            dst = refs[20 + dr]

            @pl.when(r == 0)
            def _(dst=dst):
                dst[...] = jnp.zeros_like(dst)

            cs = _gla_consts(reverse, True)
            _, vjp_fn = jax.vjp(lambda a, b, c_, d_, e_, cs=cs: gla_chunk(a, b, c_, d_, e_, cs),
                                sst_ref[0], q_ref[...], k_ref[...], v_ref[...], g_ref[...])
            grads = vjp_fn((dst[...], do_ref[...]))
            for o_ref, gval in zip(outs, grads[1:]):
                o_ref[...] = gval
            dst[...] = grads[0]

    in_specs, out_specs, out_shape = [], [], []
    for reverse in (False, True):
        im = lambda r, reverse=reverse: (_gla_chunk_index(n_ch - 1 - r, n_ch, reverse), 0)
        im3 = lambda r, reverse=reverse: (_gla_chunk_index(n_ch - 1 - r, n_ch, reverse), 0, 0)
        in_specs += [pl.BlockSpec((GCH, 128), im), pl.BlockSpec((GCH, 128), im), pl.BlockSpec((GCH, 256), im),
                     pl.BlockSpec((GCH, 128), im), pl.BlockSpec((1, 256, 128), im3), pl.BlockSpec((GCH, 256), im)]
        out_specs += [pl.BlockSpec((GCH, 128), im), pl.BlockSpec((GCH, 128), im), pl.BlockSpec((GCH, 256), im),
                      pl.BlockSpec((GCH, 128), im)]
        out_shape += [jax.ShapeDtypeStruct((t, 128), F32), jax.ShapeDtypeStruct((t, 128), F32),
                      jax.ShapeDtypeStruct((t, 256), F32), jax.ShapeDtypeStruct((t, 128), F32)]
    return pl.pallas_call(
        body, name=name, grid=(n_ch,), in_specs=in_specs, out_specs=out_specs, out_shape=out_shape,
        scratch_shapes=[pltpu.VMEM((256, 128), F32), pltpu.VMEM((256, 128), F32)],
        compiler_params=_cparams(1),
    )(q, k, v, gf, sst_f, do, q, k, v, gb, sst_b, do)


def _resident(hbm_ref, vmem_ref, sem):
    cp = pltpu.make_async_copy(hbm_ref, vmem_ref, sem)
    cp.start()
    cp.wait()


def mla_forward(q_cat, k_cat, v, name):
    t = q_cat.shape[0]
    n_t = t // TILE

    ch = KV_CH_FWD if (t - TILE) % KV_CH_FWD == 0 else KV_CH
    n_main = (t - TILE) // ch

    def body(q_ref, k_hbm, v_hbm, o_ref, lse_ref, k_s, v_s, m_s, l_s, acc_s, sem):
        i = pl.program_id(0)

        @pl.when(i == 0)
        def _():
            _resident(k_hbm, k_s, sem.at[0])
            _resident(v_hbm, v_s, sem.at[1])

        m_s[...] = jnp.full(m_s.shape, -1e30, F32)
        l_s[...] = jnp.zeros_like(l_s)
        acc_s[...] = jnp.zeros_like(acc_s)

        def chunk(r0, size):
            for h in range(HEADS):
                kh = k_s[pl.ds(r0, size), h * 256:(h + 1) * 256]
                vh = v_s[pl.ds(r0, size), h * 128:(h + 1) * 128]
                s = _dot_nt(q_ref[:, h * 256:(h + 1) * 256], kh)
                m_prev = m_s[h]
                m_next = jnp.maximum(m_prev, jnp.max(s, axis=-1, keepdims=True))
                p = jnp.exp2(s - jnp.tile(m_next, (1, size // 128)))
                alpha = jnp.exp2(m_prev - m_next)
                l_s[h] = alpha * l_s[h] + jnp.sum(p, axis=-1, keepdims=True)
                acc_s[h] = alpha * acc_s[h] + _dot(p.astype(BF), vh)
                m_s[h] = m_next

        chunk(0, TILE)

        @pl.when(i >= 1)
        def _():
            def step(c, carry):
                chunk(pl.multiple_of(TILE + c * ch, TILE), ch)
                return carry

            lax.fori_loop(0, n_main, step, 0, unroll=MLA_UNROLL)

        lane = lax.broadcasted_iota(jnp.int32, (TILE, 128), 1)
        cols = jnp.zeros((TILE, 128), F32)
        for h in range(HEADS):
            o_ref[:, h * 128:(h + 1) * 128] = acc_s[h] / l_s[h]
            cols = jnp.where(lane == h, m_s[h] + jnp.log2(l_s[h]), cols)
        lse_ref[...] = cols.T[0:8, :]

    return pl.pallas_call(
        body, name=name, grid=(n_t,),
        in_specs=[pl.BlockSpec((TILE, 1024), lambda i: (i, 0)), pl.BlockSpec(memory_space=pl.ANY),
                  pl.BlockSpec(memory_space=pl.ANY)],
        out_specs=[pl.BlockSpec((TILE, 512), lambda i: (i, 0)), pl.BlockSpec((8, TILE), lambda i: (0, i))],
        out_shape=[jax.ShapeDtypeStruct((t, 512), F32), jax.ShapeDtypeStruct((8, t), F32)],
        scratch_shapes=[pltpu.VMEM((t, 1024), BF), pltpu.VMEM((t, 512), BF), pltpu.VMEM((HEADS, TILE, 128), F32),
                        pltpu.VMEM((HEADS, TILE, 128), F32), pltpu.VMEM((HEADS, TILE, 128), F32),
                        pltpu.SemaphoreType.DMA((2,))],
        compiler_params=_cparams(1),
    )(q_cat, k_cat, v)


def mla_delta(do, o, name):
    t = do.shape[0]

    def body(do_ref, o_ref, dl_ref, dob_ref):
        d = do_ref[...]
        prod = d * o_ref[...]
        rows = [jnp.sum(prod[:, h * 128:(h + 1) * 128], axis=-1, keepdims=True) for h in range(HEADS)]
        cols = jnp.concatenate(rows + [jnp.zeros((TILE, 128 - HEADS), F32)], axis=-1)
        dl_ref[...] = cols.T[0:8, :]
        dob_ref[...] = d.astype(BF)

    return pl.pallas_call(
        body, name=name, grid=(t // TILE,),
        in_specs=[pl.BlockSpec((TILE, 512), lambda i: (i, 0)), pl.BlockSpec((TILE, 512), lambda i: (i, 0))],
        out_specs=[pl.BlockSpec((8, TILE), lambda i: (0, i)), pl.BlockSpec((TILE, 512), lambda i: (i, 0))],
        out_shape=[jax.ShapeDtypeStruct((8, t), F32), jax.ShapeDtypeStruct((t, 512), BF)],
        compiler_params=_cparams(1),
    )(do, o)


def mla_backward(q_cat, k_cat, v, lse_rows, dl_rows, do_bf, name):
    t = q_cat.shape[0]
    n_t = t // TILE
    ch = Q_CH_BWD if (t - TILE) % Q_CH_BWD == 0 else KV_CH
    n_main = (t - TILE) // ch

    def body(q_hbm, do_hbm, k_ref, v_ref, lse_ref, dl_ref, dq_ref, dk_ref, dv_ref, q_s, do_s, dk_s, dv_s, sem):
        h, j = pl.program_id(0), pl.program_id(1)

        @pl.when(j == 0)
        def _():
            _resident(q_hbm.at[:, pl.ds(pl.multiple_of(h * 256, 256), 256)], q_s, sem.at[0])
            _resident(do_hbm.at[:, pl.ds(pl.multiple_of(h * 128, 128), 128)], do_s, sem.at[1])
            dq_ref[...] = jnp.zeros_like(dq_ref)

        dk_s[...] = jnp.zeros_like(dk_s)
        dv_s[...] = jnp.zeros_like(dv_s)
        kh = k_ref[...]
        vh = v_ref[...]

        def chunk(r0, size):
            qh = q_s[pl.ds(r0, size), :]
            doh = do_s[pl.ds(r0, size), :]
            pt = jnp.exp2(_dot_nt(kh, qh) - lse_ref[pl.ds(h, 1), pl.ds(r0, size)])
            dst = (pt * (_dot_nt(vh, doh) - dl_ref[pl.ds(h, 1), pl.ds(r0, size)])).astype(BF)
            dv_s[...] += _dot(pt.astype(BF), doh)
            dk_s[...] += _dot(dst, qh)
            dq_ref[pl.ds(r0, size), :] += _dot_tn(dst, kh)

        @pl.when(j == 0)
        def _():
            chunk(0, TILE)

        def step(c, carry):
            chunk(pl.multiple_of(TILE + c * ch, TILE), ch)
            return carry

        lax.fori_loop(0, n_main, step, 0, unroll=MLA_UNROLL)
        dk_ref[...] = dk_s[...] * LN2
        dv_ref[...] = dv_s[...]

        @pl.when(j == n_t - 1)
        def _():
            dq_ref[...] = dq_ref[...] * LN2

    rows = pl.BlockSpec((8, t), lambda h, j: (0, 0))
    hbm = pl.BlockSpec(memory_space=pl.ANY)
    return pl.pallas_call(
        body, name=name, grid=(HEADS, n_t),
        in_specs=[hbm, hbm, pl.BlockSpec((TILE, 256), lambda h, j: (j, h)), pl.BlockSpec((TILE, 128), lambda h, j: (j, h)),
                  rows, rows],
        out_specs=[pl.BlockSpec((t, 256), lambda h, j: (0, h)), pl.BlockSpec((TILE, 256), lambda h, j: (j, h)),
                   pl.BlockSpec((TILE, 128), lambda h, j: (j, h))],
        out_shape=[jax.ShapeDtypeStruct((t, 1024), F32), jax.ShapeDtypeStruct((t, 1024), F32),
                   jax.ShapeDtypeStruct((t, 512), F32)],
        scratch_shapes=[pltpu.VMEM((t, 256), BF), pltpu.VMEM((t, 128), BF), pltpu.VMEM((TILE, 256), F32),
                        pltpu.VMEM((TILE, 128), F32), pltpu.SemaphoreType.DMA((2,))],
        compiler_params=_cparams(2),
    )(q_cat, do_bf, k_cat, v, lse_rows, dl_rows)


def final_loss(xf, target, fnw, name):
    t = xf.shape[0]
    n_t = t // TILE

    def body(x_ref, t_ref, w_ref, loss_ref, dx_ref, dw_ref):
        i = pl.program_id(0)

        @pl.when(i == 0)
        def _():
            loss_ref[...] = jnp.zeros_like(loss_ref)
            dw_ref[...] = jnp.zeros_like(dw_ref)
            dx_ref[...] = jnp.zeros_like(dx_ref)

        @pl.when(i >= 1)
        def _():
            y, vjp_fn = jax.vjp(_rms, x_ref[...], w_ref[...])
            err = y - t_ref[...]
            loss_ref[...] += jnp.broadcast_to(0.5 * jnp.sum(jnp.mean(err * err, axis=-1, keepdims=True)), (8, 128))
            dx, dw = vjp_fn(err * (1.0 / D))
            dx_ref[...] = dx
            dw_ref[...] += dw

    return pl.pallas_call(
        body, name=name, grid=(n_t,),
        in_specs=[pl.BlockSpec((TILE, D), lambda i: (i, 0)), pl.BlockSpec((TILE, D), lambda i: (jnp.maximum(i - 1, 0), 0)),
                  pl.BlockSpec((1, D), lambda i: (0, 0))],
        out_specs=[pl.BlockSpec((8, 128), lambda i: (0, 0)), pl.BlockSpec((TILE, D), lambda i: (i, 0)),
                   pl.BlockSpec((1, D), lambda i: (0, 0))],
        out_shape=[jax.ShapeDtypeStruct((8, 128), F32), jax.ShapeDtypeStruct((t, D), F32),
                   jax.ShapeDtypeStruct((1, D), F32)],
        compiler_params=_cparams(1),
    )(xf, target, fnw)


def exchange(xs, name, scatter):
    n = len(xs)
    blks = [tuple(x.shape[1:] if scatter else x.shape) for x in xs]
    per = N_DEV - 1

    def body(*refs):
        x_refs, o_refs = refs[:n], refs[n:2 * n]
        ssem, rsem, lsem = refs[2 * n:]
        xi, yi, ci = lax.axis_index("x"), lax.axis_index("y"), lax.axis_index("c")
        me3 = (xi, yi, ci)
        me = 4 * xi + 2 * yi + ci
        flat = lambda d: 4 * d[0] + 2 * d[1] + d[2]

        if scatter:
            def peer(k):
                return ((1 - xi) if (k >> 2) & 1 else xi, (1 - yi) if (k >> 1) & 1 else yi, (1 - ci) if k & 1 else ci)

            def copy(a, k, lands):
                dev = peer(k)
                return pltpu.make_async_remote_copy(
                    src_ref=x_refs[a].at[flat(dev)], dst_ref=o_refs[a].at[flat(dev) if lands else me],
                    send_sem=ssem.at[a * per + k - 1], recv_sem=rsem.at[a * per + k - 1],
                    device_id=dev, device_id_type=pl.DeviceIdType.MESH)

            own = [pltpu.make_async_copy(x_refs[a].at[me], o_refs[a].at[me], lsem.at[a]) for a in range(n)]
            sends = [copy(a, k, False) for a in range(n) for k in range(1, N_DEV)]
            for cp in own + sends:
                cp.start()
            for a in range(n):
                for k in range(1, N_DEV):
                    copy(a, k, True).wait_recv()
            for cp in sends:
                cp.wait_send()
            for cp in own:
                cp.wait()
            return

        sibling = (xi, yi, 1 - ci)
        chips = [(1 - xi, yi), (xi, 1 - yi), (1 - xi, 1 - yi)]

        def copy(a, k, block, to, src=None):
            rows = o_refs[a].at[flat(block)]
            return pltpu.make_async_remote_copy(
                src_ref=rows if src is None else src, dst_ref=rows,
                send_sem=ssem.at[a * per + k], recv_sem=rsem.at[a * per + k],
                device_id=to, device_id_type=pl.DeviceIdType.MESH)

        own = [pltpu.make_async_copy(x_refs[a], o_refs[a].at[me], lsem.at[a]) for a in range(n)]
        first = []
        for a in range(n):
            first.append(copy(a, 0, me3, sibling, src=x_refs[a]))
            first += [copy(a, 1 + j, me3, (*chip, ci), src=x_refs[a]) for j, chip in enumerate(chips)]
        for cp in own + first:
            cp.start()
        passed = []
        for j, chip in enumerate(chips):
            for a in range(n):
                copy(a, 1 + j, (*chip, ci), me3).wait_recv()
                fw = copy(a, 4 + j, (*chip, ci), sibling)
                fw.start()
                passed.append(fw)
        for a in range(n):
            copy(a, 0, sibling, me3).wait_recv()
            for j, chip in enumerate(chips):
                copy(a, 4 + j, (*chip, 1 - ci), me3).wait_recv()
        for cp in first + passed:
            cp.wait_send()
        for cp in own:
            cp.wait()

    hbm = pl.BlockSpec(memory_space=pl.ANY)
    res = pl.pallas_call(
        body, name=name, in_specs=[hbm] * n, out_specs=[hbm] * n,
        out_shape=[jax.ShapeDtypeStruct((N_DEV,) + blks[a], xs[a].dtype) for a in range(n)],
        scratch_shapes=[pltpu.SemaphoreType.DMA((n * per,)), pltpu.SemaphoreType.DMA((n * per,)),
                        pltpu.SemaphoreType.DMA((n,))],
        compiler_params=pltpu.CompilerParams(has_side_effects=True),
    )(*xs)
    return list(res)


def mod_forward(crows, w_mod, b_shard, name):
    cols = w_mod.shape[2]

    def body(c_ref, w_ref, b_ref, o_ref):
        o_ref[0] = _dot(_silu(c_ref[...]).astype(BF), w_ref[0].astype(BF)) + b_ref[0]

    return pl.pallas_call(
        body, name=name, grid=(2,),
        in_specs=[pl.BlockSpec((16, D), lambda l: (0, 0)), pl.BlockSpec((1, D, cols), lambda l: (l, 0, 0)),
                  pl.BlockSpec((1, 1, cols), lambda l: (l, 0, 0))],
        out_specs=pl.BlockSpec((1, 16, cols), lambda l: (l, 0, 0)),
        out_shape=jax.ShapeDtypeStruct((2, 16, cols), F32), compiler_params=_cparams(1),
    )(crows, w_mod, b_shard)


def mod_backward(crows, w_mod, d_own, d_ctx, name):
    cols = w_mod.shape[2]

    def body(c_ref, w_ref, do_ref, dc_ref, gw_ref, gs_ref):
        dc = dc_ref[0]
        dsum = dc[0:1]
        for s in range(1, N_DEV):
            dsum = dsum + dc[s:s + 1]
        row = lax.broadcasted_iota(jnp.int32, (8, cols), 0)
        d16 = jnp.concatenate([do_ref[0], jnp.where(row == 0, jnp.broadcast_to(dsum, (8, cols)), 0.0)], axis=0)
        gw_ref[0] = _dot_tn(_silu(c_ref[...]).astype(BF), d16.astype(BF))
        gs_ref[0] = _dot_nt(jnp.broadcast_to(dsum, (8, cols)).astype(BF), w_ref[0].astype(BF))

    return pl.pallas_call(
        body, name=name, grid=(2,),
        in_specs=[pl.BlockSpec((16, D), lambda l: (0, 0)), pl.BlockSpec((1, D, cols), lambda l: (l, 0, 0)),
                  pl.BlockSpec((1, 8, cols), lambda l: (l, 0, 0)), pl.BlockSpec((1, 8, cols), lambda l: (l, 0, 0))],
        out_specs=[pl.BlockSpec((1, D, cols), lambda l: (l, 0, 0)), pl.BlockSpec((1, 8, D), lambda l: (l, 0, 0))],
        out_shape=[jax.ShapeDtypeStruct((2, D, cols), F32), jax.ShapeDtypeStruct((2, 8, D), F32)],
        compiler_params=_cparams(1),
    )(crows, w_mod, d_own, d_ctx)


def silu_grad_scale(c_ctx, ds, name):
    def body(c_ref, ds_ref, o_ref):
        cc = c_ref[...]
        sg = jax.nn.sigmoid(cc)
        o_ref[...] = (ds_ref[0][0:1] + ds_ref[1][0:1]) * (sg * (1.0 + cc * (1.0 - sg)))

    return pl.pallas_call(body, name=name, out_shape=jax.ShapeDtypeStruct((1, D), F32))(c_ctx, ds)


def _adamw_math(p_ref, w_ref, m_ref, v_ref, g_ref, d_ref, nm_ref, nv_ref):
    g = p_ref[0].astype(F32)
    for s in range(1, p_ref.shape[0]):
        g = g + p_ref[s].astype(F32)
    mm = ADAM_B1 * m_ref[...] + (1.0 - ADAM_B1) * g
    vv = ADAM_B2 * v_ref[...] + (1.0 - ADAM_B2) * (g * g)
    m_hat = mm / (1.0 - ADAM_B1 ** ADAM_STEP)
    v_hat = vv / (1.0 - ADAM_B2 ** ADAM_STEP)
    g_ref[...] = g
    d_ref[...] = -ADAM_LR * (m_hat / (jnp.sqrt(v_hat) + ADAM_EPS) + ADAM_WD * w_ref[...])
    nm_ref[...] = mm
    nv_ref[...] = vv


def adamw(parts, w, m, v, name):
    n_parts, rows, cols = parts.shape
    lanes = -(-cols // 128) * 128
    block_rows = min(rows, 1 << ((ADAMW_BLOCK_ELEMS // lanes).bit_length() - 1))
    assert rows % block_rows == 0

    def body(*refs):
        _adamw_math(*refs)

    spec = pl.BlockSpec((block_rows, cols), lambda i: (i, 0))
    return pl.pallas_call(
        body, name=name, grid=(rows // block_rows,),
        in_specs=[pl.BlockSpec((n_parts, block_rows, cols), lambda i: (0, i, 0)), spec, spec, spec],
        out_specs=[spec] * 4, out_shape=[jax.ShapeDtypeStruct((rows, cols), F32)] * 4,
        compiler_params=_cparams(1),
    )(parts, w, m, v)


def adamw_group(items, name):
    n = len(items)

    def body(*refs):
        for a in range(n):
            _adamw_math(*refs[4 * a:4 * a + 4], *refs[4 * n + 4 * a:4 * n + 4 * a + 4])

    flat_in = [x for it in items for x in it]
    out_shape = [jax.ShapeDtypeStruct(it[1].shape, F32) for it in items for _ in range(4)]
    res = pl.pallas_call(body, name=name, out_shape=out_shape,
                         compiler_params=pltpu.CompilerParams(vmem_limit_bytes=VMEM_LIMIT_MB * 1024 * 1024))(*flat_in)
    return [tuple(res[4 * a:4 * a + 4]) for a in range(n)]


def _pad_cols(w, segs, total):
    parts, pos = [], 0
    for dst, src, wd in segs:
        if dst > pos:
            parts.append(jnp.zeros(w.shape[:-1] + (dst - pos,), w.dtype))
        parts.append(w[..., src:src + wd])
        pos = dst + wd
    if pos < total:
        parts.append(jnp.zeros(w.shape[:-1] + (total - pos,), w.dtype))
    return jnp.concatenate(parts, axis=-1)


def _unpad_cols(g, segs):
    return jnp.concatenate([g[..., dst:dst + wd] for dst, _, wd in segs], axis=-1)


def _rope_tables(n_lat):
    rows = n_lat // GRID_W
    row = jnp.repeat(jnp.arange(rows), GRID_W).astype(F32)
    col = jnp.tile(jnp.arange(GRID_W), rows).astype(F32)
    freq = ROPE_BASE ** (-jnp.arange(16, dtype=F32) * 2.0 / 32)
    ar, ac = row[:, None] * freq[None, :], col[:, None] * freq[None, :]
    z = jnp.zeros((n_lat, 16), F32)
    cos = jnp.concatenate([jnp.cos(ar), jnp.cos(ar), jnp.cos(ac), jnp.cos(ac), jnp.ones((n_lat, 64), F32)], axis=1)
    sa = jnp.concatenate([-jnp.sin(ar), z, -jnp.sin(ac), z, jnp.zeros((n_lat, 64), F32)], axis=1)
    sb = jnp.concatenate([z, jnp.sin(ar), z, jnp.sin(ac), jnp.zeros((n_lat, 64), F32)], axis=1)
    ident = lambda fill: jnp.full((TILE, 128), fill, F32)
    return (jnp.concatenate([ident(1.0), cos]), jnp.concatenate([ident(0.0), sa]), jnp.concatenate([ident(0.0), sb]))


def _gathered_to_full(g, name):
    if name in ("w_out", "w_ff2"):
        return jnp.transpose(g, (1, 0, 2, 3)).reshape(2, -1, g.shape[-1])
    return jnp.transpose(g, (1, 2, 0, 3)).reshape(2, g.shape[2], -1)


def _full_to_chunks(gw, name):
    if name in ("w_out", "w_ff2"):
        return jnp.transpose(gw.reshape(2, N_DEV, -1, gw.shape[-1]), (1, 0, 2, 3))
    return jnp.transpose(gw.reshape(2, gw.shape[1], N_DEV, -1), (2, 0, 1, 3))


def kernel(x, c, ctx, c_ctx, w_mod, b_mod, norm1_w, w_in, w_out, sgu_norm_w, sgu_norm_b, sgu_w, sgu_b, gla_wg_fwd, gla_bg_fwd, gla_wg_bwd, gla_bg_bwd, gla_norm_w, mla_q_norm_w, mla_w_uq, mla_kv_norm_w, mla_w_ukv, norm2_w, w_ff1, w_ff2, final_norm_w, loss_target, m_c_ctx, m_w_mod, m_b_mod, m_norm1_w, m_w_in, m_w_out, m_sgu_norm_w, m_sgu_norm_b, m_sgu_w, m_sgu_b, m_gla_wg_fwd, m_gla_bg_fwd, m_gla_wg_bwd, m_gla_bg_bwd, m_gla_norm_w, m_mla_q_norm_w, m_mla_w_uq, m_mla_kv_norm_w, m_mla_w_ukv, m_norm2_w, m_w_ff1, m_w_ff2, m_final_norm_w, v_c_ctx, v_w_mod, v_b_mod, v_norm1_w, v_w_in, v_w_out, v_sgu_norm_w, v_sgu_norm_b, v_sgu_w, v_sgu_b, v_gla_wg_fwd, v_gla_bg_fwd, v_gla_wg_bwd, v_gla_bg_bwd, v_gla_norm_w, v_mla_q_norm_w, v_mla_w_uq, v_mla_kv_norm_w, v_mla_w_ukv, v_norm2_w, v_w_ff1, v_w_ff2, v_final_norm_w):
    W = dict(c_ctx=c_ctx, w_mod=w_mod, b_mod=b_mod, norm1_w=norm1_w, w_in=w_in, w_out=w_out, sgu_norm_w=sgu_norm_w,
             sgu_norm_b=sgu_norm_b, sgu_w=sgu_w, sgu_b=sgu_b, gla_wg_fwd=gla_wg_fwd, gla_bg_fwd=gla_bg_fwd,
             gla_wg_bwd=gla_wg_bwd, gla_bg_bwd=gla_bg_bwd, gla_norm_w=gla_norm_w, mla_q_norm_w=mla_q_norm_w,
             mla_w_uq=mla_w_uq, mla_kv_norm_w=mla_kv_norm_w, mla_w_ukv=mla_w_ukv, norm2_w=norm2_w, w_ff1=w_ff1,
             w_ff2=w_ff2, final_norm_w=final_norm_w)
    M = dict(c_ctx=m_c_ctx, w_mod=m_w_mod, b_mod=m_b_mod, norm1_w=m_norm1_w, w_in=m_w_in, w_out=m_w_out,
             sgu_norm_w=m_sgu_norm_w, sgu_norm_b=m_sgu_norm_b, sgu_w=m_sgu_w, sgu_b=m_sgu_b, gla_wg_fwd=m_gla_wg_fwd,
             gla_bg_fwd=m_gla_bg_fwd, gla_wg_bwd=m_gla_wg_bwd, gla_bg_bwd=m_gla_bg_bwd, gla_norm_w=m_gla_norm_w,
             mla_q_norm_w=m_mla_q_norm_w, mla_w_uq=m_mla_w_uq, mla_kv_norm_w=m_mla_kv_norm_w, mla_w_ukv=m_mla_w_ukv,
             norm2_w=m_norm2_w, w_ff1=m_w_ff1, w_ff2=m_w_ff2, final_norm_w=m_final_norm_w)
    V = dict(c_ctx=v_c_ctx, w_mod=v_w_mod, b_mod=v_b_mod, norm1_w=v_norm1_w, w_in=v_w_in, w_out=v_w_out,
             sgu_norm_w=v_sgu_norm_w, sgu_norm_b=v_sgu_norm_b, sgu_w=v_sgu_w, sgu_b=v_sgu_b, gla_wg_fwd=v_gla_wg_fwd,
             gla_bg_fwd=v_gla_bg_fwd, gla_wg_bwd=v_gla_wg_bwd, gla_bg_bwd=v_gla_bg_bwd, gla_norm_w=v_gla_norm_w,
             mla_q_norm_w=v_mla_q_norm_w, mla_w_uq=v_mla_w_uq, mla_kv_norm_w=v_mla_kv_norm_w, mla_w_ukv=v_mla_w_ukv,
             norm2_w=v_norm2_w, w_ff1=v_w_ff1, w_ff2=v_w_ff2, final_norm_w=v_final_norm_w)

    n_lat = x.shape[1]
    assert ctx.shape[1] == TILE and n_lat % TILE == 0 and x.shape[2] == D
    t_all = TILE + n_lat
    n_t = t_all // TILE
    me = 4 * lax.axis_index("x") + 2 * lax.axis_index("y") + lax.axis_index("c")
    mod_cols = w_mod.shape[2]

    c_all = exchange([c], "ag_c", scatter=False)[0].reshape(N_DEV, D)
    crows = jnp.concatenate([c_all, c_ctx[None, :], jnp.zeros((7, D), F32)], axis=0)
    b_shard = lax.dynamic_slice_in_dim(b_mod, me * mod_cols, mod_cols, axis=1)[:, None, :]
    mod_sh = mod_forward(crows, w_mod, b_shard, "mod_fwd")
    mod_g = exchange([mod_sh.reshape(32, mod_cols)], "ag_mod", scatter=False)[0]
    mod_full = jnp.transpose(mod_g.reshape(N_DEV, 2, 16, mod_cols), (1, 2, 0, 3)).reshape(2, 16, 6 * D)
    mod_own = lax.dynamic_index_in_dim(mod_full, me, axis=1, keepdims=False)
    mod_ctx = mod_full[:, 8, :]
    pad2 = jnp.zeros((2, D), F32)
    modl = [jnp.stack([jnp.concatenate([mod_ctx[l].reshape(6, D), pad2]),
                       jnp.concatenate([mod_own[l].reshape(6, D), pad2])]) for l in range(2)]

    v2 = lambda a: a[None, :] if a.ndim == 1 else a.reshape(-1, a.shape[-1])
    gathered = exchange([v2(W[k].astype(BF)) for k in BIG_NAMES], "ag_weights", scatter=False)
    full = {k: _gathered_to_full(g.reshape((N_DEV,) + W[k].shape), k) for k, g in zip(BIG_NAMES, gathered)}
    w_in_p = _pad_cols(full["w_in"], W_IN_SEGS, P_COLS)
    w_uq_p = _pad_cols(full["mla_w_uq"], W_UQ_SEGS, 1024).astype(F32)
    w_ukv_f = full["mla_w_ukv"].astype(F32)
    wgf_p = jnp.pad(gla_wg_fwd, ((0, 0), (0, 112), (0, 0)))
    wgb_p = jnp.pad(gla_wg_bwd, ((0, 0), (0, 112), (0, 0)))
    sgu_bx = jnp.repeat(jnp.transpose(sgu_b, (0, 2, 1)), 64, axis=2)
    gnw_t = jnp.tile(gla_norm_w, (1, HEADS))
    rc, rsa, rsb = _rope_tables(n_lat)

    xin = jnp.concatenate([ctx[0], x[0]], axis=0)
    row = lambda a: a[None, :]

    def pre_ins(l, xl):
        return [("x", "tile", True, xl), ("mod", "kind", True, modl[l]), ("n1w", "full", True, row(norm1_w[l])),
                ("w_in", "wfull", False, w_in_p[l]), ("sgu_nw", "full", True, row(sgu_norm_w[l])),
                ("sgu_nb", "full", True, row(sgu_norm_b[l])), ("sgu_w", "full", True, sgu_w[l]),
                ("sgu_bx", "full", True, sgu_bx[l]), ("wgf", "full", True, wgf_p[l]), ("bgf", "full", True, row(gla_bg_fwd[l])),
                ("wgb", "full", True, wgb_p[l]), ("bgb", "full", True, row(gla_bg_bwd[l])),
                ("qnw", "full", True, row(mla_q_norm_w[l])), ("w_uq", "full", True, w_uq_p[l]),
                ("kvnw", "full", True, row(mla_kv_norm_w[l])), ("w_ukv", "full", True, w_ukv_f[l]),
                ("rc", "tile", False, rc), ("rsa", "tile", False, rsa), ("rsb", "tile", False, rsb)]

    pre_outs = [("y_sgu", 256, F32), ("qg", 128, F32), ("kg", 128, F32), ("vg", 256, F32), ("lgf", 128, F32),
                ("lgb", 128, F32), ("gr", 256, F32), ("q_cat", 1024, BF), ("k_cat", 1024, BF), ("v", 512, BF)]

    def out_ins(l, xl, a):
        return [("x", "tile", True, xl), ("mod", "kind", True, modl[l]), ("y_sgu", "tile", True, a["y_sgu"]),
                ("o_f", "tile", True, a["o_f"]), ("o_b", "tile", True, a["o_b"]), ("gr", "tile", True, a["gr"]),
                ("y_mla", "tile", True, a["y_mla"]), ("gnw", "full", True, row(gnw_t[l])),
                ("w_out", "wfull", False, full["w_out"][l])]

    def ffn_ins(l, x1):
        return [("x1", "tile", True, x1), ("mod", "kind", True, modl[l]), ("n2w", "full", True, row(norm2_w[l])),
                ("w_ff1", "wcols", False, full["w_ff1"][l]), ("w_ff2", "wrows", False, full["w_ff2"][l])]

    saved, xl = [], xin
    for l in range(2):
        a = tile_forward(pre_tile, f"pre_fwd{l}", t_all, pre_ins(l, xl), pre_outs)
        a["o_f"], a["sf"], a["o_b"], a["sb"] = gla_forward(a["qg"], a["kg"], a["vg"], a["lgf"], a["lgb"], f"gla_fwd{l}")
        a["y_mla"], a["lse"] = mla_forward(a["q_cat"], a["k_cat"], a["v"], f"mla_fwd{l}")
        a["x"] = xl
        a["x1"] = tile_forward(attn_out_tile, f"out_fwd{l}", t_all, out_ins(l, xl, a), [("x1", D, F32)])["x1"]
        ff = tile_forward(ffn_tile, f"ffn_fwd{l}", t_all, ffn_ins(l, a["x1"]), [("x2", D, F32), ("f", D, F32)])
        xl, a["f"] = ff["x2"], ff["f"]
        saved.append(a)

    loss_blk, dx, d_fnw = final_loss(xl, loss_target[0], row(final_norm_w), "final_loss")
    loss = lax.psum(loss_blk[0, 0], AXES)

    G = {}
    dmods = []
    for l in (1, 0):
        a = saved[l]
        dx1, dmod3, dn2w, zpre, zf_t, h2_t, a_ff2 = ffn_backward(
            a["x1"], modl[l], row(norm2_w[l]), full["w_ff1"][l], full["w_ff2"][l], dx, a["f"], f"ffn_bwd{l}")
        gw_ff1 = wgrad(h2_t, zpre, f"wg_ff1_{l}")
        gw_ff2 = jnp.transpose(wgrad(zf_t, a_ff2, f"wg_ff2_{l}"))
        g2, e2 = tile_backward(attn_out_tile, f"out_bwd{l}", t_all, out_ins(l, a["x"], a), [("x1", dx1)],
                               [("zt", D)], [("a_out", D)])
        gw_out = wgrad(e2["a_out"], e2["zt"], f"wg_out_{l}")
        dl_rows, do_bf = mla_delta(g2["y_mla"], a["y_mla"], f"mla_delta{l}")
        dq_cat, dk_cat, dv = mla_backward(a["q_cat"], a["k_cat"], a["v"], a["lse"], dl_rows, do_bf, f"mla_bwd{l}")
        dqf, dkf, dvf, dgf, dqb, dkb, dvb, dgb = gla_backward(
            a["qg"], a["kg"], a["vg"], a["lgf"], a["lgb"], a["sf"], a["sb"], g2["o_f"], f"gla_bwd{l}")
        cots = [("y_sgu", g2["y_sgu"]), ("qg", [dqf, dqb]), ("kg", [dkf, dkb]), ("vg", [dvf, dvb]), ("lgf", dgf),
                ("lgb", dgb), ("gr", g2["gr"]), ("q_cat", dq_cat), ("k_cat", dk_cat), ("v", dv)]
        g1, e1 = tile_backward(pre_tile, f"pre_bwd{l}", t_all, pre_ins(l, a["x"]), cots, [("zp", P_COLS)], [("a_in", D)],
                               resid=("x", g2["x"]))
        gw_in = _unpad_cols(wgrad(e1["a_in"], e1["zp"], f"wg_in_{l}", bk2=P_COLS), W_IN_SEGS)
        dx = g1["x"]
        dmods.append(g1["mod"] + g2["mod"] + dmod3)
        G[l] = dict(w_in=gw_in, w_out=gw_out, w_ff1=gw_ff1, w_ff2=gw_ff2,
                    mla_w_uq=_unpad_cols(g1["w_uq"], W_UQ_SEGS), mla_w_ukv=g1["w_ukv"],
                    norm1_w=g1["n1w"][0], norm2_w=dn2w[0], sgu_norm_w=g1["sgu_nw"][0], sgu_norm_b=g1["sgu_nb"][0],
                    sgu_w=g1["sgu_w"], sgu_b=jnp.transpose(g1["sgu_bx"].reshape(128, HEADS, 64).sum(-1)),
                    gla_wg_fwd=g1["wgf"][:16], gla_bg_fwd=g1["bgf"][0], gla_wg_bwd=g1["wgb"][:16], gla_bg_bwd=g1["bgb"][0],
                    gla_norm_w=g2["gnw"][0].reshape(HEADS, 64).sum(0), mla_q_norm_w=g1["qnw"][0], mla_kv_norm_w=g1["kvnw"][0])
    dmods = dmods[::-1]
    grad_x = dx[TILE:][None]

    dmod_pack = jnp.stack([jnp.stack([dmods[l][1, :6].reshape(-1), dmods[l][0, :6].reshape(-1)]) for l in range(2)])
    dmod_all = exchange([dmod_pack.reshape(4, 6 * D)], "ag_dmod", scatter=False)[0].reshape(N_DEV, 2, 2, 6 * D)
    dsl = lax.dynamic_slice_in_dim(dmod_all, me * mod_cols, mod_cols, axis=3)
    d_own = jnp.transpose(dsl[:, :, 0, :], (1, 0, 2))
    d_ctx = jnp.transpose(dsl[:, :, 1, :], (1, 0, 2))
    g_w_mod, ds_cc = mod_backward(crows, w_mod, d_own, d_ctx, "mod_bwd")
    g_c_ctx_part = silu_grad_scale(c_ctx[None, :], ds_cc, "silu_bwd")[0]
    g_b_mod_part = jnp.stack([dmods[l][1, :6].reshape(-1) + dmods[l][0, :6].reshape(-1) for l in range(2)])

    small_g = dict(c_ctx=g_c_ctx_part, b_mod=g_b_mod_part, final_norm_w=d_fnw[0])
    for k in SMALL_NAMES:
        if k not in small_g:
            small_g[k] = jnp.stack([G[0][k], G[1][k]])
    res = {}
    sparts = exchange([v2(small_g[k]) for k in SMALL_NAMES], "ag_small", scatter=False)
    s_out = adamw_group([(sparts[j], v2(W[k]), v2(M[k]), v2(V[k])) for j, k in enumerate(SMALL_NAMES)], "adamw_small")
    for j, k in enumerate(SMALL_NAMES):
        res[k] = [o.reshape(W[k].shape) for o in s_out[j]]

    chunks = []
    for k in BIG_NAMES:
        ch = _full_to_chunks(jnp.stack([G[0][k], G[1][k]]), k).astype(BF)
        chunks.append(ch.reshape(N_DEV, -1, ch.shape[-1]))
    bparts = exchange(chunks, "a2a_grads", scatter=True)
    for j, k in enumerate(BIG_NAMES):
        res[k] = [o.reshape(W[k].shape) for o in adamw(bparts[j], v2(W[k]), v2(M[k]), v2(V[k]), f"adamw_{k}")]
    res["w_mod"] = [o.reshape(w_mod.shape)
                    for o in adamw(v2(g_w_mod)[None], v2(w_mod), v2(m_w_mod), v2(v_w_mod), "adamw_w_mod")]
    outs = [loss, grad_x]
    for j in range(4):
        outs += [res[k][j] for k in WEIGHT_ORDER]
    return tuple(outs)
```

```python
import jax
import jax.numpy as jnp
from jax import lax
from jax.experimental import pallas as pl
from jax.experimental.pallas import tpu as pltpu

F32 = jnp.float32
BF = jnp.bfloat16

N_DEV = 8
AXES = ("x", "y", "c")
EPS = 1e-6
D = 1024
TILE = 256
GCH = 128
SGU_CHUNK = 128
HEADS = 4
ROPE_BASE = 10000.0
GRID_W = 64
GLA_TAU = 16.0
ATT_SCALE = (128 + 64) ** -0.5
ATT_SCALE_LOG2 = ATT_SCALE * 1.4426950408889634
LN2 = 0.6931471805599453
KV_CH = 512
KV_CH_FWD = 2048
Q_CH_BWD = 1024
MLA_UNROLL = 2
D_FF = 4096
FF_CH = 1024

ADAM_LR = 0.001
ADAM_B1 = 0.9
ADAM_B2 = 0.999
ADAM_EPS = 1e-08
ADAM_WD = 0.01
ADAM_STEP = 10

VMEM_LIMIT_MB = 56
ADAMW_BLOCK_ELEMS = 256 * 1024

W_IN_SEGS = ((0, 0, 128), (128, 128, 256), (384, 384, 16), (512, 400, 16), (640, 416, 256), (896, 672, 64),
             (1024, 736, 256), (1280, 992, 256), (1536, 1248, 128), (1664, 1376, 256), (1920, 1632, 256))
P_COLS = 2176
O_GK, O_GV, O_GGF, O_GGB, O_CKV, O_KR, O_SU, O_SV, O_GQ, O_GR, O_DQ = (s[0] for s in W_IN_SEGS)
W_UQ_SEGS = tuple((h * 256, h * 192, 192) for h in range(HEADS))

SMALL_NAMES = ("c_ctx", "b_mod", "norm1_w", "sgu_norm_w", "sgu_norm_b", "sgu_w", "sgu_b", "gla_wg_fwd", "gla_bg_fwd",
               "gla_wg_bwd", "gla_bg_bwd", "gla_norm_w", "mla_q_norm_w", "mla_kv_norm_w", "norm2_w", "final_norm_w")
BIG_NAMES = ("w_in", "w_out", "mla_w_uq", "mla_w_ukv", "w_ff1", "w_ff2")
WEIGHT_ORDER = ("c_ctx", "w_mod", "b_mod", "norm1_w", "w_in", "w_out", "sgu_norm_w", "sgu_norm_b", "sgu_w", "sgu_b",
                "gla_wg_fwd", "gla_bg_fwd", "gla_wg_bwd", "gla_bg_bwd", "gla_norm_w", "mla_q_norm_w", "mla_w_uq",
                "mla_kv_norm_w", "mla_w_ukv", "norm2_w", "w_ff1", "w_ff2", "final_norm_w")


def _cparams(n_axes):
    return pltpu.CompilerParams(dimension_semantics=("arbitrary",) * n_axes,
                                vmem_limit_bytes=VMEM_LIMIT_MB * 1024 * 1024)


def _dot(a, b):
    return jnp.dot(a, b, preferred_element_type=F32)


def _dot_nt(a, b):
    return lax.dot_general(a, b, (((1,), (1,)), ((), ())), preferred_element_type=F32)


def _dot_tn(a, b):
    return lax.dot_general(a, b, (((0,), (0,)), ((), ())), preferred_element_type=F32)


def _nn(a, b):
    return _dot(a.astype(BF), b.astype(BF))


def _nt(a, b):
    return _dot_nt(a.astype(BF), b.astype(BF))


def _tn(a, b):
    return _dot_tn(a.astype(BF), b.astype(BF))


nn_d = jax.custom_vjp(_nn)
nt_d = jax.custom_vjp(_nt)
tn_d = jax.custom_vjp(_tn)
nn_d.defvjp(lambda a, b: (_nn(a, b), (a, b)), lambda r, dy: (_nt(dy, r[1]), _tn(r[0], dy)))
nt_d.defvjp(lambda a, b: (_nt(a, b), (a, b)), lambda r, dy: (_nn(dy, r[1]), _tn(dy, r[0])))
tn_d.defvjp(lambda a, b: (_tn(a, b), (a, b)), lambda r, dy: (_nt(r[1], dy), _nn(r[0], dy)))


def nn_const(w_bf, diff):
    def raw(a):
        return _dot(a.astype(BF), w_bf)

    if not diff:
        return raw
    f = jax.custom_vjp(raw)
    f.defvjp(lambda a: (raw(a), None), lambda _, dy: (_dot_nt(dy.astype(BF), w_bf),))
    return f


def _split3(g):
    hi = g.astype(BF)
    r = g - hi.astype(F32)
    mid = r.astype(BF)
    lo = (r - mid.astype(F32)).astype(BF)
    return hi, mid, lo


def make_cum(tri_bf, tri_t_bf, diff):
    def raw(g, t):
        hi, mid, lo = _split3(g)
        return _dot(t, hi) + _dot(t, mid) + _dot(t, lo)

    def fwd(g):
        return raw(g, tri_bf)

    if not diff:
        return fwd
    cum = jax.custom_vjp(fwd)
    cum.defvjp(lambda g: (fwd(g), None), lambda _, db: (raw(db, tri_t_bf),))
    return cum


def _roll_lanes(x, shift):
    return pltpu.roll(x, shift, 1)


def make_rope(c, sa, sb, diff):
    def raw(x):
        return x * c + _roll_lanes(x, 112) * sa + _roll_lanes(x, 16) * sb

    if not diff:
        return raw
    f = jax.custom_vjp(raw)
    f.defvjp(lambda x: (raw(x), None),
             lambda _, dy: (dy * c + _roll_lanes(dy * sa, 16) + _roll_lanes(dy * sb, 112),))
    return f


def _ops(diff):
    return (nn_d, nt_d, tn_d) if diff else (_nn, _nt, _tn)


def _rms(x, w):
    return x * lax.rsqrt(jnp.mean(x * x, axis=-1, keepdims=True) + EPS) * w


def _gelu(x):
    return 0.5 * x * (1.0 + jnp.tanh(0.7978845608028654 * (x + 0.044715 * (x * x * x))))


def _silu(x):
    return x * jax.nn.sigmoid(x)


def _log_sigmoid(z):
    return jnp.minimum(z, 0.0) - jnp.log(1.0 + jnp.exp(-jnp.abs(z)))


def _lane_group_mask(width, group, h):
    lane = lax.broadcasted_iota(jnp.int32, (1, width), 1)
    return ((lane >= h * group) & (lane < (h + 1) * group)).astype(F32)


def pre_tile(d, c, z):
    nn, _, _ = _ops(z is not None)
    rope = make_rope(c["rc"], c["rsa"], c["rsb"], z is not None)
    mod = d["mod"]
    h = _rms(d["x"], d["n1w"]) * (1.0 + mod[1:2]) + mod[0:1]
    p = nn_const(c["w_in"], z is not None)(h)
    if z is not None:
        p = p + z["zp"]
    gk, gv = p[:, O_GK:O_GK + 128], p[:, O_GV:O_GV + 256]
    ggf, ggb = p[:, O_GGF:O_GGF + 128], p[:, O_GGB:O_GGB + 128]
    ckv, kr = p[:, O_CKV:O_CKV + 256], p[:, O_KR:O_KR + 128]
    su, sv = p[:, O_SU:O_SU + 256], p[:, O_SV:O_SV + 256]
    gq, gr, dq = p[:, O_GQ:O_GQ + 128], p[:, O_GR:O_GR + 256], p[:, O_DQ:O_DQ + 256]

    u = _gelu(su)
    gv_ = _gelu(sv)
    mu = jnp.mean(gv_, axis=-1, keepdims=True)
    cen = gv_ - mu
    vn = cen * lax.rsqrt(jnp.mean(cen * cen, axis=-1, keepdims=True) + EPS) * d["sgu_nw"] + d["sgu_nb"]
    hm = [_lane_group_mask(256, 64, hh) for hh in range(HEADS)]
    rows = []
    for ci in range(vn.shape[0] // SGU_CHUNK):
        vc = vn[ci * SGU_CHUNK:(ci + 1) * SGU_CHUNK]
        s = d["sgu_bx"]
        for hh in range(HEADS):
            s = s + hm[hh] * nn(d["sgu_w"][hh], vc)
        rows.append(s)
    y_sgu = u * jnp.concatenate(rows, axis=0)

    qg = gq * (32 ** -0.5)
    lgf = _log_sigmoid(nn(ggf, d["wgf"]) + d["bgf"]) * (1.0 / GLA_TAU)
    lgb = _log_sigmoid(nn(ggb, d["wgb"]) + d["bgb"]) * (1.0 / GLA_TAU)

    kv = nn(_rms(ckv, d["kvnw"]), d["w_ukv"])
    kr_r = rope(kr)
    q = nn(_rms(dq, d["qnw"]), d["w_uq"])
    qs, ks, vs = [], [], []
    for hh in range(HEADS):
        qs += [q[:, hh * 256:hh * 256 + 128], rope(q[:, hh * 256 + 128:(hh + 1) * 256])]
        ks += [kv[:, hh * 256:hh * 256 + 128], kr_r]
        vs += [kv[:, hh * 256 + 128:(hh + 1) * 256]]
    outs = dict(y_sgu=y_sgu, qg=qg, kg=gk, vg=gv, lgf=lgf, lgb=lgb, gr=gr,
                q_cat=jnp.concatenate(qs, axis=-1) * ATT_SCALE_LOG2, k_cat=jnp.concatenate(ks, axis=-1), v=jnp.concatenate(vs, axis=-1))
    return outs, dict(a_in=h)


def attn_out_tile(d, c, z):
    mod = d["mod"]
    o = d["o_f"] + d["o_b"]
    ms = jnp.zeros_like(o)
    for hh in range(HEADS):
        m_h = _lane_group_mask(256, 64, hh)
        ms = ms + m_h * (jnp.sum(o * o * m_h, axis=-1, keepdims=True) * (1.0 / 64))
    yg = o * lax.rsqrt(ms + EPS) * d["gnw"] * _silu(d["gr"])
    y = jnp.concatenate([d["y_sgu"], yg, d["y_mla"]], axis=-1)
    t = nn_const(c["w_out"], z is not None)(y)
    if z is not None:
        t = t + z["zt"]
    return dict(x1=d["x"] + mod[2:3] * t), dict(a_out=y)


def ffn_tile(d, c, z):
    mod = d["mod"]
    h2 = _rms(d["x1"], d["n2w"]) * (1.0 + mod[4:5]) + mod[3:4]
    f = None
    a2s = []
    for j in range(D_FF // FF_CH):
        pre = nn_const(c["w_ff1"][j], z is not None)(h2)
        if z is not None:
            pre = pre + z["zpre"][:, j * FF_CH:(j + 1) * FF_CH]
        a = jnp.maximum(pre, 0.0)
        a2 = a * a
        a2s.append(a2)
        fj = nn_const(c["w_ff2"][j], z is not None)(a2)
        f = fj if f is None else f + fj
    if z is not None:
        f = f + z["zf"]
    return dict(x2=d["x1"] + mod[5:6] * f, f=f), dict(a_ff1=h2, a_ff2=jnp.concatenate(a2s, axis=-1))


def _in_spec(kind, arr, tile):
    if kind == "tile":
        return pl.BlockSpec((tile, arr.shape[1]), lambda i: (i, 0))
    if kind == "kind":
        return pl.BlockSpec((1,) + arr.shape[1:], lambda i: (jnp.where(i < TILE // tile, 0, 1), 0, 0))
    nd = arr.ndim
    if kind in ("wfull", "wcols", "wrows"):
        return pl.BlockSpec(arr.shape, lambda i: (0,) * nd, pipeline_mode=pl.Buffered(1))
    return pl.BlockSpec(arr.shape, lambda i: (0,) * nd)


def _load(kind, ref):
    if kind == "kind":
        return ref[0]
    if kind == "wcols":
        return [ref[:, j * FF_CH:(j + 1) * FF_CH] for j in range(ref.shape[1] // FF_CH)]
    if kind == "wrows":
        return [ref[j * FF_CH:(j + 1) * FF_CH, :] for j in range(ref.shape[0] // FF_CH)]
    return ref[...]


def tile_forward(fn, name, t_all, ins, out_defs, tile=TILE):
    keys = [k for k, _, _, _ in ins]
    kinds = [kd for _, kd, _, _ in ins]
    diffs = [df for _, _, df, _ in ins]
    arrs = [a for _, _, _, a in ins]
    n_in = len(ins)

    def body(*refs):
        vals = [_load(kinds[j], refs[j]) for j in range(n_in)]
        d = {keys[j]: vals[j] for j in range(n_in) if diffs[j]}
        c = {keys[j]: vals[j] for j in range(n_in) if not diffs[j]}
        outs, _ = fn(d, c, None)
        for j, (k, _, dt) in enumerate(out_defs):
            refs[n_in + j][...] = outs[k].astype(dt)

    res = pl.pallas_call(
        body, name=name, grid=(t_all // tile,),
        in_specs=[_in_spec(kinds[j], arrs[j], tile) for j in range(n_in)],
        out_specs=[pl.BlockSpec((tile, w), lambda i: (i, 0)) for _, w, _ in out_defs],
        out_shape=[jax.ShapeDtypeStruct((t_all, w), dt) for _, w, dt in out_defs],
        compiler_params=_cparams(1),
    )(*arrs)
    return {k: r for (k, _, _), r in zip(out_defs, res)}


def tile_backward(fn, name, t_all, ins, cots, z_defs, aux_defs, tile=TILE, resid=None):
    keys = [k for k, _, _, _ in ins]
    kinds = [kd for _, kd, _, _ in ins]
    diffs = [df for _, _, df, _ in ins]
    arrs = [a for _, _, _, a in ins]
    cot_keys, cot_arrs = [], []
    for k, a in cots:
        for one in (a if isinstance(a, (list, tuple)) else [a]):
            cot_keys.append(k)
            cot_arrs.append(one)
    if resid is not None:
        cot_keys.append("resid:" + resid[0])
        cot_arrs.append(resid[1])
    n_in, n_cot = len(ins), len(cot_arrs)
    dkeys = [j for j in range(n_in) if diffs[j]]
    ctx_tiles = TILE // tile

    def body(*refs):
        i = pl.program_id(0)
        vals = [_load(kinds[j], refs[j]) for j in range(n_in)]
        d = {keys[j]: vals[j] for j in dkeys}
        c = {keys[j]: vals[j] for j in range(n_in) if not diffs[j]}
        zs = {k: jnp.zeros((tile, w), F32) for k, w in z_defs}
        outs, vjp_fn, aux = jax.vjp(lambda dd, zz: fn(dd, c, zz), d, zs, has_aux=True)
        ct = {}
        for j, k in enumerate(cot_keys):
            ct[k] = refs[n_in + j][...] + ct[k] if k in ct else refs[n_in + j][...]
        dd, dz = vjp_fn({k: ct[k].astype(outs[k].dtype) for k in outs})
        base = n_in + n_cot
        for n, j in enumerate(dkeys):
            ref, g = refs[base + n], dd[keys[j]]
            if kinds[j] == "tile":
                ref[...] = g + ct["resid:" + keys[j]] if "resid:" + keys[j] in ct else g
            else:
                first = ((i == 0) | (i == ctx_tiles)) if kinds[j] == "kind" else (i == 0)
                gv = g[None] if kinds[j] == "kind" else g

                @pl.when(first)
                def _(ref=ref, gv=gv):
                    ref[...] = gv

                @pl.when(jnp.logical_not(first))
                def _(ref=ref, gv=gv):
                    ref[...] += gv
        base += len(dkeys)
        for n, (k, _) in enumerate(z_defs):
            refs[base + n][...] = dz[k].astype(BF)
        base += len(z_defs)
        for n, (k, _) in enumerate(aux_defs):
            refs[base + n][...] = aux[k].T.astype(BF)

    out_specs, out_shape = [], []
    for j in dkeys:
        out_specs.append(_in_spec(kinds[j], arrs[j], tile))
        out_shape.append(jax.ShapeDtypeStruct(arrs[j].shape, F32))
    for _, w in z_defs:
        out_specs.append(pl.BlockSpec((tile, w), lambda i: (i, 0)))
        out_shape.append(jax.ShapeDtypeStruct((t_all, w), BF))
    for _, w in aux_defs:
        out_specs.append(pl.BlockSpec((w, tile), lambda i: (0, i)))
        out_shape.append(jax.ShapeDtypeStruct((w, t_all), BF))
    res = pl.pallas_call(
        body, name=name, grid=(t_all // tile,),
        in_specs=[_in_spec(kinds[j], arrs[j], tile) for j in range(n_in)]
        + [pl.BlockSpec((tile, a.shape[1]), lambda i: (i, 0)) for a in cot_arrs],
        out_specs=out_specs, out_shape=out_shape, compiler_params=_cparams(1),
    )(*arrs, *cot_arrs)
    grads = {keys[j]: res[n] for n, j in enumerate(dkeys)}
    extra = {k: res[len(dkeys) + n] for n, (k, _) in enumerate(list(z_defs) + list(aux_defs))}
    return grads, extra


def ffn_backward(x1, modl, n2w, w1, w2, dx2, f, name):
    t_all = x1.shape[0]
    n_ch = D_FF // FF_CH

    def head(x, mod, nw):
        return _rms(x, nw) * (1.0 + mod[4:5]) + mod[3:4]

    def body(x_ref, mod_ref, nw_ref, w1_ref, w2_ref, dx2_ref, f_ref, dx1_ref, dmod_ref, dnw_ref, zpre_ref, zf_ref, a1_ref,
             a2_ref):
        i = pl.program_id(0)
        mod = mod_ref[0]
        dx2 = dx2_ref[...]
        h2, vjp_head = jax.vjp(head, x_ref[...], mod, nw_ref[...])
        h2b = h2.astype(BF)
        dfb = (dx2 * mod[5:6]).astype(BF)
        f = f_ref[...]
        dh2 = jnp.zeros((TILE, D), F32)
        for j in range(n_ch):
            cs = slice(j * FF_CH, (j + 1) * FF_CH)
            a = jnp.maximum(_dot(h2b, w1_ref[:, cs]), 0.0)
            a2b = (a * a).astype(BF)
            dpre = (_dot_nt(dfb, w2_ref[cs, :]) * (2.0 * a)).astype(BF)
            dh2 = dh2 + _dot_nt(dpre, w1_ref[:, cs])
            zpre_ref[:, cs] = dpre
            a2_ref[:, cs] = a2b
        zf_ref[...] = (dx2 * mod[5:6]).T.astype(BF)
        a1_ref[...] = h2.T.astype(BF)
        dx1, dmod, dnw = vjp_head(dh2)
        dx1_ref[...] = dx2 + dx1
        row = lax.broadcasted_iota(jnp.int32, (8, D), 0)
        dmod = dmod + jnp.where(row == 5, jnp.sum(dx2 * f, axis=0, keepdims=True), 0.0)
        first_kind = (i == 0) | (i == 1)

        @pl.when(first_kind)
        def _():
            dmod_ref[0] = dmod

        @pl.when(jnp.logical_not(first_kind))
        def _():
            dmod_ref[0] += dmod

        @pl.when(i == 0)
        def _():
            dnw_ref[...] = dnw

        @pl.when(i > 0)
        def _():
            dnw_ref[...] += dnw

    tspec = lambda w: pl.BlockSpec((TILE, w), lambda i: (i, 0))
    once = lambda shp: pl.BlockSpec(shp, lambda i: (0, 0), pipeline_mode=pl.Buffered(1))
    kind = pl.BlockSpec((1, 8, D), lambda i: (jnp.minimum(i, 1), 0, 0))
    tr = pl.BlockSpec((D, TILE), lambda i: (0, i))
    return pl.pallas_call(
        body, name=name, grid=(t_all // TILE,),
        in_specs=[tspec(D), kind, pl.BlockSpec((1, D), lambda i: (0, 0)), once((D, D_FF)), once((D_FF, D)), tspec(D),
                  tspec(D)],
        out_specs=[tspec(D), kind, pl.BlockSpec((1, D), lambda i: (0, 0)), tspec(D_FF), tr, tr, tspec(D_FF)],
        out_shape=[jax.ShapeDtypeStruct((t_all, D), F32), jax.ShapeDtypeStruct((2, 8, D), F32),
                   jax.ShapeDtypeStruct((1, D), F32), jax.ShapeDtypeStruct((t_all, D_FF), BF),
                   jax.ShapeDtypeStruct((D, t_all), BF), jax.ShapeDtypeStruct((D, t_all), BF),
                   jax.ShapeDtypeStruct((t_all, D_FF), BF)],
        compiler_params=_cparams(1),
    )(x1, modl, n2w, w1, w2, dx2, f)


WG_TOK = 768


def wgrad(at, b, name, bk2=1024):
    k1, t = at.shape
    k2 = b.shape[1]
    bk2 = min(bk2, k2)
    tt = WG_TOK if t % WG_TOK == 0 else TILE
    nt_ = t // tt

    def body(a_ref, b_ref, o_ref, acc):
        s = pl.program_id(1)

        @pl.when(s == 0)
        def _():
            acc[...] = jnp.zeros_like(acc)

        acc[...] += _dot(a_ref[...], b_ref[...])

        @pl.when(s == nt_ - 1)
        def _():
            o_ref[...] = acc[...]

    return pl.pallas_call(
        body, name=name, grid=(k2 // bk2, nt_),
        in_specs=[pl.BlockSpec((k1, tt), lambda j, s: (0, s)), pl.BlockSpec((tt, bk2), lambda j, s: (s, j))],
        out_specs=pl.BlockSpec((k1, bk2), lambda j, s: (0, j)),
        out_shape=jax.ShapeDtypeStruct((k1, k2), F32),
        scratch_shapes=[pltpu.VMEM((k1, bk2), F32)],
        compiler_params=_cparams(2),
    )(at, b)


def _gla_consts(reverse, diff):
    r = lax.broadcasted_iota(jnp.int32, (GCH, GCH), 0)
    cc = lax.broadcasted_iota(jnp.int32, (GCH, GCH), 1)
    low = (r >= cc)
    tri = (jnp.logical_not(low) | (r == cc)) if reverse else low
    tri_f = tri.astype(F32)
    tri_t = (cc >= r) if not reverse else (cc <= r)
    hmk = [_lane_group_mask(128, 32, h) for h in range(HEADS)]
    hmv = [_lane_group_mask(256, 64, h) for h in range(HEADS)]
    e = lax.broadcasted_iota(jnp.int32, (256, 128), 0) // 64
    dk = lax.broadcasted_iota(jnp.int32, (256, 128), 1) // 32
    return dict(cum=make_cum(tri_f.astype(BF), tri_t.astype(F32).astype(BF), diff), ops=_ops(diff), reverse=reverse,
                tri4=jnp.concatenate([tri_f] * HEADS, axis=0), hmk=hmk, hmv=hmv, bd=(e == dk).astype(F32))


def gla_chunk(st, q, k, v, g, cs):
    nn, nt, tn = cs["ops"]
    b = cs["cum"](g)
    bl = jnp.sum(g, axis=0, keepdims=True)
    b_ref = jnp.sum(g[GCH // 2:] if cs["reverse"] else g[:GCH // 2], axis=0, keepdims=True)
    qe = q * jnp.exp(b)
    qs = q * jnp.exp(b - b_ref)
    ks = k * jnp.exp(b_ref - b)
    qstack = jnp.concatenate([qs * cs["hmk"][h] for h in range(HEADS)], axis=0)
    att = nt(qstack, ks) * cs["tri4"]
    ofull = nn(att, v)
    o = nt(qe, st)
    for h in range(HEADS):
        o = o + ofull[h * GCH:(h + 1) * GCH] * cs["hmv"][h]
    kd = k * jnp.exp(bl - b)
    st_new = st * jnp.exp(bl) + tn(v, kd) * cs["bd"]
    return st_new, o


def _gla_chunk_index(s, n_ch, reverse):
    ctx_ch = TILE // GCH
    if not reverse:
        return s
    return jnp.where(s < ctx_ch, ctx_ch - 1 - s, n_ch - 1 + ctx_ch - s)


def gla_forward(q, k, v, gf, gb, name):
    t = q.shape[0]
    n_ch = t // GCH

    def body(*refs):
        s = pl.program_id(0)
        for dr, reverse in enumerate((False, True)):
            q_ref, k_ref, v_ref, g_ref = refs[4 * dr:4 * dr + 4]
            o_ref, sst_ref = refs[8 + 2 * dr:8 + 2 * dr + 2]
            st = refs[12 + dr]

            @pl.when(s == 0)
            def _(st=st):
                st[...] = jnp.zeros_like(st)

            cur = st[...]
            sst_ref[0] = cur
            st_new, o = gla_chunk(cur, q_ref[...], k_ref[...], v_ref[...], g_ref[...], _gla_consts(reverse, False))
            o_ref[...] = o
            st[...] = st_new

    in_specs, out_specs, out_shape = [], [], []
    for reverse in (False, True):
        im = lambda s, reverse=reverse: (_gla_chunk_index(s, n_ch, reverse), 0)
        im3 = lambda s, reverse=reverse: (_gla_chunk_index(s, n_ch, reverse), 0, 0)
        in_specs += [pl.BlockSpec((GCH, 128), im), pl.BlockSpec((GCH, 128), im), pl.BlockSpec((GCH, 256), im),
                     pl.BlockSpec((GCH, 128), im)]
        out_specs += [pl.BlockSpec((GCH, 256), im), pl.BlockSpec((1, 256, 128), im3)]
        out_shape += [jax.ShapeDtypeStruct((t, 256), F32), jax.ShapeDtypeStruct((n_ch, 256, 128), F32)]
    return pl.pallas_call(
        body, name=name, grid=(n_ch,), in_specs=in_specs, out_specs=out_specs, out_shape=out_shape,
        scratch_shapes=[pltpu.VMEM((256, 128), F32), pltpu.VMEM((256, 128), F32)],
        compiler_params=_cparams(1),
    )(q, k, v, gf, q, k, v, gb)


def gla_backward(q, k, v, gf, gb, sst_f, sst_b, do, name):
    t = q.shape[0]
    n_ch = t // GCH

    def body(*refs):
        r = pl.program_id(0)
        for dr, reverse in enumerate((False, True)):
            q_ref, k_ref, v_ref, g_ref, sst_ref, do_ref = refs[6 * dr:6 * dr + 6]
            outs = refs[12 + 4 * dr:12 + 4 * dr + 4]
            dst = refs[20 + dr]

            @pl.when(r == 0)
            def _(dst=dst):
                dst[...] = jnp.zeros_like(dst)

            cs = _gla_consts(reverse, True)
            _, vjp_fn = jax.vjp(lambda a, b, c_, d_, e_, cs=cs: gla_chunk(a, b, c_, d_, e_, cs),
                                sst_ref[0], q_ref[...], k_ref[...], v_ref[...], g_ref[...])
            grads = vjp_fn((dst[...], do_ref[...]))
            for o_ref, gval in zip(outs, grads[1:]):
                o_ref[...] = gval
            dst[...] = grads[0]

    in_specs, out_specs, out_shape = [], [], []
    for reverse in (False, True):
        im = lambda r, reverse=reverse: (_gla_chunk_index(n_ch - 1 - r, n_ch, reverse), 0)
        im3 = lambda r, reverse=reverse: (_gla_chunk_index(n_ch - 1 - r, n_ch, reverse), 0, 0)
        in_specs += [pl.BlockSpec((GCH, 128), im), pl.BlockSpec((GCH, 128), im), pl.BlockSpec((GCH, 256), im),
                     pl.BlockSpec((GCH, 128), im), pl.BlockSpec((1, 256, 128), im3), pl.BlockSpec((GCH, 256), im)]
        out_specs += [pl.BlockSpec((GCH, 128), im), pl.BlockSpec((GCH, 128), im), pl.BlockSpec((GCH, 256), im),
                      pl.BlockSpec((GCH, 128), im)]
        out_shape += [jax.ShapeDtypeStruct((t, 128), F32), jax.ShapeDtypeStruct((t, 128), F32),
                      jax.ShapeDtypeStruct((t, 256), F32), jax.ShapeDtypeStruct((t, 128), F32)]
    return pl.pallas_call(
        body, name=name, grid=(n_ch,), in_specs=in_specs, out_specs=out_specs, out_shape=out_shape,
        scratch_shapes=[pltpu.VMEM((256, 128), F32), pltpu.VMEM((256, 128), F32)],
        compiler_params=_cparams(1),
    )(q, k, v, gf, sst_f, do, q, k, v, gb, sst_b, do)


def _resident(hbm_ref, vmem_ref, sem):
    cp = pltpu.make_async_copy(hbm_ref, vmem_ref, sem)
    cp.start()
    cp.wait()


def mla_forward(q_cat, k_cat, v, name):
    t = q_cat.shape[0]
    n_t = t // TILE

    ch = KV_CH_FWD if (t - TILE) % KV_CH_FWD == 0 else KV_CH
    n_main = (t - TILE) // ch

    def body(q_ref, k_hbm, v_hbm, o_ref, lse_ref, k_s, v_s, m_s, l_s, acc_s, sem):
        i = pl.program_id(0)

        @pl.when(i == 0)
        def _():
            _resident(k_hbm, k_s, sem.at[0])
            _resident(v_hbm, v_s, sem.at[1])

        m_s[...] = jnp.full(m_s.shape, -1e30, F32)
        l_s[...] = jnp.zeros_like(l_s)
        acc_s[...] = jnp.zeros_like(acc_s)

        def chunk(r0, size):
            for h in range(HEADS):
                kh = k_s[pl.ds(r0, size), h * 256:(h + 1) * 256]
                vh = v_s[pl.ds(r0, size), h * 128:(h + 1) * 128]
                s = _dot_nt(q_ref[:, h * 256:(h + 1) * 256], kh)
                m_prev = m_s[h]
                m_next = jnp.maximum(m_prev, jnp.max(s, axis=-1, keepdims=True))
                p = jnp.exp2(s - jnp.tile(m_next, (1, size // 128)))
                alpha = jnp.exp2(m_prev - m_next)
                l_s[h] = alpha * l_s[h] + jnp.sum(p, axis=-1, keepdims=True)
                acc_s[h] = alpha * acc_s[h] + _dot(p.astype(BF), vh)
                m_s[h] = m_next

        chunk(0, TILE)

        @pl.when(i >= 1)
        def _():
            def step(c, carry):
                chunk(pl.multiple_of(TILE + c * ch, TILE), ch)
                return carry

            lax.fori_loop(0, n_main, step, 0, unroll=MLA_UNROLL)

        lane = lax.broadcasted_iota(jnp.int32, (TILE, 128), 1)
        cols = jnp.zeros((TILE, 128), F32)
        for h in range(HEADS):
            o_ref[:, h * 128:(h + 1) * 128] = acc_s[h] / l_s[h]
            cols = jnp.where(lane == h, m_s[h] + jnp.log2(l_s[h]), cols)
        lse_ref[...] = cols.T[0:8, :]

    return pl.pallas_call(
        body, name=name, grid=(n_t,),
        in_specs=[pl.BlockSpec((TILE, 1024), lambda i: (i, 0)), pl.BlockSpec(memory_space=pl.ANY),
                  pl.BlockSpec(memory_space=pl.ANY)],
        out_specs=[pl.BlockSpec((TILE, 512), lambda i: (i, 0)), pl.BlockSpec((8, TILE), lambda i: (0, i))],
        out_shape=[jax.ShapeDtypeStruct((t, 512), F32), jax.ShapeDtypeStruct((8, t), F32)],
        scratch_shapes=[pltpu.VMEM((t, 1024), BF), pltpu.VMEM((t, 512), BF), pltpu.VMEM((HEADS, TILE, 128), F32),
                        pltpu.VMEM((HEADS, TILE, 128), F32), pltpu.VMEM((HEADS, TILE, 128), F32),
                        pltpu.SemaphoreType.DMA((2,))],
        compiler_params=_cparams(1),
    )(q_cat, k_cat, v)


def mla_delta(do, o, name):
    t = do.shape[0]

    def body(do_ref, o_ref, dl_ref, dob_ref):
        d = do_ref[...]
        prod = d * o_ref[...]
        rows = [jnp.sum(prod[:, h * 128:(h + 1) * 128], axis=-1, keepdims=True) for h in range(HEADS)]
        cols = jnp.concatenate(rows + [jnp.zeros((TILE, 128 - HEADS), F32)], axis=-1)
        dl_ref[...] = cols.T[0:8, :]
        dob_ref[...] = d.astype(BF)

    return pl.pallas_call(
        body, name=name, grid=(t // TILE,),
        in_specs=[pl.BlockSpec((TILE, 512), lambda i: (i, 0)), pl.BlockSpec((TILE, 512), lambda i: (i, 0))],
        out_specs=[pl.BlockSpec((8, TILE), lambda i: (0, i)), pl.BlockSpec((TILE, 512), lambda i: (i, 0))],
        out_shape=[jax.ShapeDtypeStruct((8, t), F32), jax.ShapeDtypeStruct((t, 512), BF)],
        compiler_params=_cparams(1),
    )(do, o)


def mla_backward(q_cat, k_cat, v, lse_rows, dl_rows, do_bf, name):
    t = q_cat.shape[0]
    n_t = t // TILE
    ch = Q_CH_BWD if (t - TILE) % Q_CH_BWD == 0 else KV_CH
    n_main = (t - TILE) // ch

    def body(q_hbm, do_hbm, k_ref, v_ref, lse_ref, dl_ref, dq_ref, dk_ref, dv_ref, q_s, do_s, dk_s, dv_s, sem):
        h, j = pl.program_id(0), pl.program_id(1)

        @pl.when(j == 0)
        def _():
            _resident(q_hbm.at[:, pl.ds(pl.multiple_of(h * 256, 256), 256)], q_s, sem.at[0])
            _resident(do_hbm.at[:, pl.ds(pl.multiple_of(h * 128, 128), 128)], do_s, sem.at[1])
            dq_ref[...] = jnp.zeros_like(dq_ref)

        dk_s[...] = jnp.zeros_like(dk_s)
        dv_s[...] = jnp.zeros_like(dv_s)
        kh = k_ref[...]
        vh = v_ref[...]

        def chunk(r0, size):
            qh = q_s[pl.ds(r0, size), :]
            doh = do_s[pl.ds(r0, size), :]
            pt = jnp.exp2(_dot_nt(kh, qh) - lse_ref[pl.ds(h, 1), pl.ds(r0, size)])
            dst = (pt * (_dot_nt(vh, doh) - dl_ref[pl.ds(h, 1), pl.ds(r0, size)])).astype(BF)
            dv_s[...] += _dot(pt.astype(BF), doh)
            dk_s[...] += _dot(dst, qh)
            dq_ref[pl.ds(r0, size), :] += _dot_tn(dst, kh)

        @pl.when(j == 0)
        def _():
            chunk(0, TILE)

        def step(c, carry):
            chunk(pl.multiple_of(TILE + c * ch, TILE), ch)
            return carry

        lax.fori_loop(0, n_main, step, 0, unroll=MLA_UNROLL)
        dk_ref[...] = dk_s[...] * LN2
        dv_ref[...] = dv_s[...]

        @pl.when(j == n_t - 1)
        def _():
            dq_ref[...] = dq_ref[...] * LN2

    rows = pl.BlockSpec((8, t), lambda h, j: (0, 0))
    hbm = pl.BlockSpec(memory_space=pl.ANY)
    return pl.pallas_call(
        body, name=name, grid=(HEADS, n_t),
        in_specs=[hbm, hbm, pl.BlockSpec((TILE, 256), lambda h, j: (j, h)), pl.BlockSpec((TILE, 128), lambda h, j: (j, h)),
                  rows, rows],
        out_specs=[pl.BlockSpec((t, 256), lambda h, j: (0, h)), pl.BlockSpec((TILE, 256), lambda h, j: (j, h)),
                   pl.BlockSpec((TILE, 128), lambda h, j: (j, h))],
        out_shape=[jax.ShapeDtypeStruct((t, 1024), F32), jax.ShapeDtypeStruct((t, 1024), F32),
                   jax.ShapeDtypeStruct((t, 512), F32)],
        scratch_shapes=[pltpu.VMEM((t, 256), BF), pltpu.VMEM((t, 128), BF), pltpu.VMEM((TILE, 256), F32),
                        pltpu.VMEM((TILE, 128), F32), pltpu.SemaphoreType.DMA((2,))],
        compiler_params=_cparams(2),
    )(q_cat, do_bf, k_cat, v, lse_rows, dl_rows)


def final_loss(xf, target, fnw, name):
    t = xf.shape[0]
    n_t = t // TILE

    def body(x_ref, t_ref, w_ref, loss_ref, dx_ref, dw_ref):
        i = pl.program_id(0)

        @pl.when(i == 0)
        def _():
            loss_ref[...] = jnp.zeros_like(loss_ref)
            dw_ref[...] = jnp.zeros_like(dw_ref)
            dx_ref[...] = jnp.zeros_like(dx_ref)

        @pl.when(i >= 1)
        def _():
            y, vjp_fn = jax.vjp(_rms, x_ref[...], w_ref[...])
            err = y - t_ref[...]
            loss_ref[...] += jnp.broadcast_to(0.5 * jnp.sum(jnp.mean(err * err, axis=-1, keepdims=True)), (8, 128))
            dx, dw = vjp_fn(err * (1.0 / D))
            dx_ref[...] = dx
            dw_ref[...] += dw

    return pl.pallas_call(
        body, name=name, grid=(n_t,),
        in_specs=[pl.BlockSpec((TILE, D), lambda i: (i, 0)), pl.BlockSpec((TILE, D), lambda i: (jnp.maximum(i - 1, 0), 0)),
                  pl.BlockSpec((1, D), lambda i: (0, 0))],
        out_specs=[pl.BlockSpec((8, 128), lambda i: (0, 0)), pl.BlockSpec((TILE, D), lambda i: (i, 0)),
                   pl.BlockSpec((1, D), lambda i: (0, 0))],
        out_shape=[jax.ShapeDtypeStruct((8, 128), F32), jax.ShapeDtypeStruct((t, D), F32),
                   jax.ShapeDtypeStruct((1, D), F32)],
        compiler_params=_cparams(1),
    )(xf, target, fnw)


def all_gather(xs, name):
    n = len(xs)
    blks = [tuple(x.shape) for x in xs]
    per = N_DEV - 1

    def body(*refs):
        x_refs, o_refs = refs[:n], refs[n:2 * n]
        ssem, rsem, lsem = refs[2 * n:]
        xi, yi, ci = lax.axis_index("x"), lax.axis_index("y"), lax.axis_index("c")
        me3 = (xi, yi, ci)
        me = 4 * xi + 2 * yi + ci
        flat = lambda d: 4 * d[0] + 2 * d[1] + d[2]
        sibling = (xi, yi, 1 - ci)
        chips = [(1 - xi, yi), (xi, 1 - yi), (1 - xi, 1 - yi)]

        def copy(a, k, block, to, src=None):
            rows = o_refs[a].at[flat(block)]
            return pltpu.make_async_remote_copy(
                src_ref=rows if src is None else src, dst_ref=rows,
                send_sem=ssem.at[a * per + k], recv_sem=rsem.at[a * per + k],
                device_id=to, device_id_type=pl.DeviceIdType.MESH)

        own = [pltpu.make_async_copy(x_refs[a], o_refs[a].at[me], lsem.at[a]) for a in range(n)]
        first = []
        for a in range(n):
            first.append(copy(a, 0, me3, sibling, src=x_refs[a]))
            first += [copy(a, 1 + j, me3, (*chip, ci), src=x_refs[a]) for j, chip in enumerate(chips)]
        for cp in own + first:
            cp.start()
        passed = []
        for j, chip in enumerate(chips):
            for a in range(n):
                copy(a, 1 + j, (*chip, ci), me3).wait_recv()
                fw = copy(a, 4 + j, (*chip, ci), sibling)
                fw.start()
                passed.append(fw)
        for a in range(n):
            copy(a, 0, sibling, me3).wait_recv()
            for j, chip in enumerate(chips):
                copy(a, 4 + j, (*chip, 1 - ci), me3).wait_recv()
        for cp in first + passed:
            cp.wait_send()
        for cp in own:
            cp.wait()

    hbm = pl.BlockSpec(memory_space=pl.ANY)
    res = pl.pallas_call(
        body, name=name, in_specs=[hbm] * n, out_specs=[hbm] * n,
        out_shape=[jax.ShapeDtypeStruct((N_DEV,) + blks[a], xs[a].dtype) for a in range(n)],
        scratch_shapes=[pltpu.SemaphoreType.DMA((n * per,)), pltpu.SemaphoreType.DMA((n * per,)),
                        pltpu.SemaphoreType.DMA((n,))],
        compiler_params=pltpu.CompilerParams(has_side_effects=True),
    )(*xs)
    return list(res)


def reduce_scatter(xs, name):
    n = len(xs)
    ci = lax.axis_index("c")
    hbm = pl.BlockSpec(memory_space=pl.ANY)
    side = pltpu.CompilerParams(has_side_effects=True)

    def pair_body(*refs):
        x_refs, o_refs = refs[:n], refs[n:2 * n]
        ssem, rsem = refs[2 * n:]
        xi, yi, cc = lax.axis_index("x"), lax.axis_index("y"), lax.axis_index("c")
        cps = []
        for a in range(n):
            for q in range(4):
                cps.append(pltpu.make_async_remote_copy(
                    src_ref=x_refs[a].at[2 * q + (1 - cc)], dst_ref=o_refs[a].at[q],
                    send_sem=ssem.at[4 * a + q], recv_sem=rsem.at[4 * a + q],
                    device_id=(xi, yi, 1 - cc), device_id_type=pl.DeviceIdType.MESH))
        for cp in cps:
            cp.start()
        for cp in cps:
            cp.wait()

    halves = [(4,) + tuple(x.shape[1:]) for x in xs]
    got = pl.pallas_call(
        pair_body, name=name + "_pair", in_specs=[hbm] * n, out_specs=[hbm] * n,
        out_shape=[jax.ShapeDtypeStruct(halves[a], xs[a].dtype) for a in range(n)],
        scratch_shapes=[pltpu.SemaphoreType.DMA((4 * n,)), pltpu.SemaphoreType.DMA((4 * n,))], compiler_params=side,
    )(*xs)

    sums = []
    for a in range(n):
        mine = lax.dynamic_index_in_dim(xs[a].reshape((4, 2) + tuple(xs[a].shape[1:])), ci, axis=1, keepdims=False)

        def add_body(a_ref, b_ref, o_ref):
            o_ref[...] = (a_ref[...].astype(F32) + b_ref[...].astype(F32)).astype(o_ref.dtype)

        blk = pl.BlockSpec((1,) + halves[a][1:], lambda q: (q, 0, 0))
        sums.append(pl.pallas_call(add_body, name=f"{name}_sum{a}", grid=(4,), in_specs=[blk, blk], out_specs=blk,
                                   out_shape=jax.ShapeDtypeStruct(halves[a], xs[a].dtype),
                                   compiler_params=_cparams(1))(mine, got[a]))

    def chips_body(*refs):
        s_refs, o_refs = refs[:n], refs[n:2 * n]
        ssem, rsem, lsem = refs[2 * n:]
        xi, yi, cc = lax.axis_index("x"), lax.axis_index("y"), lax.axis_index("c")
        my_chip = 2 * xi + yi
        chips = [(1 - xi, yi), (xi, 1 - yi), (1 - xi, 1 - yi)]
        own = [pltpu.make_async_copy(s_refs[a].at[my_chip], o_refs[a].at[my_chip], lsem.at[a]) for a in range(n)]
        sends, lands = [], []
        for a in range(n):
            for j, (px, py) in enumerate(chips):
                common = dict(send_sem=ssem.at[3 * a + j], recv_sem=rsem.at[3 * a + j], device_id=(px, py, cc),
                              device_id_type=pl.DeviceIdType.MESH)
                sends.append(pltpu.make_async_remote_copy(src_ref=s_refs[a].at[2 * px + py],
                                                          dst_ref=o_refs[a].at[my_chip], **common))
                lands.append(pltpu.make_async_remote_copy(src_ref=s_refs[a].at[2 * px + py],
                                                          dst_ref=o_refs[a].at[2 * px + py], **common))
        for cp in own + sends:
            cp.start()
        for cp in lands:
            cp.wait_recv()
        for cp in sends:
            cp.wait_send()
        for cp in own:
            cp.wait()

    res = pl.pallas_call(
        chips_body, name=name + "_chips", in_specs=[hbm] * n, out_specs=[hbm] * n,
        out_shape=[jax.ShapeDtypeStruct(halves[a], xs[a].dtype) for a in range(n)],
        scratch_shapes=[pltpu.SemaphoreType.DMA((3 * n,)), pltpu.SemaphoreType.DMA((3 * n,)),
                        pltpu.SemaphoreType.DMA((n,))], compiler_params=side,
    )(*sums)
    return list(res)


def mod_forward(crows, w_mod, b_shard, name):
    cols = w_mod.shape[2]

    def body(c_ref, w_ref, b_ref, o_ref):
        o_ref[0] = _dot(_silu(c_ref[...]).astype(BF), w_ref[0].astype(BF)) + b_ref[0]

    return pl.pallas_call(
        body, name=name, grid=(2,),
        in_specs=[pl.BlockSpec((16, D), lambda l: (0, 0)), pl.BlockSpec((1, D, cols), lambda l: (l, 0, 0)),
                  pl.BlockSpec((1, 1, cols), lambda l: (l, 0, 0))],
        out_specs=pl.BlockSpec((1, 16, cols), lambda l: (l, 0, 0)),
        out_shape=jax.ShapeDtypeStruct((2, 16, cols), F32), compiler_params=_cparams(1),
    )(crows, w_mod, b_shard)


def mod_backward(crows, w_mod, d_own, d_ctx, name):
    cols = w_mod.shape[2]

    def body(c_ref, w_ref, do_ref, dc_ref, gw_ref, gs_ref):
        dc = dc_ref[0]
        dsum = dc[0:1]
        for s in range(1, N_DEV):
            dsum = dsum + dc[s:s + 1]
        row = lax.broadcasted_iota(jnp.int32, (8, cols), 0)
        d16 = jnp.concatenate([do_ref[0], jnp.where(row == 0, jnp.broadcast_to(dsum, (8, cols)), 0.0)], axis=0)
        gw_ref[0] = _dot_tn(_silu(c_ref[...]).astype(BF), d16.astype(BF))
        gs_ref[0] = _dot_nt(jnp.broadcast_to(dsum, (8, cols)).astype(BF), w_ref[0].astype(BF))

    return pl.pallas_call(
        body, name=name, grid=(2,),
        in_specs=[pl.BlockSpec((16, D), lambda l: (0, 0)), pl.BlockSpec((1, D, cols), lambda l: (l, 0, 0)),
                  pl.BlockSpec((1, 8, cols), lambda l: (l, 0, 0)), pl.BlockSpec((1, 8, cols), lambda l: (l, 0, 0))],
        out_specs=[pl.BlockSpec((1, D, cols), lambda l: (l, 0, 0)), pl.BlockSpec((1, 8, D), lambda l: (l, 0, 0))],
        out_shape=[jax.ShapeDtypeStruct((2, D, cols), F32), jax.ShapeDtypeStruct((2, 8, D), F32)],
        compiler_params=_cparams(1),
    )(crows, w_mod, d_own, d_ctx)


def silu_grad_scale(c_ctx, ds, name):
    def body(c_ref, ds_ref, o_ref):
        cc = c_ref[...]
        sg = jax.nn.sigmoid(cc)
        o_ref[...] = (ds_ref[0][0:1] + ds_ref[1][0:1]) * (sg * (1.0 + cc * (1.0 - sg)))

    return pl.pallas_call(body, name=name, out_shape=jax.ShapeDtypeStruct((1, D), F32))(c_ctx, ds)


def _adamw_math(p_ref, w_ref, m_ref, v_ref, g_ref, d_ref, nm_ref, nv_ref):
    g = p_ref[0].astype(F32)
    for s in range(1, p_ref.shape[0]):
        g = g + p_ref[s].astype(F32)
    mm = ADAM_B1 * m_ref[...] + (1.0 - ADAM_B1) * g
    vv = ADAM_B2 * v_ref[...] + (1.0 - ADAM_B2) * (g * g)
    m_hat = mm / (1.0 - ADAM_B1 ** ADAM_STEP)
    v_hat = vv / (1.0 - ADAM_B2 ** ADAM_STEP)
    g_ref[...] = g
    d_ref[...] = -ADAM_LR * (m_hat / (jnp.sqrt(v_hat) + ADAM_EPS) + ADAM_WD * w_ref[...])
    nm_ref[...] = mm
    nv_ref[...] = vv


def adamw(parts, w, m, v, name):
    n_parts, rows, cols = parts.shape
    lanes = -(-cols // 128) * 128
    block_rows = min(rows, 1 << ((ADAMW_BLOCK_ELEMS // lanes).bit_length() - 1))
    assert rows % block_rows == 0

    def body(*refs):
        _adamw_math(*refs)

    spec = pl.BlockSpec((block_rows, cols), lambda i: (i, 0))
    return pl.pallas_call(
        body, name=name, grid=(rows // block_rows,),
        in_specs=[pl.BlockSpec((n_parts, block_rows, cols), lambda i: (0, i, 0)), spec, spec, spec],
        out_specs=[spec] * 4, out_shape=[jax.ShapeDtypeStruct((rows, cols), F32)] * 4,
        compiler_params=_cparams(1),
    )(parts, w, m, v)


def adamw_group(items, name):
    n = len(items)

    def body(*refs):
        for a in range(n):
            _adamw_math(*refs[4 * a:4 * a + 4], *refs[4 * n + 4 * a:4 * n + 4 * a + 4])

    flat_in = [x for it in items for x in it]
    out_shape = [jax.ShapeDtypeStruct(it[1].shape, F32) for it in items for _ in range(4)]
    res = pl.pallas_call(body, name=name, out_shape=out_shape,
                         compiler_params=pltpu.CompilerParams(vmem_limit_bytes=VMEM_LIMIT_MB * 1024 * 1024))(*flat_in)
    return [tuple(res[4 * a:4 * a + 4]) for a in range(n)]


def _pad_cols(w, segs, total):
    parts, pos = [], 0
    for dst, src, wd in segs:
        if dst > pos:
            parts.append(jnp.zeros(w.shape[:-1] + (dst - pos,), w.dtype))
        parts.append(w[..., src:src + wd])
        pos = dst + wd
    if pos < total:
        parts.append(jnp.zeros(w.shape[:-1] + (total - pos,), w.dtype))
    return jnp.concatenate(parts, axis=-1)


def _unpad_cols(g, segs):
    return jnp.concatenate([g[..., dst:dst + wd] for dst, _, wd in segs], axis=-1)


def _rope_tables(n_lat):
    rows = n_lat // GRID_W
    freq = ROPE_BASE ** (-jnp.arange(16, dtype=F32) * 2.0 / 32)
    a_row = jnp.arange(rows).astype(F32)[:, None] * freq[None, :]
    a_col = jnp.arange(GRID_W).astype(F32)[:, None] * freq[None, :]
    per_row = lambda tbl: jnp.repeat(tbl, GRID_W, axis=0)
    per_col = lambda tbl: jnp.tile(tbl, (rows, 1))
    cr, sr, cc, sc = per_row(jnp.cos(a_row)), per_row(jnp.sin(a_row)), per_col(jnp.cos(a_col)), per_col(jnp.sin(a_col))
    z = jnp.zeros((n_lat, 16), F32)
    cos = jnp.concatenate([cr, cr, cc, cc, jnp.ones((n_lat, 64), F32)], axis=1)
    sa = jnp.concatenate([-sr, z, -sc, z, jnp.zeros((n_lat, 64), F32)], axis=1)
    sb = jnp.concatenate([z, sr, z, sc, jnp.zeros((n_lat, 64), F32)], axis=1)
    ident = lambda fill: jnp.full((TILE, 128), fill, F32)
    return (jnp.concatenate([ident(1.0), cos]), jnp.concatenate([ident(0.0), sa]), jnp.concatenate([ident(0.0), sb]))


def _gathered_to_full(g, name):
    if name in ("w_out", "w_ff2"):
        return jnp.transpose(g, (1, 0, 2, 3)).reshape(2, -1, g.shape[-1])
    return jnp.transpose(g, (1, 2, 0, 3)).reshape(2, g.shape[2], -1)


def _full_to_chunks(gw, name):
    if name in ("w_out", "w_ff2"):
        return jnp.transpose(gw.reshape(2, N_DEV, -1, gw.shape[-1]), (1, 0, 2, 3))
    return jnp.transpose(gw.reshape(2, gw.shape[1], N_DEV, -1), (2, 0, 1, 3))


def kernel(x, c, ctx, c_ctx, w_mod, b_mod, norm1_w, w_in, w_out, sgu_norm_w, sgu_norm_b, sgu_w, sgu_b, gla_wg_fwd, gla_bg_fwd, gla_wg_bwd, gla_bg_bwd, gla_norm_w, mla_q_norm_w, mla_w_uq, mla_kv_norm_w, mla_w_ukv, norm2_w, w_ff1, w_ff2, final_norm_w, loss_target, m_c_ctx, m_w_mod, m_b_mod, m_norm1_w, m_w_in, m_w_out, m_sgu_norm_w, m_sgu_norm_b, m_sgu_w, m_sgu_b, m_gla_wg_fwd, m_gla_bg_fwd, m_gla_wg_bwd, m_gla_bg_bwd, m_gla_norm_w, m_mla_q_norm_w, m_mla_w_uq, m_mla_kv_norm_w, m_mla_w_ukv, m_norm2_w, m_w_ff1, m_w_ff2, m_final_norm_w, v_c_ctx, v_w_mod, v_b_mod, v_norm1_w, v_w_in, v_w_out, v_sgu_norm_w, v_sgu_norm_b, v_sgu_w, v_sgu_b, v_gla_wg_fwd, v_gla_bg_fwd, v_gla_wg_bwd, v_gla_bg_bwd, v_gla_norm_w, v_mla_q_norm_w, v_mla_w_uq, v_mla_kv_norm_w, v_mla_w_ukv, v_norm2_w, v_w_ff1, v_w_ff2, v_final_norm_w):
    W = dict(c_ctx=c_ctx, w_mod=w_mod, b_mod=b_mod, norm1_w=norm1_w, w_in=w_in, w_out=w_out, sgu_norm_w=sgu_norm_w,
             sgu_norm_b=sgu_norm_b, sgu_w=sgu_w, sgu_b=sgu_b, gla_wg_fwd=gla_wg_fwd, gla_bg_fwd=gla_bg_fwd,
             gla_wg_bwd=gla_wg_bwd, gla_bg_bwd=gla_bg_bwd, gla_norm_w=gla_norm_w, mla_q_norm_w=mla_q_norm_w,
             mla_w_uq=mla_w_uq, mla_kv_norm_w=mla_kv_norm_w, mla_w_ukv=mla_w_ukv, norm2_w=norm2_w, w_ff1=w_ff1,
             w_ff2=w_ff2, final_norm_w=final_norm_w)
    M = dict(c_ctx=m_c_ctx, w_mod=m_w_mod, b_mod=m_b_mod, norm1_w=m_norm1_w, w_in=m_w_in, w_out=m_w_out,
             sgu_norm_w=m_sgu_norm_w, sgu_norm_b=m_sgu_norm_b, sgu_w=m_sgu_w, sgu_b=m_sgu_b, gla_wg_fwd=m_gla_wg_fwd,
             gla_bg_fwd=m_gla_bg_fwd, gla_wg_bwd=m_gla_wg_bwd, gla_bg_bwd=m_gla_bg_bwd, gla_norm_w=m_gla_norm_w,
             mla_q_norm_w=m_mla_q_norm_w, mla_w_uq=m_mla_w_uq, mla_kv_norm_w=m_mla_kv_norm_w, mla_w_ukv=m_mla_w_ukv,
             norm2_w=m_norm2_w, w_ff1=m_w_ff1, w_ff2=m_w_ff2, final_norm_w=m_final_norm_w)
    V = dict(c_ctx=v_c_ctx, w_mod=v_w_mod, b_mod=v_b_mod, norm1_w=v_norm1_w, w_in=v_w_in, w_out=v_w_out,
             sgu_norm_w=v_sgu_norm_w, sgu_norm_b=v_sgu_norm_b, sgu_w=v_sgu_w, sgu_b=v_sgu_b, gla_wg_fwd=v_gla_wg_fwd,
             gla_bg_fwd=v_gla_bg_fwd, gla_wg_bwd=v_gla_wg_bwd, gla_bg_bwd=v_gla_bg_bwd, gla_norm_w=v_gla_norm_w,
             mla_q_norm_w=v_mla_q_norm_w, mla_w_uq=v_mla_w_uq, mla_kv_norm_w=v_mla_kv_norm_w, mla_w_ukv=v_mla_w_ukv,
             norm2_w=v_norm2_w, w_ff1=v_w_ff1, w_ff2=v_w_ff2, final_norm_w=v_final_norm_w)

    n_lat = x.shape[1]
    assert ctx.shape[1] == TILE and n_lat % TILE == 0 and x.shape[2] == D
    t_all = TILE + n_lat
    n_t = t_all // TILE
    me = 4 * lax.axis_index("x") + 2 * lax.axis_index("y") + lax.axis_index("c")
    mod_cols = w_mod.shape[2]

    c_all = all_gather([c], "ag_c")[0].reshape(N_DEV, D)
    crows = jnp.concatenate([c_all, c_ctx[None, :], jnp.zeros((7, D), F32)], axis=0)
    b_shard = lax.dynamic_slice_in_dim(b_mod, me * mod_cols, mod_cols, axis=1)[:, None, :]
    mod_sh = mod_forward(crows, w_mod, b_shard, "mod_fwd")
    mod_g = all_gather([mod_sh.reshape(32, mod_cols)], "ag_mod")[0]
    mod_full = jnp.transpose(mod_g.reshape(N_DEV, 2, 16, mod_cols), (1, 2, 0, 3)).reshape(2, 16, 6 * D)
    mod_own = lax.dynamic_index_in_dim(mod_full, me, axis=1, keepdims=False)
    mod_ctx = mod_full[:, 8, :]
    pad2 = jnp.zeros((2, D), F32)
    modl = [jnp.stack([jnp.concatenate([mod_ctx[l].reshape(6, D), pad2]),
                       jnp.concatenate([mod_own[l].reshape(6, D), pad2])]) for l in range(2)]

    v2 = lambda a: a[None, :] if a.ndim == 1 else a.reshape(-1, a.shape[-1])
    gathered = all_gather([v2(W[k].astype(BF)) for k in BIG_NAMES], "ag_weights")
    full = {k: _gathered_to_full(g.reshape((N_DEV,) + W[k].shape), k) for k, g in zip(BIG_NAMES, gathered)}
    w_in_p = _pad_cols(full["w_in"], W_IN_SEGS, P_COLS)
    w_uq_p = _pad_cols(full["mla_w_uq"], W_UQ_SEGS, 1024).astype(F32)
    w_ukv_f = full["mla_w_ukv"].astype(F32)
    wgf_p = jnp.pad(gla_wg_fwd, ((0, 0), (0, 112), (0, 0)))
    wgb_p = jnp.pad(gla_wg_bwd, ((0, 0), (0, 112), (0, 0)))
    sgu_bx = jnp.repeat(jnp.transpose(sgu_b, (0, 2, 1)), 64, axis=2)
    gnw_t = jnp.tile(gla_norm_w, (1, HEADS))
    rc, rsa, rsb = _rope_tables(n_lat)

    xin = jnp.concatenate([ctx[0], x[0]], axis=0)
    row = lambda a: a[None, :]

    def pre_ins(l, xl):
        return [("x", "tile", True, xl), ("mod", "kind", True, modl[l]), ("n1w", "full", True, row(norm1_w[l])),
                ("w_in", "wfull", False, w_in_p[l]), ("sgu_nw", "full", True, row(sgu_norm_w[l])),
                ("sgu_nb", "full", True, row(sgu_norm_b[l])), ("sgu_w", "full", True, sgu_w[l]),
                ("sgu_bx", "full", True, sgu_bx[l]), ("wgf", "full", True, wgf_p[l]), ("bgf", "full", True, row(gla_bg_fwd[l])),
                ("wgb", "full", True, wgb_p[l]), ("bgb", "full", True, row(gla_bg_bwd[l])),
                ("qnw", "full", True, row(mla_q_norm_w[l])), ("w_uq", "full", True, w_uq_p[l]),
                ("kvnw", "full", True, row(mla_kv_norm_w[l])), ("w_ukv", "full", True, w_ukv_f[l]),
                ("rc", "tile", False, rc), ("rsa", "tile", False, rsa), ("rsb", "tile", False, rsb)]

    pre_outs = [("y_sgu", 256, F32), ("qg", 128, F32), ("kg", 128, F32), ("vg", 256, F32), ("lgf", 128, F32),
                ("lgb", 128, F32), ("gr", 256, F32), ("q_cat", 1024, BF), ("k_cat", 1024, BF), ("v", 512, BF)]

    def out_ins(l, xl, a):
        return [("x", "tile", True, xl), ("mod", "kind", True, modl[l]), ("y_sgu", "tile", True, a["y_sgu"]),
                ("o_f", "tile", True, a["o_f"]), ("o_b", "tile", True, a["o_b"]), ("gr", "tile", True, a["gr"]),
                ("y_mla", "tile", True, a["y_mla"]), ("gnw", "full", True, row(gnw_t[l])),
                ("w_out", "wfull", False, full["w_out"][l])]

    def ffn_ins(l, x1):
        return [("x1", "tile", True, x1), ("mod", "kind", True, modl[l]), ("n2w", "full", True, row(norm2_w[l])),
                ("w_ff1", "wcols", False, full["w_ff1"][l]), ("w_ff2", "wrows", False, full["w_ff2"][l])]

    saved, xl = [], xin
    for l in range(2):
        a = tile_forward(pre_tile, f"pre_fwd{l}", t_all, pre_ins(l, xl), pre_outs)
        a["o_f"], a["sf"], a["o_b"], a["sb"] = gla_forward(a["qg"], a["kg"], a["vg"], a["lgf"], a["lgb"], f"gla_fwd{l}")
        a["y_mla"], a["lse"] = mla_forward(a["q_cat"], a["k_cat"], a["v"], f"mla_fwd{l}")
        a["x"] = xl
        a["x1"] = tile_forward(attn_out_tile, f"out_fwd{l}", t_all, out_ins(l, xl, a), [("x1", D, F32)])["x1"]
        ff = tile_forward(ffn_tile, f"ffn_fwd{l}", t_all, ffn_ins(l, a["x1"]), [("x2", D, F32), ("f", D, F32)])
        xl, a["f"] = ff["x2"], ff["f"]
        saved.append(a)

    loss_blk, dx, d_fnw = final_loss(xl, loss_target[0], row(final_norm_w), "final_loss")
    loss = lax.psum(loss_blk[0, 0], AXES)

    G = {}
    dmods = []
    for l in (1, 0):
        a = saved[l]
        dx1, dmod3, dn2w, zpre, zf_t, h2_t, a_ff2 = ffn_backward(
            a["x1"], modl[l], row(norm2_w[l]), full["w_ff1"][l], full["w_ff2"][l], dx, a["f"], f"ffn_bwd{l}")
        gw_ff1 = wgrad(h2_t, zpre, f"wg_ff1_{l}")
        gw_ff2 = jnp.transpose(wgrad(zf_t, a_ff2, f"wg_ff2_{l}"))
        g2, e2 = tile_backward(attn_out_tile, f"out_bwd{l}", t_all, out_ins(l, a["x"], a), [("x1", dx1)],
                               [("zt", D)], [("a_out", D)])
        gw_out = wgrad(e2["a_out"], e2["zt"], f"wg_out_{l}")
        dl_rows, do_bf = mla_delta(g2["y_mla"], a["y_mla"], f"mla_delta{l}")
        dq_cat, dk_cat, dv = mla_backward(a["q_cat"], a["k_cat"], a["v"], a["lse"], dl_rows, do_bf, f"mla_bwd{l}")
        dqf, dkf, dvf, dgf, dqb, dkb, dvb, dgb = gla_backward(
            a["qg"], a["kg"], a["vg"], a["lgf"], a["lgb"], a["sf"], a["sb"], g2["o_f"], f"gla_bwd{l}")
        cots = [("y_sgu", g2["y_sgu"]), ("qg", [dqf, dqb]), ("kg", [dkf, dkb]), ("vg", [dvf, dvb]), ("lgf", dgf),
                ("lgb", dgb), ("gr", g2["gr"]), ("q_cat", dq_cat), ("k_cat", dk_cat), ("v", dv)]
        g1, e1 = tile_backward(pre_tile, f"pre_bwd{l}", t_all, pre_ins(l, a["x"]), cots, [("zp", P_COLS)], [("a_in", D)],
                               resid=("x", g2["x"]))
        gw_in = _unpad_cols(wgrad(e1["a_in"], e1["zp"], f"wg_in_{l}", bk2=P_COLS), W_IN_SEGS)
        dx = g1["x"]
        dmods.append(g1["mod"] + g2["mod"] + dmod3)
        G[l] = dict(w_in=gw_in, w_out=gw_out, w_ff1=gw_ff1, w_ff2=gw_ff2,
                    mla_w_uq=_unpad_cols(g1["w_uq"], W_UQ_SEGS), mla_w_ukv=g1["w_ukv"],
                    norm1_w=g1["n1w"][0], norm2_w=dn2w[0], sgu_norm_w=g1["sgu_nw"][0], sgu_norm_b=g1["sgu_nb"][0],
                    sgu_w=g1["sgu_w"], sgu_b=jnp.transpose(g1["sgu_bx"].reshape(128, HEADS, 64).sum(-1)),
                    gla_wg_fwd=g1["wgf"][:16], gla_bg_fwd=g1["bgf"][0], gla_wg_bwd=g1["wgb"][:16], gla_bg_bwd=g1["bgb"][0],
                    gla_norm_w=g2["gnw"][0].reshape(HEADS, 64).sum(0), mla_q_norm_w=g1["qnw"][0], mla_kv_norm_w=g1["kvnw"][0])
    dmods = dmods[::-1]
    grad_x = dx[TILE:][None]

    dmod_pack = jnp.stack([jnp.stack([dmods[l][1, :6].reshape(-1), dmods[l][0, :6].reshape(-1)]) for l in range(2)])
    dmod_all = all_gather([dmod_pack.reshape(4, 6 * D)], "ag_dmod")[0].reshape(N_DEV, 2, 2, 6 * D)
    dsl = lax.dynamic_slice_in_dim(dmod_all, me * mod_cols, mod_cols, axis=3)
    d_own = jnp.transpose(dsl[:, :, 0, :], (1, 0, 2))
    d_ctx = jnp.transpose(dsl[:, :, 1, :], (1, 0, 2))
    g_w_mod, ds_cc = mod_backward(crows, w_mod, d_own, d_ctx, "mod_bwd")
    g_c_ctx_part = silu_grad_scale(c_ctx[None, :], ds_cc, "silu_bwd")[0]
    g_b_mod_part = jnp.stack([dmods[l][1, :6].reshape(-1) + dmods[l][0, :6].reshape(-1) for l in range(2)])

    small_g = dict(c_ctx=g_c_ctx_part, b_mod=g_b_mod_part, final_norm_w=d_fnw[0])
    for k in SMALL_NAMES:
        if k not in small_g:
            small_g[k] = jnp.stack([G[0][k], G[1][k]])
    res = {}
    sparts = all_gather([v2(small_g[k]) for k in SMALL_NAMES], "ag_small")
    s_out = adamw_group([(sparts[j], v2(W[k]), v2(M[k]), v2(V[k])) for j, k in enumerate(SMALL_NAMES)], "adamw_small")
    for j, k in enumerate(SMALL_NAMES):
        res[k] = [o.reshape(W[k].shape) for o in s_out[j]]

    chunks = []
    for k in BIG_NAMES:
        ch = _full_to_chunks(jnp.stack([G[0][k], G[1][k]]), k).astype(BF)
        chunks.append(ch.reshape(N_DEV, -1, ch.shape[-1]))
    bparts = reduce_scatter(chunks, "rs_grads")
    for j, k in enumerate(BIG_NAMES):
        res[k] = [o.reshape(W[k].shape) for o in adamw(bparts[j], v2(W[k]), v2(M[k]), v2(V[k]), f"adamw_{k}")]
    res["w_mod"] = [o.reshape(w_mod.shape)
                    for o in adamw(v2(g_w_mod)[None], v2(w_mod), v2(m_w_mod), v2(v_w_mod), "adamw_w_mod")]
    outs = [loss, grad_x]
    for j in range(4):
        outs += [res[k][j] for k in WEIGHT_ORDER]
    return tuple(outs)
```

```python
import jax
import jax.numpy as jnp
from jax import lax
from jax.experimental import pallas as pl
from jax.experimental.pallas import tpu as pltpu

F32 = jnp.float32
BF = jnp.bfloat16

N_DEV = 8
AXES = ("x", "y", "c")
EPS = 1e-6
D = 1024
TILE = 256
GCH = 128
SGU_CHUNK = 128
HEADS = 4
ROPE_BASE = 10000.0
GRID_W = 64
GLA_TAU = 16.0
ATT_SCALE = (128 + 64) ** -0.5
ATT_SCALE_LOG2 = ATT_SCALE * 1.4426950408889634
LN2 = 0.6931471805599453
KV_CH = 512
KV_CH_FWD = 2048
QT_FWD = 256
Q_CH_BWD = 1024
KT_BWD = 768
MLA_UNROLL = 2
D_FF = 4096
FF_CH = 1024

ADAM_LR = 0.001
ADAM_B1 = 0.9
ADAM_B2 = 0.999
ADAM_EPS = 1e-08
ADAM_WD = 0.01
ADAM_STEP = 10

VMEM_LIMIT_MB = 56
ADAMW_BLOCK_ELEMS = 256 * 1024

W_IN_SEGS = ((0, 0, 128), (128, 128, 256), (384, 384, 16), (512, 400, 16), (640, 416, 256), (896, 672, 64),
             (1024, 736, 256), (1280, 992, 256), (1536, 1248, 128), (1664, 1376, 256), (1920, 1632, 256))
P_COLS = 2176
O_GK, O_GV, O_GGF, O_GGB, O_CKV, O_KR, O_SU, O_SV, O_GQ, O_GR, O_DQ = (s[0] for s in W_IN_SEGS)
W_UQ_SEGS = tuple((h * 256, h * 192, 192) for h in range(HEADS))

SMALL_NAMES = ("c_ctx", "b_mod", "norm1_w", "sgu_norm_w", "sgu_norm_b", "sgu_w", "sgu_b", "gla_wg_fwd", "gla_bg_fwd",
               "gla_wg_bwd", "gla_bg_bwd", "gla_norm_w", "mla_q_norm_w", "mla_kv_norm_w", "norm2_w", "final_norm_w")
BIG_NAMES = ("w_in", "w_out", "mla_w_uq", "mla_w_ukv", "w_ff1", "w_ff2")
WEIGHT_ORDER = ("c_ctx", "w_mod", "b_mod", "norm1_w", "w_in", "w_out", "sgu_norm_w", "sgu_norm_b", "sgu_w", "sgu_b",
                "gla_wg_fwd", "gla_bg_fwd", "gla_wg_bwd", "gla_bg_bwd", "gla_norm_w", "mla_q_norm_w", "mla_w_uq",
                "mla_kv_norm_w", "mla_w_ukv", "norm2_w", "w_ff1", "w_ff2", "final_norm_w")


def _cparams(n_axes):
    return pltpu.CompilerParams(dimension_semantics=("arbitrary",) * n_axes,
                                vmem_limit_bytes=VMEM_LIMIT_MB * 1024 * 1024)


def _dot(a, b):
    return jnp.dot(a, b, preferred_element_type=F32)


def _dot_nt(a, b):
    return lax.dot_general(a, b, (((1,), (1,)), ((), ())), preferred_element_type=F32)


def _dot_tn(a, b):
    return lax.dot_general(a, b, (((0,), (0,)), ((), ())), preferred_element_type=F32)


def _nn(a, b):
    return _dot(a.astype(BF), b.astype(BF))


def _nt(a, b):
    return _dot_nt(a.astype(BF), b.astype(BF))


def _tn(a, b):
    return _dot_tn(a.astype(BF), b.astype(BF))


nn_d = jax.custom_vjp(_nn)
nt_d = jax.custom_vjp(_nt)
tn_d = jax.custom_vjp(_tn)
nn_d.defvjp(lambda a, b: (_nn(a, b), (a, b)), lambda r, dy: (_nt(dy, r[1]), _tn(r[0], dy)))
nt_d.defvjp(lambda a, b: (_nt(a, b), (a, b)), lambda r, dy: (_nn(dy, r[1]), _tn(dy, r[0])))
tn_d.defvjp(lambda a, b: (_tn(a, b), (a, b)), lambda r, dy: (_nt(r[1], dy), _nn(r[0], dy)))


def nn_const(w_bf, diff):
    def raw(a):
        return _dot(a.astype(BF), w_bf)

    if not diff:
        return raw
    f = jax.custom_vjp(raw)
    f.defvjp(lambda a: (raw(a), None), lambda _, dy: (_dot_nt(dy.astype(BF), w_bf),))
    return f


def _split3(g):
    hi = g.astype(BF)
    r = g - hi.astype(F32)
    mid = r.astype(BF)
    lo = (r - mid.astype(F32)).astype(BF)
    return hi, mid, lo


def make_cum(tri_bf, tri_t_bf, diff):
    def raw(g, t):
        hi, mid, lo = _split3(g)
        return _dot(t, hi) + _dot(t, mid) + _dot(t, lo)

    def fwd(g):
        return raw(g, tri_bf)

    if not diff:
        return fwd
    cum = jax.custom_vjp(fwd)
    cum.defvjp(lambda g: (fwd(g), None), lambda _, db: (raw(db, tri_t_bf),))
    return cum


def _roll_lanes(x, shift):
    return pltpu.roll(x, shift, 1)


def make_rope(c, sa, sb, diff):
    def raw(x):
        return x * c + _roll_lanes(x, 112) * sa + _roll_lanes(x, 16) * sb

    if not diff:
        return raw
    f = jax.custom_vjp(raw)
    f.defvjp(lambda x: (raw(x), None),
             lambda _, dy: (dy * c + _roll_lanes(dy * sa, 16) + _roll_lanes(dy * sb, 112),))
    return f


def _ops(diff):
    return (nn_d, nt_d, tn_d) if diff else (_nn, _nt, _tn)


def _rms(x, w):
    return x * lax.rsqrt(jnp.mean(x * x, axis=-1, keepdims=True) + EPS) * w


def _gelu(x):
    return 0.5 * x * (1.0 + jnp.tanh(0.7978845608028654 * (x + 0.044715 * (x * x * x))))


def _silu(x):
    return x * jax.nn.sigmoid(x)


def _log_sigmoid(z):
    return jnp.minimum(z, 0.0) - jnp.log(1.0 + jnp.exp(-jnp.abs(z)))


def _lane_group_mask(width, group, h):
    lane = lax.broadcasted_iota(jnp.int32, (1, width), 1)
    return ((lane >= h * group) & (lane < (h + 1) * group)).astype(F32)


def pre_tile(d, c, z):
    nn, _, _ = _ops(z is not None)
    rope = make_rope(c["rc"], c["rsa"], c["rsb"], z is not None)
    mod = d["mod"]
    h = _rms(d["x"], d["n1w"]) * (1.0 + mod[1:2]) + mod[0:1]
    p = nn_const(c["w_in"], z is not None)(h)
    if z is not None:
        p = p + z["zp"]
    gk, gv = p[:, O_GK:O_GK + 128], p[:, O_GV:O_GV + 256]
    ggf, ggb = p[:, O_GGF:O_GGF + 128], p[:, O_GGB:O_GGB + 128]
    ckv, kr = p[:, O_CKV:O_CKV + 256], p[:, O_KR:O_KR + 128]
    su, sv = p[:, O_SU:O_SU + 256], p[:, O_SV:O_SV + 256]
    gq, gr, dq = p[:, O_GQ:O_GQ + 128], p[:, O_GR:O_GR + 256], p[:, O_DQ:O_DQ + 256]

    u = _gelu(su)
    gv_ = _gelu(sv)
    mu = jnp.mean(gv_, axis=-1, keepdims=True)
    cen = gv_ - mu
    vn = cen * lax.rsqrt(jnp.mean(cen * cen, axis=-1, keepdims=True) + EPS) * d["sgu_nw"] + d["sgu_nb"]
    hm = [_lane_group_mask(256, 64, hh) for hh in range(HEADS)]
    rows = []
    for ci in range(vn.shape[0] // SGU_CHUNK):
        vc = vn[ci * SGU_CHUNK:(ci + 1) * SGU_CHUNK]
        s = d["sgu_bx"]
        for hh in range(HEADS):
            s = s + hm[hh] * nn(d["sgu_w"][hh], vc)
        rows.append(s)
    y_sgu = u * jnp.concatenate(rows, axis=0)

    qg = gq * (32 ** -0.5)
    lgf = _log_sigmoid(nn(ggf, d["wgf"]) + d["bgf"]) * (1.0 / GLA_TAU)
    lgb = _log_sigmoid(nn(ggb, d["wgb"]) + d["bgb"]) * (1.0 / GLA_TAU)

    kv = nn(_rms(ckv, d["kvnw"]), d["w_ukv"])
    kr_r = rope(kr)
    q = nn(_rms(dq, d["qnw"]), d["w_uq"])
    qs, ks, vs = [], [], []
    for hh in range(HEADS):
        qs += [q[:, hh * 256:hh * 256 + 128], rope(q[:, hh * 256 + 128:(hh + 1) * 256])]
        ks += [kv[:, hh * 256:hh * 256 + 128], kr_r]
        vs += [kv[:, hh * 256 + 128:(hh + 1) * 256]]
    outs = dict(y_sgu=y_sgu, qg=qg, kg=gk, vg=gv, lgf=lgf, lgb=lgb, gr=gr,
                q_cat=jnp.concatenate(qs, axis=-1) * ATT_SCALE_LOG2, k_cat=jnp.concatenate(ks, axis=-1), v=jnp.concatenate(vs, axis=-1))
    return outs, dict(a_in=h)


def attn_out_tile(d, c, z):
    mod = d["mod"]
    o = d["o_f"] + d["o_b"]
    ms = jnp.zeros_like(o)
    for hh in range(HEADS):
        m_h = _lane_group_mask(256, 64, hh)
        ms = ms + m_h * (jnp.sum(o * o * m_h, axis=-1, keepdims=True) * (1.0 / 64))
    yg = o * lax.rsqrt(ms + EPS) * d["gnw"] * _silu(d["gr"])
    y = jnp.concatenate([d["y_sgu"], yg, d["y_mla"]], axis=-1)
    t = nn_const(c["w_out"], z is not None)(y)
    if z is not None:
        t = t + z["zt"]
    return dict(x1=d["x"] + mod[2:3] * t), dict(a_out=y)


def ffn_tile(d, c, z):
    mod = d["mod"]
    h2 = _rms(d["x1"], d["n2w"]) * (1.0 + mod[4:5]) + mod[3:4]
    f = None
    a2s = []
    for j in range(D_FF // FF_CH):
        pre = nn_const(c["w_ff1"][j], z is not None)(h2)
        if z is not None:
            pre = pre + z["zpre"][:, j * FF_CH:(j + 1) * FF_CH]
        a = jnp.maximum(pre, 0.0)
        a2 = a * a
        a2s.append(a2)
        fj = nn_const(c["w_ff2"][j], z is not None)(a2)
        f = fj if f is None else f + fj
    if z is not None:
        f = f + z["zf"]
    return dict(x2=d["x1"] + mod[5:6] * f, f=f), dict(a_ff1=h2, a_ff2=jnp.concatenate(a2s, axis=-1))


def _in_spec(kind, arr, tile):
    if kind == "tile":
        return pl.BlockSpec((tile, arr.shape[1]), lambda i: (i, 0))
    if kind == "kind":
        return pl.BlockSpec((1,) + arr.shape[1:], lambda i: (jnp.where(i < TILE // tile, 0, 1), 0, 0))
    nd = arr.ndim
    if kind in ("wfull", "wcols", "wrows"):
        return pl.BlockSpec(arr.shape, lambda i: (0,) * nd, pipeline_mode=pl.Buffered(1))
    return pl.BlockSpec(arr.shape, lambda i: (0,) * nd)


def _load(kind, ref):
    if kind == "kind":
        return ref[0]
    if kind == "wcols":
        return [ref[:, j * FF_CH:(j + 1) * FF_CH] for j in range(ref.shape[1] // FF_CH)]
    if kind == "wrows":
        return [ref[j * FF_CH:(j + 1) * FF_CH, :] for j in range(ref.shape[0] // FF_CH)]
    return ref[...]


def tile_forward(fn, name, t_all, ins, out_defs, tile=TILE):
    keys = [k for k, _, _, _ in ins]
    kinds = [kd for _, kd, _, _ in ins]
    diffs = [df for _, _, df, _ in ins]
    arrs = [a for _, _, _, a in ins]
    n_in = len(ins)

    def body(*refs):
        vals = [_load(kinds[j], refs[j]) for j in range(n_in)]
        d = {keys[j]: vals[j] for j in range(n_in) if diffs[j]}
        c = {keys[j]: vals[j] for j in range(n_in) if not diffs[j]}
        outs, _ = fn(d, c, None)
        for j, (k, _, dt) in enumerate(out_defs):
            refs[n_in + j][...] = outs[k].astype(dt)

    res = pl.pallas_call(
        body, name=name, grid=(t_all // tile,),
        in_specs=[_in_spec(kinds[j], arrs[j], tile) for j in range(n_in)],
        out_specs=[pl.BlockSpec((tile, w), lambda i: (i, 0)) for _, w, _ in out_defs],
        out_shape=[jax.ShapeDtypeStruct((t_all, w), dt) for _, w, dt in out_defs],
        compiler_params=_cparams(1),
    )(*arrs)
    return {k: r for (k, _, _), r in zip(out_defs, res)}


def tile_backward(fn, name, t_all, ins, cots, z_defs, aux_defs, tile=TILE, resid=None):
    keys = [k for k, _, _, _ in ins]
    kinds = [kd for _, kd, _, _ in ins]
    diffs = [df for _, _, df, _ in ins]
    arrs = [a for _, _, _, a in ins]
    cot_keys, cot_arrs = [], []
    for k, a in cots:
        for one in (a if isinstance(a, (list, tuple)) else [a]):
            cot_keys.append(k)
            cot_arrs.append(one)
    if resid is not None:
        cot_keys.append("resid:" + resid[0])
        cot_arrs.append(resid[1])
    n_in, n_cot = len(ins), len(cot_arrs)
    dkeys = [j for j in range(n_in) if diffs[j]]
    ctx_tiles = TILE // tile

    def body(*refs):
        i = pl.program_id(0)
        vals = [_load(kinds[j], refs[j]) for j in range(n_in)]
        d = {keys[j]: vals[j] for j in dkeys}
        c = {keys[j]: vals[j] for j in range(n_in) if not diffs[j]}
        zs = {k: jnp.zeros((tile, w), F32) for k, w in z_defs}
        outs, vjp_fn, aux = jax.vjp(lambda dd, zz: fn(dd, c, zz), d, zs, has_aux=True)
        ct = {}
        for j, k in enumerate(cot_keys):
            ct[k] = refs[n_in + j][...] + ct[k] if k in ct else refs[n_in + j][...]
        dd, dz = vjp_fn({k: ct[k].astype(outs[k].dtype) for k in outs})
        base = n_in + n_cot
        for n, j in enumerate(dkeys):
            ref, g = refs[base + n], dd[keys[j]]
            if kinds[j] == "tile":
                ref[...] = g + ct["resid:" + keys[j]] if "resid:" + keys[j] in ct else g
            else:
                first = ((i == 0) | (i == ctx_tiles)) if kinds[j] == "kind" else (i == 0)
                gv = g[None] if kinds[j] == "kind" else g

                @pl.when(first)
                def _(ref=ref, gv=gv):
                    ref[...] = gv

                @pl.when(jnp.logical_not(first))
                def _(ref=ref, gv=gv):
                    ref[...] += gv
        base += len(dkeys)
        for n, (k, _) in enumerate(z_defs):
            refs[base + n][...] = dz[k].astype(BF)
        base += len(z_defs)
        for n, (k, _) in enumerate(aux_defs):
            refs[base + n][...] = aux[k].T.astype(BF)

    out_specs, out_shape = [], []
    for j in dkeys:
        out_specs.append(_in_spec(kinds[j], arrs[j], tile))
        out_shape.append(jax.ShapeDtypeStruct(arrs[j].shape, F32))
    for _, w in z_defs:
        out_specs.append(pl.BlockSpec((tile, w), lambda i: (i, 0)))
        out_shape.append(jax.ShapeDtypeStruct((t_all, w), BF))
    for _, w in aux_defs:
        out_specs.append(pl.BlockSpec((w, tile), lambda i: (0, i)))
        out_shape.append(jax.ShapeDtypeStruct((w, t_all), BF))
    res = pl.pallas_call(
        body, name=name, grid=(t_all // tile,),
        in_specs=[_in_spec(kinds[j], arrs[j], tile) for j in range(n_in)]
        + [pl.BlockSpec((tile, a.shape[1]), lambda i: (i, 0)) for a in cot_arrs],
        out_specs=out_specs, out_shape=out_shape, compiler_params=_cparams(1),
    )(*arrs, *cot_arrs)
    grads = {keys[j]: res[n] for n, j in enumerate(dkeys)}
    extra = {k: res[len(dkeys) + n] for n, (k, _) in enumerate(list(z_defs) + list(aux_defs))}
    return grads, extra


def ffn_backward(x1, modl, n2w, w1, w2, dx2, f, name):
    t_all = x1.shape[0]
    n_ch = D_FF // FF_CH

    def head(x, mod, nw):
        return _rms(x, nw) * (1.0 + mod[4:5]) + mod[3:4]

    def body(x_ref, mod_ref, nw_ref, w1_ref, w2_ref, dx2_ref, f_ref, dx1_ref, dmod_ref, dnw_ref, zpre_ref, zf_ref, a1_ref,
             a2_ref):
        i = pl.program_id(0)
        mod = mod_ref[0]
        dx2 = dx2_ref[...]
        h2, vjp_head = jax.vjp(head, x_ref[...], mod, nw_ref[...])
        h2b = h2.astype(BF)
        dfb = (dx2 * mod[5:6]).astype(BF)
        f = f_ref[...]
        dh2 = jnp.zeros((TILE, D), F32)
        for j in range(n_ch):
            cs = slice(j * FF_CH, (j + 1) * FF_CH)
            a = jnp.maximum(_dot(h2b, w1_ref[:, cs]), 0.0)
            a2b = (a * a).astype(BF)
            dpre = (_dot_nt(dfb, w2_ref[cs, :]) * (2.0 * a)).astype(BF)
            dh2 = dh2 + _dot_nt(dpre, w1_ref[:, cs])
            zpre_ref[:, cs] = dpre
            a2_ref[:, cs] = a2b
        zf_ref[...] = (dx2 * mod[5:6]).T.astype(BF)
        a1_ref[...] = h2.T.astype(BF)
        dx1, dmod, dnw = vjp_head(dh2)
        dx1_ref[...] = dx2 + dx1
        row = lax.broadcasted_iota(jnp.int32, (8, D), 0)
        dmod = dmod + jnp.where(row == 5, jnp.sum(dx2 * f, axis=0, keepdims=True), 0.0)
        first_kind = (i == 0) | (i == 1)

        @pl.when(first_kind)
        def _():
            dmod_ref[0] = dmod

        @pl.when(jnp.logical_not(first_kind))
        def _():
            dmod_ref[0] += dmod

        @pl.when(i == 0)
        def _():
            dnw_ref[...] = dnw

        @pl.when(i > 0)
        def _():
            dnw_ref[...] += dnw

    tspec = lambda w: pl.BlockSpec((TILE, w), lambda i: (i, 0))
    once = lambda shp: pl.BlockSpec(shp, lambda i: (0, 0), pipeline_mode=pl.Buffered(1))
    kind = pl.BlockSpec((1, 8, D), lambda i: (jnp.minimum(i, 1), 0, 0))
    tr = pl.BlockSpec((D, TILE), lambda i: (0, i))
    return pl.pallas_call(
        body, name=name, grid=(t_all // TILE,),
        in_specs=[tspec(D), kind, pl.BlockSpec((1, D), lambda i: (0, 0)), once((D, D_FF)), once((D_FF, D)), tspec(D),
                  tspec(D)],
        out_specs=[tspec(D), kind, pl.BlockSpec((1, D), lambda i: (0, 0)), tspec(D_FF), tr, tr, tspec(D_FF)],
        out_shape=[jax.ShapeDtypeStruct((t_all, D), F32), jax.ShapeDtypeStruct((2, 8, D), F32),
                   jax.ShapeDtypeStruct((1, D), F32), jax.ShapeDtypeStruct((t_all, D_FF), BF),
                   jax.ShapeDtypeStruct((D, t_all), BF), jax.ShapeDtypeStruct((D, t_all), BF),
                   jax.ShapeDtypeStruct((t_all, D_FF), BF)],
        compiler_params=_cparams(1),
    )(x1, modl, n2w, w1, w2, dx2, f)


WG_TOK = 768


def wgrad(at, b, name, bk2=1024):
    k1, t = at.shape
    k2 = b.shape[1]
    bk2 = min(bk2, k2)
    tt = WG_TOK if t % WG_TOK == 0 else TILE
    nt_ = t // tt

    def body(a_ref, b_ref, o_ref, acc):
        s = pl.program_id(1)

        @pl.when(s == 0)
        def _():
            acc[...] = jnp.zeros_like(acc)

        acc[...] += _dot(a_ref[...], b_ref[...])

        @pl.when(s == nt_ - 1)
        def _():
            o_ref[...] = acc[...]

    return pl.pallas_call(
        body, name=name, grid=(k2 // bk2, nt_),
        in_specs=[pl.BlockSpec((k1, tt), lambda j, s: (0, s)), pl.BlockSpec((tt, bk2), lambda j, s: (s, j))],
        out_specs=pl.BlockSpec((k1, bk2), lambda j, s: (0, j)),
        out_shape=jax.ShapeDtypeStruct((k1, k2), F32),
        scratch_shapes=[pltpu.VMEM((k1, bk2), F32)],
        compiler_params=_cparams(2),
    )(at, b)


def _gla_consts(reverse, diff):
    r = lax.broadcasted_iota(jnp.int32, (GCH, GCH), 0)
    cc = lax.broadcasted_iota(jnp.int32, (GCH, GCH), 1)
    low = (r >= cc)
    tri = (jnp.logical_not(low) | (r == cc)) if reverse else low
    tri_f = tri.astype(F32)
    tri_t = (cc >= r) if not reverse else (cc <= r)
    hmk = [_lane_group_mask(128, 32, h) for h in range(HEADS)]
    hmv = [_lane_group_mask(256, 64, h) for h in range(HEADS)]
    e = lax.broadcasted_iota(jnp.int32, (256, 128), 0) // 64
    dk = lax.broadcasted_iota(jnp.int32, (256, 128), 1) // 32
    return dict(cum=make_cum(tri_f.astype(BF), tri_t.astype(F32).astype(BF), diff), ops=_ops(diff), reverse=reverse,
                tri4=jnp.concatenate([tri_f] * HEADS, axis=0), hmk=hmk, hmv=hmv, bd=(e == dk).astype(F32))


def gla_chunk(st, q, k, v, g, cs):
    nn, nt, tn = cs["ops"]
    b = cs["cum"](g)
    bl = jnp.sum(g, axis=0, keepdims=True)
    b_ref = jnp.sum(g[GCH // 2:] if cs["reverse"] else g[:GCH // 2], axis=0, keepdims=True)
    qe = q * jnp.exp(b)
    qs = q * jnp.exp(b - b_ref)
    ks = k * jnp.exp(b_ref - b)
    qstack = jnp.concatenate([qs * cs["hmk"][h] for h in range(HEADS)], axis=0)
    att = nt(qstack, ks) * cs["tri4"]
    ofull = nn(att, v)
    o = nt(qe, st)
    for h in range(HEADS):
        o = o + ofull[h * GCH:(h + 1) * GCH] * cs["hmv"][h]
    kd = k * jnp.exp(bl - b)
    st_new = st * jnp.exp(bl) + tn(v, kd) * cs["bd"]
    return st_new, o


def _gla_chunk_index(s, n_ch, reverse):
    ctx_ch = TILE // GCH
    if not reverse:
        return s
    return jnp.where(s < ctx_ch, ctx_ch - 1 - s, n_ch - 1 + ctx_ch - s)


def gla_forward(q, k, v, gf, gb, name):
    t = q.shape[0]
    n_ch = t // GCH

    def body(*refs):
        s = pl.program_id(0)
        for dr, reverse in enumerate((False, True)):
            q_ref, k_ref, v_ref, g_ref = refs[4 * dr:4 * dr + 4]
            o_ref, sst_ref = refs[8 + 2 * dr:8 + 2 * dr + 2]
            st = refs[12 + dr]

            @pl.when(s == 0)
            def _(st=st):
                st[...] = jnp.zeros_like(st)

            cur = st[...]
            sst_ref[0] = cur
            st_new, o = gla_chunk(cur, q_ref[...], k_ref[...], v_ref[...], g_ref[...], _gla_consts(reverse, False))
            o_ref[...] = o
            st[...] = st_new

    in_specs, out_specs, out_shape = [], [], []
    for reverse in (False, True):
        im = lambda s, reverse=reverse: (_gla_chunk_index(s, n_ch, reverse), 0)
        im3 = lambda s, reverse=reverse: (_gla_chunk_index(s, n_ch, reverse), 0, 0)
        in_specs += [pl.BlockSpec((GCH, 128), im), pl.BlockSpec((GCH, 128), im), pl.BlockSpec((GCH, 256), im),
                     pl.BlockSpec((GCH, 128), im)]
        out_specs += [pl.BlockSpec((GCH, 256), im), pl.BlockSpec((1, 256, 128), im3)]
        out_shape += [jax.ShapeDtypeStruct((t, 256), F32), jax.ShapeDtypeStruct((n_ch, 256, 128), F32)]
    return pl.pallas_call(
        body, name=name, grid=(n_ch,), in_specs=in_specs, out_specs=out_specs, out_shape=out_shape,
        scratch_shapes=[pltpu.VMEM((256, 128), F32), pltpu.VMEM((256, 128), F32)],
        compiler_params=_cparams(1),
    )(q, k, v, gf, q, k, v, gb)


def gla_backward(q, k, v, gf, gb, sst_f, sst_b, do, name):
    t = q.shape[0]
    n_ch = t // GCH

    def body(*refs):
        r = pl.program_id(0)
        for dr, reverse in enumerate((False, True)):
            q_ref, k_ref, v_ref, g_ref, sst_ref, do_ref = refs[6 * dr:6 * dr + 6]
            outs = refs[12 + 4 * dr:12 + 4 * dr + 4]
            dst = refs[20 + dr]

            @pl.when(r == 0)
            def _(dst=dst):
                dst[...] = jnp.zeros_like(dst)

            cs = _gla_consts(reverse, True)
            _, vjp_fn = jax.vjp(lambda a, b, c_, d_, e_, cs=cs: gla_chunk(a, b, c_, d_, e_, cs),
                                sst_ref[0], q_ref[...], k_ref[...], v_ref[...], g_ref[...])
            grads = vjp_fn((dst[...], do_ref[...]))
            for o_ref, gval in zip(outs, grads[1:]):
                o_ref[...] = gval
            dst[...] = grads[0]

    in_specs, out_specs, out_shape = [], [], []
    for reverse in (False, True):
        im = lambda r, reverse=reverse: (_gla_chunk_index(n_ch - 1 - r, n_ch, reverse), 0)
        im3 = lambda r, reverse=reverse: (_gla_chunk_index(n_ch - 1 - r, n_ch, reverse), 0, 0)
        in_specs += [pl.BlockSpec((GCH, 128), im), pl.BlockSpec((GCH, 128), im), pl.BlockSpec((GCH, 256), im),
                     pl.BlockSpec((GCH, 128), im), pl.BlockSpec((1, 256, 128), im3), pl.BlockSpec((GCH, 256), im)]
        out_specs += [pl.BlockSpec((GCH, 128), im), pl.BlockSpec((GCH, 128), im), pl.BlockSpec((GCH, 256), im),
                      pl.BlockSpec((GCH, 128), im)]
        out_shape += [jax.ShapeDtypeStruct((t, 128), F32), jax.ShapeDtypeStruct((t, 128), F32),
                      jax.ShapeDtypeStruct((t, 256), F32), jax.ShapeDtypeStruct((t, 128), F32)]
    return pl.pallas_call(
        body, name=name, grid=(n_ch,), in_specs=in_specs, out_specs=out_specs, out_shape=out_shape,
        scratch_shapes=[pltpu.VMEM((256, 128), F32), pltpu.VMEM((256, 128), F32)],
        compiler_params=_cparams(1),
    )(q, k, v, gf, sst_f, do, q, k, v, gb, sst_b, do)


def _resident(hbm_ref, vmem_ref, sem):
    cp = pltpu.make_async_copy(hbm_ref, vmem_ref, sem)
    cp.start()
    cp.wait()


def mla_forward(q_cat, k_cat, v, name):
    t = q_cat.shape[0]
    qt = QT_FWD if t % QT_FWD == 0 else TILE
    n_t = t // qt

    ch = KV_CH_FWD if (t - TILE) % KV_CH_FWD == 0 else KV_CH
    n_main = (t - TILE) // ch

    def body(q_ref, k_hbm, v_hbm, o_ref, lse_ref, k_s, v_s, m_s, l_s, acc_s, sem):
        i = pl.program_id(0)

        @pl.when(i == 0)
        def _():
            _resident(k_hbm, k_s, sem.at[0])
            _resident(v_hbm, v_s, sem.at[1])

        m_s[...] = jnp.full(m_s.shape, -1e30, F32)
        l_s[...] = jnp.zeros_like(l_s)
        acc_s[...] = jnp.zeros_like(acc_s)

        def chunk(r0, size, hide_ctx_rows=False):
            for h in range(HEADS):
                kh = k_s[pl.ds(r0, size), h * 256:(h + 1) * 256]
                vh = v_s[pl.ds(r0, size), h * 128:(h + 1) * 128]
                s = _dot_nt(q_ref[:, h * 256:(h + 1) * 256], kh)
                if hide_ctx_rows:
                    s = jnp.where(lax.broadcasted_iota(jnp.int32, (qt, 1), 0) < TILE, -1e30, s)
                m_prev = m_s[h]
                m_next = jnp.maximum(m_prev, jnp.max(s, axis=-1, keepdims=True))
                p = jnp.exp2(s - jnp.tile(m_next, (1, size // 128)))
                alpha = jnp.exp2(m_prev - m_next)
                l_s[h] = alpha * l_s[h] + jnp.sum(p, axis=-1, keepdims=True)
                acc_s[h] = alpha * acc_s[h] + _dot(p.astype(BF), vh)
                m_s[h] = m_next

        chunk(0, TILE)

        def main_loop(hide, size, unroll):
            def step(c, carry):
                chunk(pl.multiple_of(TILE + c * size, TILE), size, hide)
                return carry

            lax.fori_loop(0, (t - TILE) // size, step, 0, unroll=unroll)

        if qt > TILE:
            pl.when(i == 0)(lambda: main_loop(True, KV_CH, 1))
        pl.when(i >= 1)(lambda: main_loop(False, ch, MLA_UNROLL))

        lane = lax.broadcasted_iota(jnp.int32, (qt, 128), 1)
        cols = jnp.zeros((qt, 128), F32)
        for h in range(HEADS):
            o_ref[:, h * 128:(h + 1) * 128] = acc_s[h] / l_s[h]
            cols = jnp.where(lane == h, m_s[h] + jnp.log2(l_s[h]), cols)
        lse_ref[...] = cols.T[0:8, :]

    return pl.pallas_call(
        body, name=name, grid=(n_t,),
        in_specs=[pl.BlockSpec((qt, 1024), lambda i: (i, 0)), pl.BlockSpec(memory_space=pl.ANY),
                  pl.BlockSpec(memory_space=pl.ANY)],
        out_specs=[pl.BlockSpec((qt, 512), lambda i: (i, 0)), pl.BlockSpec((8, qt), lambda i: (0, i))],
        out_shape=[jax.ShapeDtypeStruct((t, 512), F32), jax.ShapeDtypeStruct((8, t), F32)],
        scratch_shapes=[pltpu.VMEM((t, 1024), BF), pltpu.VMEM((t, 512), BF), pltpu.VMEM((HEADS, qt, 128), F32),
                        pltpu.VMEM((HEADS, qt, 128), F32), pltpu.VMEM((HEADS, qt, 128), F32),
                        pltpu.SemaphoreType.DMA((2,))],
        compiler_params=_cparams(1),
    )(q_cat, k_cat, v)


def mla_delta(do, o, name):
    t = do.shape[0]

    def body(do_ref, o_ref, dl_ref, dob_ref):
        d = do_ref[...]
        prod = d * o_ref[...]
        rows = [jnp.sum(prod[:, h * 128:(h + 1) * 128], axis=-1, keepdims=True) for h in range(HEADS)]
        cols = jnp.concatenate(rows + [jnp.zeros((TILE, 128 - HEADS), F32)], axis=-1)
        dl_ref[...] = cols.T[0:8, :]
        dob_ref[...] = d.astype(BF)

    return pl.pallas_call(
        body, name=name, grid=(t // TILE,),
        in_specs=[pl.BlockSpec((TILE, 512), lambda i: (i, 0)), pl.BlockSpec((TILE, 512), lambda i: (i, 0))],
        out_specs=[pl.BlockSpec((8, TILE), lambda i: (0, i)), pl.BlockSpec((TILE, 512), lambda i: (i, 0))],
        out_shape=[jax.ShapeDtypeStruct((8, t), F32), jax.ShapeDtypeStruct((t, 512), BF)],
        compiler_params=_cparams(1),
    )(do, o)


def mla_backward(q_cat, k_cat, v, lse_rows, dl_rows, do_bf, name):
    t = q_cat.shape[0]
    kt = KT_BWD if t % KT_BWD == 0 else TILE
    n_t = t // kt
    ch = Q_CH_BWD if (t - TILE) % Q_CH_BWD == 0 else KV_CH
    n_main = (t - TILE) // ch

    def body(q_hbm, do_hbm, k_ref, v_ref, lse_ref, dl_ref, dq_ref, dk_ref, dv_ref, q_s, do_s, dk_s, dv_s, sem):
        h, j = pl.program_id(0), pl.program_id(1)

        @pl.when(j == 0)
        def _():
            _resident(q_hbm.at[:, pl.ds(pl.multiple_of(h * 256, 256), 256)], q_s, sem.at[0])
            _resident(do_hbm.at[:, pl.ds(pl.multiple_of(h * 128, 128), 128)], do_s, sem.at[1])
            dq_ref[...] = jnp.zeros_like(dq_ref)

        dk_s[...] = jnp.zeros_like(dk_s)
        dv_s[...] = jnp.zeros_like(dv_s)
        kh = k_ref[...]
        vh = v_ref[...]

        def chunk(r0, size, ctx_only=False):
            qh = q_s[pl.ds(r0, size), :]
            doh = do_s[pl.ds(r0, size), :]
            pt = jnp.exp2(_dot_nt(kh, qh) - lse_ref[pl.ds(h, 1), pl.ds(r0, size)])
            if ctx_only and kt > TILE:
                pt = jnp.where(lax.broadcasted_iota(jnp.int32, (kt, 1), 0) < TILE, pt, 0.0)
            dst = (pt * (_dot_nt(vh, doh) - dl_ref[pl.ds(h, 1), pl.ds(r0, size)])).astype(BF)
            dv_s[...] += _dot(pt.astype(BF), doh)
            dk_s[...] += _dot(dst, qh)
            dq_ref[pl.ds(r0, size), :] += _dot_tn(dst, kh)

        @pl.when(j == 0)
        def _():
            chunk(0, TILE, ctx_only=True)

        def step(c, carry):
            chunk(pl.multiple_of(TILE + c * ch, TILE), ch)
            return carry

        lax.fori_loop(0, n_main, step, 0, unroll=MLA_UNROLL)
        dk_ref[...] = dk_s[...] * LN2
        dv_ref[...] = dv_s[...]

        @pl.when(j == n_t - 1)
        def _():
            dq_ref[...] = dq_ref[...] * LN2

    rows = pl.BlockSpec((8, t), lambda h, j: (0, 0))
    hbm = pl.BlockSpec(memory_space=pl.ANY)
    return pl.pallas_call(
        body, name=name, grid=(HEADS, n_t),
        in_specs=[hbm, hbm, pl.BlockSpec((kt, 256), lambda h, j: (j, h)), pl.BlockSpec((kt, 128), lambda h, j: (j, h)),
                  rows, rows],
        out_specs=[pl.BlockSpec((t, 256), lambda h, j: (0, h)), pl.BlockSpec((kt, 256), lambda h, j: (j, h)),
                   pl.BlockSpec((kt, 128), lambda h, j: (j, h))],
        out_shape=[jax.ShapeDtypeStruct((t, 1024), F32), jax.ShapeDtypeStruct((t, 1024), F32),
                   jax.ShapeDtypeStruct((t, 512), F32)],
        scratch_shapes=[pltpu.VMEM((t, 256), BF), pltpu.VMEM((t, 128), BF), pltpu.VMEM((kt, 256), F32),
                        pltpu.VMEM((kt, 128), F32), pltpu.SemaphoreType.DMA((2,))],
        compiler_params=_cparams(2),
    )(q_cat, do_bf, k_cat, v, lse_rows, dl_rows)


def final_loss(xf, target, fnw, name):
    t = xf.shape[0]
    n_t = t // TILE

    def body(x_ref, t_ref, w_ref, loss_ref, dx_ref, dw_ref):
        i = pl.program_id(0)

        @pl.when(i == 0)
        def _():
            loss_ref[...] = jnp.zeros_like(loss_ref)
            dw_ref[...] = jnp.zeros_like(dw_ref)
            dx_ref[...] = jnp.zeros_like(dx_ref)

        @pl.when(i >= 1)
        def _():
            y, vjp_fn = jax.vjp(_rms, x_ref[...], w_ref[...])
            err = y - t_ref[...]
            loss_ref[...] += jnp.broadcast_to(0.5 * jnp.sum(jnp.mean(err * err, axis=-1, keepdims=True)), (8, 128))
            dx, dw = vjp_fn(err * (1.0 / D))
            dx_ref[...] = dx
            dw_ref[...] += dw

    return pl.pallas_call(
        body, name=name, grid=(n_t,),
        in_specs=[pl.BlockSpec((TILE, D), lambda i: (i, 0)), pl.BlockSpec((TILE, D), lambda i: (jnp.maximum(i - 1, 0), 0)),
                  pl.BlockSpec((1, D), lambda i: (0, 0))],
        out_specs=[pl.BlockSpec((8, 128), lambda i: (0, 0)), pl.BlockSpec((TILE, D), lambda i: (i, 0)),
                   pl.BlockSpec((1, D), lambda i: (0, 0))],
        out_shape=[jax.ShapeDtypeStruct((8, 128), F32), jax.ShapeDtypeStruct((t, D), F32),
                   jax.ShapeDtypeStruct((1, D), F32)],
        compiler_params=_cparams(1),
    )(xf, target, fnw)


def all_gather(xs, name):
    n = len(xs)
    blks = [tuple(x.shape) for x in xs]
    per = N_DEV - 1

    def body(*refs):
        x_refs, o_refs = refs[:n], refs[n:2 * n]
        ssem, rsem, lsem = refs[2 * n:]
        xi, yi, ci = lax.axis_index("x"), lax.axis_index("y"), lax.axis_index("c")
        me3 = (xi, yi, ci)
        me = 4 * xi + 2 * yi + ci
        flat = lambda d: 4 * d[0] + 2 * d[1] + d[2]
        sibling = (xi, yi, 1 - ci)
        chips = [(1 - xi, yi), (xi, 1 - yi), (1 - xi, 1 - yi)]

        def copy(a, k, block, to, src=None):
            rows = o_refs[a].at[flat(block)]
            return pltpu.make_async_remote_copy(
                src_ref=rows if src is None else src, dst_ref=rows,
                send_sem=ssem.at[a * per + k], recv_sem=rsem.at[a * per + k],
                device_id=to, device_id_type=pl.DeviceIdType.MESH)

        own = [pltpu.make_async_copy(x_refs[a], o_refs[a].at[me], lsem.at[a]) for a in range(n)]
        first = []
        for a in range(n):
            first.append(copy(a, 0, me3, sibling, src=x_refs[a]))
            first += [copy(a, 1 + j, me3, (*chip, ci), src=x_refs[a]) for j, chip in enumerate(chips)]
        for cp in own + first:
            cp.start()
        passed = []
        for j, chip in enumerate(chips):
            for a in range(n):
                copy(a, 1 + j, (*chip, ci), me3).wait_recv()
                fw = copy(a, 4 + j, (*chip, ci), sibling)
                fw.start()
                passed.append(fw)
        for a in range(n):
            copy(a, 0, sibling, me3).wait_recv()
            for j, chip in enumerate(chips):
                copy(a, 4 + j, (*chip, 1 - ci), me3).wait_recv()
        for cp in first + passed:
            cp.wait_send()
        for cp in own:
            cp.wait()

    hbm = pl.BlockSpec(memory_space=pl.ANY)
    res = pl.pallas_call(
        body, name=name, in_specs=[hbm] * n, out_specs=[hbm] * n,
        out_shape=[jax.ShapeDtypeStruct((N_DEV,) + blks[a], xs[a].dtype) for a in range(n)],
        scratch_shapes=[pltpu.SemaphoreType.DMA((n * per,)), pltpu.SemaphoreType.DMA((n * per,)),
                        pltpu.SemaphoreType.DMA((n,))],
        compiler_params=pltpu.CompilerParams(has_side_effects=True),
    )(*xs)
    return list(res)


def reduce_scatter(xs, name):
    n = len(xs)
    ci = lax.axis_index("c")
    hbm = pl.BlockSpec(memory_space=pl.ANY)
    side = pltpu.CompilerParams(has_side_effects=True)

    def pair_body(*refs):
        x_refs, o_refs = refs[:n], refs[n:2 * n]
        ssem, rsem = refs[2 * n:]
        xi, yi, cc = lax.axis_index("x"), lax.axis_index("y"), lax.axis_index("c")
        cps = []
        for a in range(n):
            for q in range(4):
                cps.append(pltpu.make_async_remote_copy(
                    src_ref=x_refs[a].at[2 * q + (1 - cc)], dst_ref=o_refs[a].at[q],
                    send_sem=ssem.at[4 * a + q], recv_sem=rsem.at[4 * a + q],
                    device_id=(xi, yi, 1 - cc), device_id_type=pl.DeviceIdType.MESH))
        for cp in cps:
            cp.start()
        for cp in cps:
            cp.wait()

    halves = [(4,) + tuple(x.shape[1:]) for x in xs]
    got = pl.pallas_call(
        pair_body, name=name + "_pair", in_specs=[hbm] * n, out_specs=[hbm] * n,
        out_shape=[jax.ShapeDtypeStruct(halves[a], xs[a].dtype) for a in range(n)],
        scratch_shapes=[pltpu.SemaphoreType.DMA((4 * n,)), pltpu.SemaphoreType.DMA((4 * n,))], compiler_params=side,
    )(*xs)

    sums = []
    for a in range(n):
        mine = lax.dynamic_index_in_dim(xs[a].reshape((4, 2) + tuple(xs[a].shape[1:])), ci, axis=1, keepdims=False)

        def add_body(a_ref, b_ref, o_ref):
            o_ref[...] = (a_ref[...].astype(F32) + b_ref[...].astype(F32)).astype(o_ref.dtype)

        blk = pl.BlockSpec((1,) + halves[a][1:], lambda q: (q, 0, 0))
        sums.append(pl.pallas_call(add_body, name=f"{name}_sum{a}", grid=(4,), in_specs=[blk, blk], out_specs=blk,
                                   out_shape=jax.ShapeDtypeStruct(halves[a], xs[a].dtype),
                                   compiler_params=_cparams(1))(mine, got[a]))

    def chips_body(*refs):
        s_refs, o_refs = refs[:n], refs[n:2 * n]
        ssem, rsem, lsem = refs[2 * n:]
        xi, yi, cc = lax.axis_index("x"), lax.axis_index("y"), lax.axis_index("c")
        my_chip = 2 * xi + yi
        chips = [(1 - xi, yi), (xi, 1 - yi), (1 - xi, 1 - yi)]
        own = [pltpu.make_async_copy(s_refs[a].at[my_chip], o_refs[a].at[my_chip], lsem.at[a]) for a in range(n)]
        sends, lands = [], []
        for a in range(n):
            for j, (px, py) in enumerate(chips):
                common = dict(send_sem=ssem.at[3 * a + j], recv_sem=rsem.at[3 * a + j], device_id=(px, py, cc),
                              device_id_type=pl.DeviceIdType.MESH)
                sends.append(pltpu.make_async_remote_copy(src_ref=s_refs[a].at[2 * px + py],
                                                          dst_ref=o_refs[a].at[my_chip], **common))
                lands.append(pltpu.make_async_remote_copy(src_ref=s_refs[a].at[2 * px + py],
                                                          dst_ref=o_refs[a].at[2 * px + py], **common))
        for cp in own + sends:
            cp.start()
        for cp in lands:
            cp.wait_recv()
        for cp in sends:
            cp.wait_send()
        for cp in own:
            cp.wait()

    res = pl.pallas_call(
        chips_body, name=name + "_chips", in_specs=[hbm] * n, out_specs=[hbm] * n,
        out_shape=[jax.ShapeDtypeStruct(halves[a], xs[a].dtype) for a in range(n)],
        scratch_shapes=[pltpu.SemaphoreType.DMA((3 * n,)), pltpu.SemaphoreType.DMA((3 * n,)),
                        pltpu.SemaphoreType.DMA((n,))], compiler_params=side,
    )(*sums)
    return list(res)


def mod_forward(crows, w_mod, b_shard, name):
    cols = w_mod.shape[2]

    def body(c_ref, w_ref, b_ref, o_ref):
        o_ref[0] = _dot(_silu(c_ref[...]).astype(BF), w_ref[0].astype(BF)) + b_ref[0]

    return pl.pallas_call(
        body, name=name, grid=(2,),
        in_specs=[pl.BlockSpec((16, D), lambda l: (0, 0)), pl.BlockSpec((1, D, cols), lambda l: (l, 0, 0)),
                  pl.BlockSpec((1, 1, cols), lambda l: (l, 0, 0))],
        out_specs=pl.BlockSpec((1, 16, cols), lambda l: (l, 0, 0)),
        out_shape=jax.ShapeDtypeStruct((2, 16, cols), F32), compiler_params=_cparams(1),
    )(crows, w_mod, b_shard)


def mod_backward(crows, w_mod, d_own, d_ctx, name):
    cols = w_mod.shape[2]

    def body(c_ref, w_ref, do_ref, dc_ref, gw_ref, gs_ref):
        dc = dc_ref[0]
        dsum = dc[0:1]
        for s in range(1, N_DEV):
            dsum = dsum + dc[s:s + 1]
        row = lax.broadcasted_iota(jnp.int32, (8, cols), 0)
        d16 = jnp.concatenate([do_ref[0], jnp.where(row == 0, jnp.broadcast_to(dsum, (8, cols)), 0.0)], axis=0)
        gw_ref[0] = _dot_tn(_silu(c_ref[...]).astype(BF), d16.astype(BF))
        gs_ref[0] = _dot_nt(jnp.broadcast_to(dsum, (8, cols)).astype(BF), w_ref[0].astype(BF))

    return pl.pallas_call(
        body, name=name, grid=(2,),
        in_specs=[pl.BlockSpec((16, D), lambda l: (0, 0)), pl.BlockSpec((1, D, cols), lambda l: (l, 0, 0)),
                  pl.BlockSpec((1, 8, cols), lambda l: (l, 0, 0)), pl.BlockSpec((1, 8, cols), lambda l: (l, 0, 0))],
        out_specs=[pl.BlockSpec((1, D, cols), lambda l: (l, 0, 0)), pl.BlockSpec((1, 8, D), lambda l: (l, 0, 0))],
        out_shape=[jax.ShapeDtypeStruct((2, D, cols), F32), jax.ShapeDtypeStruct((2, 8, D), F32)],
        compiler_params=_cparams(1),
    )(crows, w_mod, d_own, d_ctx)


def silu_grad_scale(c_ctx, ds, name):
    def body(c_ref, ds_ref, o_ref):
        cc = c_ref[...]
        sg = jax.nn.sigmoid(cc)
        o_ref[...] = (ds_ref[0][0:1] + ds_ref[1][0:1]) * (sg * (1.0 + cc * (1.0 - sg)))

    return pl.pallas_call(body, name=name, out_shape=jax.ShapeDtypeStruct((1, D), F32))(c_ctx, ds)


def _adamw_math(p_ref, w_ref, m_ref, v_ref, g_ref, d_ref, nm_ref, nv_ref):
    g = p_ref[0].astype(F32)
    for s in range(1, p_ref.shape[0]):
        g = g + p_ref[s].astype(F32)
    mm = ADAM_B1 * m_ref[...] + (1.0 - ADAM_B1) * g
    vv = ADAM_B2 * v_ref[...] + (1.0 - ADAM_B2) * (g * g)
    m_hat = mm / (1.0 - ADAM_B1 ** ADAM_STEP)
    v_hat = vv / (1.0 - ADAM_B2 ** ADAM_STEP)
    g_ref[...] = g
    d_ref[...] = -ADAM_LR * (m_hat / (jnp.sqrt(v_hat) + ADAM_EPS) + ADAM_WD * w_ref[...])
    nm_ref[...] = mm
    nv_ref[...] = vv


def adamw(parts, w, m, v, name):
    n_parts, rows, cols = parts.shape
    lanes = -(-cols // 128) * 128
    block_rows = min(rows, 1 << ((ADAMW_BLOCK_ELEMS // lanes).bit_length() - 1))
    assert rows % block_rows == 0

    def body(*refs):
        _adamw_math(*refs)

    spec = pl.BlockSpec((block_rows, cols), lambda i: (i, 0))
    return pl.pallas_call(
        body, name=name, grid=(rows // block_rows,),
        in_specs=[pl.BlockSpec((n_parts, block_rows, cols), lambda i: (0, i, 0)), spec, spec, spec],
        out_specs=[spec] * 4, out_shape=[jax.ShapeDtypeStruct((rows, cols), F32)] * 4,
        compiler_params=_cparams(1),
    )(parts, w, m, v)


def adamw_group(items, name):
    n = len(items)

    def body(*refs):
        for a in range(n):
            _adamw_math(*refs[4 * a:4 * a + 4], *refs[4 * n + 4 * a:4 * n + 4 * a + 4])

    flat_in = [x for it in items for x in it]
    out_shape = [jax.ShapeDtypeStruct(it[1].shape, F32) for it in items for _ in range(4)]
    res = pl.pallas_call(body, name=name, out_shape=out_shape,
                         compiler_params=pltpu.CompilerParams(vmem_limit_bytes=VMEM_LIMIT_MB * 1024 * 1024))(*flat_in)
    return [tuple(res[4 * a:4 * a + 4]) for a in range(n)]


def _pad_cols(w, segs, total):
    parts, pos = [], 0
    for dst, src, wd in segs:
        if dst > pos:
            parts.append(jnp.zeros(w.shape[:-1] + (dst - pos,), w.dtype))
        parts.append(w[..., src:src + wd])
        pos = dst + wd
    if pos < total:
        parts.append(jnp.zeros(w.shape[:-1] + (total - pos,), w.dtype))
    return jnp.concatenate(parts, axis=-1)


def _unpad_cols(g, segs):
    return jnp.concatenate([g[..., dst:dst + wd] for dst, _, wd in segs], axis=-1)


def _rope_tables(n_lat):
    rows = n_lat // GRID_W
    freq = ROPE_BASE ** (-jnp.arange(16, dtype=F32) * 2.0 / 32)
    a_row = jnp.arange(rows).astype(F32)[:, None] * freq[None, :]
    a_col = jnp.arange(GRID_W).astype(F32)[:, None] * freq[None, :]
    per_row = lambda tbl: jnp.repeat(tbl, GRID_W, axis=0)
    per_col = lambda tbl: jnp.tile(tbl, (rows, 1))
    cr, sr, cc, sc = per_row(jnp.cos(a_row)), per_row(jnp.sin(a_row)), per_col(jnp.cos(a_col)), per_col(jnp.sin(a_col))
    z = jnp.zeros((n_lat, 16), F32)
    cos = jnp.concatenate([cr, cr, cc, cc, jnp.ones((n_lat, 64), F32)], axis=1)
    sa = jnp.concatenate([-sr, z, -sc, z, jnp.zeros((n_lat, 64), F32)], axis=1)
    sb = jnp.concatenate([z, sr, z, sc, jnp.zeros((n_lat, 64), F32)], axis=1)
    ident = lambda fill: jnp.full((TILE, 128), fill, F32)
    return (jnp.concatenate([ident(1.0), cos]), jnp.concatenate([ident(0.0), sa]), jnp.concatenate([ident(0.0), sb]))


def _gathered_to_full(g, name):
    if name in ("w_out", "w_ff2"):
        return jnp.transpose(g, (1, 0, 2, 3)).reshape(2, -1, g.shape[-1])
    return jnp.transpose(g, (1, 2, 0, 3)).reshape(2, g.shape[2], -1)


def _full_to_chunks(gw, name):
    if name in ("w_out", "w_ff2"):
        return jnp.transpose(gw.reshape(2, N_DEV, -1, gw.shape[-1]), (1, 0, 2, 3))
    return jnp.transpose(gw.reshape(2, gw.shape[1], N_DEV, -1), (2, 0, 1, 3))


def kernel(x, c, ctx, c_ctx, w_mod, b_mod, norm1_w, w_in, w_out, sgu_norm_w, sgu_norm_b, sgu_w, sgu_b, gla_wg_fwd, gla_bg_fwd, gla_wg_bwd, gla_bg_bwd, gla_norm_w, mla_q_norm_w, mla_w_uq, mla_kv_norm_w, mla_w_ukv, norm2_w, w_ff1, w_ff2, final_norm_w, loss_target, m_c_ctx, m_w_mod, m_b_mod, m_norm1_w, m_w_in, m_w_out, m_sgu_norm_w, m_sgu_norm_b, m_sgu_w, m_sgu_b, m_gla_wg_fwd, m_gla_bg_fwd, m_gla_wg_bwd, m_gla_bg_bwd, m_gla_norm_w, m_mla_q_norm_w, m_mla_w_uq, m_mla_kv_norm_w, m_mla_w_ukv, m_norm2_w, m_w_ff1, m_w_ff2, m_final_norm_w, v_c_ctx, v_w_mod, v_b_mod, v_norm1_w, v_w_in, v_w_out, v_sgu_norm_w, v_sgu_norm_b, v_sgu_w, v_sgu_b, v_gla_wg_fwd, v_gla_bg_fwd, v_gla_wg_bwd, v_gla_bg_bwd, v_gla_norm_w, v_mla_q_norm_w, v_mla_w_uq, v_mla_kv_norm_w, v_mla_w_ukv, v_norm2_w, v_w_ff1, v_w_ff2, v_final_norm_w):
    W = dict(c_ctx=c_ctx, w_mod=w_mod, b_mod=b_mod, norm1_w=norm1_w, w_in=w_in, w_out=w_out, sgu_norm_w=sgu_norm_w,
             sgu_norm_b=sgu_norm_b, sgu_w=sgu_w, sgu_b=sgu_b, gla_wg_fwd=gla_wg_fwd, gla_bg_fwd=gla_bg_fwd,
             gla_wg_bwd=gla_wg_bwd, gla_bg_bwd=gla_bg_bwd, gla_norm_w=gla_norm_w, mla_q_norm_w=mla_q_norm_w,
             mla_w_uq=mla_w_uq, mla_kv_norm_w=mla_kv_norm_w, mla_w_ukv=mla_w_ukv, norm2_w=norm2_w, w_ff1=w_ff1,
             w_ff2=w_ff2, final_norm_w=final_norm_w)
    M = dict(c_ctx=m_c_ctx, w_mod=m_w_mod, b_mod=m_b_mod, norm1_w=m_norm1_w, w_in=m_w_in, w_out=m_w_out,
             sgu_norm_w=m_sgu_norm_w, sgu_norm_b=m_sgu_norm_b, sgu_w=m_sgu_w, sgu_b=m_sgu_b, gla_wg_fwd=m_gla_wg_fwd,
             gla_bg_fwd=m_gla_bg_fwd, gla_wg_bwd=m_gla_wg_bwd, gla_bg_bwd=m_gla_bg_bwd, gla_norm_w=m_gla_norm_w,
             mla_q_norm_w=m_mla_q_norm_w, mla_w_uq=m_mla_w_uq, mla_kv_norm_w=m_mla_kv_norm_w, mla_w_ukv=m_mla_w_ukv,
             norm2_w=m_norm2_w, w_ff1=m_w_ff1, w_ff2=m_w_ff2, final_norm_w=m_final_norm_w)
    V = dict(c_ctx=v_c_ctx, w_mod=v_w_mod, b_mod=v_b_mod, norm1_w=v_norm1_w, w_in=v_w_in, w_out=v_w_out,
             sgu_norm_w=v_sgu_norm_w, sgu_norm_b=v_sgu_norm_b, sgu_w=v_sgu_w, sgu_b=v_sgu_b, gla_wg_fwd=v_gla_wg_fwd,
             gla_bg_fwd=v_gla_bg_fwd, gla_wg_bwd=v_gla_wg_bwd, gla_bg_bwd=v_gla_bg_bwd, gla_norm_w=v_gla_norm_w,
             mla_q_norm_w=v_mla_q_norm_w, mla_w_uq=v_mla_w_uq, mla_kv_norm_w=v_mla_kv_norm_w, mla_w_ukv=v_mla_w_ukv,
             norm2_w=v_norm2_w, w_ff1=v_w_ff1, w_ff2=v_w_ff2, final_norm_w=v_final_norm_w)

    n_lat = x.shape[1]
    assert ctx.shape[1] == TILE and n_lat % TILE == 0 and x.shape[2] == D
    t_all = TILE + n_lat
    n_t = t_all // TILE
    me = 4 * lax.axis_index("x") + 2 * lax.axis_index("y") + lax.axis_index("c")
    mod_cols = w_mod.shape[2]

    c_all = all_gather([c], "ag_c")[0].reshape(N_DEV, D)
    crows = jnp.concatenate([c_all, c_ctx[None, :], jnp.zeros((7, D), F32)], axis=0)
    b_shard = lax.dynamic_slice_in_dim(b_mod, me * mod_cols, mod_cols, axis=1)[:, None, :]
    mod_sh = mod_forward(crows, w_mod, b_shard, "mod_fwd")
    mod_g = all_gather([mod_sh.reshape(32, mod_cols)], "ag_mod")[0]
    mod_full = jnp.transpose(mod_g.reshape(N_DEV, 2, 16, mod_cols), (1, 2, 0, 3)).reshape(2, 16, 6 * D)
    mod_own = lax.dynamic_index_in_dim(mod_full, me, axis=1, keepdims=False)
    mod_ctx = mod_full[:, 8, :]
    pad2 = jnp.zeros((2, D), F32)
    modl = [jnp.stack([jnp.concatenate([mod_ctx[l].reshape(6, D), pad2]),
                       jnp.concatenate([mod_own[l].reshape(6, D), pad2])]) for l in range(2)]

    v2 = lambda a: a[None, :] if a.ndim == 1 else a.reshape(-1, a.shape[-1])
    gathered = all_gather([v2(W[k].astype(BF)) for k in BIG_NAMES], "ag_weights")
    full = {k: _gathered_to_full(g.reshape((N_DEV,) + W[k].shape), k) for k, g in zip(BIG_NAMES, gathered)}
    w_in_p = _pad_cols(full["w_in"], W_IN_SEGS, P_COLS)
    w_uq_p = _pad_cols(full["mla_w_uq"], W_UQ_SEGS, 1024).astype(F32)
    w_ukv_f = full["mla_w_ukv"].astype(F32)
    wgf_p = jnp.pad(gla_wg_fwd, ((0, 0), (0, 112), (0, 0)))
    wgb_p = jnp.pad(gla_wg_bwd, ((0, 0), (0, 112), (0, 0)))
    sgu_bx = jnp.repeat(jnp.transpose(sgu_b, (0, 2, 1)), 64, axis=2)
    gnw_t = jnp.tile(gla_norm_w, (1, HEADS))
    rc, rsa, rsb = _rope_tables(n_lat)

    xin = jnp.concatenate([ctx[0], x[0]], axis=0)
    row = lambda a: a[None, :]

    def pre_ins(l, xl):
        return [("x", "tile", True, xl), ("mod", "kind", True, modl[l]), ("n1w", "full", True, row(norm1_w[l])),
                ("w_in", "wfull", False, w_in_p[l]), ("sgu_nw", "full", True, row(sgu_norm_w[l])),
                ("sgu_nb", "full", True, row(sgu_norm_b[l])), ("sgu_w", "full", True, sgu_w[l]),
                ("sgu_bx", "full", True, sgu_bx[l]), ("wgf", "full", True, wgf_p[l]), ("bgf", "full", True, row(gla_bg_fwd[l])),
                ("wgb", "full", True, wgb_p[l]), ("bgb", "full", True, row(gla_bg_bwd[l])),
                ("qnw", "full", True, row(mla_q_norm_w[l])), ("w_uq", "full", True, w_uq_p[l]),
                ("kvnw", "full", True, row(mla_kv_norm_w[l])), ("w_ukv", "full", True, w_ukv_f[l]),
                ("rc", "tile", False, rc), ("rsa", "tile", False, rsa), ("rsb", "tile", False, rsb)]

    pre_outs = [("y_sgu", 256, F32), ("qg", 128, F32), ("kg", 128, F32), ("vg", 256, F32), ("lgf", 128, F32),
                ("lgb", 128, F32), ("gr", 256, F32), ("q_cat", 1024, BF), ("k_cat", 1024, BF), ("v", 512, BF)]

    def out_ins(l, xl, a):
        return [("x", "tile", True, xl), ("mod", "kind", True, modl[l]), ("y_sgu", "tile", True, a["y_sgu"]),
                ("o_f", "tile", True, a["o_f"]), ("o_b", "tile", True, a["o_b"]), ("gr", "tile", True, a["gr"]),
                ("y_mla", "tile", True, a["y_mla"]), ("gnw", "full", True, row(gnw_t[l])),
                ("w_out", "wfull", False, full["w_out"][l])]

    def ffn_ins(l, x1):
        return [("x1", "tile", True, x1), ("mod", "kind", True, modl[l]), ("n2w", "full", True, row(norm2_w[l])),
                ("w_ff1", "wcols", False, full["w_ff1"][l]), ("w_ff2", "wrows", False, full["w_ff2"][l])]

    saved, xl = [], xin
    for l in range(2):
        a = tile_forward(pre_tile, f"pre_fwd{l}", t_all, pre_ins(l, xl), pre_outs)
        a["o_f"], a["sf"], a["o_b"], a["sb"] = gla_forward(a["qg"], a["kg"], a["vg"], a["lgf"], a["lgb"], f"gla_fwd{l}")
        a["y_mla"], a["lse"] = mla_forward(a["q_cat"], a["k_cat"], a["v"], f"mla_fwd{l}")
        a["x"] = xl
        a["x1"] = tile_forward(attn_out_tile, f"out_fwd{l}", t_all, out_ins(l, xl, a), [("x1", D, F32)])["x1"]
        ff = tile_forward(ffn_tile, f"ffn_fwd{l}", t_all, ffn_ins(l, a["x1"]), [("x2", D, F32), ("f", D, F32)])
        xl, a["f"] = ff["x2"], ff["f"]
        saved.append(a)

    loss_blk, dx, d_fnw = final_loss(xl, loss_target[0], row(final_norm_w), "final_loss")
    loss = lax.psum(loss_blk[0, 0], AXES)

    G = {}
    dmods = []
    for l in (1, 0):
        a = saved[l]
        dx1, dmod3, dn2w, zpre, zf_t, h2_t, a_ff2 = ffn_backward(
            a["x1"], modl[l], row(norm2_w[l]), full["w_ff1"][l], full["w_ff2"][l], dx, a["f"], f"ffn_bwd{l}")
        gw_ff1 = wgrad(h2_t, zpre, f"wg_ff1_{l}")
        gw_ff2 = jnp.transpose(wgrad(zf_t, a_ff2, f"wg_ff2_{l}"))
        g2, e2 = tile_backward(attn_out_tile, f"out_bwd{l}", t_all, out_ins(l, a["x"], a), [("x1", dx1)],
                               [("zt", D)], [("a_out", D)])
        gw_out = wgrad(e2["a_out"], e2["zt"], f"wg_out_{l}")
        dl_rows, do_bf = mla_delta(g2["y_mla"], a["y_mla"], f"mla_delta{l}")
        dq_cat, dk_cat, dv = mla_backward(a["q_cat"], a["k_cat"], a["v"], a["lse"], dl_rows, do_bf, f"mla_bwd{l}")
        dqf, dkf, dvf, dgf, dqb, dkb, dvb, dgb = gla_backward(
            a["qg"], a["kg"], a["vg"], a["lgf"], a["lgb"], a["sf"], a["sb"], g2["o_f"], f"gla_bwd{l}")
        cots = [("y_sgu", g2["y_sgu"]), ("qg", [dqf, dqb]), ("kg", [dkf, dkb]), ("vg", [dvf, dvb]), ("lgf", dgf),
                ("lgb", dgb), ("gr", g2["gr"]), ("q_cat", dq_cat), ("k_cat", dk_cat), ("v", dv)]
        g1, e1 = tile_backward(pre_tile, f"pre_bwd{l}", t_all, pre_ins(l, a["x"]), cots, [("zp", P_COLS)], [("a_in", D)],
                               resid=("x", g2["x"]))
        gw_in = _unpad_cols(wgrad(e1["a_in"], e1["zp"], f"wg_in_{l}", bk2=P_COLS), W_IN_SEGS)
        dx = g1["x"]
        dmods.append(g1["mod"] + g2["mod"] + dmod3)
        G[l] = dict(w_in=gw_in, w_out=gw_out, w_ff1=gw_ff1, w_ff2=gw_ff2,
                    mla_w_uq=_unpad_cols(g1["w_uq"], W_UQ_SEGS), mla_w_ukv=g1["w_ukv"],
                    norm1_w=g1["n1w"][0], norm2_w=dn2w[0], sgu_norm_w=g1["sgu_nw"][0], sgu_norm_b=g1["sgu_nb"][0],
                    sgu_w=g1["sgu_w"], sgu_b=jnp.transpose(g1["sgu_bx"].reshape(128, HEADS, 64).sum(-1)),
                    gla_wg_fwd=g1["wgf"][:16], gla_bg_fwd=g1["bgf"][0], gla_wg_bwd=g1["wgb"][:16], gla_bg_bwd=g1["bgb"][0],
                    gla_norm_w=g2["gnw"][0].reshape(HEADS, 64).sum(0), mla_q_norm_w=g1["qnw"][0], mla_kv_norm_w=g1["kvnw"][0])
    dmods = dmods[::-1]
    grad_x = dx[TILE:][None]

    dmod_pack = jnp.stack([jnp.stack([dmods[l][1, :6].reshape(-1), dmods[l][0, :6].reshape(-1)]) for l in range(2)])
    dmod_all = all_gather([dmod_pack.reshape(4, 6 * D)], "ag_dmod")[0].reshape(N_DEV, 2, 2, 6 * D)
    dsl = lax.dynamic_slice_in_dim(dmod_all, me * mod_cols, mod_cols, axis=3)
    d_own = jnp.transpose(dsl[:, :, 0, :], (1, 0, 2))
    d_ctx = jnp.transpose(dsl[:, :, 1, :], (1, 0, 2))
    g_w_mod, ds_cc = mod_backward(crows, w_mod, d_own, d_ctx, "mod_bwd")
    g_c_ctx_part = silu_grad_scale(c_ctx[None, :], ds_cc, "silu_bwd")[0]
    g_b_mod_part = jnp.stack([dmods[l][1, :6].reshape(-1) + dmods[l][0, :6].reshape(-1) for l in range(2)])

    small_g = dict(c_ctx=g_c_ctx_part, b_mod=g_b_mod_part, final_norm_w=d_fnw[0])
    for k in SMALL_NAMES:
        if k not in small_g:
            small_g[k] = jnp.stack([G[0][k], G[1][k]])
    res = {}
    sparts = all_gather([v2(small_g[k]) for k in SMALL_NAMES], "ag_small")
    s_out = adamw_group([(sparts[j], v2(W[k]), v2(M[k]), v2(V[k])) for j, k in enumerate(SMALL_NAMES)], "adamw_small")
    for j, k in enumerate(SMALL_NAMES):
        res[k] = [o.reshape(W[k].shape) for o in s_out[j]]

    chunks = []
    for k in BIG_NAMES:
        ch = _full_to_chunks(jnp.stack([G[0][k], G[1][k]]), k).astype(BF)
        chunks.append(ch.reshape(N_DEV, -1, ch.shape[-1]))
    bparts = reduce_scatter(chunks, "rs_grads")
    for j, k in enumerate(BIG_NAMES):
        res[k] = [o.reshape(W[k].shape) for o in adamw(bparts[j], v2(W[k]), v2(M[k]), v2(V[k]), f"adamw_{k}")]
    res["w_mod"] = [o.reshape(w_mod.shape)
                    for o in adamw(v2(g_w_mod)[None], v2(w_mod), v2(m_w_mod), v2(v_w_mod), "adamw_w_mod")]
    outs = [loss, grad_x]
    for j in range(4):
        outs += [res[k][j] for k in WEIGHT_ORDER]
    return tuple(outs)
```

```python
import jax
import jax.numpy as jnp
from jax import lax
from jax.experimental import pallas as pl
from jax.experimental.pallas import tpu as pltpu

F32 = jnp.float32
BF = jnp.bfloat16

N_DEV = 8
AXES = ("x", "y", "c")
EPS = 1e-6
D = 1024
TILE = 256
GCH = 128
SGU_CHUNK = 128
HEADS = 4
ROPE_BASE = 10000.0
GRID_W = 64
GLA_TAU = 16.0
ATT_SCALE = (128 + 64) ** -0.5
ATT_SCALE_LOG2 = ATT_SCALE * 1.4426950408889634
LN2 = 0.6931471805599453
KV_CH = 512
KV_CH_FWD = 2048
QT_FWD = 256
Q_CH_BWD = 1024
KT_BWD = 768
MLA_UNROLL = 2
D_FF = 4096
FF_CH = 1024

ADAM_LR = 0.001
ADAM_B1 = 0.9
ADAM_B2 = 0.999
ADAM_EPS = 1e-08
ADAM_WD = 0.01
ADAM_STEP = 10

VMEM_LIMIT_MB = 56
ADAMW_BLOCK_ELEMS = 256 * 1024

W_IN_SEGS = ((0, 0, 128), (128, 128, 256), (384, 384, 16), (512, 400, 16), (640, 416, 256), (896, 672, 64),
             (1024, 736, 256), (1280, 992, 256), (1536, 1248, 128), (1664, 1376, 256), (1920, 1632, 256))
P_COLS = 2176
O_GK, O_GV, O_GGF, O_GGB, O_CKV, O_KR, O_SU, O_SV, O_GQ, O_GR, O_DQ = (s[0] for s in W_IN_SEGS)
W_UQ_SEGS = tuple((h * 256, h * 192, 192) for h in range(HEADS))

SMALL_NAMES = ("c_ctx", "b_mod", "norm1_w", "sgu_norm_w", "sgu_norm_b", "sgu_w", "sgu_b", "gla_wg_fwd", "gla_bg_fwd",
               "gla_wg_bwd", "gla_bg_bwd", "gla_norm_w", "mla_q_norm_w", "mla_kv_norm_w", "norm2_w", "final_norm_w")
BIG_NAMES = ("w_in", "w_out", "mla_w_uq", "mla_w_ukv", "w_ff1", "w_ff2")
WEIGHT_ORDER = ("c_ctx", "w_mod", "b_mod", "norm1_w", "w_in", "w_out", "sgu_norm_w", "sgu_norm_b", "sgu_w", "sgu_b",
                "gla_wg_fwd", "gla_bg_fwd", "gla_wg_bwd", "gla_bg_bwd", "gla_norm_w", "mla_q_norm_w", "mla_w_uq",
                "mla_kv_norm_w", "mla_w_ukv", "norm2_w", "w_ff1", "w_ff2", "final_norm_w")


def _cparams(n_axes):
    return pltpu.CompilerParams(dimension_semantics=("arbitrary",) * n_axes,
                                vmem_limit_bytes=VMEM_LIMIT_MB * 1024 * 1024)


def _dot(a, b):
    return jnp.dot(a, b, preferred_element_type=F32)


def _dot_nt(a, b):
    return lax.dot_general(a, b, (((1,), (1,)), ((), ())), preferred_element_type=F32)


def _dot_tn(a, b):
    return lax.dot_general(a, b, (((0,), (0,)), ((), ())), preferred_element_type=F32)


def _nn(a, b):
    return _dot(a.astype(BF), b.astype(BF))


def _nt(a, b):
    return _dot_nt(a.astype(BF), b.astype(BF))


def _tn(a, b):
    return _dot_tn(a.astype(BF), b.astype(BF))


nn_d = jax.custom_vjp(_nn)
nt_d = jax.custom_vjp(_nt)
tn_d = jax.custom_vjp(_tn)
nn_d.defvjp(lambda a, b: (_nn(a, b), (a, b)), lambda r, dy: (_nt(dy, r[1]), _tn(r[0], dy)))
nt_d.defvjp(lambda a, b: (_nt(a, b), (a, b)), lambda r, dy: (_nn(dy, r[1]), _tn(dy, r[0])))
tn_d.defvjp(lambda a, b: (_tn(a, b), (a, b)), lambda r, dy: (_nt(r[1], dy), _nn(r[0], dy)))


def nn_const(w_bf, diff):
    def raw(a):
        return _dot(a.astype(BF), w_bf)

    if not diff:
        return raw
    f = jax.custom_vjp(raw)
    f.defvjp(lambda a: (raw(a), None), lambda _, dy: (_dot_nt(dy.astype(BF), w_bf),))
    return f


def _split3(g):
    hi = g.astype(BF)
    r = g - hi.astype(F32)
    mid = r.astype(BF)
    lo = (r - mid.astype(F32)).astype(BF)
    return hi, mid, lo


def make_cum(tri_bf, tri_t_bf, diff):
    def raw(g, t):
        hi, mid, lo = _split3(g)
        return _dot(t, hi) + _dot(t, mid) + _dot(t, lo)

    def fwd(g):
        return raw(g, tri_bf)

    if not diff:
        return fwd
    cum = jax.custom_vjp(fwd)
    cum.defvjp(lambda g: (fwd(g), None), lambda _, db: (raw(db, tri_t_bf),))
    return cum


def _roll_lanes(x, shift):
    return pltpu.roll(x, shift, 1)


def make_rope(c, sa, sb, diff):
    def raw(x):
        return x * c + _roll_lanes(x, 112) * sa + _roll_lanes(x, 16) * sb

    if not diff:
        return raw
    f = jax.custom_vjp(raw)
    f.defvjp(lambda x: (raw(x), None),
             lambda _, dy: (dy * c + _roll_lanes(dy * sa, 16) + _roll_lanes(dy * sb, 112),))
    return f


def _ops(diff):
    return (nn_d, nt_d, tn_d) if diff else (_nn, _nt, _tn)


def _rms(x, w):
    return x * lax.rsqrt(jnp.mean(x * x, axis=-1, keepdims=True) + EPS) * w


def _gelu(x):
    return 0.5 * x * (1.0 + jnp.tanh(0.7978845608028654 * (x + 0.044715 * (x * x * x))))


def _silu(x):
    return x * jax.nn.sigmoid(x)


def _log_sigmoid(z):
    return jnp.minimum(z, 0.0) - jnp.log(1.0 + jnp.exp(-jnp.abs(z)))


def _lane_group_mask(width, group, h):
    lane = lax.broadcasted_iota(jnp.int32, (1, width), 1)
    return ((lane >= h * group) & (lane < (h + 1) * group)).astype(F32)


def pre_tile(d, c, z):
    nn, _, _ = _ops(z is not None)
    rope = make_rope(c["rc"], c["rsa"], c["rsb"], z is not None)
    mod = d["mod"]
    h = _rms(d["x"], d["n1w"]) * (1.0 + mod[1:2]) + mod[0:1]
    p = nn_const(c["w_in"], z is not None)(h)
    if z is not None:
        p = p + z["zp"]
    gk, gv = p[:, O_GK:O_GK + 128], p[:, O_GV:O_GV + 256]
    ggf, ggb = p[:, O_GGF:O_GGF + 128], p[:, O_GGB:O_GGB + 128]
    ckv, kr = p[:, O_CKV:O_CKV + 256], p[:, O_KR:O_KR + 128]
    su, sv = p[:, O_SU:O_SU + 256], p[:, O_SV:O_SV + 256]
    gq, gr, dq = p[:, O_GQ:O_GQ + 128], p[:, O_GR:O_GR + 256], p[:, O_DQ:O_DQ + 256]

    u = _gelu(su)
    gv_ = _gelu(sv)
    mu = jnp.mean(gv_, axis=-1, keepdims=True)
    cen = gv_ - mu
    vn = cen * lax.rsqrt(jnp.mean(cen * cen, axis=-1, keepdims=True) + EPS) * d["sgu_nw"] + d["sgu_nb"]
    hm = [_lane_group_mask(256, 64, hh) for hh in range(HEADS)]
    rows = []
    for ci in range(vn.shape[0] // SGU_CHUNK):
        vc = vn[ci * SGU_CHUNK:(ci + 1) * SGU_CHUNK]
        s = d["sgu_bx"]
        for hh in range(HEADS):
            s = s + hm[hh] * nn(d["sgu_w"][hh], vc)
        rows.append(s)
    y_sgu = u * jnp.concatenate(rows, axis=0)

    qg = gq * (32 ** -0.5)
    lgf = _log_sigmoid(nn(ggf, d["wgf"]) + d["bgf"]) * (1.0 / GLA_TAU)
    lgb = _log_sigmoid(nn(ggb, d["wgb"]) + d["bgb"]) * (1.0 / GLA_TAU)

    kv = nn(_rms(ckv, d["kvnw"]), d["w_ukv"])
    kr_r = rope(kr)
    q = nn(_rms(dq, d["qnw"]), d["w_uq"])
    qs, ks, vs = [], [], []
    for hh in range(HEADS):
        qs += [q[:, hh * 256:hh * 256 + 128], rope(q[:, hh * 256 + 128:(hh + 1) * 256])]
        ks += [kv[:, hh * 256:hh * 256 + 128], kr_r]
        vs += [kv[:, hh * 256 + 128:(hh + 1) * 256]]
    outs = dict(y_sgu=y_sgu, qg=qg, kg=gk, vg=gv, lgf=lgf, lgb=lgb, gr=gr,
                q_cat=jnp.concatenate(qs, axis=-1) * ATT_SCALE_LOG2, k_cat=jnp.concatenate(ks, axis=-1), v=jnp.concatenate(vs, axis=-1))
    return outs, dict(a_in=h)


def attn_out_tile(d, c, z):
    mod = d["mod"]
    o = d["o_f"] + d["o_b"]
    ms = jnp.zeros_like(o)
    for hh in range(HEADS):
        m_h = _lane_group_mask(256, 64, hh)
        ms = ms + m_h * (jnp.sum(o * o * m_h, axis=-1, keepdims=True) * (1.0 / 64))
    yg = o * lax.rsqrt(ms + EPS) * d["gnw"] * _silu(d["gr"])
    y = jnp.concatenate([d["y_sgu"], yg, d["y_mla"]], axis=-1)
    t = nn_const(c["w_out"], z is not None)(y)
    if z is not None:
        t = t + z["zt"]
    return dict(x1=d["x"] + mod[2:3] * t), dict(a_out=y)


def ffn_tile(d, c, z):
    mod = d["mod"]
    h2 = _rms(d["x1"], d["n2w"]) * (1.0 + mod[4:5]) + mod[3:4]
    f = None
    a2s = []
    for j in range(D_FF // FF_CH):
        pre = nn_const(c["w_ff1"][j], z is not None)(h2)
        if z is not None:
            pre = pre + z["zpre"][:, j * FF_CH:(j + 1) * FF_CH]
        a = jnp.maximum(pre, 0.0)
        a2 = a * a
        a2s.append(a2)
        fj = nn_const(c["w_ff2"][j], z is not None)(a2)
        f = fj if f is None else f + fj
    if z is not None:
        f = f + z["zf"]
    return dict(x2=d["x1"] + mod[5:6] * f, f=f), dict(a_ff1=h2, a_ff2=jnp.concatenate(a2s, axis=-1))


def _in_spec(kind, arr, tile):
    if kind == "tile":
        return pl.BlockSpec((tile, arr.shape[1]), lambda i: (i, 0))
    if kind == "kind":
        return pl.BlockSpec((1,) + arr.shape[1:], lambda i: (jnp.where(i < TILE // tile, 0, 1), 0, 0))
    nd = arr.ndim
    if kind in ("wfull", "wcols", "wrows"):
        return pl.BlockSpec(arr.shape, lambda i: (0,) * nd, pipeline_mode=pl.Buffered(1))
    return pl.BlockSpec(arr.shape, lambda i: (0,) * nd)


def _load(kind, ref):
    if kind == "kind":
        return ref[0]
    if kind == "wcols":
        return [ref[:, j * FF_CH:(j + 1) * FF_CH] for j in range(ref.shape[1] // FF_CH)]
    if kind == "wrows":
        return [ref[j * FF_CH:(j + 1) * FF_CH, :] for j in range(ref.shape[0] // FF_CH)]
    return ref[...]


def tile_forward(fn, name, t_all, ins, out_defs, tile=TILE):
    keys = [k for k, _, _, _ in ins]
    kinds = [kd for _, kd, _, _ in ins]
    diffs = [df for _, _, df, _ in ins]
    arrs = [a for _, _, _, a in ins]
    n_in = len(ins)

    def body(*refs):
        vals = [_load(kinds[j], refs[j]) for j in range(n_in)]
        d = {keys[j]: vals[j] for j in range(n_in) if diffs[j]}
        c = {keys[j]: vals[j] for j in range(n_in) if not diffs[j]}
        outs, _ = fn(d, c, None)
        for j, (k, _, dt) in enumerate(out_defs):
            refs[n_in + j][...] = outs[k].astype(dt)

    res = pl.pallas_call(
        body, name=name, grid=(t_all // tile,),
        in_specs=[_in_spec(kinds[j], arrs[j], tile) for j in range(n_in)],
        out_specs=[pl.BlockSpec((tile, w), lambda i: (i, 0)) for _, w, _ in out_defs],
        out_shape=[jax.ShapeDtypeStruct((t_all, w), dt) for _, w, dt in out_defs],
        compiler_params=_cparams(1),
    )(*arrs)
    return {k: r for (k, _, _), r in zip(out_defs, res)}


def tile_backward(fn, name, t_all, ins, cots, z_defs, aux_defs, tile=TILE, resid=None):
    keys = [k for k, _, _, _ in ins]
    kinds = [kd for _, kd, _, _ in ins]
    diffs = [df for _, _, df, _ in ins]
    arrs = [a for _, _, _, a in ins]
    cot_keys, cot_arrs = [], []
    for k, a in cots:
        for one in (a if isinstance(a, (list, tuple)) else [a]):
            cot_keys.append(k)
            cot_arrs.append(one)
    if resid is not None:
        cot_keys.append("resid:" + resid[0])
        cot_arrs.append(resid[1])
    n_in, n_cot = len(ins), len(cot_arrs)
    dkeys = [j for j in range(n_in) if diffs[j]]
    ctx_tiles = TILE // tile

    def body(*refs):
        i = pl.program_id(0)
        vals = [_load(kinds[j], refs[j]) for j in range(n_in)]
        d = {keys[j]: vals[j] for j in dkeys}
        c = {keys[j]: vals[j] for j in range(n_in) if not diffs[j]}
        zs = {k: jnp.zeros((tile, w), F32) for k, w in z_defs}
        outs, vjp_fn, aux = jax.vjp(lambda dd, zz: fn(dd, c, zz), d, zs, has_aux=True)
        ct = {}
        for j, k in enumerate(cot_keys):
            ct[k] = refs[n_in + j][...] + ct[k] if k in ct else refs[n_in + j][...]
        dd, dz = vjp_fn({k: ct[k].astype(outs[k].dtype) for k in outs})
        base = n_in + n_cot
        for n, j in enumerate(dkeys):
            ref, g = refs[base + n], dd[keys[j]]
            if kinds[j] == "tile":
                ref[...] = g + ct["resid:" + keys[j]] if "resid:" + keys[j] in ct else g
            else:
                first = ((i == 0) | (i == ctx_tiles)) if kinds[j] == "kind" else (i == 0)
                gv = g[None] if kinds[j] == "kind" else g

                @pl.when(first)
                def _(ref=ref, gv=gv):
                    ref[...] = gv

                @pl.when(jnp.logical_not(first))
                def _(ref=ref, gv=gv):
                    ref[...] += gv
        base += len(dkeys)
        for n, (k, _) in enumerate(z_defs):
            refs[base + n][...] = dz[k].astype(BF)
        base += len(z_defs)
        for n, (k, _) in enumerate(aux_defs):
            refs[base + n][...] = aux[k].T.astype(BF)

    out_specs, out_shape = [], []
    for j in dkeys:
        out_specs.append(_in_spec(kinds[j], arrs[j], tile))
        out_shape.append(jax.ShapeDtypeStruct(arrs[j].shape, F32))
    for _, w in z_defs:
        out_specs.append(pl.BlockSpec((tile, w), lambda i: (i, 0)))
        out_shape.append(jax.ShapeDtypeStruct((t_all, w), BF))
    for _, w in aux_defs:
        out_specs.append(pl.BlockSpec((w, tile), lambda i: (0, i)))
        out_shape.append(jax.ShapeDtypeStruct((w, t_all), BF))
    res = pl.pallas_call(
        body, name=name, grid=(t_all // tile,),
        in_specs=[_in_spec(kinds[j], arrs[j], tile) for j in range(n_in)]
        + [pl.BlockSpec((tile, a.shape[1]), lambda i: (i, 0)) for a in cot_arrs],
        out_specs=out_specs, out_shape=out_shape, compiler_params=_cparams(1),
    )(*arrs, *cot_arrs)
    grads = {keys[j]: res[n] for n, j in enumerate(dkeys)}
    extra = {k: res[len(dkeys) + n] for n, (k, _) in enumerate(list(z_defs) + list(aux_defs))}
    return grads, extra


def ffn_backward(x1, modl, n2w, w1, w2, dx2, f, name):
    t_all = x1.shape[0]
    n_ch = D_FF // FF_CH

    def head(x, mod, nw):
        return _rms(x, nw) * (1.0 + mod[4:5]) + mod[3:4]

    def body(x_ref, mod_ref, nw_ref, w1_ref, w2_ref, dx2_ref, f_ref, dx1_ref, dmod_ref, dnw_ref, zpre_ref, zf_ref, a1_ref,
             a2_ref):
        i = pl.program_id(0)
        mod = mod_ref[0]
        dx2 = dx2_ref[...]
        h2, vjp_head = jax.vjp(head, x_ref[...], mod, nw_ref[...])
        h2b = h2.astype(BF)
        dfb = (dx2 * mod[5:6]).astype(BF)
        f = f_ref[...]
        dh2 = jnp.zeros((TILE, D), F32)
        for j in range(n_ch):
            cs = slice(j * FF_CH, (j + 1) * FF_CH)
            a = jnp.maximum(_dot(h2b, w1_ref[:, cs]), 0.0)
            a2b = (a * a).astype(BF)
            dpre = (_dot_nt(dfb, w2_ref[cs, :]) * (2.0 * a)).astype(BF)
            dh2 = dh2 + _dot_nt(dpre, w1_ref[:, cs])
            zpre_ref[:, cs] = dpre
            a2_ref[:, cs] = a2b
        zf_ref[...] = (dx2 * mod[5:6]).T.astype(BF)
        a1_ref[...] = h2.T.astype(BF)
        dx1, dmod, dnw = vjp_head(dh2)
        dx1_ref[...] = dx2 + dx1
        row = lax.broadcasted_iota(jnp.int32, (8, D), 0)
        dmod = dmod + jnp.where(row == 5, jnp.sum(dx2 * f, axis=0, keepdims=True), 0.0)
        first_kind = (i == 0) | (i == 1)

        @pl.when(first_kind)
        def _():
            dmod_ref[0] = dmod

        @pl.when(jnp.logical_not(first_kind))
        def _():
            dmod_ref[0] += dmod

        @pl.when(i == 0)
        def _():
            dnw_ref[...] = dnw

        @pl.when(i > 0)
        def _():
            dnw_ref[...] += dnw

    tspec = lambda w: pl.BlockSpec((TILE, w), lambda i: (i, 0))
    once = lambda shp: pl.BlockSpec(shp, lambda i: (0, 0), pipeline_mode=pl.Buffered(1))
    kind = pl.BlockSpec((1, 8, D), lambda i: (jnp.minimum(i, 1), 0, 0))
    tr = pl.BlockSpec((D, TILE), lambda i: (0, i))
    return pl.pallas_call(
        body, name=name, grid=(t_all // TILE,),
        in_specs=[tspec(D), kind, pl.BlockSpec((1, D), lambda i: (0, 0)), once((D, D_FF)), once((D_FF, D)), tspec(D),
                  tspec(D)],
        out_specs=[tspec(D), kind, pl.BlockSpec((1, D), lambda i: (0, 0)), tspec(D_FF), tr, tr, tspec(D_FF)],
        out_shape=[jax.ShapeDtypeStruct((t_all, D), F32), jax.ShapeDtypeStruct((2, 8, D), F32),
                   jax.ShapeDtypeStruct((1, D), F32), jax.ShapeDtypeStruct((t_all, D_FF), BF),
                   jax.ShapeDtypeStruct((D, t_all), BF), jax.ShapeDtypeStruct((D, t_all), BF),
                   jax.ShapeDtypeStruct((t_all, D_FF), BF)],
        compiler_params=_cparams(1),
    )(x1, modl, n2w, w1, w2, dx2, f)


WG_TOK = 768


def wgrad(at, b, name, bk2=1024):
    k1, t = at.shape
    k2 = b.shape[1]
    bk2 = min(bk2, k2)
    tt = WG_TOK if t % WG_TOK == 0 else TILE
    nt_ = t // tt

    def body(a_ref, b_ref, o_ref, acc):
        s = pl.program_id(1)

        @pl.when(s == 0)
        def _():
            acc[...] = jnp.zeros_like(acc)

        acc[...] += _dot(a_ref[...], b_ref[...])

        @pl.when(s == nt_ - 1)
        def _():
            o_ref[...] = acc[...]

    return pl.pallas_call(
        body, name=name, grid=(k2 // bk2, nt_),
        in_specs=[pl.BlockSpec((k1, tt), lambda j, s: (0, s)), pl.BlockSpec((tt, bk2), lambda j, s: (s, j))],
        out_specs=pl.BlockSpec((k1, bk2), lambda j, s: (0, j)),
        out_shape=jax.ShapeDtypeStruct((k1, k2), F32),
        scratch_shapes=[pltpu.VMEM((k1, bk2), F32)],
        compiler_params=_cparams(2),
    )(at, b)


def _gla_consts(reverse, diff):
    r = lax.broadcasted_iota(jnp.int32, (GCH, GCH), 0)
    cc = lax.broadcasted_iota(jnp.int32, (GCH, GCH), 1)
    low = (r >= cc)
    tri = (jnp.logical_not(low) | (r == cc)) if reverse else low
    tri_f = tri.astype(F32)
    tri_t = (cc >= r) if not reverse else (cc <= r)
    hmk = [_lane_group_mask(128, 32, h) for h in range(HEADS)]
    hmv = [_lane_group_mask(256, 64, h) for h in range(HEADS)]
    e = lax.broadcasted_iota(jnp.int32, (256, 128), 0) // 64
    dk = lax.broadcasted_iota(jnp.int32, (256, 128), 1) // 32
    return dict(cum=make_cum(tri_f.astype(BF), tri_t.astype(F32).astype(BF), diff), ops=_ops(diff), reverse=reverse,
                tri4=jnp.concatenate([tri_f] * HEADS, axis=0), hmk=hmk, hmv=hmv, bd=(e == dk).astype(F32))


def gla_chunk(st, q, k, v, g, cs):
    nn, nt, tn = cs["ops"]
    b = cs["cum"](g)
    bl = jnp.sum(g, axis=0, keepdims=True)
    b_ref = jnp.sum(g[GCH // 2:] if cs["reverse"] else g[:GCH // 2], axis=0, keepdims=True)
    qe = q * jnp.exp(b)
    qs = q * jnp.exp(b - b_ref)
    ks = k * jnp.exp(b_ref - b)
    qstack = jnp.concatenate([qs * cs["hmk"][h] for h in range(HEADS)], axis=0)
    att = nt(qstack, ks) * cs["tri4"]
    ofull = nn(att, v)
    o = nt(qe, st)
    for h in range(HEADS):
        o = o + ofull[h * GCH:(h + 1) * GCH] * cs["hmv"][h]
    kd = k * jnp.exp(bl - b)
    st_new = st * jnp.exp(bl) + tn(v, kd) * cs["bd"]
    return st_new, o


def _gla_chunk_index(s, n_ch, reverse):
    ctx_ch = TILE // GCH
    if not reverse:
        return s
    return jnp.where(s < ctx_ch, ctx_ch - 1 - s, n_ch - 1 + ctx_ch - s)


def gla_forward(q, k, v, gf, gb, name):
    t = q.shape[0]
    n_ch = t // GCH

    def body(*refs):
        s = pl.program_id(0)
        for dr, reverse in enumerate((False, True)):
            q_ref, k_ref, v_ref, g_ref = refs[4 * dr:4 * dr + 4]
            o_ref, sst_ref = refs[8 + 2 * dr:8 + 2 * dr + 2]
            st = refs[12 + dr]

            @pl.when(s == 0)
            def _(st=st):
                st[...] = jnp.zeros_like(st)

            cur = st[...]
            sst_ref[0] = cur
            st_new, o = gla_chunk(cur, q_ref[...], k_ref[...], v_ref[...], g_ref[...], _gla_consts(reverse, False))
            o_ref[...] = o
            st[...] = st_new

    in_specs, out_specs, out_shape = [], [], []
    for reverse in (False, True):
        im = lambda s, reverse=reverse: (_gla_chunk_index(s, n_ch, reverse), 0)
        im3 = lambda s, reverse=reverse: (_gla_chunk_index(s, n_ch, reverse), 0, 0)
        in_specs += [pl.BlockSpec((GCH, 128), im), pl.BlockSpec((GCH, 128), im), pl.BlockSpec((GCH, 256), im),
                     pl.BlockSpec((GCH, 128), im)]
        out_specs += [pl.BlockSpec((GCH, 256), im), pl.BlockSpec((1, 256, 128), im3)]
        out_shape += [jax.ShapeDtypeStruct((t, 256), F32), jax.ShapeDtypeStruct((n_ch, 256, 128), F32)]
    return pl.pallas_call(
        body, name=name, grid=(n_ch,), in_specs=in_specs, out_specs=out_specs, out_shape=out_shape,
        scratch_shapes=[pltpu.VMEM((256, 128), F32), pltpu.VMEM((256, 128), F32)],
        compiler_params=_cparams(1),
    )(q, k, v, gf, q, k, v, gb)


def gla_backward(q, k, v, gf, gb, sst_f, sst_b, do, name):
    t = q.shape[0]
    n_ch = t // GCH

    def body(*refs):
        r = pl.program_id(0)
        for dr, reverse in enumerate((False, True)):
            q_ref, k_ref, v_ref, g_ref, sst_ref, do_ref = refs[6 * dr:6 * dr + 6]
            outs = refs[12 + 4 * dr:12 + 4 * dr + 4]
            dst = refs[20 + dr]

            @pl.when(r == 0)
            def _(dst=dst):
                dst[...] = jnp.zeros_like(dst)

            cs = _gla_consts(reverse, True)
            _, vjp_fn = jax.vjp(lambda a, b, c_, d_, e_, cs=cs: gla_chunk(a, b, c_, d_, e_, cs),
                                sst_ref[0], q_ref[...], k_ref[...], v_ref[...], g_ref[...])
            grads = vjp_fn((dst[...], do_ref[...]))
            for o_ref, gval in zip(outs, grads[1:]):
                o_ref[...] = gval
            dst[...] = grads[0]

    in_specs, out_specs, out_shape = [], [], []
    for reverse in (False, True):
        im = lambda r, reverse=reverse: (_gla_chunk_index(n_ch - 1 - r, n_ch, reverse), 0)
        im3 = lambda r, reverse=reverse: (_gla_chunk_index(n_ch - 1 - r, n_ch, reverse), 0, 0)
        in_specs += [pl.BlockSpec((GCH, 128), im), pl.BlockSpec((GCH, 128), im), pl.BlockSpec((GCH, 256), im),
                     pl.BlockSpec((GCH, 128), im), pl.BlockSpec((1, 256, 128), im3), pl.BlockSpec((GCH, 256), im)]
        out_specs += [pl.BlockSpec((GCH, 128), im), pl.BlockSpec((GCH, 128), im), pl.BlockSpec((GCH, 256), im),
                      pl.BlockSpec((GCH, 128), im)]
        out_shape += [jax.ShapeDtypeStruct((t, 128), F32), jax.ShapeDtypeStruct((t, 128), F32),
                      jax.ShapeDtypeStruct((t, 256), F32), jax.ShapeDtypeStruct((t, 128), F32)]
    return pl.pallas_call(
        body, name=name, grid=(n_ch,), in_specs=in_specs, out_specs=out_specs, out_shape=out_shape,
        scratch_shapes=[pltpu.VMEM((256, 128), F32), pltpu.VMEM((256, 128), F32)],
        compiler_params=_cparams(1),
    )(q, k, v, gf, sst_f, do, q, k, v, gb, sst_b, do)


def _resident(hbm_ref, vmem_ref, sem):
    cp = pltpu.make_async_copy(hbm_ref, vmem_ref, sem)
    cp.start()
    cp.wait()


def mla_forward(q_cat, k_cat, v, name):
    t = q_cat.shape[0]
    qt = QT_FWD if t % QT_FWD == 0 else TILE
    n_t = t // qt

    ch = KV_CH_FWD if (t - TILE) % KV_CH_FWD == 0 else KV_CH
    n_main = (t - TILE) // ch

    def body(q_ref, k_hbm, v_hbm, o_ref, lse_ref, k_s, v_s, m_s, l_s, acc_s, sem):
        i = pl.program_id(0)

        @pl.when(i == 0)
        def _():
            _resident(k_hbm, k_s, sem.at[0])
            _resident(v_hbm, v_s, sem.at[1])

        m_s[...] = jnp.full(m_s.shape, -1e30, F32)
        l_s[...] = jnp.zeros_like(l_s)
        acc_s[...] = jnp.zeros_like(acc_s)

        def chunk(r0, size, hide_ctx_rows=False):
            for h in range(HEADS):
                kh = k_s[pl.ds(r0, size), h * 256:(h + 1) * 256]
                vh = v_s[pl.ds(r0, size), h * 128:(h + 1) * 128]
                s = _dot_nt(q_ref[:, h * 256:(h + 1) * 256], kh)
                if hide_ctx_rows:
                    s = jnp.where(lax.broadcasted_iota(jnp.int32, (qt, 1), 0) < TILE, -1e30, s)
                m_prev = m_s[h]
                m_next = jnp.maximum(m_prev, jnp.max(s, axis=-1, keepdims=True))
                p = jnp.exp2(s - jnp.tile(m_next, (1, size // 128)))
                alpha = jnp.exp2(m_prev - m_next)
                l_s[h] = alpha * l_s[h] + jnp.sum(p, axis=-1, keepdims=True)
                acc_s[h] = alpha * acc_s[h] + _dot(p.astype(BF), vh)
                m_s[h] = m_next

        chunk(0, TILE)

        def main_loop(hide, size, unroll):
            def step(c, carry):
                chunk(pl.multiple_of(TILE + c * size, TILE), size, hide)
                return carry

            lax.fori_loop(0, (t - TILE) // size, step, 0, unroll=unroll)

        if qt > TILE:
            pl.when(i == 0)(lambda: main_loop(True, KV_CH, 1))
        pl.when(i >= 1)(lambda: main_loop(False, ch, MLA_UNROLL))

        lane = lax.broadcasted_iota(jnp.int32, (qt, 128), 1)
        cols = jnp.zeros((qt, 128), F32)
        for h in range(HEADS):
            o_ref[:, h * 128:(h + 1) * 128] = acc_s[h] / l_s[h]
            cols = jnp.where(lane == h, m_s[h] + jnp.log2(l_s[h]), cols)
        lse_ref[...] = cols.T[0:8, :]

    return pl.pallas_call(
        body, name=name, grid=(n_t,),
        in_specs=[pl.BlockSpec((qt, 1024), lambda i: (i, 0)), pl.BlockSpec(memory_space=pl.ANY),
                  pl.BlockSpec(memory_space=pl.ANY)],
        out_specs=[pl.BlockSpec((qt, 512), lambda i: (i, 0)), pl.BlockSpec((8, qt), lambda i: (0, i))],
        out_shape=[jax.ShapeDtypeStruct((t, 512), F32), jax.ShapeDtypeStruct((8, t), F32)],
        scratch_shapes=[pltpu.VMEM((t, 1024), BF), pltpu.VMEM((t, 512), BF), pltpu.VMEM((HEADS, qt, 128), F32),
                        pltpu.VMEM((HEADS, qt, 128), F32), pltpu.VMEM((HEADS, qt, 128), F32),
                        pltpu.SemaphoreType.DMA((2,))],
        compiler_params=_cparams(1),
    )(q_cat, k_cat, v)


def mla_delta(do, o, name):
    t = do.shape[0]

    def body(do_ref, o_ref, dl_ref, dob_ref):
        d = do_ref[...]
        prod = d * o_ref[...]
        rows = [jnp.sum(prod[:, h * 128:(h + 1) * 128], axis=-1, keepdims=True) for h in range(HEADS)]
        cols = jnp.concatenate(rows + [jnp.zeros((TILE, 128 - HEADS), F32)], axis=-1)
        dl_ref[...] = cols.T[0:8, :]
        dob_ref[...] = d.astype(BF)

    return pl.pallas_call(
        body, name=name, grid=(t // TILE,),
        in_specs=[pl.BlockSpec((TILE, 512), lambda i: (i, 0)), pl.BlockSpec((TILE, 512), lambda i: (i, 0))],
        out_specs=[pl.BlockSpec((8, TILE), lambda i: (0, i)), pl.BlockSpec((TILE, 512), lambda i: (i, 0))],
        out_shape=[jax.ShapeDtypeStruct((8, t), F32), jax.ShapeDtypeStruct((t, 512), BF)],
        compiler_params=_cparams(1),
    )(do, o)


def mla_backward(q_cat, k_cat, v, lse_rows, dl_rows, do_bf, name):
    t = q_cat.shape[0]
    kt = KT_BWD if t % KT_BWD == 0 else TILE
    n_t = t // kt
    ch = Q_CH_BWD if (t - TILE) % Q_CH_BWD == 0 else KV_CH
    n_main = (t - TILE) // ch

    def body(q_hbm, do_hbm, k_ref, v_ref, lse_ref, dl_ref, dq_ref, dk_ref, dv_ref, q_s, do_s, dk_s, dv_s, sem):
        h, j = pl.program_id(0), pl.program_id(1)

        @pl.when(j == 0)
        def _():
            _resident(q_hbm.at[:, pl.ds(pl.multiple_of(h * 256, 256), 256)], q_s, sem.at[0])
            _resident(do_hbm.at[:, pl.ds(pl.multiple_of(h * 128, 128), 128)], do_s, sem.at[1])
            dq_ref[...] = jnp.zeros_like(dq_ref)

        dk_s[...] = jnp.zeros_like(dk_s)
        dv_s[...] = jnp.zeros_like(dv_s)
        kh = k_ref[...]
        vh = v_ref[...]

        def chunk(r0, size, ctx_only=False):
            qh = q_s[pl.ds(r0, size), :]
            doh = do_s[pl.ds(r0, size), :]
            pt = jnp.exp2(_dot_nt(kh, qh) - lse_ref[pl.ds(h, 1), pl.ds(r0, size)])
            if ctx_only and kt > TILE:
                pt = jnp.where(lax.broadcasted_iota(jnp.int32, (kt, 1), 0) < TILE, pt, 0.0)
            dst = (pt * (_dot_nt(vh, doh) - dl_ref[pl.ds(h, 1), pl.ds(r0, size)])).astype(BF)
            dv_s[...] += _dot(pt.astype(BF), doh)
            dk_s[...] += _dot(dst, qh)
            dq_ref[pl.ds(r0, size), :] += _dot_tn(dst, kh)

        @pl.when(j == 0)
        def _():
            chunk(0, TILE, ctx_only=True)

        def step(c, carry):
            chunk(pl.multiple_of(TILE + c * ch, TILE), ch)
            return carry

        lax.fori_loop(0, n_main, step, 0, unroll=MLA_UNROLL)
        dk_ref[...] = dk_s[...] * LN2
        dv_ref[...] = dv_s[...]

        @pl.when(j == n_t - 1)
        def _():
            dq_ref[...] = dq_ref[...] * LN2

    rows = pl.BlockSpec((8, t), lambda h, j: (0, 0))
    hbm = pl.BlockSpec(memory_space=pl.ANY)
    return pl.pallas_call(
        body, name=name, grid=(HEADS, n_t),
        in_specs=[hbm, hbm, pl.BlockSpec((kt, 256), lambda h, j: (j, h)), pl.BlockSpec((kt, 128), lambda h, j: (j, h)),
                  rows, rows],
        out_specs=[pl.BlockSpec((t, 256), lambda h, j: (0, h)), pl.BlockSpec((kt, 256), lambda h, j: (j, h)),
                   pl.BlockSpec((kt, 128), lambda h, j: (j, h))],
        out_shape=[jax.ShapeDtypeStruct((t, 1024), F32), jax.ShapeDtypeStruct((t, 1024), F32),
                   jax.ShapeDtypeStruct((t, 512), F32)],
        scratch_shapes=[pltpu.VMEM((t, 256), BF), pltpu.VMEM((t, 128), BF), pltpu.VMEM((kt, 256), F32),
                        pltpu.VMEM((kt, 128), F32), pltpu.SemaphoreType.DMA((2,))],
        compiler_params=_cparams(2),
    )(q_cat, do_bf, k_cat, v, lse_rows, dl_rows)


def final_loss(xf, target, fnw, name):
    t = xf.shape[0]
    n_t = t // TILE

    def body(x_ref, t_ref, w_ref, loss_ref, dx_ref, dw_ref):
        i = pl.program_id(0)

        @pl.when(i == 0)
        def _():
            loss_ref[...] = jnp.zeros_like(loss_ref)
            dw_ref[...] = jnp.zeros_like(dw_ref)
            dx_ref[...] = jnp.zeros_like(dx_ref)

        @pl.when(i >= 1)
        def _():
            y, vjp_fn = jax.vjp(_rms, x_ref[...], w_ref[...])
            err = y - t_ref[...]
            loss_ref[...] += jnp.broadcast_to(0.5 * jnp.sum(jnp.mean(err * err, axis=-1, keepdims=True)), (8, 128))
            dx, dw = vjp_fn(err * (1.0 / D))
            dx_ref[...] = dx
            dw_ref[...] += dw

    return pl.pallas_call(
        body, name=name, grid=(n_t,),
        in_specs=[pl.BlockSpec((TILE, D), lambda i: (i, 0)), pl.BlockSpec((TILE, D), lambda i: (jnp.maximum(i - 1, 0), 0)),
                  pl.BlockSpec((1, D), lambda i: (0, 0))],
        out_specs=[pl.BlockSpec((8, 128), lambda i: (0, 0)), pl.BlockSpec((TILE, D), lambda i: (i, 0)),
                   pl.BlockSpec((1, D), lambda i: (0, 0))],
        out_shape=[jax.ShapeDtypeStruct((8, 128), F32), jax.ShapeDtypeStruct((t, D), F32),
                   jax.ShapeDtypeStruct((1, D), F32)],
        compiler_params=_cparams(1),
    )(xf, target, fnw)


def all_gather(xs, name):
    n = len(xs)
    blks = [tuple(x.shape) for x in xs]
    per = N_DEV - 1

    def body(*refs):
        x_refs, o_refs = refs[:n], refs[n:2 * n]
        ssem, rsem, lsem = refs[2 * n:]
        xi, yi, ci = lax.axis_index("x"), lax.axis_index("y"), lax.axis_index("c")
        me3 = (xi, yi, ci)
        me = 4 * xi + 2 * yi + ci
        flat = lambda d: 4 * d[0] + 2 * d[1] + d[2]
        sibling = (xi, yi, 1 - ci)
        chips = [(1 - xi, yi), (xi, 1 - yi), (1 - xi, 1 - yi)]

        def copy(a, k, block, to, src=None):
            rows = o_refs[a].at[flat(block)]
            return pltpu.make_async_remote_copy(
                src_ref=rows if src is None else src, dst_ref=rows,
                send_sem=ssem.at[a * per + k], recv_sem=rsem.at[a * per + k],
                device_id=to, device_id_type=pl.DeviceIdType.MESH)

        own = [pltpu.make_async_copy(x_refs[a], o_refs[a].at[me], lsem.at[a]) for a in range(n)]
        first = []
        for a in range(n):
            first.append(copy(a, 0, me3, sibling, src=x_refs[a]))
            first += [copy(a, 1 + j, me3, (*chip, ci), src=x_refs[a]) for j, chip in enumerate(chips)]
        for cp in own + first:
            cp.start()
        passed = []
        for j, chip in enumerate(chips):
            for a in range(n):
                copy(a, 1 + j, (*chip, ci), me3).wait_recv()
                fw = copy(a, 4 + j, (*chip, ci), sibling)
                fw.start()
                passed.append(fw)
        for a in range(n):
            copy(a, 0, sibling, me3).wait_recv()
            for j, chip in enumerate(chips):
                copy(a, 4 + j, (*chip, 1 - ci), me3).wait_recv()
        for cp in first + passed:
            cp.wait_send()
        for cp in own:
            cp.wait()

    hbm = pl.BlockSpec(memory_space=pl.ANY)
    res = pl.pallas_call(
        body, name=name, in_specs=[hbm] * n, out_specs=[hbm] * n,
        out_shape=[jax.ShapeDtypeStruct((N_DEV,) + blks[a], xs[a].dtype) for a in range(n)],
        scratch_shapes=[pltpu.SemaphoreType.DMA((n * per,)), pltpu.SemaphoreType.DMA((n * per,)),
                        pltpu.SemaphoreType.DMA((n,))],
        compiler_params=pltpu.CompilerParams(has_side_effects=True),
    )(*xs)
    return list(res)


def reduce_scatter(xs, name):
    n = len(xs)
    c_idx = lax.axis_index("c").astype(jnp.int32).reshape(1)
    hbm = pl.BlockSpec(memory_space=pl.ANY)
    side = pltpu.CompilerParams(has_side_effects=True)

    def pair_body(*refs):
        x_refs, o_refs = refs[:n], refs[n:2 * n]
        ssem, rsem = refs[2 * n:]
        xi, yi, cc = lax.axis_index("x"), lax.axis_index("y"), lax.axis_index("c")
        cps = [pltpu.make_async_remote_copy(
            src_ref=x_refs[a].at[1 - cc], dst_ref=o_refs[a], send_sem=ssem.at[a], recv_sem=rsem.at[a],
            device_id=(xi, yi, 1 - cc), device_id_type=pl.DeviceIdType.MESH) for a in range(n)]
        for cp in cps:
            cp.start()
        for cp in cps:
            cp.wait()

    halves = [tuple(x.shape[1:]) for x in xs]
    got = pl.pallas_call(
        pair_body, name=name + "_pair", in_specs=[hbm] * n, out_specs=[hbm] * n,
        out_shape=[jax.ShapeDtypeStruct(halves[a], xs[a].dtype) for a in range(n)],
        scratch_shapes=[pltpu.SemaphoreType.DMA((n,)), pltpu.SemaphoreType.DMA((n,))], compiler_params=side,
    )(*xs)

    sums = []
    for a in range(n):
        def add_body(c_ref, a_ref, b_ref, o_ref):
            o_ref[...] = (a_ref[0].astype(F32) + b_ref[...].astype(F32)).astype(o_ref.dtype)

        blk = pl.BlockSpec((1,) + halves[a][1:], lambda q, c_ref: (q, 0, 0))
        own = pl.BlockSpec((1, 1) + halves[a][1:], lambda q, c_ref: (c_ref[0], q, 0, 0))
        sums.append(pl.pallas_call(
            add_body, name=f"{name}_sum{a}",
            grid_spec=pltpu.PrefetchScalarGridSpec(num_scalar_prefetch=1, grid=(4,), in_specs=[own, blk], out_specs=blk),
            out_shape=jax.ShapeDtypeStruct(halves[a], xs[a].dtype), compiler_params=_cparams(1),
        )(c_idx, xs[a], got[a]))

    def chips_body(*refs):
        s_refs, o_refs = refs[:n], refs[n:2 * n]
        ssem, rsem, lsem = refs[2 * n:]
        xi, yi, cc = lax.axis_index("x"), lax.axis_index("y"), lax.axis_index("c")
        my_chip = 2 * xi + yi
        chips = [(1 - xi, yi), (xi, 1 - yi), (1 - xi, 1 - yi)]
        own = [pltpu.make_async_copy(s_refs[a].at[my_chip], o_refs[a].at[my_chip], lsem.at[a]) for a in range(n)]
        sends, lands = [], []
        for a in range(n):
            for j, (px, py) in enumerate(chips):
                common = dict(send_sem=ssem.at[3 * a + j], recv_sem=rsem.at[3 * a + j], device_id=(px, py, cc),
                              device_id_type=pl.DeviceIdType.MESH)
                sends.append(pltpu.make_async_remote_copy(src_ref=s_refs[a].at[2 * px + py],
                                                          dst_ref=o_refs[a].at[my_chip], **common))
                lands.append(pltpu.make_async_remote_copy(src_ref=s_refs[a].at[2 * px + py],
                                                          dst_ref=o_refs[a].at[2 * px + py], **common))
        for cp in own + sends:
            cp.start()
        for cp in lands:
            cp.wait_recv()
        for cp in sends:
            cp.wait_send()
        for cp in own:
            cp.wait()

    res = pl.pallas_call(
        chips_body, name=name + "_chips", in_specs=[hbm] * n, out_specs=[hbm] * n,
        out_shape=[jax.ShapeDtypeStruct(halves[a], xs[a].dtype) for a in range(n)],
        scratch_shapes=[pltpu.SemaphoreType.DMA((3 * n,)), pltpu.SemaphoreType.DMA((3 * n,)),
                        pltpu.SemaphoreType.DMA((n,))], compiler_params=side,
    )(*sums)
    return list(res)


def mod_forward(crows, w_mod, b_shard, name):
    cols = w_mod.shape[2]

    def body(c_ref, w_ref, b_ref, o_ref):
        o_ref[0] = _dot(_silu(c_ref[...]).astype(BF), w_ref[0].astype(BF)) + b_ref[0]

    return pl.pallas_call(
        body, name=name, grid=(2,),
        in_specs=[pl.BlockSpec((16, D), lambda l: (0, 0)), pl.BlockSpec((1, D, cols), lambda l: (l, 0, 0)),
                  pl.BlockSpec((1, 1, cols), lambda l: (l, 0, 0))],
        out_specs=pl.BlockSpec((1, 16, cols), lambda l: (l, 0, 0)),
        out_shape=jax.ShapeDtypeStruct((2, 16, cols), F32), compiler_params=_cparams(1),
    )(crows, w_mod, b_shard)


def mod_backward(crows, w_mod, d_own, d_ctx, name):
    cols = w_mod.shape[2]

    def body(c_ref, w_ref, do_ref, dc_ref, gw_ref, gs_ref):
        dc = dc_ref[0]
        dsum = dc[0:1]
        for s in range(1, N_DEV):
            dsum = dsum + dc[s:s + 1]
        row = lax.broadcasted_iota(jnp.int32, (8, cols), 0)
        d16 = jnp.concatenate([do_ref[0], jnp.where(row == 0, jnp.broadcast_to(dsum, (8, cols)), 0.0)], axis=0)
        gw_ref[0] = _dot_tn(_silu(c_ref[...]).astype(BF), d16.astype(BF))
        gs_ref[0] = _dot_nt(jnp.broadcast_to(dsum, (8, cols)).astype(BF), w_ref[0].astype(BF))

    return pl.pallas_call(
        body, name=name, grid=(2,),
        in_specs=[pl.BlockSpec((16, D), lambda l: (0, 0)), pl.BlockSpec((1, D, cols), lambda l: (l, 0, 0)),
                  pl.BlockSpec((1, 8, cols), lambda l: (l, 0, 0)), pl.BlockSpec((1, 8, cols), lambda l: (l, 0, 0))],
        out_specs=[pl.BlockSpec((1, D, cols), lambda l: (l, 0, 0)), pl.BlockSpec((1, 8, D), lambda l: (l, 0, 0))],
        out_shape=[jax.ShapeDtypeStruct((2, D, cols), F32), jax.ShapeDtypeStruct((2, 8, D), F32)],
        compiler_params=_cparams(1),
    )(crows, w_mod, d_own, d_ctx)


def silu_grad_scale(c_ctx, ds, name):
    def body(c_ref, ds_ref, o_ref):
        cc = c_ref[...]
        sg = jax.nn.sigmoid(cc)
        o_ref[...] = (ds_ref[0][0:1] + ds_ref[1][0:1]) * (sg * (1.0 + cc * (1.0 - sg)))

    return pl.pallas_call(body, name=name, out_shape=jax.ShapeDtypeStruct((1, D), F32))(c_ctx, ds)


def _adamw_math(p_ref, w_ref, m_ref, v_ref, g_ref, d_ref, nm_ref, nv_ref):
    g = p_ref[0].astype(F32)
    for s in range(1, p_ref.shape[0]):
        g = g + p_ref[s].astype(F32)
    mm = ADAM_B1 * m_ref[...] + (1.0 - ADAM_B1) * g
    vv = ADAM_B2 * v_ref[...] + (1.0 - ADAM_B2) * (g * g)
    m_hat = mm / (1.0 - ADAM_B1 ** ADAM_STEP)
    v_hat = vv / (1.0 - ADAM_B2 ** ADAM_STEP)
    g_ref[...] = g
    d_ref[...] = -ADAM_LR * (m_hat / (jnp.sqrt(v_hat) + ADAM_EPS) + ADAM_WD * w_ref[...])
    nm_ref[...] = mm
    nv_ref[...] = vv


def adamw(parts, w, m, v, name):
    n_parts, rows, cols = parts.shape
    lanes = -(-cols // 128) * 128
    block_rows = min(rows, 1 << ((ADAMW_BLOCK_ELEMS // lanes).bit_length() - 1))
    assert rows % block_rows == 0

    def body(*refs):
        _adamw_math(*refs)

    spec = pl.BlockSpec((block_rows, cols), lambda i: (i, 0))
    return pl.pallas_call(
        body, name=name, grid=(rows // block_rows,),
        in_specs=[pl.BlockSpec((n_parts, block_rows, cols), lambda i: (0, i, 0)), spec, spec, spec],
        out_specs=[spec] * 4, out_shape=[jax.ShapeDtypeStruct((rows, cols), F32)] * 4,
        compiler_params=_cparams(1),
    )(parts, w, m, v)


def adamw_group(items, name):
    n = len(items)

    def body(*refs):
        for a in range(n):
            _adamw_math(*refs[4 * a:4 * a + 4], *refs[4 * n + 4 * a:4 * n + 4 * a + 4])

    flat_in = [x for it in items for x in it]
    out_shape = [jax.ShapeDtypeStruct(it[1].shape, F32) for it in items for _ in range(4)]
    res = pl.pallas_call(body, name=name, out_shape=out_shape,
                         compiler_params=pltpu.CompilerParams(vmem_limit_bytes=VMEM_LIMIT_MB * 1024 * 1024))(*flat_in)
    return [tuple(res[4 * a:4 * a + 4]) for a in range(n)]


def _pad_cols(w, segs, total):
    parts, pos = [], 0
    for dst, src, wd in segs:
        if dst > pos:
            parts.append(jnp.zeros(w.shape[:-1] + (dst - pos,), w.dtype))
        parts.append(w[..., src:src + wd])
        pos = dst + wd
    if pos < total:
        parts.append(jnp.zeros(w.shape[:-1] + (total - pos,), w.dtype))
    return jnp.concatenate(parts, axis=-1)


def _unpad_cols(g, segs):
    return jnp.concatenate([g[..., dst:dst + wd] for dst, _, wd in segs], axis=-1)


def _rope_tables(n_lat):
    rows = n_lat // GRID_W
    freq = ROPE_BASE ** (-jnp.arange(16, dtype=F32) * 2.0 / 32)
    a_row = jnp.arange(rows).astype(F32)[:, None] * freq[None, :]
    a_col = jnp.arange(GRID_W).astype(F32)[:, None] * freq[None, :]
    per_row = lambda tbl: jnp.repeat(tbl, GRID_W, axis=0)
    per_col = lambda tbl: jnp.tile(tbl, (rows, 1))
    cr, sr, cc, sc = per_row(jnp.cos(a_row)), per_row(jnp.sin(a_row)), per_col(jnp.cos(a_col)), per_col(jnp.sin(a_col))
    z = jnp.zeros((n_lat, 16), F32)
    cos = jnp.concatenate([cr, cr, cc, cc, jnp.ones((n_lat, 64), F32)], axis=1)
    sa = jnp.concatenate([-sr, z, -sc, z, jnp.zeros((n_lat, 64), F32)], axis=1)
    sb = jnp.concatenate([z, sr, z, sc, jnp.zeros((n_lat, 64), F32)], axis=1)
    ident = lambda fill: jnp.full((TILE, 128), fill, F32)
    return (jnp.concatenate([ident(1.0), cos]), jnp.concatenate([ident(0.0), sa]), jnp.concatenate([ident(0.0), sb]))


def _gathered_to_full(g, name):
    if name in ("w_out", "w_ff2"):
        return jnp.transpose(g, (1, 0, 2, 3)).reshape(2, -1, g.shape[-1])
    return jnp.transpose(g, (1, 2, 0, 3)).reshape(2, g.shape[2], -1)


def _full_to_chunks(gw, name):
    if name in ("w_out", "w_ff2"):
        return jnp.transpose(gw.reshape(2, N_DEV, -1, gw.shape[-1]), (1, 0, 2, 3))
    return jnp.transpose(gw.reshape(2, gw.shape[1], N_DEV, -1), (2, 0, 1, 3))


def kernel(x, c, ctx, c_ctx, w_mod, b_mod, norm1_w, w_in, w_out, sgu_norm_w, sgu_norm_b, sgu_w, sgu_b, gla_wg_fwd, gla_bg_fwd, gla_wg_bwd, gla_bg_bwd, gla_norm_w, mla_q_norm_w, mla_w_uq, mla_kv_norm_w, mla_w_ukv, norm2_w, w_ff1, w_ff2, final_norm_w, loss_target, m_c_ctx, m_w_mod, m_b_mod, m_norm1_w, m_w_in, m_w_out, m_sgu_norm_w, m_sgu_norm_b, m_sgu_w, m_sgu_b, m_gla_wg_fwd, m_gla_bg_fwd, m_gla_wg_bwd, m_gla_bg_bwd, m_gla_norm_w, m_mla_q_norm_w, m_mla_w_uq, m_mla_kv_norm_w, m_mla_w_ukv, m_norm2_w, m_w_ff1, m_w_ff2, m_final_norm_w, v_c_ctx, v_w_mod, v_b_mod, v_norm1_w, v_w_in, v_w_out, v_sgu_norm_w, v_sgu_norm_b, v_sgu_w, v_sgu_b, v_gla_wg_fwd, v_gla_bg_fwd, v_gla_wg_bwd, v_gla_bg_bwd, v_gla_norm_w, v_mla_q_norm_w, v_mla_w_uq, v_mla_kv_norm_w, v_mla_w_ukv, v_norm2_w, v_w_ff1, v_w_ff2, v_final_norm_w):
    W = dict(c_ctx=c_ctx, w_mod=w_mod, b_mod=b_mod, norm1_w=norm1_w, w_in=w_in, w_out=w_out, sgu_norm_w=sgu_norm_w,
             sgu_norm_b=sgu_norm_b, sgu_w=sgu_w, sgu_b=sgu_b, gla_wg_fwd=gla_wg_fwd, gla_bg_fwd=gla_bg_fwd,
             gla_wg_bwd=gla_wg_bwd, gla_bg_bwd=gla_bg_bwd, gla_norm_w=gla_norm_w, mla_q_norm_w=mla_q_norm_w,
             mla_w_uq=mla_w_uq, mla_kv_norm_w=mla_kv_norm_w, mla_w_ukv=mla_w_ukv, norm2_w=norm2_w, w_ff1=w_ff1,
             w_ff2=w_ff2, final_norm_w=final_norm_w)
    M = dict(c_ctx=m_c_ctx, w_mod=m_w_mod, b_mod=m_b_mod, norm1_w=m_norm1_w, w_in=m_w_in, w_out=m_w_out,
             sgu_norm_w=m_sgu_norm_w, sgu_norm_b=m_sgu_norm_b, sgu_w=m_sgu_w, sgu_b=m_sgu_b, gla_wg_fwd=m_gla_wg_fwd,
             gla_bg_fwd=m_gla_bg_fwd, gla_wg_bwd=m_gla_wg_bwd, gla_bg_bwd=m_gla_bg_bwd, gla_norm_w=m_gla_norm_w,
             mla_q_norm_w=m_mla_q_norm_w, mla_w_uq=m_mla_w_uq, mla_kv_norm_w=m_mla_kv_norm_w, mla_w_ukv=m_mla_w_ukv,
             norm2_w=m_norm2_w, w_ff1=m_w_ff1, w_ff2=m_w_ff2, final_norm_w=m_final_norm_w)
    V = dict(c_ctx=v_c_ctx, w_mod=v_w_mod, b_mod=v_b_mod, norm1_w=v_norm1_w, w_in=v_w_in, w_out=v_w_out,
             sgu_norm_w=v_sgu_norm_w, sgu_norm_b=v_sgu_norm_b, sgu_w=v_sgu_w, sgu_b=v_sgu_b, gla_wg_fwd=v_gla_wg_fwd,
             gla_bg_fwd=v_gla_bg_fwd, gla_wg_bwd=v_gla_wg_bwd, gla_bg_bwd=v_gla_bg_bwd, gla_norm_w=v_gla_norm_w,
             mla_q_norm_w=v_mla_q_norm_w, mla_w_uq=v_mla_w_uq, mla_kv_norm_w=v_mla_kv_norm_w, mla_w_ukv=v_mla_w_ukv,
             norm2_w=v_norm2_w, w_ff1=v_w_ff1, w_ff2=v_w_ff2, final_norm_w=v_final_norm_w)

    n_lat = x.shape[1]
    assert ctx.shape[1] == TILE and n_lat % TILE == 0 and x.shape[2] == D
    t_all = TILE + n_lat
    n_t = t_all // TILE
    me = 4 * lax.axis_index("x") + 2 * lax.axis_index("y") + lax.axis_index("c")
    mod_cols = w_mod.shape[2]

    c_all = all_gather([c], "ag_c")[0].reshape(N_DEV, D)
    crows = jnp.concatenate([c_all, c_ctx[None, :], jnp.zeros((7, D), F32)], axis=0)
    b_shard = lax.dynamic_slice_in_dim(b_mod, me * mod_cols, mod_cols, axis=1)[:, None, :]
    mod_sh = mod_forward(crows, w_mod, b_shard, "mod_fwd")
    mod_g = all_gather([mod_sh.reshape(32, mod_cols)], "ag_mod")[0]
    mod_full = jnp.transpose(mod_g.reshape(N_DEV, 2, 16, mod_cols), (1, 2, 0, 3)).reshape(2, 16, 6 * D)
    mod_own = lax.dynamic_index_in_dim(mod_full, me, axis=1, keepdims=False)
    mod_ctx = mod_full[:, 8, :]
    pad2 = jnp.zeros((2, D), F32)
    modl = [jnp.stack([jnp.concatenate([mod_ctx[l].reshape(6, D), pad2]),
                       jnp.concatenate([mod_own[l].reshape(6, D), pad2])]) for l in range(2)]

    v2 = lambda a: a[None, :] if a.ndim == 1 else a.reshape(-1, a.shape[-1])
    gathered = all_gather([v2(W[k].astype(BF)) for k in BIG_NAMES], "ag_weights")
    full = {k: _gathered_to_full(g.reshape((N_DEV,) + W[k].shape), k) for k, g in zip(BIG_NAMES, gathered)}
    w_in_p = _pad_cols(full["w_in"], W_IN_SEGS, P_COLS)
    w_uq_p = _pad_cols(full["mla_w_uq"], W_UQ_SEGS, 1024).astype(F32)
    w_ukv_f = full["mla_w_ukv"].astype(F32)
    wgf_p = jnp.pad(gla_wg_fwd, ((0, 0), (0, 112), (0, 0)))
    wgb_p = jnp.pad(gla_wg_bwd, ((0, 0), (0, 112), (0, 0)))
    sgu_bx = jnp.repeat(jnp.transpose(sgu_b, (0, 2, 1)), 64, axis=2)
    gnw_t = jnp.tile(gla_norm_w, (1, HEADS))
    rc, rsa, rsb = _rope_tables(n_lat)

    xin = jnp.concatenate([ctx[0], x[0]], axis=0)
    row = lambda a: a[None, :]

    def pre_ins(l, xl):
        return [("x", "tile", True, xl), ("mod", "kind", True, modl[l]), ("n1w", "full", True, row(norm1_w[l])),
                ("w_in", "wfull", False, w_in_p[l]), ("sgu_nw", "full", True, row(sgu_norm_w[l])),
                ("sgu_nb", "full", True, row(sgu_norm_b[l])), ("sgu_w", "full", True, sgu_w[l]),
                ("sgu_bx", "full", True, sgu_bx[l]), ("wgf", "full", True, wgf_p[l]), ("bgf", "full", True, row(gla_bg_fwd[l])),
                ("wgb", "full", True, wgb_p[l]), ("bgb", "full", True, row(gla_bg_bwd[l])),
                ("qnw", "full", True, row(mla_q_norm_w[l])), ("w_uq", "full", True, w_uq_p[l]),
                ("kvnw", "full", True, row(mla_kv_norm_w[l])), ("w_ukv", "full", True, w_ukv_f[l]),
                ("rc", "tile", False, rc), ("rsa", "tile", False, rsa), ("rsb", "tile", False, rsb)]

    pre_outs = [("y_sgu", 256, F32), ("qg", 128, F32), ("kg", 128, F32), ("vg", 256, F32), ("lgf", 128, F32),
                ("lgb", 128, F32), ("gr", 256, F32), ("q_cat", 1024, BF), ("k_cat", 1024, BF), ("v", 512, BF)]

    def out_ins(l, xl, a):
        return [("x", "tile", True, xl), ("mod", "kind", True, modl[l]), ("y_sgu", "tile", True, a["y_sgu"]),
                ("o_f", "tile", True, a["o_f"]), ("o_b", "tile", True, a["o_b"]), ("gr", "tile", True, a["gr"]),
                ("y_mla", "tile", True, a["y_mla"]), ("gnw", "full", True, row(gnw_t[l])),
                ("w_out", "wfull", False, full["w_out"][l])]

    def ffn_ins(l, x1):
        return [("x1", "tile", True, x1), ("mod", "kind", True, modl[l]), ("n2w", "full", True, row(norm2_w[l])),
                ("w_ff1", "wcols", False, full["w_ff1"][l]), ("w_ff2", "wrows", False, full["w_ff2"][l])]

    saved, xl = [], xin
    for l in range(2):
        a = tile_forward(pre_tile, f"pre_fwd{l}", t_all, pre_ins(l, xl), pre_outs)
        a["o_f"], a["sf"], a["o_b"], a["sb"] = gla_forward(a["qg"], a["kg"], a["vg"], a["lgf"], a["lgb"], f"gla_fwd{l}")
        a["y_mla"], a["lse"] = mla_forward(a["q_cat"], a["k_cat"], a["v"], f"mla_fwd{l}")
        a["x"] = xl
        a["x1"] = tile_forward(attn_out_tile, f"out_fwd{l}", t_all, out_ins(l, xl, a), [("x1", D, F32)])["x1"]
        ff = tile_forward(ffn_tile, f"ffn_fwd{l}", t_all, ffn_ins(l, a["x1"]), [("x2", D, F32), ("f", D, F32)])
        xl, a["f"] = ff["x2"], ff["f"]
        saved.append(a)

    loss_blk, dx, d_fnw = final_loss(xl, loss_target[0], row(final_norm_w), "final_loss")
    loss = lax.psum(loss_blk[0, 0], AXES)

    G = {}
    dmods = []
    for l in (1, 0):
        a = saved[l]
        dx1, dmod3, dn2w, zpre, zf_t, h2_t, a_ff2 = ffn_backward(
            a["x1"], modl[l], row(norm2_w[l]), full["w_ff1"][l], full["w_ff2"][l], dx, a["f"], f"ffn_bwd{l}")
        gw_ff1 = wgrad(h2_t, zpre, f"wg_ff1_{l}")
        gw_ff2 = jnp.transpose(wgrad(zf_t, a_ff2, f"wg_ff2_{l}"))
        g2, e2 = tile_backward(attn_out_tile, f"out_bwd{l}", t_all, out_ins(l, a["x"], a), [("x1", dx1)],
                               [("zt", D)], [("a_out", D)])
        gw_out = wgrad(e2["a_out"], e2["zt"], f"wg_out_{l}")
        dl_rows, do_bf = mla_delta(g2["y_mla"], a["y_mla"], f"mla_delta{l}")
        dq_cat, dk_cat, dv = mla_backward(a["q_cat"], a["k_cat"], a["v"], a["lse"], dl_rows, do_bf, f"mla_bwd{l}")
        dqf, dkf, dvf, dgf, dqb, dkb, dvb, dgb = gla_backward(
            a["qg"], a["kg"], a["vg"], a["lgf"], a["lgb"], a["sf"], a["sb"], g2["o_f"], f"gla_bwd{l}")
        cots = [("y_sgu", g2["y_sgu"]), ("qg", [dqf, dqb]), ("kg", [dkf, dkb]), ("vg", [dvf, dvb]), ("lgf", dgf),
                ("lgb", dgb), ("gr", g2["gr"]), ("q_cat", dq_cat), ("k_cat", dk_cat), ("v", dv)]
        g1, e1 = tile_backward(pre_tile, f"pre_bwd{l}", t_all, pre_ins(l, a["x"]), cots, [("zp", P_COLS)], [("a_in", D)],
                               resid=("x", g2["x"]))
        gw_in = _unpad_cols(wgrad(e1["a_in"], e1["zp"], f"wg_in_{l}", bk2=P_COLS), W_IN_SEGS)
        dx = g1["x"]
        dmods.append(g1["mod"] + g2["mod"] + dmod3)
        G[l] = dict(w_in=gw_in, w_out=gw_out, w_ff1=gw_ff1, w_ff2=gw_ff2,
                    mla_w_uq=_unpad_cols(g1["w_uq"], W_UQ_SEGS), mla_w_ukv=g1["w_ukv"],
                    norm1_w=g1["n1w"][0], norm2_w=dn2w[0], sgu_norm_w=g1["sgu_nw"][0], sgu_norm_b=g1["sgu_nb"][0],
                    sgu_w=g1["sgu_w"], sgu_b=jnp.transpose(g1["sgu_bx"].reshape(128, HEADS, 64).sum(-1)),
                    gla_wg_fwd=g1["wgf"][:16], gla_bg_fwd=g1["bgf"][0], gla_wg_bwd=g1["wgb"][:16], gla_bg_bwd=g1["bgb"][0],
                    gla_norm_w=g2["gnw"][0].reshape(HEADS, 64).sum(0), mla_q_norm_w=g1["qnw"][0], mla_kv_norm_w=g1["kvnw"][0])
    dmods = dmods[::-1]
    grad_x = dx[TILE:][None]

    dmod_pack = jnp.stack([jnp.stack([dmods[l][1, :6].reshape(-1), dmods[l][0, :6].reshape(-1)]) for l in range(2)])
    dmod_all = all_gather([dmod_pack.reshape(4, 6 * D)], "ag_dmod")[0].reshape(N_DEV, 2, 2, 6 * D)
    dsl = lax.dynamic_slice_in_dim(dmod_all, me * mod_cols, mod_cols, axis=3)
    d_own = jnp.transpose(dsl[:, :, 0, :], (1, 0, 2))
    d_ctx = jnp.transpose(dsl[:, :, 1, :], (1, 0, 2))
    g_w_mod, ds_cc = mod_backward(crows, w_mod, d_own, d_ctx, "mod_bwd")
    g_c_ctx_part = silu_grad_scale(c_ctx[None, :], ds_cc, "silu_bwd")[0]
    g_b_mod_part = jnp.stack([dmods[l][1, :6].reshape(-1) + dmods[l][0, :6].reshape(-1) for l in range(2)])

    small_g = dict(c_ctx=g_c_ctx_part, b_mod=g_b_mod_part, final_norm_w=d_fnw[0])
    for k in SMALL_NAMES:
        if k not in small_g:
            small_g[k] = jnp.stack([G[0][k], G[1][k]])
    res = {}
    sparts = all_gather([v2(small_g[k]) for k in SMALL_NAMES], "ag_small")
    s_out = adamw_group([(sparts[j], v2(W[k]), v2(M[k]), v2(V[k])) for j, k in enumerate(SMALL_NAMES)], "adamw_small")
    for j, k in enumerate(SMALL_NAMES):
        res[k] = [o.reshape(W[k].shape) for o in s_out[j]]

    chunks = []
    for k in BIG_NAMES:
        ch = _full_to_chunks(jnp.stack([G[0][k], G[1][k]]), k).astype(BF)
        chunks.append(jnp.transpose(ch.reshape(4, 2, -1, ch.shape[-1]), (1, 0, 2, 3)))
    bparts = reduce_scatter(chunks, "rs_grads")
    for j, k in enumerate(BIG_NAMES):
        res[k] = [o.reshape(W[k].shape) for o in adamw(bparts[j], v2(W[k]), v2(M[k]), v2(V[k]), f"adamw_{k}")]
    res["w_mod"] = [o.reshape(w_mod.shape)
                    for o in adamw(v2(g_w_mod)[None], v2(w_mod), v2(m_w_mod), v2(v_w_mod), "adamw_w_mod")]
    outs = [loss, grad_x]
    for j in range(4):
        outs += [res[k][j] for k in WEIGHT_ORDER]
    return tuple(outs)
```

```python
import jax
import jax.numpy as jnp
from jax import lax
from jax.experimental import pallas as pl
from jax.experimental.pallas import tpu as pltpu

F32 = jnp.float32
BF = jnp.bfloat16

N_DEV = 8
AXES = ("x", "y", "c")
EPS = 1e-6
D = 1024
TILE = 256
GCH = 128
SGU_CHUNK = 128
HEADS = 4
ROPE_BASE = 10000.0
GRID_W = 64
GLA_TAU = 16.0
ATT_SCALE = (128 + 64) ** -0.5
ATT_SCALE_LOG2 = ATT_SCALE * 1.4426950408889634
LN2 = 0.6931471805599453
KV_CH = 512
KV_CH_FWD = 2048
QT_FWD = 256
Q_CH_BWD = 1024
KT_BWD = 768
MLA_UNROLL = 2
D_FF = 4096
FF_CH = 1024

ADAM_LR = 0.001
ADAM_B1 = 0.9
ADAM_B2 = 0.999
ADAM_EPS = 1e-08
ADAM_WD = 0.01
ADAM_STEP = 10

VMEM_LIMIT_MB = 56
ADAMW_BLOCK_ELEMS = 256 * 1024

W_IN_SEGS = ((0, 0, 128), (128, 128, 256), (384, 384, 16), (512, 400, 16), (640, 416, 256), (896, 672, 64),
             (1024, 736, 256), (1280, 992, 256), (1536, 1248, 128), (1664, 1376, 256), (1920, 1632, 256))
P_COLS = 2176
O_GK, O_GV, O_GGF, O_GGB, O_CKV, O_KR, O_SU, O_SV, O_GQ, O_GR, O_DQ = (s[0] for s in W_IN_SEGS)
W_UQ_SEGS = tuple((h * 256, h * 192, 192) for h in range(HEADS))

SMALL_NAMES = ("c_ctx", "b_mod", "norm1_w", "sgu_norm_w", "sgu_norm_b", "sgu_w", "sgu_b", "gla_wg_fwd", "gla_bg_fwd",
               "gla_wg_bwd", "gla_bg_bwd", "gla_norm_w", "mla_q_norm_w", "mla_kv_norm_w", "norm2_w", "final_norm_w")
BIG_NAMES = ("w_in", "w_out", "mla_w_uq", "mla_w_ukv", "w_ff1", "w_ff2")
WEIGHT_ORDER = ("c_ctx", "w_mod", "b_mod", "norm1_w", "w_in", "w_out", "sgu_norm_w", "sgu_norm_b", "sgu_w", "sgu_b",
                "gla_wg_fwd", "gla_bg_fwd", "gla_wg_bwd", "gla_bg_bwd", "gla_norm_w", "mla_q_norm_w", "mla_w_uq",
                "mla_kv_norm_w", "mla_w_ukv", "norm2_w", "w_ff1", "w_ff2", "final_norm_w")


def _cparams(n_axes):
    return pltpu.CompilerParams(dimension_semantics=("arbitrary",) * n_axes,
                                vmem_limit_bytes=VMEM_LIMIT_MB * 1024 * 1024)


def _dot(a, b):
    return jnp.dot(a, b, preferred_element_type=F32)


def _dot_nt(a, b):
    return lax.dot_general(a, b, (((1,), (1,)), ((), ())), preferred_element_type=F32)


def _dot_tn(a, b):
    return lax.dot_general(a, b, (((0,), (0,)), ((), ())), preferred_element_type=F32)


def _nn(a, b):
    return _dot(a.astype(BF), b.astype(BF))


def _nt(a, b):
    return _dot_nt(a.astype(BF), b.astype(BF))


def _tn(a, b):
    return _dot_tn(a.astype(BF), b.astype(BF))


nn_d = jax.custom_vjp(_nn)
nt_d = jax.custom_vjp(_nt)
tn_d = jax.custom_vjp(_tn)
nn_d.defvjp(lambda a, b: (_nn(a, b), (a, b)), lambda r, dy: (_nt(dy, r[1]), _tn(r[0], dy)))
nt_d.defvjp(lambda a, b: (_nt(a, b), (a, b)), lambda r, dy: (_nn(dy, r[1]), _tn(dy, r[0])))
tn_d.defvjp(lambda a, b: (_tn(a, b), (a, b)), lambda r, dy: (_nt(r[1], dy), _nn(r[0], dy)))


def nn_const(w_bf, diff):
    def raw(a):
        return _dot(a.astype(BF), w_bf)

    if not diff:
        return raw
    f = jax.custom_vjp(raw)
    f.defvjp(lambda a: (raw(a), None), lambda _, dy: (_dot_nt(dy.astype(BF), w_bf),))
    return f


def _split3(g):
    hi = g.astype(BF)
    r = g - hi.astype(F32)
    mid = r.astype(BF)
    lo = (r - mid.astype(F32)).astype(BF)
    return hi, mid, lo


def make_cum(tri_bf, tri_t_bf, diff):
    def raw(g, t):
        hi, mid, lo = _split3(g)
        return _dot(t, hi) + _dot(t, mid) + _dot(t, lo)

    def fwd(g):
        return raw(g, tri_bf)

    if not diff:
        return fwd
    cum = jax.custom_vjp(fwd)
    cum.defvjp(lambda g: (fwd(g), None), lambda _, db: (raw(db, tri_t_bf),))
    return cum


def _roll_lanes(x, shift):
    return pltpu.roll(x, shift, 1)


def make_rope(c, sa, sb, diff):
    def raw(x):
        return x * c + _roll_lanes(x, 112) * sa + _roll_lanes(x, 16) * sb

    if not diff:
        return raw
    f = jax.custom_vjp(raw)
    f.defvjp(lambda x: (raw(x), None),
             lambda _, dy: (dy * c + _roll_lanes(dy * sa, 16) + _roll_lanes(dy * sb, 112),))
    return f


def _ops(diff):
    return (nn_d, nt_d, tn_d) if diff else (_nn, _nt, _tn)


def _rms(x, w):
    return x * lax.rsqrt(jnp.mean(x * x, axis=-1, keepdims=True) + EPS) * w


def _gelu(x):
    return 0.5 * x * (1.0 + jnp.tanh(0.7978845608028654 * (x + 0.044715 * (x * x * x))))


def _silu(x):
    return x * jax.nn.sigmoid(x)


def _log_sigmoid(z):
    return jnp.minimum(z, 0.0) - jnp.log(1.0 + jnp.exp(-jnp.abs(z)))


def _lane_group_mask(width, group, h):
    lane = lax.broadcasted_iota(jnp.int32, (1, width), 1)
    return ((lane >= h * group) & (lane < (h + 1) * group)).astype(F32)


def pre_tile(d, c, z):
    nn, _, _ = _ops(z is not None)
    rope = make_rope(c["rc"], c["rsa"], c["rsb"], z is not None)
    mod = d["mod"]
    h = _rms(d["x"], d["n1w"]) * (1.0 + mod[1:2]) + mod[0:1]
    p = nn_const(c["w_in"], z is not None)(h)
    if z is not None:
        p = p + z["zp"]
    gk, gv = p[:, O_GK:O_GK + 128], p[:, O_GV:O_GV + 256]
    ggf, ggb = p[:, O_GGF:O_GGF + 128], p[:, O_GGB:O_GGB + 128]
    ckv, kr = p[:, O_CKV:O_CKV + 256], p[:, O_KR:O_KR + 128]
    su, sv = p[:, O_SU:O_SU + 256], p[:, O_SV:O_SV + 256]
    gq, gr, dq = p[:, O_GQ:O_GQ + 128], p[:, O_GR:O_GR + 256], p[:, O_DQ:O_DQ + 256]

    u = _gelu(su)
    gv_ = _gelu(sv)
    mu = jnp.mean(gv_, axis=-1, keepdims=True)
    cen = gv_ - mu
    vn = cen * lax.rsqrt(jnp.mean(cen * cen, axis=-1, keepdims=True) + EPS) * d["sgu_nw"] + d["sgu_nb"]
    hm = [_lane_group_mask(256, 64, hh) for hh in range(HEADS)]
    rows = []
    for ci in range(vn.shape[0] // SGU_CHUNK):
        vc = vn[ci * SGU_CHUNK:(ci + 1) * SGU_CHUNK]
        s = d["sgu_bx"]
        for hh in range(HEADS):
            s = s + hm[hh] * nn(d["sgu_w"][hh], vc)
        rows.append(s)
    y_sgu = u * jnp.concatenate(rows, axis=0)

    qg = gq * (32 ** -0.5)
    lgf = _log_sigmoid(nn(ggf, d["wgf"]) + d["bgf"]) * (1.0 / GLA_TAU)
    lgb = _log_sigmoid(nn(ggb, d["wgb"]) + d["bgb"]) * (1.0 / GLA_TAU)

    kv = nn(_rms(ckv, d["kvnw"]), d["w_ukv"])
    kr_r = rope(kr)
    q = nn(_rms(dq, d["qnw"]), d["w_uq"])
    qs, ks, vs = [], [], []
    for hh in range(HEADS):
        qs += [q[:, hh * 256:hh * 256 + 128], rope(q[:, hh * 256 + 128:(hh + 1) * 256])]
        ks += [kv[:, hh * 256:hh * 256 + 128], kr_r]
        vs += [kv[:, hh * 256 + 128:(hh + 1) * 256]]
    outs = dict(y_sgu=y_sgu, qg=qg, kg=gk, vg=gv, lgf=lgf, lgb=lgb, gr=gr,
                q_cat=jnp.concatenate(qs, axis=-1) * ATT_SCALE_LOG2, k_cat=jnp.concatenate(ks, axis=-1), v=jnp.concatenate(vs, axis=-1))
    return outs, dict(a_in=h)


def attn_out_tile(d, c, z):
    mod = d["mod"]
    o = d["o_f"] + c["o_b"]
    ms = jnp.zeros_like(o)
    for hh in range(HEADS):
        m_h = _lane_group_mask(256, 64, hh)
        ms = ms + m_h * (jnp.sum(o * o * m_h, axis=-1, keepdims=True) * (1.0 / 64))
    yg = o * lax.rsqrt(ms + EPS) * d["gnw"] * _silu(d["gr"])
    y = jnp.concatenate([d["y_sgu"], yg, d["y_mla"]], axis=-1)
    t = nn_const(c["w_out"], z is not None)(y)
    if z is not None:
        t = t + z["zt"]
    return dict(x1=d["x"] + mod[2:3] * t), dict(a_out=y)


def ffn_tile(d, c, z):
    mod = d["mod"]
    h2 = _rms(d["x1"], d["n2w"]) * (1.0 + mod[4:5]) + mod[3:4]
    f = None
    for j in range(D_FF // FF_CH):
        a = jnp.maximum(nn_const(c["w_ff1"][j], False)(h2), 0.0)
        fj = nn_const(c["w_ff2"][j], False)(a * a)
        f = fj if f is None else f + fj
    return dict(x2=d["x1"] + mod[5:6] * f, f=f), {}


def _in_spec(kind, arr, tile):
    if kind == "tile":
        return pl.BlockSpec((tile, arr.shape[1]), lambda i: (i, 0))
    if kind == "kind":
        return pl.BlockSpec((1,) + arr.shape[1:], lambda i: (jnp.where(i < TILE // tile, 0, 1), 0, 0))
    nd = arr.ndim
    if kind in ("wfull", "wcols", "wrows"):
        return pl.BlockSpec(arr.shape, lambda i: (0,) * nd, pipeline_mode=pl.Buffered(1))
    return pl.BlockSpec(arr.shape, lambda i: (0,) * nd)


def _load(kind, ref):
    if kind == "kind":
        return ref[0]
    if kind == "wcols":
        return [ref[:, j * FF_CH:(j + 1) * FF_CH] for j in range(ref.shape[1] // FF_CH)]
    if kind == "wrows":
        return [ref[j * FF_CH:(j + 1) * FF_CH, :] for j in range(ref.shape[0] // FF_CH)]
    return ref[...]


def tile_forward(fn, name, t_all, ins, out_defs, tile=TILE):
    keys = [k for k, _, _, _ in ins]
    kinds = [kd for _, kd, _, _ in ins]
    diffs = [df for _, _, df, _ in ins]
    arrs = [a for _, _, _, a in ins]
    n_in = len(ins)

    def body(*refs):
        vals = [_load(kinds[j], refs[j]) for j in range(n_in)]
        d = {keys[j]: vals[j] for j in range(n_in) if diffs[j]}
        c = {keys[j]: vals[j] for j in range(n_in) if not diffs[j]}
        outs, _ = fn(d, c, None)
        for j, (k, _, dt) in enumerate(out_defs):
            refs[n_in + j][...] = outs[k].astype(dt)

    res = pl.pallas_call(
        body, name=name, grid=(t_all // tile,),
        in_specs=[_in_spec(kinds[j], arrs[j], tile) for j in range(n_in)],
        out_specs=[pl.BlockSpec((tile, w), lambda i: (i, 0)) for _, w, _ in out_defs],
        out_shape=[jax.ShapeDtypeStruct((t_all, w), dt) for _, w, dt in out_defs],
        compiler_params=_cparams(1),
    )(*arrs)
    return {k: r for (k, _, _), r in zip(out_defs, res)}


def tile_backward(fn, name, t_all, ins, cots, z_defs, aux_defs, tile=TILE, resid=None):
    keys = [k for k, _, _, _ in ins]
    kinds = [kd for _, kd, _, _ in ins]
    diffs = [df for _, _, df, _ in ins]
    arrs = [a for _, _, _, a in ins]
    cot_keys, cot_arrs = [], []
    for k, a in cots:
        for one in (a if isinstance(a, (list, tuple)) else [a]):
            cot_keys.append(k)
            cot_arrs.append(one)
    if resid is not None:
        cot_keys.append("resid:" + resid[0])
        cot_arrs.append(resid[1])
    n_in, n_cot = len(ins), len(cot_arrs)
    dkeys = [j for j in range(n_in) if diffs[j]]
    ctx_tiles = TILE // tile

    def body(*refs):
        i = pl.program_id(0)
        vals = [_load(kinds[j], refs[j]) for j in range(n_in)]
        d = {keys[j]: vals[j] for j in dkeys}
        c = {keys[j]: vals[j] for j in range(n_in) if not diffs[j]}
        zs = {k: jnp.zeros((tile, w), F32) for k, w in z_defs}
        outs, vjp_fn, aux = jax.vjp(lambda dd, zz: fn(dd, c, zz), d, zs, has_aux=True)
        ct = {}
        for j, k in enumerate(cot_keys):
            ct[k] = refs[n_in + j][...] + ct[k] if k in ct else refs[n_in + j][...]
        dd, dz = vjp_fn({k: ct[k].astype(outs[k].dtype) for k in outs})
        base = n_in + n_cot
        for n, j in enumerate(dkeys):
            ref, g = refs[base + n], dd[keys[j]]
            if kinds[j] == "tile":
                ref[...] = g + ct["resid:" + keys[j]] if "resid:" + keys[j] in ct else g
            else:
                first = ((i == 0) | (i == ctx_tiles)) if kinds[j] == "kind" else (i == 0)
                gv = g[None] if kinds[j] == "kind" else g

                @pl.when(first)
                def _(ref=ref, gv=gv):
                    ref[...] = gv

                @pl.when(jnp.logical_not(first))
                def _(ref=ref, gv=gv):
                    ref[...] += gv
        base += len(dkeys)
        for n, (k, _) in enumerate(z_defs):
            refs[base + n][...] = dz[k].astype(BF)
        base += len(z_defs)
        for n, (k, _) in enumerate(aux_defs):
            refs[base + n][...] = aux[k].T.astype(BF)

    out_specs, out_shape = [], []
    for j in dkeys:
        out_specs.append(_in_spec(kinds[j], arrs[j], tile))
        out_shape.append(jax.ShapeDtypeStruct(arrs[j].shape, F32))
    for _, w in z_defs:
        out_specs.append(pl.BlockSpec((tile, w), lambda i: (i, 0)))
        out_shape.append(jax.ShapeDtypeStruct((t_all, w), BF))
    for _, w in aux_defs:
        out_specs.append(pl.BlockSpec((w, tile), lambda i: (0, i)))
        out_shape.append(jax.ShapeDtypeStruct((w, t_all), BF))
    res = pl.pallas_call(
        body, name=name, grid=(t_all // tile,),
        in_specs=[_in_spec(kinds[j], arrs[j], tile) for j in range(n_in)]
        + [pl.BlockSpec((tile, a.shape[1]), lambda i: (i, 0)) for a in cot_arrs],
        out_specs=out_specs, out_shape=out_shape, compiler_params=_cparams(1),
    )(*arrs, *cot_arrs)
    grads = {keys[j]: res[n] for n, j in enumerate(dkeys)}
    extra = {k: res[len(dkeys) + n] for n, (k, _) in enumerate(list(z_defs) + list(aux_defs))}
    return grads, extra


def ffn_backward(x1, modl, n2w, w1, w2, dx2, f, name):
    t_all = x1.shape[0]
    n_ch = D_FF // FF_CH

    def head(x, mod, nw):
        return _rms(x, nw) * (1.0 + mod[4:5]) + mod[3:4]

    def body(x_ref, mod_ref, nw_ref, w1_ref, w2_ref, dx2_ref, f_ref, dx1_ref, dmod_ref, dnw_ref, zpre_ref, zf_ref, a1_ref,
             a2_ref):
        i = pl.program_id(0)
        mod = mod_ref[0]
        dx2 = dx2_ref[...]
        h2, vjp_head = jax.vjp(head, x_ref[...], mod, nw_ref[...])
        h2b = h2.astype(BF)
        dfb = (dx2 * mod[5:6]).astype(BF)
        f = f_ref[...]
        dh2 = jnp.zeros((TILE, D), F32)
        for j in range(n_ch):
            cs = slice(j * FF_CH, (j + 1) * FF_CH)
            a = jnp.maximum(_dot(h2b, w1_ref[:, cs]), 0.0)
            a2b = (a * a).astype(BF)
            dpre = (_dot_nt(dfb, w2_ref[cs, :]) * (2.0 * a)).astype(BF)
            dh2 = dh2 + _dot_nt(dpre, w1_ref[:, cs])
            zpre_ref[:, cs] = dpre
            a2_ref[:, cs] = a2b
        zf_ref[...] = (dx2 * mod[5:6]).T.astype(BF)
        a1_ref[...] = h2.T.astype(BF)
        dx1, dmod, dnw = vjp_head(dh2)
        dx1_ref[...] = dx2 + dx1
        row = lax.broadcasted_iota(jnp.int32, (8, D), 0)
        dmod = dmod + jnp.where(row == 5, jnp.sum(dx2 * f, axis=0, keepdims=True), 0.0)
        first_kind = (i == 0) | (i == 1)

        @pl.when(first_kind)
        def _():
            dmod_ref[0] = dmod

        @pl.when(jnp.logical_not(first_kind))
        def _():
            dmod_ref[0] += dmod

        @pl.when(i == 0)
        def _():
            dnw_ref[...] = dnw

        @pl.when(i > 0)
        def _():
            dnw_ref[...] += dnw

    tspec = lambda w: pl.BlockSpec((TILE, w), lambda i: (i, 0))
    once = lambda shp: pl.BlockSpec(shp, lambda i: (0, 0), pipeline_mode=pl.Buffered(1))
    kind = pl.BlockSpec((1, 8, D), lambda i: (jnp.minimum(i, 1), 0, 0))
    tr = pl.BlockSpec((D, TILE), lambda i: (0, i))
    return pl.pallas_call(
        body, name=name, grid=(t_all // TILE,),
        in_specs=[tspec(D), kind, pl.BlockSpec((1, D), lambda i: (0, 0)), once((D, D_FF)), once((D_FF, D)), tspec(D),
                  tspec(D)],
        out_specs=[tspec(D), kind, pl.BlockSpec((1, D), lambda i: (0, 0)), tspec(D_FF), tr, tr, tspec(D_FF)],
        out_shape=[jax.ShapeDtypeStruct((t_all, D), F32), jax.ShapeDtypeStruct((2, 8, D), F32),
                   jax.ShapeDtypeStruct((1, D), F32), jax.ShapeDtypeStruct((t_all, D_FF), BF),
                   jax.ShapeDtypeStruct((D, t_all), BF), jax.ShapeDtypeStruct((D, t_all), BF),
                   jax.ShapeDtypeStruct((t_all, D_FF), BF)],
        compiler_params=_cparams(1),
    )(x1, modl, n2w, w1, w2, dx2, f)


WG_TOK = 768


def wgrad(at, b, name, bk2=1024):
    k1, t = at.shape
    k2 = b.shape[1]
    bk2 = min(bk2, k2)
    tt = WG_TOK if t % WG_TOK == 0 else TILE
    nt_ = t // tt

    def body(a_ref, b_ref, o_ref, acc):
        s = pl.program_id(1)

        @pl.when(s == 0)
        def _():
            acc[...] = jnp.zeros_like(acc)

        acc[...] += _dot(a_ref[...], b_ref[...])

        @pl.when(s == nt_ - 1)
        def _():
            o_ref[...] = acc[...]

    return pl.pallas_call(
        body, name=name, grid=(k2 // bk2, nt_),
        in_specs=[pl.BlockSpec((k1, tt), lambda j, s: (0, s)), pl.BlockSpec((tt, bk2), lambda j, s: (s, j))],
        out_specs=pl.BlockSpec((k1, bk2), lambda j, s: (0, j)),
        out_shape=jax.ShapeDtypeStruct((k1, k2), F32),
        scratch_shapes=[pltpu.VMEM((k1, bk2), F32)],
        compiler_params=_cparams(2),
    )(at, b)


def _gla_consts(reverse, diff):
    r = lax.broadcasted_iota(jnp.int32, (GCH, GCH), 0)
    cc = lax.broadcasted_iota(jnp.int32, (GCH, GCH), 1)
    low = (r >= cc)
    tri = (jnp.logical_not(low) | (r == cc)) if reverse else low
    tri_f = tri.astype(F32)
    tri_t = (cc >= r) if not reverse else (cc <= r)
    hmk = [_lane_group_mask(128, 32, h) for h in range(HEADS)]
    hmv = [_lane_group_mask(256, 64, h) for h in range(HEADS)]
    e = lax.broadcasted_iota(jnp.int32, (256, 128), 0) // 64
    dk = lax.broadcasted_iota(jnp.int32, (256, 128), 1) // 32
    return dict(cum=make_cum(tri_f.astype(BF), tri_t.astype(F32).astype(BF), diff), ops=_ops(diff), reverse=reverse,
                tri4=jnp.concatenate([tri_f] * HEADS, axis=0), hmk=hmk, hmv=hmv, bd=(e == dk).astype(F32))


def gla_chunk(st, q, k, v, g, cs):
    nn, nt, tn = cs["ops"]
    b = cs["cum"](g)
    bl = jnp.sum(g, axis=0, keepdims=True)
    b_ref = jnp.sum(g[GCH // 2:] if cs["reverse"] else g[:GCH // 2], axis=0, keepdims=True)
    qe = q * jnp.exp(b)
    qs = q * jnp.exp(b - b_ref)
    ks = k * jnp.exp(b_ref - b)
    qstack = jnp.concatenate([qs * cs["hmk"][h] for h in range(HEADS)], axis=0)
    att = nt(qstack, ks) * cs["tri4"]
    ofull = nn(att, v)
    o = nt(qe, st)
    for h in range(HEADS):
        o = o + ofull[h * GCH:(h + 1) * GCH] * cs["hmv"][h]
    kd = k * jnp.exp(bl - b)
    st_new = st * jnp.exp(bl) + tn(v, kd) * cs["bd"]
    return st_new, o


def _gla_chunk_index(s, n_ch, reverse):
    ctx_ch = TILE // GCH
    if not reverse:
        return s
    return jnp.where(s < ctx_ch, ctx_ch - 1 - s, n_ch - 1 + ctx_ch - s)


def gla_forward(q, k, v, gf, gb, name):
    t = q.shape[0]
    n_ch = t // GCH

    def body(*refs):
        s = pl.program_id(0)
        for dr, reverse in enumerate((False, True)):
            q_ref, k_ref, v_ref, g_ref = refs[4 * dr:4 * dr + 4]
            o_ref, sst_ref = refs[8 + 2 * dr:8 + 2 * dr + 2]
            st = refs[12 + dr]

            @pl.when(s == 0)
            def _(st=st):
                st[...] = jnp.zeros_like(st)

            cur = st[...]
            sst_ref[0] = cur
            st_new, o = gla_chunk(cur, q_ref[...], k_ref[...], v_ref[...], g_ref[...], _gla_consts(reverse, False))
            o_ref[...] = o
            st[...] = st_new

    in_specs, out_specs, out_shape = [], [], []
    for reverse in (False, True):
        im = lambda s, reverse=reverse: (_gla_chunk_index(s, n_ch, reverse), 0)
        im3 = lambda s, reverse=reverse: (_gla_chunk_index(s, n_ch, reverse), 0, 0)
        in_specs += [pl.BlockSpec((GCH, 128), im), pl.BlockSpec((GCH, 128), im), pl.BlockSpec((GCH, 256), im),
                     pl.BlockSpec((GCH, 128), im)]
        out_specs += [pl.BlockSpec((GCH, 256), im), pl.BlockSpec((1, 256, 128), im3)]
        out_shape += [jax.ShapeDtypeStruct((t, 256), F32), jax.ShapeDtypeStruct((n_ch, 256, 128), F32)]
    return pl.pallas_call(
        body, name=name, grid=(n_ch,), in_specs=in_specs, out_specs=out_specs, out_shape=out_shape,
        scratch_shapes=[pltpu.VMEM((256, 128), F32), pltpu.VMEM((256, 128), F32)],
        compiler_params=_cparams(1),
    )(q, k, v, gf, q, k, v, gb)


def gla_backward(q, k, v, gf, gb, sst_f, sst_b, do, name):
    t = q.shape[0]
    n_ch = t // GCH

    def body(*refs):
        r = pl.program_id(0)
        for dr, reverse in enumerate((False, True)):
            q_ref, k_ref, v_ref, g_ref, sst_ref, do_ref = refs[6 * dr:6 * dr + 6]
            outs = refs[12 + 4 * dr:12 + 4 * dr + 4]
            dst = refs[20 + dr]

            @pl.when(r == 0)
            def _(dst=dst):
                dst[...] = jnp.zeros_like(dst)

            cs = _gla_consts(reverse, True)
            _, vjp_fn = jax.vjp(lambda a, b, c_, d_, e_, cs=cs: gla_chunk(a, b, c_, d_, e_, cs),
                                sst_ref[0], q_ref[...], k_ref[...], v_ref[...], g_ref[...])
            grads = vjp_fn((dst[...], do_ref[...]))
            for o_ref, gval in zip(outs, grads[1:]):
                o_ref[...] = gval
            dst[...] = grads[0]

    in_specs, out_specs, out_shape = [], [], []
    for reverse in (False, True):
        im = lambda r, reverse=reverse: (_gla_chunk_index(n_ch - 1 - r, n_ch, reverse), 0)
        im3 = lambda r, reverse=reverse: (_gla_chunk_index(n_ch - 1 - r, n_ch, reverse), 0, 0)
        in_specs += [pl.BlockSpec((GCH, 128), im), pl.BlockSpec((GCH, 128), im), pl.BlockSpec((GCH, 256), im),
                     pl.BlockSpec((GCH, 128), im), pl.BlockSpec((1, 256, 128), im3), pl.BlockSpec((GCH, 256), im)]
        out_specs += [pl.BlockSpec((GCH, 128), im), pl.BlockSpec((GCH, 128), im), pl.BlockSpec((GCH, 256), im),
                      pl.BlockSpec((GCH, 128), im)]
        out_shape += [jax.ShapeDtypeStruct((t, 128), F32), jax.ShapeDtypeStruct((t, 128), F32),
                      jax.ShapeDtypeStruct((t, 256), F32), jax.ShapeDtypeStruct((t, 128), F32)]
    return pl.pallas_call(
        body, name=name, grid=(n_ch,), in_specs=in_specs, out_specs=out_specs, out_shape=out_shape,
        scratch_shapes=[pltpu.VMEM((256, 128), F32), pltpu.VMEM((256, 128), F32)],
        compiler_params=_cparams(1),
    )(q, k, v, gf, sst_f, do, q, k, v, gb, sst_b, do)


def _resident(hbm_ref, vmem_ref, sem):
    cp = pltpu.make_async_copy(hbm_ref, vmem_ref, sem)
    cp.start()
    cp.wait()


def mla_forward(q_cat, k_cat, v, name):
    t = q_cat.shape[0]
    qt = QT_FWD if t % QT_FWD == 0 else TILE
    n_t = t // qt

    ch = KV_CH_FWD if (t - TILE) % KV_CH_FWD == 0 else KV_CH
    n_main = (t - TILE) // ch

    def body(q_ref, k_hbm, v_hbm, o_ref, lse_ref, k_s, v_s, m_s, l_s, acc_s, sem):
        i = pl.program_id(0)

        @pl.when(i == 0)
        def _():
            _resident(k_hbm, k_s, sem.at[0])
            _resident(v_hbm, v_s, sem.at[1])

        m_s[...] = jnp.full(m_s.shape, -1e30, F32)
        l_s[...] = jnp.zeros_like(l_s)
        acc_s[...] = jnp.zeros_like(acc_s)

        def chunk(r0, size, hide_ctx_rows=False):
            for h in range(HEADS):
                kh = k_s[pl.ds(r0, size), h * 256:(h + 1) * 256]
                vh = v_s[pl.ds(r0, size), h * 128:(h + 1) * 128]
                s = _dot_nt(q_ref[:, h * 256:(h + 1) * 256], kh)
                if hide_ctx_rows:
                    s = jnp.where(lax.broadcasted_iota(jnp.int32, (qt, 1), 0) < TILE, -1e30, s)
                m_prev = m_s[h]
                m_col = jnp.maximum(jnp.max(m_prev, axis=-1, keepdims=True), jnp.max(s, axis=-1, keepdims=True))
                m_next = jnp.broadcast_to(m_col, m_prev.shape)
                p = jnp.exp2(s - m_col)
                alpha = jnp.exp2(m_prev - m_next)
                l_s[h] = alpha * l_s[h] + jnp.sum(p, axis=-1, keepdims=True)
                acc_s[h] = alpha * acc_s[h] + _dot(p.astype(BF), vh)
                m_s[h] = m_next

        chunk(0, TILE)

        def main_loop(hide, size, unroll):
            def step(c, carry):
                chunk(pl.multiple_of(TILE + c * size, TILE), size, hide)
                return carry

            lax.fori_loop(0, (t - TILE) // size, step, 0, unroll=unroll)

        if qt > TILE:
            pl.when(i == 0)(lambda: main_loop(True, KV_CH, 1))
        pl.when(i >= 1)(lambda: main_loop(False, ch, MLA_UNROLL))

        lane = lax.broadcasted_iota(jnp.int32, (qt, 128), 1)
        cols = jnp.zeros((qt, 128), F32)
        for h in range(HEADS):
            o_ref[:, h * 128:(h + 1) * 128] = acc_s[h] / l_s[h]
            cols = jnp.where(lane == h, m_s[h] + jnp.log2(l_s[h]), cols)
        lse_ref[...] = cols.T[0:8, :]

    return pl.pallas_call(
        body, name=name, grid=(n_t,),
        in_specs=[pl.BlockSpec((qt, 1024), lambda i: (i, 0)), pl.BlockSpec(memory_space=pl.ANY),
                  pl.BlockSpec(memory_space=pl.ANY)],
        out_specs=[pl.BlockSpec((qt, 512), lambda i: (i, 0)), pl.BlockSpec((8, qt), lambda i: (0, i))],
        out_shape=[jax.ShapeDtypeStruct((t, 512), F32), jax.ShapeDtypeStruct((8, t), F32)],
        scratch_shapes=[pltpu.VMEM((t, 1024), BF), pltpu.VMEM((t, 512), BF), pltpu.VMEM((HEADS, qt, 128), F32),
                        pltpu.VMEM((HEADS, qt, 128), F32), pltpu.VMEM((HEADS, qt, 128), F32),
                        pltpu.SemaphoreType.DMA((2,))],
        compiler_params=_cparams(1),
    )(q_cat, k_cat, v)


def mla_delta(do, o, name):
    t = do.shape[0]

    def body(do_ref, o_ref, dl_ref, dob_ref):
        d = do_ref[...]
        prod = d * o_ref[...]
        rows = [jnp.sum(prod[:, h * 128:(h + 1) * 128], axis=-1, keepdims=True) for h in range(HEADS)]
        cols = jnp.concatenate(rows + [jnp.zeros((TILE, 128 - HEADS), F32)], axis=-1)
        dl_ref[...] = cols.T[0:8, :]
        dob_ref[...] = d.astype(BF)

    return pl.pallas_call(
        body, name=name, grid=(t // TILE,),
        in_specs=[pl.BlockSpec((TILE, 512), lambda i: (i, 0)), pl.BlockSpec((TILE, 512), lambda i: (i, 0))],
        out_specs=[pl.BlockSpec((8, TILE), lambda i: (0, i)), pl.BlockSpec((TILE, 512), lambda i: (i, 0))],
        out_shape=[jax.ShapeDtypeStruct((8, t), F32), jax.ShapeDtypeStruct((t, 512), BF)],
        compiler_params=_cparams(1),
    )(do, o)


def mla_backward(q_cat, k_cat, v, lse_rows, dl_rows, do_bf, name):
    t = q_cat.shape[0]
    kt = KT_BWD if t % KT_BWD == 0 else TILE
    n_t = t // kt
    ch = Q_CH_BWD if (t - TILE) % Q_CH_BWD == 0 else KV_CH
    n_main = (t - TILE) // ch

    def body(q_hbm, do_hbm, k_ref, v_ref, lse_ref, dl_ref, dq_ref, dk_ref, dv_ref, q_s, do_s, dk_s, dv_s, sem):
        h, j = pl.program_id(0), pl.program_id(1)

        @pl.when(j == 0)
        def _():
            _resident(q_hbm.at[:, pl.ds(pl.multiple_of(h * 256, 256), 256)], q_s, sem.at[0])
            _resident(do_hbm.at[:, pl.ds(pl.multiple_of(h * 128, 128), 128)], do_s, sem.at[1])
            dq_ref[...] = jnp.zeros_like(dq_ref)

        dk_s[...] = jnp.zeros_like(dk_s)
        dv_s[...] = jnp.zeros_like(dv_s)
        kh = k_ref[...]
        vh = v_ref[...]

        def chunk(r0, size, ctx_only=False):
            qh = q_s[pl.ds(r0, size), :]
            doh = do_s[pl.ds(r0, size), :]
            pt = jnp.exp2(_dot_nt(kh, qh) - lse_ref[pl.ds(h, 1), pl.ds(r0, size)])
            if ctx_only and kt > TILE:
                pt = jnp.where(lax.broadcasted_iota(jnp.int32, (kt, 1), 0) < TILE, pt, 0.0)
            dst = (pt * (_dot_nt(vh, doh) - dl_ref[pl.ds(h, 1), pl.ds(r0, size)])).astype(BF)
            dv_s[...] += _dot(pt.astype(BF), doh)
            dk_s[...] += _dot(dst, qh)
            dq_ref[pl.ds(r0, size), :] += _dot_tn(dst, kh)

        @pl.when(j == 0)
        def _():
            chunk(0, TILE, ctx_only=True)

        def step(c, carry):
            chunk(pl.multiple_of(TILE + c * ch, TILE), ch)
            return carry

        lax.fori_loop(0, n_main, step, 0, unroll=MLA_UNROLL)
        dk_ref[...] = dk_s[...] * LN2
        dv_ref[...] = dv_s[...]

        @pl.when(j == n_t - 1)
        def _():
            dq_ref[...] = dq_ref[...] * LN2

    rows = pl.BlockSpec((8, t), lambda h, j: (0, 0))
    hbm = pl.BlockSpec(memory_space=pl.ANY)
    return pl.pallas_call(
        body, name=name, grid=(HEADS, n_t),
        in_specs=[hbm, hbm, pl.BlockSpec((kt, 256), lambda h, j: (j, h)), pl.BlockSpec((kt, 128), lambda h, j: (j, h)),
                  rows, rows],
        out_specs=[pl.BlockSpec((t, 256), lambda h, j: (0, h)), pl.BlockSpec((kt, 256), lambda h, j: (j, h)),
                   pl.BlockSpec((kt, 128), lambda h, j: (j, h))],
        out_shape=[jax.ShapeDtypeStruct((t, 1024), F32), jax.ShapeDtypeStruct((t, 1024), F32),
                   jax.ShapeDtypeStruct((t, 512), F32)],
        scratch_shapes=[pltpu.VMEM((t, 256), BF), pltpu.VMEM((t, 128), BF), pltpu.VMEM((kt, 256), F32),
                        pltpu.VMEM((kt, 128), F32), pltpu.SemaphoreType.DMA((2,))],
        compiler_params=_cparams(2),
    )(q_cat, do_bf, k_cat, v, lse_rows, dl_rows)


def final_loss(xf, target, fnw, name):
    t = xf.shape[0]
    n_t = t // TILE

    def body(x_ref, t_ref, w_ref, loss_ref, dx_ref, dw_ref):
        i = pl.program_id(0)

        @pl.when(i == 0)
        def _():
            loss_ref[...] = jnp.zeros_like(loss_ref)
            dw_ref[...] = jnp.zeros_like(dw_ref)
            dx_ref[...] = jnp.zeros_like(dx_ref)

        @pl.when(i >= 1)
        def _():
            y, vjp_fn = jax.vjp(_rms, x_ref[...], w_ref[...])
            err = y - t_ref[...]
            loss_ref[...] += jnp.broadcast_to(0.5 * jnp.sum(jnp.mean(err * err, axis=-1, keepdims=True)), (8, 128))
            dx, dw = vjp_fn(err * (1.0 / D))
            dx_ref[...] = dx
            dw_ref[...] += dw

    return pl.pallas_call(
        body, name=name, grid=(n_t,),
        in_specs=[pl.BlockSpec((TILE, D), lambda i: (i, 0)), pl.BlockSpec((TILE, D), lambda i: (jnp.maximum(i - 1, 0), 0)),
                  pl.BlockSpec((1, D), lambda i: (0, 0))],
        out_specs=[pl.BlockSpec((8, 128), lambda i: (0, 0)), pl.BlockSpec((TILE, D), lambda i: (i, 0)),
                   pl.BlockSpec((1, D), lambda i: (0, 0))],
        out_shape=[jax.ShapeDtypeStruct((8, 128), F32), jax.ShapeDtypeStruct((t, D), F32),
                   jax.ShapeDtypeStruct((1, D), F32)],
        compiler_params=_cparams(1),
    )(xf, target, fnw)


def all_gather(xs, name):
    n = len(xs)
    blks = [tuple(x.shape) for x in xs]
    per = N_DEV - 1

    def body(*refs):
        x_refs, o_refs = refs[:n], refs[n:2 * n]
        ssem, rsem, lsem = refs[2 * n:]
        xi, yi, ci = lax.axis_index("x"), lax.axis_index("y"), lax.axis_index("c")
        me3 = (xi, yi, ci)
        me = 4 * xi + 2 * yi + ci
        flat = lambda d: 4 * d[0] + 2 * d[1] + d[2]
        sibling = (xi, yi, 1 - ci)
        chips = [(1 - xi, yi), (xi, 1 - yi), (1 - xi, 1 - yi)]

        def copy(a, k, block, to, src=None):
            rows = o_refs[a].at[flat(block)]
            return pltpu.make_async_remote_copy(
                src_ref=rows if src is None else src, dst_ref=rows,
                send_sem=ssem.at[a * per + k], recv_sem=rsem.at[a * per + k],
                device_id=to, device_id_type=pl.DeviceIdType.MESH)

        own = [pltpu.make_async_copy(x_refs[a], o_refs[a].at[me], lsem.at[a]) for a in range(n)]
        first = []
        for a in range(n):
            first.append(copy(a, 0, me3, sibling, src=x_refs[a]))
            first += [copy(a, 1 + j, me3, (*chip, ci), src=x_refs[a]) for j, chip in enumerate(chips)]
        for cp in own + first:
            cp.start()
        passed = []
        for j, chip in enumerate(chips):
            for a in range(n):
                copy(a, 1 + j, (*chip, ci), me3).wait_recv()
                fw = copy(a, 4 + j, (*chip, ci), sibling)
                fw.start()
                passed.append(fw)
        for a in range(n):
            copy(a, 0, sibling, me3).wait_recv()
            for j, chip in enumerate(chips):
                copy(a, 4 + j, (*chip, 1 - ci), me3).wait_recv()
        for cp in first + passed:
            cp.wait_send()
        for cp in own:
            cp.wait()

    hbm = pl.BlockSpec(memory_space=pl.ANY)
    res = pl.pallas_call(
        body, name=name, in_specs=[hbm] * n, out_specs=[hbm] * n,
        out_shape=[jax.ShapeDtypeStruct((N_DEV,) + blks[a], xs[a].dtype) for a in range(n)],
        scratch_shapes=[pltpu.SemaphoreType.DMA((n * per,)), pltpu.SemaphoreType.DMA((n * per,)),
                        pltpu.SemaphoreType.DMA((n,))],
        compiler_params=pltpu.CompilerParams(has_side_effects=True),
    )(*xs)
    return list(res)


def reduce_scatter(xs, name):
    n = len(xs)
    c_idx = lax.axis_index("c").astype(jnp.int32).reshape(1)
    hbm = pl.BlockSpec(memory_space=pl.ANY)
    side = pltpu.CompilerParams(has_side_effects=True)

    def pair_body(*refs):
        x_refs, o_refs = refs[:n], refs[n:2 * n]
        ssem, rsem = refs[2 * n:]
        xi, yi, cc = lax.axis_index("x"), lax.axis_index("y"), lax.axis_index("c")
        cps = [pltpu.make_async_remote_copy(
            src_ref=x_refs[a].at[1 - cc], dst_ref=o_refs[a], send_sem=ssem.at[a], recv_sem=rsem.at[a],
            device_id=(xi, yi, 1 - cc), device_id_type=pl.DeviceIdType.MESH) for a in range(n)]
        for cp in cps:
            cp.start()
        for cp in cps:
            cp.wait()

    halves = [tuple(x.shape[1:]) for x in xs]
    got = pl.pallas_call(
        pair_body, name=name + "_pair", in_specs=[hbm] * n, out_specs=[hbm] * n,
        out_shape=[jax.ShapeDtypeStruct(halves[a], xs[a].dtype) for a in range(n)],
        scratch_shapes=[pltpu.SemaphoreType.DMA((n,)), pltpu.SemaphoreType.DMA((n,))], compiler_params=side,
    )(*xs)

    sums = []
    for a in range(n):
        def add_body(c_ref, a_ref, b_ref, o_ref):
            o_ref[...] = (a_ref[0].astype(F32) + b_ref[...].astype(F32)).astype(o_ref.dtype)

        blk = pl.BlockSpec((1,) + halves[a][1:], lambda q, c_ref: (q, 0, 0))
        own = pl.BlockSpec((1, 1) + halves[a][1:], lambda q, c_ref: (c_ref[0], q, 0, 0))
        sums.append(pl.pallas_call(
            add_body, name=f"{name}_sum{a}",
            grid_spec=pltpu.PrefetchScalarGridSpec(num_scalar_prefetch=1, grid=(4,), in_specs=[own, blk], out_specs=blk),
            out_shape=jax.ShapeDtypeStruct(halves[a], xs[a].dtype), compiler_params=_cparams(1),
        )(c_idx, xs[a], got[a]))

    def chips_body(*refs):
        s_refs, o_refs = refs[:n], refs[n:2 * n]
        ssem, rsem, lsem = refs[2 * n:]
        xi, yi, cc = lax.axis_index("x"), lax.axis_index("y"), lax.axis_index("c")
        my_chip = 2 * xi + yi
        chips = [(1 - xi, yi), (xi, 1 - yi), (1 - xi, 1 - yi)]
        own = [pltpu.make_async_copy(s_refs[a].at[my_chip], o_refs[a].at[my_chip], lsem.at[a]) for a in range(n)]
        sends, lands = [], []
        for a in range(n):
            for j, (px, py) in enumerate(chips):
                common = dict(send_sem=ssem.at[3 * a + j], recv_sem=rsem.at[3 * a + j], device_id=(px, py, cc),
                              device_id_type=pl.DeviceIdType.MESH)
                sends.append(pltpu.make_async_remote_copy(src_ref=s_refs[a].at[2 * px + py],
                                                          dst_ref=o_refs[a].at[my_chip], **common))
                lands.append(pltpu.make_async_remote_copy(src_ref=s_refs[a].at[2 * px + py],
                                                          dst_ref=o_refs[a].at[2 * px + py], **common))
        for cp in own + sends:
            cp.start()
        for cp in lands:
            cp.wait_recv()
        for cp in sends:
            cp.wait_send()
        for cp in own:
            cp.wait()

    res = pl.pallas_call(
        chips_body, name=name + "_chips", in_specs=[hbm] * n, out_specs=[hbm] * n,
        out_shape=[jax.ShapeDtypeStruct(halves[a], xs[a].dtype) for a in range(n)],
        scratch_shapes=[pltpu.SemaphoreType.DMA((3 * n,)), pltpu.SemaphoreType.DMA((3 * n,)),
                        pltpu.SemaphoreType.DMA((n,))], compiler_params=side,
    )(*sums)
    return list(res)


def mod_forward(crows, w_mod, b_shard, name):
    cols = w_mod.shape[2]

    def body(c_ref, w_ref, b_ref, o_ref):
        o_ref[0] = _dot(_silu(c_ref[...]).astype(BF), w_ref[0].astype(BF)) + b_ref[0]

    return pl.pallas_call(
        body, name=name, grid=(2,),
        in_specs=[pl.BlockSpec((16, D), lambda l: (0, 0)), pl.BlockSpec((1, D, cols), lambda l: (l, 0, 0)),
                  pl.BlockSpec((1, 1, cols), lambda l: (l, 0, 0))],
        out_specs=pl.BlockSpec((1, 16, cols), lambda l: (l, 0, 0)),
        out_shape=jax.ShapeDtypeStruct((2, 16, cols), F32), compiler_params=_cparams(1),
    )(crows, w_mod, b_shard)


def mod_backward(crows, w_mod, d_own, d_ctx, name):
    cols = w_mod.shape[2]

    def body(c_ref, w_ref, do_ref, dc_ref, gw_ref, gs_ref):
        dc = dc_ref[0]
        dsum = dc[0:1]
        for s in range(1, N_DEV):
            dsum = dsum + dc[s:s + 1]
        row = lax.broadcasted_iota(jnp.int32, (8, cols), 0)
        d16 = jnp.concatenate([do_ref[0], jnp.where(row == 0, jnp.broadcast_to(dsum, (8, cols)), 0.0)], axis=0)
        gw_ref[0] = _dot_tn(_silu(c_ref[...]).astype(BF), d16.astype(BF))
        gs_ref[0] = _dot_nt(jnp.broadcast_to(dsum, (8, cols)).astype(BF), w_ref[0].astype(BF))

    return pl.pallas_call(
        body, name=name, grid=(2,),
        in_specs=[pl.BlockSpec((16, D), lambda l: (0, 0)), pl.BlockSpec((1, D, cols), lambda l: (l, 0, 0)),
                  pl.BlockSpec((1, 8, cols), lambda l: (l, 0, 0)), pl.BlockSpec((1, 8, cols), lambda l: (l, 0, 0))],
        out_specs=[pl.BlockSpec((1, D, cols), lambda l: (l, 0, 0)), pl.BlockSpec((1, 8, D), lambda l: (l, 0, 0))],
        out_shape=[jax.ShapeDtypeStruct((2, D, cols), F32), jax.ShapeDtypeStruct((2, 8, D), F32)],
        compiler_params=_cparams(1),
    )(crows, w_mod, d_own, d_ctx)


def silu_grad_scale(c_ctx, ds, name):
    def body(c_ref, ds_ref, o_ref):
        cc = c_ref[...]
        sg = jax.nn.sigmoid(cc)
        o_ref[...] = (ds_ref[0][0:1] + ds_ref[1][0:1]) * (sg * (1.0 + cc * (1.0 - sg)))

    return pl.pallas_call(body, name=name, out_shape=jax.ShapeDtypeStruct((1, D), F32))(c_ctx, ds)


def _adamw_math(p_ref, w_ref, m_ref, v_ref, g_ref, d_ref, nm_ref, nv_ref):
    g = p_ref[0].astype(F32)
    for s in range(1, p_ref.shape[0]):
        g = g + p_ref[s].astype(F32)
    mm = ADAM_B1 * m_ref[...] + (1.0 - ADAM_B1) * g
    vv = ADAM_B2 * v_ref[...] + (1.0 - ADAM_B2) * (g * g)
    m_hat = mm / (1.0 - ADAM_B1 ** ADAM_STEP)
    v_hat = vv / (1.0 - ADAM_B2 ** ADAM_STEP)
    g_ref[...] = g
    d_ref[...] = -ADAM_LR * (m_hat / (jnp.sqrt(v_hat) + ADAM_EPS) + ADAM_WD * w_ref[...])
    nm_ref[...] = mm
    nv_ref[...] = vv


def adamw(parts, w, m, v, name):
    n_parts, rows, cols = parts.shape
    lanes = -(-cols // 128) * 128
    block_rows = min(rows, 1 << ((ADAMW_BLOCK_ELEMS // lanes).bit_length() - 1))
    assert rows % block_rows == 0

    def body(*refs):
        _adamw_math(*refs)

    spec = pl.BlockSpec((block_rows, cols), lambda i: (i, 0))
    return pl.pallas_call(
        body, name=name, grid=(rows // block_rows,),
        in_specs=[pl.BlockSpec((n_parts, block_rows, cols), lambda i: (0, i, 0)), spec, spec, spec],
        out_specs=[spec] * 4, out_shape=[jax.ShapeDtypeStruct((rows, cols), F32)] * 4,
        compiler_params=_cparams(1),
    )(parts, w, m, v)


def adamw_group(items, name):
    n = len(items)

    def body(*refs):
        for a in range(n):
            _adamw_math(*refs[4 * a:4 * a + 4], *refs[4 * n + 4 * a:4 * n + 4 * a + 4])

    flat_in = [x for it in items for x in it]
    out_shape = [jax.ShapeDtypeStruct(it[1].shape, F32) for it in items for _ in range(4)]
    res = pl.pallas_call(body, name=name, out_shape=out_shape,
                         compiler_params=pltpu.CompilerParams(vmem_limit_bytes=VMEM_LIMIT_MB * 1024 * 1024))(*flat_in)
    return [tuple(res[4 * a:4 * a + 4]) for a in range(n)]


def _pad_cols(w, segs, total):
    parts, pos = [], 0
    for dst, src, wd in segs:
        if dst > pos:
            parts.append(jnp.zeros(w.shape[:-1] + (dst - pos,), w.dtype))
        parts.append(w[..., src:src + wd])
        pos = dst + wd
    if pos < total:
        parts.append(jnp.zeros(w.shape[:-1] + (total - pos,), w.dtype))
    return jnp.concatenate(parts, axis=-1)


def _unpad_cols(g, segs):
    return jnp.concatenate([g[..., dst:dst + wd] for dst, _, wd in segs], axis=-1)


def _rope_tables(n_lat):
    rows = n_lat // GRID_W
    freq = ROPE_BASE ** (-jnp.arange(16, dtype=F32) * 2.0 / 32)
    a_row = jnp.arange(rows).astype(F32)[:, None] * freq[None, :]
    a_col = jnp.arange(GRID_W).astype(F32)[:, None] * freq[None, :]
    per_row = lambda tbl: jnp.repeat(tbl, GRID_W, axis=0)
    per_col = lambda tbl: jnp.tile(tbl, (rows, 1))
    cr, sr, cc, sc = per_row(jnp.cos(a_row)), per_row(jnp.sin(a_row)), per_col(jnp.cos(a_col)), per_col(jnp.sin(a_col))
    z = jnp.zeros((n_lat, 16), F32)
    cos = jnp.concatenate([cr, cr, cc, cc, jnp.ones((n_lat, 64), F32)], axis=1)
    sa = jnp.concatenate([-sr, z, -sc, z, jnp.zeros((n_lat, 64), F32)], axis=1)
    sb = jnp.concatenate([z, sr, z, sc, jnp.zeros((n_lat, 64), F32)], axis=1)
    ident = lambda fill: jnp.full((TILE, 128), fill, F32)
    return (jnp.concatenate([ident(1.0), cos]), jnp.concatenate([ident(0.0), sa]), jnp.concatenate([ident(0.0), sb]))


def _gathered_to_full(g, name):
    if name in ("w_out", "w_ff2"):
        return jnp.transpose(g, (1, 0, 2, 3)).reshape(2, -1, g.shape[-1])
    return jnp.transpose(g, (1, 2, 0, 3)).reshape(2, g.shape[2], -1)


def _full_to_chunks(gw, name):
    if name in ("w_out", "w_ff2"):
        return jnp.transpose(gw.reshape(2, N_DEV, -1, gw.shape[-1]), (1, 0, 2, 3))
    return jnp.transpose(gw.reshape(2, gw.shape[1], N_DEV, -1), (2, 0, 1, 3))


def kernel(x, c, ctx, c_ctx, w_mod, b_mod, norm1_w, w_in, w_out, sgu_norm_w, sgu_norm_b, sgu_w, sgu_b, gla_wg_fwd, gla_bg_fwd, gla_wg_bwd, gla_bg_bwd, gla_norm_w, mla_q_norm_w, mla_w_uq, mla_kv_norm_w, mla_w_ukv, norm2_w, w_ff1, w_ff2, final_norm_w, loss_target, m_c_ctx, m_w_mod, m_b_mod, m_norm1_w, m_w_in, m_w_out, m_sgu_norm_w, m_sgu_norm_b, m_sgu_w, m_sgu_b, m_gla_wg_fwd, m_gla_bg_fwd, m_gla_wg_bwd, m_gla_bg_bwd, m_gla_norm_w, m_mla_q_norm_w, m_mla_w_uq, m_mla_kv_norm_w, m_mla_w_ukv, m_norm2_w, m_w_ff1, m_w_ff2, m_final_norm_w, v_c_ctx, v_w_mod, v_b_mod, v_norm1_w, v_w_in, v_w_out, v_sgu_norm_w, v_sgu_norm_b, v_sgu_w, v_sgu_b, v_gla_wg_fwd, v_gla_bg_fwd, v_gla_wg_bwd, v_gla_bg_bwd, v_gla_norm_w, v_mla_q_norm_w, v_mla_w_uq, v_mla_kv_norm_w, v_mla_w_ukv, v_norm2_w, v_w_ff1, v_w_ff2, v_final_norm_w):
    W = dict(c_ctx=c_ctx, w_mod=w_mod, b_mod=b_mod, norm1_w=norm1_w, w_in=w_in, w_out=w_out, sgu_norm_w=sgu_norm_w,
             sgu_norm_b=sgu_norm_b, sgu_w=sgu_w, sgu_b=sgu_b, gla_wg_fwd=gla_wg_fwd, gla_bg_fwd=gla_bg_fwd,
             gla_wg_bwd=gla_wg_bwd, gla_bg_bwd=gla_bg_bwd, gla_norm_w=gla_norm_w, mla_q_norm_w=mla_q_norm_w,
             mla_w_uq=mla_w_uq, mla_kv_norm_w=mla_kv_norm_w, mla_w_ukv=mla_w_ukv, norm2_w=norm2_w, w_ff1=w_ff1,
             w_ff2=w_ff2, final_norm_w=final_norm_w)
    M = dict(c_ctx=m_c_ctx, w_mod=m_w_mod, b_mod=m_b_mod, norm1_w=m_norm1_w, w_in=m_w_in, w_out=m_w_out,
             sgu_norm_w=m_sgu_norm_w, sgu_norm_b=m_sgu_norm_b, sgu_w=m_sgu_w, sgu_b=m_sgu_b, gla_wg_fwd=m_gla_wg_fwd,
             gla_bg_fwd=m_gla_bg_fwd, gla_wg_bwd=m_gla_wg_bwd, gla_bg_bwd=m_gla_bg_bwd, gla_norm_w=m_gla_norm_w,
             mla_q_norm_w=m_mla_q_norm_w, mla_w_uq=m_mla_w_uq, mla_kv_norm_w=m_mla_kv_norm_w, mla_w_ukv=m_mla_w_ukv,
             norm2_w=m_norm2_w, w_ff1=m_w_ff1, w_ff2=m_w_ff2, final_norm_w=m_final_norm_w)
    V = dict(c_ctx=v_c_ctx, w_mod=v_w_mod, b_mod=v_b_mod, norm1_w=v_norm1_w, w_in=v_w_in, w_out=v_w_out,
             sgu_norm_w=v_sgu_norm_w, sgu_norm_b=v_sgu_norm_b, sgu_w=v_sgu_w, sgu_b=v_sgu_b, gla_wg_fwd=v_gla_wg_fwd,
             gla_bg_fwd=v_gla_bg_fwd, gla_wg_bwd=v_gla_wg_bwd, gla_bg_bwd=v_gla_bg_bwd, gla_norm_w=v_gla_norm_w,
             mla_q_norm_w=v_mla_q_norm_w, mla_w_uq=v_mla_w_uq, mla_kv_norm_w=v_mla_kv_norm_w, mla_w_ukv=v_mla_w_ukv,
             norm2_w=v_norm2_w, w_ff1=v_w_ff1, w_ff2=v_w_ff2, final_norm_w=v_final_norm_w)

    n_lat = x.shape[1]
    assert ctx.shape[1] == TILE and n_lat % TILE == 0 and x.shape[2] == D
    t_all = TILE + n_lat
    n_t = t_all // TILE
    me = 4 * lax.axis_index("x") + 2 * lax.axis_index("y") + lax.axis_index("c")
    mod_cols = w_mod.shape[2]

    c_all = all_gather([c], "ag_c")[0].reshape(N_DEV, D)
    crows = jnp.concatenate([c_all, c_ctx[None, :], jnp.zeros((7, D), F32)], axis=0)
    b_shard = lax.dynamic_slice_in_dim(b_mod, me * mod_cols, mod_cols, axis=1)[:, None, :]
    mod_sh = mod_forward(crows, w_mod, b_shard, "mod_fwd")
    mod_g = all_gather([mod_sh.reshape(32, mod_cols)], "ag_mod")[0]
    mod_full = jnp.transpose(mod_g.reshape(N_DEV, 2, 16, mod_cols), (1, 2, 0, 3)).reshape(2, 16, 6 * D)
    mod_own = lax.dynamic_index_in_dim(mod_full, me, axis=1, keepdims=False)
    mod_ctx = mod_full[:, 8, :]
    pad2 = jnp.zeros((2, D), F32)
    modl = [jnp.stack([jnp.concatenate([mod_ctx[l].reshape(6, D), pad2]),
                       jnp.concatenate([mod_own[l].reshape(6, D), pad2])]) for l in range(2)]

    v2 = lambda a: a[None, :] if a.ndim == 1 else a.reshape(-1, a.shape[-1])
    gathered = all_gather([v2(W[k].astype(BF)) for k in BIG_NAMES], "ag_weights")
    full = {k: _gathered_to_full(g.reshape((N_DEV,) + W[k].shape), k) for k, g in zip(BIG_NAMES, gathered)}
    w_in_p = _pad_cols(full["w_in"], W_IN_SEGS, P_COLS)
    w_uq_p = _pad_cols(full["mla_w_uq"], W_UQ_SEGS, 1024).astype(F32)
    w_ukv_f = full["mla_w_ukv"].astype(F32)
    wgf_p = jnp.pad(gla_wg_fwd, ((0, 0), (0, 112), (0, 0)))
    wgb_p = jnp.pad(gla_wg_bwd, ((0, 0), (0, 112), (0, 0)))
    sgu_bx = jnp.repeat(jnp.transpose(sgu_b, (0, 2, 1)), 64, axis=2)
    gnw_t = jnp.tile(gla_norm_w, (1, HEADS))
    rc, rsa, rsb = _rope_tables(n_lat)

    xin = jnp.concatenate([ctx[0], x[0]], axis=0)
    row = lambda a: a[None, :]

    def pre_ins(l, xl):
        return [("x", "tile", True, xl), ("mod", "kind", True, modl[l]), ("n1w", "full", True, row(norm1_w[l])),
                ("w_in", "wfull", False, w_in_p[l]), ("sgu_nw", "full", True, row(sgu_norm_w[l])),
                ("sgu_nb", "full", True, row(sgu_norm_b[l])), ("sgu_w", "full", True, sgu_w[l]),
                ("sgu_bx", "full", True, sgu_bx[l]), ("wgf", "full", True, wgf_p[l]), ("bgf", "full", True, row(gla_bg_fwd[l])),
                ("wgb", "full", True, wgb_p[l]), ("bgb", "full", True, row(gla_bg_bwd[l])),
                ("qnw", "full", True, row(mla_q_norm_w[l])), ("w_uq", "full", True, w_uq_p[l]),
                ("kvnw", "full", True, row(mla_kv_norm_w[l])), ("w_ukv", "full", True, w_ukv_f[l]),
                ("rc", "tile", False, rc), ("rsa", "tile", False, rsa), ("rsb", "tile", False, rsb)]

    pre_outs = [("y_sgu", 256, F32), ("qg", 128, F32), ("kg", 128, F32), ("vg", 256, F32), ("lgf", 128, F32),
                ("lgb", 128, F32), ("gr", 256, F32), ("q_cat", 1024, BF), ("k_cat", 1024, BF), ("v", 512, BF)]

    def out_ins(l, xl, a):
        return [("x", "tile", True, xl), ("mod", "kind", True, modl[l]), ("y_sgu", "tile", True, a["y_sgu"]),
                ("o_f", "tile", True, a["o_f"]), ("o_b", "tile", False, a["o_b"]), ("gr", "tile", True, a["gr"]),
                ("y_mla", "tile", True, a["y_mla"]), ("gnw", "full", True, row(gnw_t[l])),
                ("w_out", "wfull", False, full["w_out"][l])]

    def ffn_ins(l, x1):
        return [("x1", "tile", True, x1), ("mod", "kind", True, modl[l]), ("n2w", "full", True, row(norm2_w[l])),
                ("w_ff1", "wcols", False, full["w_ff1"][l]), ("w_ff2", "wrows", False, full["w_ff2"][l])]

    saved, xl = [], xin
    for l in range(2):
        a = tile_forward(pre_tile, f"pre_fwd{l}", t_all, pre_ins(l, xl), pre_outs)
        a["o_f"], a["sf"], a["o_b"], a["sb"] = gla_forward(a["qg"], a["kg"], a["vg"], a["lgf"], a["lgb"], f"gla_fwd{l}")
        a["y_mla"], a["lse"] = mla_forward(a["q_cat"], a["k_cat"], a["v"], f"mla_fwd{l}")
        a["x"] = xl
        a["x1"] = tile_forward(attn_out_tile, f"out_fwd{l}", t_all, out_ins(l, xl, a), [("x1", D, F32)])["x1"]
        ff = tile_forward(ffn_tile, f"ffn_fwd{l}", t_all, ffn_ins(l, a["x1"]), [("x2", D, F32), ("f", D, F32)])
        xl, a["f"] = ff["x2"], ff["f"]
        saved.append(a)

    loss_blk, dx, d_fnw = final_loss(xl, loss_target[0], row(final_norm_w), "final_loss")
    loss = lax.psum(loss_blk[0, 0], AXES)

    G = {}
    dmods = []
    for l in (1, 0):
        a = saved[l]
        dx1, dmod3, dn2w, zpre, zf_t, h2_t, a_ff2 = ffn_backward(
            a["x1"], modl[l], row(norm2_w[l]), full["w_ff1"][l], full["w_ff2"][l], dx, a["f"], f"ffn_bwd{l}")
        gw_ff1 = wgrad(h2_t, zpre, f"wg_ff1_{l}")
        gw_ff2 = jnp.transpose(wgrad(zf_t, a_ff2, f"wg_ff2_{l}"))
        g2, e2 = tile_backward(attn_out_tile, f"out_bwd{l}", t_all, out_ins(l, a["x"], a), [("x1", dx1)],
                               [("zt", D)], [("a_out", D)])
        gw_out = wgrad(e2["a_out"], e2["zt"], f"wg_out_{l}")
        dl_rows, do_bf = mla_delta(g2["y_mla"], a["y_mla"], f"mla_delta{l}")
        dq_cat, dk_cat, dv = mla_backward(a["q_cat"], a["k_cat"], a["v"], a["lse"], dl_rows, do_bf, f"mla_bwd{l}")
        dqf, dkf, dvf, dgf, dqb, dkb, dvb, dgb = gla_backward(
            a["qg"], a["kg"], a["vg"], a["lgf"], a["lgb"], a["sf"], a["sb"], g2["o_f"], f"gla_bwd{l}")
        cots = [("y_sgu", g2["y_sgu"]), ("qg", [dqf, dqb]), ("kg", [dkf, dkb]), ("vg", [dvf, dvb]), ("lgf", dgf),
                ("lgb", dgb), ("gr", g2["gr"]), ("q_cat", dq_cat), ("k_cat", dk_cat), ("v", dv)]
        g1, e1 = tile_backward(pre_tile, f"pre_bwd{l}", t_all, pre_ins(l, a["x"]), cots, [("zp", P_COLS)], [("a_in", D)],
                               resid=("x", g2["x"]))
        gw_in = _unpad_cols(wgrad(e1["a_in"], e1["zp"], f"wg_in_{l}", bk2=P_COLS), W_IN_SEGS)
        dx = g1["x"]
        dmods.append(g1["mod"] + g2["mod"] + dmod3)
        G[l] = dict(w_in=gw_in, w_out=gw_out, w_ff1=gw_ff1, w_ff2=gw_ff2,
                    mla_w_uq=_unpad_cols(g1["w_uq"], W_UQ_SEGS), mla_w_ukv=g1["w_ukv"],
                    norm1_w=g1["n1w"][0], norm2_w=dn2w[0], sgu_norm_w=g1["sgu_nw"][0], sgu_norm_b=g1["sgu_nb"][0],
                    sgu_w=g1["sgu_w"], sgu_b=jnp.transpose(g1["sgu_bx"].reshape(128, HEADS, 64).sum(-1)),
                    gla_wg_fwd=g1["wgf"][:16], gla_bg_fwd=g1["bgf"][0], gla_wg_bwd=g1["wgb"][:16], gla_bg_bwd=g1["bgb"][0],
                    gla_norm_w=g2["gnw"][0].reshape(HEADS, 64).sum(0), mla_q_norm_w=g1["qnw"][0], mla_kv_norm_w=g1["kvnw"][0])
    dmods = dmods[::-1]
    grad_x = dx[TILE:][None]

    dmod_pack = jnp.stack([jnp.stack([dmods[l][1, :6].reshape(-1), dmods[l][0, :6].reshape(-1)]) for l in range(2)])
    dmod_all = all_gather([dmod_pack.reshape(4, 6 * D)], "ag_dmod")[0].reshape(N_DEV, 2, 2, 6 * D)
    dsl = lax.dynamic_slice_in_dim(dmod_all, me * mod_cols, mod_cols, axis=3)
    d_own = jnp.transpose(dsl[:, :, 0, :], (1, 0, 2))
    d_ctx = jnp.transpose(dsl[:, :, 1, :], (1, 0, 2))
    g_w_mod, ds_cc = mod_backward(crows, w_mod, d_own, d_ctx, "mod_bwd")
    g_c_ctx_part = silu_grad_scale(c_ctx[None, :], ds_cc, "silu_bwd")[0]
    g_b_mod_part = jnp.stack([dmods[l][1, :6].reshape(-1) + dmods[l][0, :6].reshape(-1) for l in range(2)])

    small_g = dict(c_ctx=g_c_ctx_part, b_mod=g_b_mod_part, final_norm_w=d_fnw[0])
    for k in SMALL_NAMES:
        if k not in small_g:
            small_g[k] = jnp.stack([G[0][k], G[1][k]])
    res = {}
    sparts = all_gather([v2(small_g[k]) for k in SMALL_NAMES], "ag_small")
    s_out = adamw_group([(sparts[j], v2(W[k]), v2(M[k]), v2(V[k])) for j, k in enumerate(SMALL_NAMES)], "adamw_small")
    for j, k in enumerate(SMALL_NAMES):
        res[k] = [o.reshape(W[k].shape) for o in s_out[j]]

    chunks = []
    for k in BIG_NAMES:
        ch = _full_to_chunks(jnp.stack([G[0][k], G[1][k]]), k).astype(BF)
        chunks.append(jnp.transpose(ch.reshape(4, 2, -1, ch.shape[-1]), (1, 0, 2, 3)))
    bparts = reduce_scatter(chunks, "rs_grads")
    for j, k in enumerate(BIG_NAMES):
        res[k] = [o.reshape(W[k].shape) for o in adamw(bparts[j], v2(W[k]), v2(M[k]), v2(V[k]), f"adamw_{k}")]
    res["w_mod"] = [o.reshape(w_mod.shape)
                    for o in adamw(v2(g_w_mod)[None], v2(w_mod), v2(m_w_mod), v2(v_w_mod), "adamw_w_mod")]
    outs = [loss, grad_x]
    for j in range(4):
        outs += [res[k][j] for k in WEIGHT_ORDER]
    return tuple(outs)
```

```python
import jax
import jax.numpy as jnp
from jax import lax
from jax.experimental import pallas as pl
from jax.experimental.pallas import tpu as pltpu

F32 = jnp.float32
BF = jnp.bfloat16

N_DEV = 8
AXES = ("x", "y", "c")
EPS = 1e-6
D = 1024
TILE = 256
GCH = 128
SGU_CHUNK = 128
HEADS = 4
ROPE_BASE = 10000.0
GRID_W = 64
GLA_TAU = 16.0
ATT_SCALE = (128 + 64) ** -0.5
ATT_SCALE_LOG2 = ATT_SCALE * 1.4426950408889634
LN2 = 0.6931471805599453
KV_CH = 512
KV_CH_FWD = 2048
QT_FWD = 256
Q_CH_BWD = 1024
KT_BWD = 768
MLA_UNROLL = 2
D_FF = 4096
FF_CH = 1024

ADAM_LR = 0.001
ADAM_B1 = 0.9
ADAM_B2 = 0.999
ADAM_EPS = 1e-08
ADAM_WD = 0.01
ADAM_STEP = 10

VMEM_LIMIT_MB = 56
ADAMW_BLOCK_ELEMS = 256 * 1024

W_IN_SEGS = ((0, 0, 128), (128, 128, 256), (384, 384, 16), (512, 400, 16), (640, 416, 256), (896, 672, 64),
             (1024, 736, 256), (1280, 992, 256), (1536, 1248, 128), (1664, 1376, 256), (1920, 1632, 256))
P_COLS = 2176
O_GK, O_GV, O_GGF, O_GGB, O_CKV, O_KR, O_SU, O_SV, O_GQ, O_GR, O_DQ = (s[0] for s in W_IN_SEGS)
W_UQ_SEGS = tuple((h * 256, h * 192, 192) for h in range(HEADS))

SMALL_NAMES = ("c_ctx", "b_mod", "norm1_w", "sgu_norm_w", "sgu_norm_b", "sgu_w", "sgu_b", "gla_wg_fwd", "gla_bg_fwd",
               "gla_wg_bwd", "gla_bg_bwd", "gla_norm_w", "mla_q_norm_w", "mla_kv_norm_w", "norm2_w", "final_norm_w")
BIG_NAMES = ("w_in", "w_out", "mla_w_uq", "mla_w_ukv", "w_ff1", "w_ff2")
WEIGHT_ORDER = ("c_ctx", "w_mod", "b_mod", "norm1_w", "w_in", "w_out", "sgu_norm_w", "sgu_norm_b", "sgu_w", "sgu_b",
                "gla_wg_fwd", "gla_bg_fwd", "gla_wg_bwd", "gla_bg_bwd", "gla_norm_w", "mla_q_norm_w", "mla_w_uq",
                "mla_kv_norm_w", "mla_w_ukv", "norm2_w", "w_ff1", "w_ff2", "final_norm_w")


def _cparams(n_axes):
    return pltpu.CompilerParams(dimension_semantics=("arbitrary",) * n_axes,
                                vmem_limit_bytes=VMEM_LIMIT_MB * 1024 * 1024)


def _dot(a, b):
    return jnp.dot(a, b, preferred_element_type=F32)


def _dot_nt(a, b):
    return lax.dot_general(a, b, (((1,), (1,)), ((), ())), preferred_element_type=F32)


def _dot_tn(a, b):
    return lax.dot_general(a, b, (((0,), (0,)), ((), ())), preferred_element_type=F32)


def _nn(a, b):
    return _dot(a.astype(BF), b.astype(BF))


def _nt(a, b):
    return _dot_nt(a.astype(BF), b.astype(BF))


def _tn(a, b):
    return _dot_tn(a.astype(BF), b.astype(BF))


nn_d = jax.custom_vjp(_nn)
nt_d = jax.custom_vjp(_nt)
tn_d = jax.custom_vjp(_tn)
nn_d.defvjp(lambda a, b: (_nn(a, b), (a, b)), lambda r, dy: (_nt(dy, r[1]), _tn(r[0], dy)))
nt_d.defvjp(lambda a, b: (_nt(a, b), (a, b)), lambda r, dy: (_nn(dy, r[1]), _tn(dy, r[0])))
tn_d.defvjp(lambda a, b: (_tn(a, b), (a, b)), lambda r, dy: (_nt(r[1], dy), _nn(r[0], dy)))


def nn_const(w_bf, diff):
    def raw(a):
        return _dot(a.astype(BF), w_bf)

    if not diff:
        return raw
    f = jax.custom_vjp(raw)
    f.defvjp(lambda a: (raw(a), None), lambda _, dy: (_dot_nt(dy.astype(BF), w_bf),))
    return f


def _split3(g):
    hi = g.astype(BF)
    r = g - hi.astype(F32)
    mid = r.astype(BF)
    lo = (r - mid.astype(F32)).astype(BF)
    return hi, mid, lo


def make_cum(tri_bf, tri_t_bf, diff):
    def raw(g, t):
        hi, mid, lo = _split3(g)
        return _dot(t, hi) + _dot(t, mid) + _dot(t, lo)

    def fwd(g):
        return raw(g, tri_bf)

    if not diff:
        return fwd
    cum = jax.custom_vjp(fwd)
    cum.defvjp(lambda g: (fwd(g), None), lambda _, db: (raw(db, tri_t_bf),))
    return cum


def _roll_lanes(x, shift):
    return pltpu.roll(x, shift, 1)


def make_rope(c, sa, sb, diff):
    def raw(x):
        return x * c + _roll_lanes(x, 112) * sa + _roll_lanes(x, 16) * sb

    if not diff:
        return raw
    f = jax.custom_vjp(raw)
    f.defvjp(lambda x: (raw(x), None),
             lambda _, dy: (dy * c + _roll_lanes(dy * sa, 16) + _roll_lanes(dy * sb, 112),))
    return f


def _ops(diff):
    return (nn_d, nt_d, tn_d) if diff else (_nn, _nt, _tn)


def _rms(x, w):
    return x * lax.rsqrt(jnp.mean(x * x, axis=-1, keepdims=True) + EPS) * w


def _gelu(x):
    return 0.5 * x * (1.0 + jnp.tanh(0.7978845608028654 * (x + 0.044715 * (x * x * x))))


def _silu(x):
    return x * jax.nn.sigmoid(x)


def _log_sigmoid(z):
    return jnp.minimum(z, 0.0) - jnp.log(1.0 + jnp.exp(-jnp.abs(z)))


def _lane_group_mask(width, group, h):
    lane = lax.broadcasted_iota(jnp.int32, (1, width), 1)
    return ((lane >= h * group) & (lane < (h + 1) * group)).astype(F32)


def pre_tile(d, c, z):
    nn, _, _ = _ops(z is not None)
    rope = make_rope(c["rc"], c["rsa"], c["rsb"], z is not None)
    mod = d["mod"]
    h = _rms(d["x"], d["n1w"]) * (1.0 + mod[1:2]) + mod[0:1]
    p = nn_const(c["w_in"], z is not None)(h)
    if z is not None:
        p = p + z["zp"]
    gk, gv = p[:, O_GK:O_GK + 128], p[:, O_GV:O_GV + 256]
    ggf, ggb = p[:, O_GGF:O_GGF + 128], p[:, O_GGB:O_GGB + 128]
    ckv, kr = p[:, O_CKV:O_CKV + 256], p[:, O_KR:O_KR + 128]
    su, sv = p[:, O_SU:O_SU + 256], p[:, O_SV:O_SV + 256]
    gq, gr, dq = p[:, O_GQ:O_GQ + 128], p[:, O_GR:O_GR + 256], p[:, O_DQ:O_DQ + 256]

    u = _gelu(su)
    gv_ = _gelu(sv)
    mu = jnp.mean(gv_, axis=-1, keepdims=True)
    cen = gv_ - mu
    vn = cen * lax.rsqrt(jnp.mean(cen * cen, axis=-1, keepdims=True) + EPS) * d["sgu_nw"] + d["sgu_nb"]
    hm = [_lane_group_mask(256, 64, hh) for hh in range(HEADS)]
    rows = []
    for ci in range(vn.shape[0] // SGU_CHUNK):
        vc = vn[ci * SGU_CHUNK:(ci + 1) * SGU_CHUNK]
        s = d["sgu_bx"]
        for hh in range(HEADS):
            s = s + hm[hh] * nn(d["sgu_w"][hh], vc)
        rows.append(s)
    y_sgu = u * jnp.concatenate(rows, axis=0)

    qg = gq * (32 ** -0.5)
    lgf = _log_sigmoid(nn(ggf, d["wgf"]) + d["bgf"]) * (1.0 / GLA_TAU)
    lgb = _log_sigmoid(nn(ggb, d["wgb"]) + d["bgb"]) * (1.0 / GLA_TAU)

    kv = nn(_rms(ckv, d["kvnw"]), d["w_ukv"])
    kr_r = rope(kr)
    q = nn(_rms(dq, d["qnw"]), d["w_uq"])
    qs, ks, vs = [], [], []
    for hh in range(HEADS):
        qs += [q[:, hh * 256:hh * 256 + 128], rope(q[:, hh * 256 + 128:(hh + 1) * 256])]
        ks += [kv[:, hh * 256:hh * 256 + 128], kr_r]
        vs += [kv[:, hh * 256 + 128:(hh + 1) * 256]]
    outs = dict(y_sgu=y_sgu, qg=qg, kg=gk, vg=gv, lgf=lgf, lgb=lgb, gr=gr,
                q_cat=jnp.concatenate(qs, axis=-1) * ATT_SCALE_LOG2, k_cat=jnp.concatenate(ks, axis=-1), v=jnp.concatenate(vs, axis=-1))
    return outs, dict(a_in=h)


def attn_out_tile(d, c, z):
    mod = d["mod"]
    o = d["o_f"] + c["o_b"]
    ms = jnp.zeros_like(o)
    for hh in range(HEADS):
        m_h = _lane_group_mask(256, 64, hh)
        ms = ms + m_h * (jnp.sum(o * o * m_h, axis=-1, keepdims=True) * (1.0 / 64))
    yg = o * lax.rsqrt(ms + EPS) * d["gnw"] * _silu(d["gr"])
    y = jnp.concatenate([d["y_sgu"], yg, d["y_mla"]], axis=-1)
    t = nn_const(c["w_out"], z is not None)(y)
    if z is not None:
        t = t + z["zt"]
    return dict(x1=d["x"] + mod[2:3] * t), dict(a_out=y)


def ffn_tile(d, c, z):
    mod = d["mod"]
    h2 = _rms(d["x1"], d["n2w"]) * (1.0 + mod[4:5]) + mod[3:4]
    f = None
    for j in range(D_FF // FF_CH):
        a = jnp.maximum(nn_const(c["w_ff1"][j], False)(h2), 0.0)
        fj = nn_const(c["w_ff2"][j], False)(a * a)
        f = fj if f is None else f + fj
    return dict(x2=d["x1"] + mod[5:6] * f, f=f), {}


def _in_spec(kind, arr, tile):
    if kind == "tile":
        return pl.BlockSpec((tile, arr.shape[1]), lambda i: (i, 0))
    if kind == "kind":
        return pl.BlockSpec((1,) + arr.shape[1:], lambda i: (jnp.where(i < TILE // tile, 0, 1), 0, 0))
    nd = arr.ndim
    if kind in ("wfull", "wcols", "wrows"):
        return pl.BlockSpec(arr.shape, lambda i: (0,) * nd, pipeline_mode=pl.Buffered(1))
    return pl.BlockSpec(arr.shape, lambda i: (0,) * nd)


def _load(kind, ref):
    if kind == "kind":
        return ref[0]
    if kind == "wcols":
        return [ref[:, j * FF_CH:(j + 1) * FF_CH] for j in range(ref.shape[1] // FF_CH)]
    if kind == "wrows":
        return [ref[j * FF_CH:(j + 1) * FF_CH, :] for j in range(ref.shape[0] // FF_CH)]
    return ref[...]


def tile_forward(fn, name, t_all, ins, out_defs, tile=TILE):
    keys = [k for k, _, _, _ in ins]
    kinds = [kd for _, kd, _, _ in ins]
    diffs = [df for _, _, df, _ in ins]
    arrs = [a for _, _, _, a in ins]
    n_in = len(ins)

    def body(*refs):
        vals = [_load(kinds[j], refs[j]) for j in range(n_in)]
        d = {keys[j]: vals[j] for j in range(n_in) if diffs[j]}
        c = {keys[j]: vals[j] for j in range(n_in) if not diffs[j]}
        outs, _ = fn(d, c, None)
        for j, (k, _, dt) in enumerate(out_defs):
            refs[n_in + j][...] = outs[k].astype(dt)

    res = pl.pallas_call(
        body, name=name, grid=(t_all // tile,),
        in_specs=[_in_spec(kinds[j], arrs[j], tile) for j in range(n_in)],
        out_specs=[pl.BlockSpec((tile, w), lambda i: (i, 0)) for _, w, _ in out_defs],
        out_shape=[jax.ShapeDtypeStruct((t_all, w), dt) for _, w, dt in out_defs],
        compiler_params=_cparams(1),
    )(*arrs)
    return {k: r for (k, _, _), r in zip(out_defs, res)}


def tile_backward(fn, name, t_all, ins, cots, z_defs, aux_defs, tile=TILE, resid=None):
    keys = [k for k, _, _, _ in ins]
    kinds = [kd for _, kd, _, _ in ins]
    diffs = [df for _, _, df, _ in ins]
    arrs = [a for _, _, _, a in ins]
    cot_keys, cot_arrs = [], []
    for k, a in cots:
        for one in (a if isinstance(a, (list, tuple)) else [a]):
            cot_keys.append(k)
            cot_arrs.append(one)
    if resid is not None:
        cot_keys.append("resid:" + resid[0])
        cot_arrs.append(resid[1])
    n_in, n_cot = len(ins), len(cot_arrs)
    dkeys = [j for j in range(n_in) if diffs[j]]
    ctx_tiles = TILE // tile

    def body(*refs):
        i = pl.program_id(0)
        vals = [_load(kinds[j], refs[j]) for j in range(n_in)]
        d = {keys[j]: vals[j] for j in dkeys}
        c = {keys[j]: vals[j] for j in range(n_in) if not diffs[j]}
        zs = {k: jnp.zeros((tile, w), F32) for k, w in z_defs}
        outs, vjp_fn, aux = jax.vjp(lambda dd, zz: fn(dd, c, zz), d, zs, has_aux=True)
        ct = {}
        for j, k in enumerate(cot_keys):
            ct[k] = refs[n_in + j][...] + ct[k] if k in ct else refs[n_in + j][...]
        dd, dz = vjp_fn({k: ct[k].astype(outs[k].dtype) for k in outs})
        base = n_in + n_cot
        for n, j in enumerate(dkeys):
            ref, g = refs[base + n], dd[keys[j]]
            if kinds[j] == "tile":
                ref[...] = g + ct["resid:" + keys[j]] if "resid:" + keys[j] in ct else g
            else:
                first = ((i == 0) | (i == ctx_tiles)) if kinds[j] == "kind" else (i == 0)
                gv = g[None] if kinds[j] == "kind" else g

                @pl.when(first)
                def _(ref=ref, gv=gv):
                    ref[...] = gv

                @pl.when(jnp.logical_not(first))
                def _(ref=ref, gv=gv):
                    ref[...] += gv
        base += len(dkeys)
        for n, (k, _) in enumerate(z_defs):
            refs[base + n][...] = dz[k].astype(BF)
        base += len(z_defs)
        for n, (k, _) in enumerate(aux_defs):
            refs[base + n][...] = aux[k].T.astype(BF)

    out_specs, out_shape = [], []
    for j in dkeys:
        out_specs.append(_in_spec(kinds[j], arrs[j], tile))
        out_shape.append(jax.ShapeDtypeStruct(arrs[j].shape, F32))
    for _, w in z_defs:
        out_specs.append(pl.BlockSpec((tile, w), lambda i: (i, 0)))
        out_shape.append(jax.ShapeDtypeStruct((t_all, w), BF))
    for _, w in aux_defs:
        out_specs.append(pl.BlockSpec((w, tile), lambda i: (0, i)))
        out_shape.append(jax.ShapeDtypeStruct((w, t_all), BF))
    res = pl.pallas_call(
        body, name=name, grid=(t_all // tile,),
        in_specs=[_in_spec(kinds[j], arrs[j], tile) for j in range(n_in)]
        + [pl.BlockSpec((tile, a.shape[1]), lambda i: (i, 0)) for a in cot_arrs],
        out_specs=out_specs, out_shape=out_shape, compiler_params=_cparams(1),
    )(*arrs, *cot_arrs)
    grads = {keys[j]: res[n] for n, j in enumerate(dkeys)}
    extra = {k: res[len(dkeys) + n] for n, (k, _) in enumerate(list(z_defs) + list(aux_defs))}
    return grads, extra


def ffn_backward(x1, modl, n2w, w1, w2, dx2, f, name):
    t_all = x1.shape[0]
    n_ch = D_FF // FF_CH

    def head(x, mod, nw):
        return _rms(x, nw) * (1.0 + mod[4:5]) + mod[3:4]

    def body(x_ref, mod_ref, nw_ref, w1_ref, w2_ref, dx2_ref, f_ref, dx1_ref, dmod_ref, dnw_ref, zpre_ref, zf_ref, a1_ref,
             a2_ref):
        i = pl.program_id(0)
        mod = mod_ref[0]
        dx2 = dx2_ref[...]
        h2, vjp_head = jax.vjp(head, x_ref[...], mod, nw_ref[...])
        h2b = h2.astype(BF)
        dfb = (dx2 * mod[5:6]).astype(BF)
        f = f_ref[...]
        dh2 = jnp.zeros((TILE, D), F32)
        for j in range(n_ch):
            cs = slice(j * FF_CH, (j + 1) * FF_CH)
            a = jnp.maximum(_dot(h2b, w1_ref[:, cs]), 0.0)
            a2b = (a * a).astype(BF)
            dpre = (_dot_nt(dfb, w2_ref[cs, :]) * (2.0 * a)).astype(BF)
            dh2 = dh2 + _dot_nt(dpre, w1_ref[:, cs])
            zpre_ref[:, cs] = dpre
            a2_ref[:, cs] = a2b
        zf_ref[...] = (dx2 * mod[5:6]).T.astype(BF)
        a1_ref[...] = h2.T.astype(BF)
        dx1, dmod, dnw = vjp_head(dh2)
        dx1_ref[...] = dx2 + dx1
        row = lax.broadcasted_iota(jnp.int32, (8, D), 0)
        dmod = dmod + jnp.where(row == 5, jnp.sum(dx2 * f, axis=0, keepdims=True), 0.0)
        first_kind = (i == 0) | (i == 1)

        @pl.when(first_kind)
        def _():
            dmod_ref[0] = dmod

        @pl.when(jnp.logical_not(first_kind))
        def _():
            dmod_ref[0] += dmod

        @pl.when(i == 0)
        def _():
            dnw_ref[...] = dnw

        @pl.when(i > 0)
        def _():
            dnw_ref[...] += dnw

    tspec = lambda w: pl.BlockSpec((TILE, w), lambda i: (i, 0))
    once = lambda shp: pl.BlockSpec(shp, lambda i: (0, 0), pipeline_mode=pl.Buffered(1))
    kind = pl.BlockSpec((1, 8, D), lambda i: (jnp.minimum(i, 1), 0, 0))
    tr = pl.BlockSpec((D, TILE), lambda i: (0, i))
    return pl.pallas_call(
        body, name=name, grid=(t_all // TILE,),
        in_specs=[tspec(D), kind, pl.BlockSpec((1, D), lambda i: (0, 0)), once((D, D_FF)), once((D_FF, D)), tspec(D),
                  tspec(D)],
        out_specs=[tspec(D), kind, pl.BlockSpec((1, D), lambda i: (0, 0)), tspec(D_FF), tr, tr, tspec(D_FF)],
        out_shape=[jax.ShapeDtypeStruct((t_all, D), F32), jax.ShapeDtypeStruct((2, 8, D), F32),
                   jax.ShapeDtypeStruct((1, D), F32), jax.ShapeDtypeStruct((t_all, D_FF), BF),
                   jax.ShapeDtypeStruct((D, t_all), BF), jax.ShapeDtypeStruct((D, t_all), BF),
                   jax.ShapeDtypeStruct((t_all, D_FF), BF)],
        compiler_params=_cparams(1),
    )(x1, modl, n2w, w1, w2, dx2, f)


WG_TOK = 768


def wgrad(at, b, name, bk2=1024):
    k1, t = at.shape
    k2 = b.shape[1]
    bk2 = min(bk2, k2)
    tt = WG_TOK if t % WG_TOK == 0 else TILE
    nt_ = t // tt

    def body(a_ref, b_ref, o_ref, acc):
        s = pl.program_id(1)

        @pl.when(s == 0)
        def _():
            acc[...] = jnp.zeros_like(acc)

        acc[...] += _dot(a_ref[...], b_ref[...])

        @pl.when(s == nt_ - 1)
        def _():
            o_ref[...] = acc[...]

    return pl.pallas_call(
        body, name=name, grid=(k2 // bk2, nt_),
        in_specs=[pl.BlockSpec((k1, tt), lambda j, s: (0, s)), pl.BlockSpec((tt, bk2), lambda j, s: (s, j))],
        out_specs=pl.BlockSpec((k1, bk2), lambda j, s: (0, j)),
        out_shape=jax.ShapeDtypeStruct((k1, k2), F32),
        scratch_shapes=[pltpu.VMEM((k1, bk2), F32)],
        compiler_params=_cparams(2),
    )(at, b)


def _gla_consts(reverse, diff):
    r = lax.broadcasted_iota(jnp.int32, (GCH, GCH), 0)
    cc = lax.broadcasted_iota(jnp.int32, (GCH, GCH), 1)
    low = (r >= cc)
    tri = (jnp.logical_not(low) | (r == cc)) if reverse else low
    tri_f = tri.astype(F32)
    tri_t = (cc >= r) if not reverse else (cc <= r)
    hmk = [_lane_group_mask(128, 32, h) for h in range(HEADS)]
    hmv = [_lane_group_mask(256, 64, h) for h in range(HEADS)]
    e = lax.broadcasted_iota(jnp.int32, (256, 128), 0) // 64
    dk = lax.broadcasted_iota(jnp.int32, (256, 128), 1) // 32
    return dict(cum=make_cum(tri_f.astype(BF), tri_t.astype(F32).astype(BF), diff), ops=_ops(diff), reverse=reverse,
                tri4=jnp.concatenate([tri_f] * HEADS, axis=0), hmk=hmk, hmv=hmv, bd=(e == dk).astype(F32))


def gla_chunk(st, q, k, v, g, cs):
    nn, nt, tn = cs["ops"]
    b = cs["cum"](g)
    bl = jnp.sum(g, axis=0, keepdims=True)
    b_ref = jnp.sum(g[GCH // 2:] if cs["reverse"] else g[:GCH // 2], axis=0, keepdims=True)
    qe = q * jnp.exp(b)
    qs = q * jnp.exp(b - b_ref)
    ks = k * jnp.exp(b_ref - b)
    qstack = jnp.concatenate([qs * cs["hmk"][h] for h in range(HEADS)], axis=0)
    att = nt(qstack, ks) * cs["tri4"]
    ofull = nn(att, v)
    o = nt(qe, st)
    for h in range(HEADS):
        o = o + ofull[h * GCH:(h + 1) * GCH] * cs["hmv"][h]
    kd = k * jnp.exp(bl - b)
    st_new = st * jnp.exp(bl) + tn(v, kd) * cs["bd"]
    return st_new, o


def _gla_chunk_index(s, n_ch, reverse):
    ctx_ch = TILE // GCH
    if not reverse:
        return s
    return jnp.where(s < ctx_ch, ctx_ch - 1 - s, n_ch - 1 + ctx_ch - s)


def gla_forward(q, k, v, gf, gb, name):
    t = q.shape[0]
    n_ch = t // GCH

    def body(*refs):
        s = pl.program_id(0)
        for dr, reverse in enumerate((False, True)):
            q_ref, k_ref, v_ref, g_ref = refs[4 * dr:4 * dr + 4]
            o_ref, sst_ref = refs[8 + 2 * dr:8 + 2 * dr + 2]
            st = refs[12 + dr]

            @pl.when(s == 0)
            def _(st=st):
                st[...] = jnp.zeros_like(st)

            cur = st[...]
            sst_ref[0] = cur
            st_new, o = gla_chunk(cur, q_ref[...], k_ref[...], v_ref[...], g_ref[...], _gla_consts(reverse, False))
            o_ref[...] = o
            st[...] = st_new

    in_specs, out_specs, out_shape = [], [], []
    for reverse in (False, True):
        im = lambda s, reverse=reverse: (_gla_chunk_index(s, n_ch, reverse), 0)
        im3 = lambda s, reverse=reverse: (_gla_chunk_index(s, n_ch, reverse), 0, 0)
        in_specs += [pl.BlockSpec((GCH, 128), im), pl.BlockSpec((GCH, 128), im), pl.BlockSpec((GCH, 256), im),
                     pl.BlockSpec((GCH, 128), im)]
        out_specs += [pl.BlockSpec((GCH, 256), im), pl.BlockSpec((1, 256, 128), im3)]
        out_shape += [jax.ShapeDtypeStruct((t, 256), F32), jax.ShapeDtypeStruct((n_ch, 256, 128), F32)]
    return pl.pallas_call(
        body, name=name, grid=(n_ch,), in_specs=in_specs, out_specs=out_specs, out_shape=out_shape,
        scratch_shapes=[pltpu.VMEM((256, 128), F32), pltpu.VMEM((256, 128), F32)],
        compiler_params=_cparams(1),
    )(q, k, v, gf, q, k, v, gb)


def gla_backward(q, k, v, gf, gb, sst_f, sst_b, do, name):
    t = q.shape[0]
    n_ch = t // GCH

    def body(*refs):
        r = pl.program_id(0)
        for dr, reverse in enumerate((False, True)):
            q_ref, k_ref, v_ref, g_ref, sst_ref, do_ref = refs[6 * dr:6 * dr + 6]
            outs = refs[12 + 4 * dr:12 + 4 * dr + 4]
            dst = refs[20 + dr]

            @pl.when(r == 0)
            def _(dst=dst):
                dst[...] = jnp.zeros_like(dst)

            cs = _gla_consts(reverse, True)
            _, vjp_fn = jax.vjp(lambda a, b, c_, d_, e_, cs=cs: gla_chunk(a, b, c_, d_, e_, cs),
                                sst_ref[0], q_ref[...], k_ref[...], v_ref[...], g_ref[...])
            grads = vjp_fn((dst[...], do_ref[...]))
            for o_ref, gval in zip(outs, grads[1:]):
                o_ref[...] = gval
            dst[...] = grads[0]

    in_specs, out_specs, out_shape = [], [], []
    for reverse in (False, True):
        im = lambda r, reverse=reverse: (_gla_chunk_index(n_ch - 1 - r, n_ch, reverse), 0)
        im3 = lambda r, reverse=reverse: (_gla_chunk_index(n_ch - 1 - r, n_ch, reverse), 0, 0)
        in_specs += [pl.BlockSpec((GCH, 128), im), pl.BlockSpec((GCH, 128), im), pl.BlockSpec((GCH, 256), im),
                     pl.BlockSpec((GCH, 128), im), pl.BlockSpec((1, 256, 128), im3), pl.BlockSpec((GCH, 256), im)]
        out_specs += [pl.BlockSpec((GCH, 128), im), pl.BlockSpec((GCH, 128), im), pl.BlockSpec((GCH, 256), im),
                      pl.BlockSpec((GCH, 128), im)]
        out_shape += [jax.ShapeDtypeStruct((t, 128), F32), jax.ShapeDtypeStruct((t, 128), F32),
                      jax.ShapeDtypeStruct((t, 256), F32), jax.ShapeDtypeStruct((t, 128), F32)]
    return pl.pallas_call(
        body, name=name, grid=(n_ch,), in_specs=in_specs, out_specs=out_specs, out_shape=out_shape,
        scratch_shapes=[pltpu.VMEM((256, 128), F32), pltpu.VMEM((256, 128), F32)],
        compiler_params=_cparams(1),
    )(q, k, v, gf, sst_f, do, q, k, v, gb, sst_b, do)


def _resident(hbm_ref, vmem_ref, sem):
    cp = pltpu.make_async_copy(hbm_ref, vmem_ref, sem)
    cp.start()
    cp.wait()


def mla_forward(q_cat, k_cat, v, name):
    t = q_cat.shape[0]
    qt = QT_FWD if t % QT_FWD == 0 else TILE
    n_t = t // qt

    ch = KV_CH_FWD if (t - TILE) % KV_CH_FWD == 0 else KV_CH
    n_main = (t - TILE) // ch

    def body(q_ref, k_hbm, v_hbm, o_ref, lse_ref, k_s, v_s, m_s, l_s, acc_s, sem):
        i = pl.program_id(0)

        @pl.when(i == 0)
        def _():
            _resident(k_hbm, k_s, sem.at[0])
            _resident(v_hbm, v_s, sem.at[1])

        m_s[...] = jnp.full(m_s.shape, -1e30, F32)
        l_s[...] = jnp.zeros_like(l_s)
        acc_s[...] = jnp.zeros_like(acc_s)

        def chunk(r0, size, hide_ctx_rows=False):
            for h in range(HEADS):
                kh = k_s[pl.ds(r0, size), h * 256:(h + 1) * 256]
                vh = v_s[pl.ds(r0, size), h * 128:(h + 1) * 128]
                s = _dot_nt(q_ref[:, h * 256:(h + 1) * 256], kh)
                if hide_ctx_rows:
                    s = jnp.where(lax.broadcasted_iota(jnp.int32, (qt, 1), 0) < TILE, -1e30, s)
                m_prev = m_s[h]
                m_col = jnp.maximum(jnp.max(m_prev, axis=-1, keepdims=True), jnp.max(s, axis=-1, keepdims=True))
                m_next = jnp.broadcast_to(m_col, m_prev.shape)
                p = jnp.exp2(s - m_col)
                alpha = jnp.exp2(m_prev - m_next)
                l_s[h] = alpha * l_s[h] + jnp.sum(p, axis=-1, keepdims=True)
                acc_s[h] = alpha * acc_s[h] + _dot(p.astype(BF), vh)
                m_s[h] = m_next

        chunk(0, TILE)

        def main_loop(hide, size, unroll):
            def step(c, carry):
                chunk(pl.multiple_of(TILE + c * size, TILE), size, hide)
                return carry

            lax.fori_loop(0, (t - TILE) // size, step, 0, unroll=unroll)

        if qt > TILE:
            pl.when(i == 0)(lambda: main_loop(True, KV_CH, 1))
        pl.when(i >= 1)(lambda: main_loop(False, ch, MLA_UNROLL))

        lane = lax.broadcasted_iota(jnp.int32, (qt, 128), 1)
        cols = jnp.zeros((qt, 128), F32)
        for h in range(HEADS):
            o_ref[:, h * 128:(h + 1) * 128] = acc_s[h] / l_s[h]
            cols = jnp.where(lane == h, m_s[h] + jnp.log2(l_s[h]), cols)
        lse_ref[...] = cols.T[0:8, :]

    return pl.pallas_call(
        body, name=name, grid=(n_t,),
        in_specs=[pl.BlockSpec((qt, 1024), lambda i: (i, 0)), pl.BlockSpec(memory_space=pl.ANY),
                  pl.BlockSpec(memory_space=pl.ANY)],
        out_specs=[pl.BlockSpec((qt, 512), lambda i: (i, 0)), pl.BlockSpec((8, qt), lambda i: (0, i))],
        out_shape=[jax.ShapeDtypeStruct((t, 512), F32), jax.ShapeDtypeStruct((8, t), F32)],
        scratch_shapes=[pltpu.VMEM((t, 1024), BF), pltpu.VMEM((t, 512), BF), pltpu.VMEM((HEADS, qt, 128), F32),
                        pltpu.VMEM((HEADS, qt, 128), F32), pltpu.VMEM((HEADS, qt, 128), F32),
                        pltpu.SemaphoreType.DMA((2,))],
        compiler_params=_cparams(1),
    )(q_cat, k_cat, v)


def mla_delta(do, o, name):
    t = do.shape[0]

    def body(do_ref, o_ref, dl_ref, dob_ref):
        d = do_ref[...]
        prod = d * o_ref[...]
        rows = [jnp.sum(prod[:, h * 128:(h + 1) * 128], axis=-1, keepdims=True) for h in range(HEADS)]
        cols = jnp.concatenate(rows + [jnp.zeros((TILE, 128 - HEADS), F32)], axis=-1)
        dl_ref[...] = cols.T[0:8, :]
        dob_ref[...] = d.astype(BF)

    return pl.pallas_call(
        body, name=name, grid=(t // TILE,),
        in_specs=[pl.BlockSpec((TILE, 512), lambda i: (i, 0)), pl.BlockSpec((TILE, 512), lambda i: (i, 0))],
        out_specs=[pl.BlockSpec((8, TILE), lambda i: (0, i)), pl.BlockSpec((TILE, 512), lambda i: (i, 0))],
        out_shape=[jax.ShapeDtypeStruct((8, t), F32), jax.ShapeDtypeStruct((t, 512), BF)],
        compiler_params=_cparams(1),
    )(do, o)


def mla_backward(q_cat, k_cat, v, lse_rows, dl_rows, do_bf, name):
    t = q_cat.shape[0]
    kt = KT_BWD if t % KT_BWD == 0 else TILE
    n_t = t // kt
    ch = Q_CH_BWD if (t - TILE) % Q_CH_BWD == 0 else KV_CH
    n_main = (t - TILE) // ch

    def body(q_hbm, do_hbm, k_ref, v_ref, lse_ref, dl_ref, dq_ref, dk_ref, dv_ref, q_s, do_s, dk_s, dv_s, sem):
        h, j = pl.program_id(0), pl.program_id(1)

        @pl.when(j == 0)
        def _():
            _resident(q_hbm.at[:, pl.ds(pl.multiple_of(h * 256, 256), 256)], q_s, sem.at[0])
            _resident(do_hbm.at[:, pl.ds(pl.multiple_of(h * 128, 128), 128)], do_s, sem.at[1])
            dq_ref[...] = jnp.zeros_like(dq_ref)

        dk_s[...] = jnp.zeros_like(dk_s)
        dv_s[...] = jnp.zeros_like(dv_s)
        kh = k_ref[...]
        vh = v_ref[...]

        def chunk(r0, size, ctx_only=False):
            qh = q_s[pl.ds(r0, size), :]
            doh = do_s[pl.ds(r0, size), :]
            pt = jnp.exp2(_dot_nt(kh, qh) - lse_ref[pl.ds(h, 1), pl.ds(r0, size)])
            if ctx_only and kt > TILE:
                pt = jnp.where(lax.broadcasted_iota(jnp.int32, (kt, 1), 0) < TILE, pt, 0.0)
            dst = (pt * (_dot_nt(vh, doh) - dl_ref[pl.ds(h, 1), pl.ds(r0, size)])).astype(BF)
            dv_s[...] += _dot(pt.astype(BF), doh)
            dk_s[...] += _dot(dst, qh)
            dq_ref[pl.ds(r0, size), :] += _dot_tn(dst, kh)

        @pl.when(j == 0)
        def _():
            chunk(0, TILE, ctx_only=True)

        def step(c, carry):
            chunk(pl.multiple_of(TILE + c * ch, TILE), ch)
            return carry

        lax.fori_loop(0, n_main, step, 0, unroll=MLA_UNROLL)
        dk_ref[...] = dk_s[...] * LN2
        dv_ref[...] = dv_s[...]

        @pl.when(j == n_t - 1)
        def _():
            dq_ref[...] = dq_ref[...] * LN2

    rows = pl.BlockSpec((8, t), lambda h, j: (0, 0))
    hbm = pl.BlockSpec(memory_space=pl.ANY)
    return pl.pallas_call(
        body, name=name, grid=(HEADS, n_t),
        in_specs=[hbm, hbm, pl.BlockSpec((kt, 256), lambda h, j: (j, h)), pl.BlockSpec((kt, 128), lambda h, j: (j, h)),
                  rows, rows],
        out_specs=[pl.BlockSpec((t, 256), lambda h, j: (0, h)), pl.BlockSpec((kt, 256), lambda h, j: (j, h)),
                   pl.BlockSpec((kt, 128), lambda h, j: (j, h))],
        out_shape=[jax.ShapeDtypeStruct((t, 1024), F32), jax.ShapeDtypeStruct((t, 1024), F32),
                   jax.ShapeDtypeStruct((t, 512), F32)],
        scratch_shapes=[pltpu.VMEM((t, 256), BF), pltpu.VMEM((t, 128), BF), pltpu.VMEM((kt, 256), F32),
                        pltpu.VMEM((kt, 128), F32), pltpu.SemaphoreType.DMA((2,))],
        compiler_params=_cparams(2),
    )(q_cat, do_bf, k_cat, v, lse_rows, dl_rows)


def final_loss(xf, target, fnw, name):
    t = xf.shape[0]
    n_t = t // TILE

    def body(x_ref, t_ref, w_ref, loss_ref, dx_ref, dw_ref):
        i = pl.program_id(0)

        @pl.when(i == 0)
        def _():
            loss_ref[...] = jnp.zeros_like(loss_ref)
            dw_ref[...] = jnp.zeros_like(dw_ref)
            dx_ref[...] = jnp.zeros_like(dx_ref)

        @pl.when(i >= 1)
        def _():
            y, vjp_fn = jax.vjp(_rms, x_ref[...], w_ref[...])
            err = y - t_ref[...]
            loss_ref[...] += jnp.broadcast_to(0.5 * jnp.sum(jnp.mean(err * err, axis=-1, keepdims=True)), (8, 128))
            dx, dw = vjp_fn(err * (1.0 / D))
            dx_ref[...] = dx
            dw_ref[...] += dw

    return pl.pallas_call(
        body, name=name, grid=(n_t,),
        in_specs=[pl.BlockSpec((TILE, D), lambda i: (i, 0)), pl.BlockSpec((TILE, D), lambda i: (jnp.maximum(i - 1, 0), 0)),
                  pl.BlockSpec((1, D), lambda i: (0, 0))],
        out_specs=[pl.BlockSpec((8, 128), lambda i: (0, 0)), pl.BlockSpec((TILE, D), lambda i: (i, 0)),
                   pl.BlockSpec((1, D), lambda i: (0, 0))],
        out_shape=[jax.ShapeDtypeStruct((8, 128), F32), jax.ShapeDtypeStruct((t, D), F32),
                   jax.ShapeDtypeStruct((1, D), F32)],
        compiler_params=_cparams(1),
    )(xf, target, fnw)


def all_gather(xs, name):
    n = len(xs)
    blks = [tuple(x.shape) for x in xs]
    per = N_DEV - 1

    def body(*refs):
        x_refs, o_refs = refs[:n], refs[n:2 * n]
        ssem, rsem, lsem = refs[2 * n:]
        xi, yi, ci = lax.axis_index("x"), lax.axis_index("y"), lax.axis_index("c")
        me3 = (xi, yi, ci)
        me = 4 * xi + 2 * yi + ci
        flat = lambda d: 4 * d[0] + 2 * d[1] + d[2]
        sibling = (xi, yi, 1 - ci)
        chips = [(1 - xi, yi), (xi, 1 - yi), (1 - xi, 1 - yi)]

        def copy(a, k, block, to, src=None):
            rows = o_refs[a].at[flat(block)]
            return pltpu.make_async_remote_copy(
                src_ref=rows if src is None else src, dst_ref=rows,
                send_sem=ssem.at[a * per + k], recv_sem=rsem.at[a * per + k],
                device_id=to, device_id_type=pl.DeviceIdType.MESH)

        own = [pltpu.make_async_copy(x_refs[a], o_refs[a].at[me], lsem.at[a]) for a in range(n)]
        first = []
        for a in range(n):
            first.append(copy(a, 0, me3, sibling, src=x_refs[a]))
            first += [copy(a, 1 + j, me3, (*chip, ci), src=x_refs[a]) for j, chip in enumerate(chips)]
        for cp in own + first:
            cp.start()
        passed = []
        for j, chip in enumerate(chips):
            for a in range(n):
                copy(a, 1 + j, (*chip, ci), me3).wait_recv()
                fw = copy(a, 4 + j, (*chip, ci), sibling)
                fw.start()
                passed.append(fw)
        for a in range(n):
            copy(a, 0, sibling, me3).wait_recv()
            for j, chip in enumerate(chips):
                copy(a, 4 + j, (*chip, 1 - ci), me3).wait_recv()
        for cp in first + passed:
            cp.wait_send()
        for cp in own:
            cp.wait()

    hbm = pl.BlockSpec(memory_space=pl.ANY)
    res = pl.pallas_call(
        body, name=name, in_specs=[hbm] * n, out_specs=[hbm] * n,
        out_shape=[jax.ShapeDtypeStruct((N_DEV,) + blks[a], xs[a].dtype) for a in range(n)],
        scratch_shapes=[pltpu.SemaphoreType.DMA((n * per,)), pltpu.SemaphoreType.DMA((n * per,)),
                        pltpu.SemaphoreType.DMA((n,))],
        compiler_params=pltpu.CompilerParams(has_side_effects=True),
    )(*xs)
    return list(res)


def reduce_scatter(xs, name):
    n = len(xs)
    c_idx = lax.axis_index("c").astype(jnp.int32).reshape(1)
    hbm = pl.BlockSpec(memory_space=pl.ANY)
    side = pltpu.CompilerParams(has_side_effects=True)

    def pair_body(*refs):
        x_refs, o_refs = refs[:n], refs[n:2 * n]
        ssem, rsem = refs[2 * n:]
        xi, yi, cc = lax.axis_index("x"), lax.axis_index("y"), lax.axis_index("c")
        cps = [pltpu.make_async_remote_copy(
            src_ref=x_refs[a].at[1 - cc], dst_ref=o_refs[a], send_sem=ssem.at[a], recv_sem=rsem.at[a],
            device_id=(xi, yi, 1 - cc), device_id_type=pl.DeviceIdType.MESH) for a in range(n)]
        for cp in cps:
            cp.start()
        for cp in cps:
            cp.wait()

    halves = [tuple(x.shape[1:]) for x in xs]
    got = pl.pallas_call(
        pair_body, name=name + "_pair", in_specs=[hbm] * n, out_specs=[hbm] * n,
        out_shape=[jax.ShapeDtypeStruct(halves[a], xs[a].dtype) for a in range(n)],
        scratch_shapes=[pltpu.SemaphoreType.DMA((n,)), pltpu.SemaphoreType.DMA((n,))], compiler_params=side,
    )(*xs)

    sums = []
    for a in range(n):
        def add_body(c_ref, a_ref, b_ref, o_ref):
            o_ref[...] = (a_ref[0].astype(F32) + b_ref[...].astype(F32)).astype(o_ref.dtype)

        blk = pl.BlockSpec((1,) + halves[a][1:], lambda q, c_ref: (q, 0, 0))
        own = pl.BlockSpec((1, 1) + halves[a][1:], lambda q, c_ref: (c_ref[0], q, 0, 0))
        sums.append(pl.pallas_call(
            add_body, name=f"{name}_sum{a}",
            grid_spec=pltpu.PrefetchScalarGridSpec(num_scalar_prefetch=1, grid=(4,), in_specs=[own, blk], out_specs=blk),
            out_shape=jax.ShapeDtypeStruct(halves[a], xs[a].dtype), compiler_params=_cparams(1),
        )(c_idx, xs[a], got[a]))

    def chips_body(*refs):
        s_refs, o_refs = refs[:n], refs[n:2 * n]
        ssem, rsem, lsem = refs[2 * n:]
        xi, yi, cc = lax.axis_index("x"), lax.axis_index("y"), lax.axis_index("c")
        my_chip = 2 * xi + yi
        chips = [(1 - xi, yi), (xi, 1 - yi), (1 - xi, 1 - yi)]
        own = [pltpu.make_async_copy(s_refs[a].at[my_chip], o_refs[a].at[my_chip], lsem.at[a]) for a in range(n)]
        sends, lands = [], []
        for a in range(n):
            for j, (px, py) in enumerate(chips):
                common = dict(send_sem=ssem.at[3 * a + j], recv_sem=rsem.at[3 * a + j], device_id=(px, py, cc),
                              device_id_type=pl.DeviceIdType.MESH)
                sends.append(pltpu.make_async_remote_copy(src_ref=s_refs[a].at[2 * px + py],
                                                          dst_ref=o_refs[a].at[my_chip], **common))
                lands.append(pltpu.make_async_remote_copy(src_ref=s_refs[a].at[2 * px + py],
                                                          dst_ref=o_refs[a].at[2 * px + py], **common))
        for cp in own + sends:
            cp.start()
        for cp in lands:
            cp.wait_recv()
        for cp in sends:
            cp.wait_send()
        for cp in own:
            cp.wait()

    res = pl.pallas_call(
        chips_body, name=name + "_chips", in_specs=[hbm] * n, out_specs=[hbm] * n,
        out_shape=[jax.ShapeDtypeStruct(halves[a], xs[a].dtype) for a in range(n)],
        scratch_shapes=[pltpu.SemaphoreType.DMA((3 * n,)), pltpu.SemaphoreType.DMA((3 * n,)),
                        pltpu.SemaphoreType.DMA((n,))], compiler_params=side,
    )(*sums)
    return list(res)


def mod_forward(crows, w_mod, b_shard, name):
    cols = w_mod.shape[2]

    def body(c_ref, w_ref, b_ref, o_ref):
        o_ref[0] = _dot(_silu(c_ref[...]).astype(BF), w_ref[0].astype(BF)) + b_ref[0]

    return pl.pallas_call(
        body, name=name, grid=(2,),
        in_specs=[pl.BlockSpec((16, D), lambda l: (0, 0)), pl.BlockSpec((1, D, cols), lambda l: (l, 0, 0)),
                  pl.BlockSpec((1, 1, cols), lambda l: (l, 0, 0))],
        out_specs=pl.BlockSpec((1, 16, cols), lambda l: (l, 0, 0)),
        out_shape=jax.ShapeDtypeStruct((2, 16, cols), F32), compiler_params=_cparams(1),
    )(crows, w_mod, b_shard)


def mod_backward(crows, w_mod, d_own, d_ctx, name):
    cols = w_mod.shape[2]

    def body(c_ref, w_ref, do_ref, dc_ref, gw_ref, gs_ref):
        dc = dc_ref[0]
        dsum = dc[0:1]
        for s in range(1, N_DEV):
            dsum = dsum + dc[s:s + 1]
        row = lax.broadcasted_iota(jnp.int32, (8, cols), 0)
        d16 = jnp.concatenate([do_ref[0], jnp.where(row == 0, jnp.broadcast_to(dsum, (8, cols)), 0.0)], axis=0)
        gw_ref[0] = _dot_tn(_silu(c_ref[...]).astype(BF), d16.astype(BF))
        gs_ref[0] = _dot_nt(jnp.broadcast_to(dsum, (8, cols)).astype(BF), w_ref[0].astype(BF))

    return pl.pallas_call(
        body, name=name, grid=(2,),
        in_specs=[pl.BlockSpec((16, D), lambda l: (0, 0)), pl.BlockSpec((1, D, cols), lambda l: (l, 0, 0)),
                  pl.BlockSpec((1, 8, cols), lambda l: (l, 0, 0)), pl.BlockSpec((1, 8, cols), lambda l: (l, 0, 0))],
        out_specs=[pl.BlockSpec((1, D, cols), lambda l: (l, 0, 0)), pl.BlockSpec((1, 8, D), lambda l: (l, 0, 0))],
        out_shape=[jax.ShapeDtypeStruct((2, D, cols), F32), jax.ShapeDtypeStruct((2, 8, D), F32)],
        compiler_params=_cparams(1),
    )(crows, w_mod, d_own, d_ctx)


def silu_grad_scale(c_ctx, ds, name):
    def body(c_ref, ds_ref, o_ref):
        cc = c_ref[...]
        sg = jax.nn.sigmoid(cc)
        o_ref[...] = (ds_ref[0][0:1] + ds_ref[1][0:1]) * (sg * (1.0 + cc * (1.0 - sg)))

    return pl.pallas_call(body, name=name, out_shape=jax.ShapeDtypeStruct((1, D), F32))(c_ctx, ds)


def _adamw_math(p_ref, w_ref, m_ref, v_ref, g_ref, d_ref, nm_ref, nv_ref):
    g = p_ref[0].astype(F32)
    for s in range(1, p_ref.shape[0]):
        g = g + p_ref[s].astype(F32)
    mm = ADAM_B1 * m_ref[...] + (1.0 - ADAM_B1) * g
    vv = ADAM_B2 * v_ref[...] + (1.0 - ADAM_B2) * (g * g)
    m_hat = mm / (1.0 - ADAM_B1 ** ADAM_STEP)
    v_hat = vv / (1.0 - ADAM_B2 ** ADAM_STEP)
    g_ref[...] = g
    d_ref[...] = -ADAM_LR * (m_hat / (jnp.sqrt(v_hat) + ADAM_EPS) + ADAM_WD * w_ref[...])
    nm_ref[...] = mm
    nv_ref[...] = vv


def adamw(parts, w, m, v, name):
    n_parts, rows, cols = parts.shape
    lanes = -(-cols // 128) * 128
    block_rows = min(rows, 1 << ((ADAMW_BLOCK_ELEMS // lanes).bit_length() - 1))
    assert rows % block_rows == 0

    def body(*refs):
        _adamw_math(*refs)

    spec = pl.BlockSpec((block_rows, cols), lambda i: (i, 0))
    return pl.pallas_call(
        body, name=name, grid=(rows // block_rows,),
        in_specs=[pl.BlockSpec((n_parts, block_rows, cols), lambda i: (0, i, 0)), spec, spec, spec],
        out_specs=[spec] * 4, out_shape=[jax.ShapeDtypeStruct((rows, cols), F32)] * 4,
        compiler_params=_cparams(1),
    )(parts, w, m, v)


def adamw_group(items, name):
    n = len(items)

    def body(*refs):
        for a in range(n):
            _adamw_math(*refs[4 * a:4 * a + 4], *refs[4 * n + 4 * a:4 * n + 4 * a + 4])

    flat_in = [x for it in items for x in it]
    out_shape = [jax.ShapeDtypeStruct(it[1].shape, F32) for it in items for _ in range(4)]
    res = pl.pallas_call(body, name=name, out_shape=out_shape,
                         compiler_params=pltpu.CompilerParams(vmem_limit_bytes=VMEM_LIMIT_MB * 1024 * 1024))(*flat_in)
    return [tuple(res[4 * a:4 * a + 4]) for a in range(n)]


def _pad_cols(w, segs, total):
    parts, pos = [], 0
    for dst, src, wd in segs:
        if dst > pos:
            parts.append(jnp.zeros(w.shape[:-1] + (dst - pos,), w.dtype))
        parts.append(w[..., src:src + wd])
        pos = dst + wd
    if pos < total:
        parts.append(jnp.zeros(w.shape[:-1] + (total - pos,), w.dtype))
    return jnp.concatenate(parts, axis=-1)


def _unpad_cols(g, segs):
    return jnp.concatenate([g[..., dst:dst + wd] for dst, _, wd in segs], axis=-1)


def _rope_tables(n_lat):
    rows = n_lat // GRID_W
    freq = ROPE_BASE ** (-jnp.arange(16, dtype=F32) * 2.0 / 32)
    a_row = jnp.arange(rows).astype(F32)[:, None] * freq[None, :]
    a_col = jnp.arange(GRID_W).astype(F32)[:, None] * freq[None, :]
    per_row = lambda tbl: jnp.repeat(tbl, GRID_W, axis=0)
    per_col = lambda tbl: jnp.tile(tbl, (rows, 1))
    cr, sr, cc, sc = per_row(jnp.cos(a_row)), per_row(jnp.sin(a_row)), per_col(jnp.cos(a_col)), per_col(jnp.sin(a_col))
    z = jnp.zeros((n_lat, 16), F32)
    cos = jnp.concatenate([cr, cr, cc, cc, jnp.ones((n_lat, 64), F32)], axis=1)
    sa = jnp.concatenate([-sr, z, -sc, z, jnp.zeros((n_lat, 64), F32)], axis=1)
    sb = jnp.concatenate([z, sr, z, sc, jnp.zeros((n_lat, 64), F32)], axis=1)
    ident = lambda fill: jnp.full((TILE, 128), fill, F32)
    return (jnp.concatenate([ident(1.0), cos]), jnp.concatenate([ident(0.0), sa]), jnp.concatenate([ident(0.0), sb]))


def _gathered_to_full(g, name):
    if name in ("w_out", "w_ff2"):
        return jnp.transpose(g, (1, 0, 2, 3)).reshape(2, -1, g.shape[-1])
    return jnp.transpose(g, (1, 2, 0, 3)).reshape(2, g.shape[2], -1)


def _layer_to_halves(g, name):
    if name in ("w_out", "w_ff2"):
        return jnp.transpose(g.reshape(4, 2, -1, g.shape[-1]), (1, 0, 2, 3))
    return jnp.transpose(g.reshape(g.shape[0], 4, 2, -1), (2, 1, 0, 3))


def kernel(x, c, ctx, c_ctx, w_mod, b_mod, norm1_w, w_in, w_out, sgu_norm_w, sgu_norm_b, sgu_w, sgu_b, gla_wg_fwd, gla_bg_fwd, gla_wg_bwd, gla_bg_bwd, gla_norm_w, mla_q_norm_w, mla_w_uq, mla_kv_norm_w, mla_w_ukv, norm2_w, w_ff1, w_ff2, final_norm_w, loss_target, m_c_ctx, m_w_mod, m_b_mod, m_norm1_w, m_w_in, m_w_out, m_sgu_norm_w, m_sgu_norm_b, m_sgu_w, m_sgu_b, m_gla_wg_fwd, m_gla_bg_fwd, m_gla_wg_bwd, m_gla_bg_bwd, m_gla_norm_w, m_mla_q_norm_w, m_mla_w_uq, m_mla_kv_norm_w, m_mla_w_ukv, m_norm2_w, m_w_ff1, m_w_ff2, m_final_norm_w, v_c_ctx, v_w_mod, v_b_mod, v_norm1_w, v_w_in, v_w_out, v_sgu_norm_w, v_sgu_norm_b, v_sgu_w, v_sgu_b, v_gla_wg_fwd, v_gla_bg_fwd, v_gla_wg_bwd, v_gla_bg_bwd, v_gla_norm_w, v_mla_q_norm_w, v_mla_w_uq, v_mla_kv_norm_w, v_mla_w_ukv, v_norm2_w, v_w_ff1, v_w_ff2, v_final_norm_w):
    W = dict(c_ctx=c_ctx, w_mod=w_mod, b_mod=b_mod, norm1_w=norm1_w, w_in=w_in, w_out=w_out, sgu_norm_w=sgu_norm_w,
             sgu_norm_b=sgu_norm_b, sgu_w=sgu_w, sgu_b=sgu_b, gla_wg_fwd=gla_wg_fwd, gla_bg_fwd=gla_bg_fwd,
             gla_wg_bwd=gla_wg_bwd, gla_bg_bwd=gla_bg_bwd, gla_norm_w=gla_norm_w, mla_q_norm_w=mla_q_norm_w,
             mla_w_uq=mla_w_uq, mla_kv_norm_w=mla_kv_norm_w, mla_w_ukv=mla_w_ukv, norm2_w=norm2_w, w_ff1=w_ff1,
             w_ff2=w_ff2, final_norm_w=final_norm_w)
    M = dict(c_ctx=m_c_ctx, w_mod=m_w_mod, b_mod=m_b_mod, norm1_w=m_norm1_w, w_in=m_w_in, w_out=m_w_out,
             sgu_norm_w=m_sgu_norm_w, sgu_norm_b=m_sgu_norm_b, sgu_w=m_sgu_w, sgu_b=m_sgu_b, gla_wg_fwd=m_gla_wg_fwd,
             gla_bg_fwd=m_gla_bg_fwd, gla_wg_bwd=m_gla_wg_bwd, gla_bg_bwd=m_gla_bg_bwd, gla_norm_w=m_gla_norm_w,
             mla_q_norm_w=m_mla_q_norm_w, mla_w_uq=m_mla_w_uq, mla_kv_norm_w=m_mla_kv_norm_w, mla_w_ukv=m_mla_w_ukv,
             norm2_w=m_norm2_w, w_ff1=m_w_ff1, w_ff2=m_w_ff2, final_norm_w=m_final_norm_w)
    V = dict(c_ctx=v_c_ctx, w_mod=v_w_mod, b_mod=v_b_mod, norm1_w=v_norm1_w, w_in=v_w_in, w_out=v_w_out,
             sgu_norm_w=v_sgu_norm_w, sgu_norm_b=v_sgu_norm_b, sgu_w=v_sgu_w, sgu_b=v_sgu_b, gla_wg_fwd=v_gla_wg_fwd,
             gla_bg_fwd=v_gla_bg_fwd, gla_wg_bwd=v_gla_wg_bwd, gla_bg_bwd=v_gla_bg_bwd, gla_norm_w=v_gla_norm_w,
             mla_q_norm_w=v_mla_q_norm_w, mla_w_uq=v_mla_w_uq, mla_kv_norm_w=v_mla_kv_norm_w, mla_w_ukv=v_mla_w_ukv,
             norm2_w=v_norm2_w, w_ff1=v_w_ff1, w_ff2=v_w_ff2, final_norm_w=v_final_norm_w)

    n_lat = x.shape[1]
    assert ctx.shape[1] == TILE and n_lat % TILE == 0 and x.shape[2] == D
    t_all = TILE + n_lat
    n_t = t_all // TILE
    me = 4 * lax.axis_index("x") + 2 * lax.axis_index("y") + lax.axis_index("c")
    mod_cols = w_mod.shape[2]

    c_all = all_gather([c], "ag_c")[0].reshape(N_DEV, D)
    crows = jnp.concatenate([c_all, c_ctx[None, :], jnp.zeros((7, D), F32)], axis=0)
    b_shard = lax.dynamic_slice_in_dim(b_mod, me * mod_cols, mod_cols, axis=1)[:, None, :]
    mod_sh = mod_forward(crows, w_mod, b_shard, "mod_fwd")
    mod_g = all_gather([mod_sh.reshape(32, mod_cols)], "ag_mod")[0]
    mod_full = jnp.transpose(mod_g.reshape(N_DEV, 2, 16, mod_cols), (1, 2, 0, 3)).reshape(2, 16, 6 * D)
    mod_own = lax.dynamic_index_in_dim(mod_full, me, axis=1, keepdims=False)
    mod_ctx = mod_full[:, 8, :]
    pad2 = jnp.zeros((2, D), F32)
    modl = [jnp.stack([jnp.concatenate([mod_ctx[l].reshape(6, D), pad2]),
                       jnp.concatenate([mod_own[l].reshape(6, D), pad2])]) for l in range(2)]

    v2 = lambda a: a[None, :] if a.ndim == 1 else a.reshape(-1, a.shape[-1])
    gathered = all_gather([v2(W[k].astype(BF)) for k in BIG_NAMES], "ag_weights")
    full = {k: _gathered_to_full(g.reshape((N_DEV,) + W[k].shape), k) for k, g in zip(BIG_NAMES, gathered)}
    w_in_p = _pad_cols(full["w_in"], W_IN_SEGS, P_COLS)
    w_uq_p = _pad_cols(full["mla_w_uq"], W_UQ_SEGS, 1024).astype(F32)
    w_ukv_f = full["mla_w_ukv"].astype(F32)
    wgf_p = jnp.pad(gla_wg_fwd, ((0, 0), (0, 112), (0, 0)))
    wgb_p = jnp.pad(gla_wg_bwd, ((0, 0), (0, 112), (0, 0)))
    sgu_bx = jnp.repeat(jnp.transpose(sgu_b, (0, 2, 1)), 64, axis=2)
    gnw_t = jnp.tile(gla_norm_w, (1, HEADS))
    rc, rsa, rsb = _rope_tables(n_lat)

    xin = jnp.concatenate([ctx[0], x[0]], axis=0)
    row = lambda a: a[None, :]

    def pre_ins(l, xl):
        return [("x", "tile", True, xl), ("mod", "kind", True, modl[l]), ("n1w", "full", True, row(norm1_w[l])),
                ("w_in", "wfull", False, w_in_p[l]), ("sgu_nw", "full", True, row(sgu_norm_w[l])),
                ("sgu_nb", "full", True, row(sgu_norm_b[l])), ("sgu_w", "full", True, sgu_w[l]),
                ("sgu_bx", "full", True, sgu_bx[l]), ("wgf", "full", True, wgf_p[l]), ("bgf", "full", True, row(gla_bg_fwd[l])),
                ("wgb", "full", True, wgb_p[l]), ("bgb", "full", True, row(gla_bg_bwd[l])),
                ("qnw", "full", True, row(mla_q_norm_w[l])), ("w_uq", "full", True, w_uq_p[l]),
                ("kvnw", "full", True, row(mla_kv_norm_w[l])), ("w_ukv", "full", True, w_ukv_f[l]),
                ("rc", "tile", False, rc), ("rsa", "tile", False, rsa), ("rsb", "tile", False, rsb)]

    pre_outs = [("y_sgu", 256, F32), ("qg", 128, F32), ("kg", 128, F32), ("vg", 256, F32), ("lgf", 128, F32),
                ("lgb", 128, F32), ("gr", 256, F32), ("q_cat", 1024, BF), ("k_cat", 1024, BF), ("v", 512, BF)]

    def out_ins(l, xl, a):
        return [("x", "tile", True, xl), ("mod", "kind", True, modl[l]), ("y_sgu", "tile", True, a["y_sgu"]),
                ("o_f", "tile", True, a["o_f"]), ("o_b", "tile", False, a["o_b"]), ("gr", "tile", True, a["gr"]),
                ("y_mla", "tile", True, a["y_mla"]), ("gnw", "full", True, row(gnw_t[l])),
                ("w_out", "wfull", False, full["w_out"][l])]

    def ffn_ins(l, x1):
        return [("x1", "tile", True, x1), ("mod", "kind", True, modl[l]), ("n2w", "full", True, row(norm2_w[l])),
                ("w_ff1", "wcols", False, full["w_ff1"][l]), ("w_ff2", "wrows", False, full["w_ff2"][l])]

    saved, xl = [], xin
    for l in range(2):
        a = tile_forward(pre_tile, f"pre_fwd{l}", t_all, pre_ins(l, xl), pre_outs)
        a["o_f"], a["sf"], a["o_b"], a["sb"] = gla_forward(a["qg"], a["kg"], a["vg"], a["lgf"], a["lgb"], f"gla_fwd{l}")
        a["y_mla"], a["lse"] = mla_forward(a["q_cat"], a["k_cat"], a["v"], f"mla_fwd{l}")
        a["x"] = xl
        a["x1"] = tile_forward(attn_out_tile, f"out_fwd{l}", t_all, out_ins(l, xl, a), [("x1", D, F32)])["x1"]
        ff = tile_forward(ffn_tile, f"ffn_fwd{l}", t_all, ffn_ins(l, a["x1"]), [("x2", D, F32), ("f", D, F32)])
        xl, a["f"] = ff["x2"], ff["f"]
        saved.append(a)

    loss_blk, dx, d_fnw = final_loss(xl, loss_target[0], row(final_norm_w), "final_loss")
    loss = lax.psum(loss_blk[0, 0], AXES)

    G = {}
    dmods = []
    for l in (1, 0):
        a = saved[l]
        dx1, dmod3, dn2w, zpre, zf_t, h2_t, a_ff2 = ffn_backward(
            a["x1"], modl[l], row(norm2_w[l]), full["w_ff1"][l], full["w_ff2"][l], dx, a["f"], f"ffn_bwd{l}")
        gw_ff1 = wgrad(h2_t, zpre, f"wg_ff1_{l}")
        gw_ff2 = jnp.transpose(wgrad(zf_t, a_ff2, f"wg_ff2_{l}"))
        g2, e2 = tile_backward(attn_out_tile, f"out_bwd{l}", t_all, out_ins(l, a["x"], a), [("x1", dx1)],
                               [("zt", D)], [("a_out", D)])
        gw_out = wgrad(e2["a_out"], e2["zt"], f"wg_out_{l}")
        dl_rows, do_bf = mla_delta(g2["y_mla"], a["y_mla"], f"mla_delta{l}")
        dq_cat, dk_cat, dv = mla_backward(a["q_cat"], a["k_cat"], a["v"], a["lse"], dl_rows, do_bf, f"mla_bwd{l}")
        dqf, dkf, dvf, dgf, dqb, dkb, dvb, dgb = gla_backward(
            a["qg"], a["kg"], a["vg"], a["lgf"], a["lgb"], a["sf"], a["sb"], g2["o_f"], f"gla_bwd{l}")
        cots = [("y_sgu", g2["y_sgu"]), ("qg", [dqf, dqb]), ("kg", [dkf, dkb]), ("vg", [dvf, dvb]), ("lgf", dgf),
                ("lgb", dgb), ("gr", g2["gr"]), ("q_cat", dq_cat), ("k_cat", dk_cat), ("v", dv)]
        g1, e1 = tile_backward(pre_tile, f"pre_bwd{l}", t_all, pre_ins(l, a["x"]), cots, [("zp", P_COLS)], [("a_in", D)],
                               resid=("x", g2["x"]))
        gw_in = _unpad_cols(wgrad(e1["a_in"], e1["zp"], f"wg_in_{l}", bk2=P_COLS), W_IN_SEGS)
        dx = g1["x"]
        dmods.append(g1["mod"] + g2["mod"] + dmod3)
        G[l] = dict(w_in=gw_in, w_out=gw_out, w_ff1=gw_ff1, w_ff2=gw_ff2,
                    mla_w_uq=_unpad_cols(g1["w_uq"], W_UQ_SEGS), mla_w_ukv=g1["w_ukv"],
                    norm1_w=g1["n1w"][0], norm2_w=dn2w[0], sgu_norm_w=g1["sgu_nw"][0], sgu_norm_b=g1["sgu_nb"][0],
                    sgu_w=g1["sgu_w"], sgu_b=jnp.transpose(g1["sgu_bx"].reshape(128, HEADS, 64).sum(-1)),
                    gla_wg_fwd=g1["wgf"][:16], gla_bg_fwd=g1["bgf"][0], gla_wg_bwd=g1["wgb"][:16], gla_bg_bwd=g1["bgb"][0],
                    gla_norm_w=g2["gnw"][0].reshape(HEADS, 64).sum(0), mla_q_norm_w=g1["qnw"][0], mla_kv_norm_w=g1["kvnw"][0])
    dmods = dmods[::-1]
    grad_x = dx[TILE:][None]

    dmod_pack = jnp.stack([jnp.stack([dmods[l][1, :6].reshape(-1), dmods[l][0, :6].reshape(-1)]) for l in range(2)])
    dmod_all = all_gather([dmod_pack.reshape(4, 6 * D)], "ag_dmod")[0].reshape(N_DEV, 2, 2, 6 * D)
    dsl = lax.dynamic_slice_in_dim(dmod_all, me * mod_cols, mod_cols, axis=3)
    d_own = jnp.transpose(dsl[:, :, 0, :], (1, 0, 2))
    d_ctx = jnp.transpose(dsl[:, :, 1, :], (1, 0, 2))
    g_w_mod, ds_cc = mod_backward(crows, w_mod, d_own, d_ctx, "mod_bwd")
    g_c_ctx_part = silu_grad_scale(c_ctx[None, :], ds_cc, "silu_bwd")[0]
    g_b_mod_part = jnp.stack([dmods[l][1, :6].reshape(-1) + dmods[l][0, :6].reshape(-1) for l in range(2)])

    small_g = dict(c_ctx=g_c_ctx_part, b_mod=g_b_mod_part, final_norm_w=d_fnw[0])
    for k in SMALL_NAMES:
        if k not in small_g:
            small_g[k] = jnp.stack([G[0][k], G[1][k]])
    res = {}
    sparts = all_gather([v2(small_g[k]) for k in SMALL_NAMES], "ag_small")
    s_out = adamw_group([(sparts[j], v2(W[k]), v2(M[k]), v2(V[k])) for j, k in enumerate(SMALL_NAMES)], "adamw_small")
    for j, k in enumerate(SMALL_NAMES):
        res[k] = [o.reshape(W[k].shape) for o in s_out[j]]

    chunks = []
    for k in BIG_NAMES:
        chunks.append(jnp.concatenate([_layer_to_halves(G[l][k].astype(BF), k) for l in range(2)], axis=2))
    bparts = reduce_scatter(chunks, "rs_grads")
    for j, k in enumerate(BIG_NAMES):
        res[k] = [o.reshape(W[k].shape) for o in adamw(bparts[j], v2(W[k]), v2(M[k]), v2(V[k]), f"adamw_{k}")]
    res["w_mod"] = [o.reshape(w_mod.shape)
                    for o in adamw(v2(g_w_mod)[None], v2(w_mod), v2(m_w_mod), v2(v_w_mod), "adamw_w_mod")]
    outs = [loss, grad_x]
    for j in range(4):
        outs += [res[k][j] for k in WEIGHT_ORDER]
    return tuple(outs)
```

```python
import jax
import jax.numpy as jnp
from jax import lax
from jax.experimental import pallas as pl
from jax.experimental.pallas import tpu as pltpu

F32 = jnp.float32
BF = jnp.bfloat16

N_DEV = 8
AXES = ("x", "y", "c")
EPS = 1e-6
D = 1024
TILE = 256
GCH = 128
SGU_CHUNK = 128
HEADS = 4
ROPE_BASE = 10000.0
GRID_W = 64
GLA_TAU = 16.0
ATT_SCALE = (128 + 64) ** -0.5
ATT_SCALE_LOG2 = ATT_SCALE * 1.4426950408889634
LN2 = 0.6931471805599453
KV_CH = 512
KV_CH_FWD = 2048
QT_FWD = 256
Q_CH_BWD = 1024
KT_BWD = 768
MLA_UNROLL = 2
D_FF = 4096
FF_CH = 1024

ADAM_LR = 0.001
ADAM_B1 = 0.9
ADAM_B2 = 0.999
ADAM_EPS = 1e-08
ADAM_WD = 0.01
ADAM_STEP = 10

VMEM_LIMIT_MB = 56
ADAMW_BLOCK_ELEMS = 256 * 1024

W_IN_SEGS = ((0, 0, 128), (128, 128, 256), (384, 384, 16), (512, 400, 16), (640, 416, 256), (896, 672, 64),
             (1024, 736, 256), (1280, 992, 256), (1536, 1248, 128), (1664, 1376, 256), (1920, 1632, 256))
P_COLS = 2176
O_GK, O_GV, O_GGF, O_GGB, O_CKV, O_KR, O_SU, O_SV, O_GQ, O_GR, O_DQ = (s[0] for s in W_IN_SEGS)
W_UQ_SEGS = tuple((h * 256, h * 192, 192) for h in range(HEADS))

SMALL_NAMES = ("c_ctx", "b_mod", "norm1_w", "sgu_norm_w", "sgu_norm_b", "sgu_w", "sgu_b", "gla_wg_fwd", "gla_bg_fwd",
               "gla_wg_bwd", "gla_bg_bwd", "gla_norm_w", "mla_q_norm_w", "mla_kv_norm_w", "norm2_w", "final_norm_w")
BIG_NAMES = ("w_in", "w_out", "mla_w_uq", "mla_w_ukv", "w_ff1", "w_ff2")
WEIGHT_ORDER = ("c_ctx", "w_mod", "b_mod", "norm1_w", "w_in", "w_out", "sgu_norm_w", "sgu_norm_b", "sgu_w", "sgu_b",
                "gla_wg_fwd", "gla_bg_fwd", "gla_wg_bwd", "gla_bg_bwd", "gla_norm_w", "mla_q_norm_w", "mla_w_uq",
                "mla_kv_norm_w", "mla_w_ukv", "norm2_w", "w_ff1", "w_ff2", "final_norm_w")


def _cparams(n_axes):
    return pltpu.CompilerParams(dimension_semantics=("arbitrary",) * n_axes,
                                vmem_limit_bytes=VMEM_LIMIT_MB * 1024 * 1024)


def _dot(a, b):
    return jnp.dot(a, b, preferred_element_type=F32)


def _dot_nt(a, b):
    return lax.dot_general(a, b, (((1,), (1,)), ((), ())), preferred_element_type=F32)


def _dot_tn(a, b):
    return lax.dot_general(a, b, (((0,), (0,)), ((), ())), preferred_element_type=F32)


def _nn(a, b):
    return _dot(a.astype(BF), b.astype(BF))


def _nt(a, b):
    return _dot_nt(a.astype(BF), b.astype(BF))


def _tn(a, b):
    return _dot_tn(a.astype(BF), b.astype(BF))


nn_d = jax.custom_vjp(_nn)
nt_d = jax.custom_vjp(_nt)
tn_d = jax.custom_vjp(_tn)
nn_d.defvjp(lambda a, b: (_nn(a, b), (a, b)), lambda r, dy: (_nt(dy, r[1]), _tn(r[0], dy)))
nt_d.defvjp(lambda a, b: (_nt(a, b), (a, b)), lambda r, dy: (_nn(dy, r[1]), _tn(dy, r[0])))
tn_d.defvjp(lambda a, b: (_tn(a, b), (a, b)), lambda r, dy: (_nt(r[1], dy), _nn(r[0], dy)))


def nn_const(w_bf, diff):
    def raw(a):
        return _dot(a.astype(BF), w_bf)

    if not diff:
        return raw
    f = jax.custom_vjp(raw)
    f.defvjp(lambda a: (raw(a), None), lambda _, dy: (_dot_nt(dy.astype(BF), w_bf),))
    return f


def _split3(g):
    hi = g.astype(BF)
    r = g - hi.astype(F32)
    mid = r.astype(BF)
    lo = (r - mid.astype(F32)).astype(BF)
    return hi, mid, lo


def make_cum(tri_bf, tri_t_bf, diff):
    def raw(g, t):
        hi, mid, lo = _split3(g)
        return _dot(t, hi) + _dot(t, mid) + _dot(t, lo)

    def fwd(g):
        return raw(g, tri_bf)

    if not diff:
        return fwd
    cum = jax.custom_vjp(fwd)
    cum.defvjp(lambda g: (fwd(g), None), lambda _, db: (raw(db, tri_t_bf),))
    return cum


def _roll_lanes(x, shift):
    return pltpu.roll(x, shift, 1)


def make_rope(c, sa, sb, diff):
    def raw(x):
        return x * c + _roll_lanes(x, 112) * sa + _roll_lanes(x, 16) * sb

    if not diff:
        return raw
    f = jax.custom_vjp(raw)
    f.defvjp(lambda x: (raw(x), None),
             lambda _, dy: (dy * c + _roll_lanes(dy * sa, 16) + _roll_lanes(dy * sb, 112),))
    return f


def _ops(diff):
    return (nn_d, nt_d, tn_d) if diff else (_nn, _nt, _tn)


def _rms(x, w):
    return x * lax.rsqrt(jnp.mean(x * x, axis=-1, keepdims=True) + EPS) * w


def _gelu(x):
    return 0.5 * x * (1.0 + jnp.tanh(0.7978845608028654 * (x + 0.044715 * (x * x * x))))


def _silu(x):
    return x * jax.nn.sigmoid(x)


def _log_sigmoid(z):
    return jnp.minimum(z, 0.0) - jnp.log(1.0 + jnp.exp(-jnp.abs(z)))


def _lane_group_mask(width, group, h):
    lane = lax.broadcasted_iota(jnp.int32, (1, width), 1)
    return ((lane >= h * group) & (lane < (h + 1) * group)).astype(F32)


def pre_tile(d, c, z):
    nn, _, _ = _ops(z is not None)
    rope = make_rope(c["rc"], c["rsa"], c["rsb"], z is not None)
    mod = d["mod"]
    h = _rms(d["x"], d["n1w"]) * (1.0 + mod[1:2]) + mod[0:1]
    p = nn_const(c["w_in"], z is not None)(h)
    if z is not None:
        p = p + z["zp"]
    gk, gv = p[:, O_GK:O_GK + 128], p[:, O_GV:O_GV + 256]
    ggf, ggb = p[:, O_GGF:O_GGF + 128], p[:, O_GGB:O_GGB + 128]
    ckv, kr = p[:, O_CKV:O_CKV + 256], p[:, O_KR:O_KR + 128]
    su, sv = p[:, O_SU:O_SU + 256], p[:, O_SV:O_SV + 256]
    gq, gr, dq = p[:, O_GQ:O_GQ + 128], p[:, O_GR:O_GR + 256], p[:, O_DQ:O_DQ + 256]

    u = _gelu(su)
    gv_ = _gelu(sv)
    mu = jnp.mean(gv_, axis=-1, keepdims=True)
    cen = gv_ - mu
    vn = cen * lax.rsqrt(jnp.mean(cen * cen, axis=-1, keepdims=True) + EPS) * d["sgu_nw"] + d["sgu_nb"]
    hm = [_lane_group_mask(256, 64, hh) for hh in range(HEADS)]
    rows = []
    for ci in range(vn.shape[0] // SGU_CHUNK):
        vc = vn[ci * SGU_CHUNK:(ci + 1) * SGU_CHUNK]
        s = d["sgu_bx"]
        for hh in range(HEADS):
            s = s + hm[hh] * nn(d["sgu_w"][hh], vc)
        rows.append(s)
    y_sgu = u * jnp.concatenate(rows, axis=0)

    qg = gq * (32 ** -0.5)
    lgf = _log_sigmoid(nn(ggf, d["wgf"]) + d["bgf"]) * (1.0 / GLA_TAU)
    lgb = _log_sigmoid(nn(ggb, d["wgb"]) + d["bgb"]) * (1.0 / GLA_TAU)

    kv = nn(_rms(ckv, d["kvnw"]), d["w_ukv"])
    kr_r = rope(kr)
    q = nn(_rms(dq, d["qnw"]), d["w_uq"])
    qs, ks, vs = [], [], []
    for hh in range(HEADS):
        qs += [q[:, hh * 256:hh * 256 + 128], rope(q[:, hh * 256 + 128:(hh + 1) * 256])]
        ks += [kv[:, hh * 256:hh * 256 + 128], kr_r]
        vs += [kv[:, hh * 256 + 128:(hh + 1) * 256]]
    outs = dict(y_sgu=y_sgu, qg=qg, kg=gk, vg=gv, lgf=lgf, lgb=lgb, gr=gr,
                q_cat=jnp.concatenate(qs, axis=-1) * ATT_SCALE_LOG2, k_cat=jnp.concatenate(ks, axis=-1), v=jnp.concatenate(vs, axis=-1))
    return outs, dict(a_in=h)


def attn_out_tile(d, c, z):
    mod = d["mod"]
    o = d["o_f"] + c["o_b"]
    ms = jnp.zeros_like(o)
    for hh in range(HEADS):
        m_h = _lane_group_mask(256, 64, hh)
        ms = ms + m_h * (jnp.sum(o * o * m_h, axis=-1, keepdims=True) * (1.0 / 64))
    yg = o * lax.rsqrt(ms + EPS) * d["gnw"] * _silu(d["gr"])
    y = jnp.concatenate([d["y_sgu"], yg, d["y_mla"]], axis=-1)
    t = nn_const(c["w_out"], z is not None)(y)
    if z is not None:
        t = t + z["zt"]
    return dict(x1=d["x"] + mod[2:3] * t), dict(a_out=y)


def ffn_tile(d, c, z):
    mod = d["mod"]
    h2 = _rms(d["x1"], d["n2w"]) * (1.0 + mod[4:5]) + mod[3:4]
    f = None
    for j in range(D_FF // FF_CH):
        a = jnp.maximum(nn_const(c["w_ff1"][j], False)(h2), 0.0)
        fj = nn_const(c["w_ff2"][j], False)(a * a)
        f = fj if f is None else f + fj
    return dict(x2=d["x1"] + mod[5:6] * f, f=f), {}


def out_ffn_tile(d, c, z):
    x1 = attn_out_tile(d, c, None)[0]["x1"]
    o2 = ffn_tile(dict(x1=x1, mod=d["mod"], n2w=d["n2w"]), c, None)[0]
    return dict(x1=x1, x2=o2["x2"], f=o2["f"]), {}


def _in_spec(kind, arr, tile):
    if kind == "tile":
        return pl.BlockSpec((tile, arr.shape[1]), lambda i: (i, 0))
    if kind == "kind":
        return pl.BlockSpec((1,) + arr.shape[1:], lambda i: (jnp.where(i < TILE // tile, 0, 1), 0, 0))
    nd = arr.ndim
    if kind in ("wfull", "wcols", "wrows"):
        return pl.BlockSpec(arr.shape, lambda i: (0,) * nd, pipeline_mode=pl.Buffered(1))
    return pl.BlockSpec(arr.shape, lambda i: (0,) * nd)


def _load(kind, ref):
    if kind == "kind":
        return ref[0]
    if kind == "wcols":
        return [ref[:, j * FF_CH:(j + 1) * FF_CH] for j in range(ref.shape[1] // FF_CH)]
    if kind == "wrows":
        return [ref[j * FF_CH:(j + 1) * FF_CH, :] for j in range(ref.shape[0] // FF_CH)]
    return ref[...]


def tile_forward(fn, name, t_all, ins, out_defs, tile=TILE):
    keys = [k for k, _, _, _ in ins]
    kinds = [kd for _, kd, _, _ in ins]
    diffs = [df for _, _, df, _ in ins]
    arrs = [a for _, _, _, a in ins]
    n_in = len(ins)

    def body(*refs):
        vals = [_load(kinds[j], refs[j]) for j in range(n_in)]
        d = {keys[j]: vals[j] for j in range(n_in) if diffs[j]}
        c = {keys[j]: vals[j] for j in range(n_in) if not diffs[j]}
        outs, _ = fn(d, c, None)
        for j, (k, _, dt) in enumerate(out_defs):
            refs[n_in + j][...] = outs[k].astype(dt)

    res = pl.pallas_call(
        body, name=name, grid=(t_all // tile,),
        in_specs=[_in_spec(kinds[j], arrs[j], tile) for j in range(n_in)],
        out_specs=[pl.BlockSpec((tile, w), lambda i: (i, 0)) for _, w, _ in out_defs],
        out_shape=[jax.ShapeDtypeStruct((t_all, w), dt) for _, w, dt in out_defs],
        compiler_params=_cparams(1),
    )(*arrs)
    return {k: r for (k, _, _), r in zip(out_defs, res)}


def tile_backward(fn, name, t_all, ins, cots, z_defs, aux_defs, tile=TILE, resid=None):
    keys = [k for k, _, _, _ in ins]
    kinds = [kd for _, kd, _, _ in ins]
    diffs = [df for _, _, df, _ in ins]
    arrs = [a for _, _, _, a in ins]
    cot_keys, cot_arrs = [], []
    for k, a in cots:
        for one in (a if isinstance(a, (list, tuple)) else [a]):
            cot_keys.append(k)
            cot_arrs.append(one)
    if resid is not None:
        cot_keys.append("resid:" + resid[0])
        cot_arrs.append(resid[1])
    n_in, n_cot = len(ins), len(cot_arrs)
    dkeys = [j for j in range(n_in) if diffs[j]]
    ctx_tiles = TILE // tile

    def body(*refs):
        i = pl.program_id(0)
        vals = [_load(kinds[j], refs[j]) for j in range(n_in)]
        d = {keys[j]: vals[j] for j in dkeys}
        c = {keys[j]: vals[j] for j in range(n_in) if not diffs[j]}
        zs = {k: jnp.zeros((tile, w), F32) for k, w in z_defs}
        outs, vjp_fn, aux = jax.vjp(lambda dd, zz: fn(dd, c, zz), d, zs, has_aux=True)
        ct = {}
        for j, k in enumerate(cot_keys):
            ct[k] = refs[n_in + j][...] + ct[k] if k in ct else refs[n_in + j][...]
        dd, dz = vjp_fn({k: ct[k].astype(outs[k].dtype) for k in outs})
        base = n_in + n_cot
        for n, j in enumerate(dkeys):
            ref, g = refs[base + n], dd[keys[j]]
            if kinds[j] == "tile":
                ref[...] = g + ct["resid:" + keys[j]] if "resid:" + keys[j] in ct else g
            else:
                first = ((i == 0) | (i == ctx_tiles)) if kinds[j] == "kind" else (i == 0)
                gv = g[None] if kinds[j] == "kind" else g

                @pl.when(first)
                def _(ref=ref, gv=gv):
                    ref[...] = gv

                @pl.when(jnp.logical_not(first))
                def _(ref=ref, gv=gv):
                    ref[...] += gv
        base += len(dkeys)
        for n, (k, _) in enumerate(z_defs):
            refs[base + n][...] = dz[k].astype(BF)
        base += len(z_defs)
        for n, (k, _) in enumerate(aux_defs):
            refs[base + n][...] = aux[k].T.astype(BF)

    out_specs, out_shape = [], []
    for j in dkeys:
        out_specs.append(_in_spec(kinds[j], arrs[j], tile))
        out_shape.append(jax.ShapeDtypeStruct(arrs[j].shape, F32))
    for _, w in z_defs:
        out_specs.append(pl.BlockSpec((tile, w), lambda i: (i, 0)))
        out_shape.append(jax.ShapeDtypeStruct((t_all, w), BF))
    for _, w in aux_defs:
        out_specs.append(pl.BlockSpec((w, tile), lambda i: (0, i)))
        out_shape.append(jax.ShapeDtypeStruct((w, t_all), BF))
    res = pl.pallas_call(
        body, name=name, grid=(t_all // tile,),
        in_specs=[_in_spec(kinds[j], arrs[j], tile) for j in range(n_in)]
        + [pl.BlockSpec((tile, a.shape[1]), lambda i: (i, 0)) for a in cot_arrs],
        out_specs=out_specs, out_shape=out_shape, compiler_params=_cparams(1),
    )(*arrs, *cot_arrs)
    grads = {keys[j]: res[n] for n, j in enumerate(dkeys)}
    extra = {k: res[len(dkeys) + n] for n, (k, _) in enumerate(list(z_defs) + list(aux_defs))}
    return grads, extra


def ffn_backward(x1, modl, n2w, w1, w2, dx2, f, name):
    t_all = x1.shape[0]
    n_ch = D_FF // FF_CH

    def head(x, mod, nw):
        return _rms(x, nw) * (1.0 + mod[4:5]) + mod[3:4]

    def body(x_ref, mod_ref, nw_ref, w1_ref, w2_ref, dx2_ref, f_ref, dx1_ref, dmod_ref, dnw_ref, zpre_ref, zf_ref, a1_ref,
             a2_ref):
        i = pl.program_id(0)
        mod = mod_ref[0]
        dx2 = dx2_ref[...]
        h2, vjp_head = jax.vjp(head, x_ref[...], mod, nw_ref[...])
        h2b = h2.astype(BF)
        dfb = (dx2 * mod[5:6]).astype(BF)
        f = f_ref[...]
        dh2 = jnp.zeros((TILE, D), F32)
        for j in range(n_ch):
            cs = slice(j * FF_CH, (j + 1) * FF_CH)
            a = jnp.maximum(_dot(h2b, w1_ref[:, cs]), 0.0)
            a2b = (a * a).astype(BF)
            dpre = (_dot_nt(dfb, w2_ref[cs, :]) * (2.0 * a)).astype(BF)
            dh2 = dh2 + _dot_nt(dpre, w1_ref[:, cs])
            zpre_ref[:, cs] = dpre
            a2_ref[:, cs] = a2b
        zf_ref[...] = (dx2 * mod[5:6]).T.astype(BF)
        a1_ref[...] = h2.T.astype(BF)
        dx1, dmod, dnw = vjp_head(dh2)
        dx1_ref[...] = dx2 + dx1
        row = lax.broadcasted_iota(jnp.int32, (8, D), 0)
        dmod = dmod + jnp.where(row == 5, jnp.sum(dx2 * f, axis=0, keepdims=True), 0.0)
        first_kind = (i == 0) | (i == 1)

        @pl.when(first_kind)
        def _():
            dmod_ref[0] = dmod

        @pl.when(jnp.logical_not(first_kind))
        def _():
            dmod_ref[0] += dmod

        @pl.when(i == 0)
        def _():
            dnw_ref[...] = dnw

        @pl.when(i > 0)
        def _():
            dnw_ref[...] += dnw

    tspec = lambda w: pl.BlockSpec((TILE, w), lambda i: (i, 0))
    once = lambda shp: pl.BlockSpec(shp, lambda i: (0, 0), pipeline_mode=pl.Buffered(1))
    kind = pl.BlockSpec((1, 8, D), lambda i: (jnp.minimum(i, 1), 0, 0))
    tr = pl.BlockSpec((D, TILE), lambda i: (0, i))
    return pl.pallas_call(
        body, name=name, grid=(t_all // TILE,),
        in_specs=[tspec(D), kind, pl.BlockSpec((1, D), lambda i: (0, 0)), once((D, D_FF)), once((D_FF, D)), tspec(D),
                  tspec(D)],
        out_specs=[tspec(D), kind, pl.BlockSpec((1, D), lambda i: (0, 0)), tspec(D_FF), tr, tr, tspec(D_FF)],
        out_shape=[jax.ShapeDtypeStruct((t_all, D), F32), jax.ShapeDtypeStruct((2, 8, D), F32),
                   jax.ShapeDtypeStruct((1, D), F32), jax.ShapeDtypeStruct((t_all, D_FF), BF),
                   jax.ShapeDtypeStruct((D, t_all), BF), jax.ShapeDtypeStruct((D, t_all), BF),
                   jax.ShapeDtypeStruct((t_all, D_FF), BF)],
        compiler_params=_cparams(1),
    )(x1, modl, n2w, w1, w2, dx2, f)


WG_TOK = 768


def wgrad(at, b, name, bk2=1024):
    k1, t = at.shape
    k2 = b.shape[1]
    bk2 = min(bk2, k2)
    tt = WG_TOK if t % WG_TOK == 0 else TILE
    nt_ = t // tt

    def body(a_ref, b_ref, o_ref, acc):
        s = pl.program_id(1)

        @pl.when(s == 0)
        def _():
            acc[...] = jnp.zeros_like(acc)

        acc[...] += _dot(a_ref[...], b_ref[...])

        @pl.when(s == nt_ - 1)
        def _():
            o_ref[...] = acc[...]

    return pl.pallas_call(
        body, name=name, grid=(k2 // bk2, nt_),
        in_specs=[pl.BlockSpec((k1, tt), lambda j, s: (0, s)), pl.BlockSpec((tt, bk2), lambda j, s: (s, j))],
        out_specs=pl.BlockSpec((k1, bk2), lambda j, s: (0, j)),
        out_shape=jax.ShapeDtypeStruct((k1, k2), F32),
        scratch_shapes=[pltpu.VMEM((k1, bk2), F32)],
        compiler_params=_cparams(2),
    )(at, b)


def _gla_consts(reverse, diff):
    r = lax.broadcasted_iota(jnp.int32, (GCH, GCH), 0)
    cc = lax.broadcasted_iota(jnp.int32, (GCH, GCH), 1)
    low = (r >= cc)
    tri = (jnp.logical_not(low) | (r == cc)) if reverse else low
    tri_f = tri.astype(F32)
    tri_t = (cc >= r) if not reverse else (cc <= r)
    hmk = [_lane_group_mask(128, 32, h) for h in range(HEADS)]
    hmv = [_lane_group_mask(256, 64, h) for h in range(HEADS)]
    e = lax.broadcasted_iota(jnp.int32, (256, 128), 0) // 64
    dk = lax.broadcasted_iota(jnp.int32, (256, 128), 1) // 32
    return dict(cum=make_cum(tri_f.astype(BF), tri_t.astype(F32).astype(BF), diff), ops=_ops(diff), reverse=reverse,
                tri4=jnp.concatenate([tri_f] * HEADS, axis=0), hmk=hmk, hmv=hmv, bd=(e == dk).astype(F32))


def gla_chunk(st, q, k, v, g, cs):
    nn, nt, tn = cs["ops"]
    b = cs["cum"](g)
    bl = jnp.sum(g, axis=0, keepdims=True)
    b_ref = jnp.sum(g[GCH // 2:] if cs["reverse"] else g[:GCH // 2], axis=0, keepdims=True)
    qe = q * jnp.exp(b)
    qs = q * jnp.exp(b - b_ref)
    ks = k * jnp.exp(b_ref - b)
    qstack = jnp.concatenate([qs * cs["hmk"][h] for h in range(HEADS)], axis=0)
    att = nt(qstack, ks) * cs["tri4"]
    ofull = nn(att, v)
    o = nt(qe, st)
    for h in range(HEADS):
        o = o + ofull[h * GCH:(h + 1) * GCH] * cs["hmv"][h]
    kd = k * jnp.exp(bl - b)
    st_new = st * jnp.exp(bl) + tn(v, kd) * cs["bd"]
    return st_new, o


def _gla_chunk_index(s, n_ch, reverse):
    ctx_ch = TILE // GCH
    if not reverse:
        return s
    return jnp.where(s < ctx_ch, ctx_ch - 1 - s, n_ch - 1 + ctx_ch - s)


def gla_forward(q, k, v, gf, gb, name):
    t = q.shape[0]
    n_ch = t // GCH

    def body(*refs):
        s = pl.program_id(0)
        for dr, reverse in enumerate((False, True)):
            q_ref, k_ref, v_ref, g_ref = refs[4 * dr:4 * dr + 4]
            o_ref, sst_ref = refs[8 + 2 * dr:8 + 2 * dr + 2]
            st = refs[12 + dr]

            @pl.when(s == 0)
            def _(st=st):
                st[...] = jnp.zeros_like(st)

            cur = st[...]
            sst_ref[0] = cur
            st_new, o = gla_chunk(cur, q_ref[...], k_ref[...], v_ref[...], g_ref[...], _gla_consts(reverse, False))
            o_ref[...] = o
            st[...] = st_new

    in_specs, out_specs, out_shape = [], [], []
    for reverse in (False, True):
        im = lambda s, reverse=reverse: (_gla_chunk_index(s, n_ch, reverse), 0)
        im3 = lambda s, reverse=reverse: (_gla_chunk_index(s, n_ch, reverse), 0, 0)
        in_specs += [pl.BlockSpec((GCH, 128), im), pl.BlockSpec((GCH, 128), im), pl.BlockSpec((GCH, 256), im),
                     pl.BlockSpec((GCH, 128), im)]
        out_specs += [pl.BlockSpec((GCH, 256), im), pl.BlockSpec((1, 256, 128), im3)]
        out_shape += [jax.ShapeDtypeStruct((t, 256), F32), jax.ShapeDtypeStruct((n_ch, 256, 128), F32)]
    return pl.pallas_call(
        body, name=name, grid=(n_ch,), in_specs=in_specs, out_specs=out_specs, out_shape=out_shape,
        scratch_shapes=[pltpu.VMEM((256, 128), F32), pltpu.VMEM((256, 128), F32)],
        compiler_params=_cparams(1),
    )(q, k, v, gf, q, k, v, gb)


def gla_backward(q, k, v, gf, gb, sst_f, sst_b, do, name):
    t = q.shape[0]
    n_ch = t // GCH

    def body(*refs):
        r = pl.program_id(0)
        for dr, reverse in enumerate((False, True)):
            q_ref, k_ref, v_ref, g_ref, sst_ref, do_ref = refs[6 * dr:6 * dr + 6]
            outs = refs[12 + 4 * dr:12 + 4 * dr + 4]
            dst = refs[20 + dr]

            @pl.when(r == 0)
            def _(dst=dst):
                dst[...] = jnp.zeros_like(dst)

            cs = _gla_consts(reverse, True)
            _, vjp_fn = jax.vjp(lambda a, b, c_, d_, e_, cs=cs: gla_chunk(a, b, c_, d_, e_, cs),
                                sst_ref[0], q_ref[...], k_ref[...], v_ref[...], g_ref[...])
            grads = vjp_fn((dst[...], do_ref[...]))
            for o_ref, gval in zip(outs, grads[1:]):
                o_ref[...] = gval
            dst[...] = grads[0]

    in_specs, out_specs, out_shape = [], [], []
    for reverse in (False, True):
        im = lambda r, reverse=reverse: (_gla_chunk_index(n_ch - 1 - r, n_ch, reverse), 0)
        im3 = lambda r, reverse=reverse: (_gla_chunk_index(n_ch - 1 - r, n_ch, reverse), 0, 0)
        in_specs += [pl.BlockSpec((GCH, 128), im), pl.BlockSpec((GCH, 128), im), pl.BlockSpec((GCH, 256), im),
                     pl.BlockSpec((GCH, 128), im), pl.BlockSpec((1, 256, 128), im3), pl.BlockSpec((GCH, 256), im)]
        out_specs += [pl.BlockSpec((GCH, 128), im), pl.BlockSpec((GCH, 128), im), pl.BlockSpec((GCH, 256), im),
                      pl.BlockSpec((GCH, 128), im)]
        out_shape += [jax.ShapeDtypeStruct((t, 128), F32), jax.ShapeDtypeStruct((t, 128), F32),
                      jax.ShapeDtypeStruct((t, 256), F32), jax.ShapeDtypeStruct((t, 128), F32)]
    return pl.pallas_call(
        body, name=name, grid=(n_ch,), in_specs=in_specs, out_specs=out_specs, out_shape=out_shape,
        scratch_shapes=[pltpu.VMEM((256, 128), F32), pltpu.VMEM((256, 128), F32)],
        compiler_params=_cparams(1),
    )(q, k, v, gf, sst_f, do, q, k, v, gb, sst_b, do)


def _resident(hbm_ref, vmem_ref, sem):
    cp = pltpu.make_async_copy(hbm_ref, vmem_ref, sem)
    cp.start()
    cp.wait()


def mla_forward(q_cat, k_cat, v, name):
    t = q_cat.shape[0]
    qt = QT_FWD if t % QT_FWD == 0 else TILE
    n_t = t // qt

    ch = KV_CH_FWD if (t - TILE) % KV_CH_FWD == 0 else KV_CH
    n_main = (t - TILE) // ch

    def body(q_ref, k_hbm, v_hbm, o_ref, lse_ref, k_s, v_s, m_s, l_s, acc_s, sem):
        i = pl.program_id(0)

        @pl.when(i == 0)
        def _():
            _resident(k_hbm, k_s, sem.at[0])
            _resident(v_hbm, v_s, sem.at[1])

        m_s[...] = jnp.full(m_s.shape, -1e30, F32)
        l_s[...] = jnp.zeros_like(l_s)
        acc_s[...] = jnp.zeros_like(acc_s)

        def chunk(r0, size, hide_ctx_rows=False):
            for h in range(HEADS):
                kh = k_s[pl.ds(r0, size), h * 256:(h + 1) * 256]
                vh = v_s[pl.ds(r0, size), h * 128:(h + 1) * 128]
                s = _dot_nt(q_ref[:, h * 256:(h + 1) * 256], kh)
                if hide_ctx_rows:
                    s = jnp.where(lax.broadcasted_iota(jnp.int32, (qt, 1), 0) < TILE, -1e30, s)
                m_prev = m_s[h]
                m_col = jnp.maximum(jnp.max(m_prev, axis=-1, keepdims=True), jnp.max(s, axis=-1, keepdims=True))
                m_next = jnp.broadcast_to(m_col, m_prev.shape)
                p = jnp.exp2(s - m_col)
                alpha = jnp.exp2(m_prev - m_next)
                l_s[h] = alpha * l_s[h] + jnp.sum(p, axis=-1, keepdims=True)
                acc_s[h] = alpha * acc_s[h] + _dot(p.astype(BF), vh)
                m_s[h] = m_next

        chunk(0, TILE)

        def main_loop(hide, size, unroll):
            def step(c, carry):
                chunk(pl.multiple_of(TILE + c * size, TILE), size, hide)
                return carry

            lax.fori_loop(0, (t - TILE) // size, step, 0, unroll=unroll)

        if qt > TILE:
            pl.when(i == 0)(lambda: main_loop(True, KV_CH, 1))
        pl.when(i >= 1)(lambda: main_loop(False, ch, MLA_UNROLL))

        lane = lax.broadcasted_iota(jnp.int32, (qt, 128), 1)
        cols = jnp.zeros((qt, 128), F32)
        for h in range(HEADS):
            o_ref[:, h * 128:(h + 1) * 128] = acc_s[h] / l_s[h]
            cols = jnp.where(lane == h, m_s[h] + jnp.log2(l_s[h]), cols)
        lse_ref[...] = cols.T[0:8, :]

    return pl.pallas_call(
        body, name=name, grid=(n_t,),
        in_specs=[pl.BlockSpec((qt, 1024), lambda i: (i, 0)), pl.BlockSpec(memory_space=pl.ANY),
                  pl.BlockSpec(memory_space=pl.ANY)],
        out_specs=[pl.BlockSpec((qt, 512), lambda i: (i, 0)), pl.BlockSpec((8, qt), lambda i: (0, i))],
        out_shape=[jax.ShapeDtypeStruct((t, 512), F32), jax.ShapeDtypeStruct((8, t), F32)],
        scratch_shapes=[pltpu.VMEM((t, 1024), BF), pltpu.VMEM((t, 512), BF), pltpu.VMEM((HEADS, qt, 128), F32),
                        pltpu.VMEM((HEADS, qt, 128), F32), pltpu.VMEM((HEADS, qt, 128), F32),
                        pltpu.SemaphoreType.DMA((2,))],
        compiler_params=_cparams(1),
    )(q_cat, k_cat, v)


def mla_delta(do, o, name):
    t = do.shape[0]

    def body(do_ref, o_ref, dl_ref, dob_ref):
        d = do_ref[...]
        prod = d * o_ref[...]
        rows = [jnp.sum(prod[:, h * 128:(h + 1) * 128], axis=-1, keepdims=True) for h in range(HEADS)]
        cols = jnp.concatenate(rows + [jnp.zeros((TILE, 128 - HEADS), F32)], axis=-1)
        dl_ref[...] = cols.T[0:8, :]
        dob_ref[...] = d.astype(BF)

    return pl.pallas_call(
        body, name=name, grid=(t // TILE,),
        in_specs=[pl.BlockSpec((TILE, 512), lambda i: (i, 0)), pl.BlockSpec((TILE, 512), lambda i: (i, 0))],
        out_specs=[pl.BlockSpec((8, TILE), lambda i: (0, i)), pl.BlockSpec((TILE, 512), lambda i: (i, 0))],
        out_shape=[jax.ShapeDtypeStruct((8, t), F32), jax.ShapeDtypeStruct((t, 512), BF)],
        compiler_params=_cparams(1),
    )(do, o)


def mla_backward(q_cat, k_cat, v, lse_rows, dl_rows, do_bf, name):
    t = q_cat.shape[0]
    kt = KT_BWD if t % KT_BWD == 0 else TILE
    n_t = t // kt
    ch = Q_CH_BWD if (t - TILE) % Q_CH_BWD == 0 else KV_CH
    n_main = (t - TILE) // ch

    def body(q_hbm, do_hbm, k_ref, v_ref, lse_ref, dl_ref, dq_ref, dk_ref, dv_ref, q_s, do_s, dk_s, dv_s, sem):
        h, j = pl.program_id(0), pl.program_id(1)

        @pl.when(j == 0)
        def _():
            _resident(q_hbm.at[:, pl.ds(pl.multiple_of(h * 256, 256), 256)], q_s, sem.at[0])
            _resident(do_hbm.at[:, pl.ds(pl.multiple_of(h * 128, 128), 128)], do_s, sem.at[1])
            dq_ref[...] = jnp.zeros_like(dq_ref)

        dk_s[...] = jnp.zeros_like(dk_s)
        dv_s[...] = jnp.zeros_like(dv_s)
        kh = k_ref[...]
        vh = v_ref[...]

        def chunk(r0, size, ctx_only=False):
            qh = q_s[pl.ds(r0, size), :]
            doh = do_s[pl.ds(r0, size), :]
            pt = jnp.exp2(_dot_nt(kh, qh) - lse_ref[pl.ds(h, 1), pl.ds(r0, size)])
            if ctx_only and kt > TILE:
                pt = jnp.where(lax.broadcasted_iota(jnp.int32, (kt, 1), 0) < TILE, pt, 0.0)
            dst = (pt * (_dot_nt(vh, doh) - dl_ref[pl.ds(h, 1), pl.ds(r0, size)])).astype(BF)
            dv_s[...] += _dot(pt.astype(BF), doh)
            dk_s[...] += _dot(dst, qh)
            dq_ref[pl.ds(r0, size), :] += _dot_tn(dst, kh)

        @pl.when(j == 0)
        def _():
            chunk(0, TILE, ctx_only=True)

        def step(c, carry):
            chunk(pl.multiple_of(TILE + c * ch, TILE), ch)
            return carry

        lax.fori_loop(0, n_main, step, 0, unroll=MLA_UNROLL)
        dk_ref[...] = dk_s[...] * LN2
        dv_ref[...] = dv_s[...]

        @pl.when(j == n_t - 1)
        def _():
            dq_ref[...] = dq_ref[...] * LN2

    rows = pl.BlockSpec((8, t), lambda h, j: (0, 0))
    hbm = pl.BlockSpec(memory_space=pl.ANY)
    return pl.pallas_call(
        body, name=name, grid=(HEADS, n_t),
        in_specs=[hbm, hbm, pl.BlockSpec((kt, 256), lambda h, j: (j, h)), pl.BlockSpec((kt, 128), lambda h, j: (j, h)),
                  rows, rows],
        out_specs=[pl.BlockSpec((t, 256), lambda h, j: (0, h)), pl.BlockSpec((kt, 256), lambda h, j: (j, h)),
                   pl.BlockSpec((kt, 128), lambda h, j: (j, h))],
        out_shape=[jax.ShapeDtypeStruct((t, 1024), F32), jax.ShapeDtypeStruct((t, 1024), F32),
                   jax.ShapeDtypeStruct((t, 512), F32)],
        scratch_shapes=[pltpu.VMEM((t, 256), BF), pltpu.VMEM((t, 128), BF), pltpu.VMEM((kt, 256), F32),
                        pltpu.VMEM((kt, 128), F32), pltpu.SemaphoreType.DMA((2,))],
        compiler_params=_cparams(2),
    )(q_cat, do_bf, k_cat, v, lse_rows, dl_rows)


def final_loss(xf, target, fnw, name):
    t = xf.shape[0]
    n_t = t // TILE

    def body(x_ref, t_ref, w_ref, loss_ref, dx_ref, dw_ref):
        i = pl.program_id(0)

        @pl.when(i == 0)
        def _():
            loss_ref[...] = jnp.zeros_like(loss_ref)
            dw_ref[...] = jnp.zeros_like(dw_ref)
            dx_ref[...] = jnp.zeros_like(dx_ref)

        @pl.when(i >= 1)
        def _():
            y, vjp_fn = jax.vjp(_rms, x_ref[...], w_ref[...])
            err = y - t_ref[...]
            loss_ref[...] += jnp.broadcast_to(0.5 * jnp.sum(jnp.mean(err * err, axis=-1, keepdims=True)), (8, 128))
            dx, dw = vjp_fn(err * (1.0 / D))
            dx_ref[...] = dx
            dw_ref[...] += dw

    return pl.pallas_call(
        body, name=name, grid=(n_t,),
        in_specs=[pl.BlockSpec((TILE, D), lambda i: (i, 0)), pl.BlockSpec((TILE, D), lambda i: (jnp.maximum(i - 1, 0), 0)),
                  pl.BlockSpec((1, D), lambda i: (0, 0))],
        out_specs=[pl.BlockSpec((8, 128), lambda i: (0, 0)), pl.BlockSpec((TILE, D), lambda i: (i, 0)),
                   pl.BlockSpec((1, D), lambda i: (0, 0))],
        out_shape=[jax.ShapeDtypeStruct((8, 128), F32), jax.ShapeDtypeStruct((t, D), F32),
                   jax.ShapeDtypeStruct((1, D), F32)],
        compiler_params=_cparams(1),
    )(xf, target, fnw)


def all_gather(xs, name):
    n = len(xs)
    blks = [tuple(x.shape) for x in xs]
    per = N_DEV - 1

    def body(*refs):
        x_refs, o_refs = refs[:n], refs[n:2 * n]
        ssem, rsem, lsem = refs[2 * n:]
        xi, yi, ci = lax.axis_index("x"), lax.axis_index("y"), lax.axis_index("c")
        me3 = (xi, yi, ci)
        me = 4 * xi + 2 * yi + ci
        flat = lambda d: 4 * d[0] + 2 * d[1] + d[2]
        sibling = (xi, yi, 1 - ci)
        chips = [(1 - xi, yi), (xi, 1 - yi), (1 - xi, 1 - yi)]

        def copy(a, k, block, to, src=None):
            rows = o_refs[a].at[flat(block)]
            return pltpu.make_async_remote_copy(
                src_ref=rows if src is None else src, dst_ref=rows,
                send_sem=ssem.at[a * per + k], recv_sem=rsem.at[a * per + k],
                device_id=to, device_id_type=pl.DeviceIdType.MESH)

        own = [pltpu.make_async_copy(x_refs[a], o_refs[a].at[me], lsem.at[a]) for a in range(n)]
        first = []
        for a in range(n):
            first.append(copy(a, 0, me3, sibling, src=x_refs[a]))
            first += [copy(a, 1 + j, me3, (*chip, ci), src=x_refs[a]) for j, chip in enumerate(chips)]
        for cp in own + first:
            cp.start()
        passed = []
        for j, chip in enumerate(chips):
            for a in range(n):
                copy(a, 1 + j, (*chip, ci), me3).wait_recv()
                fw = copy(a, 4 + j, (*chip, ci), sibling)
                fw.start()
                passed.append(fw)
        for a in range(n):
            copy(a, 0, sibling, me3).wait_recv()
            for j, chip in enumerate(chips):
                copy(a, 4 + j, (*chip, 1 - ci), me3).wait_recv()
        for cp in first + passed:
            cp.wait_send()
        for cp in own:
            cp.wait()

    hbm = pl.BlockSpec(memory_space=pl.ANY)
    res = pl.pallas_call(
        body, name=name, in_specs=[hbm] * n, out_specs=[hbm] * n,
        out_shape=[jax.ShapeDtypeStruct((N_DEV,) + blks[a], xs[a].dtype) for a in range(n)],
        scratch_shapes=[pltpu.SemaphoreType.DMA((n * per,)), pltpu.SemaphoreType.DMA((n * per,)),
                        pltpu.SemaphoreType.DMA((n,))],
        compiler_params=pltpu.CompilerParams(has_side_effects=True),
    )(*xs)
    return list(res)


def reduce_scatter(xs, name):
    n = len(xs)
    c_idx = lax.axis_index("c").astype(jnp.int32).reshape(1)
    hbm = pl.BlockSpec(memory_space=pl.ANY)
    side = pltpu.CompilerParams(has_side_effects=True)

    def pair_body(*refs):
        x_refs, o_refs = refs[:n], refs[n:2 * n]
        ssem, rsem = refs[2 * n:]
        xi, yi, cc = lax.axis_index("x"), lax.axis_index("y"), lax.axis_index("c")
        cps = [pltpu.make_async_remote_copy(
            src_ref=x_refs[a].at[1 - cc], dst_ref=o_refs[a], send_sem=ssem.at[a], recv_sem=rsem.at[a],
            device_id=(xi, yi, 1 - cc), device_id_type=pl.DeviceIdType.MESH) for a in range(n)]
        for cp in cps:
            cp.start()
        for cp in cps:
            cp.wait()

    halves = [tuple(x.shape[1:]) for x in xs]
    got = pl.pallas_call(
        pair_body, name=name + "_pair", in_specs=[hbm] * n, out_specs=[hbm] * n,
        out_shape=[jax.ShapeDtypeStruct(halves[a], xs[a].dtype) for a in range(n)],
        scratch_shapes=[pltpu.SemaphoreType.DMA((n,)), pltpu.SemaphoreType.DMA((n,))], compiler_params=side,
    )(*xs)

    sums = []
    for a in range(n):
        def add_body(c_ref, a_ref, b_ref, o_ref):
            o_ref[...] = (a_ref[0].astype(F32) + b_ref[...].astype(F32)).astype(o_ref.dtype)

        blk = pl.BlockSpec((1,) + halves[a][1:], lambda q, c_ref: (q, 0, 0))
        own = pl.BlockSpec((1, 1) + halves[a][1:], lambda q, c_ref: (c_ref[0], q, 0, 0))
        sums.append(pl.pallas_call(
            add_body, name=f"{name}_sum{a}",
            grid_spec=pltpu.PrefetchScalarGridSpec(num_scalar_prefetch=1, grid=(4,), in_specs=[own, blk], out_specs=blk),
            out_shape=jax.ShapeDtypeStruct(halves[a], xs[a].dtype), compiler_params=_cparams(1),
        )(c_idx, xs[a], got[a]))

    def chips_body(*refs):
        s_refs, o_refs = refs[:n], refs[n:2 * n]
        ssem, rsem, lsem = refs[2 * n:]
        xi, yi, cc = lax.axis_index("x"), lax.axis_index("y"), lax.axis_index("c")
        my_chip = 2 * xi + yi
        chips = [(1 - xi, yi), (xi, 1 - yi), (1 - xi, 1 - yi)]
        own = [pltpu.make_async_copy(s_refs[a].at[my_chip], o_refs[a].at[my_chip], lsem.at[a]) for a in range(n)]
        sends, lands = [], []
        for a in range(n):
            for j, (px, py) in enumerate(chips):
                common = dict(send_sem=ssem.at[3 * a + j], recv_sem=rsem.at[3 * a + j], device_id=(px, py, cc),
                              device_id_type=pl.DeviceIdType.MESH)
                sends.append(pltpu.make_async_remote_copy(src_ref=s_refs[a].at[2 * px + py],
                                                          dst_ref=o_refs[a].at[my_chip], **common))
                lands.append(pltpu.make_async_remote_copy(src_ref=s_refs[a].at[2 * px + py],
                                                          dst_ref=o_refs[a].at[2 * px + py], **common))
        for cp in own + sends:
            cp.start()
        for cp in lands:
            cp.wait_recv()
        for cp in sends:
            cp.wait_send()
        for cp in own:
            cp.wait()

    res = pl.pallas_call(
        chips_body, name=name + "_chips", in_specs=[hbm] * n, out_specs=[hbm] * n,
        out_shape=[jax.ShapeDtypeStruct(halves[a], xs[a].dtype) for a in range(n)],
        scratch_shapes=[pltpu.SemaphoreType.DMA((3 * n,)), pltpu.SemaphoreType.DMA((3 * n,)),
                        pltpu.SemaphoreType.DMA((n,))], compiler_params=side,
    )(*sums)
    return list(res)


def mod_forward(crows, w_mod, b_shard, name):
    cols = w_mod.shape[2]

    def body(c_ref, w_ref, b_ref, o_ref):
        o_ref[0] = _dot(_silu(c_ref[...]).astype(BF), w_ref[0].astype(BF)) + b_ref[0]

    return pl.pallas_call(
        body, name=name, grid=(2,),
        in_specs=[pl.BlockSpec((16, D), lambda l: (0, 0)), pl.BlockSpec((1, D, cols), lambda l: (l, 0, 0)),
                  pl.BlockSpec((1, 1, cols), lambda l: (l, 0, 0))],
        out_specs=pl.BlockSpec((1, 16, cols), lambda l: (l, 0, 0)),
        out_shape=jax.ShapeDtypeStruct((2, 16, cols), F32), compiler_params=_cparams(1),
    )(crows, w_mod, b_shard)


def mod_backward(crows, w_mod, d_own, d_ctx, name):
    cols = w_mod.shape[2]

    def body(c_ref, w_ref, do_ref, dc_ref, gw_ref, gs_ref):
        dc = dc_ref[0]
        dsum = dc[0:1]
        for s in range(1, N_DEV):
            dsum = dsum + dc[s:s + 1]
        row = lax.broadcasted_iota(jnp.int32, (8, cols), 0)
        d16 = jnp.concatenate([do_ref[0], jnp.where(row == 0, jnp.broadcast_to(dsum, (8, cols)), 0.0)], axis=0)
        gw_ref[0] = _dot_tn(_silu(c_ref[...]).astype(BF), d16.astype(BF))
        gs_ref[0] = _dot_nt(jnp.broadcast_to(dsum, (8, cols)).astype(BF), w_ref[0].astype(BF))

    return pl.pallas_call(
        body, name=name, grid=(2,),
        in_specs=[pl.BlockSpec((16, D), lambda l: (0, 0)), pl.BlockSpec((1, D, cols), lambda l: (l, 0, 0)),
                  pl.BlockSpec((1, 8, cols), lambda l: (l, 0, 0)), pl.BlockSpec((1, 8, cols), lambda l: (l, 0, 0))],
        out_specs=[pl.BlockSpec((1, D, cols), lambda l: (l, 0, 0)), pl.BlockSpec((1, 8, D), lambda l: (l, 0, 0))],
        out_shape=[jax.ShapeDtypeStruct((2, D, cols), F32), jax.ShapeDtypeStruct((2, 8, D), F32)],
        compiler_params=_cparams(1),
    )(crows, w_mod, d_own, d_ctx)


def silu_grad_scale(c_ctx, ds, name):
    def body(c_ref, ds_ref, o_ref):
        cc = c_ref[...]
        sg = jax.nn.sigmoid(cc)
        o_ref[...] = (ds_ref[0][0:1] + ds_ref[1][0:1]) * (sg * (1.0 + cc * (1.0 - sg)))

    return pl.pallas_call(body, name=name, out_shape=jax.ShapeDtypeStruct((1, D), F32))(c_ctx, ds)


def _adamw_math(p_ref, w_ref, m_ref, v_ref, g_ref, d_ref, nm_ref, nv_ref):
    g = p_ref[0].astype(F32)
    for s in range(1, p_ref.shape[0]):
        g = g + p_ref[s].astype(F32)
    mm = ADAM_B1 * m_ref[...] + (1.0 - ADAM_B1) * g
    vv = ADAM_B2 * v_ref[...] + (1.0 - ADAM_B2) * (g * g)
    m_hat = mm / (1.0 - ADAM_B1 ** ADAM_STEP)
    v_hat = vv / (1.0 - ADAM_B2 ** ADAM_STEP)
    g_ref[...] = g
    d_ref[...] = -ADAM_LR * (m_hat / (jnp.sqrt(v_hat) + ADAM_EPS) + ADAM_WD * w_ref[...])
    nm_ref[...] = mm
    nv_ref[...] = vv


def adamw(parts, w, m, v, name):
    n_parts, rows, cols = parts.shape
    lanes = -(-cols // 128) * 128
    block_rows = min(rows, 1 << ((ADAMW_BLOCK_ELEMS // lanes).bit_length() - 1))
    assert rows % block_rows == 0

    def body(*refs):
        _adamw_math(*refs)

    spec = pl.BlockSpec((block_rows, cols), lambda i: (i, 0))
    return pl.pallas_call(
        body, name=name, grid=(rows // block_rows,),
        in_specs=[pl.BlockSpec((n_parts, block_rows, cols), lambda i: (0, i, 0)), spec, spec, spec],
        out_specs=[spec] * 4, out_shape=[jax.ShapeDtypeStruct((rows, cols), F32)] * 4,
        compiler_params=_cparams(1),
    )(parts, w, m, v)


def adamw_group(items, name):
    n = len(items)

    def body(*refs):
        for a in range(n):
            _adamw_math(*refs[4 * a:4 * a + 4], *refs[4 * n + 4 * a:4 * n + 4 * a + 4])

    flat_in = [x for it in items for x in it]
    out_shape = [jax.ShapeDtypeStruct(it[1].shape, F32) for it in items for _ in range(4)]
    res = pl.pallas_call(body, name=name, out_shape=out_shape,
                         compiler_params=pltpu.CompilerParams(vmem_limit_bytes=VMEM_LIMIT_MB * 1024 * 1024))(*flat_in)
    return [tuple(res[4 * a:4 * a + 4]) for a in range(n)]


def _pad_cols(w, segs, total):
    parts, pos = [], 0
    for dst, src, wd in segs:
        if dst > pos:
            parts.append(jnp.zeros(w.shape[:-1] + (dst - pos,), w.dtype))
        parts.append(w[..., src:src + wd])
        pos = dst + wd
    if pos < total:
        parts.append(jnp.zeros(w.shape[:-1] + (total - pos,), w.dtype))
    return jnp.concatenate(parts, axis=-1)


def _unpad_cols(g, segs):
    return jnp.concatenate([g[..., dst:dst + wd] for dst, _, wd in segs], axis=-1)


def _rope_tables(n_lat):
    rows = n_lat // GRID_W
    freq = ROPE_BASE ** (-jnp.arange(16, dtype=F32) * 2.0 / 32)
    a_row = jnp.arange(rows).astype(F32)[:, None] * freq[None, :]
    a_col = jnp.arange(GRID_W).astype(F32)[:, None] * freq[None, :]
    per_row = lambda tbl: jnp.repeat(tbl, GRID_W, axis=0)
    per_col = lambda tbl: jnp.tile(tbl, (rows, 1))
    cr, sr, cc, sc = per_row(jnp.cos(a_row)), per_row(jnp.sin(a_row)), per_col(jnp.cos(a_col)), per_col(jnp.sin(a_col))
    z = jnp.zeros((n_lat, 16), F32)
    cos = jnp.concatenate([cr, cr, cc, cc, jnp.ones((n_lat, 64), F32)], axis=1)
    sa = jnp.concatenate([-sr, z, -sc, z, jnp.zeros((n_lat, 64), F32)], axis=1)
    sb = jnp.concatenate([z, sr, z, sc, jnp.zeros((n_lat, 64), F32)], axis=1)
    ident = lambda fill: jnp.full((TILE, 128), fill, F32)
    return (jnp.concatenate([ident(1.0), cos]), jnp.concatenate([ident(0.0), sa]), jnp.concatenate([ident(0.0), sb]))


def _gathered_to_full(g, name):
    if name in ("w_out", "w_ff2"):
        return jnp.transpose(g, (1, 0, 2, 3)).reshape(2, -1, g.shape[-1])
    return jnp.transpose(g, (1, 2, 0, 3)).reshape(2, g.shape[2], -1)


def _layer_to_halves(g, name):
    if name in ("w_out", "w_ff2"):
        return jnp.transpose(g.reshape(4, 2, -1, g.shape[-1]), (1, 0, 2, 3))
    return jnp.transpose(g.reshape(g.shape[0], 4, 2, -1), (2, 1, 0, 3))


def kernel(x, c, ctx, c_ctx, w_mod, b_mod, norm1_w, w_in, w_out, sgu_norm_w, sgu_norm_b, sgu_w, sgu_b, gla_wg_fwd, gla_bg_fwd, gla_wg_bwd, gla_bg_bwd, gla_norm_w, mla_q_norm_w, mla_w_uq, mla_kv_norm_w, mla_w_ukv, norm2_w, w_ff1, w_ff2, final_norm_w, loss_target, m_c_ctx, m_w_mod, m_b_mod, m_norm1_w, m_w_in, m_w_out, m_sgu_norm_w, m_sgu_norm_b, m_sgu_w, m_sgu_b, m_gla_wg_fwd, m_gla_bg_fwd, m_gla_wg_bwd, m_gla_bg_bwd, m_gla_norm_w, m_mla_q_norm_w, m_mla_w_uq, m_mla_kv_norm_w, m_mla_w_ukv, m_norm2_w, m_w_ff1, m_w_ff2, m_final_norm_w, v_c_ctx, v_w_mod, v_b_mod, v_norm1_w, v_w_in, v_w_out, v_sgu_norm_w, v_sgu_norm_b, v_sgu_w, v_sgu_b, v_gla_wg_fwd, v_gla_bg_fwd, v_gla_wg_bwd, v_gla_bg_bwd, v_gla_norm_w, v_mla_q_norm_w, v_mla_w_uq, v_mla_kv_norm_w, v_mla_w_ukv, v_norm2_w, v_w_ff1, v_w_ff2, v_final_norm_w):
    W = dict(c_ctx=c_ctx, w_mod=w_mod, b_mod=b_mod, norm1_w=norm1_w, w_in=w_in, w_out=w_out, sgu_norm_w=sgu_norm_w,
             sgu_norm_b=sgu_norm_b, sgu_w=sgu_w, sgu_b=sgu_b, gla_wg_fwd=gla_wg_fwd, gla_bg_fwd=gla_bg_fwd,
             gla_wg_bwd=gla_wg_bwd, gla_bg_bwd=gla_bg_bwd, gla_norm_w=gla_norm_w, mla_q_norm_w=mla_q_norm_w,
             mla_w_uq=mla_w_uq, mla_kv_norm_w=mla_kv_norm_w, mla_w_ukv=mla_w_ukv, norm2_w=norm2_w, w_ff1=w_ff1,
             w_ff2=w_ff2, final_norm_w=final_norm_w)
    M = dict(c_ctx=m_c_ctx, w_mod=m_w_mod, b_mod=m_b_mod, norm1_w=m_norm1_w, w_in=m_w_in, w_out=m_w_out,
             sgu_norm_w=m_sgu_norm_w, sgu_norm_b=m_sgu_norm_b, sgu_w=m_sgu_w, sgu_b=m_sgu_b, gla_wg_fwd=m_gla_wg_fwd,
             gla_bg_fwd=m_gla_bg_fwd, gla_wg_bwd=m_gla_wg_bwd, gla_bg_bwd=m_gla_bg_bwd, gla_norm_w=m_gla_norm_w,
             mla_q_norm_w=m_mla_q_norm_w, mla_w_uq=m_mla_w_uq, mla_kv_norm_w=m_mla_kv_norm_w, mla_w_ukv=m_mla_w_ukv,
             norm2_w=m_norm2_w, w_ff1=m_w_ff1, w_ff2=m_w_ff2, final_norm_w=m_final_norm_w)
    V = dict(c_ctx=v_c_ctx, w_mod=v_w_mod, b_mod=v_b_mod, norm1_w=v_norm1_w, w_in=v_w_in, w_out=v_w_out,
             sgu_norm_w=v_sgu_norm_w, sgu_norm_b=v_sgu_norm_b, sgu_w=v_sgu_w, sgu_b=v_sgu_b, gla_wg_fwd=v_gla_wg_fwd,
             gla_bg_fwd=v_gla_bg_fwd, gla_wg_bwd=v_gla_wg_bwd, gla_bg_bwd=v_gla_bg_bwd, gla_norm_w=v_gla_norm_w,
             mla_q_norm_w=v_mla_q_norm_w, mla_w_uq=v_mla_w_uq, mla_kv_norm_w=v_mla_kv_norm_w, mla_w_ukv=v_mla_w_ukv,
             norm2_w=v_norm2_w, w_ff1=v_w_ff1, w_ff2=v_w_ff2, final_norm_w=v_final_norm_w)

    n_lat = x.shape[1]
    assert ctx.shape[1] == TILE and n_lat % TILE == 0 and x.shape[2] == D
    t_all = TILE + n_lat
    n_t = t_all // TILE
    me = 4 * lax.axis_index("x") + 2 * lax.axis_index("y") + lax.axis_index("c")
    mod_cols = w_mod.shape[2]

    c_all = all_gather([c], "ag_c")[0].reshape(N_DEV, D)
    crows = jnp.concatenate([c_all, c_ctx[None, :], jnp.zeros((7, D), F32)], axis=0)
    b_shard = lax.dynamic_slice_in_dim(b_mod, me * mod_cols, mod_cols, axis=1)[:, None, :]
    mod_sh = mod_forward(crows, w_mod, b_shard, "mod_fwd")
    mod_g = all_gather([mod_sh.reshape(32, mod_cols)], "ag_mod")[0]
    mod_full = jnp.transpose(mod_g.reshape(N_DEV, 2, 16, mod_cols), (1, 2, 0, 3)).reshape(2, 16, 6 * D)
    mod_own = lax.dynamic_index_in_dim(mod_full, me, axis=1, keepdims=False)
    mod_ctx = mod_full[:, 8, :]
    pad2 = jnp.zeros((2, D), F32)
    modl = [jnp.stack([jnp.concatenate([mod_ctx[l].reshape(6, D), pad2]),
                       jnp.concatenate([mod_own[l].reshape(6, D), pad2])]) for l in range(2)]

    v2 = lambda a: a[None, :] if a.ndim == 1 else a.reshape(-1, a.shape[-1])
    gathered = all_gather([v2(W[k].astype(BF)) for k in BIG_NAMES], "ag_weights")
    full = {k: _gathered_to_full(g.reshape((N_DEV,) + W[k].shape), k) for k, g in zip(BIG_NAMES, gathered)}
    w_in_p = _pad_cols(full["w_in"], W_IN_SEGS, P_COLS)
    w_uq_p = _pad_cols(full["mla_w_uq"], W_UQ_SEGS, 1024).astype(F32)
    w_ukv_f = full["mla_w_ukv"].astype(F32)
    wgf_p = jnp.pad(gla_wg_fwd, ((0, 0), (0, 112), (0, 0)))
    wgb_p = jnp.pad(gla_wg_bwd, ((0, 0), (0, 112), (0, 0)))
    sgu_bx = jnp.repeat(jnp.transpose(sgu_b, (0, 2, 1)), 64, axis=2)
    gnw_t = jnp.tile(gla_norm_w, (1, HEADS))
    rc, rsa, rsb = _rope_tables(n_lat)

    xin = jnp.concatenate([ctx[0], x[0]], axis=0)
    row = lambda a: a[None, :]

    def pre_ins(l, xl):
        return [("x", "tile", True, xl), ("mod", "kind", True, modl[l]), ("n1w", "full", True, row(norm1_w[l])),
                ("w_in", "wfull", False, w_in_p[l]), ("sgu_nw", "full", True, row(sgu_norm_w[l])),
                ("sgu_nb", "full", True, row(sgu_norm_b[l])), ("sgu_w", "full", True, sgu_w[l]),
                ("sgu_bx", "full", True, sgu_bx[l]), ("wgf", "full", True, wgf_p[l]), ("bgf", "full", True, row(gla_bg_fwd[l])),
                ("wgb", "full", True, wgb_p[l]), ("bgb", "full", True, row(gla_bg_bwd[l])),
                ("qnw", "full", True, row(mla_q_norm_w[l])), ("w_uq", "full", True, w_uq_p[l]),
                ("kvnw", "full", True, row(mla_kv_norm_w[l])), ("w_ukv", "full", True, w_ukv_f[l]),
                ("rc", "tile", False, rc), ("rsa", "tile", False, rsa), ("rsb", "tile", False, rsb)]

    pre_outs = [("y_sgu", 256, F32), ("qg", 128, F32), ("kg", 128, F32), ("vg", 256, F32), ("lgf", 128, F32),
                ("lgb", 128, F32), ("gr", 256, F32), ("q_cat", 1024, BF), ("k_cat", 1024, BF), ("v", 512, BF)]

    def out_ins(l, xl, a):
        return [("x", "tile", True, xl), ("mod", "kind", True, modl[l]), ("y_sgu", "tile", True, a["y_sgu"]),
                ("o_f", "tile", True, a["o_f"]), ("o_b", "tile", False, a["o_b"]), ("gr", "tile", True, a["gr"]),
                ("y_mla", "tile", True, a["y_mla"]), ("gnw", "full", True, row(gnw_t[l])),
                ("w_out", "wfull", False, full["w_out"][l])]

    def ffn_ins(l, x1):
        return [("x1", "tile", True, x1), ("mod", "kind", True, modl[l]), ("n2w", "full", True, row(norm2_w[l])),
                ("w_ff1", "wcols", False, full["w_ff1"][l]), ("w_ff2", "wrows", False, full["w_ff2"][l])]

    saved, xl = [], xin
    for l in range(2):
        a = tile_forward(pre_tile, f"pre_fwd{l}", t_all, pre_ins(l, xl), pre_outs)
        a["o_f"], a["sf"], a["o_b"], a["sb"] = gla_forward(a["qg"], a["kg"], a["vg"], a["lgf"], a["lgb"], f"gla_fwd{l}")
        a["y_mla"], a["lse"] = mla_forward(a["q_cat"], a["k_cat"], a["v"], f"mla_fwd{l}")
        a["x"] = xl
        ff = tile_forward(out_ffn_tile, f"out_ffn_fwd{l}", t_all, out_ins(l, xl, a) + ffn_ins(l, None)[2:],
                          [("x1", D, F32), ("x2", D, F32), ("f", D, F32)])
        a["x1"], xl, a["f"] = ff["x1"], ff["x2"], ff["f"]
        saved.append(a)

    loss_blk, dx, d_fnw = final_loss(xl, loss_target[0], row(final_norm_w), "final_loss")
    loss = lax.psum(loss_blk[0, 0], AXES)

    G = {}
    dmods = []
    for l in (1, 0):
        a = saved[l]
        dx1, dmod3, dn2w, zpre, zf_t, h2_t, a_ff2 = ffn_backward(
            a["x1"], modl[l], row(norm2_w[l]), full["w_ff1"][l], full["w_ff2"][l], dx, a["f"], f"ffn_bwd{l}")
        gw_ff1 = wgrad(h2_t, zpre, f"wg_ff1_{l}")
        gw_ff2 = jnp.transpose(wgrad(zf_t, a_ff2, f"wg_ff2_{l}"))
        g2, e2 = tile_backward(attn_out_tile, f"out_bwd{l}", t_all, out_ins(l, a["x"], a), [("x1", dx1)],
                               [("zt", D)], [("a_out", D)])
        gw_out = wgrad(e2["a_out"], e2["zt"], f"wg_out_{l}")
        dl_rows, do_bf = mla_delta(g2["y_mla"], a["y_mla"], f"mla_delta{l}")
        dq_cat, dk_cat, dv = mla_backward(a["q_cat"], a["k_cat"], a["v"], a["lse"], dl_rows, do_bf, f"mla_bwd{l}")
        dqf, dkf, dvf, dgf, dqb, dkb, dvb, dgb = gla_backward(
            a["qg"], a["kg"], a["vg"], a["lgf"], a["lgb"], a["sf"], a["sb"], g2["o_f"], f"gla_bwd{l}")
        cots = [("y_sgu", g2["y_sgu"]), ("qg", [dqf, dqb]), ("kg", [dkf, dkb]), ("vg", [dvf, dvb]), ("lgf", dgf),
                ("lgb", dgb), ("gr", g2["gr"]), ("q_cat", dq_cat), ("k_cat", dk_cat), ("v", dv)]
        g1, e1 = tile_backward(pre_tile, f"pre_bwd{l}", t_all, pre_ins(l, a["x"]), cots, [("zp", P_COLS)], [("a_in", D)],
                               resid=("x", g2["x"]))
        gw_in = _unpad_cols(wgrad(e1["a_in"], e1["zp"], f"wg_in_{l}", bk2=P_COLS), W_IN_SEGS)
        dx = g1["x"]
        dmods.append(g1["mod"] + g2["mod"] + dmod3)
        G[l] = dict(w_in=gw_in, w_out=gw_out, w_ff1=gw_ff1, w_ff2=gw_ff2,
                    mla_w_uq=_unpad_cols(g1["w_uq"], W_UQ_SEGS), mla_w_ukv=g1["w_ukv"],
                    norm1_w=g1["n1w"][0], norm2_w=dn2w[0], sgu_norm_w=g1["sgu_nw"][0], sgu_norm_b=g1["sgu_nb"][0],
                    sgu_w=g1["sgu_w"], sgu_b=jnp.transpose(g1["sgu_bx"].reshape(128, HEADS, 64).sum(-1)),
                    gla_wg_fwd=g1["wgf"][:16], gla_bg_fwd=g1["bgf"][0], gla_wg_bwd=g1["wgb"][:16], gla_bg_bwd=g1["bgb"][0],
                    gla_norm_w=g2["gnw"][0].reshape(HEADS, 64).sum(0), mla_q_norm_w=g1["qnw"][0], mla_kv_norm_w=g1["kvnw"][0])
    dmods = dmods[::-1]
    grad_x = dx[TILE:][None]

    dmod_pack = jnp.stack([jnp.stack([dmods[l][1, :6].reshape(-1), dmods[l][0, :6].reshape(-1)]) for l in range(2)])
    dmod_all = all_gather([dmod_pack.reshape(4, 6 * D)], "ag_dmod")[0].reshape(N_DEV, 2, 2, 6 * D)
    dsl = lax.dynamic_slice_in_dim(dmod_all, me * mod_cols, mod_cols, axis=3)
    d_own = jnp.transpose(dsl[:, :, 0, :], (1, 0, 2))
    d_ctx = jnp.transpose(dsl[:, :, 1, :], (1, 0, 2))
    g_w_mod, ds_cc = mod_backward(crows, w_mod, d_own, d_ctx, "mod_bwd")
    g_c_ctx_part = silu_grad_scale(c_ctx[None, :], ds_cc, "silu_bwd")[0]
    g_b_mod_part = jnp.stack([dmods[l][1, :6].reshape(-1) + dmods[l][0, :6].reshape(-1) for l in range(2)])

    small_g = dict(c_ctx=g_c_ctx_part, b_mod=g_b_mod_part, final_norm_w=d_fnw[0])
    for k in SMALL_NAMES:
        if k not in small_g:
            small_g[k] = jnp.stack([G[0][k], G[1][k]])
    res = {}
    sparts = all_gather([v2(small_g[k]) for k in SMALL_NAMES], "ag_small")
    s_out = adamw_group([(sparts[j], v2(W[k]), v2(M[k]), v2(V[k])) for j, k in enumerate(SMALL_NAMES)], "adamw_small")
    for j, k in enumerate(SMALL_NAMES):
        res[k] = [o.reshape(W[k].shape) for o in s_out[j]]

    chunks = []
    for k in BIG_NAMES:
        chunks.append(jnp.concatenate([_layer_to_halves(G[l][k].astype(BF), k) for l in range(2)], axis=2))
    bparts = reduce_scatter(chunks, "rs_grads")
    for j, k in enumerate(BIG_NAMES):
        res[k] = [o.reshape(W[k].shape) for o in adamw(bparts[j], v2(W[k]), v2(M[k]), v2(V[k]), f"adamw_{k}")]
    res["w_mod"] = [o.reshape(w_mod.shape)
                    for o in adamw(v2(g_w_mod)[None], v2(w_mod), v2(m_w_mod), v2(v_w_mod), "adamw_w_mod")]
    outs = [loss, grad_x]
    for j in range(4):
        outs += [res[k][j] for k in WEIGHT_ORDER]
    return tuple(outs)
```

```python
import jax
import jax.numpy as jnp
from jax import lax
from jax.experimental import pallas as pl
from jax.experimental.pallas import tpu as pltpu

F32 = jnp.float32
BF = jnp.bfloat16

N_DEV = 8
AXES = ("x", "y", "c")
EPS = 1e-6
D = 1024
TILE = 256
GCH = 128
SGU_CHUNK = 128
HEADS = 4
ROPE_BASE = 10000.0
GRID_W = 64
GLA_TAU = 16.0
ATT_SCALE = (128 + 64) ** -0.5
ATT_SCALE_LOG2 = ATT_SCALE * 1.4426950408889634
LN2 = 0.6931471805599453
KV_CH = 512
KV_CH_FWD = 2048
QT_FWD = 256
Q_CH_BWD = 1024
KT_BWD = 768
MLA_UNROLL = 2
D_FF = 4096
FF_CH = 1024

ADAM_LR = 0.001
ADAM_B1 = 0.9
ADAM_B2 = 0.999
ADAM_EPS = 1e-08
ADAM_WD = 0.01
ADAM_STEP = 10

VMEM_LIMIT_MB = 56
ADAMW_BLOCK_ELEMS = 256 * 1024

W_IN_SEGS = ((0, 0, 128), (128, 128, 256), (384, 384, 16), (512, 400, 16), (640, 416, 256), (896, 672, 64),
             (1024, 736, 256), (1280, 992, 256), (1536, 1248, 128), (1664, 1376, 256), (1920, 1632, 256))
P_COLS = 2176
O_GK, O_GV, O_GGF, O_GGB, O_CKV, O_KR, O_SU, O_SV, O_GQ, O_GR, O_DQ = (s[0] for s in W_IN_SEGS)
W_UQ_SEGS = tuple((h * 256, h * 192, 192) for h in range(HEADS))

SMALL_NAMES = ("c_ctx", "b_mod", "norm1_w", "sgu_norm_w", "sgu_norm_b", "sgu_w", "sgu_b", "gla_wg_fwd", "gla_bg_fwd",
               "gla_wg_bwd", "gla_bg_bwd", "gla_norm_w", "mla_q_norm_w", "mla_kv_norm_w", "norm2_w", "final_norm_w")
BIG_NAMES = ("w_in", "w_out", "mla_w_uq", "mla_w_ukv", "w_ff1", "w_ff2")
WEIGHT_ORDER = ("c_ctx", "w_mod", "b_mod", "norm1_w", "w_in", "w_out", "sgu_norm_w", "sgu_norm_b", "sgu_w", "sgu_b",
                "gla_wg_fwd", "gla_bg_fwd", "gla_wg_bwd", "gla_bg_bwd", "gla_norm_w", "mla_q_norm_w", "mla_w_uq",
                "mla_kv_norm_w", "mla_w_ukv", "norm2_w", "w_ff1", "w_ff2", "final_norm_w")


def _cparams(n_axes):
    return pltpu.CompilerParams(dimension_semantics=("arbitrary",) * n_axes,
                                vmem_limit_bytes=VMEM_LIMIT_MB * 1024 * 1024)


def _dot(a, b):
    return jnp.dot(a, b, preferred_element_type=F32)


def _dot_nt(a, b):
    return lax.dot_general(a, b, (((1,), (1,)), ((), ())), preferred_element_type=F32)


def _dot_tn(a, b):
    return lax.dot_general(a, b, (((0,), (0,)), ((), ())), preferred_element_type=F32)


def _nn(a, b):
    return _dot(a.astype(BF), b.astype(BF))


def _nt(a, b):
    return _dot_nt(a.astype(BF), b.astype(BF))


def _tn(a, b):
    return _dot_tn(a.astype(BF), b.astype(BF))


nn_d = jax.custom_vjp(_nn)
nt_d = jax.custom_vjp(_nt)
tn_d = jax.custom_vjp(_tn)
nn_d.defvjp(lambda a, b: (_nn(a, b), (a, b)), lambda r, dy: (_nt(dy, r[1]), _tn(r[0], dy)))
nt_d.defvjp(lambda a, b: (_nt(a, b), (a, b)), lambda r, dy: (_nn(dy, r[1]), _tn(dy, r[0])))
tn_d.defvjp(lambda a, b: (_tn(a, b), (a, b)), lambda r, dy: (_nt(r[1], dy), _nn(r[0], dy)))


def nn_const(w_bf, diff):
    def raw(a):
        return _dot(a.astype(BF), w_bf)

    if not diff:
        return raw
    f = jax.custom_vjp(raw)
    f.defvjp(lambda a: (raw(a), None), lambda _, dy: (_dot_nt(dy.astype(BF), w_bf),))
    return f


def _split3(g):
    hi = g.astype(BF)
    r = g - hi.astype(F32)
    mid = r.astype(BF)
    lo = (r - mid.astype(F32)).astype(BF)
    return hi, mid, lo


def make_cum(tri_bf, tri_t_bf, diff):
    def raw(g, t):
        hi, mid, lo = _split3(g)
        return _dot(t, hi) + _dot(t, mid) + _dot(t, lo)

    def fwd(g):
        return raw(g, tri_bf)

    if not diff:
        return fwd
    cum = jax.custom_vjp(fwd)
    cum.defvjp(lambda g: (fwd(g), None), lambda _, db: (raw(db, tri_t_bf),))
    return cum


def _roll_lanes(x, shift):
    return pltpu.roll(x, shift, 1)


def make_rope(c, sa, sb, diff):
    def raw(x):
        return x * c + _roll_lanes(x, 112) * sa + _roll_lanes(x, 16) * sb

    if not diff:
        return raw
    f = jax.custom_vjp(raw)
    f.defvjp(lambda x: (raw(x), None),
             lambda _, dy: (dy * c + _roll_lanes(dy * sa, 16) + _roll_lanes(dy * sb, 112),))
    return f


def _ops(diff):
    return (nn_d, nt_d, tn_d) if diff else (_nn, _nt, _tn)


def _rms(x, w):
    return x * lax.rsqrt(jnp.mean(x * x, axis=-1, keepdims=True) + EPS) * w


def _gelu(x):
    return 0.5 * x * (1.0 + jnp.tanh(0.7978845608028654 * (x + 0.044715 * (x * x * x))))


def _silu(x):
    return x * jax.nn.sigmoid(x)


def _log_sigmoid(z):
    return jnp.minimum(z, 0.0) - jnp.log(1.0 + jnp.exp(-jnp.abs(z)))


def _lane_group_mask(width, group, h):
    lane = lax.broadcasted_iota(jnp.int32, (1, width), 1)
    return ((lane >= h * group) & (lane < (h + 1) * group)).astype(F32)


def pre_tile(d, c, z):
    nn, _, _ = _ops(z is not None)
    rope = make_rope(c["rc"], c["rsa"], c["rsb"], z is not None)
    mod = d["mod"]
    h = _rms(d["x"], d["n1w"]) * (1.0 + mod[1:2]) + mod[0:1]
    p = nn_const(c["w_in"], z is not None)(h)
    if z is not None:
        p = p + z["zp"]
    gk, gv = p[:, O_GK:O_GK + 128], p[:, O_GV:O_GV + 256]
    ggf, ggb = p[:, O_GGF:O_GGF + 128], p[:, O_GGB:O_GGB + 128]
    ckv, kr = p[:, O_CKV:O_CKV + 256], p[:, O_KR:O_KR + 128]
    su, sv = p[:, O_SU:O_SU + 256], p[:, O_SV:O_SV + 256]
    gq, gr, dq = p[:, O_GQ:O_GQ + 128], p[:, O_GR:O_GR + 256], p[:, O_DQ:O_DQ + 256]

    u = _gelu(su)
    gv_ = _gelu(sv)
    mu = jnp.mean(gv_, axis=-1, keepdims=True)
    cen = gv_ - mu
    vn = cen * lax.rsqrt(jnp.mean(cen * cen, axis=-1, keepdims=True) + EPS) * d["sgu_nw"] + d["sgu_nb"]
    hm = [_lane_group_mask(256, 64, hh) for hh in range(HEADS)]
    rows = []
    for ci in range(vn.shape[0] // SGU_CHUNK):
        vc = vn[ci * SGU_CHUNK:(ci + 1) * SGU_CHUNK]
        s = d["sgu_bx"]
        for hh in range(HEADS):
            s = s + hm[hh] * nn(d["sgu_w"][hh], vc)
        rows.append(s)
    y_sgu = u * jnp.concatenate(rows, axis=0)

    qg = gq * (32 ** -0.5)
    lgf = _log_sigmoid(nn(ggf, d["wgf"]) + d["bgf"]) * (1.0 / GLA_TAU)
    lgb = _log_sigmoid(nn(ggb, d["wgb"]) + d["bgb"]) * (1.0 / GLA_TAU)

    kv = nn(_rms(ckv, d["kvnw"]), d["w_ukv"])
    kr_r = rope(kr)
    q = nn(_rms(dq, d["qnw"]), d["w_uq"])
    qs, ks, vs = [], [], []
    for hh in range(HEADS):
        qs += [q[:, hh * 256:hh * 256 + 128], rope(q[:, hh * 256 + 128:(hh + 1) * 256])]
        ks += [kv[:, hh * 256:hh * 256 + 128], kr_r]
        vs += [kv[:, hh * 256 + 128:(hh + 1) * 256]]
    outs = dict(y_sgu=y_sgu, qg=qg, kg=gk, vg=gv, lgf=lgf, lgb=lgb, gr=gr,
                q_cat=jnp.concatenate(qs, axis=-1) * ATT_SCALE_LOG2, k_cat=jnp.concatenate(ks, axis=-1), v=jnp.concatenate(vs, axis=-1))
    return outs, dict(a_in=h)


def attn_out_tile(d, c, z):
    mod = d["mod"]
    o = d["o_f"] + c["o_b"]
    ms = jnp.zeros_like(o)
    for hh in range(HEADS):
        m_h = _lane_group_mask(256, 64, hh)
        ms = ms + m_h * (jnp.sum(o * o * m_h, axis=-1, keepdims=True) * (1.0 / 64))
    yg = o * lax.rsqrt(ms + EPS) * d["gnw"] * _silu(d["gr"])
    y = jnp.concatenate([d["y_sgu"], yg, d["y_mla"]], axis=-1)
    t = nn_const(c["w_out"], z is not None)(y)
    if z is not None:
        t = t + z["zt"]
    return dict(x1=d["x"] + mod[2:3] * t), dict(a_out=y)


def ffn_tile(d, c, z):
    mod = d["mod"]
    h2 = _rms(d["x1"], d["n2w"]) * (1.0 + mod[4:5]) + mod[3:4]
    f = None
    for j in range(D_FF // FF_CH):
        a = jnp.maximum(nn_const(c["w_ff1"][j], False)(h2), 0.0)
        fj = nn_const(c["w_ff2"][j], False)(a * a)
        f = fj if f is None else f + fj
    return dict(x2=d["x1"] + mod[5:6] * f, f=f), {}


def out_ffn_tile(d, c, z):
    x1 = attn_out_tile(d, c, None)[0]["x1"]
    o2 = ffn_tile(dict(x1=x1, mod=d["mod"], n2w=d["n2w"]), c, None)[0]
    return dict(x1=x1, x2=o2["x2"], f=o2["f"]), {}


def _in_spec(kind, arr, tile):
    if kind == "tile":
        return pl.BlockSpec((tile, arr.shape[1]), lambda i: (i, 0))
    if kind == "kind":
        return pl.BlockSpec((1,) + arr.shape[1:], lambda i: (jnp.where(i < TILE // tile, 0, 1), 0, 0))
    nd = arr.ndim
    if kind in ("wfull", "wcols", "wrows"):
        return pl.BlockSpec(arr.shape, lambda i: (0,) * nd, pipeline_mode=pl.Buffered(1))
    return pl.BlockSpec(arr.shape, lambda i: (0,) * nd)


def _load(kind, ref):
    if kind == "kind":
        return ref[0]
    if kind == "wcols":
        return [ref[:, j * FF_CH:(j + 1) * FF_CH] for j in range(ref.shape[1] // FF_CH)]
    if kind == "wrows":
        return [ref[j * FF_CH:(j + 1) * FF_CH, :] for j in range(ref.shape[0] // FF_CH)]
    return ref[...]


def tile_forward(fn, name, t_all, ins, out_defs, tile=TILE):
    keys = [k for k, _, _, _ in ins]
    kinds = [kd for _, kd, _, _ in ins]
    diffs = [df for _, _, df, _ in ins]
    arrs = [a for _, _, _, a in ins]
    n_in = len(ins)

    def body(*refs):
        vals = [_load(kinds[j], refs[j]) for j in range(n_in)]
        d = {keys[j]: vals[j] for j in range(n_in) if diffs[j]}
        c = {keys[j]: vals[j] for j in range(n_in) if not diffs[j]}
        outs, _ = fn(d, c, None)
        for j, (k, _, dt) in enumerate(out_defs):
            refs[n_in + j][...] = outs[k].astype(dt)

    res = pl.pallas_call(
        body, name=name, grid=(t_all // tile,),
        in_specs=[_in_spec(kinds[j], arrs[j], tile) for j in range(n_in)],
        out_specs=[pl.BlockSpec((tile, w), lambda i: (i, 0)) for _, w, _ in out_defs],
        out_shape=[jax.ShapeDtypeStruct((t_all, w), dt) for _, w, dt in out_defs],
        compiler_params=_cparams(1),
    )(*arrs)
    return {k: r for (k, _, _), r in zip(out_defs, res)}


def _mla_delta_extra(d, dd):
    do = dd["y_mla"]
    prod = do * d["y_mla"]
    rows = [jnp.sum(prod[:, h * 128:(h + 1) * 128], axis=-1, keepdims=True) for h in range(HEADS)]
    cols = jnp.concatenate(rows + [jnp.zeros((prod.shape[0], 128 - HEADS), F32)], axis=-1)
    return [cols.T[0:8, :], do.astype(BF)]


def tile_backward(fn, name, t_all, ins, cots, z_defs, aux_defs, tile=TILE, resid=None, mla_delta=False):
    keys = [k for k, _, _, _ in ins]
    kinds = [kd for _, kd, _, _ in ins]
    diffs = [df for _, _, df, _ in ins]
    arrs = [a for _, _, _, a in ins]
    cot_keys, cot_arrs = [], []
    for k, a in cots:
        for one in (a if isinstance(a, (list, tuple)) else [a]):
            cot_keys.append(k)
            cot_arrs.append(one)
    if resid is not None:
        cot_keys.append("resid:" + resid[0])
        cot_arrs.append(resid[1])
    n_in, n_cot = len(ins), len(cot_arrs)
    dkeys = [j for j in range(n_in) if diffs[j]]
    ctx_tiles = TILE // tile

    def body(*refs):
        i = pl.program_id(0)
        vals = [_load(kinds[j], refs[j]) for j in range(n_in)]
        d = {keys[j]: vals[j] for j in dkeys}
        c = {keys[j]: vals[j] for j in range(n_in) if not diffs[j]}
        zs = {k: jnp.zeros((tile, w), F32) for k, w in z_defs}
        outs, vjp_fn, aux = jax.vjp(lambda dd, zz: fn(dd, c, zz), d, zs, has_aux=True)
        ct = {}
        for j, k in enumerate(cot_keys):
            ct[k] = refs[n_in + j][...] + ct[k] if k in ct else refs[n_in + j][...]
        dd, dz = vjp_fn({k: ct[k].astype(outs[k].dtype) for k in outs})
        base = n_in + n_cot
        for n, j in enumerate(dkeys):
            ref, g = refs[base + n], dd[keys[j]]
            if kinds[j] == "tile":
                ref[...] = g + ct["resid:" + keys[j]] if "resid:" + keys[j] in ct else g
            else:
                first = ((i == 0) | (i == ctx_tiles)) if kinds[j] == "kind" else (i == 0)
                gv = g[None] if kinds[j] == "kind" else g

                @pl.when(first)
                def _(ref=ref, gv=gv):
                    ref[...] = gv

                @pl.when(jnp.logical_not(first))
                def _(ref=ref, gv=gv):
                    ref[...] += gv
        base += len(dkeys)
        for n, (k, _) in enumerate(z_defs):
            refs[base + n][...] = dz[k].astype(BF)
        base += len(z_defs)
        for n, (k, _) in enumerate(aux_defs):
            refs[base + n][...] = aux[k].T.astype(BF)
        if mla_delta:
            base += len(aux_defs)
            for n, val in enumerate(_mla_delta_extra(d, dd)):
                refs[base + n][...] = val

    out_specs, out_shape = [], []
    for j in dkeys:
        out_specs.append(_in_spec(kinds[j], arrs[j], tile))
        out_shape.append(jax.ShapeDtypeStruct(arrs[j].shape, F32))
    for _, w in z_defs:
        out_specs.append(pl.BlockSpec((tile, w), lambda i: (i, 0)))
        out_shape.append(jax.ShapeDtypeStruct((t_all, w), BF))
    for _, w in aux_defs:
        out_specs.append(pl.BlockSpec((w, tile), lambda i: (0, i)))
        out_shape.append(jax.ShapeDtypeStruct((w, t_all), BF))
    if mla_delta:
        out_specs += [pl.BlockSpec((8, tile), lambda i: (0, i)), pl.BlockSpec((tile, 512), lambda i: (i, 0))]
        out_shape += [jax.ShapeDtypeStruct((8, t_all), F32), jax.ShapeDtypeStruct((t_all, 512), BF)]
    res = pl.pallas_call(
        body, name=name, grid=(t_all // tile,),
        in_specs=[_in_spec(kinds[j], arrs[j], tile) for j in range(n_in)]
        + [pl.BlockSpec((tile, a.shape[1]), lambda i: (i, 0)) for a in cot_arrs],
        out_specs=out_specs, out_shape=out_shape, compiler_params=_cparams(1),
    )(*arrs, *cot_arrs)
    grads = {keys[j]: res[n] for n, j in enumerate(dkeys)}
    extra = {k: res[len(dkeys) + n] for n, (k, _) in enumerate(list(z_defs) + list(aux_defs))}
    if mla_delta:
        extra["dl_rows"], extra["do_bf"] = res[-2], res[-1]
    return grads, extra


def ffn_backward(x1, modl, n2w, w1, w2, dx2, f, name):
    t_all = x1.shape[0]
    n_ch = D_FF // FF_CH

    def head(x, mod, nw):
        return _rms(x, nw) * (1.0 + mod[4:5]) + mod[3:4]

    def body(x_ref, mod_ref, nw_ref, w1_ref, w2_ref, dx2_ref, f_ref, dx1_ref, dmod_ref, dnw_ref, zpre_ref, zf_ref, a1_ref,
             a2_ref):
        i = pl.program_id(0)
        mod = mod_ref[0]
        dx2 = dx2_ref[...]
        h2, vjp_head = jax.vjp(head, x_ref[...], mod, nw_ref[...])
        h2b = h2.astype(BF)
        dfb = (dx2 * mod[5:6]).astype(BF)
        f = f_ref[...]
        dh2 = jnp.zeros((TILE, D), F32)
        for j in range(n_ch):
            cs = slice(j * FF_CH, (j + 1) * FF_CH)
            a = jnp.maximum(_dot(h2b, w1_ref[:, cs]), 0.0)
            a2b = (a * a).astype(BF)
            dpre = (_dot_nt(dfb, w2_ref[cs, :]) * (2.0 * a)).astype(BF)
            dh2 = dh2 + _dot_nt(dpre, w1_ref[:, cs])
            zpre_ref[:, cs] = dpre
            a2_ref[:, cs] = a2b
        zf_ref[...] = (dx2 * mod[5:6]).T.astype(BF)
        a1_ref[...] = h2.T.astype(BF)
        dx1, dmod, dnw = vjp_head(dh2)
        dx1_ref[...] = dx2 + dx1
        row = lax.broadcasted_iota(jnp.int32, (8, D), 0)
        dmod = dmod + jnp.where(row == 5, jnp.sum(dx2 * f, axis=0, keepdims=True), 0.0)
        first_kind = (i == 0) | (i == 1)

        @pl.when(first_kind)
        def _():
            dmod_ref[0] = dmod

        @pl.when(jnp.logical_not(first_kind))
        def _():
            dmod_ref[0] += dmod

        @pl.when(i == 0)
        def _():
            dnw_ref[...] = dnw

        @pl.when(i > 0)
        def _():
            dnw_ref[...] += dnw

    tspec = lambda w: pl.BlockSpec((TILE, w), lambda i: (i, 0))
    once = lambda shp: pl.BlockSpec(shp, lambda i: (0, 0), pipeline_mode=pl.Buffered(1))
    kind = pl.BlockSpec((1, 8, D), lambda i: (jnp.minimum(i, 1), 0, 0))
    tr = pl.BlockSpec((D, TILE), lambda i: (0, i))
    return pl.pallas_call(
        body, name=name, grid=(t_all // TILE,),
        in_specs=[tspec(D), kind, pl.BlockSpec((1, D), lambda i: (0, 0)), once((D, D_FF)), once((D_FF, D)), tspec(D),
                  tspec(D)],
        out_specs=[tspec(D), kind, pl.BlockSpec((1, D), lambda i: (0, 0)), tspec(D_FF), tr, tr, tspec(D_FF)],
        out_shape=[jax.ShapeDtypeStruct((t_all, D), F32), jax.ShapeDtypeStruct((2, 8, D), F32),
                   jax.ShapeDtypeStruct((1, D), F32), jax.ShapeDtypeStruct((t_all, D_FF), BF),
                   jax.ShapeDtypeStruct((D, t_all), BF), jax.ShapeDtypeStruct((D, t_all), BF),
                   jax.ShapeDtypeStruct((t_all, D_FF), BF)],
        compiler_params=_cparams(1),
    )(x1, modl, n2w, w1, w2, dx2, f)


WG_TOK = 768


def wgrad(at, b, name, bk2=1024):
    k1, t = at.shape
    k2 = b.shape[1]
    bk2 = min(bk2, k2)
    tt = WG_TOK if t % WG_TOK == 0 else TILE
    nt_ = t // tt

    def body(a_ref, b_ref, o_ref, acc):
        s = pl.program_id(1)

        @pl.when(s == 0)
        def _():
            acc[...] = jnp.zeros_like(acc)

        acc[...] += _dot(a_ref[...], b_ref[...])

        @pl.when(s == nt_ - 1)
        def _():
            o_ref[...] = acc[...]

    return pl.pallas_call(
        body, name=name, grid=(k2 // bk2, nt_),
        in_specs=[pl.BlockSpec((k1, tt), lambda j, s: (0, s)), pl.BlockSpec((tt, bk2), lambda j, s: (s, j))],
        out_specs=pl.BlockSpec((k1, bk2), lambda j, s: (0, j)),
        out_shape=jax.ShapeDtypeStruct((k1, k2), F32),
        scratch_shapes=[pltpu.VMEM((k1, bk2), F32)],
        compiler_params=_cparams(2),
    )(at, b)


def _gla_consts(reverse, diff):
    r = lax.broadcasted_iota(jnp.int32, (GCH, GCH), 0)
    cc = lax.broadcasted_iota(jnp.int32, (GCH, GCH), 1)
    low = (r >= cc)
    tri = (jnp.logical_not(low) | (r == cc)) if reverse else low
    tri_f = tri.astype(F32)
    tri_t = (cc >= r) if not reverse else (cc <= r)
    hmk = [_lane_group_mask(128, 32, h) for h in range(HEADS)]
    hmv = [_lane_group_mask(256, 64, h) for h in range(HEADS)]
    e = lax.broadcasted_iota(jnp.int32, (256, 128), 0) // 64
    dk = lax.broadcasted_iota(jnp.int32, (256, 128), 1) // 32
    return dict(cum=make_cum(tri_f.astype(BF), tri_t.astype(F32).astype(BF), diff), ops=_ops(diff), reverse=reverse,
                tri4=jnp.concatenate([tri_f] * HEADS, axis=0), hmk=hmk, hmv=hmv, bd=(e == dk).astype(F32))


def gla_chunk(st, q, k, v, g, cs):
    nn, nt, tn = cs["ops"]
    b = cs["cum"](g)
    bl = jnp.sum(g, axis=0, keepdims=True)
    b_ref = jnp.sum(g[GCH // 2:] if cs["reverse"] else g[:GCH // 2], axis=0, keepdims=True)
    qe = q * jnp.exp(b)
    qs = q * jnp.exp(b - b_ref)
    ks = k * jnp.exp(b_ref - b)
    qstack = jnp.concatenate([qs * cs["hmk"][h] for h in range(HEADS)], axis=0)
    att = nt(qstack, ks) * cs["tri4"]
    ofull = nn(att, v)
    o = nt(qe, st)
    for h in range(HEADS):
        o = o + ofull[h * GCH:(h + 1) * GCH] * cs["hmv"][h]
    kd = k * jnp.exp(bl - b)
    st_new = st * jnp.exp(bl) + tn(v, kd) * cs["bd"]
    return st_new, o


def _gla_chunk_index(s, n_ch, reverse):
    ctx_ch = TILE // GCH
    if not reverse:
        return s
    return jnp.where(s < ctx_ch, ctx_ch - 1 - s, n_ch - 1 + ctx_ch - s)


def gla_forward(q, k, v, gf, gb, name):
    t = q.shape[0]
    n_ch = t // GCH

    def body(*refs):
        s = pl.program_id(0)
        for dr, reverse in enumerate((False, True)):
            q_ref, k_ref, v_ref, g_ref = refs[4 * dr:4 * dr + 4]
            o_ref, sst_ref = refs[8 + 2 * dr:8 + 2 * dr + 2]
            st = refs[12 + dr]

            @pl.when(s == 0)
            def _(st=st):
                st[...] = jnp.zeros_like(st)

            cur = st[...]
            sst_ref[0] = cur
            st_new, o = gla_chunk(cur, q_ref[...], k_ref[...], v_ref[...], g_ref[...], _gla_consts(reverse, False))
            o_ref[...] = o
            st[...] = st_new

    in_specs, out_specs, out_shape = [], [], []
    for reverse in (False, True):
        im = lambda s, reverse=reverse: (_gla_chunk_index(s, n_ch, reverse), 0)
        im3 = lambda s, reverse=reverse: (_gla_chunk_index(s, n_ch, reverse), 0, 0)
        in_specs += [pl.BlockSpec((GCH, 128), im), pl.BlockSpec((GCH, 128), im), pl.BlockSpec((GCH, 256), im),
                     pl.BlockSpec((GCH, 128), im)]
        out_specs += [pl.BlockSpec((GCH, 256), im), pl.BlockSpec((1, 256, 128), im3)]
        out_shape += [jax.ShapeDtypeStruct((t, 256), F32), jax.ShapeDtypeStruct((n_ch, 256, 128), F32)]
    return pl.pallas_call(
        body, name=name, grid=(n_ch,), in_specs=in_specs, out_specs=out_specs, out_shape=out_shape,
        scratch_shapes=[pltpu.VMEM((256, 128), F32), pltpu.VMEM((256, 128), F32)],
        compiler_params=_cparams(1),
    )(q, k, v, gf, q, k, v, gb)


def gla_backward(q, k, v, gf, gb, sst_f, sst_b, do, name):
    t = q.shape[0]
    n_ch = t // GCH

    def body(*refs):
        r = pl.program_id(0)
        for dr, reverse in enumerate((False, True)):
            q_ref, k_ref, v_ref, g_ref, sst_ref, do_ref = refs[6 * dr:6 * dr + 6]
            outs = refs[12 + 4 * dr:12 + 4 * dr + 4]
            dst = refs[20 + dr]

            @pl.when(r == 0)
            def _(dst=dst):
                dst[...] = jnp.zeros_like(dst)

            cs = _gla_consts(reverse, True)
            _, vjp_fn = jax.vjp(lambda a, b, c_, d_, e_, cs=cs: gla_chunk(a, b, c_, d_, e_, cs),
                                sst_ref[0], q_ref[...], k_ref[...], v_ref[...], g_ref[...])
            grads = vjp_fn((dst[...], do_ref[...]))
            for o_ref, gval in zip(outs, grads[1:]):
                o_ref[...] = gval
            dst[...] = grads[0]

    in_specs, out_specs, out_shape = [], [], []
    for reverse in (False, True):
        im = lambda r, reverse=reverse: (_gla_chunk_index(n_ch - 1 - r, n_ch, reverse), 0)
        im3 = lambda r, reverse=reverse: (_gla_chunk_index(n_ch - 1 - r, n_ch, reverse), 0, 0)
        in_specs += [pl.BlockSpec((GCH, 128), im), pl.BlockSpec((GCH, 128), im), pl.BlockSpec((GCH, 256), im),
                     pl.BlockSpec((GCH, 128), im), pl.BlockSpec((1, 256, 128), im3), pl.BlockSpec((GCH, 256), im)]
        out_specs += [pl.BlockSpec((GCH, 128), im), pl.BlockSpec((GCH, 128), im), pl.BlockSpec((GCH, 256), im),
                      pl.BlockSpec((GCH, 128), im)]
        out_shape += [jax.ShapeDtypeStruct((t, 128), F32), jax.ShapeDtypeStruct((t, 128), F32),
                      jax.ShapeDtypeStruct((t, 256), F32), jax.ShapeDtypeStruct((t, 128), F32)]
    return pl.pallas_call(
        body, name=name, grid=(n_ch,), in_specs=in_specs, out_specs=out_specs, out_shape=out_shape,
        scratch_shapes=[pltpu.VMEM((256, 128), F32), pltpu.VMEM((256, 128), F32)],
        compiler_params=_cparams(1),
    )(q, k, v, gf, sst_f, do, q, k, v, gb, sst_b, do)


def _resident(hbm_ref, vmem_ref, sem):
    cp = pltpu.make_async_copy(hbm_ref, vmem_ref, sem)
    cp.start()
    cp.wait()


def mla_forward(q_cat, k_cat, v, name):
    t = q_cat.shape[0]
    qt = QT_FWD if t % QT_FWD == 0 else TILE
    n_t = t // qt

    ch = KV_CH_FWD if (t - TILE) % KV_CH_FWD == 0 else KV_CH
    n_main = (t - TILE) // ch

    def body(q_ref, k_hbm, v_hbm, o_ref, lse_ref, k_s, v_s, m_s, l_s, acc_s, sem):
        i = pl.program_id(0)

        @pl.when(i == 0)
        def _():
            _resident(k_hbm, k_s, sem.at[0])
            _resident(v_hbm, v_s, sem.at[1])

        m_s[...] = jnp.full(m_s.shape, -1e30, F32)
        l_s[...] = jnp.zeros_like(l_s)
        acc_s[...] = jnp.zeros_like(acc_s)

        def chunk(r0, size, hide_ctx_rows=False):
            for h in range(HEADS):
                kh = k_s[pl.ds(r0, size), h * 256:(h + 1) * 256]
                vh = v_s[pl.ds(r0, size), h * 128:(h + 1) * 128]
                s = _dot_nt(q_ref[:, h * 256:(h + 1) * 256], kh)
                if hide_ctx_rows:
                    s = jnp.where(lax.broadcasted_iota(jnp.int32, (qt, 1), 0) < TILE, -1e30, s)
                m_prev = m_s[h]
                m_col = jnp.maximum(jnp.max(m_prev, axis=-1, keepdims=True), jnp.max(s, axis=-1, keepdims=True))
                m_next = jnp.broadcast_to(m_col, m_prev.shape)
                p = jnp.exp2(s - m_col)
                alpha = jnp.exp2(m_prev - m_next)
                l_s[h] = alpha * l_s[h] + jnp.sum(p, axis=-1, keepdims=True)
                acc_s[h] = alpha * acc_s[h] + _dot(p.astype(BF), vh)
                m_s[h] = m_next

        chunk(0, TILE)

        def main_loop(hide, size, unroll):
            def step(c, carry):
                chunk(pl.multiple_of(TILE + c * size, TILE), size, hide)
                return carry

            lax.fori_loop(0, (t - TILE) // size, step, 0, unroll=unroll)

        if qt > TILE:
            pl.when(i == 0)(lambda: main_loop(True, KV_CH, 1))
        pl.when(i >= 1)(lambda: main_loop(False, ch, MLA_UNROLL))

        lane = lax.broadcasted_iota(jnp.int32, (qt, 128), 1)
        cols = jnp.zeros((qt, 128), F32)
        for h in range(HEADS):
            o_ref[:, h * 128:(h + 1) * 128] = acc_s[h] / l_s[h]
            cols = jnp.where(lane == h, m_s[h] + jnp.log2(l_s[h]), cols)
        lse_ref[...] = cols.T[0:8, :]

    return pl.pallas_call(
        body, name=name, grid=(n_t,),
        in_specs=[pl.BlockSpec((qt, 1024), lambda i: (i, 0)), pl.BlockSpec(memory_space=pl.ANY),
                  pl.BlockSpec(memory_space=pl.ANY)],
        out_specs=[pl.BlockSpec((qt, 512), lambda i: (i, 0)), pl.BlockSpec((8, qt), lambda i: (0, i))],
        out_shape=[jax.ShapeDtypeStruct((t, 512), F32), jax.ShapeDtypeStruct((8, t), F32)],
        scratch_shapes=[pltpu.VMEM((t, 1024), BF), pltpu.VMEM((t, 512), BF), pltpu.VMEM((HEADS, qt, 128), F32),
                        pltpu.VMEM((HEADS, qt, 128), F32), pltpu.VMEM((HEADS, qt, 128), F32),
                        pltpu.SemaphoreType.DMA((2,))],
        compiler_params=_cparams(1),
    )(q_cat, k_cat, v)


def mla_backward(q_cat, k_cat, v, lse_rows, dl_rows, do_bf, name):
    t = q_cat.shape[0]
    kt = KT_BWD if t % KT_BWD == 0 else TILE
    n_t = t // kt
    ch = Q_CH_BWD if (t - TILE) % Q_CH_BWD == 0 else KV_CH
    n_main = (t - TILE) // ch

    def body(q_hbm, do_hbm, k_ref, v_ref, lse_ref, dl_ref, dq_ref, dk_ref, dv_ref, q_s, do_s, dk_s, dv_s, sem):
        h, j = pl.program_id(0), pl.program_id(1)

        @pl.when(j == 0)
        def _():
            _resident(q_hbm.at[:, pl.ds(pl.multiple_of(h * 256, 256), 256)], q_s, sem.at[0])
            _resident(do_hbm.at[:, pl.ds(pl.multiple_of(h * 128, 128), 128)], do_s, sem.at[1])
            dq_ref[...] = jnp.zeros_like(dq_ref)

        dk_s[...] = jnp.zeros_like(dk_s)
        dv_s[...] = jnp.zeros_like(dv_s)
        kh = k_ref[...]
        vh = v_ref[...]

        def chunk(r0, size, ctx_only=False):
            qh = q_s[pl.ds(r0, size), :]
            doh = do_s[pl.ds(r0, size), :]
            pt = jnp.exp2(_dot_nt(kh, qh) - lse_ref[pl.ds(h, 1), pl.ds(r0, size)])
            if ctx_only and kt > TILE:
                pt = jnp.where(lax.broadcasted_iota(jnp.int32, (kt, 1), 0) < TILE, pt, 0.0)
            dst = (pt * (_dot_nt(vh, doh) - dl_ref[pl.ds(h, 1), pl.ds(r0, size)])).astype(BF)
            dv_s[...] += _dot(pt.astype(BF), doh)
            dk_s[...] += _dot(dst, qh)
            dq_ref[pl.ds(r0, size), :] += _dot_tn(dst, kh)

        @pl.when(j == 0)
        def _():
            chunk(0, TILE, ctx_only=True)

        def step(c, carry):
            chunk(pl.multiple_of(TILE + c * ch, TILE), ch)
            return carry

        lax.fori_loop(0, n_main, step, 0, unroll=MLA_UNROLL)
        dk_ref[...] = dk_s[...] * LN2
        dv_ref[...] = dv_s[...]

        @pl.when(j == n_t - 1)
        def _():
            dq_ref[...] = dq_ref[...] * LN2

    rows = pl.BlockSpec((8, t), lambda h, j: (0, 0))
    hbm = pl.BlockSpec(memory_space=pl.ANY)
    return pl.pallas_call(
        body, name=name, grid=(HEADS, n_t),
        in_specs=[hbm, hbm, pl.BlockSpec((kt, 256), lambda h, j: (j, h)), pl.BlockSpec((kt, 128), lambda h, j: (j, h)),
                  rows, rows],
        out_specs=[pl.BlockSpec((t, 256), lambda h, j: (0, h)), pl.BlockSpec((kt, 256), lambda h, j: (j, h)),
                   pl.BlockSpec((kt, 128), lambda h, j: (j, h))],
        out_shape=[jax.ShapeDtypeStruct((t, 1024), F32), jax.ShapeDtypeStruct((t, 1024), F32),
                   jax.ShapeDtypeStruct((t, 512), F32)],
        scratch_shapes=[pltpu.VMEM((t, 256), BF), pltpu.VMEM((t, 128), BF), pltpu.VMEM((kt, 256), F32),
                        pltpu.VMEM((kt, 128), F32), pltpu.SemaphoreType.DMA((2,))],
        compiler_params=_cparams(2),
    )(q_cat, do_bf, k_cat, v, lse_rows, dl_rows)


def final_loss(xf, target, fnw, name):
    t = xf.shape[0]
    n_t = t // TILE

    def body(x_ref, t_ref, w_ref, loss_ref, dx_ref, dw_ref):
        i = pl.program_id(0)

        @pl.when(i == 0)
        def _():
            loss_ref[...] = jnp.zeros_like(loss_ref)
            dw_ref[...] = jnp.zeros_like(dw_ref)
            dx_ref[...] = jnp.zeros_like(dx_ref)

        @pl.when(i >= 1)
        def _():
            y, vjp_fn = jax.vjp(_rms, x_ref[...], w_ref[...])
            err = y - t_ref[...]
            loss_ref[...] += jnp.broadcast_to(0.5 * jnp.sum(jnp.mean(err * err, axis=-1, keepdims=True)), (8, 128))
            dx, dw = vjp_fn(err * (1.0 / D))
            dx_ref[...] = dx
            dw_ref[...] += dw

    return pl.pallas_call(
        body, name=name, grid=(n_t,),
        in_specs=[pl.BlockSpec((TILE, D), lambda i: (i, 0)), pl.BlockSpec((TILE, D), lambda i: (jnp.maximum(i - 1, 0), 0)),
                  pl.BlockSpec((1, D), lambda i: (0, 0))],
        out_specs=[pl.BlockSpec((8, 128), lambda i: (0, 0)), pl.BlockSpec((TILE, D), lambda i: (i, 0)),
                   pl.BlockSpec((1, D), lambda i: (0, 0))],
        out_shape=[jax.ShapeDtypeStruct((8, 128), F32), jax.ShapeDtypeStruct((t, D), F32),
                   jax.ShapeDtypeStruct((1, D), F32)],
        compiler_params=_cparams(1),
    )(xf, target, fnw)


def all_gather(xs, name):
    n = len(xs)
    blks = [tuple(x.shape) for x in xs]
    per = N_DEV - 1

    def body(*refs):
        x_refs, o_refs = refs[:n], refs[n:2 * n]
        ssem, rsem, lsem = refs[2 * n:]
        xi, yi, ci = lax.axis_index("x"), lax.axis_index("y"), lax.axis_index("c")
        me3 = (xi, yi, ci)
        me = 4 * xi + 2 * yi + ci
        flat = lambda d: 4 * d[0] + 2 * d[1] + d[2]
        sibling = (xi, yi, 1 - ci)
        chips = [(1 - xi, yi), (xi, 1 - yi), (1 - xi, 1 - yi)]

        def copy(a, k, block, to, src=None):
            rows = o_refs[a].at[flat(block)]
            return pltpu.make_async_remote_copy(
                src_ref=rows if src is None else src, dst_ref=rows,
                send_sem=ssem.at[a * per + k], recv_sem=rsem.at[a * per + k],
                device_id=to, device_id_type=pl.DeviceIdType.MESH)

        own = [pltpu.make_async_copy(x_refs[a], o_refs[a].at[me], lsem.at[a]) for a in range(n)]
        first = []
        for a in range(n):
            first.append(copy(a, 0, me3, sibling, src=x_refs[a]))
            first += [copy(a, 1 + j, me3, (*chip, ci), src=x_refs[a]) for j, chip in enumerate(chips)]
        for cp in own + first:
            cp.start()
        passed = []
        for j, chip in enumerate(chips):
            for a in range(n):
                copy(a, 1 + j, (*chip, ci), me3).wait_recv()
                fw = copy(a, 4 + j, (*chip, ci), sibling)
                fw.start()
                passed.append(fw)
        for a in range(n):
            copy(a, 0, sibling, me3).wait_recv()
            for j, chip in enumerate(chips):
                copy(a, 4 + j, (*chip, 1 - ci), me3).wait_recv()
        for cp in first + passed:
            cp.wait_send()
        for cp in own:
            cp.wait()

    hbm = pl.BlockSpec(memory_space=pl.ANY)
    res = pl.pallas_call(
        body, name=name, in_specs=[hbm] * n, out_specs=[hbm] * n,
        out_shape=[jax.ShapeDtypeStruct((N_DEV,) + blks[a], xs[a].dtype) for a in range(n)],
        scratch_shapes=[pltpu.SemaphoreType.DMA((n * per,)), pltpu.SemaphoreType.DMA((n * per,)),
                        pltpu.SemaphoreType.DMA((n,))],
        compiler_params=pltpu.CompilerParams(has_side_effects=True),
    )(*xs)
    return list(res)


def reduce_scatter(xs, name):
    n = len(xs)
    c_idx = lax.axis_index("c").astype(jnp.int32).reshape(1)
    hbm = pl.BlockSpec(memory_space=pl.ANY)
    side = pltpu.CompilerParams(has_side_effects=True)

    def pair_body(*refs):
        x_refs, o_refs = refs[:n], refs[n:2 * n]
        ssem, rsem = refs[2 * n:]
        xi, yi, cc = lax.axis_index("x"), lax.axis_index("y"), lax.axis_index("c")
        cps = [pltpu.make_async_remote_copy(
            src_ref=x_refs[a].at[1 - cc], dst_ref=o_refs[a], send_sem=ssem.at[a], recv_sem=rsem.at[a],
            device_id=(xi, yi, 1 - cc), device_id_type=pl.DeviceIdType.MESH) for a in range(n)]
        for cp in cps:
            cp.start()
        for cp in cps:
            cp.wait()

    halves = [tuple(x.shape[1:]) for x in xs]
    got = pl.pallas_call(
        pair_body, name=name + "_pair", in_specs=[hbm] * n, out_specs=[hbm] * n,
        out_shape=[jax.ShapeDtypeStruct(halves[a], xs[a].dtype) for a in range(n)],
        scratch_shapes=[pltpu.SemaphoreType.DMA((n,)), pltpu.SemaphoreType.DMA((n,))], compiler_params=side,
    )(*xs)

    sums = []
    for a in range(n):
        def add_body(c_ref, a_ref, b_ref, o_ref):
            o_ref[...] = (a_ref[0].astype(F32) + b_ref[...].astype(F32)).astype(o_ref.dtype)

        blk = pl.BlockSpec((1,) + halves[a][1:], lambda q, c_ref: (q, 0, 0))
        own = pl.BlockSpec((1, 1) + halves[a][1:], lambda q, c_ref: (c_ref[0], q, 0, 0))
        sums.append(pl.pallas_call(
            add_body, name=f"{name}_sum{a}",
            grid_spec=pltpu.PrefetchScalarGridSpec(num_scalar_prefetch=1, grid=(4,), in_specs=[own, blk], out_specs=blk),
            out_shape=jax.ShapeDtypeStruct(halves[a], xs[a].dtype), compiler_params=_cparams(1),
        )(c_idx, xs[a], got[a]))

    def chips_body(*refs):
        s_refs, o_refs = refs[:n], refs[n:2 * n]
        ssem, rsem, lsem = refs[2 * n:]
        xi, yi, cc = lax.axis_index("x"), lax.axis_index("y"), lax.axis_index("c")
        my_chip = 2 * xi + yi
        chips = [(1 - xi, yi), (xi, 1 - yi), (1 - xi, 1 - yi)]
        own = [pltpu.make_async_copy(s_refs[a].at[my_chip], o_refs[a].at[my_chip], lsem.at[a]) for a in range(n)]
        sends, lands = [], []
        for a in range(n):
            for j, (px, py) in enumerate(chips):
                common = dict(send_sem=ssem.at[3 * a + j], recv_sem=rsem.at[3 * a + j], device_id=(px, py, cc),
                              device_id_type=pl.DeviceIdType.MESH)
                sends.append(pltpu.make_async_remote_copy(src_ref=s_refs[a].at[2 * px + py],
                                                          dst_ref=o_refs[a].at[my_chip], **common))
                lands.append(pltpu.make_async_remote_copy(src_ref=s_refs[a].at[2 * px + py],
                                                          dst_ref=o_refs[a].at[2 * px + py], **common))
        for cp in own + sends:
            cp.start()
        for cp in lands:
            cp.wait_recv()
        for cp in sends:
            cp.wait_send()
        for cp in own:
            cp.wait()

    res = pl.pallas_call(
        chips_body, name=name + "_chips", in_specs=[hbm] * n, out_specs=[hbm] * n,
        out_shape=[jax.ShapeDtypeStruct(halves[a], xs[a].dtype) for a in range(n)],
        scratch_shapes=[pltpu.SemaphoreType.DMA((3 * n,)), pltpu.SemaphoreType.DMA((3 * n,)),
                        pltpu.SemaphoreType.DMA((n,))], compiler_params=side,
    )(*sums)
    return list(res)


def mod_forward(crows, w_mod, b_shard, name):
    cols = w_mod.shape[2]

    def body(c_ref, w_ref, b_ref, o_ref):
        o_ref[0] = _dot(_silu(c_ref[...]).astype(BF), w_ref[0].astype(BF)) + b_ref[0]

    return pl.pallas_call(
        body, name=name, grid=(2,),
        in_specs=[pl.BlockSpec((16, D), lambda l: (0, 0)), pl.BlockSpec((1, D, cols), lambda l: (l, 0, 0)),
                  pl.BlockSpec((1, 1, cols), lambda l: (l, 0, 0))],
        out_specs=pl.BlockSpec((1, 16, cols), lambda l: (l, 0, 0)),
        out_shape=jax.ShapeDtypeStruct((2, 16, cols), F32), compiler_params=_cparams(1),
    )(crows, w_mod, b_shard)


def mod_backward(crows, w_mod, d_own, d_ctx, name):
    cols = w_mod.shape[2]

    def body(c_ref, w_ref, do_ref, dc_ref, gw_ref, gs_ref):
        dc = dc_ref[0]
        dsum = dc[0:1]
        for s in range(1, N_DEV):
            dsum = dsum + dc[s:s + 1]
        row = lax.broadcasted_iota(jnp.int32, (8, cols), 0)
        d16 = jnp.concatenate([do_ref[0], jnp.where(row == 0, jnp.broadcast_to(dsum, (8, cols)), 0.0)], axis=0)
        gw_ref[0] = _dot_tn(_silu(c_ref[...]).astype(BF), d16.astype(BF))
        gs_ref[0] = _dot_nt(jnp.broadcast_to(dsum, (8, cols)).astype(BF), w_ref[0].astype(BF))

    return pl.pallas_call(
        body, name=name, grid=(2,),
        in_specs=[pl.BlockSpec((16, D), lambda l: (0, 0)), pl.BlockSpec((1, D, cols), lambda l: (l, 0, 0)),
                  pl.BlockSpec((1, 8, cols), lambda l: (l, 0, 0)), pl.BlockSpec((1, 8, cols), lambda l: (l, 0, 0))],
        out_specs=[pl.BlockSpec((1, D, cols), lambda l: (l, 0, 0)), pl.BlockSpec((1, 8, D), lambda l: (l, 0, 0))],
        out_shape=[jax.ShapeDtypeStruct((2, D, cols), F32), jax.ShapeDtypeStruct((2, 8, D), F32)],
        compiler_params=_cparams(1),
    )(crows, w_mod, d_own, d_ctx)


def silu_grad_scale(c_ctx, ds, name):
    def body(c_ref, ds_ref, o_ref):
        cc = c_ref[...]
        sg = jax.nn.sigmoid(cc)
        o_ref[...] = (ds_ref[0][0:1] + ds_ref[1][0:1]) * (sg * (1.0 + cc * (1.0 - sg)))

    return pl.pallas_call(body, name=name, out_shape=jax.ShapeDtypeStruct((1, D), F32))(c_ctx, ds)


def _adamw_math(p_ref, w_ref, m_ref, v_ref, g_ref, d_ref, nm_ref, nv_ref):
    g = p_ref[0].astype(F32)
    for s in range(1, p_ref.shape[0]):
        g = g + p_ref[s].astype(F32)
    mm = ADAM_B1 * m_ref[...] + (1.0 - ADAM_B1) * g
    vv = ADAM_B2 * v_ref[...] + (1.0 - ADAM_B2) * (g * g)
    m_hat = mm / (1.0 - ADAM_B1 ** ADAM_STEP)
    v_hat = vv / (1.0 - ADAM_B2 ** ADAM_STEP)
    g_ref[...] = g
    d_ref[...] = -ADAM_LR * (m_hat / (jnp.sqrt(v_hat) + ADAM_EPS) + ADAM_WD * w_ref[...])
    nm_ref[...] = mm
    nv_ref[...] = vv


def adamw(parts, w, m, v, name):
    n_parts, rows, cols = parts.shape
    lanes = -(-cols // 128) * 128
    block_rows = min(rows, 1 << ((ADAMW_BLOCK_ELEMS // lanes).bit_length() - 1))
    assert rows % block_rows == 0

    def body(*refs):
        _adamw_math(*refs)

    spec = pl.BlockSpec((block_rows, cols), lambda i: (i, 0))
    return pl.pallas_call(
        body, name=name, grid=(rows // block_rows,),
        in_specs=[pl.BlockSpec((n_parts, block_rows, cols), lambda i: (0, i, 0)), spec, spec, spec],
        out_specs=[spec] * 4, out_shape=[jax.ShapeDtypeStruct((rows, cols), F32)] * 4,
        compiler_params=_cparams(1),
    )(parts, w, m, v)


def adamw_group(items, name):
    n = len(items)

    def body(*refs):
        for a in range(n):
            _adamw_math(*refs[4 * a:4 * a + 4], *refs[4 * n + 4 * a:4 * n + 4 * a + 4])

    flat_in = [x for it in items for x in it]
    out_shape = [jax.ShapeDtypeStruct(it[1].shape, F32) for it in items for _ in range(4)]
    res = pl.pallas_call(body, name=name, out_shape=out_shape,
                         compiler_params=pltpu.CompilerParams(vmem_limit_bytes=VMEM_LIMIT_MB * 1024 * 1024))(*flat_in)
    return [tuple(res[4 * a:4 * a + 4]) for a in range(n)]


def _pad_cols(w, segs, total):
    parts, pos = [], 0
    for dst, src, wd in segs:
        if dst > pos:
            parts.append(jnp.zeros(w.shape[:-1] + (dst - pos,), w.dtype))
        parts.append(w[..., src:src + wd])
        pos = dst + wd
    if pos < total:
        parts.append(jnp.zeros(w.shape[:-1] + (total - pos,), w.dtype))
    return jnp.concatenate(parts, axis=-1)


def _unpad_cols(g, segs):
    return jnp.concatenate([g[..., dst:dst + wd] for dst, _, wd in segs], axis=-1)


def _rope_tables(n_lat):
    rows = n_lat // GRID_W
    freq = ROPE_BASE ** (-jnp.arange(16, dtype=F32) * 2.0 / 32)
    a_row = jnp.arange(rows).astype(F32)[:, None] * freq[None, :]
    a_col = jnp.arange(GRID_W).astype(F32)[:, None] * freq[None, :]
    per_row = lambda tbl: jnp.repeat(tbl, GRID_W, axis=0)
    per_col = lambda tbl: jnp.tile(tbl, (rows, 1))
    cr, sr, cc, sc = per_row(jnp.cos(a_row)), per_row(jnp.sin(a_row)), per_col(jnp.cos(a_col)), per_col(jnp.sin(a_col))
    z = jnp.zeros((n_lat, 16), F32)
    cos = jnp.concatenate([cr, cr, cc, cc, jnp.ones((n_lat, 64), F32)], axis=1)
    sa = jnp.concatenate([-sr, z, -sc, z, jnp.zeros((n_lat, 64), F32)], axis=1)
    sb = jnp.concatenate([z, sr, z, sc, jnp.zeros((n_lat, 64), F32)], axis=1)
    ident = lambda fill: jnp.full((TILE, 128), fill, F32)
    return (jnp.concatenate([ident(1.0), cos]), jnp.concatenate([ident(0.0), sa]), jnp.concatenate([ident(0.0), sb]))


def _gathered_to_full(g, name):
    if name in ("w_out", "w_ff2"):
        return jnp.transpose(g, (1, 0, 2, 3)).reshape(2, -1, g.shape[-1])
    return jnp.transpose(g, (1, 2, 0, 3)).reshape(2, g.shape[2], -1)


def _layer_to_halves(g, name):
    if name in ("w_out", "w_ff2"):
        return jnp.transpose(g.reshape(4, 2, -1, g.shape[-1]), (1, 0, 2, 3))
    return jnp.transpose(g.reshape(g.shape[0], 4, 2, -1), (2, 1, 0, 3))


def kernel(x, c, ctx, c_ctx, w_mod, b_mod, norm1_w, w_in, w_out, sgu_norm_w, sgu_norm_b, sgu_w, sgu_b, gla_wg_fwd, gla_bg_fwd, gla_wg_bwd, gla_bg_bwd, gla_norm_w, mla_q_norm_w, mla_w_uq, mla_kv_norm_w, mla_w_ukv, norm2_w, w_ff1, w_ff2, final_norm_w, loss_target, m_c_ctx, m_w_mod, m_b_mod, m_norm1_w, m_w_in, m_w_out, m_sgu_norm_w, m_sgu_norm_b, m_sgu_w, m_sgu_b, m_gla_wg_fwd, m_gla_bg_fwd, m_gla_wg_bwd, m_gla_bg_bwd, m_gla_norm_w, m_mla_q_norm_w, m_mla_w_uq, m_mla_kv_norm_w, m_mla_w_ukv, m_norm2_w, m_w_ff1, m_w_ff2, m_final_norm_w, v_c_ctx, v_w_mod, v_b_mod, v_norm1_w, v_w_in, v_w_out, v_sgu_norm_w, v_sgu_norm_b, v_sgu_w, v_sgu_b, v_gla_wg_fwd, v_gla_bg_fwd, v_gla_wg_bwd, v_gla_bg_bwd, v_gla_norm_w, v_mla_q_norm_w, v_mla_w_uq, v_mla_kv_norm_w, v_mla_w_ukv, v_norm2_w, v_w_ff1, v_w_ff2, v_final_norm_w):
    W = dict(c_ctx=c_ctx, w_mod=w_mod, b_mod=b_mod, norm1_w=norm1_w, w_in=w_in, w_out=w_out, sgu_norm_w=sgu_norm_w,
             sgu_norm_b=sgu_norm_b, sgu_w=sgu_w, sgu_b=sgu_b, gla_wg_fwd=gla_wg_fwd, gla_bg_fwd=gla_bg_fwd,
             gla_wg_bwd=gla_wg_bwd, gla_bg_bwd=gla_bg_bwd, gla_norm_w=gla_norm_w, mla_q_norm_w=mla_q_norm_w,
             mla_w_uq=mla_w_uq, mla_kv_norm_w=mla_kv_norm_w, mla_w_ukv=mla_w_ukv, norm2_w=norm2_w, w_ff1=w_ff1,
             w_ff2=w_ff2, final_norm_w=final_norm_w)
    M = dict(c_ctx=m_c_ctx, w_mod=m_w_mod, b_mod=m_b_mod, norm1_w=m_norm1_w, w_in=m_w_in, w_out=m_w_out,
             sgu_norm_w=m_sgu_norm_w, sgu_norm_b=m_sgu_norm_b, sgu_w=m_sgu_w, sgu_b=m_sgu_b, gla_wg_fwd=m_gla_wg_fwd,
             gla_bg_fwd=m_gla_bg_fwd, gla_wg_bwd=m_gla_wg_bwd, gla_bg_bwd=m_gla_bg_bwd, gla_norm_w=m_gla_norm_w,
             mla_q_norm_w=m_mla_q_norm_w, mla_w_uq=m_mla_w_uq, mla_kv_norm_w=m_mla_kv_norm_w, mla_w_ukv=m_mla_w_ukv,
             norm2_w=m_norm2_w, w_ff1=m_w_ff1, w_ff2=m_w_ff2, final_norm_w=m_final_norm_w)
    V = dict(c_ctx=v_c_ctx, w_mod=v_w_mod, b_mod=v_b_mod, norm1_w=v_norm1_w, w_in=v_w_in, w_out=v_w_out,
             sgu_norm_w=v_sgu_norm_w, sgu_norm_b=v_sgu_norm_b, sgu_w=v_sgu_w, sgu_b=v_sgu_b, gla_wg_fwd=v_gla_wg_fwd,
             gla_bg_fwd=v_gla_bg_fwd, gla_wg_bwd=v_gla_wg_bwd, gla_bg_bwd=v_gla_bg_bwd, gla_norm_w=v_gla_norm_w,
             mla_q_norm_w=v_mla_q_norm_w, mla_w_uq=v_mla_w_uq, mla_kv_norm_w=v_mla_kv_norm_w, mla_w_ukv=v_mla_w_ukv,
             norm2_w=v_norm2_w, w_ff1=v_w_ff1, w_ff2=v_w_ff2, final_norm_w=v_final_norm_w)

    n_lat = x.shape[1]
    assert ctx.shape[1] == TILE and n_lat % TILE == 0 and x.shape[2] == D
    t_all = TILE + n_lat
    n_t = t_all // TILE
    me = 4 * lax.axis_index("x") + 2 * lax.axis_index("y") + lax.axis_index("c")
    mod_cols = w_mod.shape[2]

    c_all = all_gather([c], "ag_c")[0].reshape(N_DEV, D)
    crows = jnp.concatenate([c_all, c_ctx[None, :], jnp.zeros((7, D), F32)], axis=0)
    b_shard = lax.dynamic_slice_in_dim(b_mod, me * mod_cols, mod_cols, axis=1)[:, None, :]
    mod_sh = mod_forward(crows, w_mod, b_shard, "mod_fwd")
    mod_g = all_gather([mod_sh.reshape(32, mod_cols)], "ag_mod")[0]
    mod_full = jnp.transpose(mod_g.reshape(N_DEV, 2, 16, mod_cols), (1, 2, 0, 3)).reshape(2, 16, 6 * D)
    mod_own = lax.dynamic_index_in_dim(mod_full, me, axis=1, keepdims=False)
    mod_ctx = mod_full[:, 8, :]
    pad2 = jnp.zeros((2, D), F32)
    modl = [jnp.stack([jnp.concatenate([mod_ctx[l].reshape(6, D), pad2]),
                       jnp.concatenate([mod_own[l].reshape(6, D), pad2])]) for l in range(2)]

    v2 = lambda a: a[None, :] if a.ndim == 1 else a.reshape(-1, a.shape[-1])
    gathered = all_gather([v2(W[k].astype(BF)) for k in BIG_NAMES], "ag_weights")
    full = {k: _gathered_to_full(g.reshape((N_DEV,) + W[k].shape), k) for k, g in zip(BIG_NAMES, gathered)}
    w_in_p = _pad_cols(full["w_in"], W_IN_SEGS, P_COLS)
    w_uq_p = _pad_cols(full["mla_w_uq"], W_UQ_SEGS, 1024).astype(F32)
    w_ukv_f = full["mla_w_ukv"].astype(F32)
    wgf_p = jnp.pad(gla_wg_fwd, ((0, 0), (0, 112), (0, 0)))
    wgb_p = jnp.pad(gla_wg_bwd, ((0, 0), (0, 112), (0, 0)))
    sgu_bx = jnp.repeat(jnp.transpose(sgu_b, (0, 2, 1)), 64, axis=2)
    gnw_t = jnp.tile(gla_norm_w, (1, HEADS))
    rc, rsa, rsb = _rope_tables(n_lat)

    xin = jnp.concatenate([ctx[0], x[0]], axis=0)
    row = lambda a: a[None, :]

    def pre_ins(l, xl):
        return [("x", "tile", True, xl), ("mod", "kind", True, modl[l]), ("n1w", "full", True, row(norm1_w[l])),
                ("w_in", "wfull", False, w_in_p[l]), ("sgu_nw", "full", True, row(sgu_norm_w[l])),
                ("sgu_nb", "full", True, row(sgu_norm_b[l])), ("sgu_w", "full", True, sgu_w[l]),
                ("sgu_bx", "full", True, sgu_bx[l]), ("wgf", "full", True, wgf_p[l]), ("bgf", "full", True, row(gla_bg_fwd[l])),
                ("wgb", "full", True, wgb_p[l]), ("bgb", "full", True, row(gla_bg_bwd[l])),
                ("qnw", "full", True, row(mla_q_norm_w[l])), ("w_uq", "full", True, w_uq_p[l]),
                ("kvnw", "full", True, row(mla_kv_norm_w[l])), ("w_ukv", "full", True, w_ukv_f[l]),
                ("rc", "tile", False, rc), ("rsa", "tile", False, rsa), ("rsb", "tile", False, rsb)]

    pre_outs = [("y_sgu", 256, F32), ("qg", 128, F32), ("kg", 128, F32), ("vg", 256, F32), ("lgf", 128, F32),
                ("lgb", 128, F32), ("gr", 256, F32), ("q_cat", 1024, BF), ("k_cat", 1024, BF), ("v", 512, BF)]

    def out_ins(l, xl, a):
        return [("x", "tile", True, xl), ("mod", "kind", True, modl[l]), ("y_sgu", "tile", True, a["y_sgu"]),
                ("o_f", "tile", True, a["o_f"]), ("o_b", "tile", False, a["o_b"]), ("gr", "tile", True, a["gr"]),
                ("y_mla", "tile", True, a["y_mla"]), ("gnw", "full", True, row(gnw_t[l])),
                ("w_out", "wfull", False, full["w_out"][l])]

    def ffn_ins(l, x1):
        return [("x1", "tile", True, x1), ("mod", "kind", True, modl[l]), ("n2w", "full", True, row(norm2_w[l])),
                ("w_ff1", "wcols", False, full["w_ff1"][l]), ("w_ff2", "wrows", False, full["w_ff2"][l])]

    saved, xl = [], xin
    for l in range(2):
        a = tile_forward(pre_tile, f"pre_fwd{l}", t_all, pre_ins(l, xl), pre_outs)
        a["o_f"], a["sf"], a["o_b"], a["sb"] = gla_forward(a["qg"], a["kg"], a["vg"], a["lgf"], a["lgb"], f"gla_fwd{l}")
        a["y_mla"], a["lse"] = mla_forward(a["q_cat"], a["k_cat"], a["v"], f"mla_fwd{l}")
        a["x"] = xl
        ff = tile_forward(out_ffn_tile, f"out_ffn_fwd{l}", t_all, out_ins(l, xl, a) + ffn_ins(l, None)[2:],
                          [("x1", D, F32), ("x2", D, F32), ("f", D, F32)])
        a["x1"], xl, a["f"] = ff["x1"], ff["x2"], ff["f"]
        saved.append(a)

    loss_blk, dx, d_fnw = final_loss(xl, loss_target[0], row(final_norm_w), "final_loss")
    loss = lax.psum(loss_blk[0, 0], AXES)

    G = {}
    dmods = []
    for l in (1, 0):
        a = saved[l]
        dx1, dmod3, dn2w, zpre, zf_t, h2_t, a_ff2 = ffn_backward(
            a["x1"], modl[l], row(norm2_w[l]), full["w_ff1"][l], full["w_ff2"][l], dx, a["f"], f"ffn_bwd{l}")
        gw_ff1 = wgrad(h2_t, zpre, f"wg_ff1_{l}")
        gw_ff2 = jnp.transpose(wgrad(zf_t, a_ff2, f"wg_ff2_{l}"))
        g2, e2 = tile_backward(attn_out_tile, f"out_bwd{l}", t_all, out_ins(l, a["x"], a), [("x1", dx1)],
                               [("zt", D)], [("a_out", D)], mla_delta=True)
        gw_out = wgrad(e2["a_out"], e2["zt"], f"wg_out_{l}")
        dq_cat, dk_cat, dv = mla_backward(a["q_cat"], a["k_cat"], a["v"], a["lse"], e2["dl_rows"], e2["do_bf"],
                                          f"mla_bwd{l}")
        dqf, dkf, dvf, dgf, dqb, dkb, dvb, dgb = gla_backward(
            a["qg"], a["kg"], a["vg"], a["lgf"], a["lgb"], a["sf"], a["sb"], g2["o_f"], f"gla_bwd{l}")
        cots = [("y_sgu", g2["y_sgu"]), ("qg", [dqf, dqb]), ("kg", [dkf, dkb]), ("vg", [dvf, dvb]), ("lgf", dgf),
                ("lgb", dgb), ("gr", g2["gr"]), ("q_cat", dq_cat), ("k_cat", dk_cat), ("v", dv)]
        g1, e1 = tile_backward(pre_tile, f"pre_bwd{l}", t_all, pre_ins(l, a["x"]), cots, [("zp", P_COLS)], [("a_in", D)],
                               resid=("x", g2["x"]))
        gw_in = _unpad_cols(wgrad(e1["a_in"], e1["zp"], f"wg_in_{l}", bk2=P_COLS), W_IN_SEGS)
        dx = g1["x"]
        dmods.append(g1["mod"] + g2["mod"] + dmod3)
        G[l] = dict(w_in=gw_in, w_out=gw_out, w_ff1=gw_ff1, w_ff2=gw_ff2,
                    mla_w_uq=_unpad_cols(g1["w_uq"], W_UQ_SEGS), mla_w_ukv=g1["w_ukv"],
                    norm1_w=g1["n1w"][0], norm2_w=dn2w[0], sgu_norm_w=g1["sgu_nw"][0], sgu_norm_b=g1["sgu_nb"][0],
                    sgu_w=g1["sgu_w"], sgu_b=jnp.transpose(g1["sgu_bx"].reshape(128, HEADS, 64).sum(-1)),
                    gla_wg_fwd=g1["wgf"][:16], gla_bg_fwd=g1["bgf"][0], gla_wg_bwd=g1["wgb"][:16], gla_bg_bwd=g1["bgb"][0],
                    gla_norm_w=g2["gnw"][0].reshape(HEADS, 64).sum(0), mla_q_norm_w=g1["qnw"][0], mla_kv_norm_w=g1["kvnw"][0])
    dmods = dmods[::-1]
    grad_x = dx[TILE:][None]

    dmod_pack = jnp.stack([jnp.stack([dmods[l][1, :6].reshape(-1), dmods[l][0, :6].reshape(-1)]) for l in range(2)])
    dmod_all = all_gather([dmod_pack.reshape(4, 6 * D)], "ag_dmod")[0].reshape(N_DEV, 2, 2, 6 * D)
    dsl = lax.dynamic_slice_in_dim(dmod_all, me * mod_cols, mod_cols, axis=3)
    d_own = jnp.transpose(dsl[:, :, 0, :], (1, 0, 2))
    d_ctx = jnp.transpose(dsl[:, :, 1, :], (1, 0, 2))
    g_w_mod, ds_cc = mod_backward(crows, w_mod, d_own, d_ctx, "mod_bwd")
    g_c_ctx_part = silu_grad_scale(c_ctx[None, :], ds_cc, "silu_bwd")[0]
    g_b_mod_part = jnp.stack([dmods[l][1, :6].reshape(-1) + dmods[l][0, :6].reshape(-1) for l in range(2)])

    small_g = dict(c_ctx=g_c_ctx_part, b_mod=g_b_mod_part, final_norm_w=d_fnw[0])
    for k in SMALL_NAMES:
        if k not in small_g:
            small_g[k] = jnp.stack([G[0][k], G[1][k]])
    res = {}
    sparts = all_gather([v2(small_g[k]) for k in SMALL_NAMES], "ag_small")
    s_out = adamw_group([(sparts[j], v2(W[k]), v2(M[k]), v2(V[k])) for j, k in enumerate(SMALL_NAMES)], "adamw_small")
    for j, k in enumerate(SMALL_NAMES):
        res[k] = [o.reshape(W[k].shape) for o in s_out[j]]

    chunks = []
    for k in BIG_NAMES:
        chunks.append(jnp.concatenate([_layer_to_halves(G[l][k].astype(BF), k) for l in range(2)], axis=2))
    bparts = reduce_scatter(chunks, "rs_grads")
    for j, k in enumerate(BIG_NAMES):
        res[k] = [o.reshape(W[k].shape) for o in adamw(bparts[j], v2(W[k]), v2(M[k]), v2(V[k]), f"adamw_{k}")]
    res["w_mod"] = [o.reshape(w_mod.shape)
                    for o in adamw(v2(g_w_mod)[None], v2(w_mod), v2(m_w_mod), v2(v_w_mod), "adamw_w_mod")]
    outs = [loss, grad_x]
    for j in range(4):
        outs += [res[k][j] for k in WEIGHT_ORDER]
    return tuple(outs)
```
